```python
import math
import jax, jax.numpy as jnp
from jax import lax
import numpy as np

D_MODEL = 1024
BATCH = 2
SEQ = 8192
DEPTH = 1
DEC_BATCH = 32
DEC_SEQ = 32
PAST_LEN = 4096

CHUNK = 64
D_MIX = D_MODEL
D_SSM = D_MIX // 2
SSM_GROUP = 16
N_SSM_GROUPS = D_SSM // SSM_GROUP
SSM_STATE = 64
D_ATT = D_MIX - D_SSM
HEAD_DIM = 64
N_ATT_HEADS = D_ATT // HEAD_DIM
Q_BLOCK = 128
N_EXPERTS = 32
TOP_K = 4
D_FF = D_MODEL
SWIGLU_LIMIT = 7.0
SWIGLU_ALPHA = 1.702
MOE_BLOCK = 256
NORM_EPS = 1e-5
D_IN = D_SSM + 3 * D_ATT + N_ATT_HEADS

kernel_name = 'hymba_s5_fox_moe_stream_step'


def rms_norm(x, g):
    xf = x.astype(jnp.float32)
    y = xf * lax.rsqrt(jnp.mean(xf * xf, axis=-1, keepdims=True) + NORM_EPS)
    return (y * g.astype(jnp.float32)).astype(x.dtype)


def modulate(h, shift, scale):
    return h * (1.0 + scale[:, None, :]) + shift[:, None, :]


def s5_group(u, s0, log_dt, a_re, a_im, b_re, b_im, c_re, c_im, d_skip, w_glu, b_glu):
    f32 = jnp.float32
    bsz, t = u.shape[0], u.shape[1]
    uf = u.astype(f32).reshape(bsz, t, N_SSM_GROUPS, SSM_GROUP)
    lam = lax.complex(a_re.astype(f32), a_im.astype(f32))
    dt = jnp.exp(log_dt.astype(f32))[:, None]
    lam_bar = jnp.exp(lam * dt)
    b_bar = ((lam_bar - 1.0) / lam)[:, :, None] * lax.complex(b_re.astype(f32), b_im.astype(f32))
    c_mat = lax.complex(c_re.astype(f32), c_im.astype(f32))
    bu = jnp.einsum('gpc,btgc->btgp', b_bar, uf.astype(jnp.complex64))
    if s0 is not None:
        bu = bu.at[:, 0].add(lam_bar[None] * s0)
    a_seq = jnp.broadcast_to(lam_bar, bu.shape)

    def combine(left, right):
        a_l, b_l = left
        a_r, b_r = right
        return a_l * a_r, a_r * b_l + b_r

    _, states = lax.associative_scan(combine, (a_seq, bu), axis=1)
    y = jnp.einsum('gcp,btgp->btgc', c_mat, states).real + d_skip.astype(f32).reshape(N_SSM_GROUPS, SSM_GROUP) * uf
    y = jax.nn.gelu(y.reshape(bsz, t, D_SSM))
    out = y * jax.nn.sigmoid(y @ w_glu.astype(f32) + b_glu.astype(f32))
    return out.astype(u.dtype), states[:, -1]


def forgetting_attention(q, k, v, logf, cache_k, cache_v, cache_logf):
    bsz, t = q.shape[0], q.shape[1]
    if cache_k is None:
        k_all, v_all, logf_all = k, v, logf
    else:
        k_all = jnp.concatenate([cache_k.astype(k.dtype), k], axis=1)
        v_all = jnp.concatenate([cache_v.astype(v.dtype), v], axis=1)
        logf_all = jnp.concatenate([cache_logf.astype(jnp.float32), logf], axis=1)
    n_keys = k_all.shape[1]
    past = n_keys - t
    f_cum = jnp.cumsum(logf_all, axis=1)
    f_key = jnp.transpose(f_cum, (0, 2, 1))
    k_pos = jnp.arange(n_keys)
    n_blk = max(1, t // Q_BLOCK)
    blk = t // n_blk
    q_b = jnp.moveaxis(q.reshape(bsz, n_blk, blk, N_ATT_HEADS, HEAD_DIM), 1, 0)
    fq_b = jnp.moveaxis(f_cum[:, past:].reshape(bsz, n_blk, blk, N_ATT_HEADS), 1, 0)
    qpos_b = (past + jnp.arange(t)).reshape(n_blk, blk)
    scale = HEAD_DIM ** -0.5

    def one_block(args):
        qb, fqb, qp = args
        s = jnp.einsum('bqhd,bshd->bhqs', qb, k_all).astype(jnp.float32) * scale
        bias = jnp.transpose(fqb, (0, 2, 1))[:, :, :, None] - f_key[:, :, None, :]
        visible = k_pos[None, :] <= qp[:, None]
        s = jnp.where(visible[None, None], s + bias, -jnp.inf)
        p = jax.nn.softmax(s, axis=-1)
        return jnp.einsum('bhqs,bshd->bqhd', p.astype(v_all.dtype), v_all)

    out = lax.map(one_block, (q_b, fq_b, qpos_b))
    return jnp.moveaxis(out, 0, 1).reshape(bsz, t, D_ATT)


def mixer_sublayer(h, p, cache_k, cache_v, cache_logf, s0):
    bsz, t = h.shape[0], h.shape[1]
    proj = h @ p['w_in']
    u, q, k, v, f_logit = jnp.split(proj, [D_SSM, D_SSM + D_ATT, D_SSM + 2 * D_ATT, D_SSM + 3 * D_ATT], axis=-1)
    heads = (bsz, t, N_ATT_HEADS, HEAD_DIM)
    q, k, v = q.reshape(heads), k.reshape(heads), v.reshape(heads)
    logf = jax.nn.log_sigmoid((f_logit + p['b_forget']).astype(jnp.float32))
    y_ssm, s_last = s5_group(u, s0, p['ssm_log_dt'], p['ssm_a_re'], p['ssm_a_im'], p['ssm_b_re'], p['ssm_b_im'],
                             p['ssm_c_re'], p['ssm_c_im'], p['ssm_d'], p['w_glu'], p['b_glu'])
    y_att = forgetting_attention(q, k, v, logf, cache_k, cache_v, cache_logf)
    y = jnp.concatenate([y_ssm, y_att.astype(y_ssm.dtype)], axis=-1) @ p['w_out']
    return y, k, v, logf, s_last


def moe_ffn(h, w_router, b_router, w_up, b_up, w_down, b_down):
    n_tok, d = h.shape
    logits = (h @ w_router + b_router).astype(jnp.float32)
    top_val, top_idx = lax.top_k(logits, TOP_K)
    gates = jax.nn.softmax(top_val, axis=-1)
    n_assign = n_tok * TOP_K
    flat_e = top_idx.reshape(n_assign)
    flat_tok = jnp.repeat(jnp.arange(n_tok, dtype=jnp.int32), TOP_K)
    flat_g = gates.reshape(n_assign).astype(h.dtype)
    onehot = jax.nn.one_hot(flat_e, N_EXPERTS, dtype=jnp.int32)
    counts = jnp.sum(onehot, axis=0)
    rank = jnp.sum(jnp.cumsum(onehot, axis=0) * onehot, axis=1) - 1
    padded = (counts + MOE_BLOCK - 1) // MOE_BLOCK * MOE_BLOCK
    pad_end = jnp.cumsum(padded)
    dest = pad_end[flat_e] - padded[flat_e] + rank
    n_blocks = -(-n_assign // MOE_BLOCK) + N_EXPERTS
    n_rows = n_blocks * MOE_BLOCK
    row_tok = jnp.zeros((n_rows,), jnp.int32).at[dest].set(flat_tok)
    row_gate = jnp.zeros((n_rows,), h.dtype).at[dest].set(flat_g)
    block_e = jnp.minimum(jnp.searchsorted(pad_end, jnp.arange(n_blocks) * MOE_BLOCK, side='right'), N_EXPERTS - 1)

    def expert_block(args):
        tok, g, e = args
        xb = h[tok]
        up = xb @ w_up[e] + b_up[e]
        x_glu = jnp.minimum(up[:, :D_FF], SWIGLU_LIMIT)
        x_lin = jnp.clip(up[:, D_FF:], -SWIGLU_LIMIT, SWIGLU_LIMIT)
        act = (x_lin + 1.0) * (x_glu * jax.nn.sigmoid(SWIGLU_ALPHA * x_glu))
        return (act @ w_down[e] + b_down[e]) * g[:, None]

    outs = lax.map(expert_block, (row_tok.reshape(n_blocks, MOE_BLOCK), row_gate.reshape(n_blocks, MOE_BLOCK), block_e))
    return jnp.zeros_like(h).at[row_tok].add(outs.reshape(n_rows, d))


def layer_forward(x, c, p, cache_k, cache_v, cache_logf, s0):
    mod = jax.nn.silu(c) @ p['w_ada'] + p['b_ada']
    sh_mix, sc_mix, gt_mix, sh_ffn, sc_ffn, gt_ffn = jnp.split(mod, 6, axis=-1)
    h = modulate(rms_norm(x, p['g_mix']), sh_mix, sc_mix)
    y, k, v, logf, s_last = mixer_sublayer(h, p, cache_k, cache_v, cache_logf, s0)
    x = x + gt_mix[:, None, :] * y
    bsz, t, d = x.shape
    h = modulate(rms_norm(x, p['g_ffn']), sh_ffn, sc_ffn)
    y = moe_ffn(h.reshape(bsz * t, d), p['w_router'], p['b_router'], p['w_up'], p['b_up'],
                p['w_down'], p['b_down']).reshape(bsz, t, d)
    x = x + gt_ffn[:, None, :] * y
    return x, k, v, logf, s_last


def setup_inputs(seed: int = 0) -> dict:
    key = jax.random.key(seed)
    ks = jax.random.split(key, 33)
    f32 = jnp.float32

    def nrm(k, shape, s=1.0):
        return s * jax.random.normal(k, shape, f32)

    L, D, H, G, P = DEPTH, D_MODEL, N_ATT_HEADS, N_SSM_GROUPS, SSM_STATE
    a_im = math.pi * jnp.arange(P, dtype=f32)
    return {
        'x_prompt': nrm(ks[0], (BATCH, SEQ, D)),
        'x_sample': nrm(ks[1], (DEC_BATCH, DEC_SEQ, D)),
        'c_prompt': nrm(ks[2], (BATCH, D)),
        'c_sample': nrm(ks[3], (DEC_BATCH, D)),
        'cache_k': nrm(ks[4], (L, DEC_BATCH, PAST_LEN, H, HEAD_DIM)),
        'cache_v': nrm(ks[5], (L, DEC_BATCH, PAST_LEN, H, HEAD_DIM)),
        'cache_logf': jax.nn.log_sigmoid(nrm(ks[6], (L, DEC_BATCH, PAST_LEN, H)) + 3.0),
        'state_ssm_re': nrm(ks[7], (L, DEC_BATCH, G, P), 0.5),
        'state_ssm_im': nrm(ks[8], (L, DEC_BATCH, G, P), 0.5),
        'w_ada': nrm(ks[9], (L, D, 6 * D), D ** -0.5),
        'b_ada': nrm(ks[10], (L, 6 * D), 0.01),
        'g_mix': 1.0 + nrm(ks[11], (L, D), 0.01),
        'w_in': nrm(ks[12], (L, D, D_IN), D ** -0.5),
        'b_forget': jax.random.uniform(ks[13], (L, H), f32, 1.0, 5.0),
        'ssm_log_dt': jax.random.uniform(ks[14], (L, G), f32, math.log(1e-3), math.log(1e-1)),
        'ssm_a_re': -0.5 + nrm(ks[15], (L, G, P), 0.01),
        'ssm_a_im': a_im + nrm(ks[16], (L, G, P), 0.01),
        'ssm_b_re': nrm(ks[17], (L, G, P, SSM_GROUP), (2 * SSM_GROUP) ** -0.5),
        'ssm_b_im': nrm(ks[18], (L, G, P, SSM_GROUP), (2 * SSM_GROUP) ** -0.5),
        'ssm_c_re': nrm(ks[19], (L, G, SSM_GROUP, P), 0.5 ** 0.5),
        'ssm_c_im': nrm(ks[20], (L, G, SSM_GROUP, P), 0.5 ** 0.5),
        'ssm_d': nrm(ks[21], (L, D_SSM)),
        'w_glu': nrm(ks[22], (L, D_SSM, D_SSM), D_SSM ** -0.5),
        'b_glu': nrm(ks[23], (L, D_SSM), 0.01),
        'w_out': nrm(ks[24], (L, D_MIX, D), D_MIX ** -0.5),
        'g_ffn': 1.0 + nrm(ks[25], (L, D), 0.01),
        'w_router': nrm(ks[26], (L, D, N_EXPERTS), D ** -0.5),
        'b_router': nrm(ks[27], (L, N_EXPERTS), 0.01),
        'w_up': nrm(ks[28], (L, N_EXPERTS, D, 2 * D_FF), D ** -0.5),
        'b_up': nrm(ks[29], (L, N_EXPERTS, 2 * D_FF), 0.01),
        'w_down': nrm(ks[30], (L, N_EXPERTS, D_FF, D), D_FF ** -0.5),
        'b_down': nrm(ks[31], (L, N_EXPERTS, D), 0.01),
        'g_final': 1.0 + nrm(ks[32], (D,), 0.01),
    }


def reference(x_prompt, x_sample, c_prompt, c_sample, cache_k, cache_v, cache_logf, state_ssm_re, state_ssm_im,
              w_ada, b_ada, g_mix, w_in, b_forget, ssm_log_dt, ssm_a_re, ssm_a_im, ssm_b_re, ssm_b_im,
              ssm_c_re, ssm_c_im, ssm_d, w_glu, b_glu, w_out, g_ffn, w_router, b_router, w_up, b_up,
              w_down, b_down, g_final):
    assert x_sample.shape[1] <= CHUNK
    xp, xs = x_prompt, x_sample
    k_p, v_p, f_p, sre_p, sim_p = [], [], [], [], []
    k_s, v_s, f_s, sre_s, sim_s = [], [], [], [], []
    for l in range(DEPTH):
        p = {'w_ada': w_ada[l], 'b_ada': b_ada[l], 'g_mix': g_mix[l], 'w_in': w_in[l], 'b_forget': b_forget[l],
             'ssm_log_dt': ssm_log_dt[l], 'ssm_a_re': ssm_a_re[l], 'ssm_a_im': ssm_a_im[l],
             'ssm_b_re': ssm_b_re[l], 'ssm_b_im': ssm_b_im[l], 'ssm_c_re': ssm_c_re[l], 'ssm_c_im': ssm_c_im[l],
             'ssm_d': ssm_d[l], 'w_glu': w_glu[l], 'b_glu': b_glu[l], 'w_out': w_out[l], 'g_ffn': g_ffn[l],
             'w_router': w_router[l], 'b_router': b_router[l], 'w_up': w_up[l], 'b_up': b_up[l],
             'w_down': w_down[l], 'b_down': b_down[l]}
        xp, k_new, v_new, f_new, s_new = layer_forward(xp, c_prompt, p, None, None, None, None)
        k_p.append(k_new)
        v_p.append(v_new)
        f_p.append(f_new)
        sre_p.append(jnp.real(s_new))
        sim_p.append(jnp.imag(s_new))
        s0 = lax.complex(state_ssm_re[l].astype(jnp.float32), state_ssm_im[l].astype(jnp.float32))
        xs, k_new, v_new, f_new, s_new = layer_forward(xs, c_sample, p, cache_k[l], cache_v[l], cache_logf[l], s0)
        k_s.append(k_new)
        v_s.append(v_new)
        f_s.append(f_new)
        sre_s.append(jnp.real(s_new))
        sim_s.append(jnp.imag(s_new))
    y_prompt = rms_norm(xp, g_final)
    y_sample = rms_norm(xs, g_final)
    return (y_prompt, y_sample, jnp.stack(k_p), jnp.stack(v_p), jnp.stack(f_p), jnp.stack(sre_p), jnp.stack(sim_p),
            jnp.stack(k_s), jnp.stack(v_s), jnp.stack(f_s), jnp.stack(sre_s), jnp.stack(sim_s))
```

```python
import functools
import math

import jax
import jax.numpy as jnp
from jax import lax
from jax.experimental import pallas as pl
from jax.experimental.pallas import tpu as pltpu

F32 = jnp.float32
BF16 = jnp.bfloat16
I32 = jnp.int32

D_MODEL = 1024
D_SSM = 512
SSM_GROUP = 16
N_SSM_GROUPS = 32
SSM_STATE = 64
D_ATT = 512
HEAD_DIM = 64
N_HEADS = 8
N_EXPERTS = 32
TOP_K = 4
D_FF = 1024
SWIGLU_LIMIT = 7.0
SWIGLU_ALPHA = 1.702
NORM_EPS = 1e-5

LANES = 128
N_OCT = D_SSM // LANES
OCT_STATE = 2 * (N_SSM_GROUPS // N_OCT) * SSM_STATE
SSM_CHUNK = 8
SSM_ROWS = 256
ATT_BLK = 256
SAMPLE_KV_BLK = 512
MOE_BLK = 256
VMEM_LIMIT = 52 * 1024 * 1024


def _cparams(sem, vmem=VMEM_LIMIT):
    return pltpu.CompilerParams(dimension_semantics=sem, vmem_limit_bytes=vmem)


def _dot(a, b):
    return jnp.dot(a, b, preferred_element_type=F32)


def _dot_nt(a, b):
    return lax.dot_general(a, b, (((1,), (1,)), ((), ())), preferred_element_type=F32)


def _dot_hi(a, b):
    return jnp.dot(a, b, preferred_element_type=F32, precision=lax.Precision.HIGHEST)


def _ada_kernel(c_ref, w_ref, b_ref, o_ref):
    c = c_ref[...]
    s = c * jax.nn.sigmoid(c)
    o_ref[...] = _dot_hi(s, w_ref[...]) + b_ref[...]


def _ada(c_all, w_ada, b_ada):
    m, d = c_all.shape
    n = w_ada.shape[1]
    return pl.pallas_call(
        _ada_kernel,
        out_shape=jax.ShapeDtypeStruct((m, n), F32),
        grid=(n // d,),
        in_specs=[pl.BlockSpec((m, d), lambda j: (0, 0)),
                  pl.BlockSpec((d, d), lambda j: (0, j)),
                  pl.BlockSpec((1, d), lambda j: (0, j))],
        out_specs=pl.BlockSpec((m, d), lambda j: (0, j)),
        compiler_params=_cparams(("arbitrary",)),
        name="ada",
    )(c_all, w_ada, b_ada)


def _norm_mod(x, g, scale, shift):
    ms = jnp.mean(x * x, axis=-1, keepdims=True)
    y = x * lax.rsqrt(ms + NORM_EPS) * g
    return y * (1.0 + scale) + shift


def _log_sigmoid(z):
    return jnp.minimum(z, 0.0) - jnp.log1p(jnp.exp(-jnp.abs(z)))


def _inproj_kernel(x_ref, sh_ref, sc_ref, g_ref, wu_ref, wq_ref, wk_ref, wv_ref, wvt_ref, wf_ref, bf_ref,
                   u4_ref, q_ref, k_ref, v_ref, vt_ref, lf_ref, *, transposed):
    nb, tt, d = x_ref.shape
    tm = nb * tt
    h = _norm_mod(x_ref[...], g_ref[...], sc_ref[...], sh_ref[...]).reshape(tm, d).astype(BF16)
    u = _dot(h, wu_ref[...])
    for o in range(N_OCT):
        u4_ref[o] = u[:, o * LANES:(o + 1) * LANES].astype(BF16)
    k_ref[...] = _dot(h, wk_ref[...])
    v_ref[...] = _dot(h, wv_ref[...])
    lf_ref[...] = _log_sigmoid(_dot(h, wf_ref[...]) + bf_ref[...])
    scale = HEAD_DIM ** -0.5
    if transposed:
        qt = (_dot_nt(wq_ref[...], h) * scale).astype(BF16)
        vt = _dot_nt(wvt_ref[...], h).astype(BF16)
        for j in range(tm // ATT_BLK):
            q_ref[j] = qt[:, j * ATT_BLK:(j + 1) * ATT_BLK]
            vt_ref[j] = vt[:, j * ATT_BLK:(j + 1) * ATT_BLK]
    else:
        q_ref[...] = (_dot(h, wq_ref[...]) * scale).astype(BF16)
        vt_ref[...] = jnp.zeros_like(vt_ref)


def _inproj(x, shift, scale, g_mix, w, *, transposed, tm):
    b, t, d = x.shape
    n = b * t
    tm = min(tm, n)
    tt = min(t, tm)
    nb = tm // tt
    tpb = t // tt
    grid = (n // tm,)

    def xmap(i):
        return (i // tpb, i % tpb, 0) if nb == 1 else (i, 0, 0)

    def bmap(i):
        return (i // tpb, 0, 0) if nb == 1 else (i, 0, 0)

    full = lambda shape: pl.BlockSpec(shape, lambda i: (0,) * len(shape))
    in_specs = [pl.BlockSpec((nb, tt, d), xmap),
                pl.BlockSpec((nb, 1, d), bmap), pl.BlockSpec((nb, 1, d), bmap),
                full((1, d)),
                full(w["wu"].shape), full(w["wq"].shape), full(w["wk"].shape), full(w["wv"].shape),
                full(w["wvt"].shape), full(w["wf"].shape), full((1, N_HEADS))]
    nblk = n // ATT_BLK
    if transposed:
        q_shape = jax.ShapeDtypeStruct((nblk, D_ATT, ATT_BLK), BF16)
        q_spec = pl.BlockSpec((tm // ATT_BLK, D_ATT, ATT_BLK), lambda i: (i, 0, 0))
        vt_shape, vt_spec = q_shape, q_spec
    else:
        q_shape = jax.ShapeDtypeStruct((n, D_ATT), BF16)
        q_spec = pl.BlockSpec((tm, D_ATT), lambda i: (i, 0))
        vt_shape = jax.ShapeDtypeStruct((8, LANES), BF16)
        vt_spec = pl.BlockSpec((8, LANES), lambda i: (0, 0))
    out_shape = (jax.ShapeDtypeStruct((N_OCT, n, LANES), BF16), q_shape,
                 jax.ShapeDtypeStruct((n, D_ATT), F32), jax.ShapeDtypeStruct((n, D_ATT), F32),
                 vt_shape, jax.ShapeDtypeStruct((n, N_HEADS), F32))
    out_specs = (pl.BlockSpec((N_OCT, tm, LANES), lambda i: (0, i, 0)), q_spec,
                 pl.BlockSpec((tm, D_ATT), lambda i: (i, 0)), pl.BlockSpec((tm, D_ATT), lambda i: (i, 0)),
                 vt_spec, pl.BlockSpec((tm, N_HEADS), lambda i: (i, 0)))
    return pl.pallas_call(
        functools.partial(_inproj_kernel, transposed=transposed),
        out_shape=out_shape, grid=grid, in_specs=in_specs, out_specs=out_specs,
        compiler_params=_cparams(("arbitrary",)),
        name="inproj_t" if transposed else "inproj",
    )(x, shift, scale, g_mix, w["wu"], w["wq"], w["wk"], w["wv"], w["wvt"], w["wf"], w["bf"])


def _ssm_operators(log_dt, a_re, a_im, b_re, b_im, c_re, c_im, d_skip):
    L, G, P, C = SSM_CHUNK, N_SSM_GROUPS, SSM_STATE, SSM_GROUP
    q = G // N_OCT
    lam = lax.complex(a_re, a_im)
    dt = jnp.exp(log_dt)[:, None]
    lam_dt = lam * dt
    lam_bar = jnp.exp(lam_dt)
    b_bar = ((lam_bar - 1.0) / lam)[:, :, None] * lax.complex(b_re, b_im)
    c_mat = lax.complex(c_re, c_im)
    pw = jnp.exp(lam_dt[None] * jnp.arange(L + 1, dtype=F32)[:, None, None])
    kern = jnp.real(jnp.einsum('gcp,dgp,gpk->dgck', c_mat, pw[:L], b_bar))
    eye = jnp.eye(q, dtype=F32)
    j_idx = jnp.arange(L)[:, None]
    t_idx = jnp.arange(L)[None, :]
    lag = jnp.clip(t_idx - j_idx, 0, L - 1)
    kt = jnp.where((t_idx >= j_idx)[:, :, None, None, None], kern[lag], 0.0)
    kt = jnp.transpose(kt, (0, 1, 2, 4, 3)).reshape(L, L, N_OCT, q, C, C)
    tmat = jnp.einsum('jtoqkc,qr->ojqktrc', kt, eye).reshape(N_OCT, L * LANES, L * LANES)
    dvec = jnp.tile(d_skip.reshape(N_OCT, 1, LANES), (1, L, 1)).reshape(N_OCT, L * LANES)
    tmat = tmat + jax.vmap(jnp.diag)(dvec)
    pwr = pw[L - 1 - jnp.arange(L)]
    wb = jnp.einsum('jgp,gpk->jgkp', pwr, b_bar).reshape(L, N_OCT, q, C, P)
    we = jnp.einsum('joqkp,qr->ojqkrp', wb, eye.astype(wb.dtype))
    wend = jnp.stack([jnp.real(we), jnp.imag(we)], axis=4).reshape(N_OCT, L * LANES, OCT_STATE)
    cl = (c_mat[None] * pw[1:L + 1][:, :, None, :]).reshape(L, N_OCT, q, C, P)
    wi_re = jnp.einsum('toqcp,qr->oqptrc', jnp.real(cl), eye)
    wi_im = jnp.einsum('toqcp,qr->oqptrc', -jnp.imag(cl), eye)
    win = jnp.stack([wi_re, wi_im], axis=1).reshape(N_OCT, OCT_STATE, L * LANES)
    dec = pw[L].reshape(N_OCT, q * P)
    decay = jnp.concatenate([jnp.real(dec), jnp.imag(dec)], axis=-1).reshape(N_OCT, 1, OCT_STATE)
    return tmat.astype(BF16), wend.astype(BF16), win.astype(BF16), decay.astype(F32)


def _ssm_kernel(u_ref, tmat_ref, wend_ref, win_ref, dec_ref, s0_ref, y_ref, sfin_ref, e_ref, sp_ref, st_ref,
                *, nb, cpt):
    b = pl.program_id(1)
    i = pl.program_id(2)
    half = OCT_STATE // 2
    whole = nb == s0_ref.shape[0]

    @pl.when(i == 0)
    def _():
        st_ref[...] = s0_ref[...] if whole else s0_ref[pl.ds(b, 1), :]

    u = u_ref[...]
    e_ref[...] = _dot(u, wend_ref[...])
    dec = dec_ref[...]
    ar, ai = dec[:, :half], dec[:, half:]

    def per_batch(bb, carry):
        def per_chunk(r, st):
            row = bb * cpt + r
            sp_ref[pl.ds(row, 1), :] = st
            e = e_ref[pl.ds(row, 1), :]
            re, im = st[:, :half], st[:, half:]
            nre = ar * re - ai * im + e[:, :half]
            nim = ar * im + ai * re + e[:, half:]
            return jnp.concatenate([nre, nim], axis=-1)

        st = lax.fori_loop(0, cpt, per_chunk, st_ref[pl.ds(bb, 1), :])
        st_ref[pl.ds(bb, 1), :] = st
        return carry

    lax.fori_loop(0, nb, per_batch, 0)
    y = _dot(u, tmat_ref[...]) + _dot(sp_ref[...].astype(BF16), win_ref[...])
    y_ref[...] = jax.nn.gelu(y).astype(BF16)
    if whole:
        sfin_ref[...] = st_ref[...]
    else:
        sfin_ref[pl.ds(b, 1), :] = st_ref[...]


def _ssm(u4, ops, s0, batch, n_chunks):
    tmat, wend, win, decay = ops
    rows = batch * n_chunks
    kdim = SSM_CHUNK * LANES
    if n_chunks % SSM_ROWS == 0:
        nb, cpt = 1, SSM_ROWS
    else:
        nb, cpt = batch, n_chunks
    r = nb * cpt
    tiles = n_chunks // cpt
    grid = (N_OCT, batch // nb, tiles)
    wspec = lambda shape: pl.BlockSpec((None,) + shape, lambda o, b, i: (o, 0, 0))
    return pl.pallas_call(
        functools.partial(_ssm_kernel, nb=nb, cpt=cpt),
        out_shape=(jax.ShapeDtypeStruct((N_OCT, rows, kdim), BF16),
                   jax.ShapeDtypeStruct((N_OCT, batch, OCT_STATE), F32)),
        grid=grid,
        in_specs=[pl.BlockSpec((None, r, kdim), lambda o, b, i: (o, b * tiles + i, 0)),
                  wspec((kdim, kdim)), wspec((kdim, OCT_STATE)), wspec((OCT_STATE, kdim)), wspec((1, OCT_STATE)),
                  pl.BlockSpec((None, batch, OCT_STATE), lambda o, b, i: (o, 0, 0))],
        out_specs=(pl.BlockSpec((None, r, kdim), lambda o, b, i: (o, b * tiles + i, 0)),
                   pl.BlockSpec((None, batch, OCT_STATE), lambda o, b, i: (o, 0, 0))),
        scratch_shapes=[pltpu.VMEM((r, OCT_STATE), F32), pltpu.VMEM((r, OCT_STATE), F32),
                        pltpu.VMEM((nb, OCT_STATE), F32)],
        compiler_params=_cparams(("arbitrary", "arbitrary", "arbitrary")),
        name="ssm",
    )(u4, tmat, wend, win, decay, s0)


def _bias_placement():
    import numpy as np
    pm = np.zeros((3, N_HEADS, N_HEADS // 2 * LANES), np.float32)
    for piece in range(3):
        for h in range(N_HEADS):
            pm[piece, h, LANES * (h // 2) + 3 * (h % 2) + piece] = 1.0
    return jnp.asarray(pm)


def _fprep_p_kernel(lf_ref, k_ref, pm_ref, ka_ref, carry_ref):
    i = pl.program_id(1)

    @pl.when(i == 0)
    def _():
        carry_ref[...] = jnp.zeros_like(carry_ref)

    tile = lf_ref.shape[0]
    r = lax.broadcasted_iota(I32, (tile, tile), 0)
    c = lax.broadcasted_iota(I32, (tile, tile), 1)
    tri = (c <= r).astype(F32)
    cum = _dot_hi(tri, lf_ref[...]) + carry_ref[0:1, 0:N_HEADS]
    carry_ref[0:1, 0:N_HEADS] = cum[tile - 1:tile, :]
    nf = -cum
    hi = nf.astype(BF16).astype(F32)
    r1 = nf - hi
    mid = r1.astype(BF16).astype(F32)
    lo = (r1 - mid).astype(BF16).astype(F32)
    slab = _dot(hi, pm_ref[0]) + _dot(mid, pm_ref[1]) + _dot(lo, pm_ref[2])
    k = k_ref[...]
    for p in range(N_HEADS // 2):
        ka_ref[:, 2 * p * LANES:(2 * p + 1) * LANES] = k[:, p * LANES:(p + 1) * LANES].astype(BF16)
        ka_ref[:, (2 * p + 1) * LANES:(2 * p + 2) * LANES] = slab[:, p * LANES:(p + 1) * LANES].astype(BF16)


def _fprep_p(logf, k, batch, t):
    tile = ATT_BLK
    tiles = t // tile
    pm = _bias_placement()
    kaug = pl.pallas_call(
        _fprep_p_kernel,
        out_shape=jax.ShapeDtypeStruct((batch * t, 2 * D_ATT), BF16),
        grid=(batch, tiles),
        in_specs=[pl.BlockSpec((tile, N_HEADS), lambda b, i: (b * tiles + i, 0)),
                  pl.BlockSpec((tile, D_ATT), lambda b, i: (b * tiles + i, 0)),
                  pl.BlockSpec(pm.shape, lambda b, i: (0, 0, 0))],
        out_specs=pl.BlockSpec((tile, 2 * D_ATT), lambda b, i: (b * tiles + i, 0)),
        scratch_shapes=[pltpu.VMEM((8, LANES), F32)],
        compiler_params=_cparams(("arbitrary", "arbitrary")),
        name="fprep_p",
    )(logf, k, pm)
    return kaug.reshape(batch * tiles, tile, 2 * D_ATT)


def _attn_p_kernel(qt_ref, ka_ref, vt_ref, o_ref, qbd_ref, m_ref, l_ref, acc_ref):
    i = pl.program_id(2)
    bq = ATT_BLK
    hd = HEAD_DIM
    qb = qt_ref[...]
    zero = jnp.zeros((hd, bq), BF16)
    qbd_ref[0:hd, 0:bq] = qb[0:hd]
    qbd_ref[0:hd, bq:2 * bq] = zero
    qbd_ref[hd:2 * hd, 0:bq] = zero
    qbd_ref[hd:2 * hd, bq:2 * bq] = qb[hd:2 * hd]
    r = lax.broadcasted_iota(I32, (LANES, 2 * bq), 0)
    c = lax.broadcasted_iota(I32, (LANES, 2 * bq), 1)
    ones = ((r < 3) & (c < bq)) | ((r >= 3) & (r < 6) & (c >= bq))
    qbd_ref[2 * hd:2 * hd + LANES, :] = jnp.where(ones, 1.0, 0.0).astype(BF16)
    m_ref[...] = jnp.full_like(m_ref, -jnp.inf)
    l_ref[...] = jnp.zeros_like(l_ref)
    acc_ref[...] = jnp.zeros_like(acc_ref)

    def step(j, masked):
        s = _dot(ka_ref[j], qbd_ref[...])
        if masked:
            key = lax.broadcasted_iota(I32, s.shape, 0)
            qq = lax.broadcasted_iota(I32, s.shape, 1)
            qq = jnp.where(qq >= bq, qq - bq, qq)
            s = jnp.where(key <= qq, s, -jnp.inf)
        m_prev = m_ref[...]
        m_new = jnp.maximum(m_prev, jnp.max(s, axis=0, keepdims=True))
        alpha = jnp.exp(m_prev - m_new)
        p = jnp.exp(s - m_new)
        l_ref[...] = alpha * l_ref[...] + jnp.sum(p, axis=0, keepdims=True)
        m_ref[...] = m_new
        pb = p.astype(BF16)
        vb = vt_ref[j]
        acc_ref[0:hd] = alpha[:, 0:bq] * acc_ref[0:hd] + _dot(vb[0:hd], pb[:, 0:bq])
        acc_ref[hd:2 * hd] = alpha[:, bq:2 * bq] * acc_ref[hd:2 * hd] + _dot(vb[hd:2 * hd], pb[:, bq:2 * bq])

    def body(j, carry):
        step(j, False)
        return carry

    lax.fori_loop(0, i, body, 0)
    step(i, True)
    l = l_ref[...]
    o_ref[0:hd] = acc_ref[0:hd] / l[:, 0:bq]
    o_ref[hd:2 * hd] = acc_ref[hd:2 * hd] / l[:, bq:2 * bq]


def _attn_p(qt, kaug, vt, batch, t):
    blk = ATT_BLK
    nq = t // blk
    npair = N_HEADS // 2
    qt4 = qt.reshape(batch, nq, D_ATT, blk)
    vt4 = vt.reshape(batch, nq, D_ATT, blk)
    ka4 = kaug.reshape(batch, nq, blk, 2 * D_ATT)
    return pl.pallas_call(
        _attn_p_kernel,
        out_shape=jax.ShapeDtypeStruct((batch, D_ATT, t), F32),
        grid=(batch, npair, nq),
        in_specs=[pl.BlockSpec((None, None, LANES, blk), lambda b, p, i: (b, i, p, 0)),
                  pl.BlockSpec((None, nq, blk, 2 * LANES), lambda b, p, i: (b, 0, 0, p)),
                  pl.BlockSpec((None, nq, LANES, blk), lambda b, p, i: (b, 0, p, 0))],
        out_specs=pl.BlockSpec((None, LANES, blk), lambda b, p, i: (b, p, i)),
        scratch_shapes=[pltpu.VMEM((2 * LANES, 2 * blk), BF16), pltpu.VMEM((1, 2 * blk), F32),
                        pltpu.VMEM((1, 2 * blk), F32), pltpu.VMEM((LANES, blk), F32)],
        compiler_params=_cparams(("arbitrary", "arbitrary", "arbitrary")),
        name="attn_p",
    )(qt4, ka4, vt4)


def _lane_cumsum(x):
    n = x.shape[-1]
    lane = lax.broadcasted_iota(I32, x.shape, x.ndim - 1)
    s = 1
    while s < n:
        x = x + jnp.where(lane >= s, pltpu.roll(x, s, x.ndim - 1), 0.0)
        s *= 2
    return x


def _fprep_s_kernel(cl_ref, ln_ref, fc_ref, fn_ref):
    b, h, p = cl_ref.shape
    cum = _lane_cumsum(cl_ref[...].reshape(b * h, p))
    fc_ref[...] = (-cum).reshape(b, h, p)
    total = cum[:, p - 1:p]
    cn = _lane_cumsum(ln_ref[...].reshape(b * h, LANES))
    fn_ref[...] = (-(total + cn)).reshape(b, h, LANES)


def _fprep_s(cache_lf_t, new_lf_t):
    b, h, p = cache_lf_t.shape
    return pl.pallas_call(
        _fprep_s_kernel,
        out_shape=(jax.ShapeDtypeStruct((b, h, p), F32), jax.ShapeDtypeStruct((b, h, LANES), F32)),
        grid=(1,),
        in_specs=[pl.BlockSpec((b, h, p), lambda i: (0, 0, 0)), pl.BlockSpec((b, h, LANES), lambda i: (0, 0, 0))],
        out_specs=(pl.BlockSpec((b, h, p), lambda i: (0, 0, 0)), pl.BlockSpec((b, h, LANES), lambda i: (0, 0, 0))),
        compiler_params=_cparams(("arbitrary",)),
        name="fprep_s",
    )(cache_lf_t, new_lf_t)


def _attn_s_kernel(q_ref, ck_ref, cv_ref, kn_ref, vn_ref, fc_ref, fn_ref, o_ref, qbd_ref, m_ref, l_ref, acc_ref,
                   *, tq):
    j = pl.program_id(1)
    nkv = pl.num_programs(1)
    rows = N_HEADS * tq
    bk = ck_ref.shape[0]

    @pl.when(j == 0)
    def _():
        q = q_ref[...]
        qrep = jnp.broadcast_to(q[None], (N_HEADS, tq, D_ATT)).reshape(rows, D_ATT)
        rh = lax.broadcasted_iota(I32, (rows, D_ATT), 0) // tq
        ch = lax.broadcasted_iota(I32, (rows, D_ATT), 1) // HEAD_DIM
        qbd_ref[...] = jnp.where(rh == ch, qrep, jnp.zeros_like(qrep))
        m_ref[...] = jnp.full_like(m_ref, -jnp.inf)
        l_ref[...] = jnp.zeros_like(l_ref)
        acc_ref[...] = jnp.zeros_like(acc_ref)

    def update(s, vb):
        m_prev = m_ref[...]
        m_new = jnp.maximum(m_prev, jnp.max(s, axis=1, keepdims=True))
        alpha = jnp.exp(m_prev - m_new)
        p = jnp.exp(s - m_new)
        l_ref[...] = alpha * l_ref[...] + jnp.sum(p, axis=1, keepdims=True)
        m_ref[...] = m_new
        acc_ref[...] = alpha * acc_ref[...] + _dot(p.astype(BF16), vb)

    def bias(f, width):
        return jnp.broadcast_to(f[:, None, :], (N_HEADS, tq, width)).reshape(rows, width)

    s = _dot_nt(qbd_ref[...], ck_ref[...].astype(BF16)) + bias(fc_ref[...], bk)
    update(s, cv_ref[...].astype(BF16))

    @pl.when(j == nkv - 1)
    def _():
        s2 = _dot_nt(qbd_ref[...], kn_ref[...].astype(BF16)) + bias(fn_ref[...][:, 0:tq], tq)
        key = lax.broadcasted_iota(I32, (rows, tq), 1)
        qq = lax.broadcasted_iota(I32, (rows, tq), 0) % tq
        update(jnp.where(key <= qq, s2, -jnp.inf), vn_ref[...].astype(BF16))
        o = acc_ref[...] / l_ref[...]
        rh = lax.broadcasted_iota(I32, (rows, D_ATT), 0) // tq
        ch = lax.broadcasted_iota(I32, (rows, D_ATT), 1) // HEAD_DIM
        o = jnp.where(rh == ch, o, 0.0).reshape(N_HEADS, tq, D_ATT)
        o_ref[...] = jnp.sum(o, axis=0)


def _attn_s(q, cache_k, cache_v, k_new, v_new, fc, fn):
    b, tq, _ = q.shape
    p = cache_k.shape[1]
    bk = min(SAMPLE_KV_BLK, p)
    nkv = p // bk
    rows = N_HEADS * tq
    tok = lambda: pl.BlockSpec((None, tq, D_ATT), lambda bi, j: (bi, 0, 0))
    return pl.pallas_call(
        functools.partial(_attn_s_kernel, tq=tq),
        out_shape=jax.ShapeDtypeStruct((b, tq, D_ATT), F32),
        grid=(b, nkv),
        in_specs=[tok(),
                  pl.BlockSpec((None, bk, D_ATT), lambda bi, j: (bi, j, 0)),
                  pl.BlockSpec((None, bk, D_ATT), lambda bi, j: (bi, j, 0)),
                  tok(), tok(),
                  pl.BlockSpec((None, N_HEADS, bk), lambda bi, j: (bi, 0, j)),
                  pl.BlockSpec((None, N_HEADS, LANES), lambda bi, j: (bi, 0, 0))],
        out_specs=tok(),
        scratch_shapes=[pltpu.VMEM((rows, D_ATT), BF16), pltpu.VMEM((rows, 1), F32),
                        pltpu.VMEM((rows, 1), F32), pltpu.VMEM((rows, D_ATT), F32)],
        compiler_params=_cparams(("arbitrary", "arbitrary")),
        name="attn_s",
    )(q, cache_k, cache_v, k_new, v_new, fc, fn)


def _outproj_kernel(x_ref, y4_ref, att_ref, gt_ref, sh_ref, sc_ref, gf_ref, wglu_ref, bglu_ref, wout_ref,
                    wr_ref, br_ref, cin_ref,
                    x1_ref, h2_ref, idx_ref, gate_ref, rank_ref, cnt_ref, carry_ref, *, att_transposed):
    i = pl.program_id(0)
    nb, tt, d = x_ref.shape
    tm = nb * tt

    @pl.when(i == 0)
    def _():
        carry_ref[...] = cin_ref[...]

    ys = jnp.concatenate([y4_ref[o] for o in range(N_OCT)], axis=-1)
    ysf = ys.astype(F32)
    glu = ysf * jax.nn.sigmoid(_dot(ys, wglu_ref[...]) + bglu_ref[...])
    att = att_ref[...]
    if att_transposed:
        att = att.T
    mix = _dot(glu.astype(BF16), wout_ref[0:D_SSM, :]) + _dot(att.astype(BF16), wout_ref[D_SSM:, :])
    x1 = x_ref[...] + gt_ref[...] * mix.reshape(nb, tt, d)
    x1_ref[...] = x1
    h2 = _norm_mod(x1, gf_ref[...], sc_ref[...], sh_ref[...]).reshape(tm, d)
    h2_ref[...] = h2

    logits = _dot_hi(h2, wr_ref[...]) + br_ref[...]
    lane = lax.broadcasted_iota(I32, logits.shape, 1)
    work = logits
    vals, idxs = [], []
    for _ in range(TOP_K):
        mx = jnp.max(work, axis=-1, keepdims=True)
        ix = jnp.min(jnp.where(work == mx, lane, N_EXPERTS), axis=-1, keepdims=True)
        vals.append(mx)
        idxs.append(ix)
        work = jnp.where(lane == ix, -jnp.inf, work)
    ex = [jnp.exp(v - vals[0]) for v in vals]
    den = ex[0] + ex[1] + ex[2] + ex[3]
    chosen = work == -jnp.inf
    mh = jnp.where(chosen, 1.0, 0.0)
    r = lax.broadcasted_iota(I32, (tm, tm), 0)
    c = lax.broadcasted_iota(I32, (tm, tm), 1)
    tri = jnp.where(c < r, 1.0, 0.0).astype(BF16)
    carry = carry_ref[0:1, 0:N_EXPERTS]
    before = _dot(tri, mh.astype(BF16)) + carry
    carry_ref[0:1, 0:N_EXPERTS] = carry + jnp.sum(mh, axis=0, keepdims=True)
    for kk in range(TOP_K):
        idx_ref[:, kk:kk + 1] = idxs[kk]
        gate_ref[:, kk:kk + 1] = ex[kk] / den
        rk = jnp.sum(jnp.where(lane == idxs[kk], before, 0.0), axis=-1, keepdims=True)
        rank_ref[:, kk:kk + 1] = rk.astype(I32)
    cnt_ref[...] = carry_ref[...]


def _outproj(x, y4, att, gt, sh, sc, g_ffn, w, counts_in, *, att_transposed, tm):
    b, t, d = x.shape
    n = b * t
    tm = min(tm, n)
    tt = min(t, tm)
    nb = tm // tt
    tpb = t // tt

    def xmap(i):
        return (i // tpb, i % tpb, 0) if nb == 1 else (i, 0, 0)

    def bmap(i):
        return (i // tpb, 0, 0) if nb == 1 else (i, 0, 0)

    full = lambda shape: pl.BlockSpec(shape, lambda i: (0,) * len(shape))
    if att_transposed:
        att_spec = pl.BlockSpec((None, D_ATT, tm), lambda i: (i // tpb, 0, i % tpb))
    else:
        att_spec = pl.BlockSpec((tm, D_ATT), lambda i: (i, 0))
    tok = lambda w_, dt: (jax.ShapeDtypeStruct((n, w_), dt), pl.BlockSpec((tm, w_), lambda i: (i, 0)))
    outs = [(jax.ShapeDtypeStruct((b, t, d), F32), pl.BlockSpec((nb, tt, d), xmap)),
            tok(d, F32), tok(TOP_K, I32), tok(TOP_K, F32), tok(TOP_K, I32),
            (jax.ShapeDtypeStruct((8, LANES), F32), full((8, LANES)))]
    return pl.pallas_call(
        functools.partial(_outproj_kernel, att_transposed=att_transposed),
        out_shape=tuple(o[0] for o in outs),
        grid=(n // tm,),
        in_specs=[pl.BlockSpec((nb, tt, d), xmap),
                  pl.BlockSpec((N_OCT, tm, LANES), lambda i: (0, i, 0)),
                  att_spec,
                  pl.BlockSpec((nb, 1, d), bmap), pl.BlockSpec((nb, 1, d), bmap), pl.BlockSpec((nb, 1, d), bmap),
                  full((1, d)), full((D_SSM, D_SSM)), full((1, D_SSM)), full((d, d)),
                  full((d, N_EXPERTS)), full((1, N_EXPERTS)), full((8, LANES))],
        out_specs=tuple(o[1] for o in outs),
        scratch_shapes=[pltpu.VMEM((8, LANES), F32)],
        compiler_params=_cparams(("arbitrary",)),
        name="outproj_t" if att_transposed else "outproj",
    )(x, y4, att, gt, sh, sc, g_ffn, w["wglu"], w["bglu"], w["wout"], w["wr"], w["br"], counts_in)


def _dispatch_kernel(off_ref, zst_ref, nu_ref, idx_ref, rank_ref, h_ref, xs_ref, zero_ref, zsem, sem, *, tm):
    i = pl.program_id(0)
    n_blocks = xs_ref.shape[0] // MOE_BLK

    @pl.when(i == 0)
    def _():
        zero_ref[...] = jnp.zeros_like(zero_ref)

        def zero_copy(start):
            start = pl.multiple_of(start, MOE_BLK)
            return pltpu.make_async_copy(zero_ref, xs_ref.at[pl.ds(start, MOE_BLK)], zsem)

        for e in range(N_EXPERTS):
            @pl.when(zst_ref[e] >= 0)
            def _():
                zero_copy(jnp.maximum(zst_ref[e], 0)).start()

        def tail_start(j, carry):
            zero_copy(j * MOE_BLK).start()
            return carry

        def tail_wait(j, carry):
            zero_copy(j * MOE_BLK).wait()
            return carry

        lax.fori_loop(nu_ref[0], n_blocks, tail_start, 0)
        for e in range(N_EXPERTS):
            @pl.when(zst_ref[e] >= 0)
            def _():
                zero_copy(jnp.maximum(zst_ref[e], 0)).wait()
        lax.fori_loop(nu_ref[0], n_blocks, tail_wait, 0)

    base = i * tm

    def issue(t, carry):
        for kk in range(TOP_K):
            a = t * TOP_K + kk
            dst = off_ref[idx_ref[0, 0, a]] + rank_ref[0, 0, a]
            pltpu.make_async_copy(h_ref.at[pl.ds(base + t, 1)], xs_ref.at[pl.ds(dst, 1)], sem).start()
        return carry

    lax.fori_loop(0, tm, issue, 0)

    def drain(t, carry):
        for kk in range(TOP_K):
            pltpu.make_async_copy(h_ref.at[pl.ds(0, 1)], xs_ref.at[pl.ds(0, 1)], sem).wait()
        return carry

    lax.fori_loop(0, tm, drain, 0)


def _dispatch(h2, idx, rank, off, zstart, n_used, n_rows, tm):
    n, d = h2.shape
    nt = n // tm
    idx3 = idx.reshape(nt, 1, tm * TOP_K)
    rank3 = rank.reshape(nt, 1, tm * TOP_K)
    smem_blk = lambda: pl.BlockSpec((1, 1, tm * TOP_K), lambda i, *_: (i, 0, 0), memory_space=pltpu.SMEM)
    return pl.pallas_call(
        functools.partial(_dispatch_kernel, tm=tm),
        out_shape=jax.ShapeDtypeStruct((n_rows, d), F32),
        grid_spec=pltpu.PrefetchScalarGridSpec(
            num_scalar_prefetch=3, grid=(nt,),
            in_specs=[smem_blk(), smem_blk(), pl.BlockSpec(memory_space=pl.ANY)],
            out_specs=pl.BlockSpec(memory_space=pl.ANY),
            scratch_shapes=[pltpu.VMEM((MOE_BLK, d), F32), pltpu.SemaphoreType.DMA, pltpu.SemaphoreType.DMA]),
        compiler_params=_cparams(("arbitrary",)),
        name="moe_dispatch",
    )(off, zstart, n_used, idx3, rank3, h2)


def _expert_kernel(be_ref, nu_ref, xs_ref, wup_ref, bup_ref, wdn_ref, bdn_ref, ys_ref, wup_bf, wdn_bf):
    i = pl.program_id(0)
    e = be_ref[i]
    prev = be_ref[jnp.maximum(i - 1, 0)]

    @pl.when(i < nu_ref[0])
    def _():
        @pl.when((i == 0) | (prev != e))
        def _():
            wup_bf[...] = wup_ref[...].astype(BF16)
            wdn_bf[...] = wdn_ref[...].astype(BF16)

        up = _dot(xs_ref[...].astype(BF16), wup_bf[...]) + bup_ref[...]
        x_glu = jnp.minimum(up[:, :D_FF], SWIGLU_LIMIT)
        x_lin = jnp.clip(up[:, D_FF:], -SWIGLU_LIMIT, SWIGLU_LIMIT)
        act = (x_lin + 1.0) * (x_glu * jax.nn.sigmoid(SWIGLU_ALPHA * x_glu))
        ys_ref[...] = _dot(act.astype(BF16), wdn_bf[...]) + bdn_ref[...]

    @pl.when(i >= nu_ref[0])
    def _():
        ys_ref[...] = jnp.zeros_like(ys_ref)


def _experts(xs, block_e, n_used, w_up, b_up, w_down, b_down):
    n_rows, d = xs.shape
    nblk = n_rows // MOE_BLK
    rowmap = lambda i, be, nu: (jnp.minimum(i, nu[0] - 1), 0)
    emap = lambda i, be, nu: (be[i], 0, 0)
    return pl.pallas_call(
        _expert_kernel,
        out_shape=jax.ShapeDtypeStruct((n_rows, d), F32),
        grid_spec=pltpu.PrefetchScalarGridSpec(
            num_scalar_prefetch=2, grid=(nblk,),
            in_specs=[pl.BlockSpec((MOE_BLK, d), rowmap),
                      pl.BlockSpec((None, d, 2 * D_FF), emap), pl.BlockSpec((None, 1, 2 * D_FF), emap),
                      pl.BlockSpec((None, D_FF, d), emap), pl.BlockSpec((None, 1, d), emap)],
            out_specs=pl.BlockSpec((MOE_BLK, d), lambda i, be, nu: (i, 0)),
            scratch_shapes=[pltpu.VMEM((d, 2 * D_FF), BF16), pltpu.VMEM((D_FF, d), BF16)]),
        compiler_params=_cparams(("arbitrary",)),
        name="moe_experts",
    )(block_e, n_used, xs, w_up, b_up, w_down, b_down)


def _combine_kernel(off_ref, idx_ref, rank_ref, x1_ref, gate_ref, gt_ref, gfin_ref, ys_ref, y_ref, buf_ref, sem,
                    *, tm):
    nb, tt, d = x1_ref.shape

    def issue(t, carry):
        for kk in range(TOP_K):
            a = t * TOP_K + kk
            src = off_ref[idx_ref[0, 0, a]] + rank_ref[0, 0, a]
            pltpu.make_async_copy(ys_ref.at[pl.ds(src, 1)], buf_ref.at[kk, pl.ds(t, 1)], sem).start()
        return carry

    lax.fori_loop(0, tm, issue, 0)

    def drain(t, carry):
        for kk in range(TOP_K):
            pltpu.make_async_copy(ys_ref.at[pl.ds(0, 1)], buf_ref.at[kk, pl.ds(0, 1)], sem).wait()
        return carry

    lax.fori_loop(0, tm, drain, 0)
    gate = gate_ref[...]
    moe = gate[:, 0:1] * buf_ref[0]
    for kk in range(1, TOP_K):
        moe = moe + gate[:, kk:kk + 1] * buf_ref[kk]
    x2 = x1_ref[...] + gt_ref[...] * moe.reshape(nb, tt, d)
    ms = jnp.mean(x2 * x2, axis=-1, keepdims=True)
    y_ref[...] = x2 * lax.rsqrt(ms + NORM_EPS) * gfin_ref[...]


def _combine(x1, gate, idx, rank, off, ys, gt, g_final, tm):
    b, t, d = x1.shape
    n = b * t
    tm = min(tm, n)
    tt = min(t, tm)
    nb = tm // tt
    tpb = t // tt
    nt = n // tm
    idx3 = idx.reshape(nt, 1, tm * TOP_K)
    rank3 = rank.reshape(nt, 1, tm * TOP_K)

    def xmap(i, *_):
        return (i // tpb, i % tpb, 0) if nb == 1 else (i, 0, 0)

    def bmap(i, *_):
        return (i // tpb, 0, 0) if nb == 1 else (i, 0, 0)

    smem_blk = lambda: pl.BlockSpec((1, 1, tm * TOP_K), lambda i, *_: (i, 0, 0), memory_space=pltpu.SMEM)
    return pl.pallas_call(
        functools.partial(_combine_kernel, tm=tm),
        out_shape=jax.ShapeDtypeStruct((b, t, d), F32),
        grid_spec=pltpu.PrefetchScalarGridSpec(
            num_scalar_prefetch=1, grid=(nt,),
            in_specs=[smem_blk(), smem_blk(),
                      pl.BlockSpec((nb, tt, d), xmap),
                      pl.BlockSpec((tm, TOP_K), lambda i, *_: (i, 0)),
                      pl.BlockSpec((nb, 1, d), bmap),
                      pl.BlockSpec((1, d), lambda i, *_: (0, 0)),
                      pl.BlockSpec(memory_space=pl.ANY)],
            out_specs=pl.BlockSpec((nb, tt, d), xmap),
            scratch_shapes=[pltpu.VMEM((TOP_K, tm, d), F32), pltpu.SemaphoreType.DMA]),
        compiler_params=_cparams(("arbitrary",)),
        name="moe_combine",
    )(off, idx3, rank3, x1, gate, gt, g_final, ys)


def _moe(x1, h2, idx, gate, rank, counts, gt_ffn, g_final, w_up, b_up, w_down, b_down):
    n, d = h2.shape
    cnt = counts[0, :N_EXPERTS].astype(I32)
    padded = (cnt + MOE_BLK - 1) // MOE_BLK * MOE_BLK
    pad_end = jnp.cumsum(padded)
    off = pad_end - padded
    n_blocks = -(-n * TOP_K // MOE_BLK) + N_EXPERTS
    n_rows = n_blocks * MOE_BLK
    block_e = jnp.minimum(jnp.searchsorted(pad_end, jnp.arange(n_blocks, dtype=I32) * MOE_BLK, side='right'),
                          N_EXPERTS - 1).astype(I32)
    n_used = (pad_end[-1:] // MOE_BLK).astype(I32)
    zstart = jnp.where(padded > 0, pad_end - MOE_BLK, -1).astype(I32)
    tm = min(256, n)
    xs = _dispatch(h2, idx, rank, off.astype(I32), zstart, n_used, n_rows, tm)
    ys = _experts(xs, block_e, n_used, w_up, b_up, w_down, b_down)
    return _combine(x1, gate, idx, rank, off.astype(I32), ys, gt_ffn, g_final, min(128, n))


def _stream(x, mod, w, ssm_ops, s0, cache, params):
    b, t, d = x.shape
    n = b * t
    sh_mix, sc_mix, gt_mix, sh_ffn, sc_ffn, gt_ffn = mod
    prompt = cache is None
    tm = 512
    u4, q, k, v, vt, logf = _inproj(x, sh_mix, sc_mix, params["g_mix"], w, transposed=prompt, tm=tm)
    n_chunks = t // SSM_CHUNK
    y4, s_fin = _ssm(u4.reshape(N_OCT, b * n_chunks, SSM_CHUNK * LANES), ssm_ops, s0, b, n_chunks)
    y4 = y4.reshape(N_OCT, n, LANES)
    if prompt:
        kaug = _fprep_p(logf, k, b, t)
        att = _attn_p(q, kaug, vt, b, t)
    else:
        cache_k, cache_v, cache_lf = cache
        p = cache_k.shape[1]
        cl_t = jnp.transpose(cache_lf, (0, 2, 1))
        ln_t = jnp.pad(jnp.transpose(logf.reshape(b, t, N_HEADS), (0, 2, 1)), ((0, 0), (0, 0), (0, LANES - t)))
        fc, fn = _fprep_s(cl_t, ln_t)
        att = _attn_s(q.reshape(b, t, D_ATT), cache_k.reshape(b, p, D_ATT), cache_v.reshape(b, p, D_ATT),
                      k.reshape(b, t, D_ATT), v.reshape(b, t, D_ATT), fc, fn).reshape(n, D_ATT)
    counts0 = jnp.zeros((8, LANES), F32)
    x1, h2, idx, gate, rank, counts = _outproj(x, y4, att, gt_mix, sh_ffn, sc_ffn, params["g_ffn"], w, counts0,
                                               att_transposed=prompt, tm=256)
    y = _moe(x1, h2, idx, gate, rank, counts, gt_ffn, params["g_final"],
             params["w_up"], params["b_up"], params["w_down"], params["b_down"])
    return y, k, v, logf, s_fin


def _state_in(re, im):
    b = re.shape[0]
    s = jnp.concatenate([re.reshape(b, N_OCT, OCT_STATE // 2), im.reshape(b, N_OCT, OCT_STATE // 2)], axis=-1)
    return jnp.transpose(s, (1, 0, 2))


def _state_out(s):
    b = s.shape[1]
    s = jnp.transpose(s, (1, 0, 2))
    half = OCT_STATE // 2
    re = s[:, :, :half].reshape(1, b, N_SSM_GROUPS, SSM_STATE)
    im = s[:, :, half:].reshape(1, b, N_SSM_GROUPS, SSM_STATE)
    return re, im


def kernel(x_prompt, x_sample, c_prompt, c_sample, cache_k, cache_v, cache_logf, state_ssm_re, state_ssm_im, w_ada, b_ada, g_mix, w_in, b_forget, ssm_log_dt, ssm_a_re, ssm_a_im, ssm_b_re, ssm_b_im, ssm_c_re, ssm_c_im, ssm_d, w_glu, b_glu, w_out, g_ffn, w_router, b_router, w_up, b_up, w_down, b_down, g_final):
    assert w_ada.shape[0] == 1, "single-layer trunk"
    d = D_MODEL
    bp, tp, _ = x_prompt.shape
    bs, ts, _ = x_sample.shape
    n_c = bp + bs
    c_all = jnp.pad(jnp.concatenate([c_prompt, c_sample], axis=0), ((0, -n_c % 8), (0, 0)))
    mod = _ada(c_all, w_ada[0], b_ada[0].reshape(1, 6 * d))
    mod_p = [mod[:bp, j * d:(j + 1) * d].reshape(bp, 1, d) for j in range(6)]
    mod_s = [mod[bp:n_c, j * d:(j + 1) * d].reshape(bs, 1, d) for j in range(6)]
    wi = w_in[0]
    o1, o2, o3, o4 = D_SSM, D_SSM + D_ATT, D_SSM + 2 * D_ATT, D_SSM + 3 * D_ATT
    w_q = wi[:, o1:o2].astype(BF16)
    w_v = wi[:, o3:o4].astype(BF16)
    base = {"wu": wi[:, :o1].astype(BF16), "wk": wi[:, o2:o3].astype(BF16), "wv": w_v,
            "wf": wi[:, o4:].astype(BF16), "bf": b_forget[0].reshape(1, N_HEADS),
            "wglu": w_glu[0].astype(BF16), "bglu": b_glu[0].reshape(1, D_SSM), "wout": w_out[0].astype(BF16),
            "wr": w_router[0], "br": b_router[0].reshape(1, N_EXPERTS)}
    w_p = dict(base, wq=w_q.T, wvt=w_v.T)
    w_s = dict(base, wq=w_q, wvt=jnp.zeros((8, LANES), BF16))
    params = {"g_mix": g_mix[0].reshape(1, d), "g_ffn": g_ffn[0].reshape(1, d), "g_final": g_final.reshape(1, d),
              "w_up": w_up[0], "b_up": b_up[0].reshape(N_EXPERTS, 1, 2 * D_FF),
              "w_down": w_down[0], "b_down": b_down[0].reshape(N_EXPERTS, 1, d)}
    ssm_ops = _ssm_operators(ssm_log_dt[0], ssm_a_re[0], ssm_a_im[0], ssm_b_re[0], ssm_b_im[0],
                             ssm_c_re[0], ssm_c_im[0], ssm_d[0])
    zero_state = jnp.zeros((N_OCT, bp, OCT_STATE), F32)
    y_p, k_p, v_p, f_p, s_p = _stream(x_prompt, mod_p, w_p, ssm_ops, zero_state, None, params)
    s0 = _state_in(state_ssm_re[0], state_ssm_im[0])
    y_s, k_s, v_s, f_s, s_s = _stream(x_sample, mod_s, w_s, ssm_ops, s0,
                                      (cache_k[0], cache_v[0], cache_logf[0]), params)
    sre_p, sim_p = _state_out(s_p)
    sre_s, sim_s = _state_out(s_s)
    hd = (N_HEADS, HEAD_DIM)
    return (y_p, y_s,
            k_p.reshape(1, bp, tp, *hd), v_p.reshape(1, bp, tp, *hd), f_p.reshape(1, bp, tp, N_HEADS), sre_p, sim_p,
            k_s.reshape(1, bs, ts, *hd), v_s.reshape(1, bs, ts, *hd), f_s.reshape(1, bs, ts, N_HEADS), sre_s, sim_s)
```

```python
import functools
import math

import jax
import jax.numpy as jnp
from jax import lax
from jax.experimental import pallas as pl
from jax.experimental.pallas import tpu as pltpu

F32 = jnp.float32
BF16 = jnp.bfloat16
I32 = jnp.int32

D_MODEL = 1024
D_SSM = 512
SSM_GROUP = 16
N_SSM_GROUPS = 32
SSM_STATE = 64
D_ATT = 512
HEAD_DIM = 64
N_HEADS = 8
N_EXPERTS = 32
TOP_K = 4
D_FF = 1024
SWIGLU_LIMIT = 7.0
SWIGLU_ALPHA = 1.702
NORM_EPS = 1e-5

LANES = 128
N_OCT = D_SSM // LANES
OCT_STATE = 2 * (N_SSM_GROUPS // N_OCT) * SSM_STATE
SSM_CHUNK = 8
SSM_ROWS = 256
ATT_BLK = 256
ATT_PAIRS = 2
LOG2E = math.log2(math.e)
SAMPLE_KV_BLK = 512
MOE_BLK = 256
VMEM_LIMIT = 52 * 1024 * 1024


def _cparams(sem, vmem=VMEM_LIMIT):
    return pltpu.CompilerParams(dimension_semantics=sem, vmem_limit_bytes=vmem)


def _dot(a, b):
    return jnp.dot(a, b, preferred_element_type=F32)


def _dot_nt(a, b):
    return lax.dot_general(a, b, (((1,), (1,)), ((), ())), preferred_element_type=F32)


def _dot_hi(a, b):
    return jnp.dot(a, b, preferred_element_type=F32, precision=lax.Precision.HIGHEST)


def _ada_kernel(c_ref, w_ref, b_ref, o_ref):
    c = c_ref[...]
    s = c * jax.nn.sigmoid(c)
    o_ref[...] = _dot_hi(s, w_ref[...]) + b_ref[...]


def _ada(c_all, w_ada, b_ada):
    m, d = c_all.shape
    n = w_ada.shape[1]
    return pl.pallas_call(
        _ada_kernel,
        out_shape=jax.ShapeDtypeStruct((m, n), F32),
        grid=(n // d,),
        in_specs=[pl.BlockSpec((m, d), lambda j: (0, 0)),
                  pl.BlockSpec((d, d), lambda j: (0, j)),
                  pl.BlockSpec((1, d), lambda j: (0, j))],
        out_specs=pl.BlockSpec((m, d), lambda j: (0, j)),
        compiler_params=_cparams(("arbitrary",)),
        name="ada",
    )(c_all, w_ada, b_ada)


def _norm_mod(x, g, scale, shift):
    ms = jnp.mean(x * x, axis=-1, keepdims=True)
    y = x * lax.rsqrt(ms + NORM_EPS) * g
    return y * (1.0 + scale) + shift


def _log_sigmoid(z):
    return jnp.minimum(z, 0.0) - jnp.log1p(jnp.exp(-jnp.abs(z)))


def _inproj_kernel(x_ref, sh_ref, sc_ref, g_ref, wu_ref, wq_ref, wk_ref, wv_ref, wvt_ref, wf_ref, bf_ref,
                   u4_ref, q_ref, k_ref, v_ref, vt_ref, lf_ref, *, transposed):
    nb, tt, d = x_ref.shape
    tm = nb * tt
    h = _norm_mod(x_ref[...], g_ref[...], sc_ref[...], sh_ref[...]).reshape(tm, d).astype(BF16)
    u = _dot(h, wu_ref[...])
    for o in range(N_OCT):
        u4_ref[o] = u[:, o * LANES:(o + 1) * LANES].astype(BF16)
    k_ref[...] = _dot(h, wk_ref[...])
    v_ref[...] = _dot(h, wv_ref[...])
    lf_ref[...] = _log_sigmoid(_dot(h, wf_ref[...]) + bf_ref[...])
    scale = HEAD_DIM ** -0.5
    if transposed:
        qt = (_dot_nt(wq_ref[...], h) * (scale * LOG2E)).astype(BF16)
        vt = _dot_nt(wvt_ref[...], h).astype(BF16)
        for j in range(tm // ATT_BLK):
            q_ref[j] = qt[:, j * ATT_BLK:(j + 1) * ATT_BLK]
            vt_ref[j] = vt[:, j * ATT_BLK:(j + 1) * ATT_BLK]
    else:
        q_ref[...] = (_dot(h, wq_ref[...]) * scale).astype(BF16)
        vt_ref[...] = jnp.zeros_like(vt_ref)


def _inproj(x, shift, scale, g_mix, w, *, transposed, tm):
    b, t, d = x.shape
    n = b * t
    tm = min(tm, n)
    tt = min(t, tm)
    nb = tm // tt
    tpb = t // tt
    grid = (n // tm,)

    def xmap(i):
        return (i // tpb, i % tpb, 0) if nb == 1 else (i, 0, 0)

    def bmap(i):
        return (i // tpb, 0, 0) if nb == 1 else (i, 0, 0)

    full = lambda shape: pl.BlockSpec(shape, lambda i: (0,) * len(shape))
    in_specs = [pl.BlockSpec((nb, tt, d), xmap),
                pl.BlockSpec((nb, 1, d), bmap), pl.BlockSpec((nb, 1, d), bmap),
                full((1, d)),
                full(w["wu"].shape), full(w["wq"].shape), full(w["wk"].shape), full(w["wv"].shape),
                full(w["wvt"].shape), full(w["wf"].shape), full((1, N_HEADS))]
    nblk = n // ATT_BLK
    if transposed:
        q_shape = jax.ShapeDtypeStruct((nblk, D_ATT, ATT_BLK), BF16)
        q_spec = pl.BlockSpec((tm // ATT_BLK, D_ATT, ATT_BLK), lambda i: (i, 0, 0))
        vt_shape, vt_spec = q_shape, q_spec
    else:
        q_shape = jax.ShapeDtypeStruct((n, D_ATT), BF16)
        q_spec = pl.BlockSpec((tm, D_ATT), lambda i: (i, 0))
        vt_shape = jax.ShapeDtypeStruct((8, LANES), BF16)
        vt_spec = pl.BlockSpec((8, LANES), lambda i: (0, 0))
    out_shape = (jax.ShapeDtypeStruct((N_OCT, n, LANES), BF16), q_shape,
                 jax.ShapeDtypeStruct((n, D_ATT), F32), jax.ShapeDtypeStruct((n, D_ATT), F32),
                 vt_shape, jax.ShapeDtypeStruct((n, N_HEADS), F32))
    out_specs = (pl.BlockSpec((N_OCT, tm, LANES), lambda i: (0, i, 0)), q_spec,
                 pl.BlockSpec((tm, D_ATT), lambda i: (i, 0)), pl.BlockSpec((tm, D_ATT), lambda i: (i, 0)),
                 vt_spec, pl.BlockSpec((tm, N_HEADS), lambda i: (i, 0)))
    return pl.pallas_call(
        functools.partial(_inproj_kernel, transposed=transposed),
        out_shape=out_shape, grid=grid, in_specs=in_specs, out_specs=out_specs,
        compiler_params=_cparams(("arbitrary",)),
        name="inproj_t" if transposed else "inproj",
    )(x, shift, scale, g_mix, w["wu"], w["wq"], w["wk"], w["wv"], w["wvt"], w["wf"], w["bf"])


def _ssm_operators(log_dt, a_re, a_im, b_re, b_im, c_re, c_im, d_skip):
    L, G, P, C = SSM_CHUNK, N_SSM_GROUPS, SSM_STATE, SSM_GROUP
    q = G // N_OCT
    lam = lax.complex(a_re, a_im)
    dt = jnp.exp(log_dt)[:, None]
    lam_dt = lam * dt
    lam_bar = jnp.exp(lam_dt)
    b_bar = ((lam_bar - 1.0) / lam)[:, :, None] * lax.complex(b_re, b_im)
    c_mat = lax.complex(c_re, c_im)
    pw = jnp.exp(lam_dt[None] * jnp.arange(L + 1, dtype=F32)[:, None, None])
    kern = jnp.real(jnp.einsum('gcp,dgp,gpk->dgck', c_mat, pw[:L], b_bar))
    ksm = jnp.transpose(kern, (1, 0, 3, 2)).reshape(N_OCT, q, L, C, C)
    ksm = jnp.transpose(ksm, (0, 2, 1, 3, 4)).reshape(N_OCT, L, q * C, C)
    pwr = pw[L - 1 - jnp.arange(L)]
    wb = jnp.einsum('jgp,gpk->jgkp', pwr, b_bar)
    wsm = jnp.stack([jnp.real(wb), jnp.imag(wb)], axis=0).reshape(2, L, N_OCT, q * C, P)
    wsm = jnp.transpose(wsm, (2, 1, 0, 3, 4))
    cl = c_mat[None] * pw[1:L + 1][:, :, None, :]
    vsm = jnp.stack([jnp.real(cl), -jnp.imag(cl)], axis=0).reshape(2, L, N_OCT, q, C, P)
    vsm = jnp.transpose(vsm, (2, 1, 0, 3, 5, 4)).reshape(N_OCT, L, OCT_STATE, C)
    tmat, wend, win = _ssm_prep(ksm, wsm, vsm, d_skip.reshape(N_OCT, 1, LANES))
    dec = pw[L].reshape(N_OCT, q * P)
    decay = jnp.concatenate([jnp.real(dec), jnp.imag(dec)], axis=-1).reshape(N_OCT, 1, OCT_STATE)
    return tmat, wend, win, decay.astype(F32)


def _ssm_prep_kernel(ksm_ref, wsm_ref, vsm_ref, d_ref, tmat_ref, wend_ref, win_ref):
    L, C, P = SSM_CHUNK, SSM_GROUP, SSM_STATE
    half = OCT_STATE // 2

    def spread(period, width):
        r = lax.broadcasted_iota(I32, (period, width), 0)
        c = lax.broadcasted_iota(I32, (period, width), 1)
        return jnp.where(c % period == r, 1.0, 0.0)

    def same_group(shape, row_div, row_mod, lane_div):
        r = lax.broadcasted_iota(I32, shape, 0)
        c = lax.broadcasted_iota(I32, shape, 1)
        return (r % row_mod) // row_div == c // lane_div

    e_c = spread(C, LANES)
    e_p = spread(P, half)
    grp = same_group((LANES, LANES), C, LANES, C)
    rr = lax.broadcasted_iota(I32, (LANES, LANES), 0)
    cc = lax.broadcasted_iota(I32, (LANES, LANES), 1)
    lag = []
    for d in range(L):
        blk = jnp.where(grp, _dot_hi(ksm_ref[d], e_c), 0.0)
        if d == 0:
            blk = blk + jnp.where(rr == cc, jnp.broadcast_to(d_ref[...], (LANES, LANES)), 0.0)
        lag.append(blk.astype(BF16))
    zero = jnp.zeros((LANES, LANES), BF16)
    for j in range(L):
        for t in range(L):
            tmat_ref[j * LANES:(j + 1) * LANES, t * LANES:(t + 1) * LANES] = lag[t - j] if t >= j else zero
    grp_e = same_group((LANES, half), C, LANES, P)
    for j in range(L):
        for part in range(2):
            blk = jnp.where(grp_e, _dot_hi(wsm_ref[j, part], e_p), 0.0)
            wend_ref[j * LANES:(j + 1) * LANES, part * half:(part + 1) * half] = blk.astype(BF16)
    grp_i = same_group((OCT_STATE, LANES), P, half, C)
    for t in range(L):
        blk = jnp.where(grp_i, _dot_hi(vsm_ref[t], e_c), 0.0)
        win_ref[:, t * LANES:(t + 1) * LANES] = blk.astype(BF16)


def _ssm_prep(ksm, wsm, vsm, dvec):
    kdim = SSM_CHUNK * LANES
    blk = lambda a: pl.BlockSpec((None,) + a.shape[1:], lambda o: (o,) + (0,) * (a.ndim - 1))
    out = lambda r, c: (jax.ShapeDtypeStruct((N_OCT, r, c), BF16), pl.BlockSpec((None, r, c), lambda o: (o, 0, 0)))
    outs = [out(kdim, kdim), out(kdim, OCT_STATE), out(OCT_STATE, kdim)]
    return pl.pallas_call(
        _ssm_prep_kernel,
        out_shape=tuple(o[0] for o in outs),
        grid=(N_OCT,),
        in_specs=[blk(ksm), blk(wsm), blk(vsm), blk(dvec)],
        out_specs=tuple(o[1] for o in outs),
        compiler_params=_cparams(("arbitrary",)),
        name="ssm_prep",
    )(ksm, wsm, vsm, dvec)


def _ssm_kernel(u_ref, tmat_ref, wend_ref, win_ref, dec_ref, s0_ref, y_ref, sfin_ref, e_ref, sp_ref, st_ref,
                *, nb, cpt):
    b = pl.program_id(1)
    i = pl.program_id(2)
    half = OCT_STATE // 2
    whole = nb == s0_ref.shape[0]

    @pl.when(i == 0)
    def _():
        st_ref[...] = s0_ref[...] if whole else s0_ref[pl.ds(b, 1), :]

    u = u_ref[...]
    e_ref[...] = _dot(u, wend_ref[...])
    dec = dec_ref[...]
    ar, ai = dec[:, :half], dec[:, half:]

    def per_batch(bb, carry):
        def per_chunk(r, st):
            row = bb * cpt + r
            sp_ref[pl.ds(row, 1), :] = st
            e = e_ref[pl.ds(row, 1), :]
            re, im = st[:, :half], st[:, half:]
            nre = ar * re - ai * im + e[:, :half]
            nim = ar * im + ai * re + e[:, half:]
            return jnp.concatenate([nre, nim], axis=-1)

        st = lax.fori_loop(0, cpt, per_chunk, st_ref[pl.ds(bb, 1), :])
        st_ref[pl.ds(bb, 1), :] = st
        return carry

    lax.fori_loop(0, nb, per_batch, 0)
    y = _dot(u, tmat_ref[...]) + _dot(sp_ref[...].astype(BF16), win_ref[...])
    y_ref[...] = jax.nn.gelu(y).astype(BF16)
    if whole:
        sfin_ref[...] = st_ref[...]
    else:
        sfin_ref[pl.ds(b, 1), :] = st_ref[...]


def _ssm(u4, ops, s0, batch, n_chunks):
    tmat, wend, win, decay = ops
    rows = batch * n_chunks
    kdim = SSM_CHUNK * LANES
    if n_chunks % SSM_ROWS == 0:
        nb, cpt = 1, SSM_ROWS
    else:
        nb, cpt = batch, n_chunks
    r = nb * cpt
    tiles = n_chunks // cpt
    grid = (N_OCT, batch // nb, tiles)
    wspec = lambda shape: pl.BlockSpec((None,) + shape, lambda o, b, i: (o, 0, 0))
    return pl.pallas_call(
        functools.partial(_ssm_kernel, nb=nb, cpt=cpt),
        out_shape=(jax.ShapeDtypeStruct((N_OCT, rows, kdim), BF16),
                   jax.ShapeDtypeStruct((N_OCT, batch, OCT_STATE), F32)),
        grid=grid,
        in_specs=[pl.BlockSpec((None, r, kdim), lambda o, b, i: (o, b * tiles + i, 0)),
                  wspec((kdim, kdim)), wspec((kdim, OCT_STATE)), wspec((OCT_STATE, kdim)), wspec((1, OCT_STATE)),
                  pl.BlockSpec((None, batch, OCT_STATE), lambda o, b, i: (o, 0, 0))],
        out_specs=(pl.BlockSpec((None, r, kdim), lambda o, b, i: (o, b * tiles + i, 0)),
                   pl.BlockSpec((None, batch, OCT_STATE), lambda o, b, i: (o, 0, 0))),
        scratch_shapes=[pltpu.VMEM((r, OCT_STATE), F32), pltpu.VMEM((r, OCT_STATE), F32),
                        pltpu.VMEM((nb, OCT_STATE), F32)],
        compiler_params=_cparams(("arbitrary", "arbitrary", "arbitrary")),
        name="ssm",
    )(u4, tmat, wend, win, decay, s0)


def _bias_placement():
    import numpy as np
    pm = np.zeros((3, N_HEADS, N_HEADS // 2 * LANES), np.float32)
    for piece in range(3):
        for h in range(N_HEADS):
            pm[piece, h, LANES * (h // 2) + 3 * (h % 2) + piece] = 1.0
    return jnp.asarray(pm)


def _fprep_p_kernel(lf_ref, k_ref, pm_ref, ka_ref, carry_ref):
    i = pl.program_id(1)

    @pl.when(i == 0)
    def _():
        carry_ref[...] = jnp.zeros_like(carry_ref)

    tile = lf_ref.shape[0]
    r = lax.broadcasted_iota(I32, (tile, tile), 0)
    c = lax.broadcasted_iota(I32, (tile, tile), 1)
    tri = (c <= r).astype(F32)
    cum = _dot_hi(tri, lf_ref[...]) + carry_ref[0:1, 0:N_HEADS]
    carry_ref[0:1, 0:N_HEADS] = cum[tile - 1:tile, :]
    nf = cum * (-LOG2E)
    hi = nf.astype(BF16).astype(F32)
    r1 = nf - hi
    mid = r1.astype(BF16).astype(F32)
    lo = (r1 - mid).astype(BF16).astype(F32)
    slab = _dot(hi, pm_ref[0]) + _dot(mid, pm_ref[1]) + _dot(lo, pm_ref[2])
    k = k_ref[...]
    for p in range(N_HEADS // 2):
        ka_ref[:, 2 * p * LANES:(2 * p + 1) * LANES] = k[:, p * LANES:(p + 1) * LANES].astype(BF16)
        ka_ref[:, (2 * p + 1) * LANES:(2 * p + 2) * LANES] = slab[:, p * LANES:(p + 1) * LANES].astype(BF16)


def _fprep_p(logf, k, batch, t):
    tile = ATT_BLK
    tiles = t // tile
    pm = _bias_placement()
    kaug = pl.pallas_call(
        _fprep_p_kernel,
        out_shape=jax.ShapeDtypeStruct((batch * t, 2 * D_ATT), BF16),
        grid=(batch, tiles),
        in_specs=[pl.BlockSpec((tile, N_HEADS), lambda b, i: (b * tiles + i, 0)),
                  pl.BlockSpec((tile, D_ATT), lambda b, i: (b * tiles + i, 0)),
                  pl.BlockSpec(pm.shape, lambda b, i: (0, 0, 0))],
        out_specs=pl.BlockSpec((tile, 2 * D_ATT), lambda b, i: (b * tiles + i, 0)),
        scratch_shapes=[pltpu.VMEM((8, LANES), F32)],
        compiler_params=_cparams(("arbitrary", "arbitrary")),
        name="fprep_p",
    )(logf, k, pm)
    return kaug.reshape(batch * tiles, tile, 2 * D_ATT)


def _attn_p_kernel(qt_ref, ka_ref, vt_ref, o_ref, qbd_ref, acc_ref, sa_ref, sb_ref):
    i = pl.program_id(2)
    bq = ATT_BLK
    hd = HEAD_DIM
    r = lax.broadcasted_iota(I32, (LANES, 2 * bq), 0)
    c = lax.broadcasted_iota(I32, (LANES, 2 * bq), 1)
    ones = jnp.where(((r < 3) & (c < bq)) | ((r >= 3) & (r < 6) & (c >= bq)), 1.0, 0.0).astype(BF16)
    zero = jnp.zeros((hd, bq), BF16)
    for pp in range(ATT_PAIRS):
        qb = qt_ref[pp * LANES:(pp + 1) * LANES, :]
        qbd_ref[pp, 0:hd, 0:bq] = qb[0:hd]
        qbd_ref[pp, 0:hd, bq:2 * bq] = zero
        qbd_ref[pp, hd:2 * hd, 0:bq] = zero
        qbd_ref[pp, hd:2 * hd, bq:2 * bq] = qb[hd:2 * hd]
        qbd_ref[pp, 2 * hd:2 * hd + LANES, :] = ones
    acc_ref[...] = jnp.zeros_like(acc_ref)

    nq = ka_ref.shape[0]

    def scores(s_ref, j):
        for pp in range(ATT_PAIRS):
            s_ref[pp] = _dot(ka_ref[j, :, pp * 2 * LANES:(pp + 1) * 2 * LANES], qbd_ref[pp])

    def attend(s_ref, j, carry, masked):
        out = []
        for pp in range(ATT_PAIRS):
            m_prev, l_prev = carry[2 * pp], carry[2 * pp + 1]
            s = s_ref[pp]
            if masked:
                key = j * bq + lax.broadcasted_iota(I32, s.shape, 0)
                qq = lax.broadcasted_iota(I32, s.shape, 1)
                qq = i * bq + jnp.where(qq >= bq, qq - bq, qq)
                s = jnp.where(key <= qq, s, -jnp.inf)
            m_new = jnp.maximum(m_prev, jnp.max(s, axis=0, keepdims=True))
            alpha = jnp.exp2(m_prev - m_new)
            p = jnp.exp2(s - m_new)
            l_new = alpha * l_prev + jnp.sum(p, axis=0, keepdims=True)
            pb = p.astype(BF16)
            vb = vt_ref[jnp.minimum(j, nq - 1), pp * LANES:(pp + 1) * LANES, :]
            r0 = pp * LANES
            acc_ref[r0:r0 + hd] = alpha[:, 0:bq] * acc_ref[r0:r0 + hd] + _dot(vb[0:hd], pb[:, 0:bq])
            acc_ref[r0 + hd:r0 + 2 * hd] = (alpha[:, bq:2 * bq] * acc_ref[r0 + hd:r0 + 2 * hd]
                                            + _dot(vb[hd:2 * hd], pb[:, bq:2 * bq]))
            out.extend((m_new, l_new))
        return tuple(out)

    def double_step(m, carry):
        j = 2 * m
        scores(sb_ref, j + 1)
        carry = attend(sa_ref, j, carry, False)
        scores(sa_ref, j + 2)
        return attend(sb_ref, j + 1, carry, False)

    init = (jnp.full((1, 2 * bq), -jnp.inf, F32), jnp.zeros((1, 2 * bq), F32)) * ATT_PAIRS
    scores(sa_ref, 0)
    carry = lax.fori_loop(0, i // 2, double_step, init)
    j = 2 * (i // 2)
    scores(sb_ref, jnp.minimum(j + 1, nq - 1))
    carry = attend(sa_ref, j, carry, True)
    carry = attend(sb_ref, j + 1, carry, True)
    for pp in range(ATT_PAIRS):
        l = carry[2 * pp + 1]
        r0 = pp * LANES
        o_ref[r0:r0 + hd] = acc_ref[r0:r0 + hd] / l[:, 0:bq]
        o_ref[r0 + hd:r0 + 2 * hd] = acc_ref[r0 + hd:r0 + 2 * hd] / l[:, bq:2 * bq]


def _attn_p(qt, kaug, vt, batch, t):
    blk = ATT_BLK
    nq = t // blk
    ngrp = N_HEADS // 2 // ATT_PAIRS
    rows = ATT_PAIRS * LANES
    qt4 = qt.reshape(batch, nq, D_ATT, blk)
    vt4 = vt.reshape(batch, nq, D_ATT, blk)
    ka4 = kaug.reshape(batch, nq, blk, 2 * D_ATT)
    return pl.pallas_call(
        _attn_p_kernel,
        out_shape=jax.ShapeDtypeStruct((batch, D_ATT, t), F32),
        grid=(batch, ngrp, nq),
        in_specs=[pl.BlockSpec((None, None, rows, blk), lambda b, p, i: (b, i, p, 0)),
                  pl.BlockSpec((None, nq, blk, 2 * rows), lambda b, p, i: (b, 0, 0, p)),
                  pl.BlockSpec((None, nq, rows, blk), lambda b, p, i: (b, 0, p, 0))],
        out_specs=pl.BlockSpec((None, rows, blk), lambda b, p, i: (b, p, i)),
        scratch_shapes=[pltpu.VMEM((ATT_PAIRS, 2 * LANES, 2 * blk), BF16), pltpu.VMEM((rows, blk), F32),
                        pltpu.VMEM((ATT_PAIRS, blk, 2 * blk), F32), pltpu.VMEM((ATT_PAIRS, blk, 2 * blk), F32)],
        compiler_params=_cparams(("arbitrary", "arbitrary", "arbitrary")),
        name="attn_p",
    )(qt4, ka4, vt4)


def _lane_cumsum(x):
    n = x.shape[-1]
    lane = lax.broadcasted_iota(I32, x.shape, x.ndim - 1)
    s = 1
    while s < n:
        x = x + jnp.where(lane >= s, pltpu.roll(x, s, x.ndim - 1), 0.0)
        s *= 2
    return x


def _fprep_s_kernel(cl_ref, ln_ref, fc_ref, fn_ref):
    b, h, p = cl_ref.shape
    cum = _lane_cumsum(cl_ref[...].reshape(b * h, p))
    fc_ref[...] = (-cum).reshape(b, h, p)
    total = cum[:, p - 1:p]
    cn = _lane_cumsum(ln_ref[...].reshape(b * h, LANES))
    fn_ref[...] = (-(total + cn)).reshape(b, h, LANES)


def _fprep_s(cache_lf_t, new_lf_t):
    b, h, p = cache_lf_t.shape
    return pl.pallas_call(
        _fprep_s_kernel,
        out_shape=(jax.ShapeDtypeStruct((b, h, p), F32), jax.ShapeDtypeStruct((b, h, LANES), F32)),
        grid=(1,),
        in_specs=[pl.BlockSpec((b, h, p), lambda i: (0, 0, 0)), pl.BlockSpec((b, h, LANES), lambda i: (0, 0, 0))],
        out_specs=(pl.BlockSpec((b, h, p), lambda i: (0, 0, 0)), pl.BlockSpec((b, h, LANES), lambda i: (0, 0, 0))),
        compiler_params=_cparams(("arbitrary",)),
        name="fprep_s",
    )(cache_lf_t, new_lf_t)


def _attn_s_kernel(q_ref, ck_ref, cv_ref, kn_ref, vn_ref, fc_ref, fn_ref, o_ref, qbd_ref, m_ref, l_ref, acc_ref,
                   *, tq):
    j = pl.program_id(1)
    nkv = pl.num_programs(1)
    rows = N_HEADS * tq
    bk = ck_ref.shape[0]

    @pl.when(j == 0)
    def _():
        q = q_ref[...]
        qrep = jnp.broadcast_to(q[None], (N_HEADS, tq, D_ATT)).reshape(rows, D_ATT)
        rh = lax.broadcasted_iota(I32, (rows, D_ATT), 0) // tq
        ch = lax.broadcasted_iota(I32, (rows, D_ATT), 1) // HEAD_DIM
        qbd_ref[...] = jnp.where(rh == ch, qrep, jnp.zeros_like(qrep))
        m_ref[...] = jnp.full_like(m_ref, -jnp.inf)
        l_ref[...] = jnp.zeros_like(l_ref)
        acc_ref[...] = jnp.zeros_like(acc_ref)

    def update(s, vb):
        m_prev = m_ref[...]
        m_new = jnp.maximum(m_prev, jnp.max(s, axis=1, keepdims=True))
        alpha = jnp.exp(m_prev - m_new)
        p = jnp.exp(s - m_new)
        l_ref[...] = alpha * l_ref[...] + jnp.sum(p, axis=1, keepdims=True)
        m_ref[...] = m_new
        acc_ref[...] = alpha * acc_ref[...] + _dot(p.astype(BF16), vb)

    def bias(f, width):
        return jnp.broadcast_to(f[:, None, :], (N_HEADS, tq, width)).reshape(rows, width)

    s = _dot_nt(qbd_ref[...], ck_ref[...].astype(BF16)) + bias(fc_ref[...], bk)
    update(s, cv_ref[...].astype(BF16))

    @pl.when(j == nkv - 1)
    def _():
        s2 = _dot_nt(qbd_ref[...], kn_ref[...].astype(BF16)) + bias(fn_ref[...][:, 0:tq], tq)
        key = lax.broadcasted_iota(I32, (rows, tq), 1)
        qq = lax.broadcasted_iota(I32, (rows, tq), 0) % tq
        update(jnp.where(key <= qq, s2, -jnp.inf), vn_ref[...].astype(BF16))
        o = acc_ref[...] / l_ref[...]
        rh = lax.broadcasted_iota(I32, (rows, D_ATT), 0) // tq
        ch = lax.broadcasted_iota(I32, (rows, D_ATT), 1) // HEAD_DIM
        o = jnp.where(rh == ch, o, 0.0).reshape(N_HEADS, tq, D_ATT)
        o_ref[...] = jnp.sum(o, axis=0)


def _attn_s(q, cache_k, cache_v, k_new, v_new, fc, fn):
    b, tq, _ = q.shape
    p = cache_k.shape[1]
    bk = min(SAMPLE_KV_BLK, p)
    nkv = p // bk
    rows = N_HEADS * tq
    tok = lambda: pl.BlockSpec((None, tq, D_ATT), lambda bi, j: (bi, 0, 0))
    return pl.pallas_call(
        functools.partial(_attn_s_kernel, tq=tq),
        out_shape=jax.ShapeDtypeStruct((b, tq, D_ATT), F32),
        grid=(b, nkv),
        in_specs=[tok(),
                  pl.BlockSpec((None, bk, D_ATT), lambda bi, j: (bi, j, 0)),
                  pl.BlockSpec((None, bk, D_ATT), lambda bi, j: (bi, j, 0)),
                  tok(), tok(),
                  pl.BlockSpec((None, N_HEADS, bk), lambda bi, j: (bi, 0, j)),
                  pl.BlockSpec((None, N_HEADS, LANES), lambda bi, j: (bi, 0, 0))],
        out_specs=tok(),
        scratch_shapes=[pltpu.VMEM((rows, D_ATT), BF16), pltpu.VMEM((rows, 1), F32),
                        pltpu.VMEM((rows, 1), F32), pltpu.VMEM((rows, D_ATT), F32)],
        compiler_params=_cparams(("arbitrary", "arbitrary")),
        name="attn_s",
    )(q, cache_k, cache_v, k_new, v_new, fc, fn)


def _outproj_kernel(x_ref, y4_ref, att_ref, gt_ref, sh_ref, sc_ref, gf_ref, wglu_ref, bglu_ref, wout_ref,
                    wr_ref, br_ref, cin_ref,
                    x1_ref, h2_ref, idx_ref, gate_ref, rank_ref, cnt_ref, carry_ref, *, att_transposed):
    i = pl.program_id(0)
    nb, tt, d = x_ref.shape
    tm = nb * tt

    @pl.when(i == 0)
    def _():
        carry_ref[...] = cin_ref[...]

    ys = jnp.concatenate([y4_ref[o] for o in range(N_OCT)], axis=-1)
    ysf = ys.astype(F32)
    glu = ysf * jax.nn.sigmoid(_dot(ys, wglu_ref[...]) + bglu_ref[...])
    att = att_ref[...]
    if att_transposed:
        att = att.T
    mix = _dot(glu.astype(BF16), wout_ref[0:D_SSM, :]) + _dot(att.astype(BF16), wout_ref[D_SSM:, :])
    x1 = x_ref[...] + gt_ref[...] * mix.reshape(nb, tt, d)
    x1_ref[...] = x1
    h2 = _norm_mod(x1, gf_ref[...], sc_ref[...], sh_ref[...]).reshape(tm, d)
    h2_ref[...] = h2

    logits = _dot_hi(h2, wr_ref[...]) + br_ref[...]
    lane = lax.broadcasted_iota(I32, logits.shape, 1)
    work = logits
    vals, idxs = [], []
    for _ in range(TOP_K):
        mx = jnp.max(work, axis=-1, keepdims=True)
        ix = jnp.min(jnp.where(work == mx, lane, N_EXPERTS), axis=-1, keepdims=True)
        vals.append(mx)
        idxs.append(ix)
        work = jnp.where(lane == ix, -jnp.inf, work)
    ex = [jnp.exp(v - vals[0]) for v in vals]
    den = ex[0] + ex[1] + ex[2] + ex[3]
    chosen = work == -jnp.inf
    mh = jnp.where(chosen, 1.0, 0.0)
    r = lax.broadcasted_iota(I32, (tm, tm), 0)
    c = lax.broadcasted_iota(I32, (tm, tm), 1)
    tri = jnp.where(c < r, 1.0, 0.0).astype(BF16)
    carry = carry_ref[0:1, 0:N_EXPERTS]
    before = _dot(tri, mh.astype(BF16)) + carry
    carry_ref[0:1, 0:N_EXPERTS] = carry + jnp.sum(mh, axis=0, keepdims=True)
    for kk in range(TOP_K):
        idx_ref[:, kk:kk + 1] = idxs[kk]
        gate_ref[:, kk:kk + 1] = ex[kk] / den
        rk = jnp.sum(jnp.where(lane == idxs[kk], before, 0.0), axis=-1, keepdims=True)
        rank_ref[:, kk:kk + 1] = rk.astype(I32)
    cnt_ref[...] = carry_ref[...]


def _outproj(x, y4, att, gt, sh, sc, g_ffn, w, counts_in, *, att_transposed, tm):
    b, t, d = x.shape
    n = b * t
    tm = min(tm, n)
    tt = min(t, tm)
    nb = tm // tt
    tpb = t // tt

    def xmap(i):
        return (i // tpb, i % tpb, 0) if nb == 1 else (i, 0, 0)

    def bmap(i):
        return (i // tpb, 0, 0) if nb == 1 else (i, 0, 0)

    full = lambda shape: pl.BlockSpec(shape, lambda i: (0,) * len(shape))
    if att_transposed:
        att_spec = pl.BlockSpec((None, D_ATT, tm), lambda i: (i // tpb, 0, i % tpb))
    else:
        att_spec = pl.BlockSpec((tm, D_ATT), lambda i: (i, 0))
    tok = lambda w_, dt: (jax.ShapeDtypeStruct((n, w_), dt), pl.BlockSpec((tm, w_), lambda i: (i, 0)))
    outs = [(jax.ShapeDtypeStruct((b, t, d), F32), pl.BlockSpec((nb, tt, d), xmap)),
            tok(d, F32), tok(TOP_K, I32), tok(TOP_K, F32), tok(TOP_K, I32),
            (jax.ShapeDtypeStruct((8, LANES), F32), full((8, LANES)))]
    return pl.pallas_call(
        functools.partial(_outproj_kernel, att_transposed=att_transposed),
        out_shape=tuple(o[0] for o in outs),
        grid=(n // tm,),
        in_specs=[pl.BlockSpec((nb, tt, d), xmap),
                  pl.BlockSpec((N_OCT, tm, LANES), lambda i: (0, i, 0)),
                  att_spec,
                  pl.BlockSpec((nb, 1, d), bmap), pl.BlockSpec((nb, 1, d), bmap), pl.BlockSpec((nb, 1, d), bmap),
                  full((1, d)), full((D_SSM, D_SSM)), full((1, D_SSM)), full((d, d)),
                  full((d, N_EXPERTS)), full((1, N_EXPERTS)), full((8, LANES))],
        out_specs=tuple(o[1] for o in outs),
        scratch_shapes=[pltpu.VMEM((8, LANES), F32)],
        compiler_params=_cparams(("arbitrary",)),
        name="outproj_t" if att_transposed else "outproj",
    )(x, y4, att, gt, sh, sc, g_ffn, w["wglu"], w["bglu"], w["wout"], w["wr"], w["br"], counts_in)


def _dispatch_kernel(off_ref, zst_ref, nu_ref, idx_ref, rank_ref, h_ref, xs_ref, zero_ref, zsem, sem, *, tm):
    i = pl.program_id(0)
    n_blocks = xs_ref.shape[0] // MOE_BLK

    @pl.when(i == 0)
    def _():
        zero_ref[...] = jnp.zeros_like(zero_ref)

        def zero_copy(start):
            start = pl.multiple_of(start, MOE_BLK)
            return pltpu.make_async_copy(zero_ref, xs_ref.at[pl.ds(start, MOE_BLK)], zsem)

        for e in range(N_EXPERTS):
            @pl.when(zst_ref[e] >= 0)
            def _():
                zero_copy(jnp.maximum(zst_ref[e], 0)).start()

        def tail_start(j, carry):
            zero_copy(j * MOE_BLK).start()
            return carry

        def tail_wait(j, carry):
            zero_copy(j * MOE_BLK).wait()
            return carry

        lax.fori_loop(nu_ref[0], n_blocks, tail_start, 0)
        for e in range(N_EXPERTS):
            @pl.when(zst_ref[e] >= 0)
            def _():
                zero_copy(jnp.maximum(zst_ref[e], 0)).wait()
        lax.fori_loop(nu_ref[0], n_blocks, tail_wait, 0)

    def issue(t, carry):
        for kk in range(TOP_K):
            a = t * TOP_K + kk
            dst = off_ref[idx_ref[0, 0, a]] + rank_ref[0, 0, a]
            pltpu.make_async_copy(h_ref.at[pl.ds(t, 1)], xs_ref.at[pl.ds(dst, 1)], sem).start()
        return carry

    lax.fori_loop(0, tm, issue, 0, unroll=4)

    def drain(t, carry):
        for kk in range(TOP_K):
            pltpu.make_async_copy(h_ref.at[pl.ds(0, 1)], xs_ref.at[pl.ds(0, 1)], sem).wait()
        return carry

    lax.fori_loop(0, tm, drain, 0)


def _dispatch(h2, idx, rank, off, zstart, n_used, n_rows, tm):
    n, d = h2.shape
    nt = n // tm
    idx3 = idx.reshape(nt, 1, tm * TOP_K)
    rank3 = rank.reshape(nt, 1, tm * TOP_K)
    smem_blk = lambda: pl.BlockSpec((1, 1, tm * TOP_K), lambda i, *_: (i, 0, 0), memory_space=pltpu.SMEM)
    return pl.pallas_call(
        functools.partial(_dispatch_kernel, tm=tm),
        out_shape=jax.ShapeDtypeStruct((n_rows, d), F32),
        grid_spec=pltpu.PrefetchScalarGridSpec(
            num_scalar_prefetch=3, grid=(nt,),
            in_specs=[smem_blk(), smem_blk(), pl.BlockSpec((tm, d), lambda i, *_: (i, 0))],
            out_specs=pl.BlockSpec(memory_space=pl.ANY),
            scratch_shapes=[pltpu.VMEM((MOE_BLK, d), F32), pltpu.SemaphoreType.DMA, pltpu.SemaphoreType.DMA]),
        compiler_params=_cparams(("arbitrary",)),
        name="moe_dispatch",
    )(off, zstart, n_used, idx3, rank3, h2)


def _expert_kernel(be_ref, nu_ref, xs_ref, wup_ref, bup_ref, wdn_ref, bdn_ref, ys_ref, wup_bf, wdn_bf):
    i = pl.program_id(0)
    e = be_ref[i]
    prev = be_ref[jnp.maximum(i - 1, 0)]

    @pl.when(i < nu_ref[0])
    def _():
        @pl.when((i == 0) | (prev != e))
        def _():
            wup_bf[...] = wup_ref[...].astype(BF16)
            wdn_bf[...] = wdn_ref[...].astype(BF16)

        up = _dot(xs_ref[...].astype(BF16), wup_bf[...]) + bup_ref[...]
        x_glu = jnp.minimum(up[:, :D_FF], SWIGLU_LIMIT)
        x_lin = jnp.clip(up[:, D_FF:], -SWIGLU_LIMIT, SWIGLU_LIMIT)
        act = (x_lin + 1.0) * (x_glu * jax.nn.sigmoid(SWIGLU_ALPHA * x_glu))
        ys_ref[...] = _dot(act.astype(BF16), wdn_bf[...]) + bdn_ref[...]

    @pl.when(i >= nu_ref[0])
    def _():
        ys_ref[...] = jnp.zeros_like(ys_ref)


def _experts(xs, block_e, n_used, w_up, b_up, w_down, b_down):
    n_rows, d = xs.shape
    nblk = n_rows // MOE_BLK
    rowmap = lambda i, be, nu: (jnp.minimum(i, nu[0] - 1), 0)
    emap = lambda i, be, nu: (be[i], 0, 0)
    return pl.pallas_call(
        _expert_kernel,
        out_shape=jax.ShapeDtypeStruct((n_rows, d), F32),
        grid_spec=pltpu.PrefetchScalarGridSpec(
            num_scalar_prefetch=2, grid=(nblk,),
            in_specs=[pl.BlockSpec((MOE_BLK, d), rowmap),
                      pl.BlockSpec((None, d, 2 * D_FF), emap), pl.BlockSpec((None, 1, 2 * D_FF), emap),
                      pl.BlockSpec((None, D_FF, d), emap), pl.BlockSpec((None, 1, d), emap)],
            out_specs=pl.BlockSpec((MOE_BLK, d), lambda i, be, nu: (i, 0)),
            scratch_shapes=[pltpu.VMEM((d, 2 * D_FF), BF16), pltpu.VMEM((D_FF, d), BF16)]),
        compiler_params=_cparams(("arbitrary",)),
        name="moe_experts",
    )(block_e, n_used, xs, w_up, b_up, w_down, b_down)


def _combine_kernel(off_ref, idx_ref, rank_ref, x1_ref, gate_ref, gt_ref, gfin_ref, ys_ref, y_ref, buf_ref, sem,
                    *, tm):
    nb, tt, d = x1_ref.shape

    def issue(t, carry):
        for kk in range(TOP_K):
            a = t * TOP_K + kk
            src = off_ref[idx_ref[0, 0, a]] + rank_ref[0, 0, a]
            pltpu.make_async_copy(ys_ref.at[pl.ds(src, 1)], buf_ref.at[kk, pl.ds(t, 1)], sem).start()
        return carry

    lax.fori_loop(0, tm, issue, 0)

    def drain(t, carry):
        for kk in range(TOP_K):
            pltpu.make_async_copy(ys_ref.at[pl.ds(0, 1)], buf_ref.at[kk, pl.ds(0, 1)], sem).wait()
        return carry

    lax.fori_loop(0, tm, drain, 0)
    gate = gate_ref[...]
    moe = gate[:, 0:1] * buf_ref[0]
    for kk in range(1, TOP_K):
        moe = moe + gate[:, kk:kk + 1] * buf_ref[kk]
    x2 = x1_ref[...] + gt_ref[...] * moe.reshape(nb, tt, d)
    ms = jnp.mean(x2 * x2, axis=-1, keepdims=True)
    y_ref[...] = x2 * lax.rsqrt(ms + NORM_EPS) * gfin_ref[...]


def _combine(x1, gate, idx, rank, off, ys, gt, g_final, tm):
    b, t, d = x1.shape
    n = b * t
    tm = min(tm, n)
    tt = min(t, tm)
    nb = tm // tt
    tpb = t // tt
    nt = n // tm
    idx3 = idx.reshape(nt, 1, tm * TOP_K)
    rank3 = rank.reshape(nt, 1, tm * TOP_K)

    def xmap(i, *_):
        return (i // tpb, i % tpb, 0) if nb == 1 else (i, 0, 0)

    def bmap(i, *_):
        return (i // tpb, 0, 0) if nb == 1 else (i, 0, 0)

    smem_blk = lambda: pl.BlockSpec((1, 1, tm * TOP_K), lambda i, *_: (i, 0, 0), memory_space=pltpu.SMEM)
    return pl.pallas_call(
        functools.partial(_combine_kernel, tm=tm),
        out_shape=jax.ShapeDtypeStruct((b, t, d), F32),
        grid_spec=pltpu.PrefetchScalarGridSpec(
            num_scalar_prefetch=1, grid=(nt,),
            in_specs=[smem_blk(), smem_blk(),
                      pl.BlockSpec((nb, tt, d), xmap),
                      pl.BlockSpec((tm, TOP_K), lambda i, *_: (i, 0)),
                      pl.BlockSpec((nb, 1, d), bmap),
                      pl.BlockSpec((1, d), lambda i, *_: (0, 0)),
                      pl.BlockSpec(memory_space=pl.ANY)],
            out_specs=pl.BlockSpec((nb, tt, d), xmap),
            scratch_shapes=[pltpu.VMEM((TOP_K, tm, d), F32), pltpu.SemaphoreType.DMA]),
        compiler_params=_cparams(("arbitrary",)),
        name="moe_combine",
    )(off, idx3, rank3, x1, gate, gt, g_final, ys)


def _moe(x1, h2, idx, gate, rank, counts, gt_ffn, g_final, w_up, b_up, w_down, b_down):
    n, d = h2.shape
    cnt = counts[0, :N_EXPERTS].astype(I32)
    padded = (cnt + MOE_BLK - 1) // MOE_BLK * MOE_BLK
    pad_end = jnp.cumsum(padded)
    off = pad_end - padded
    n_blocks = -(-n * TOP_K // MOE_BLK) + N_EXPERTS
    n_rows = n_blocks * MOE_BLK
    starts = jnp.arange(n_blocks, dtype=I32) * MOE_BLK
    block_e = jnp.minimum(jnp.sum((pad_end[None, :] <= starts[:, None]).astype(I32), axis=1), N_EXPERTS - 1)
    n_used = (pad_end[-1:] // MOE_BLK).astype(I32)
    zstart = jnp.where(padded > 0, pad_end - MOE_BLK, -1).astype(I32)
    tm = min(256, n)
    xs = _dispatch(h2, idx, rank, off.astype(I32), zstart, n_used, n_rows, tm)
    ys = _experts(xs, block_e, n_used, w_up, b_up, w_down, b_down)
    return _combine(x1, gate, idx, rank, off.astype(I32), ys, gt_ffn, g_final, min(128, n))


def _stream(x, mod, w, ssm_ops, s0, cache, params):
    b, t, d = x.shape
    n = b * t
    sh_mix, sc_mix, gt_mix, sh_ffn, sc_ffn, gt_ffn = mod
    prompt = cache is None
    tm = 512
    u4, q, k, v, vt, logf = _inproj(x, sh_mix, sc_mix, params["g_mix"], w, transposed=prompt, tm=tm)
    n_chunks = t // SSM_CHUNK
    y4, s_fin = _ssm(u4.reshape(N_OCT, b * n_chunks, SSM_CHUNK * LANES), ssm_ops, s0, b, n_chunks)
    y4 = y4.reshape(N_OCT, n, LANES)
    if prompt:
        kaug = _fprep_p(logf, k, b, t)
        att = _attn_p(q, kaug, vt, b, t)
    else:
        cache_k, cache_v, cache_lf = cache
        p = cache_k.shape[1]
        cl_t = jnp.transpose(cache_lf, (0, 2, 1))
        ln_t = jnp.pad(jnp.transpose(logf.reshape(b, t, N_HEADS), (0, 2, 1)), ((0, 0), (0, 0), (0, LANES - t)))
        fc, fn = _fprep_s(cl_t, ln_t)
        att = _attn_s(q.reshape(b, t, D_ATT), cache_k.reshape(b, p, D_ATT), cache_v.reshape(b, p, D_ATT),
                      k.reshape(b, t, D_ATT), v.reshape(b, t, D_ATT), fc, fn).reshape(n, D_ATT)
    counts0 = jnp.zeros((8, LANES), F32)
    x1, h2, idx, gate, rank, counts = _outproj(x, y4, att, gt_mix, sh_ffn, sc_ffn, params["g_ffn"], w, counts0,
                                               att_transposed=prompt, tm=256)
    y = _moe(x1, h2, idx, gate, rank, counts, gt_ffn, params["g_final"],
             params["w_up"], params["b_up"], params["w_down"], params["b_down"])
    return y, k, v, logf, s_fin


def _state_in(re, im):
    b = re.shape[0]
    s = jnp.concatenate([re.reshape(b, N_OCT, OCT_STATE // 2), im.reshape(b, N_OCT, OCT_STATE // 2)], axis=-1)
    return jnp.transpose(s, (1, 0, 2))


def _state_out(s):
    b = s.shape[1]
    s = jnp.transpose(s, (1, 0, 2))
    half = OCT_STATE // 2
    re = s[:, :, :half].reshape(1, b, N_SSM_GROUPS, SSM_STATE)
    im = s[:, :, half:].reshape(1, b, N_SSM_GROUPS, SSM_STATE)
    return re, im


def kernel(x_prompt, x_sample, c_prompt, c_sample, cache_k, cache_v, cache_logf, state_ssm_re, state_ssm_im, w_ada, b_ada, g_mix, w_in, b_forget, ssm_log_dt, ssm_a_re, ssm_a_im, ssm_b_re, ssm_b_im, ssm_c_re, ssm_c_im, ssm_d, w_glu, b_glu, w_out, g_ffn, w_router, b_router, w_up, b_up, w_down, b_down, g_final):
    assert w_ada.shape[0] == 1, "single-layer trunk"
    d = D_MODEL
    bp, tp, _ = x_prompt.shape
    bs, ts, _ = x_sample.shape
    n_c = bp + bs
    c_all = jnp.pad(jnp.concatenate([c_prompt, c_sample], axis=0), ((0, -n_c % 8), (0, 0)))
    mod = _ada(c_all, w_ada[0], b_ada[0].reshape(1, 6 * d))
    mod_p = [mod[:bp, j * d:(j + 1) * d].reshape(bp, 1, d) for j in range(6)]
    mod_s = [mod[bp:n_c, j * d:(j + 1) * d].reshape(bs, 1, d) for j in range(6)]
    wi = w_in[0]
    o1, o2, o3, o4 = D_SSM, D_SSM + D_ATT, D_SSM + 2 * D_ATT, D_SSM + 3 * D_ATT
    w_q = wi[:, o1:o2].astype(BF16)
    w_v = wi[:, o3:o4].astype(BF16)
    base = {"wu": wi[:, :o1].astype(BF16), "wk": wi[:, o2:o3].astype(BF16), "wv": w_v,
            "wf": wi[:, o4:].astype(BF16), "bf": b_forget[0].reshape(1, N_HEADS),
            "wglu": w_glu[0].astype(BF16), "bglu": b_glu[0].reshape(1, D_SSM), "wout": w_out[0].astype(BF16),
            "wr": w_router[0], "br": b_router[0].reshape(1, N_EXPERTS)}
    w_p = dict(base, wq=w_q.T, wvt=w_v.T)
    w_s = dict(base, wq=w_q, wvt=jnp.zeros((8, LANES), BF16))
    params = {"g_mix": g_mix[0].reshape(1, d), "g_ffn": g_ffn[0].reshape(1, d), "g_final": g_final.reshape(1, d),
              "w_up": w_up[0], "b_up": b_up[0].reshape(N_EXPERTS, 1, 2 * D_FF),
              "w_down": w_down[0], "b_down": b_down[0].reshape(N_EXPERTS, 1, d)}
    ssm_ops = _ssm_operators(ssm_log_dt[0], ssm_a_re[0], ssm_a_im[0], ssm_b_re[0], ssm_b_im[0],
                             ssm_c_re[0], ssm_c_im[0], ssm_d[0])
    zero_state = jnp.zeros((N_OCT, bp, OCT_STATE), F32)
    y_p, k_p, v_p, f_p, s_p = _stream(x_prompt, mod_p, w_p, ssm_ops, zero_state, None, params)
    s0 = _state_in(state_ssm_re[0], state_ssm_im[0])
    y_s, k_s, v_s, f_s, s_s = _stream(x_sample, mod_s, w_s, ssm_ops, s0,
                                      (cache_k[0], cache_v[0], cache_logf[0]), params)
    sre_p, sim_p = _state_out(s_p)
    sre_s, sim_s = _state_out(s_s)
    hd = (N_HEADS, HEAD_DIM)
    return (y_p, y_s,
            k_p.reshape(1, bp, tp, *hd), v_p.reshape(1, bp, tp, *hd), f_p.reshape(1, bp, tp, N_HEADS), sre_p, sim_p,
            k_s.reshape(1, bs, ts, *hd), v_s.reshape(1, bs, ts, *hd), f_s.reshape(1, bs, ts, N_HEADS), sre_s, sim_s)
```

```python
import functools
import math

import jax
import jax.numpy as jnp
from jax import lax
from jax.experimental import pallas as pl
from jax.experimental.pallas import tpu as pltpu

F32 = jnp.float32
BF16 = jnp.bfloat16
I32 = jnp.int32

D_MODEL = 1024
D_SSM = 512
SSM_GROUP = 16
N_SSM_GROUPS = 32
SSM_STATE = 64
D_ATT = 512
HEAD_DIM = 64
N_HEADS = 8
N_EXPERTS = 32
TOP_K = 4
D_FF = 1024
SWIGLU_LIMIT = 7.0
SWIGLU_ALPHA = 1.702
NORM_EPS = 1e-5

LANES = 128
N_OCT = D_SSM // LANES
OCT_STATE = 2 * (N_SSM_GROUPS // N_OCT) * SSM_STATE
SSM_CHUNK = 8
SSM_ROWS = 256
ATT_BLK = 256
ATT_PAIRS = 2
LOG2E = math.log2(math.e)
SAMPLE_KV_BLK = 512
MOE_BLK = 256
VMEM_LIMIT = 52 * 1024 * 1024


def _cparams(sem, vmem=VMEM_LIMIT):
    return pltpu.CompilerParams(dimension_semantics=sem, vmem_limit_bytes=vmem)


def _dot(a, b):
    return jnp.dot(a, b, preferred_element_type=F32)


def _dot_nt(a, b):
    return lax.dot_general(a, b, (((1,), (1,)), ((), ())), preferred_element_type=F32)


def _dot_hi(a, b):
    return jnp.dot(a, b, preferred_element_type=F32, precision=lax.Precision.HIGHEST)


def _ada_kernel(c_ref, w_ref, b_ref, o_ref):
    c = c_ref[...]
    s = c * jax.nn.sigmoid(c)
    o_ref[...] = _dot_hi(s, w_ref[...]) + b_ref[...]


def _ada(c_all, w_ada, b_ada):
    m, d = c_all.shape
    n = w_ada.shape[1]
    return pl.pallas_call(
        _ada_kernel,
        out_shape=jax.ShapeDtypeStruct((m, n), F32),
        grid=(n // d,),
        in_specs=[pl.BlockSpec((m, d), lambda j: (0, 0)),
                  pl.BlockSpec((d, d), lambda j: (0, j)),
                  pl.BlockSpec((1, d), lambda j: (0, j))],
        out_specs=pl.BlockSpec((m, d), lambda j: (0, j)),
        compiler_params=_cparams(("arbitrary",)),
        name="ada",
    )(c_all, w_ada, b_ada)


def _norm_mod(x, g, scale, shift):
    ms = jnp.mean(x * x, axis=-1, keepdims=True)
    y = x * lax.rsqrt(ms + NORM_EPS) * g
    return y * (1.0 + scale) + shift


def _log_sigmoid(z):
    return jnp.minimum(z, 0.0) - jnp.log1p(jnp.exp(-jnp.abs(z)))


def _inproj_kernel(x_ref, sh_ref, sc_ref, g_ref, wu_ref, wq_ref, wk_ref, wv_ref, wvt_ref, wf_ref, bf_ref,
                   u4_ref, q_ref, k_ref, v_ref, vt_ref, lf_ref, *, transposed):
    nb, tt, d = x_ref.shape
    tm = nb * tt
    h = _norm_mod(x_ref[...], g_ref[...], sc_ref[...], sh_ref[...]).reshape(tm, d).astype(BF16)
    u = _dot(h, wu_ref[...])
    for o in range(N_OCT):
        u4_ref[o] = u[:, o * LANES:(o + 1) * LANES].astype(BF16)
    k_ref[...] = _dot(h, wk_ref[...])
    v_ref[...] = _dot(h, wv_ref[...])
    lf_ref[...] = _log_sigmoid(_dot(h, wf_ref[...]) + bf_ref[...])
    scale = HEAD_DIM ** -0.5
    if transposed:
        qt = (_dot_nt(wq_ref[...], h) * (scale * LOG2E)).astype(BF16)
        vt = _dot_nt(wvt_ref[...], h).astype(BF16)
        for j in range(tm // ATT_BLK):
            q_ref[j] = qt[:, j * ATT_BLK:(j + 1) * ATT_BLK]
            vt_ref[j] = vt[:, j * ATT_BLK:(j + 1) * ATT_BLK]
    else:
        q_ref[...] = (_dot(h, wq_ref[...]) * scale).astype(BF16)
        vt_ref[...] = jnp.zeros_like(vt_ref)


def _inproj(x, shift, scale, g_mix, w, *, transposed, tm):
    b, t, d = x.shape
    n = b * t
    tm = min(tm, n)
    tt = min(t, tm)
    nb = tm // tt
    tpb = t // tt
    grid = (n // tm,)

    def xmap(i):
        return (i // tpb, i % tpb, 0) if nb == 1 else (i, 0, 0)

    def bmap(i):
        return (i // tpb, 0, 0) if nb == 1 else (i, 0, 0)

    full = lambda shape: pl.BlockSpec(shape, lambda i: (0,) * len(shape))
    in_specs = [pl.BlockSpec((nb, tt, d), xmap),
                pl.BlockSpec((nb, 1, d), bmap), pl.BlockSpec((nb, 1, d), bmap),
                full((1, d)),
                full(w["wu"].shape), full(w["wq"].shape), full(w["wk"].shape), full(w["wv"].shape),
                full(w["wvt"].shape), full(w["wf"].shape), full((1, N_HEADS))]
    nblk = n // ATT_BLK
    if transposed:
        q_shape = jax.ShapeDtypeStruct((nblk, D_ATT, ATT_BLK), BF16)
        q_spec = pl.BlockSpec((tm // ATT_BLK, D_ATT, ATT_BLK), lambda i: (i, 0, 0))
        vt_shape, vt_spec = q_shape, q_spec
    else:
        q_shape = jax.ShapeDtypeStruct((n, D_ATT), BF16)
        q_spec = pl.BlockSpec((tm, D_ATT), lambda i: (i, 0))
        vt_shape = jax.ShapeDtypeStruct((8, LANES), BF16)
        vt_spec = pl.BlockSpec((8, LANES), lambda i: (0, 0))
    out_shape = (jax.ShapeDtypeStruct((N_OCT, n, LANES), BF16), q_shape,
                 jax.ShapeDtypeStruct((n, D_ATT), F32), jax.ShapeDtypeStruct((n, D_ATT), F32),
                 vt_shape, jax.ShapeDtypeStruct((n, N_HEADS), F32))
    out_specs = (pl.BlockSpec((N_OCT, tm, LANES), lambda i: (0, i, 0)), q_spec,
                 pl.BlockSpec((tm, D_ATT), lambda i: (i, 0)), pl.BlockSpec((tm, D_ATT), lambda i: (i, 0)),
                 vt_spec, pl.BlockSpec((tm, N_HEADS), lambda i: (i, 0)))
    return pl.pallas_call(
        functools.partial(_inproj_kernel, transposed=transposed),
        out_shape=out_shape, grid=grid, in_specs=in_specs, out_specs=out_specs,
        compiler_params=_cparams(("arbitrary",)),
        name="inproj_t" if transposed else "inproj",
    )(x, shift, scale, g_mix, w["wu"], w["wq"], w["wk"], w["wv"], w["wvt"], w["wf"], w["bf"])


def _ssm_operators(log_dt, a_re, a_im, b_re, b_im, c_re, c_im, d_skip):
    L, G, P, C = SSM_CHUNK, N_SSM_GROUPS, SSM_STATE, SSM_GROUP
    q = G // N_OCT
    lam = lax.complex(a_re, a_im)
    dt = jnp.exp(log_dt)[:, None]
    lam_dt = lam * dt
    lam_bar = jnp.exp(lam_dt)
    b_bar = ((lam_bar - 1.0) / lam)[:, :, None] * lax.complex(b_re, b_im)
    c_mat = lax.complex(c_re, c_im)
    pw = jnp.exp(lam_dt[None] * jnp.arange(L + 1, dtype=F32)[:, None, None])
    kern = jnp.real(jnp.einsum('gcp,dgp,gpk->dgck', c_mat, pw[:L], b_bar))
    ksm = jnp.transpose(kern, (1, 0, 3, 2)).reshape(N_OCT, q, L, C, C)
    ksm = jnp.transpose(ksm, (0, 2, 1, 3, 4)).reshape(N_OCT, L, q * C, C)
    pwr = pw[L - 1 - jnp.arange(L)]
    wb = jnp.einsum('jgp,gpk->jgkp', pwr, b_bar)
    wsm = jnp.stack([jnp.real(wb), jnp.imag(wb)], axis=0).reshape(2, L, N_OCT, q * C, P)
    wsm = jnp.transpose(wsm, (2, 1, 0, 3, 4))
    cl = c_mat[None] * pw[1:L + 1][:, :, None, :]
    vsm = jnp.stack([jnp.real(cl), -jnp.imag(cl)], axis=0).reshape(2, L, N_OCT, q, C, P)
    vsm = jnp.transpose(vsm, (2, 1, 0, 3, 5, 4)).reshape(N_OCT, L, OCT_STATE, C)
    tmat, wend, win = _ssm_prep(ksm, wsm, vsm, d_skip.reshape(N_OCT, 1, LANES))
    dec = pw[L].reshape(N_OCT, q * P)
    decay = jnp.concatenate([jnp.real(dec), jnp.imag(dec)], axis=-1).reshape(N_OCT, 1, OCT_STATE)
    return tmat, wend, win, decay.astype(F32)


def _ssm_prep_kernel(ksm_ref, wsm_ref, vsm_ref, d_ref, tmat_ref, wend_ref, win_ref):
    L, C, P = SSM_CHUNK, SSM_GROUP, SSM_STATE
    half = OCT_STATE // 2

    def spread(period, width):
        r = lax.broadcasted_iota(I32, (period, width), 0)
        c = lax.broadcasted_iota(I32, (period, width), 1)
        return jnp.where(c % period == r, 1.0, 0.0)

    def same_group(shape, row_div, row_mod, lane_div):
        r = lax.broadcasted_iota(I32, shape, 0)
        c = lax.broadcasted_iota(I32, shape, 1)
        return (r % row_mod) // row_div == c // lane_div

    e_c = spread(C, LANES)
    e_p = spread(P, half)
    grp = same_group((LANES, LANES), C, LANES, C)
    rr = lax.broadcasted_iota(I32, (LANES, LANES), 0)
    cc = lax.broadcasted_iota(I32, (LANES, LANES), 1)
    lag = []
    for d in range(L):
        blk = jnp.where(grp, _dot_hi(ksm_ref[d], e_c), 0.0)
        if d == 0:
            blk = blk + jnp.where(rr == cc, jnp.broadcast_to(d_ref[...], (LANES, LANES)), 0.0)
        lag.append(blk.astype(BF16))
    zero = jnp.zeros((LANES, LANES), BF16)
    for j in range(L):
        for t in range(L):
            tmat_ref[j * LANES:(j + 1) * LANES, t * LANES:(t + 1) * LANES] = lag[t - j] if t >= j else zero
    grp_e = same_group((LANES, half), C, LANES, P)
    for j in range(L):
        for part in range(2):
            blk = jnp.where(grp_e, _dot_hi(wsm_ref[j, part], e_p), 0.0)
            wend_ref[j * LANES:(j + 1) * LANES, part * half:(part + 1) * half] = blk.astype(BF16)
    grp_i = same_group((OCT_STATE, LANES), P, half, C)
    for t in range(L):
        blk = jnp.where(grp_i, _dot_hi(vsm_ref[t], e_c), 0.0)
        win_ref[:, t * LANES:(t + 1) * LANES] = blk.astype(BF16)


def _ssm_prep(ksm, wsm, vsm, dvec):
    kdim = SSM_CHUNK * LANES
    blk = lambda a: pl.BlockSpec((None,) + a.shape[1:], lambda o: (o,) + (0,) * (a.ndim - 1))
    out = lambda r, c: (jax.ShapeDtypeStruct((N_OCT, r, c), BF16), pl.BlockSpec((None, r, c), lambda o: (o, 0, 0)))
    outs = [out(kdim, kdim), out(kdim, OCT_STATE), out(OCT_STATE, kdim)]
    return pl.pallas_call(
        _ssm_prep_kernel,
        out_shape=tuple(o[0] for o in outs),
        grid=(N_OCT,),
        in_specs=[blk(ksm), blk(wsm), blk(vsm), blk(dvec)],
        out_specs=tuple(o[1] for o in outs),
        compiler_params=_cparams(("arbitrary",)),
        name="ssm_prep",
    )(ksm, wsm, vsm, dvec)


def _ssm_kernel(u_ref, tmat_ref, wend_ref, win_ref, dec_ref, s0_ref, y_ref, sfin_ref, e_ref, sp_ref, st_ref,
                *, nb, cpt):
    b = pl.program_id(1)
    i = pl.program_id(2)
    half = OCT_STATE // 2
    whole = nb == s0_ref.shape[0]

    @pl.when(i == 0)
    def _():
        st_ref[...] = s0_ref[...] if whole else s0_ref[pl.ds(b, 1), :]

    u = u_ref[...]
    e_ref[...] = _dot(u, wend_ref[...])
    dec = dec_ref[...]
    ar, ai = dec[:, :half], dec[:, half:]

    def per_batch(bb, carry):
        def per_chunk(r, st):
            row = bb * cpt + r
            sp_ref[pl.ds(row, 1), :] = st
            e = e_ref[pl.ds(row, 1), :]
            re, im = st[:, :half], st[:, half:]
            nre = ar * re - ai * im + e[:, :half]
            nim = ar * im + ai * re + e[:, half:]
            return jnp.concatenate([nre, nim], axis=-1)

        st = lax.fori_loop(0, cpt, per_chunk, st_ref[pl.ds(bb, 1), :])
        st_ref[pl.ds(bb, 1), :] = st
        return carry

    lax.fori_loop(0, nb, per_batch, 0)
    y = _dot(u, tmat_ref[...]) + _dot(sp_ref[...].astype(BF16), win_ref[...])
    y_ref[...] = jax.nn.gelu(y).astype(BF16)
    if whole:
        sfin_ref[...] = st_ref[...]
    else:
        sfin_ref[pl.ds(b, 1), :] = st_ref[...]


def _ssm(u4, ops, s0, batch, n_chunks):
    tmat, wend, win, decay = ops
    rows = batch * n_chunks
    kdim = SSM_CHUNK * LANES
    if n_chunks % SSM_ROWS == 0:
        nb, cpt = 1, SSM_ROWS
    else:
        nb, cpt = batch, n_chunks
    r = nb * cpt
    tiles = n_chunks // cpt
    grid = (N_OCT, batch // nb, tiles)
    wspec = lambda shape: pl.BlockSpec((None,) + shape, lambda o, b, i: (o, 0, 0))
    return pl.pallas_call(
        functools.partial(_ssm_kernel, nb=nb, cpt=cpt),
        out_shape=(jax.ShapeDtypeStruct((N_OCT, rows, kdim), BF16),
                   jax.ShapeDtypeStruct((N_OCT, batch, OCT_STATE), F32)),
        grid=grid,
        in_specs=[pl.BlockSpec((None, r, kdim), lambda o, b, i: (o, b * tiles + i, 0)),
                  wspec((kdim, kdim)), wspec((kdim, OCT_STATE)), wspec((OCT_STATE, kdim)), wspec((1, OCT_STATE)),
                  pl.BlockSpec((None, batch, OCT_STATE), lambda o, b, i: (o, 0, 0))],
        out_specs=(pl.BlockSpec((None, r, kdim), lambda o, b, i: (o, b * tiles + i, 0)),
                   pl.BlockSpec((None, batch, OCT_STATE), lambda o, b, i: (o, 0, 0))),
        scratch_shapes=[pltpu.VMEM((r, OCT_STATE), F32), pltpu.VMEM((r, OCT_STATE), F32),
                        pltpu.VMEM((nb, OCT_STATE), F32)],
        compiler_params=_cparams(("arbitrary", "arbitrary", "arbitrary")),
        name="ssm",
    )(u4, tmat, wend, win, decay, s0)


def _bias_placement():
    import numpy as np
    pm = np.zeros((3, N_HEADS, N_HEADS // 2 * LANES), np.float32)
    for piece in range(3):
        for h in range(N_HEADS):
            pm[piece, h, LANES * (h // 2) + 3 * (h % 2) + piece] = 1.0
    return jnp.asarray(pm)


def _fprep_p_kernel(lf_ref, k_ref, pm_ref, ka_ref, carry_ref):
    i = pl.program_id(1)

    @pl.when(i == 0)
    def _():
        carry_ref[...] = jnp.zeros_like(carry_ref)

    tile = lf_ref.shape[0]
    r = lax.broadcasted_iota(I32, (tile, tile), 0)
    c = lax.broadcasted_iota(I32, (tile, tile), 1)
    tri = (c <= r).astype(F32)
    cum = _dot_hi(tri, lf_ref[...]) + carry_ref[0:1, 0:N_HEADS]
    carry_ref[0:1, 0:N_HEADS] = cum[tile - 1:tile, :]
    nf = cum * (-LOG2E)
    hi = nf.astype(BF16).astype(F32)
    r1 = nf - hi
    mid = r1.astype(BF16).astype(F32)
    lo = (r1 - mid).astype(BF16).astype(F32)
    slab = _dot(hi, pm_ref[0]) + _dot(mid, pm_ref[1]) + _dot(lo, pm_ref[2])
    k = k_ref[...]
    for p in range(N_HEADS // 2):
        ka_ref[:, 2 * p * LANES:(2 * p + 1) * LANES] = k[:, p * LANES:(p + 1) * LANES].astype(BF16)
        ka_ref[:, (2 * p + 1) * LANES:(2 * p + 2) * LANES] = slab[:, p * LANES:(p + 1) * LANES].astype(BF16)


def _fprep_p(logf, k, batch, t):
    tile = ATT_BLK
    tiles = t // tile
    pm = _bias_placement()
    kaug = pl.pallas_call(
        _fprep_p_kernel,
        out_shape=jax.ShapeDtypeStruct((batch * t, 2 * D_ATT), BF16),
        grid=(batch, tiles),
        in_specs=[pl.BlockSpec((tile, N_HEADS), lambda b, i: (b * tiles + i, 0)),
                  pl.BlockSpec((tile, D_ATT), lambda b, i: (b * tiles + i, 0)),
                  pl.BlockSpec(pm.shape, lambda b, i: (0, 0, 0))],
        out_specs=pl.BlockSpec((tile, 2 * D_ATT), lambda b, i: (b * tiles + i, 0)),
        scratch_shapes=[pltpu.VMEM((8, LANES), F32)],
        compiler_params=_cparams(("arbitrary", "arbitrary")),
        name="fprep_p",
    )(logf, k, pm)
    return kaug.reshape(batch * tiles, tile, 2 * D_ATT)


def _attn_p_kernel(qt_ref, ka_ref, vt_ref, o_ref, qbd_ref, acc_ref, sa_ref, sb_ref):
    i = pl.program_id(2)
    bq = ATT_BLK
    hd = HEAD_DIM
    r = lax.broadcasted_iota(I32, (LANES, 2 * bq), 0)
    c = lax.broadcasted_iota(I32, (LANES, 2 * bq), 1)
    ones = jnp.where(((r < 3) & (c < bq)) | ((r >= 3) & (r < 6) & (c >= bq)), 1.0, 0.0).astype(BF16)
    zero = jnp.zeros((hd, bq), BF16)
    for pp in range(ATT_PAIRS):
        qb = qt_ref[pp * LANES:(pp + 1) * LANES, :]
        qbd_ref[pp, 0:hd, 0:bq] = qb[0:hd]
        qbd_ref[pp, 0:hd, bq:2 * bq] = zero
        qbd_ref[pp, hd:2 * hd, 0:bq] = zero
        qbd_ref[pp, hd:2 * hd, bq:2 * bq] = qb[hd:2 * hd]
        qbd_ref[pp, 2 * hd:2 * hd + LANES, :] = ones
    acc_ref[...] = jnp.zeros_like(acc_ref)

    nq = ka_ref.shape[0]

    def scores(s_ref, j):
        for pp in range(ATT_PAIRS):
            s_ref[pp] = _dot(ka_ref[j, :, pp * 2 * LANES:(pp + 1) * 2 * LANES], qbd_ref[pp])

    def attend(s_ref, j, carry, masked):
        out = []
        for pp in range(ATT_PAIRS):
            m_prev, l_prev = carry[2 * pp], carry[2 * pp + 1]
            s = s_ref[pp]
            if masked:
                key = j * bq + lax.broadcasted_iota(I32, s.shape, 0)
                qq = lax.broadcasted_iota(I32, s.shape, 1)
                qq = i * bq + jnp.where(qq >= bq, qq - bq, qq)
                s = jnp.where(key <= qq, s, -jnp.inf)
            m_new = jnp.maximum(m_prev, jnp.max(s, axis=0, keepdims=True))
            alpha = jnp.exp2(m_prev - m_new)
            p = jnp.exp2(s - m_new)
            l_new = alpha * l_prev + jnp.sum(p, axis=0, keepdims=True)
            pb = p.astype(BF16)
            vb = vt_ref[jnp.minimum(j, nq - 1), pp * LANES:(pp + 1) * LANES, :]
            r0 = pp * LANES
            acc_ref[r0:r0 + hd] = alpha[:, 0:bq] * acc_ref[r0:r0 + hd] + _dot(vb[0:hd], pb[:, 0:bq])
            acc_ref[r0 + hd:r0 + 2 * hd] = (alpha[:, bq:2 * bq] * acc_ref[r0 + hd:r0 + 2 * hd]
                                            + _dot(vb[hd:2 * hd], pb[:, bq:2 * bq]))
            out.extend((m_new, l_new))
        return tuple(out)

    def double_step(m, carry):
        j = 2 * m
        scores(sb_ref, j + 1)
        carry = attend(sa_ref, j, carry, False)
        scores(sa_ref, j + 2)
        return attend(sb_ref, j + 1, carry, False)

    init = (jnp.full((1, 2 * bq), -jnp.inf, F32), jnp.zeros((1, 2 * bq), F32)) * ATT_PAIRS
    scores(sa_ref, 0)
    carry = lax.fori_loop(0, i // 2, double_step, init)
    j = 2 * (i // 2)
    scores(sb_ref, jnp.minimum(j + 1, nq - 1))
    carry = attend(sa_ref, j, carry, True)
    carry = attend(sb_ref, j + 1, carry, True)
    for pp in range(ATT_PAIRS):
        l = carry[2 * pp + 1]
        r0 = pp * LANES
        o_ref[r0:r0 + hd] = acc_ref[r0:r0 + hd] / l[:, 0:bq]
        o_ref[r0 + hd:r0 + 2 * hd] = acc_ref[r0 + hd:r0 + 2 * hd] / l[:, bq:2 * bq]


def _attn_p(qt, kaug, vt, batch, t):
    blk = ATT_BLK
    nq = t // blk
    ngrp = N_HEADS // 2 // ATT_PAIRS
    rows = ATT_PAIRS * LANES
    qt4 = qt.reshape(batch, nq, D_ATT, blk)
    vt4 = vt.reshape(batch, nq, D_ATT, blk)
    ka4 = kaug.reshape(batch, nq, blk, 2 * D_ATT)
    return pl.pallas_call(
        _attn_p_kernel,
        out_shape=jax.ShapeDtypeStruct((batch, D_ATT, t), F32),
        grid=(batch, ngrp, nq),
        in_specs=[pl.BlockSpec((None, None, rows, blk), lambda b, p, i: (b, i, p, 0)),
                  pl.BlockSpec((None, nq, blk, 2 * rows), lambda b, p, i: (b, 0, 0, p)),
                  pl.BlockSpec((None, nq, rows, blk), lambda b, p, i: (b, 0, p, 0))],
        out_specs=pl.BlockSpec((None, rows, blk), lambda b, p, i: (b, p, i)),
        scratch_shapes=[pltpu.VMEM((ATT_PAIRS, 2 * LANES, 2 * blk), BF16), pltpu.VMEM((rows, blk), F32),
                        pltpu.VMEM((ATT_PAIRS, blk, 2 * blk), F32), pltpu.VMEM((ATT_PAIRS, blk, 2 * blk), F32)],
        compiler_params=_cparams(("arbitrary", "arbitrary", "arbitrary")),
        name="attn_p",
    )(qt4, ka4, vt4)


def _lane_cumsum(x):
    n = x.shape[-1]
    lane = lax.broadcasted_iota(I32, x.shape, x.ndim - 1)
    s = 1
    while s < n:
        x = x + jnp.where(lane >= s, pltpu.roll(x, s, x.ndim - 1), 0.0)
        s *= 2
    return x


def _fprep_s_kernel(cl_ref, ln_ref, fc_ref, fn_ref):
    b, h, p = cl_ref.shape
    cum = _lane_cumsum(cl_ref[...].reshape(b * h, p))
    fc_ref[...] = (-cum).reshape(b, h, p)
    total = cum[:, p - 1:p]
    cn = _lane_cumsum(ln_ref[...].reshape(b * h, LANES))
    fn_ref[...] = (-(total + cn)).reshape(b, h, LANES)


def _fprep_s(cache_lf_t, new_lf_t):
    b, h, p = cache_lf_t.shape
    return pl.pallas_call(
        _fprep_s_kernel,
        out_shape=(jax.ShapeDtypeStruct((b, h, p), F32), jax.ShapeDtypeStruct((b, h, LANES), F32)),
        grid=(1,),
        in_specs=[pl.BlockSpec((b, h, p), lambda i: (0, 0, 0)), pl.BlockSpec((b, h, LANES), lambda i: (0, 0, 0))],
        out_specs=(pl.BlockSpec((b, h, p), lambda i: (0, 0, 0)), pl.BlockSpec((b, h, LANES), lambda i: (0, 0, 0))),
        compiler_params=_cparams(("arbitrary",)),
        name="fprep_s",
    )(cache_lf_t, new_lf_t)


def _attn_s_kernel(q_ref, ck_ref, cv_ref, kn_ref, vn_ref, fc_ref, fn_ref, o_ref, qbd_ref, m_ref, l_ref, acc_ref,
                   *, tq):
    j = pl.program_id(1)
    nkv = pl.num_programs(1)
    rows = N_HEADS * tq
    bk = ck_ref.shape[0] // N_HEADS

    @pl.when(j == 0)
    def _():
        q = q_ref[...]
        qrep = jnp.broadcast_to(q[None], (N_HEADS, tq, D_ATT)).reshape(rows, D_ATT)
        rh = lax.broadcasted_iota(I32, (rows, D_ATT), 0) // tq
        ch = lax.broadcasted_iota(I32, (rows, D_ATT), 1) // HEAD_DIM
        qbd_ref[...] = jnp.where(rh == ch, qrep, jnp.zeros_like(qrep))
        m_ref[...] = jnp.full_like(m_ref, -jnp.inf)
        l_ref[...] = jnp.zeros_like(l_ref)
        acc_ref[...] = jnp.zeros_like(acc_ref)

    def update(s, vb):
        m_prev = m_ref[...]
        m_new = jnp.maximum(m_prev, jnp.max(s, axis=1, keepdims=True))
        alpha = jnp.exp(m_prev - m_new)
        p = jnp.exp(s - m_new)
        l_ref[...] = alpha * l_ref[...] + jnp.sum(p, axis=1, keepdims=True)
        m_ref[...] = m_new
        acc_ref[...] = alpha * acc_ref[...] + _dot(p.astype(BF16), vb)

    def bias(f, width):
        return jnp.broadcast_to(f[:, None, :], (N_HEADS, tq, width)).reshape(rows, width)

    def heads_to_lanes(ref):
        parts = [ref[pl.ds(h, bk, stride=N_HEADS), :] for h in range(N_HEADS)]
        return jnp.concatenate(parts, axis=-1).astype(BF16)

    s = _dot_nt(qbd_ref[...], heads_to_lanes(ck_ref)) + bias(fc_ref[...], bk)
    update(s, heads_to_lanes(cv_ref))

    @pl.when(j == nkv - 1)
    def _():
        s2 = _dot_nt(qbd_ref[...], kn_ref[...].astype(BF16)) + bias(fn_ref[...][:, 0:tq], tq)
        key = lax.broadcasted_iota(I32, (rows, tq), 1)
        qq = lax.broadcasted_iota(I32, (rows, tq), 0) % tq
        update(jnp.where(key <= qq, s2, -jnp.inf), vn_ref[...].astype(BF16))
        o = acc_ref[...] / l_ref[...]
        rh = lax.broadcasted_iota(I32, (rows, D_ATT), 0) // tq
        ch = lax.broadcasted_iota(I32, (rows, D_ATT), 1) // HEAD_DIM
        o = jnp.where(rh == ch, o, 0.0).reshape(N_HEADS, tq, D_ATT)
        o_ref[...] = jnp.sum(o, axis=0)


def _attn_s(q, cache_k, cache_v, k_new, v_new, fc, fn):
    b, tq, _ = q.shape
    p = cache_k.shape[1] // N_HEADS
    bk = min(SAMPLE_KV_BLK, p)
    nkv = p // bk
    rows = N_HEADS * tq
    tok = lambda: pl.BlockSpec((None, tq, D_ATT), lambda bi, j: (bi, 0, 0))
    return pl.pallas_call(
        functools.partial(_attn_s_kernel, tq=tq),
        out_shape=jax.ShapeDtypeStruct((b, tq, D_ATT), F32),
        grid=(b, nkv),
        in_specs=[tok(),
                  pl.BlockSpec((None, bk * N_HEADS, HEAD_DIM), lambda bi, j: (bi, j, 0)),
                  pl.BlockSpec((None, bk * N_HEADS, HEAD_DIM), lambda bi, j: (bi, j, 0)),
                  tok(), tok(),
                  pl.BlockSpec((None, N_HEADS, bk), lambda bi, j: (bi, 0, j)),
                  pl.BlockSpec((None, N_HEADS, LANES), lambda bi, j: (bi, 0, 0))],
        out_specs=tok(),
        scratch_shapes=[pltpu.VMEM((rows, D_ATT), BF16), pltpu.VMEM((rows, 1), F32),
                        pltpu.VMEM((rows, 1), F32), pltpu.VMEM((rows, D_ATT), F32)],
        compiler_params=_cparams(("arbitrary", "arbitrary")),
        name="attn_s",
    )(q, cache_k, cache_v, k_new, v_new, fc, fn)


def _outproj_kernel(x_ref, y4_ref, att_ref, gt_ref, sh_ref, sc_ref, gf_ref, wglu_ref, bglu_ref, wout_ref,
                    wr_ref, br_ref, cin_ref,
                    x1_ref, h2_ref, idx_ref, gate_ref, rank_ref, cnt_ref, carry_ref, *, att_transposed):
    i = pl.program_id(0)
    nb, tt, d = x_ref.shape
    tm = nb * tt

    @pl.when(i == 0)
    def _():
        carry_ref[...] = cin_ref[...]

    ys = jnp.concatenate([y4_ref[o] for o in range(N_OCT)], axis=-1)
    ysf = ys.astype(F32)
    glu = ysf * jax.nn.sigmoid(_dot(ys, wglu_ref[...]) + bglu_ref[...])
    att = att_ref[...]
    if att_transposed:
        att = att.T
    mix = _dot(glu.astype(BF16), wout_ref[0:D_SSM, :]) + _dot(att.astype(BF16), wout_ref[D_SSM:, :])
    x1 = x_ref[...] + gt_ref[...] * mix.reshape(nb, tt, d)
    x1_ref[...] = x1
    h2 = _norm_mod(x1, gf_ref[...], sc_ref[...], sh_ref[...]).reshape(tm, d)
    h2_ref[...] = h2

    logits = _dot_hi(h2, wr_ref[...]) + br_ref[...]
    lane = lax.broadcasted_iota(I32, logits.shape, 1)
    work = logits
    vals, idxs = [], []
    for _ in range(TOP_K):
        mx = jnp.max(work, axis=-1, keepdims=True)
        ix = jnp.min(jnp.where(work == mx, lane, N_EXPERTS), axis=-1, keepdims=True)
        vals.append(mx)
        idxs.append(ix)
        work = jnp.where(lane == ix, -jnp.inf, work)
    ex = [jnp.exp(v - vals[0]) for v in vals]
    den = ex[0] + ex[1] + ex[2] + ex[3]
    chosen = work == -jnp.inf
    mh = jnp.where(chosen, 1.0, 0.0)
    r = lax.broadcasted_iota(I32, (tm, tm), 0)
    c = lax.broadcasted_iota(I32, (tm, tm), 1)
    tri = jnp.where(c < r, 1.0, 0.0).astype(BF16)
    carry = carry_ref[0:1, 0:N_EXPERTS]
    before = _dot(tri, mh.astype(BF16)) + carry
    carry_ref[0:1, 0:N_EXPERTS] = carry + jnp.sum(mh, axis=0, keepdims=True)
    for kk in range(TOP_K):
        idx_ref[:, kk:kk + 1] = idxs[kk]
        gate_ref[:, kk:kk + 1] = ex[kk] / den
        rk = jnp.sum(jnp.where(lane == idxs[kk], before, 0.0), axis=-1, keepdims=True)
        rank_ref[:, kk:kk + 1] = rk.astype(I32)
    cnt_ref[...] = carry_ref[...]


def _outproj(x, y4, att, gt, sh, sc, g_ffn, w, counts_in, *, att_transposed, tm):
    b, t, d = x.shape
    n = b * t
    tm = min(tm, n)
    tt = min(t, tm)
    nb = tm // tt
    tpb = t // tt

    def xmap(i):
        return (i // tpb, i % tpb, 0) if nb == 1 else (i, 0, 0)

    def bmap(i):
        return (i // tpb, 0, 0) if nb == 1 else (i, 0, 0)

    full = lambda shape: pl.BlockSpec(shape, lambda i: (0,) * len(shape))
    if att_transposed:
        att_spec = pl.BlockSpec((None, D_ATT, tm), lambda i: (i // tpb, 0, i % tpb))
    else:
        att_spec = pl.BlockSpec((tm, D_ATT), lambda i: (i, 0))
    tok = lambda w_, dt: (jax.ShapeDtypeStruct((n, w_), dt), pl.BlockSpec((tm, w_), lambda i: (i, 0)))
    outs = [(jax.ShapeDtypeStruct((b, t, d), F32), pl.BlockSpec((nb, tt, d), xmap)),
            tok(d, F32), tok(TOP_K, I32), tok(TOP_K, F32), tok(TOP_K, I32),
            (jax.ShapeDtypeStruct((8, LANES), F32), full((8, LANES)))]
    return pl.pallas_call(
        functools.partial(_outproj_kernel, att_transposed=att_transposed),
        out_shape=tuple(o[0] for o in outs),
        grid=(n // tm,),
        in_specs=[pl.BlockSpec((nb, tt, d), xmap),
                  pl.BlockSpec((N_OCT, tm, LANES), lambda i: (0, i, 0)),
                  att_spec,
                  pl.BlockSpec((nb, 1, d), bmap), pl.BlockSpec((nb, 1, d), bmap), pl.BlockSpec((nb, 1, d), bmap),
                  full((1, d)), full((D_SSM, D_SSM)), full((1, D_SSM)), full((d, d)),
                  full((d, N_EXPERTS)), full((1, N_EXPERTS)), full((8, LANES))],
        out_specs=tuple(o[1] for o in outs),
        scratch_shapes=[pltpu.VMEM((8, LANES), F32)],
        compiler_params=_cparams(("arbitrary",)),
        name="outproj_t" if att_transposed else "outproj",
    )(x, y4, att, gt, sh, sc, g_ffn, w["wglu"], w["bglu"], w["wout"], w["wr"], w["br"], counts_in)


def _dispatch_kernel(zst_ref, nu_ref, dest_ref, h_ref, xs_ref, zero_ref, zsem, sem, *, tm):
    i = pl.program_id(0)
    n_blocks = xs_ref.shape[0] // MOE_BLK

    @pl.when(i == 0)
    def _():
        zero_ref[...] = jnp.zeros_like(zero_ref)

        def zero_copy(start):
            start = pl.multiple_of(start, MOE_BLK)
            return pltpu.make_async_copy(zero_ref, xs_ref.at[pl.ds(start, MOE_BLK)], zsem)

        for e in range(N_EXPERTS):
            @pl.when(zst_ref[e] >= 0)
            def _():
                zero_copy(jnp.maximum(zst_ref[e], 0)).start()

        def tail_start(j, carry):
            zero_copy(j * MOE_BLK).start()
            return carry

        def tail_wait(j, carry):
            zero_copy(j * MOE_BLK).wait()
            return carry

        lax.fori_loop(nu_ref[0], n_blocks, tail_start, 0)
        for e in range(N_EXPERTS):
            @pl.when(zst_ref[e] >= 0)
            def _():
                zero_copy(jnp.maximum(zst_ref[e], 0)).wait()
        lax.fori_loop(nu_ref[0], n_blocks, tail_wait, 0)

    def issue(t, carry):
        for kk in range(TOP_K):
            dst = dest_ref[0, 0, t * TOP_K + kk]
            pltpu.make_async_copy(h_ref.at[pl.ds(t, 1)], xs_ref.at[pl.ds(dst, 1)], sem).start()
        return carry

    lax.fori_loop(0, tm, issue, 0, unroll=8)
    n_copied = tm * TOP_K
    pltpu.make_async_copy(xs_ref.at[pl.ds(0, n_copied)], xs_ref.at[pl.ds(0, n_copied)], sem).wait()


def _dispatch(h2, dest, zstart, n_used, n_rows, tm):
    n, d = h2.shape
    nt = n // tm
    dest3 = dest.reshape(nt, 1, tm * TOP_K)
    return pl.pallas_call(
        functools.partial(_dispatch_kernel, tm=tm),
        out_shape=jax.ShapeDtypeStruct((n_rows, d), F32),
        grid_spec=pltpu.PrefetchScalarGridSpec(
            num_scalar_prefetch=2, grid=(nt,),
            in_specs=[pl.BlockSpec((1, 1, tm * TOP_K), lambda i, *_: (i, 0, 0), memory_space=pltpu.SMEM),
                      pl.BlockSpec((tm, d), lambda i, *_: (i, 0))],
            out_specs=pl.BlockSpec(memory_space=pl.ANY),
            scratch_shapes=[pltpu.VMEM((MOE_BLK, d), F32), pltpu.SemaphoreType.DMA, pltpu.SemaphoreType.DMA]),
        compiler_params=_cparams(("arbitrary",)),
        name="moe_dispatch",
    )(zstart, n_used, dest3, h2)


def _expert_kernel(be_ref, nu_ref, xs_ref, wup_ref, bup_ref, wdn_ref, bdn_ref, ys_ref, wup_bf, wdn_bf):
    i = pl.program_id(0)
    e = be_ref[i]
    prev = be_ref[jnp.maximum(i - 1, 0)]

    @pl.when(i < nu_ref[0])
    def _():
        @pl.when((i == 0) | (prev != e))
        def _():
            wup_bf[...] = wup_ref[...].astype(BF16)
            wdn_bf[...] = wdn_ref[...].astype(BF16)

        up = _dot(xs_ref[...].astype(BF16), wup_bf[...]) + bup_ref[...]
        x_glu = jnp.minimum(up[:, :D_FF], SWIGLU_LIMIT)
        x_lin = jnp.clip(up[:, D_FF:], -SWIGLU_LIMIT, SWIGLU_LIMIT)
        act = (x_lin + 1.0) * (x_glu * jax.nn.sigmoid(SWIGLU_ALPHA * x_glu))
        ys_ref[...] = _dot(act.astype(BF16), wdn_bf[...]) + bdn_ref[...]

    @pl.when(i >= nu_ref[0])
    def _():
        ys_ref[...] = jnp.zeros_like(ys_ref)


def _experts(xs, block_e, n_used, w_up, b_up, w_down, b_down):
    n_rows, d = xs.shape
    nblk = n_rows // MOE_BLK
    rowmap = lambda i, be, nu: (jnp.minimum(i, nu[0] - 1), 0)
    emap = lambda i, be, nu: (be[i], 0, 0)
    return pl.pallas_call(
        _expert_kernel,
        out_shape=jax.ShapeDtypeStruct((n_rows, d), F32),
        grid_spec=pltpu.PrefetchScalarGridSpec(
            num_scalar_prefetch=2, grid=(nblk,),
            in_specs=[pl.BlockSpec((MOE_BLK, d), rowmap),
                      pl.BlockSpec((None, d, 2 * D_FF), emap), pl.BlockSpec((None, 1, 2 * D_FF), emap),
                      pl.BlockSpec((None, D_FF, d), emap), pl.BlockSpec((None, 1, d), emap)],
            out_specs=pl.BlockSpec((MOE_BLK, d), lambda i, be, nu: (i, 0)),
            scratch_shapes=[pltpu.VMEM((d, 2 * D_FF), BF16), pltpu.VMEM((D_FF, d), BF16)]),
        compiler_params=_cparams(("arbitrary",)),
        name="moe_experts",
    )(block_e, n_used, xs, w_up, b_up, w_down, b_down)


def _combine_kernel(dest_ref, x1_ref, gate_ref, gt_ref, gfin_ref, ys_ref, y_ref, buf_ref, sem, *, tm):
    nb, tt, d = x1_ref.shape

    def issue(t, carry):
        for kk in range(TOP_K):
            src = dest_ref[0, 0, t * TOP_K + kk]
            pltpu.make_async_copy(ys_ref.at[pl.ds(src, 1)], buf_ref.at[pl.ds(kk * tm + t, 1)], sem).start()
        return carry

    lax.fori_loop(0, tm, issue, 0, unroll=8)
    pltpu.make_async_copy(ys_ref.at[pl.ds(0, TOP_K * tm)], buf_ref, sem).wait()
    gate = gate_ref[...]
    moe = gate[:, 0:1] * buf_ref[0:tm]
    for kk in range(1, TOP_K):
        moe = moe + gate[:, kk:kk + 1] * buf_ref[kk * tm:(kk + 1) * tm]
    x2 = x1_ref[...] + gt_ref[...] * moe.reshape(nb, tt, d)
    ms = jnp.mean(x2 * x2, axis=-1, keepdims=True)
    y_ref[...] = x2 * lax.rsqrt(ms + NORM_EPS) * gfin_ref[...]


def _combine(x1, gate, dest, ys, gt, g_final, tm):
    b, t, d = x1.shape
    n = b * t
    tm = min(tm, n)
    tt = min(t, tm)
    nb = tm // tt
    tpb = t // tt
    nt = n // tm
    dest3 = dest.reshape(nt, 1, tm * TOP_K)

    def xmap(i, *_):
        return (i // tpb, i % tpb, 0) if nb == 1 else (i, 0, 0)

    def bmap(i, *_):
        return (i // tpb, 0, 0) if nb == 1 else (i, 0, 0)

    return pl.pallas_call(
        functools.partial(_combine_kernel, tm=tm),
        out_shape=jax.ShapeDtypeStruct((b, t, d), F32),
        grid=(nt,),
        in_specs=[pl.BlockSpec((1, 1, tm * TOP_K), lambda i: (i, 0, 0), memory_space=pltpu.SMEM),
                  pl.BlockSpec((nb, tt, d), xmap),
                  pl.BlockSpec((tm, TOP_K), lambda i: (i, 0)),
                  pl.BlockSpec((nb, 1, d), bmap),
                  pl.BlockSpec((1, d), lambda i: (0, 0)),
                  pl.BlockSpec(memory_space=pl.ANY)],
        out_specs=pl.BlockSpec((nb, tt, d), xmap),
        scratch_shapes=[pltpu.VMEM((TOP_K * tm, d), F32), pltpu.SemaphoreType.DMA],
        compiler_params=_cparams(("arbitrary",)),
        name="moe_combine",
    )(dest3, x1, gate, gt, g_final, ys)


def _moe(x1, h2, idx, gate, rank, counts, gt_ffn, g_final, w_up, b_up, w_down, b_down):
    n, d = h2.shape
    cnt = counts[0, :N_EXPERTS].astype(I32)
    padded = (cnt + MOE_BLK - 1) // MOE_BLK * MOE_BLK
    pad_end = jnp.cumsum(padded)
    off = pad_end - padded
    n_blocks = -(-n * TOP_K // MOE_BLK) + N_EXPERTS
    n_rows = n_blocks * MOE_BLK
    starts = jnp.arange(n_blocks, dtype=I32) * MOE_BLK
    block_e = jnp.minimum(jnp.sum((pad_end[None, :] <= starts[:, None]).astype(I32), axis=1), N_EXPERTS - 1)
    n_used = (pad_end[-1:] // MOE_BLK).astype(I32)
    zstart = jnp.where(padded > 0, pad_end - MOE_BLK, -1).astype(I32)
    dest = (jnp.sum(jnp.where(idx[:, :, None] == jnp.arange(N_EXPERTS, dtype=I32), off.astype(I32), 0), axis=-1)
            + rank).astype(I32)
    tm = min(256, n)
    xs = _dispatch(h2, dest, zstart, n_used, n_rows, tm)
    ys = _experts(xs, block_e, n_used, w_up, b_up, w_down, b_down)
    return _combine(x1, gate, dest, ys, gt_ffn, g_final, min(128, n))


def _stream(x, mod, w, ssm_ops, s0, cache, params):
    b, t, d = x.shape
    n = b * t
    sh_mix, sc_mix, gt_mix, sh_ffn, sc_ffn, gt_ffn = mod
    prompt = cache is None
    tm = 512
    u4, q, k, v, vt, logf = _inproj(x, sh_mix, sc_mix, params["g_mix"], w, transposed=prompt, tm=tm)
    n_chunks = t // SSM_CHUNK
    y4, s_fin = _ssm(u4.reshape(N_OCT, b * n_chunks, SSM_CHUNK * LANES), ssm_ops, s0, b, n_chunks)
    y4 = y4.reshape(N_OCT, n, LANES)
    if prompt:
        kaug = _fprep_p(logf, k, b, t)
        att = _attn_p(q, kaug, vt, b, t)
    else:
        cache_k, cache_v, cache_lf = cache
        p = cache_k.shape[1]
        cl_t = jnp.transpose(cache_lf, (0, 2, 1))
        ln_t = jnp.pad(jnp.transpose(logf.reshape(b, t, N_HEADS), (0, 2, 1)), ((0, 0), (0, 0), (0, LANES - t)))
        fc, fn = _fprep_s(cl_t, ln_t)
        att = _attn_s(q.reshape(b, t, D_ATT), cache_k.reshape(b, p * N_HEADS, HEAD_DIM),
                      cache_v.reshape(b, p * N_HEADS, HEAD_DIM),
                      k.reshape(b, t, D_ATT), v.reshape(b, t, D_ATT), fc, fn).reshape(n, D_ATT)
    counts0 = jnp.zeros((8, LANES), F32)
    x1, h2, idx, gate, rank, counts = _outproj(x, y4, att, gt_mix, sh_ffn, sc_ffn, params["g_ffn"], w, counts0,
                                               att_transposed=prompt, tm=256)
    y = _moe(x1, h2, idx, gate, rank, counts, gt_ffn, params["g_final"],
             params["w_up"], params["b_up"], params["w_down"], params["b_down"])
    return y, k, v, logf, s_fin


def _state_in(re, im):
    b = re.shape[0]
    s = jnp.concatenate([re.reshape(b, N_OCT, OCT_STATE // 2), im.reshape(b, N_OCT, OCT_STATE // 2)], axis=-1)
    return jnp.transpose(s, (1, 0, 2))


def _state_out(s):
    b = s.shape[1]
    s = jnp.transpose(s, (1, 0, 2))
    half = OCT_STATE // 2
    re = s[:, :, :half].reshape(1, b, N_SSM_GROUPS, SSM_STATE)
    im = s[:, :, half:].reshape(1, b, N_SSM_GROUPS, SSM_STATE)
    return re, im


def kernel(x_prompt, x_sample, c_prompt, c_sample, cache_k, cache_v, cache_logf, state_ssm_re, state_ssm_im, w_ada, b_ada, g_mix, w_in, b_forget, ssm_log_dt, ssm_a_re, ssm_a_im, ssm_b_re, ssm_b_im, ssm_c_re, ssm_c_im, ssm_d, w_glu, b_glu, w_out, g_ffn, w_router, b_router, w_up, b_up, w_down, b_down, g_final):
    assert w_ada.shape[0] == 1, "single-layer trunk"
    d = D_MODEL
    bp, tp, _ = x_prompt.shape
    bs, ts, _ = x_sample.shape
    n_c = bp + bs
    c_all = jnp.pad(jnp.concatenate([c_prompt, c_sample], axis=0), ((0, -n_c % 8), (0, 0)))
    mod = _ada(c_all, w_ada[0], b_ada[0].reshape(1, 6 * d))
    mod_p = [mod[:bp, j * d:(j + 1) * d].reshape(bp, 1, d) for j in range(6)]
    mod_s = [mod[bp:n_c, j * d:(j + 1) * d].reshape(bs, 1, d) for j in range(6)]
    wi = w_in[0]
    o1, o2, o3, o4 = D_SSM, D_SSM + D_ATT, D_SSM + 2 * D_ATT, D_SSM + 3 * D_ATT
    w_q = wi[:, o1:o2].astype(BF16)
    w_v = wi[:, o3:o4].astype(BF16)
    base = {"wu": wi[:, :o1].astype(BF16), "wk": wi[:, o2:o3].astype(BF16), "wv": w_v,
            "wf": wi[:, o4:].astype(BF16), "bf": b_forget[0].reshape(1, N_HEADS),
            "wglu": w_glu[0].astype(BF16), "bglu": b_glu[0].reshape(1, D_SSM), "wout": w_out[0].astype(BF16),
            "wr": w_router[0], "br": b_router[0].reshape(1, N_EXPERTS)}
    w_p = dict(base, wq=w_q.T, wvt=w_v.T)
    w_s = dict(base, wq=w_q, wvt=jnp.zeros((8, LANES), BF16))
    params = {"g_mix": g_mix[0].reshape(1, d), "g_ffn": g_ffn[0].reshape(1, d), "g_final": g_final.reshape(1, d),
              "w_up": w_up[0], "b_up": b_up[0].reshape(N_EXPERTS, 1, 2 * D_FF),
              "w_down": w_down[0], "b_down": b_down[0].reshape(N_EXPERTS, 1, d)}
    ssm_ops = _ssm_operators(ssm_log_dt[0], ssm_a_re[0], ssm_a_im[0], ssm_b_re[0], ssm_b_im[0],
                             ssm_c_re[0], ssm_c_im[0], ssm_d[0])
    zero_state = jnp.zeros((N_OCT, bp, OCT_STATE), F32)
    y_p, k_p, v_p, f_p, s_p = _stream(x_prompt, mod_p, w_p, ssm_ops, zero_state, None, params)
    s0 = _state_in(state_ssm_re[0], state_ssm_im[0])
    y_s, k_s, v_s, f_s, s_s = _stream(x_sample, mod_s, w_s, ssm_ops, s0,
                                      (cache_k[0], cache_v[0], cache_logf[0]), params)
    sre_p, sim_p = _state_out(s_p)
    sre_s, sim_s = _state_out(s_s)
    hd = (N_HEADS, HEAD_DIM)
    return (y_p, y_s,
            k_p.reshape(1, bp, tp, *hd), v_p.reshape(1, bp, tp, *hd), f_p.reshape(1, bp, tp, N_HEADS), sre_p, sim_p,
            k_s.reshape(1, bs, ts, *hd), v_s.reshape(1, bs, ts, *hd), f_s.reshape(1, bs, ts, N_HEADS), sre_s, sim_s)
```

```python
import functools
import math

import jax
import jax.numpy as jnp
from jax import lax
from jax.experimental import pallas as pl
from jax.experimental.pallas import tpu as pltpu

F32 = jnp.float32
BF16 = jnp.bfloat16
I32 = jnp.int32

D_MODEL = 1024
D_SSM = 512
SSM_GROUP = 16
N_SSM_GROUPS = 32
SSM_STATE = 64
D_ATT = 512
HEAD_DIM = 64
N_HEADS = 8
N_EXPERTS = 32
TOP_K = 4
D_FF = 1024
SWIGLU_LIMIT = 7.0
SWIGLU_ALPHA = 1.702
NORM_EPS = 1e-5

LANES = 128
N_OCT = D_SSM // LANES
OCT_STATE = 2 * (N_SSM_GROUPS // N_OCT) * SSM_STATE
SSM_CHUNK = 8
SSM_ROWS = 256
ATT_BLK = 256
ATT_PAIRS = 2
LOG2E = math.log2(math.e)
SAMPLE_KV_BLK = 512
MOE_BLK = 256
VMEM_LIMIT = 52 * 1024 * 1024


def _cparams(sem, vmem=VMEM_LIMIT):
    return pltpu.CompilerParams(dimension_semantics=sem, vmem_limit_bytes=vmem)


def _dot(a, b):
    return jnp.dot(a, b, preferred_element_type=F32)


def _dot_nt(a, b):
    return lax.dot_general(a, b, (((1,), (1,)), ((), ())), preferred_element_type=F32)


def _dot_hi(a, b):
    return jnp.dot(a, b, preferred_element_type=F32, precision=lax.Precision.HIGHEST)


def _ada_kernel(c_ref, w_ref, b_ref, o_ref):
    c = c_ref[...]
    s = c * jax.nn.sigmoid(c)
    o_ref[...] = _dot_hi(s, w_ref[...]) + b_ref[...]


def _ada(c_all, w_ada, b_ada):
    m, d = c_all.shape
    n = w_ada.shape[1]
    return pl.pallas_call(
        _ada_kernel,
        out_shape=jax.ShapeDtypeStruct((m, n), F32),
        grid=(n // d,),
        in_specs=[pl.BlockSpec((m, d), lambda j: (0, 0)),
                  pl.BlockSpec((d, d), lambda j: (0, j)),
                  pl.BlockSpec((1, d), lambda j: (0, j))],
        out_specs=pl.BlockSpec((m, d), lambda j: (0, j)),
        compiler_params=_cparams(("arbitrary",)),
        name="ada",
    )(c_all, w_ada, b_ada)


def _norm_mod(x, g, scale, shift):
    ms = jnp.mean(x * x, axis=-1, keepdims=True)
    y = x * lax.rsqrt(ms + NORM_EPS) * g
    return y * (1.0 + scale) + shift


def _log_sigmoid(z):
    return jnp.minimum(z, 0.0) - jnp.log1p(jnp.exp(-jnp.abs(z)))


def _inproj_kernel(x_ref, sh_ref, sc_ref, g_ref, wu_ref, wq_ref, wkt_ref, wvt_ref, wft_ref, bfc_ref,
                   wk_ref, wf_ref, bfr_ref,
                   u4_ref, q_ref, kt_ref, vt_ref, lft_ref, vtb_ref, ktok_ref, lftok_ref, *, prompt):
    nb, tt, d = x_ref.shape
    tm = nb * tt
    h = _norm_mod(x_ref[...], g_ref[...], sc_ref[...], sh_ref[...]).reshape(tm, d).astype(BF16)
    u = _dot(h, wu_ref[...])
    for o in range(N_OCT):
        u4_ref[o] = u[:, o * LANES:(o + 1) * LANES].astype(BF16)
    kt = _dot_nt(wkt_ref[...], h)
    vt = _dot_nt(wvt_ref[...], h)
    lft = _log_sigmoid(_dot_nt(wft_ref[...], h) + bfc_ref[...])
    scale = HEAD_DIM ** -0.5
    if prompt:
        kt_ref[...] = kt
        vt_ref[...] = vt
        lft_ref[...] = lft
        qt = (_dot_nt(wq_ref[...], h) * (scale * LOG2E)).astype(BF16)
        vtb = vt.astype(BF16)
        for j in range(tm // ATT_BLK):
            q_ref[j] = qt[:, j * ATT_BLK:(j + 1) * ATT_BLK]
            vtb_ref[j] = vtb[:, j * ATT_BLK:(j + 1) * ATT_BLK]
        ktok_ref[...] = _dot(h, wk_ref[...]).astype(BF16)
        lftok_ref[...] = _log_sigmoid(_dot(h, wf_ref[...]) + bfr_ref[...])
    else:
        for bl in range(nb):
            kt_ref[bl] = kt[:, bl * tt:(bl + 1) * tt]
            vt_ref[bl] = vt[:, bl * tt:(bl + 1) * tt]
            lft_ref[bl] = lft[:, bl * tt:(bl + 1) * tt]
        q_ref[...] = (_dot(h, wq_ref[...]) * scale).astype(BF16)
        vtb_ref[...] = jnp.zeros_like(vtb_ref)
        ktok_ref[...] = jnp.zeros_like(ktok_ref)
        lftok_ref[...] = jnp.zeros_like(lftok_ref)


def _inproj(x, shift, scale, g_mix, w, *, prompt, tm):
    b, t, d = x.shape
    n = b * t
    tm = min(tm, n)
    tt = min(t, tm)
    nb = tm // tt
    tpb = t // tt
    assert (nb == 1) == prompt
    grid = (n // tm,)

    def xmap(i):
        return (i // tpb, i % tpb, 0) if nb == 1 else (i, 0, 0)

    def bmap(i):
        return (i // tpb, 0, 0) if nb == 1 else (i, 0, 0)

    def tmap(i):
        return (i // tpb, 0, i % tpb) if nb == 1 else (i, 0, 0)

    full = lambda a: pl.BlockSpec(a.shape, lambda i: (0,) * a.ndim)
    names = ("wu", "wq", "wkt", "wvt", "wft", "bfc", "wk", "wf", "bfr")
    in_specs = [pl.BlockSpec((nb, tt, d), xmap),
                pl.BlockSpec((nb, 1, d), bmap), pl.BlockSpec((nb, 1, d), bmap),
                pl.BlockSpec((1, d), lambda i: (0, 0))] + [full(w[k]) for k in names]
    tblk = lambda rows: pl.BlockSpec((None if nb == 1 else nb, rows, tt), tmap)
    dummy = (jax.ShapeDtypeStruct((8, LANES), F32), pl.BlockSpec((8, LANES), lambda i: (0, 0)))
    if prompt:
        nblk = n // ATT_BLK
        blocked = (jax.ShapeDtypeStruct((nblk, D_ATT, ATT_BLK), BF16),
                   pl.BlockSpec((tm // ATT_BLK, D_ATT, ATT_BLK), lambda i: (i, 0, 0)))
        q_out, vtb_out = blocked, blocked
        ktok_out = (jax.ShapeDtypeStruct((n, D_ATT), BF16), pl.BlockSpec((tm, D_ATT), lambda i: (i, 0)))
        lftok_out = (jax.ShapeDtypeStruct((n, N_HEADS), F32), pl.BlockSpec((tm, N_HEADS), lambda i: (i, 0)))
    else:
        q_out = (jax.ShapeDtypeStruct((n, D_ATT), BF16), pl.BlockSpec((tm, D_ATT), lambda i: (i, 0)))
        vtb_out = ktok_out = lftok_out = dummy
    outs = [(jax.ShapeDtypeStruct((N_OCT, n, LANES), BF16), pl.BlockSpec((N_OCT, tm, LANES), lambda i: (0, i, 0))),
            q_out,
            (jax.ShapeDtypeStruct((b, D_ATT, t), F32), tblk(D_ATT)),
            (jax.ShapeDtypeStruct((b, D_ATT, t), F32), tblk(D_ATT)),
            (jax.ShapeDtypeStruct((b, N_HEADS, t), F32), tblk(N_HEADS)),
            vtb_out, ktok_out, lftok_out]
    return pl.pallas_call(
        functools.partial(_inproj_kernel, prompt=prompt),
        out_shape=tuple(o[0] for o in outs), grid=grid, in_specs=in_specs, out_specs=tuple(o[1] for o in outs),
        compiler_params=_cparams(("arbitrary",)),
        name="inproj_p" if prompt else "inproj_s",
    )(x, shift, scale, g_mix, *[w[k] for k in names])


def _ssm_operators(log_dt, a_re, a_im, b_re, b_im, c_re, c_im, d_skip):
    L, G, P, C = SSM_CHUNK, N_SSM_GROUPS, SSM_STATE, SSM_GROUP
    q = G // N_OCT
    lam = lax.complex(a_re, a_im)
    dt = jnp.exp(log_dt)[:, None]
    lam_dt = lam * dt
    lam_bar = jnp.exp(lam_dt)
    b_bar = ((lam_bar - 1.0) / lam)[:, :, None] * lax.complex(b_re, b_im)
    c_mat = lax.complex(c_re, c_im)
    pw = jnp.exp(lam_dt[None] * jnp.arange(L + 1, dtype=F32)[:, None, None])
    kern = jnp.real(jnp.einsum('gcp,dgp,gpk->dgck', c_mat, pw[:L], b_bar))
    ksm = jnp.transpose(kern, (1, 0, 3, 2)).reshape(N_OCT, q, L, C, C)
    ksm = jnp.transpose(ksm, (0, 2, 1, 3, 4)).reshape(N_OCT, L, q * C, C)
    pwr = pw[L - 1 - jnp.arange(L)]
    wb = jnp.einsum('jgp,gpk->jgkp', pwr, b_bar)
    wsm = jnp.stack([jnp.real(wb), jnp.imag(wb)], axis=0).reshape(2, L, N_OCT, q * C, P)
    wsm = jnp.transpose(wsm, (2, 1, 0, 3, 4))
    cl = c_mat[None] * pw[1:L + 1][:, :, None, :]
    vsm = jnp.stack([jnp.real(cl), -jnp.imag(cl)], axis=0).reshape(2, L, N_OCT, q, C, P)
    vsm = jnp.transpose(vsm, (2, 1, 0, 3, 5, 4)).reshape(N_OCT, L, OCT_STATE, C)
    tmat, wend, win = _ssm_prep(ksm, wsm, vsm, d_skip.reshape(N_OCT, 1, LANES))
    dec = pw[L].reshape(N_OCT, q * P)
    decay = jnp.concatenate([jnp.real(dec), jnp.imag(dec)], axis=-1).reshape(N_OCT, 1, OCT_STATE)
    return tmat, wend, win, decay.astype(F32)


def _ssm_prep_kernel(ksm_ref, wsm_ref, vsm_ref, d_ref, tmat_ref, wend_ref, win_ref):
    L, C, P = SSM_CHUNK, SSM_GROUP, SSM_STATE
    half = OCT_STATE // 2

    def spread(period, width):
        r = lax.broadcasted_iota(I32, (period, width), 0)
        c = lax.broadcasted_iota(I32, (period, width), 1)
        return jnp.where(c % period == r, 1.0, 0.0)

    def same_group(shape, row_div, row_mod, lane_div):
        r = lax.broadcasted_iota(I32, shape, 0)
        c = lax.broadcasted_iota(I32, shape, 1)
        return (r % row_mod) // row_div == c // lane_div

    e_c = spread(C, LANES)
    e_p = spread(P, half)
    grp = same_group((LANES, LANES), C, LANES, C)
    rr = lax.broadcasted_iota(I32, (LANES, LANES), 0)
    cc = lax.broadcasted_iota(I32, (LANES, LANES), 1)
    lag = []
    for d in range(L):
        blk = jnp.where(grp, _dot_hi(ksm_ref[d], e_c), 0.0)
        if d == 0:
            blk = blk + jnp.where(rr == cc, jnp.broadcast_to(d_ref[...], (LANES, LANES)), 0.0)
        lag.append(blk.astype(BF16))
    zero = jnp.zeros((LANES, LANES), BF16)
    for j in range(L):
        for t in range(L):
            tmat_ref[j * LANES:(j + 1) * LANES, t * LANES:(t + 1) * LANES] = lag[t - j] if t >= j else zero
    grp_e = same_group((LANES, half), C, LANES, P)
    for j in range(L):
        for part in range(2):
            blk = jnp.where(grp_e, _dot_hi(wsm_ref[j, part], e_p), 0.0)
            wend_ref[j * LANES:(j + 1) * LANES, part * half:(part + 1) * half] = blk.astype(BF16)
    grp_i = same_group((OCT_STATE, LANES), P, half, C)
    for t in range(L):
        blk = jnp.where(grp_i, _dot_hi(vsm_ref[t], e_c), 0.0)
        win_ref[:, t * LANES:(t + 1) * LANES] = blk.astype(BF16)


def _ssm_prep(ksm, wsm, vsm, dvec):
    kdim = SSM_CHUNK * LANES
    blk = lambda a: pl.BlockSpec((None,) + a.shape[1:], lambda o: (o,) + (0,) * (a.ndim - 1))
    out = lambda r, c: (jax.ShapeDtypeStruct((N_OCT, r, c), BF16), pl.BlockSpec((None, r, c), lambda o: (o, 0, 0)))
    outs = [out(kdim, kdim), out(kdim, OCT_STATE), out(OCT_STATE, kdim)]
    return pl.pallas_call(
        _ssm_prep_kernel,
        out_shape=tuple(o[0] for o in outs),
        grid=(N_OCT,),
        in_specs=[blk(ksm), blk(wsm), blk(vsm), blk(dvec)],
        out_specs=tuple(o[1] for o in outs),
        compiler_params=_cparams(("arbitrary",)),
        name="ssm_prep",
    )(ksm, wsm, vsm, dvec)


def _ssm_kernel(u_ref, tmat_ref, wend_ref, win_ref, dec_ref, s0_ref, y_ref, sfin_ref, e_ref, sp_ref, st_ref,
                *, nb, cpt):
    b = pl.program_id(1)
    i = pl.program_id(2)
    half = OCT_STATE // 2
    whole = nb == s0_ref.shape[0]

    @pl.when(i == 0)
    def _():
        st_ref[...] = s0_ref[...] if whole else s0_ref[pl.ds(b, 1), :]

    u = u_ref[...]
    e_ref[...] = _dot(u, wend_ref[...])
    dec = dec_ref[...]
    ar, ai = dec[:, :half], dec[:, half:]

    def per_batch(bb, carry):
        def per_chunk(r, st):
            row = bb * cpt + r
            sp_ref[pl.ds(row, 1), :] = st
            e = e_ref[pl.ds(row, 1), :]
            re, im = st[:, :half], st[:, half:]
            nre = ar * re - ai * im + e[:, :half]
            nim = ar * im + ai * re + e[:, half:]
            return jnp.concatenate([nre, nim], axis=-1)

        st = lax.fori_loop(0, cpt, per_chunk, st_ref[pl.ds(bb, 1), :])
        st_ref[pl.ds(bb, 1), :] = st
        return carry

    lax.fori_loop(0, nb, per_batch, 0)
    y = _dot(u, tmat_ref[...]) + _dot(sp_ref[...].astype(BF16), win_ref[...])
    y_ref[...] = jax.nn.gelu(y).astype(BF16)
    if whole:
        sfin_ref[...] = st_ref[...]
    else:
        sfin_ref[pl.ds(b, 1), :] = st_ref[...]


def _ssm(u4, ops, s0, batch, n_chunks):
    tmat, wend, win, decay = ops
    rows = batch * n_chunks
    kdim = SSM_CHUNK * LANES
    if n_chunks % SSM_ROWS == 0:
        nb, cpt = 1, SSM_ROWS
    else:
        nb, cpt = batch, n_chunks
    r = nb * cpt
    tiles = n_chunks // cpt
    grid = (N_OCT, batch // nb, tiles)
    wspec = lambda shape: pl.BlockSpec((None,) + shape, lambda o, b, i: (o, 0, 0))
    return pl.pallas_call(
        functools.partial(_ssm_kernel, nb=nb, cpt=cpt),
        out_shape=(jax.ShapeDtypeStruct((N_OCT, rows, kdim), BF16),
                   jax.ShapeDtypeStruct((N_OCT, batch, OCT_STATE), F32)),
        grid=grid,
        in_specs=[pl.BlockSpec((None, r, kdim), lambda o, b, i: (o, b * tiles + i, 0)),
                  wspec((kdim, kdim)), wspec((kdim, OCT_STATE)), wspec((OCT_STATE, kdim)), wspec((1, OCT_STATE)),
                  pl.BlockSpec((None, batch, OCT_STATE), lambda o, b, i: (o, 0, 0))],
        out_specs=(pl.BlockSpec((None, r, kdim), lambda o, b, i: (o, b * tiles + i, 0)),
                   pl.BlockSpec((None, batch, OCT_STATE), lambda o, b, i: (o, 0, 0))),
        scratch_shapes=[pltpu.VMEM((r, OCT_STATE), F32), pltpu.VMEM((r, OCT_STATE), F32),
                        pltpu.VMEM((nb, OCT_STATE), F32)],
        compiler_params=_cparams(("arbitrary", "arbitrary", "arbitrary")),
        name="ssm",
    )(u4, tmat, wend, win, decay, s0)


def _bias_placement():
    import numpy as np
    pm = np.zeros((3, N_HEADS, N_HEADS // 2 * LANES), np.float32)
    for piece in range(3):
        for h in range(N_HEADS):
            pm[piece, h, LANES * (h // 2) + 3 * (h % 2) + piece] = 1.0
    return jnp.asarray(pm)


def _fprep_p_kernel(lf_ref, k_ref, pm_ref, ka_ref, carry_ref):
    i = pl.program_id(1)

    @pl.when(i == 0)
    def _():
        carry_ref[...] = jnp.zeros_like(carry_ref)

    tile = lf_ref.shape[0]
    r = lax.broadcasted_iota(I32, (tile, tile), 0)
    c = lax.broadcasted_iota(I32, (tile, tile), 1)
    tri = (c <= r).astype(F32)
    cum = _dot_hi(tri, lf_ref[...]) + carry_ref[0:1, 0:N_HEADS]
    carry_ref[0:1, 0:N_HEADS] = cum[tile - 1:tile, :]
    nf = cum * (-LOG2E)
    hi = nf.astype(BF16).astype(F32)
    r1 = nf - hi
    mid = r1.astype(BF16).astype(F32)
    lo = (r1 - mid).astype(BF16).astype(F32)
    slab = _dot(hi, pm_ref[0]) + _dot(mid, pm_ref[1]) + _dot(lo, pm_ref[2])
    k = k_ref[...]
    for p in range(N_HEADS // 2):
        ka_ref[:, 2 * p * LANES:(2 * p + 1) * LANES] = k[:, p * LANES:(p + 1) * LANES].astype(BF16)
        ka_ref[:, (2 * p + 1) * LANES:(2 * p + 2) * LANES] = slab[:, p * LANES:(p + 1) * LANES].astype(BF16)


def _fprep_p(logf, k, batch, t):
    tile = ATT_BLK
    tiles = t // tile
    pm = _bias_placement()
    kaug = pl.pallas_call(
        _fprep_p_kernel,
        out_shape=jax.ShapeDtypeStruct((batch * t, 2 * D_ATT), BF16),
        grid=(batch, tiles),
        in_specs=[pl.BlockSpec((tile, N_HEADS), lambda b, i: (b * tiles + i, 0)),
                  pl.BlockSpec((tile, D_ATT), lambda b, i: (b * tiles + i, 0)),
                  pl.BlockSpec(pm.shape, lambda b, i: (0, 0, 0))],
        out_specs=pl.BlockSpec((tile, 2 * D_ATT), lambda b, i: (b * tiles + i, 0)),
        scratch_shapes=[pltpu.VMEM((8, LANES), F32)],
        compiler_params=_cparams(("arbitrary", "arbitrary")),
        name="fprep_p",
    )(logf, k, pm)
    return kaug.reshape(batch * tiles, tile, 2 * D_ATT)


def _attn_p_kernel(qt_ref, ka_ref, vt_ref, o_ref, qbd_ref, acc_ref, sa_ref, sb_ref):
    i = pl.program_id(2)
    bq = ATT_BLK
    hd = HEAD_DIM
    r = lax.broadcasted_iota(I32, (LANES, 2 * bq), 0)
    c = lax.broadcasted_iota(I32, (LANES, 2 * bq), 1)
    ones = jnp.where(((r < 3) & (c < bq)) | ((r >= 3) & (r < 6) & (c >= bq)), 1.0, 0.0).astype(BF16)
    zero = jnp.zeros((hd, bq), BF16)
    for pp in range(ATT_PAIRS):
        qb = qt_ref[pp * LANES:(pp + 1) * LANES, :]
        qbd_ref[pp, 0:hd, 0:bq] = qb[0:hd]
        qbd_ref[pp, 0:hd, bq:2 * bq] = zero
        qbd_ref[pp, hd:2 * hd, 0:bq] = zero
        qbd_ref[pp, hd:2 * hd, bq:2 * bq] = qb[hd:2 * hd]
        qbd_ref[pp, 2 * hd:2 * hd + LANES, :] = ones
    acc_ref[...] = jnp.zeros_like(acc_ref)

    nq = ka_ref.shape[0]

    def scores(s_ref, j):
        for pp in range(ATT_PAIRS):
            s_ref[pp] = _dot(ka_ref[j, :, pp * 2 * LANES:(pp + 1) * 2 * LANES], qbd_ref[pp])

    def attend(s_ref, j, carry, masked):
        out = []
        for pp in range(ATT_PAIRS):
            m_prev, l_prev = carry[2 * pp], carry[2 * pp + 1]
            s = s_ref[pp]
            if masked:
                key = j * bq + lax.broadcasted_iota(I32, s.shape, 0)
                qq = lax.broadcasted_iota(I32, s.shape, 1)
                qq = i * bq + jnp.where(qq >= bq, qq - bq, qq)
                s = jnp.where(key <= qq, s, -jnp.inf)
            m_new = jnp.maximum(m_prev, jnp.max(s, axis=0, keepdims=True))
            alpha = jnp.exp2(m_prev - m_new)
            p = jnp.exp2(s - m_new)
            l_new = alpha * l_prev + jnp.sum(p, axis=0, keepdims=True)
            pb = p.astype(BF16)
            vb = vt_ref[jnp.minimum(j, nq - 1), pp * LANES:(pp + 1) * LANES, :]
            r0 = pp * LANES
            acc_ref[r0:r0 + hd] = alpha[:, 0:bq] * acc_ref[r0:r0 + hd] + _dot(vb[0:hd], pb[:, 0:bq])
            acc_ref[r0 + hd:r0 + 2 * hd] = (alpha[:, bq:2 * bq] * acc_ref[r0 + hd:r0 + 2 * hd]
                                            + _dot(vb[hd:2 * hd], pb[:, bq:2 * bq]))
            out.extend((m_new, l_new))
        return tuple(out)

    def double_step(m, carry):
        j = 2 * m
        scores(sb_ref, j + 1)
        carry = attend(sa_ref, j, carry, False)
        scores(sa_ref, j + 2)
        return attend(sb_ref, j + 1, carry, False)

    init = (jnp.full((1, 2 * bq), -jnp.inf, F32), jnp.zeros((1, 2 * bq), F32)) * ATT_PAIRS
    scores(sa_ref, 0)
    carry = lax.fori_loop(0, i // 2, double_step, init)
    j = 2 * (i // 2)
    scores(sb_ref, jnp.minimum(j + 1, nq - 1))
    carry = attend(sa_ref, j, carry, True)
    carry = attend(sb_ref, j + 1, carry, True)
    for pp in range(ATT_PAIRS):
        l = carry[2 * pp + 1]
        r0 = pp * LANES
        o_ref[r0:r0 + hd] = acc_ref[r0:r0 + hd] / l[:, 0:bq]
        o_ref[r0 + hd:r0 + 2 * hd] = acc_ref[r0 + hd:r0 + 2 * hd] / l[:, bq:2 * bq]


def _attn_p(qt, kaug, vt, batch, t):
    blk = ATT_BLK
    nq = t // blk
    ngrp = N_HEADS // 2 // ATT_PAIRS
    rows = ATT_PAIRS * LANES
    qt4 = qt.reshape(batch, nq, D_ATT, blk)
    vt4 = vt.reshape(batch, nq, D_ATT, blk)
    ka4 = kaug.reshape(batch, nq, blk, 2 * D_ATT)
    return pl.pallas_call(
        _attn_p_kernel,
        out_shape=jax.ShapeDtypeStruct((batch, D_ATT, t), F32),
        grid=(batch, ngrp, nq),
        in_specs=[pl.BlockSpec((None, None, rows, blk), lambda b, p, i: (b, i, p, 0)),
                  pl.BlockSpec((None, nq, blk, 2 * rows), lambda b, p, i: (b, 0, 0, p)),
                  pl.BlockSpec((None, nq, rows, blk), lambda b, p, i: (b, 0, p, 0))],
        out_specs=pl.BlockSpec((None, rows, blk), lambda b, p, i: (b, p, i)),
        scratch_shapes=[pltpu.VMEM((ATT_PAIRS, 2 * LANES, 2 * blk), BF16), pltpu.VMEM((rows, blk), F32),
                        pltpu.VMEM((ATT_PAIRS, blk, 2 * blk), F32), pltpu.VMEM((ATT_PAIRS, blk, 2 * blk), F32)],
        compiler_params=_cparams(("arbitrary", "arbitrary", "arbitrary")),
        name="attn_p",
    )(qt4, ka4, vt4)


def _lane_cumsum(x):
    n = x.shape[-1]
    lane = lax.broadcasted_iota(I32, x.shape, x.ndim - 1)
    s = 1
    while s < n:
        x = x + jnp.where(lane >= s, pltpu.roll(x, s, x.ndim - 1), 0.0)
        s *= 2
    return x


def _fprep_s_kernel(cl_ref, ln_ref, fc_ref, fn_ref):
    b, h, p = cl_ref.shape
    cum = _lane_cumsum(cl_ref[...].reshape(b * h, p))
    fc_ref[...] = (-cum).reshape(b, h, p)
    total = cum[:, p - 1:p]
    cn = _lane_cumsum(ln_ref[...].reshape(b * h, LANES))
    fn_ref[...] = (-(total + cn)).reshape(b, h, LANES)


def _fprep_s(cache_lf_t, new_lf_t):
    b, h, p = cache_lf_t.shape
    return pl.pallas_call(
        _fprep_s_kernel,
        out_shape=(jax.ShapeDtypeStruct((b, h, p), F32), jax.ShapeDtypeStruct((b, h, LANES), F32)),
        grid=(1,),
        in_specs=[pl.BlockSpec((b, h, p), lambda i: (0, 0, 0)), pl.BlockSpec((b, h, LANES), lambda i: (0, 0, 0))],
        out_specs=(pl.BlockSpec((b, h, p), lambda i: (0, 0, 0)), pl.BlockSpec((b, h, LANES), lambda i: (0, 0, 0))),
        compiler_params=_cparams(("arbitrary",)),
        name="fprep_s",
    )(cache_lf_t, new_lf_t)


def _attn_s_kernel(q_ref, ck_ref, cv_ref, kn_ref, vn_ref, fc_ref, fn_ref, o_ref, qbd_ref, m_ref, l_ref, acc_ref,
                   *, tq):
    j = pl.program_id(1)
    nkv = pl.num_programs(1)
    rows = N_HEADS * tq
    bk = ck_ref.shape[1]

    @pl.when(j == 0)
    def _():
        q = q_ref[...]
        qrep = jnp.broadcast_to(q[None], (N_HEADS, tq, D_ATT)).reshape(rows, D_ATT)
        rh = lax.broadcasted_iota(I32, (rows, D_ATT), 0) // tq
        ch = lax.broadcasted_iota(I32, (rows, D_ATT), 1) // HEAD_DIM
        qbd_ref[...] = jnp.where(rh == ch, qrep, jnp.zeros_like(qrep))
        m_ref[...] = jnp.full_like(m_ref, -jnp.inf)
        l_ref[...] = jnp.zeros_like(l_ref)
        acc_ref[...] = jnp.zeros_like(acc_ref)

    def update(s, vt):
        m_prev = m_ref[...]
        m_new = jnp.maximum(m_prev, jnp.max(s, axis=1, keepdims=True))
        alpha = jnp.exp(m_prev - m_new)
        p = jnp.exp(s - m_new)
        l_ref[...] = alpha * l_ref[...] + jnp.sum(p, axis=1, keepdims=True)
        m_ref[...] = m_new
        acc_ref[...] = alpha * acc_ref[...] + _dot_nt(p.astype(BF16), vt)

    def bias(f, width):
        return jnp.broadcast_to(f[:, None, :], (N_HEADS, tq, width)).reshape(rows, width)

    s = _dot(qbd_ref[...], ck_ref[...].astype(BF16)) + bias(fc_ref[...], bk)
    update(s, cv_ref[...].astype(BF16))

    @pl.when(j == nkv - 1)
    def _():
        s2 = _dot(qbd_ref[...], kn_ref[...].astype(BF16)) + bias(fn_ref[...][:, 0:tq], tq)
        key = lax.broadcasted_iota(I32, (rows, tq), 1)
        qq = lax.broadcasted_iota(I32, (rows, tq), 0) % tq
        update(jnp.where(key <= qq, s2, -jnp.inf), vn_ref[...].astype(BF16))
        o = acc_ref[...] / l_ref[...]
        rh = lax.broadcasted_iota(I32, (rows, D_ATT), 0) // tq
        ch = lax.broadcasted_iota(I32, (rows, D_ATT), 1) // HEAD_DIM
        o = jnp.where(rh == ch, o, 0.0).reshape(N_HEADS, tq, D_ATT)
        o_ref[...] = jnp.sum(o, axis=0)


def _attn_s(q, cache_k, cache_v, k_new, v_new, fc, fn):
    b, tq, _ = q.shape
    p = cache_k.shape[2]
    bk = min(SAMPLE_KV_BLK, p)
    nkv = p // bk
    rows = N_HEADS * tq
    tok = lambda: pl.BlockSpec((None, tq, D_ATT), lambda bi, j: (bi, 0, 0))
    new = lambda: pl.BlockSpec((None, D_ATT, tq), lambda bi, j: (bi, 0, 0))
    return pl.pallas_call(
        functools.partial(_attn_s_kernel, tq=tq),
        out_shape=jax.ShapeDtypeStruct((b, tq, D_ATT), F32),
        grid=(b, nkv),
        in_specs=[tok(),
                  pl.BlockSpec((None, D_ATT, bk), lambda bi, j: (bi, 0, j)),
                  pl.BlockSpec((None, D_ATT, bk), lambda bi, j: (bi, 0, j)),
                  new(), new(),
                  pl.BlockSpec((None, N_HEADS, bk), lambda bi, j: (bi, 0, j)),
                  pl.BlockSpec((None, N_HEADS, LANES), lambda bi, j: (bi, 0, 0))],
        out_specs=tok(),
        scratch_shapes=[pltpu.VMEM((rows, D_ATT), BF16), pltpu.VMEM((rows, 1), F32),
                        pltpu.VMEM((rows, 1), F32), pltpu.VMEM((rows, D_ATT), F32)],
        compiler_params=_cparams(("arbitrary", "arbitrary")),
        name="attn_s",
    )(q, cache_k, cache_v, k_new, v_new, fc, fn)


def _outproj_kernel(x_ref, y4_ref, att_ref, gt_ref, sh_ref, sc_ref, gf_ref, wglu_ref, bglu_ref, wout_ref,
                    wr_ref, br_ref, cin_ref,
                    x1_ref, h2_ref, idx_ref, gate_ref, rank_ref, cnt_ref, carry_ref, *, att_transposed):
    i = pl.program_id(0)
    nb, tt, d = x_ref.shape
    tm = nb * tt

    @pl.when(i == 0)
    def _():
        carry_ref[...] = cin_ref[...]

    ys = jnp.concatenate([y4_ref[o] for o in range(N_OCT)], axis=-1)
    ysf = ys.astype(F32)
    glu = ysf * jax.nn.sigmoid(_dot(ys, wglu_ref[...]) + bglu_ref[...])
    att = att_ref[...]
    if att_transposed:
        att = att.T
    mix = _dot(glu.astype(BF16), wout_ref[0:D_SSM, :]) + _dot(att.astype(BF16), wout_ref[D_SSM:, :])
    x1 = x_ref[...] + gt_ref[...] * mix.reshape(nb, tt, d)
    x1_ref[...] = x1
    h2 = _norm_mod(x1, gf_ref[...], sc_ref[...], sh_ref[...]).reshape(tm, d)
    h2_ref[...] = h2

    logits = _dot_hi(h2, wr_ref[...]) + br_ref[...]
    lane = lax.broadcasted_iota(I32, logits.shape, 1)
    work = logits
    vals, idxs = [], []
    for _ in range(TOP_K):
        mx = jnp.max(work, axis=-1, keepdims=True)
        ix = jnp.min(jnp.where(work == mx, lane, N_EXPERTS), axis=-1, keepdims=True)
        vals.append(mx)
        idxs.append(ix)
        work = jnp.where(lane == ix, -jnp.inf, work)
    ex = [jnp.exp(v - vals[0]) for v in vals]
    den = ex[0] + ex[1] + ex[2] + ex[3]
    chosen = work == -jnp.inf
    mh = jnp.where(chosen, 1.0, 0.0)
    r = lax.broadcasted_iota(I32, (tm, tm), 0)
    c = lax.broadcasted_iota(I32, (tm, tm), 1)
    tri = jnp.where(c < r, 1.0, 0.0).astype(BF16)
    carry = carry_ref[0:1, 0:N_EXPERTS]
    before = _dot(tri, mh.astype(BF16)) + carry
    carry_ref[0:1, 0:N_EXPERTS] = carry + jnp.sum(mh, axis=0, keepdims=True)
    for kk in range(TOP_K):
        idx_ref[:, kk:kk + 1] = idxs[kk]
        gate_ref[:, kk:kk + 1] = ex[kk] / den
        rk = jnp.sum(jnp.where(lane == idxs[kk], before, 0.0), axis=-1, keepdims=True)
        rank_ref[:, kk:kk + 1] = rk.astype(I32)
    cnt_ref[...] = carry_ref[...]


def _outproj(x, y4, att, gt, sh, sc, g_ffn, w, counts_in, *, att_transposed, tm):
    b, t, d = x.shape
    n = b * t
    tm = min(tm, n)
    tt = min(t, tm)
    nb = tm // tt
    tpb = t // tt

    def xmap(i):
        return (i // tpb, i % tpb, 0) if nb == 1 else (i, 0, 0)

    def bmap(i):
        return (i // tpb, 0, 0) if nb == 1 else (i, 0, 0)

    full = lambda shape: pl.BlockSpec(shape, lambda i: (0,) * len(shape))
    if att_transposed:
        att_spec = pl.BlockSpec((None, D_ATT, tm), lambda i: (i // tpb, 0, i % tpb))
    else:
        att_spec = pl.BlockSpec((tm, D_ATT), lambda i: (i, 0))
    tok = lambda w_, dt: (jax.ShapeDtypeStruct((n, w_), dt), pl.BlockSpec((tm, w_), lambda i: (i, 0)))
    outs = [(jax.ShapeDtypeStruct((b, t, d), F32), pl.BlockSpec((nb, tt, d), xmap)),
            tok(d, F32), tok(TOP_K, I32), tok(TOP_K, F32), tok(TOP_K, I32),
            (jax.ShapeDtypeStruct((8, LANES), F32), full((8, LANES)))]
    return pl.pallas_call(
        functools.partial(_outproj_kernel, att_transposed=att_transposed),
        out_shape=tuple(o[0] for o in outs),
        grid=(n // tm,),
        in_specs=[pl.BlockSpec((nb, tt, d), xmap),
                  pl.BlockSpec((N_OCT, tm, LANES), lambda i: (0, i, 0)),
                  att_spec,
                  pl.BlockSpec((nb, 1, d), bmap), pl.BlockSpec((nb, 1, d), bmap), pl.BlockSpec((nb, 1, d), bmap),
                  full((1, d)), full((D_SSM, D_SSM)), full((1, D_SSM)), full((d, d)),
                  full((d, N_EXPERTS)), full((1, N_EXPERTS)), full((8, LANES))],
        out_specs=tuple(o[1] for o in outs),
        scratch_shapes=[pltpu.VMEM((8, LANES), F32)],
        compiler_params=_cparams(("arbitrary",)),
        name="outproj_t" if att_transposed else "outproj",
    )(x, y4, att, gt, sh, sc, g_ffn, w["wglu"], w["bglu"], w["wout"], w["wr"], w["br"], counts_in)


def _dispatch_kernel(zst_ref, nu_ref, dest_ref, h_ref, xs_ref, zero_ref, zsem, sem, *, tm):
    i = pl.program_id(0)
    n_blocks = xs_ref.shape[0] // MOE_BLK

    @pl.when(i == 0)
    def _():
        zero_ref[...] = jnp.zeros_like(zero_ref)

        def zero_copy(start):
            start = pl.multiple_of(start, MOE_BLK)
            return pltpu.make_async_copy(zero_ref, xs_ref.at[pl.ds(start, MOE_BLK)], zsem)

        for e in range(N_EXPERTS):
            @pl.when(zst_ref[e] >= 0)
            def _():
                zero_copy(jnp.maximum(zst_ref[e], 0)).start()

        def tail_start(j, carry):
            zero_copy(j * MOE_BLK).start()
            return carry

        def tail_wait(j, carry):
            zero_copy(j * MOE_BLK).wait()
            return carry

        lax.fori_loop(nu_ref[0], n_blocks, tail_start, 0)
        for e in range(N_EXPERTS):
            @pl.when(zst_ref[e] >= 0)
            def _():
                zero_copy(jnp.maximum(zst_ref[e], 0)).wait()
        lax.fori_loop(nu_ref[0], n_blocks, tail_wait, 0)

    def issue(t, carry):
        for kk in range(TOP_K):
            dst = dest_ref[0, 0, t * TOP_K + kk]
            pltpu.make_async_copy(h_ref.at[pl.ds(t, 1)], xs_ref.at[pl.ds(dst, 1)], sem).start()
        return carry

    lax.fori_loop(0, tm, issue, 0, unroll=8)
    n_copied = tm * TOP_K
    pltpu.make_async_copy(xs_ref.at[pl.ds(0, n_copied)], xs_ref.at[pl.ds(0, n_copied)], sem).wait()


def _dispatch(h2, dest, zstart, n_used, n_rows, tm):
    n, d = h2.shape
    nt = n // tm
    dest3 = dest.reshape(nt, 1, tm * TOP_K)
    return pl.pallas_call(
        functools.partial(_dispatch_kernel, tm=tm),
        out_shape=jax.ShapeDtypeStruct((n_rows, d), F32),
        grid_spec=pltpu.PrefetchScalarGridSpec(
            num_scalar_prefetch=2, grid=(nt,),
            in_specs=[pl.BlockSpec((1, 1, tm * TOP_K), lambda i, *_: (i, 0, 0), memory_space=pltpu.SMEM),
                      pl.BlockSpec((tm, d), lambda i, *_: (i, 0))],
            out_specs=pl.BlockSpec(memory_space=pl.ANY),
            scratch_shapes=[pltpu.VMEM((MOE_BLK, d), F32), pltpu.SemaphoreType.DMA, pltpu.SemaphoreType.DMA]),
        compiler_params=_cparams(("arbitrary",)),
        name="moe_dispatch",
    )(zstart, n_used, dest3, h2)


def _expert_kernel(be_ref, nu_ref, xs_ref, wup_ref, bup_ref, wdn_ref, bdn_ref, ys_ref, wup_bf, wdn_bf):
    i = pl.program_id(0)
    e = be_ref[i]
    prev = be_ref[jnp.maximum(i - 1, 0)]

    @pl.when(i < nu_ref[0])
    def _():
        @pl.when((i == 0) | (prev != e))
        def _():
            wup_bf[...] = wup_ref[...].astype(BF16)
            wdn_bf[...] = wdn_ref[...].astype(BF16)

        up = _dot(xs_ref[...].astype(BF16), wup_bf[...]) + bup_ref[...]
        x_glu = jnp.minimum(up[:, :D_FF], SWIGLU_LIMIT)
        x_lin = jnp.clip(up[:, D_FF:], -SWIGLU_LIMIT, SWIGLU_LIMIT)
        act = (x_lin + 1.0) * (x_glu * jax.nn.sigmoid(SWIGLU_ALPHA * x_glu))
        ys_ref[...] = _dot(act.astype(BF16), wdn_bf[...]) + bdn_ref[...]

    @pl.when(i >= nu_ref[0])
    def _():
        ys_ref[...] = jnp.zeros_like(ys_ref)


def _experts(xs, block_e, n_used, w_up, b_up, w_down, b_down):
    n_rows, d = xs.shape
    nblk = n_rows // MOE_BLK
    rowmap = lambda i, be, nu: (jnp.minimum(i, nu[0] - 1), 0)
    emap = lambda i, be, nu: (be[i], 0, 0)
    return pl.pallas_call(
        _expert_kernel,
        out_shape=jax.ShapeDtypeStruct((n_rows, d), F32),
        grid_spec=pltpu.PrefetchScalarGridSpec(
            num_scalar_prefetch=2, grid=(nblk,),
            in_specs=[pl.BlockSpec((MOE_BLK, d), rowmap),
                      pl.BlockSpec((None, d, 2 * D_FF), emap), pl.BlockSpec((None, 1, 2 * D_FF), emap),
                      pl.BlockSpec((None, D_FF, d), emap), pl.BlockSpec((None, 1, d), emap)],
            out_specs=pl.BlockSpec((MOE_BLK, d), lambda i, be, nu: (i, 0)),
            scratch_shapes=[pltpu.VMEM((d, 2 * D_FF), BF16), pltpu.VMEM((D_FF, d), BF16)]),
        compiler_params=_cparams(("arbitrary",)),
        name="moe_experts",
    )(block_e, n_used, xs, w_up, b_up, w_down, b_down)


def _combine_kernel(dest_ref, x1_ref, gate_ref, gt_ref, gfin_ref, ys_ref, y_ref, buf_ref, sem, *, tm):
    nb, tt, d = x1_ref.shape

    def issue(t, carry):
        for kk in range(TOP_K):
            src = dest_ref[0, 0, t * TOP_K + kk]
            pltpu.make_async_copy(ys_ref.at[pl.ds(src, 1)], buf_ref.at[pl.ds(kk * tm + t, 1)], sem).start()
        return carry

    lax.fori_loop(0, tm, issue, 0, unroll=8)
    pltpu.make_async_copy(ys_ref.at[pl.ds(0, TOP_K * tm)], buf_ref, sem).wait()
    gate = gate_ref[...]
    moe = gate[:, 0:1] * buf_ref[0:tm]
    for kk in range(1, TOP_K):
        moe = moe + gate[:, kk:kk + 1] * buf_ref[kk * tm:(kk + 1) * tm]
    x2 = x1_ref[...] + gt_ref[...] * moe.reshape(nb, tt, d)
    ms = jnp.mean(x2 * x2, axis=-1, keepdims=True)
    y_ref[...] = x2 * lax.rsqrt(ms + NORM_EPS) * gfin_ref[...]


def _combine(x1, gate, dest, ys, gt, g_final, tm):
    b, t, d = x1.shape
    n = b * t
    tm = min(tm, n)
    tt = min(t, tm)
    nb = tm // tt
    tpb = t // tt
    nt = n // tm
    dest3 = dest.reshape(nt, 1, tm * TOP_K)

    def xmap(i, *_):
        return (i // tpb, i % tpb, 0) if nb == 1 else (i, 0, 0)

    def bmap(i, *_):
        return (i // tpb, 0, 0) if nb == 1 else (i, 0, 0)

    return pl.pallas_call(
        functools.partial(_combine_kernel, tm=tm),
        out_shape=jax.ShapeDtypeStruct((b, t, d), F32),
        grid=(nt,),
        in_specs=[pl.BlockSpec((1, 1, tm * TOP_K), lambda i: (i, 0, 0), memory_space=pltpu.SMEM),
                  pl.BlockSpec((nb, tt, d), xmap),
                  pl.BlockSpec((tm, TOP_K), lambda i: (i, 0)),
                  pl.BlockSpec((nb, 1, d), bmap),
                  pl.BlockSpec((1, d), lambda i: (0, 0)),
                  pl.BlockSpec(memory_space=pl.ANY)],
        out_specs=pl.BlockSpec((nb, tt, d), xmap),
        scratch_shapes=[pltpu.VMEM((TOP_K * tm, d), F32), pltpu.SemaphoreType.DMA],
        compiler_params=_cparams(("arbitrary",)),
        name="moe_combine",
    )(dest3, x1, gate, gt, g_final, ys)


def _moe(x1, h2, idx, gate, rank, counts, gt_ffn, g_final, w_up, b_up, w_down, b_down):
    n, d = h2.shape
    cnt = counts[0, :N_EXPERTS].astype(I32)
    padded = (cnt + MOE_BLK - 1) // MOE_BLK * MOE_BLK
    pad_end = jnp.cumsum(padded)
    off = pad_end - padded
    n_blocks = -(-n * TOP_K // MOE_BLK) + N_EXPERTS
    n_rows = n_blocks * MOE_BLK
    starts = jnp.arange(n_blocks, dtype=I32) * MOE_BLK
    block_e = jnp.minimum(jnp.sum((pad_end[None, :] <= starts[:, None]).astype(I32), axis=1), N_EXPERTS - 1)
    n_used = (pad_end[-1:] // MOE_BLK).astype(I32)
    zstart = jnp.where(padded > 0, pad_end - MOE_BLK, -1).astype(I32)
    dest = (jnp.sum(jnp.where(idx[:, :, None] == jnp.arange(N_EXPERTS, dtype=I32), off.astype(I32), 0), axis=-1)
            + rank).astype(I32)
    tm = min(256, n)
    xs = _dispatch(h2, dest, zstart, n_used, n_rows, tm)
    ys = _experts(xs, block_e, n_used, w_up, b_up, w_down, b_down)
    return _combine(x1, gate, dest, ys, gt_ffn, g_final, min(128, n))


def _stream(x, mod, w, ssm_ops, s0, cache, params):
    b, t, d = x.shape
    n = b * t
    sh_mix, sc_mix, gt_mix, sh_ffn, sc_ffn, gt_ffn = mod
    prompt = cache is None
    tm = 512
    u4, q, kt, vt, lft, vtb, k_tok, lf_tok = _inproj(x, sh_mix, sc_mix, params["g_mix"], w, prompt=prompt, tm=tm)
    n_chunks = t // SSM_CHUNK
    y4, s_fin = _ssm(u4.reshape(N_OCT, b * n_chunks, SSM_CHUNK * LANES), ssm_ops, s0, b, n_chunks)
    y4 = y4.reshape(N_OCT, n, LANES)
    if prompt:
        kaug = _fprep_p(lf_tok, k_tok, b, t)
        att = _attn_p(q, kaug, vtb, b, t)
    else:
        cache_k, cache_v, cache_lf = cache
        p = cache_k.shape[1]
        ck_t = jnp.transpose(cache_k, (0, 2, 3, 1)).reshape(b, D_ATT, p)
        cv_t = jnp.transpose(cache_v, (0, 2, 3, 1)).reshape(b, D_ATT, p)
        cl_t = jnp.transpose(cache_lf, (0, 2, 1))
        ln_t = jnp.pad(lft, ((0, 0), (0, 0), (0, LANES - t)))
        fc, fn = _fprep_s(cl_t, ln_t)
        att = _attn_s(q.reshape(b, t, D_ATT), ck_t, cv_t, kt, vt, fc, fn).reshape(n, D_ATT)
    counts0 = jnp.zeros((8, LANES), F32)
    x1, h2, idx, gate, rank, counts = _outproj(x, y4, att, gt_mix, sh_ffn, sc_ffn, params["g_ffn"], w, counts0,
                                               att_transposed=prompt, tm=256)
    y = _moe(x1, h2, idx, gate, rank, counts, gt_ffn, params["g_final"],
             params["w_up"], params["b_up"], params["w_down"], params["b_down"])
    heads = lambda a: jnp.transpose(a.reshape(b, N_HEADS, HEAD_DIM, t), (0, 3, 1, 2))[None]
    return y, heads(kt), heads(vt), jnp.transpose(lft, (0, 2, 1))[None], s_fin


def _state_in(re, im):
    b = re.shape[0]
    s = jnp.concatenate([re.reshape(b, N_OCT, OCT_STATE // 2), im.reshape(b, N_OCT, OCT_STATE // 2)], axis=-1)
    return jnp.transpose(s, (1, 0, 2))


def _state_out(s):
    b = s.shape[1]
    s = jnp.transpose(s, (1, 0, 2))
    half = OCT_STATE // 2
    re = s[:, :, :half].reshape(1, b, N_SSM_GROUPS, SSM_STATE)
    im = s[:, :, half:].reshape(1, b, N_SSM_GROUPS, SSM_STATE)
    return re, im


def kernel(x_prompt, x_sample, c_prompt, c_sample, cache_k, cache_v, cache_logf, state_ssm_re, state_ssm_im, w_ada, b_ada, g_mix, w_in, b_forget, ssm_log_dt, ssm_a_re, ssm_a_im, ssm_b_re, ssm_b_im, ssm_c_re, ssm_c_im, ssm_d, w_glu, b_glu, w_out, g_ffn, w_router, b_router, w_up, b_up, w_down, b_down, g_final):
    assert w_ada.shape[0] == 1, "single-layer trunk"
    d = D_MODEL
    bp, tp, _ = x_prompt.shape
    bs, ts, _ = x_sample.shape
    n_c = bp + bs
    c_all = jnp.pad(jnp.concatenate([c_prompt, c_sample], axis=0), ((0, -n_c % 8), (0, 0)))
    mod = _ada(c_all, w_ada[0], b_ada[0].reshape(1, 6 * d))
    mod_p = [mod[:bp, j * d:(j + 1) * d].reshape(bp, 1, d) for j in range(6)]
    mod_s = [mod[bp:n_c, j * d:(j + 1) * d].reshape(bs, 1, d) for j in range(6)]
    wi = w_in[0]
    o1, o2, o3, o4 = D_SSM, D_SSM + D_ATT, D_SSM + 2 * D_ATT, D_SSM + 3 * D_ATT
    w_q = wi[:, o1:o2].astype(BF16)
    w_k = wi[:, o2:o3].astype(BF16)
    w_f = wi[:, o4:].astype(BF16)
    unused = jnp.zeros((8, LANES), BF16)
    base = {"wu": wi[:, :o1].astype(BF16), "wkt": w_k.T, "wvt": wi[:, o3:o4].astype(BF16).T, "wft": w_f.T,
            "bfc": b_forget[0].reshape(N_HEADS, 1), "bfr": b_forget[0].reshape(1, N_HEADS),
            "wglu": w_glu[0].astype(BF16), "bglu": b_glu[0].reshape(1, D_SSM), "wout": w_out[0].astype(BF16),
            "wr": w_router[0], "br": b_router[0].reshape(1, N_EXPERTS)}
    w_p = dict(base, wq=w_q.T, wk=w_k, wf=w_f)
    w_s = dict(base, wq=w_q, wk=unused, wf=unused)
    params = {"g_mix": g_mix[0].reshape(1, d), "g_ffn": g_ffn[0].reshape(1, d), "g_final": g_final.reshape(1, d),
              "w_up": w_up[0], "b_up": b_up[0].reshape(N_EXPERTS, 1, 2 * D_FF),
              "w_down": w_down[0], "b_down": b_down[0].reshape(N_EXPERTS, 1, d)}
    ssm_ops = _ssm_operators(ssm_log_dt[0], ssm_a_re[0], ssm_a_im[0], ssm_b_re[0], ssm_b_im[0],
                             ssm_c_re[0], ssm_c_im[0], ssm_d[0])
    zero_state = jnp.zeros((N_OCT, bp, OCT_STATE), F32)
    y_p, k_p, v_p, f_p, s_p = _stream(x_prompt, mod_p, w_p, ssm_ops, zero_state, None, params)
    s0 = _state_in(state_ssm_re[0], state_ssm_im[0])
    y_s, k_s, v_s, f_s, s_s = _stream(x_sample, mod_s, w_s, ssm_ops, s0,
                                      (cache_k[0], cache_v[0], cache_logf[0]), params)
    sre_p, sim_p = _state_out(s_p)
    sre_s, sim_s = _state_out(s_s)
    return (y_p, y_s, k_p, v_p, f_p, sre_p, sim_p, k_s, v_s, f_s, sre_s, sim_s)
```

```python
import functools
import math

import jax
import jax.numpy as jnp
from jax import lax
from jax.experimental import pallas as pl
from jax.experimental.pallas import tpu as pltpu

F32 = jnp.float32
BF16 = jnp.bfloat16
I32 = jnp.int32

D_MODEL = 1024
D_SSM = 512
SSM_GROUP = 16
N_SSM_GROUPS = 32
SSM_STATE = 64
D_ATT = 512
HEAD_DIM = 64
N_HEADS = 8
N_EXPERTS = 32
TOP_K = 4
D_FF = 1024
SWIGLU_LIMIT = 7.0
SWIGLU_ALPHA = 1.702
NORM_EPS = 1e-5

LANES = 128
N_OCT = D_SSM // LANES
OCT_STATE = 2 * (N_SSM_GROUPS // N_OCT) * SSM_STATE
SSM_CHUNK = 8
SSM_ROWS = 256
ATT_BLK = 256
ATT_PAIRS = 2
LOG2E = math.log2(math.e)
SAMPLE_KV_BLK = 512
MOE_BLK = 256
VMEM_LIMIT = 52 * 1024 * 1024


def _cparams(sem, vmem=VMEM_LIMIT):
    return pltpu.CompilerParams(dimension_semantics=sem, vmem_limit_bytes=vmem)


def _dot(a, b):
    return jnp.dot(a, b, preferred_element_type=F32)


def _dot_nt(a, b):
    return lax.dot_general(a, b, (((1,), (1,)), ((), ())), preferred_element_type=F32)


def _dot_hi(a, b):
    return jnp.dot(a, b, preferred_element_type=F32, precision=lax.Precision.HIGHEST)


def _ada_kernel(c_ref, w_ref, b_ref, o_ref):
    c = c_ref[...]
    s = c * jax.nn.sigmoid(c)
    o_ref[...] = _dot_hi(s, w_ref[...]) + b_ref[...]


def _ada(c_all, w_ada, b_ada):
    m, d = c_all.shape
    n = w_ada.shape[1]
    return pl.pallas_call(
        _ada_kernel,
        out_shape=jax.ShapeDtypeStruct((m, n), F32),
        grid=(n // d,),
        in_specs=[pl.BlockSpec((m, d), lambda j: (0, 0)),
                  pl.BlockSpec((d, d), lambda j: (0, j)),
                  pl.BlockSpec((1, d), lambda j: (0, j))],
        out_specs=pl.BlockSpec((m, d), lambda j: (0, j)),
        compiler_params=_cparams(("arbitrary",)),
        name="ada",
    )(c_all, w_ada, b_ada)


def _norm_mod(x, g, scale, shift):
    ms = jnp.mean(x * x, axis=-1, keepdims=True)
    y = x * lax.rsqrt(ms + NORM_EPS) * g
    return y * (1.0 + scale) + shift


def _log_sigmoid(z):
    return jnp.minimum(z, 0.0) - jnp.log1p(jnp.exp(-jnp.abs(z)))


def _inproj_kernel(x_ref, sh_ref, sc_ref, g_ref, wu_ref, wq_ref, wkt_ref, wvt_ref, wft_ref, bfc_ref,
                   wk_ref, wf_ref, bfr_ref,
                   u4_ref, q_ref, kt_ref, vt_ref, lft_ref, vtb_ref, ktok_ref, lftok_ref, *, prompt):
    nb, tt, d = x_ref.shape
    tm = nb * tt
    h = _norm_mod(x_ref[...], g_ref[...], sc_ref[...], sh_ref[...]).reshape(tm, d).astype(BF16)
    u = _dot(h, wu_ref[...])
    for o in range(N_OCT):
        u4_ref[o] = u[:, o * LANES:(o + 1) * LANES].astype(BF16)
    kt = _dot_nt(wkt_ref[...], h)
    vt = _dot_nt(wvt_ref[...], h)
    lft = _log_sigmoid(_dot_nt(wft_ref[...], h) + bfc_ref[...])
    scale = HEAD_DIM ** -0.5
    if prompt:
        kt_ref[...] = kt
        vt_ref[...] = vt
        lft_ref[...] = lft
        qt = (_dot_nt(wq_ref[...], h) * (scale * LOG2E)).astype(BF16)
        vtb = vt.astype(BF16)
        for j in range(tm // ATT_BLK):
            q_ref[j] = qt[:, j * ATT_BLK:(j + 1) * ATT_BLK]
            vtb_ref[j] = vtb[:, j * ATT_BLK:(j + 1) * ATT_BLK]
        ktok_ref[...] = _dot(h, wk_ref[...]).astype(BF16)
        lftok_ref[...] = _log_sigmoid(_dot(h, wf_ref[...]) + bfr_ref[...])
    else:
        for bl in range(nb):
            kt_ref[bl] = kt[:, bl * tt:(bl + 1) * tt]
            vt_ref[bl] = vt[:, bl * tt:(bl + 1) * tt]
            lft_ref[bl] = lft[:, bl * tt:(bl + 1) * tt]
        q_ref[...] = (_dot(h, wq_ref[...]) * scale).astype(BF16)
        vtb_ref[...] = jnp.zeros_like(vtb_ref)
        ktok_ref[...] = jnp.zeros_like(ktok_ref)
        lftok_ref[...] = jnp.zeros_like(lftok_ref)


def _inproj(x, shift, scale, g_mix, w, *, prompt, tm):
    b, t, d = x.shape
    n = b * t
    tm = min(tm, n)
    tt = min(t, tm)
    nb = tm // tt
    tpb = t // tt
    assert (nb == 1) == prompt
    grid = (n // tm,)

    def xmap(i):
        return (i // tpb, i % tpb, 0) if nb == 1 else (i, 0, 0)

    def bmap(i):
        return (i // tpb, 0, 0) if nb == 1 else (i, 0, 0)

    def tmap(i):
        return (i // tpb, 0, i % tpb) if nb == 1 else (i, 0, 0)

    full = lambda a: pl.BlockSpec(a.shape, lambda i: (0,) * a.ndim)
    names = ("wu", "wq", "wkt", "wvt", "wft", "bfc", "wk", "wf", "bfr")
    in_specs = [pl.BlockSpec((nb, tt, d), xmap),
                pl.BlockSpec((nb, 1, d), bmap), pl.BlockSpec((nb, 1, d), bmap),
                pl.BlockSpec((1, d), lambda i: (0, 0))] + [full(w[k]) for k in names]
    tblk = lambda rows: pl.BlockSpec((None if nb == 1 else nb, rows, tt), tmap)
    dummy = (jax.ShapeDtypeStruct((8, LANES), F32), pl.BlockSpec((8, LANES), lambda i: (0, 0)))
    if prompt:
        nblk = n // ATT_BLK
        blocked = (jax.ShapeDtypeStruct((nblk, D_ATT, ATT_BLK), BF16),
                   pl.BlockSpec((tm // ATT_BLK, D_ATT, ATT_BLK), lambda i: (i, 0, 0)))
        q_out, vtb_out = blocked, blocked
        ktok_out = (jax.ShapeDtypeStruct((n, D_ATT), BF16), pl.BlockSpec((tm, D_ATT), lambda i: (i, 0)))
        lftok_out = (jax.ShapeDtypeStruct((n, N_HEADS), F32), pl.BlockSpec((tm, N_HEADS), lambda i: (i, 0)))
    else:
        q_out = (jax.ShapeDtypeStruct((n, D_ATT), BF16), pl.BlockSpec((tm, D_ATT), lambda i: (i, 0)))
        vtb_out = ktok_out = lftok_out = dummy
    outs = [(jax.ShapeDtypeStruct((N_OCT, n, LANES), BF16), pl.BlockSpec((N_OCT, tm, LANES), lambda i: (0, i, 0))),
            q_out,
            (jax.ShapeDtypeStruct((b, D_ATT, t), F32), tblk(D_ATT)),
            (jax.ShapeDtypeStruct((b, D_ATT, t), F32), tblk(D_ATT)),
            (jax.ShapeDtypeStruct((b, N_HEADS, t), F32), tblk(N_HEADS)),
            vtb_out, ktok_out, lftok_out]
    return pl.pallas_call(
        functools.partial(_inproj_kernel, prompt=prompt),
        out_shape=tuple(o[0] for o in outs), grid=grid, in_specs=in_specs, out_specs=tuple(o[1] for o in outs),
        compiler_params=_cparams(("arbitrary",)),
        name="inproj_p" if prompt else "inproj_s",
    )(x, shift, scale, g_mix, *[w[k] for k in names])


def _ssm_operators(log_dt, a_re, a_im, b_re, b_im, c_re, c_im, d_skip):
    L, G, P, C = SSM_CHUNK, N_SSM_GROUPS, SSM_STATE, SSM_GROUP
    q = G // N_OCT
    lam = lax.complex(a_re, a_im)
    dt = jnp.exp(log_dt)[:, None]
    lam_dt = lam * dt
    lam_bar = jnp.exp(lam_dt)
    b_bar = ((lam_bar - 1.0) / lam)[:, :, None] * lax.complex(b_re, b_im)
    c_mat = lax.complex(c_re, c_im)
    pw = jnp.exp(lam_dt[None] * jnp.arange(L + 1, dtype=F32)[:, None, None])
    kern = jnp.real(jnp.einsum('gcp,dgp,gpk->dgck', c_mat, pw[:L], b_bar))
    ksm = jnp.transpose(kern, (1, 0, 3, 2)).reshape(N_OCT, q, L, C, C)
    ksm = jnp.transpose(ksm, (0, 2, 1, 3, 4)).reshape(N_OCT, L, q * C, C)
    pwr = pw[L - 1 - jnp.arange(L)]
    wb = jnp.einsum('jgp,gpk->jgkp', pwr, b_bar)
    wsm = jnp.stack([jnp.real(wb), jnp.imag(wb)], axis=0).reshape(2, L, N_OCT, q * C, P)
    wsm = jnp.transpose(wsm, (2, 1, 0, 3, 4))
    cl = c_mat[None] * pw[1:L + 1][:, :, None, :]
    vsm = jnp.stack([jnp.real(cl), -jnp.imag(cl)], axis=0).reshape(2, L, N_OCT, q, C, P)
    vsm = jnp.transpose(vsm, (2, 1, 0, 3, 5, 4)).reshape(N_OCT, L, OCT_STATE, C)
    tmat, wend, win = _ssm_prep(ksm, wsm, vsm, d_skip.reshape(N_OCT, 1, LANES))
    dec = pw[L].reshape(N_OCT, q * P)
    decay = jnp.concatenate([jnp.real(dec), jnp.imag(dec)], axis=-1).reshape(N_OCT, 1, OCT_STATE)
    return tmat, wend, win, decay.astype(F32)


def _ssm_prep_kernel(ksm_ref, wsm_ref, vsm_ref, d_ref, tmat_ref, wend_ref, win_ref):
    L, C, P = SSM_CHUNK, SSM_GROUP, SSM_STATE
    half = OCT_STATE // 2

    def spread(period, width):
        r = lax.broadcasted_iota(I32, (period, width), 0)
        c = lax.broadcasted_iota(I32, (period, width), 1)
        return jnp.where(c % period == r, 1.0, 0.0)

    def same_group(shape, row_div, row_mod, lane_div):
        r = lax.broadcasted_iota(I32, shape, 0)
        c = lax.broadcasted_iota(I32, shape, 1)
        return (r % row_mod) // row_div == c // lane_div

    e_c = spread(C, LANES)
    e_p = spread(P, half)
    grp = same_group((LANES, LANES), C, LANES, C)
    rr = lax.broadcasted_iota(I32, (LANES, LANES), 0)
    cc = lax.broadcasted_iota(I32, (LANES, LANES), 1)
    lag = []
    for d in range(L):
        blk = jnp.where(grp, _dot_hi(ksm_ref[d], e_c), 0.0)
        if d == 0:
            blk = blk + jnp.where(rr == cc, jnp.broadcast_to(d_ref[...], (LANES, LANES)), 0.0)
        lag.append(blk.astype(BF16))
    zero = jnp.zeros((LANES, LANES), BF16)
    for j in range(L):
        for t in range(L):
            tmat_ref[j * LANES:(j + 1) * LANES, t * LANES:(t + 1) * LANES] = lag[t - j] if t >= j else zero
    grp_e = same_group((LANES, half), C, LANES, P)
    for j in range(L):
        for part in range(2):
            blk = jnp.where(grp_e, _dot_hi(wsm_ref[j, part], e_p), 0.0)
            wend_ref[j * LANES:(j + 1) * LANES, part * half:(part + 1) * half] = blk.astype(BF16)
    grp_i = same_group((OCT_STATE, LANES), P, half, C)
    for t in range(L):
        blk = jnp.where(grp_i, _dot_hi(vsm_ref[t], e_c), 0.0)
        win_ref[:, t * LANES:(t + 1) * LANES] = blk.astype(BF16)


def _ssm_prep(ksm, wsm, vsm, dvec):
    kdim = SSM_CHUNK * LANES
    blk = lambda a: pl.BlockSpec((None,) + a.shape[1:], lambda o: (o,) + (0,) * (a.ndim - 1))
    out = lambda r, c: (jax.ShapeDtypeStruct((N_OCT, r, c), BF16), pl.BlockSpec((None, r, c), lambda o: (o, 0, 0)))
    outs = [out(kdim, kdim), out(kdim, OCT_STATE), out(OCT_STATE, kdim)]
    return pl.pallas_call(
        _ssm_prep_kernel,
        out_shape=tuple(o[0] for o in outs),
        grid=(N_OCT,),
        in_specs=[blk(ksm), blk(wsm), blk(vsm), blk(dvec)],
        out_specs=tuple(o[1] for o in outs),
        compiler_params=_cparams(("arbitrary",)),
        name="ssm_prep",
    )(ksm, wsm, vsm, dvec)


def _ssm_kernel(u_ref, tmat_ref, wend_ref, win_ref, dec_ref, s0_ref, y_ref, sfin_ref, e_ref, sp_ref, st_ref,
                *, nb, cpt):
    b = pl.program_id(1)
    i = pl.program_id(2)
    half = OCT_STATE // 2
    whole = nb == s0_ref.shape[0]

    @pl.when(i == 0)
    def _():
        st_ref[...] = s0_ref[...] if whole else s0_ref[pl.ds(b, 1), :]

    u = u_ref[...]
    e_ref[...] = _dot(u, wend_ref[...])
    dec = dec_ref[...]
    ar, ai = dec[:, :half], dec[:, half:]

    def per_batch(bb, carry):
        def per_chunk(r, st):
            row = bb * cpt + r
            sp_ref[pl.ds(row, 1), :] = st
            e = e_ref[pl.ds(row, 1), :]
            re, im = st[:, :half], st[:, half:]
            nre = ar * re - ai * im + e[:, :half]
            nim = ar * im + ai * re + e[:, half:]
            return jnp.concatenate([nre, nim], axis=-1)

        st = lax.fori_loop(0, cpt, per_chunk, st_ref[pl.ds(bb, 1), :])
        st_ref[pl.ds(bb, 1), :] = st
        return carry

    lax.fori_loop(0, nb, per_batch, 0)
    y = _dot(u, tmat_ref[...]) + _dot(sp_ref[...].astype(BF16), win_ref[...])
    y_ref[...] = jax.nn.gelu(y).astype(BF16)
    if whole:
        sfin_ref[...] = st_ref[...]
    else:
        sfin_ref[pl.ds(b, 1), :] = st_ref[...]


def _ssm(u4, ops, s0, batch, n_chunks):
    tmat, wend, win, decay = ops
    rows = batch * n_chunks
    kdim = SSM_CHUNK * LANES
    if n_chunks % SSM_ROWS == 0:
        nb, cpt = 1, SSM_ROWS
    else:
        nb, cpt = batch, n_chunks
    r = nb * cpt
    tiles = n_chunks // cpt
    grid = (N_OCT, batch // nb, tiles)
    wspec = lambda shape: pl.BlockSpec((None,) + shape, lambda o, b, i: (o, 0, 0))
    return pl.pallas_call(
        functools.partial(_ssm_kernel, nb=nb, cpt=cpt),
        out_shape=(jax.ShapeDtypeStruct((N_OCT, rows, kdim), BF16),
                   jax.ShapeDtypeStruct((N_OCT, batch, OCT_STATE), F32)),
        grid=grid,
        in_specs=[pl.BlockSpec((None, r, kdim), lambda o, b, i: (o, b * tiles + i, 0)),
                  wspec((kdim, kdim)), wspec((kdim, OCT_STATE)), wspec((OCT_STATE, kdim)), wspec((1, OCT_STATE)),
                  pl.BlockSpec((None, batch, OCT_STATE), lambda o, b, i: (o, 0, 0))],
        out_specs=(pl.BlockSpec((None, r, kdim), lambda o, b, i: (o, b * tiles + i, 0)),
                   pl.BlockSpec((None, batch, OCT_STATE), lambda o, b, i: (o, 0, 0))),
        scratch_shapes=[pltpu.VMEM((r, OCT_STATE), F32), pltpu.VMEM((r, OCT_STATE), F32),
                        pltpu.VMEM((nb, OCT_STATE), F32)],
        compiler_params=_cparams(("arbitrary", "arbitrary", "arbitrary")),
        name="ssm",
    )(u4, tmat, wend, win, decay, s0)


def _bias_placement():
    import numpy as np
    pm = np.zeros((3, N_HEADS, N_HEADS // 2 * LANES), np.float32)
    for piece in range(3):
        for h in range(N_HEADS):
            pm[piece, h, LANES * (h // 2) + 3 * (h % 2) + piece] = 1.0
    return jnp.asarray(pm)


def _fprep_p_kernel(lf_ref, k_ref, pm_ref, ka_ref, carry_ref):
    i = pl.program_id(1)

    @pl.when(i == 0)
    def _():
        carry_ref[...] = jnp.zeros_like(carry_ref)

    tile = lf_ref.shape[0]
    r = lax.broadcasted_iota(I32, (tile, tile), 0)
    c = lax.broadcasted_iota(I32, (tile, tile), 1)
    tri = (c <= r).astype(F32)
    cum = _dot_hi(tri, lf_ref[...]) + carry_ref[0:1, 0:N_HEADS]
    carry_ref[0:1, 0:N_HEADS] = cum[tile - 1:tile, :]
    nf = cum * (-LOG2E)
    hi = nf.astype(BF16).astype(F32)
    r1 = nf - hi
    mid = r1.astype(BF16).astype(F32)
    lo = (r1 - mid).astype(BF16).astype(F32)
    slab = _dot(hi, pm_ref[0]) + _dot(mid, pm_ref[1]) + _dot(lo, pm_ref[2])
    k = k_ref[...]
    for p in range(N_HEADS // 2):
        ka_ref[:, 2 * p * LANES:(2 * p + 1) * LANES] = k[:, p * LANES:(p + 1) * LANES].astype(BF16)
        ka_ref[:, (2 * p + 1) * LANES:(2 * p + 2) * LANES] = slab[:, p * LANES:(p + 1) * LANES].astype(BF16)


def _fprep_p(logf, k, batch, t):
    tile = ATT_BLK
    tiles = t // tile
    pm = _bias_placement()
    kaug = pl.pallas_call(
        _fprep_p_kernel,
        out_shape=jax.ShapeDtypeStruct((batch * t, 2 * D_ATT), BF16),
        grid=(batch, tiles),
        in_specs=[pl.BlockSpec((tile, N_HEADS), lambda b, i: (b * tiles + i, 0)),
                  pl.BlockSpec((tile, D_ATT), lambda b, i: (b * tiles + i, 0)),
                  pl.BlockSpec(pm.shape, lambda b, i: (0, 0, 0))],
        out_specs=pl.BlockSpec((tile, 2 * D_ATT), lambda b, i: (b * tiles + i, 0)),
        scratch_shapes=[pltpu.VMEM((8, LANES), F32)],
        compiler_params=_cparams(("arbitrary", "arbitrary")),
        name="fprep_p",
    )(logf, k, pm)
    return kaug.reshape(batch * tiles, tile, 2 * D_ATT)


def _attn_p_kernel(qt_ref, ka_ref, vt_ref, o_ref, qbd_ref, acc_ref, sa_ref, sb_ref):
    i = pl.program_id(2)
    bq = ATT_BLK
    hd = HEAD_DIM
    r = lax.broadcasted_iota(I32, (LANES, 2 * bq), 0)
    c = lax.broadcasted_iota(I32, (LANES, 2 * bq), 1)
    ones = jnp.where(((r < 3) & (c < bq)) | ((r >= 3) & (r < 6) & (c >= bq)), 1.0, 0.0).astype(BF16)
    zero = jnp.zeros((hd, bq), BF16)
    for pp in range(ATT_PAIRS):
        qb = qt_ref[pp * LANES:(pp + 1) * LANES, :]
        qbd_ref[pp, 0:hd, 0:bq] = qb[0:hd]
        qbd_ref[pp, 0:hd, bq:2 * bq] = zero
        qbd_ref[pp, hd:2 * hd, 0:bq] = zero
        qbd_ref[pp, hd:2 * hd, bq:2 * bq] = qb[hd:2 * hd]
        qbd_ref[pp, 2 * hd:2 * hd + LANES, :] = ones
    acc_ref[...] = jnp.zeros_like(acc_ref)

    nq = ka_ref.shape[0]

    def scores(s_ref, j):
        for pp in range(ATT_PAIRS):
            s_ref[pp] = _dot(ka_ref[j, :, pp * 2 * LANES:(pp + 1) * 2 * LANES], qbd_ref[pp])

    def attend(s_ref, j, carry, masked):
        out = []
        for pp in range(ATT_PAIRS):
            m_prev, l_prev = carry[2 * pp], carry[2 * pp + 1]
            s = s_ref[pp]
            if masked:
                key = j * bq + lax.broadcasted_iota(I32, s.shape, 0)
                qq = lax.broadcasted_iota(I32, s.shape, 1)
                qq = i * bq + jnp.where(qq >= bq, qq - bq, qq)
                s = jnp.where(key <= qq, s, -jnp.inf)
            m_new = jnp.maximum(m_prev, jnp.max(s, axis=0, keepdims=True))
            alpha = jnp.exp2(m_prev - m_new)
            p = jnp.exp2(s - m_new)
            l_new = alpha * l_prev + jnp.sum(p, axis=0, keepdims=True)
            pb = p.astype(BF16)
            vb = vt_ref[jnp.minimum(j, nq - 1), pp * LANES:(pp + 1) * LANES, :]
            r0 = pp * LANES
            acc_ref[r0:r0 + hd] = alpha[:, 0:bq] * acc_ref[r0:r0 + hd] + _dot(vb[0:hd], pb[:, 0:bq])
            acc_ref[r0 + hd:r0 + 2 * hd] = (alpha[:, bq:2 * bq] * acc_ref[r0 + hd:r0 + 2 * hd]
                                            + _dot(vb[hd:2 * hd], pb[:, bq:2 * bq]))
            out.extend((m_new, l_new))
        return tuple(out)

    def double_step(m, carry):
        j = 2 * m
        scores(sb_ref, j + 1)
        carry = attend(sa_ref, j, carry, False)
        scores(sa_ref, j + 2)
        return attend(sb_ref, j + 1, carry, False)

    init = (jnp.full((1, 2 * bq), -jnp.inf, F32), jnp.zeros((1, 2 * bq), F32)) * ATT_PAIRS
    scores(sa_ref, 0)
    carry = lax.fori_loop(0, i // 2, double_step, init)
    j = 2 * (i // 2)
    scores(sb_ref, jnp.minimum(j + 1, nq - 1))
    carry = attend(sa_ref, j, carry, True)
    carry = attend(sb_ref, j + 1, carry, True)
    for pp in range(ATT_PAIRS):
        l = carry[2 * pp + 1]
        r0 = pp * LANES
        o_ref[r0:r0 + hd] = acc_ref[r0:r0 + hd] / l[:, 0:bq]
        o_ref[r0 + hd:r0 + 2 * hd] = acc_ref[r0 + hd:r0 + 2 * hd] / l[:, bq:2 * bq]


def _attn_p(qt, kaug, vt, batch, t):
    blk = ATT_BLK
    nq = t // blk
    ngrp = N_HEADS // 2 // ATT_PAIRS
    rows = ATT_PAIRS * LANES
    qt4 = qt.reshape(batch, nq, D_ATT, blk)
    vt4 = vt.reshape(batch, nq, D_ATT, blk)
    ka4 = kaug.reshape(batch, nq, blk, 2 * D_ATT)
    return pl.pallas_call(
        _attn_p_kernel,
        out_shape=jax.ShapeDtypeStruct((batch, D_ATT, t), F32),
        grid=(batch, ngrp, nq),
        in_specs=[pl.BlockSpec((None, None, rows, blk), lambda b, p, i: (b, i, p, 0)),
                  pl.BlockSpec((None, nq, blk, 2 * rows), lambda b, p, i: (b, 0, 0, p)),
                  pl.BlockSpec((None, nq, rows, blk), lambda b, p, i: (b, 0, p, 0))],
        out_specs=pl.BlockSpec((None, rows, blk), lambda b, p, i: (b, p, i)),
        scratch_shapes=[pltpu.VMEM((ATT_PAIRS, 2 * LANES, 2 * blk), BF16), pltpu.VMEM((rows, blk), F32),
                        pltpu.VMEM((ATT_PAIRS, blk, 2 * blk), F32), pltpu.VMEM((ATT_PAIRS, blk, 2 * blk), F32)],
        compiler_params=_cparams(("arbitrary", "arbitrary", "arbitrary")),
        name="attn_p",
    )(qt4, ka4, vt4)


def _lane_cumsum(x):
    n = x.shape[-1]
    lane = lax.broadcasted_iota(I32, x.shape, x.ndim - 1)
    s = 1
    while s < n:
        x = x + jnp.where(lane >= s, pltpu.roll(x, s, x.ndim - 1), 0.0)
        s *= 2
    return x


def _fprep_s_kernel(cl_ref, ln_ref, fc_ref, fn_ref):
    b, h, p = cl_ref.shape
    cum = _lane_cumsum(cl_ref[...].reshape(b * h, p))
    fc_ref[...] = (-cum).reshape(b, h, p)
    total = cum[:, p - 1:p]
    cn = _lane_cumsum(ln_ref[...].reshape(b * h, LANES))
    fn_ref[...] = (-(total + cn)).reshape(b, h, LANES)


def _fprep_s(cache_lf_t, new_lf_t):
    b, h, p = cache_lf_t.shape
    return pl.pallas_call(
        _fprep_s_kernel,
        out_shape=(jax.ShapeDtypeStruct((b, h, p), F32), jax.ShapeDtypeStruct((b, h, LANES), F32)),
        grid=(1,),
        in_specs=[pl.BlockSpec((b, h, p), lambda i: (0, 0, 0)), pl.BlockSpec((b, h, LANES), lambda i: (0, 0, 0))],
        out_specs=(pl.BlockSpec((b, h, p), lambda i: (0, 0, 0)), pl.BlockSpec((b, h, LANES), lambda i: (0, 0, 0))),
        compiler_params=_cparams(("arbitrary",)),
        name="fprep_s",
    )(cache_lf_t, new_lf_t)


def _attn_s_kernel(q_ref, ck_ref, cv_ref, kn_ref, vn_ref, fc_ref, fn_ref, o_ref, qbd_ref, m_ref, l_ref, acc_ref,
                   *, tq):
    j = pl.program_id(1)
    nkv = pl.num_programs(1)
    rows = N_HEADS * tq
    bk = ck_ref.shape[1]

    @pl.when(j == 0)
    def _():
        q = q_ref[...]
        qrep = jnp.broadcast_to(q[None], (N_HEADS, tq, D_ATT)).reshape(rows, D_ATT)
        rh = lax.broadcasted_iota(I32, (rows, D_ATT), 0) // tq
        ch = lax.broadcasted_iota(I32, (rows, D_ATT), 1) // HEAD_DIM
        qbd_ref[...] = jnp.where(rh == ch, qrep, jnp.zeros_like(qrep))
        m_ref[...] = jnp.full_like(m_ref, -jnp.inf)
        l_ref[...] = jnp.zeros_like(l_ref)
        acc_ref[...] = jnp.zeros_like(acc_ref)

    def update(s, vt):
        m_prev = m_ref[...]
        m_new = jnp.maximum(m_prev, jnp.max(s, axis=1, keepdims=True))
        alpha = jnp.exp(m_prev - m_new)
        p = jnp.exp(s - m_new)
        l_ref[...] = alpha * l_ref[...] + jnp.sum(p, axis=1, keepdims=True)
        m_ref[...] = m_new
        acc_ref[...] = alpha * acc_ref[...] + _dot_nt(p.astype(BF16), vt)

    def bias(f, width):
        return jnp.broadcast_to(f[:, None, :], (N_HEADS, tq, width)).reshape(rows, width)

    s = _dot(qbd_ref[...], ck_ref[...].astype(BF16)) + bias(fc_ref[...], bk)
    update(s, cv_ref[...].astype(BF16))

    @pl.when(j == nkv - 1)
    def _():
        s2 = _dot(qbd_ref[...], kn_ref[...].astype(BF16)) + bias(fn_ref[...][:, 0:tq], tq)
        key = lax.broadcasted_iota(I32, (rows, tq), 1)
        qq = lax.broadcasted_iota(I32, (rows, tq), 0) % tq
        update(jnp.where(key <= qq, s2, -jnp.inf), vn_ref[...].astype(BF16))
        o = acc_ref[...] / l_ref[...]
        rh = lax.broadcasted_iota(I32, (rows, D_ATT), 0) // tq
        ch = lax.broadcasted_iota(I32, (rows, D_ATT), 1) // HEAD_DIM
        o = jnp.where(rh == ch, o, 0.0).reshape(N_HEADS, tq, D_ATT)
        o_ref[...] = jnp.sum(o, axis=0)


def _attn_s(q, cache_k, cache_v, k_new, v_new, fc, fn):
    b, tq, _ = q.shape
    p = cache_k.shape[2]
    bk = min(SAMPLE_KV_BLK, p)
    nkv = p // bk
    rows = N_HEADS * tq
    tok = lambda: pl.BlockSpec((None, tq, D_ATT), lambda bi, j: (bi, 0, 0))
    new = lambda: pl.BlockSpec((None, D_ATT, tq), lambda bi, j: (bi, 0, 0))
    return pl.pallas_call(
        functools.partial(_attn_s_kernel, tq=tq),
        out_shape=jax.ShapeDtypeStruct((b, tq, D_ATT), F32),
        grid=(b, nkv),
        in_specs=[tok(),
                  pl.BlockSpec((None, D_ATT, bk), lambda bi, j: (bi, 0, j)),
                  pl.BlockSpec((None, D_ATT, bk), lambda bi, j: (bi, 0, j)),
                  new(), new(),
                  pl.BlockSpec((None, N_HEADS, bk), lambda bi, j: (bi, 0, j)),
                  pl.BlockSpec((None, N_HEADS, LANES), lambda bi, j: (bi, 0, 0))],
        out_specs=tok(),
        scratch_shapes=[pltpu.VMEM((rows, D_ATT), BF16), pltpu.VMEM((rows, 1), F32),
                        pltpu.VMEM((rows, 1), F32), pltpu.VMEM((rows, D_ATT), F32)],
        compiler_params=_cparams(("arbitrary", "arbitrary")),
        name="attn_s",
    )(q, cache_k, cache_v, k_new, v_new, fc, fn)


def _outproj_kernel(x_ref, y4_ref, att_ref, gt_ref, sh_ref, sc_ref, gf_ref, wglu_ref, bglu_ref, wout_ref,
                    wr_ref, br_ref, cin_ref,
                    x1_ref, h2_ref, idx_ref, gate_ref, rank_ref, cnt_ref, carry_ref, *, att_transposed):
    i = pl.program_id(0)
    nb, tt, d = x_ref.shape
    tm = nb * tt

    @pl.when(i == 0)
    def _():
        carry_ref[...] = cin_ref[...]

    ys = jnp.concatenate([y4_ref[o] for o in range(N_OCT)], axis=-1)
    ysf = ys.astype(F32)
    glu = ysf * jax.nn.sigmoid(_dot(ys, wglu_ref[...]) + bglu_ref[...])
    att = att_ref[...]
    if att_transposed:
        att = att.T
    mix = _dot(glu.astype(BF16), wout_ref[0:D_SSM, :]) + _dot(att.astype(BF16), wout_ref[D_SSM:, :])
    x1 = x_ref[...] + gt_ref[...] * mix.reshape(nb, tt, d)
    x1_ref[...] = x1
    h2 = _norm_mod(x1, gf_ref[...], sc_ref[...], sh_ref[...]).reshape(tm, d)
    h2_ref[...] = h2

    logits = lax.dot_general(wr_ref[...], h2, (((1,), (1,)), ((), ())), preferred_element_type=F32,
                             precision=lax.Precision.HIGHEST) + br_ref[...]
    sub = lax.broadcasted_iota(I32, logits.shape, 0)
    work = logits
    vals, idxs = [], []
    for _ in range(TOP_K):
        mx = jnp.max(work, axis=0, keepdims=True)
        ix = jnp.min(jnp.where(work == mx, sub, N_EXPERTS), axis=0, keepdims=True)
        vals.append(mx)
        idxs.append(ix)
        work = jnp.where(sub == ix, -jnp.inf, work)
    ex = [jnp.exp(v - vals[0]) for v in vals]
    den = ex[0] + ex[1] + ex[2] + ex[3]
    mh = jnp.where(work == -jnp.inf, 1.0, 0.0)
    r = lax.broadcasted_iota(I32, (tm, tm), 0)
    c = lax.broadcasted_iota(I32, (tm, tm), 1)
    earlier = jnp.where(r < c, 1.0, 0.0).astype(BF16)
    carry = carry_ref[...]
    before = _dot(mh.astype(BF16), earlier) + carry[:, 0:1]
    carry_ref[...] = carry + jnp.sum(mh, axis=1, keepdims=True)
    for kk in range(TOP_K):
        idx_ref[kk:kk + 1, :] = idxs[kk]
        gate_ref[kk:kk + 1, :] = ex[kk] / den
        rk = jnp.sum(jnp.where(sub == idxs[kk], before, 0.0), axis=0, keepdims=True)
        rank_ref[kk:kk + 1, :] = rk.astype(I32)
    cnt_ref[...] = carry_ref[...]


def _outproj(x, y4, att, gt, sh, sc, g_ffn, w, counts_in, *, att_transposed, tm):
    b, t, d = x.shape
    n = b * t
    tm = min(tm, n)
    tt = min(t, tm)
    nb = tm // tt
    tpb = t // tt

    def xmap(i):
        return (i // tpb, i % tpb, 0) if nb == 1 else (i, 0, 0)

    def bmap(i):
        return (i // tpb, 0, 0) if nb == 1 else (i, 0, 0)

    full = lambda shape: pl.BlockSpec(shape, lambda i: (0,) * len(shape))
    if att_transposed:
        att_spec = pl.BlockSpec((None, D_ATT, tm), lambda i: (i // tpb, 0, i % tpb))
    else:
        att_spec = pl.BlockSpec((tm, D_ATT), lambda i: (i, 0))
    choice = lambda dt: (jax.ShapeDtypeStruct((TOP_K, n), dt), pl.BlockSpec((TOP_K, tm), lambda i: (0, i)))
    outs = [(jax.ShapeDtypeStruct((b, t, d), F32), pl.BlockSpec((nb, tt, d), xmap)),
            (jax.ShapeDtypeStruct((n, d), F32), pl.BlockSpec((tm, d), lambda i: (i, 0))),
            choice(I32), choice(F32), choice(I32),
            (jax.ShapeDtypeStruct((N_EXPERTS, LANES), F32), full((N_EXPERTS, LANES)))]
    return pl.pallas_call(
        functools.partial(_outproj_kernel, att_transposed=att_transposed),
        out_shape=tuple(o[0] for o in outs),
        grid=(n // tm,),
        in_specs=[pl.BlockSpec((nb, tt, d), xmap),
                  pl.BlockSpec((N_OCT, tm, LANES), lambda i: (0, i, 0)),
                  att_spec,
                  pl.BlockSpec((nb, 1, d), bmap), pl.BlockSpec((nb, 1, d), bmap), pl.BlockSpec((nb, 1, d), bmap),
                  full((1, d)), full((D_SSM, D_SSM)), full((1, D_SSM)), full((d, d)),
                  full((N_EXPERTS, d)), full((N_EXPERTS, 1)), full((N_EXPERTS, LANES))],
        out_specs=tuple(o[1] for o in outs),
        scratch_shapes=[pltpu.VMEM((N_EXPERTS, LANES), F32)],
        compiler_params=_cparams(("arbitrary",)),
        name="outproj_t" if att_transposed else "outproj",
    )(x, y4, att, gt, sh, sc, g_ffn, w["wglu"], w["bglu"], w["wout"], w["wr"], w["br"], counts_in)


def _dispatch_kernel(zst_ref, nu_ref, dest_ref, *rest, tm, tiles):
    h_refs = rest[:len(tiles)]
    xs_ref, zero_ref, zsem, sem = rest[len(tiles):]
    i = pl.program_id(0)
    n_blocks = xs_ref.shape[0] // MOE_BLK

    @pl.when(i == 0)
    def _():
        zero_ref[...] = jnp.zeros_like(zero_ref)

        def zero_copy(start):
            start = pl.multiple_of(start, MOE_BLK)
            return pltpu.make_async_copy(zero_ref, xs_ref.at[pl.ds(start, MOE_BLK)], zsem)

        for e in range(N_EXPERTS):
            @pl.when(zst_ref[e] >= 0)
            def _():
                zero_copy(jnp.maximum(zst_ref[e], 0)).start()

        def tail_start(j, carry):
            zero_copy(j * MOE_BLK).start()
            return carry

        def tail_wait(j, carry):
            zero_copy(j * MOE_BLK).wait()
            return carry

        lax.fori_loop(nu_ref[0], n_blocks, tail_start, 0)
        for e in range(N_EXPERTS):
            @pl.when(zst_ref[e] >= 0)
            def _():
                zero_copy(jnp.maximum(zst_ref[e], 0)).wait()
        lax.fori_loop(nu_ref[0], n_blocks, tail_wait, 0)

    def scatter_rows(h_ref):
        def issue(t, carry):
            for kk in range(TOP_K):
                dst = dest_ref[0, 0, t * TOP_K + kk]
                pltpu.make_async_copy(h_ref.at[pl.ds(t, 1)], xs_ref.at[pl.ds(dst, 1)], sem).start()
            return carry

        lax.fori_loop(0, tm, issue, 0, unroll=8)

    first_tile = 0
    for h_ref, n_tiles in zip(h_refs, tiles):
        pl.when((i >= first_tile) & (i < first_tile + n_tiles))(functools.partial(scatter_rows, h_ref))
        first_tile += n_tiles
    n_copied = tm * TOP_K
    pltpu.make_async_copy(xs_ref.at[pl.ds(0, n_copied)], xs_ref.at[pl.ds(0, n_copied)], sem).wait()


def _dispatch(h2s, dest, zstart, n_used, n_rows, tm):
    d = h2s[0].shape[1]
    tiles = tuple(h.shape[0] // tm for h in h2s)
    nt = sum(tiles)
    dest3 = dest.reshape(nt, 1, tm * TOP_K)
    in_specs = [pl.BlockSpec((1, 1, tm * TOP_K), lambda i, *_: (i, 0, 0), memory_space=pltpu.SMEM)]
    first_tile = 0
    for n_tiles in tiles:
        in_specs.append(pl.BlockSpec(
            (tm, d), lambda i, *_, f=first_tile, m=n_tiles: (jnp.clip(i - f, 0, m - 1), 0)))
        first_tile += n_tiles
    return pl.pallas_call(
        functools.partial(_dispatch_kernel, tm=tm, tiles=tiles),
        out_shape=jax.ShapeDtypeStruct((n_rows, d), F32),
        grid_spec=pltpu.PrefetchScalarGridSpec(
            num_scalar_prefetch=2, grid=(nt,),
            in_specs=in_specs,
            out_specs=pl.BlockSpec(memory_space=pl.ANY),
            scratch_shapes=[pltpu.VMEM((MOE_BLK, d), F32), pltpu.SemaphoreType.DMA, pltpu.SemaphoreType.DMA]),
        compiler_params=_cparams(("arbitrary",)),
        name="moe_dispatch",
    )(zstart, n_used, dest3, *h2s)


def _expert_kernel(be_ref, nu_ref, xs_ref, wup_ref, bup_ref, wdn_ref, bdn_ref, ys_ref, wup_bf, wdn_bf):
    i = pl.program_id(0)
    e = be_ref[i]
    prev = be_ref[jnp.maximum(i - 1, 0)]

    @pl.when(i < nu_ref[0])
    def _():
        @pl.when((i == 0) | (prev != e))
        def _():
            wup_bf[...] = wup_ref[...].astype(BF16)
            wdn_bf[...] = wdn_ref[...].astype(BF16)

        up = _dot(xs_ref[...].astype(BF16), wup_bf[...]) + bup_ref[...]
        x_glu = jnp.minimum(up[:, :D_FF], SWIGLU_LIMIT)
        x_lin = jnp.clip(up[:, D_FF:], -SWIGLU_LIMIT, SWIGLU_LIMIT)
        act = (x_lin + 1.0) * (x_glu * jax.nn.sigmoid(SWIGLU_ALPHA * x_glu))
        ys_ref[...] = _dot(act.astype(BF16), wdn_bf[...]) + bdn_ref[...]

    @pl.when(i >= nu_ref[0])
    def _():
        ys_ref[...] = jnp.zeros_like(ys_ref)


def _experts(xs, block_e, n_used, w_up, b_up, w_down, b_down):
    n_rows, d = xs.shape
    nblk = n_rows // MOE_BLK
    rowmap = lambda i, be, nu: (jnp.minimum(i, nu[0] - 1), 0)
    emap = lambda i, be, nu: (be[i], 0, 0)
    return pl.pallas_call(
        _expert_kernel,
        out_shape=jax.ShapeDtypeStruct((n_rows, d), F32),
        grid_spec=pltpu.PrefetchScalarGridSpec(
            num_scalar_prefetch=2, grid=(nblk,),
            in_specs=[pl.BlockSpec((MOE_BLK, d), rowmap),
                      pl.BlockSpec((None, d, 2 * D_FF), emap), pl.BlockSpec((None, 1, 2 * D_FF), emap),
                      pl.BlockSpec((None, D_FF, d), emap), pl.BlockSpec((None, 1, d), emap)],
            out_specs=pl.BlockSpec((MOE_BLK, d), lambda i, be, nu: (i, 0)),
            scratch_shapes=[pltpu.VMEM((d, 2 * D_FF), BF16), pltpu.VMEM((D_FF, d), BF16)]),
        compiler_params=_cparams(("arbitrary",)),
        name="moe_experts",
    )(block_e, n_used, xs, w_up, b_up, w_down, b_down)


def _combine_kernel(dest_ref, x1_ref, gate_ref, gt_ref, gfin_ref, ys_ref, y_ref, buf_ref, sem, *, tm):
    nb, tt, d = x1_ref.shape

    def issue(t, carry):
        for kk in range(TOP_K):
            src = dest_ref[0, 0, t * TOP_K + kk]
            pltpu.make_async_copy(ys_ref.at[pl.ds(src, 1)], buf_ref.at[pl.ds(kk * tm + t, 1)], sem).start()
        return carry

    lax.fori_loop(0, tm, issue, 0, unroll=8)
    pltpu.make_async_copy(ys_ref.at[pl.ds(0, TOP_K * tm)], buf_ref, sem).wait()
    gate = gate_ref[...]
    moe = gate[:, 0:1] * buf_ref[0:tm]
    for kk in range(1, TOP_K):
        moe = moe + gate[:, kk:kk + 1] * buf_ref[kk * tm:(kk + 1) * tm]
    x2 = x1_ref[...] + gt_ref[...] * moe.reshape(nb, tt, d)
    ms = jnp.mean(x2 * x2, axis=-1, keepdims=True)
    y_ref[...] = x2 * lax.rsqrt(ms + NORM_EPS) * gfin_ref[...]


def _combine(x1, gate, dest, ys, gt, g_final, tm):
    b, t, d = x1.shape
    n = b * t
    tm = min(tm, n)
    tt = min(t, tm)
    nb = tm // tt
    tpb = t // tt
    nt = n // tm
    dest3 = dest.reshape(nt, 1, tm * TOP_K)

    def xmap(i, *_):
        return (i // tpb, i % tpb, 0) if nb == 1 else (i, 0, 0)

    def bmap(i, *_):
        return (i // tpb, 0, 0) if nb == 1 else (i, 0, 0)

    return pl.pallas_call(
        functools.partial(_combine_kernel, tm=tm),
        out_shape=jax.ShapeDtypeStruct((b, t, d), F32),
        grid=(nt,),
        in_specs=[pl.BlockSpec((1, 1, tm * TOP_K), lambda i: (i, 0, 0), memory_space=pltpu.SMEM),
                  pl.BlockSpec((nb, tt, d), xmap),
                  pl.BlockSpec((tm, TOP_K), lambda i: (i, 0)),
                  pl.BlockSpec((nb, 1, d), bmap),
                  pl.BlockSpec((1, d), lambda i: (0, 0)),
                  pl.BlockSpec(memory_space=pl.ANY)],
        out_specs=pl.BlockSpec((nb, tt, d), xmap),
        scratch_shapes=[pltpu.VMEM((TOP_K * tm, d), F32), pltpu.SemaphoreType.DMA],
        compiler_params=_cparams(("arbitrary",)),
        name="moe_combine",
    )(dest3, x1, gate, gt, g_final, ys)


def _moe(streams, counts, g_final, w_up, b_up, w_down, b_down):
    n_total = sum(s["h2"].shape[0] for s in streams)
    cnt = counts[:, 0].astype(I32)
    padded = (cnt + MOE_BLK - 1) // MOE_BLK * MOE_BLK
    pad_end = jnp.cumsum(padded)
    off = (pad_end - padded).astype(I32)
    n_blocks = -(-n_total * TOP_K // MOE_BLK) + N_EXPERTS
    n_rows = n_blocks * MOE_BLK
    starts = jnp.arange(n_blocks, dtype=I32) * MOE_BLK
    block_e = jnp.minimum(jnp.sum((pad_end[None, :] <= starts[:, None]).astype(I32), axis=1), N_EXPERTS - 1)
    n_used = (pad_end[-1:] // MOE_BLK).astype(I32)
    zstart = jnp.where(padded > 0, pad_end - MOE_BLK, -1).astype(I32)
    experts = jnp.arange(N_EXPERTS, dtype=I32)
    dests = []
    for s in streams:
        idx = s["idx"].T
        dests.append((jnp.sum(jnp.where(idx[:, :, None] == experts, off, 0), axis=-1) + s["rank"].T).astype(I32))
    tm = min([256] + [s["h2"].shape[0] for s in streams])
    xs = _dispatch([s["h2"] for s in streams], jnp.concatenate(dests, axis=0), zstart, n_used, n_rows, tm)
    ys = _experts(xs, block_e, n_used, w_up, b_up, w_down, b_down)
    return [_combine(s["x1"], s["gate"].T, dest, ys, s["gt"], g_final, 128) for s, dest in zip(streams, dests)]


def _stream(x, mod, w, ssm_ops, s0, cache, params, counts_in):
    b, t, d = x.shape
    n = b * t
    sh_mix, sc_mix, gt_mix, sh_ffn, sc_ffn, gt_ffn = mod
    prompt = cache is None
    tm = 512
    u4, q, kt, vt, lft, vtb, k_tok, lf_tok = _inproj(x, sh_mix, sc_mix, params["g_mix"], w, prompt=prompt, tm=tm)
    n_chunks = t // SSM_CHUNK
    y4, s_fin = _ssm(u4.reshape(N_OCT, b * n_chunks, SSM_CHUNK * LANES), ssm_ops, s0, b, n_chunks)
    y4 = y4.reshape(N_OCT, n, LANES)
    if prompt:
        kaug = _fprep_p(lf_tok, k_tok, b, t)
        att = _attn_p(q, kaug, vtb, b, t)
    else:
        cache_k, cache_v, cache_lf = cache
        p = cache_k.shape[1]
        ck_t = jnp.transpose(cache_k, (0, 2, 3, 1)).reshape(b, D_ATT, p)
        cv_t = jnp.transpose(cache_v, (0, 2, 3, 1)).reshape(b, D_ATT, p)
        cl_t = jnp.transpose(cache_lf, (0, 2, 1))
        ln_t = jnp.pad(lft, ((0, 0), (0, 0), (0, LANES - t)))
        fc, fn = _fprep_s(cl_t, ln_t)
        att = _attn_s(q.reshape(b, t, D_ATT), ck_t, cv_t, kt, vt, fc, fn).reshape(n, D_ATT)
    x1, h2, idx, gate, rank, counts = _outproj(x, y4, att, gt_mix, sh_ffn, sc_ffn, params["g_ffn"], w, counts_in,
                                               att_transposed=prompt, tm=512)
    routed = {"x1": x1, "h2": h2, "idx": idx, "gate": gate, "rank": rank, "gt": gt_ffn}
    heads = lambda a: jnp.transpose(a.reshape(b, N_HEADS, HEAD_DIM, t), (0, 3, 1, 2))[None]
    return routed, counts, heads(kt), heads(vt), jnp.transpose(lft, (0, 2, 1))[None], s_fin


def _state_in(re, im):
    b = re.shape[0]
    s = jnp.concatenate([re.reshape(b, N_OCT, OCT_STATE // 2), im.reshape(b, N_OCT, OCT_STATE // 2)], axis=-1)
    return jnp.transpose(s, (1, 0, 2))


def _state_out(s):
    b = s.shape[1]
    s = jnp.transpose(s, (1, 0, 2))
    half = OCT_STATE // 2
    re = s[:, :, :half].reshape(1, b, N_SSM_GROUPS, SSM_STATE)
    im = s[:, :, half:].reshape(1, b, N_SSM_GROUPS, SSM_STATE)
    return re, im


def kernel(x_prompt, x_sample, c_prompt, c_sample, cache_k, cache_v, cache_logf, state_ssm_re, state_ssm_im, w_ada, b_ada, g_mix, w_in, b_forget, ssm_log_dt, ssm_a_re, ssm_a_im, ssm_b_re, ssm_b_im, ssm_c_re, ssm_c_im, ssm_d, w_glu, b_glu, w_out, g_ffn, w_router, b_router, w_up, b_up, w_down, b_down, g_final):
    assert w_ada.shape[0] == 1, "single-layer trunk"
    d = D_MODEL
    bp, tp, _ = x_prompt.shape
    bs, ts, _ = x_sample.shape
    n_c = bp + bs
    c_all = jnp.pad(jnp.concatenate([c_prompt, c_sample], axis=0), ((0, -n_c % 8), (0, 0)))
    mod = _ada(c_all, w_ada[0], b_ada[0].reshape(1, 6 * d))
    mod_p = [mod[:bp, j * d:(j + 1) * d].reshape(bp, 1, d) for j in range(6)]
    mod_s = [mod[bp:n_c, j * d:(j + 1) * d].reshape(bs, 1, d) for j in range(6)]
    wi = w_in[0]
    o1, o2, o3, o4 = D_SSM, D_SSM + D_ATT, D_SSM + 2 * D_ATT, D_SSM + 3 * D_ATT
    w_q = wi[:, o1:o2].astype(BF16)
    w_k = wi[:, o2:o3].astype(BF16)
    w_f = wi[:, o4:].astype(BF16)
    unused = jnp.zeros((8, LANES), BF16)
    base = {"wu": wi[:, :o1].astype(BF16), "wkt": w_k.T, "wvt": wi[:, o3:o4].astype(BF16).T, "wft": w_f.T,
            "bfc": b_forget[0].reshape(N_HEADS, 1), "bfr": b_forget[0].reshape(1, N_HEADS),
            "wglu": w_glu[0].astype(BF16), "bglu": b_glu[0].reshape(1, D_SSM), "wout": w_out[0].astype(BF16),
            "wr": w_router[0].T, "br": b_router[0].reshape(N_EXPERTS, 1)}
    w_p = dict(base, wq=w_q.T, wk=w_k, wf=w_f)
    w_s = dict(base, wq=w_q, wk=unused, wf=unused)
    params = {"g_mix": g_mix[0].reshape(1, d), "g_ffn": g_ffn[0].reshape(1, d), "g_final": g_final.reshape(1, d),
              "w_up": w_up[0], "b_up": b_up[0].reshape(N_EXPERTS, 1, 2 * D_FF),
              "w_down": w_down[0], "b_down": b_down[0].reshape(N_EXPERTS, 1, d)}
    ssm_ops = _ssm_operators(ssm_log_dt[0], ssm_a_re[0], ssm_a_im[0], ssm_b_re[0], ssm_b_im[0],
                             ssm_c_re[0], ssm_c_im[0], ssm_d[0])
    zero_state = jnp.zeros((N_OCT, bp, OCT_STATE), F32)
    counts0 = jnp.zeros((N_EXPERTS, LANES), F32)
    r_p, counts_p, k_p, v_p, f_p, s_p = _stream(x_prompt, mod_p, w_p, ssm_ops, zero_state, None, params, counts0)
    s0 = _state_in(state_ssm_re[0], state_ssm_im[0])
    r_s, counts, k_s, v_s, f_s, s_s = _stream(x_sample, mod_s, w_s, ssm_ops, s0,
                                              (cache_k[0], cache_v[0], cache_logf[0]), params, counts_p)
    y_p, y_s = _moe([r_p, r_s], counts, params["g_final"],
                    params["w_up"], params["b_up"], params["w_down"], params["b_down"])
    sre_p, sim_p = _state_out(s_p)
    sre_s, sim_s = _state_out(s_s)
    return (y_p, y_s, k_p, v_p, f_p, sre_p, sim_p, k_s, v_s, f_s, sre_s, sim_s)
```

```python
import functools
import math

import jax
import jax.numpy as jnp
from jax import lax
from jax.experimental import pallas as pl
from jax.experimental.pallas import tpu as pltpu

F32 = jnp.float32
BF16 = jnp.bfloat16
I32 = jnp.int32

D_MODEL = 1024
D_SSM = 512
SSM_GROUP = 16
N_SSM_GROUPS = 32
SSM_STATE = 64
D_ATT = 512
HEAD_DIM = 64
N_HEADS = 8
N_EXPERTS = 32
TOP_K = 4
D_FF = 1024
SWIGLU_LIMIT = 7.0
SWIGLU_ALPHA = 1.702
NORM_EPS = 1e-5

LANES = 128
N_OCT = D_SSM // LANES
OCT_STATE = 2 * (N_SSM_GROUPS // N_OCT) * SSM_STATE
SSM_CHUNK = 8
SSM_ROWS = 256
ATT_BLK = 256
ATT_PAIRS = 2
LOG2E = math.log2(math.e)
SAMPLE_KV_BLK = 512
MOE_BLK = 256
VMEM_LIMIT = 52 * 1024 * 1024


def _cparams(sem, vmem=VMEM_LIMIT):
    return pltpu.CompilerParams(dimension_semantics=sem, vmem_limit_bytes=vmem)


def _dot(a, b):
    return jnp.dot(a, b, preferred_element_type=F32)


def _dot_nt(a, b):
    return lax.dot_general(a, b, (((1,), (1,)), ((), ())), preferred_element_type=F32)


def _dot_hi(a, b):
    return jnp.dot(a, b, preferred_element_type=F32, precision=lax.Precision.HIGHEST)


def _ada_kernel(c_ref, w_ref, b_ref, o_ref):
    c = c_ref[...]
    s = c * jax.nn.sigmoid(c)
    o_ref[...] = _dot_hi(s, w_ref[...]) + b_ref[...]


def _ada(c_all, w_ada, b_ada):
    m, d = c_all.shape
    n = w_ada.shape[1]
    return pl.pallas_call(
        _ada_kernel,
        out_shape=jax.ShapeDtypeStruct((m, n), F32),
        grid=(n // d,),
        in_specs=[pl.BlockSpec((m, d), lambda j: (0, 0)),
                  pl.BlockSpec((d, d), lambda j: (0, j)),
                  pl.BlockSpec((1, d), lambda j: (0, j))],
        out_specs=pl.BlockSpec((m, d), lambda j: (0, j)),
        compiler_params=_cparams(("arbitrary",)),
        name="ada",
    )(c_all, w_ada, b_ada)


def _norm_mod(x, g, scale, shift):
    ms = jnp.mean(x * x, axis=-1, keepdims=True)
    y = x * lax.rsqrt(ms + NORM_EPS) * g
    return y * (1.0 + scale) + shift


def _log_sigmoid(z):
    return jnp.minimum(z, 0.0) - jnp.log1p(jnp.exp(-jnp.abs(z)))


def _inproj_kernel(x_ref, sh_ref, sc_ref, g_ref, wu_ref, wq_ref, wkt_ref, wvt_ref, wft_ref, bfc_ref,
                   wk_ref, wf_ref, bfr_ref,
                   u4_ref, q_ref, kt_ref, vt_ref, lft_ref, vtb_ref, ktok_ref, lftok_ref, *, prompt):
    nb, tt, d = x_ref.shape
    tm = nb * tt
    h = _norm_mod(x_ref[...], g_ref[...], sc_ref[...], sh_ref[...]).reshape(tm, d).astype(BF16)
    u = _dot(h, wu_ref[...])
    for o in range(N_OCT):
        u4_ref[o] = u[:, o * LANES:(o + 1) * LANES].astype(BF16)
    kt = _dot_nt(wkt_ref[...], h)
    vt = _dot_nt(wvt_ref[...], h)
    lft = _log_sigmoid(_dot_nt(wft_ref[...], h) + bfc_ref[...])
    scale = HEAD_DIM ** -0.5
    if prompt:
        kt_ref[...] = kt
        vt_ref[...] = vt
        lft_ref[...] = lft
        qt = (_dot_nt(wq_ref[...], h) * (scale * LOG2E)).astype(BF16)
        vtb = vt.astype(BF16)
        for j in range(tm // ATT_BLK):
            q_ref[j] = qt[:, j * ATT_BLK:(j + 1) * ATT_BLK]
            vtb_ref[j] = vtb[:, j * ATT_BLK:(j + 1) * ATT_BLK]
        ktok_ref[...] = _dot(h, wk_ref[...]).astype(BF16)
        lftok_ref[...] = _log_sigmoid(_dot(h, wf_ref[...]) + bfr_ref[...])
    else:
        for bl in range(nb):
            kt_ref[bl] = kt[:, bl * tt:(bl + 1) * tt]
            vt_ref[bl] = vt[:, bl * tt:(bl + 1) * tt]
            lft_ref[bl] = lft[:, bl * tt:(bl + 1) * tt]
        q_ref[...] = (_dot(h, wq_ref[...]) * scale).astype(BF16)
        vtb_ref[...] = jnp.zeros_like(vtb_ref)
        ktok_ref[...] = jnp.zeros_like(ktok_ref)
        lftok_ref[...] = jnp.zeros_like(lftok_ref)


def _inproj(x, shift, scale, g_mix, w, *, prompt, tm):
    b, t, d = x.shape
    n = b * t
    tm = min(tm, n)
    tt = min(t, tm)
    nb = tm // tt
    tpb = t // tt
    assert (nb == 1) == prompt
    grid = (n // tm,)

    def xmap(i):
        return (i // tpb, i % tpb, 0) if nb == 1 else (i, 0, 0)

    def bmap(i):
        return (i // tpb, 0, 0) if nb == 1 else (i, 0, 0)

    def tmap(i):
        return (i // tpb, 0, i % tpb) if nb == 1 else (i, 0, 0)

    full = lambda a: pl.BlockSpec(a.shape, lambda i: (0,) * a.ndim)
    names = ("wu", "wq", "wkt", "wvt", "wft", "bfc", "wk", "wf", "bfr")
    in_specs = [pl.BlockSpec((nb, tt, d), xmap),
                pl.BlockSpec((nb, 1, d), bmap), pl.BlockSpec((nb, 1, d), bmap),
                pl.BlockSpec((1, d), lambda i: (0, 0))] + [full(w[k]) for k in names]
    tblk = lambda rows: pl.BlockSpec((None if nb == 1 else nb, rows, tt), tmap)
    dummy = (jax.ShapeDtypeStruct((8, LANES), F32), pl.BlockSpec((8, LANES), lambda i: (0, 0)))
    if prompt:
        nblk = n // ATT_BLK
        blocked = (jax.ShapeDtypeStruct((nblk, D_ATT, ATT_BLK), BF16),
                   pl.BlockSpec((tm // ATT_BLK, D_ATT, ATT_BLK), lambda i: (i, 0, 0)))
        q_out, vtb_out = blocked, blocked
        ktok_out = (jax.ShapeDtypeStruct((n, D_ATT), BF16), pl.BlockSpec((tm, D_ATT), lambda i: (i, 0)))
        lftok_out = (jax.ShapeDtypeStruct((n, N_HEADS), F32), pl.BlockSpec((tm, N_HEADS), lambda i: (i, 0)))
    else:
        q_out = (jax.ShapeDtypeStruct((n, D_ATT), BF16), pl.BlockSpec((tm, D_ATT), lambda i: (i, 0)))
        vtb_out = ktok_out = lftok_out = dummy
    outs = [(jax.ShapeDtypeStruct((N_OCT, n, LANES), BF16), pl.BlockSpec((N_OCT, tm, LANES), lambda i: (0, i, 0))),
            q_out,
            (jax.ShapeDtypeStruct((b, D_ATT, t), F32), tblk(D_ATT)),
            (jax.ShapeDtypeStruct((b, D_ATT, t), F32), tblk(D_ATT)),
            (jax.ShapeDtypeStruct((b, N_HEADS, t), F32), tblk(N_HEADS)),
            vtb_out, ktok_out, lftok_out]
    return pl.pallas_call(
        functools.partial(_inproj_kernel, prompt=prompt),
        out_shape=tuple(o[0] for o in outs), grid=grid, in_specs=in_specs, out_specs=tuple(o[1] for o in outs),
        compiler_params=_cparams(("arbitrary",)),
        name="inproj_p" if prompt else "inproj_s",
    )(x, shift, scale, g_mix, *[w[k] for k in names])


def _ssm_operators(log_dt, a_re, a_im, b_re, b_im, c_re, c_im, d_skip):
    L, G, P, C = SSM_CHUNK, N_SSM_GROUPS, SSM_STATE, SSM_GROUP
    q = G // N_OCT
    lam = lax.complex(a_re, a_im)
    dt = jnp.exp(log_dt)[:, None]
    lam_dt = lam * dt
    lam_bar = jnp.exp(lam_dt)
    b_bar = ((lam_bar - 1.0) / lam)[:, :, None] * lax.complex(b_re, b_im)
    c_mat = lax.complex(c_re, c_im)
    pw = jnp.exp(lam_dt[None] * jnp.arange(L + 1, dtype=F32)[:, None, None])
    kern = jnp.real(jnp.einsum('gcp,dgp,gpk->dgck', c_mat, pw[:L], b_bar))
    ksm = jnp.transpose(kern, (1, 0, 3, 2)).reshape(N_OCT, q, L, C, C)
    ksm = jnp.transpose(ksm, (0, 2, 1, 3, 4)).reshape(N_OCT, L, q * C, C)
    pwr = pw[L - 1 - jnp.arange(L)]
    wb = jnp.einsum('jgp,gpk->jgkp', pwr, b_bar)
    wsm = jnp.stack([jnp.real(wb), jnp.imag(wb)], axis=0).reshape(2, L, N_OCT, q * C, P)
    wsm = jnp.transpose(wsm, (2, 1, 0, 3, 4))
    cl = c_mat[None] * pw[1:L + 1][:, :, None, :]
    vsm = jnp.stack([jnp.real(cl), -jnp.imag(cl)], axis=0).reshape(2, L, N_OCT, q, C, P)
    vsm = jnp.transpose(vsm, (2, 1, 0, 3, 5, 4)).reshape(N_OCT, L, OCT_STATE, C)
    tmat, wend, win = _ssm_prep(ksm, wsm, vsm, d_skip.reshape(N_OCT, 1, LANES))
    dec = pw[L].reshape(N_OCT, q * P)
    decay = jnp.concatenate([jnp.real(dec), jnp.imag(dec)], axis=-1).reshape(N_OCT, 1, OCT_STATE)
    return tmat, wend, win, decay.astype(F32)


def _ssm_prep_kernel(ksm_ref, wsm_ref, vsm_ref, d_ref, tmat_ref, wend_ref, win_ref):
    L, C, P = SSM_CHUNK, SSM_GROUP, SSM_STATE
    half = OCT_STATE // 2

    def spread(period, width):
        r = lax.broadcasted_iota(I32, (period, width), 0)
        c = lax.broadcasted_iota(I32, (period, width), 1)
        return jnp.where(c % period == r, 1.0, 0.0)

    def same_group(shape, row_div, row_mod, lane_div):
        r = lax.broadcasted_iota(I32, shape, 0)
        c = lax.broadcasted_iota(I32, shape, 1)
        return (r % row_mod) // row_div == c // lane_div

    e_c = spread(C, LANES)
    e_p = spread(P, half)
    grp = same_group((LANES, LANES), C, LANES, C)
    rr = lax.broadcasted_iota(I32, (LANES, LANES), 0)
    cc = lax.broadcasted_iota(I32, (LANES, LANES), 1)
    lag = []
    for d in range(L):
        blk = jnp.where(grp, _dot_hi(ksm_ref[d], e_c), 0.0)
        if d == 0:
            blk = blk + jnp.where(rr == cc, jnp.broadcast_to(d_ref[...], (LANES, LANES)), 0.0)
        lag.append(blk.astype(BF16))
    zero = jnp.zeros((LANES, LANES), BF16)
    for j in range(L):
        for t in range(L):
            tmat_ref[j * LANES:(j + 1) * LANES, t * LANES:(t + 1) * LANES] = lag[t - j] if t >= j else zero
    grp_e = same_group((LANES, half), C, LANES, P)
    for j in range(L):
        for part in range(2):
            blk = jnp.where(grp_e, _dot_hi(wsm_ref[j, part], e_p), 0.0)
            wend_ref[j * LANES:(j + 1) * LANES, part * half:(part + 1) * half] = blk.astype(BF16)
    grp_i = same_group((OCT_STATE, LANES), P, half, C)
    for t in range(L):
        blk = jnp.where(grp_i, _dot_hi(vsm_ref[t], e_c), 0.0)
        win_ref[:, t * LANES:(t + 1) * LANES] = blk.astype(BF16)


def _ssm_prep(ksm, wsm, vsm, dvec):
    kdim = SSM_CHUNK * LANES
    blk = lambda a: pl.BlockSpec((None,) + a.shape[1:], lambda o: (o,) + (0,) * (a.ndim - 1))
    out = lambda r, c: (jax.ShapeDtypeStruct((N_OCT, r, c), BF16), pl.BlockSpec((None, r, c), lambda o: (o, 0, 0)))
    outs = [out(kdim, kdim), out(kdim, OCT_STATE), out(OCT_STATE, kdim)]
    return pl.pallas_call(
        _ssm_prep_kernel,
        out_shape=tuple(o[0] for o in outs),
        grid=(N_OCT,),
        in_specs=[blk(ksm), blk(wsm), blk(vsm), blk(dvec)],
        out_specs=tuple(o[1] for o in outs),
        compiler_params=_cparams(("arbitrary",)),
        name="ssm_prep",
    )(ksm, wsm, vsm, dvec)


def _ssm_kernel(u_ref, tmat_ref, wend_ref, win_ref, dec_ref, s0_ref, y_ref, sfin_ref, e_ref, sp_ref, st_ref,
                *, nb, cpt):
    b = pl.program_id(1)
    i = pl.program_id(2)
    half = OCT_STATE // 2
    whole = nb == s0_ref.shape[0]

    @pl.when(i == 0)
    def _():
        st_ref[...] = s0_ref[...] if whole else s0_ref[pl.ds(b, 1), :]

    u = u_ref[...]
    e_ref[...] = _dot(u, wend_ref[...])
    dec = dec_ref[...]
    ar, ai = dec[:, :half], dec[:, half:]

    def per_batch(bb, carry):
        def per_chunk(r, st):
            row = bb * cpt + r
            sp_ref[pl.ds(row, 1), :] = st
            e = e_ref[pl.ds(row, 1), :]
            re, im = st[:, :half], st[:, half:]
            nre = ar * re - ai * im + e[:, :half]
            nim = ar * im + ai * re + e[:, half:]
            return jnp.concatenate([nre, nim], axis=-1)

        st = lax.fori_loop(0, cpt, per_chunk, st_ref[pl.ds(bb, 1), :])
        st_ref[pl.ds(bb, 1), :] = st
        return carry

    lax.fori_loop(0, nb, per_batch, 0)
    y = _dot(u, tmat_ref[...]) + _dot(sp_ref[...].astype(BF16), win_ref[...])
    y_ref[...] = jax.nn.gelu(y).astype(BF16)
    if whole:
        sfin_ref[...] = st_ref[...]
    else:
        sfin_ref[pl.ds(b, 1), :] = st_ref[...]


def _ssm(u4, ops, s0, batch, n_chunks):
    tmat, wend, win, decay = ops
    rows = batch * n_chunks
    kdim = SSM_CHUNK * LANES
    if n_chunks % SSM_ROWS == 0:
        nb, cpt = 1, SSM_ROWS
    else:
        nb, cpt = batch, n_chunks
    r = nb * cpt
    tiles = n_chunks // cpt
    grid = (N_OCT, batch // nb, tiles)
    wspec = lambda shape: pl.BlockSpec((None,) + shape, lambda o, b, i: (o, 0, 0))
    return pl.pallas_call(
        functools.partial(_ssm_kernel, nb=nb, cpt=cpt),
        out_shape=(jax.ShapeDtypeStruct((N_OCT, rows, kdim), BF16),
                   jax.ShapeDtypeStruct((N_OCT, batch, OCT_STATE), F32)),
        grid=grid,
        in_specs=[pl.BlockSpec((None, r, kdim), lambda o, b, i: (o, b * tiles + i, 0)),
                  wspec((kdim, kdim)), wspec((kdim, OCT_STATE)), wspec((OCT_STATE, kdim)), wspec((1, OCT_STATE)),
                  pl.BlockSpec((None, batch, OCT_STATE), lambda o, b, i: (o, 0, 0))],
        out_specs=(pl.BlockSpec((None, r, kdim), lambda o, b, i: (o, b * tiles + i, 0)),
                   pl.BlockSpec((None, batch, OCT_STATE), lambda o, b, i: (o, 0, 0))),
        scratch_shapes=[pltpu.VMEM((r, OCT_STATE), F32), pltpu.VMEM((r, OCT_STATE), F32),
                        pltpu.VMEM((nb, OCT_STATE), F32)],
        compiler_params=_cparams(("arbitrary", "arbitrary", "arbitrary")),
        name="ssm",
    )(u4, tmat, wend, win, decay, s0)


def _bias_placement():
    import numpy as np
    pm = np.zeros((3, N_HEADS, N_HEADS // 2 * LANES), np.float32)
    for piece in range(3):
        for h in range(N_HEADS):
            pm[piece, h, LANES * (h // 2) + 3 * (h % 2) + piece] = 1.0
    return jnp.asarray(pm)


def _fprep_p_kernel(lf_ref, k_ref, pm_ref, ka_ref, carry_ref):
    i = pl.program_id(1)

    @pl.when(i == 0)
    def _():
        carry_ref[...] = jnp.zeros_like(carry_ref)

    tile = lf_ref.shape[0]
    r = lax.broadcasted_iota(I32, (tile, tile), 0)
    c = lax.broadcasted_iota(I32, (tile, tile), 1)
    tri = (c <= r).astype(F32)
    cum = _dot_hi(tri, lf_ref[...]) + carry_ref[0:1, 0:N_HEADS]
    carry_ref[0:1, 0:N_HEADS] = cum[tile - 1:tile, :]
    nf = cum * (-LOG2E)
    hi = nf.astype(BF16).astype(F32)
    r1 = nf - hi
    mid = r1.astype(BF16).astype(F32)
    lo = (r1 - mid).astype(BF16).astype(F32)
    slab = _dot(hi, pm_ref[0]) + _dot(mid, pm_ref[1]) + _dot(lo, pm_ref[2])
    k = k_ref[...]
    for p in range(N_HEADS // 2):
        ka_ref[:, 2 * p * LANES:(2 * p + 1) * LANES] = k[:, p * LANES:(p + 1) * LANES].astype(BF16)
        ka_ref[:, (2 * p + 1) * LANES:(2 * p + 2) * LANES] = slab[:, p * LANES:(p + 1) * LANES].astype(BF16)


def _fprep_p(logf, k, batch, t):
    tile = ATT_BLK
    tiles = t // tile
    pm = _bias_placement()
    kaug = pl.pallas_call(
        _fprep_p_kernel,
        out_shape=jax.ShapeDtypeStruct((batch * t, 2 * D_ATT), BF16),
        grid=(batch, tiles),
        in_specs=[pl.BlockSpec((tile, N_HEADS), lambda b, i: (b * tiles + i, 0)),
                  pl.BlockSpec((tile, D_ATT), lambda b, i: (b * tiles + i, 0)),
                  pl.BlockSpec(pm.shape, lambda b, i: (0, 0, 0))],
        out_specs=pl.BlockSpec((tile, 2 * D_ATT), lambda b, i: (b * tiles + i, 0)),
        scratch_shapes=[pltpu.VMEM((8, LANES), F32)],
        compiler_params=_cparams(("arbitrary", "arbitrary")),
        name="fprep_p",
    )(logf, k, pm)
    return kaug.reshape(batch * tiles, tile, 2 * D_ATT)


def _attn_p_kernel(qt_ref, ka_ref, vt_ref, o_ref, qbd_ref, acc_ref, sa_ref, sb_ref):
    i = pl.program_id(2)
    bq = ATT_BLK
    hd = HEAD_DIM
    r = lax.broadcasted_iota(I32, (LANES, 2 * bq), 0)
    c = lax.broadcasted_iota(I32, (LANES, 2 * bq), 1)
    ones = jnp.where(((r < 3) & (c < bq)) | ((r >= 3) & (r < 6) & (c >= bq)), 1.0, 0.0).astype(BF16)
    zero = jnp.zeros((hd, bq), BF16)
    for pp in range(ATT_PAIRS):
        qb = qt_ref[pp * LANES:(pp + 1) * LANES, :]
        qbd_ref[pp, 0:hd, 0:bq] = qb[0:hd]
        qbd_ref[pp, 0:hd, bq:2 * bq] = zero
        qbd_ref[pp, hd:2 * hd, 0:bq] = zero
        qbd_ref[pp, hd:2 * hd, bq:2 * bq] = qb[hd:2 * hd]
        qbd_ref[pp, 2 * hd:2 * hd + LANES, :] = ones
    acc_ref[...] = jnp.zeros_like(acc_ref)

    nq = ka_ref.shape[0]

    def scores(s_ref, j):
        for pp in range(ATT_PAIRS):
            s_ref[pp] = _dot(ka_ref[j, :, pp * 2 * LANES:(pp + 1) * 2 * LANES], qbd_ref[pp])

    def attend(s_ref, j, carry, masked):
        out = []
        for pp in range(ATT_PAIRS):
            m_prev, l_prev = carry[2 * pp], carry[2 * pp + 1]
            s = s_ref[pp]
            if masked:
                key = j * bq + lax.broadcasted_iota(I32, s.shape, 0)
                qq = lax.broadcasted_iota(I32, s.shape, 1)
                qq = i * bq + jnp.where(qq >= bq, qq - bq, qq)
                s = jnp.where(key <= qq, s, -jnp.inf)
            m_new = jnp.maximum(m_prev, jnp.max(s, axis=0, keepdims=True))
            alpha = jnp.exp2(m_prev - m_new)
            p = jnp.exp2(s - m_new)
            l_new = alpha * l_prev + jnp.sum(p, axis=0, keepdims=True)
            pb = p.astype(BF16)
            vb = vt_ref[jnp.minimum(j, nq - 1), pp * LANES:(pp + 1) * LANES, :]
            r0 = pp * LANES
            acc_ref[r0:r0 + hd] = alpha[:, 0:bq] * acc_ref[r0:r0 + hd] + _dot(vb[0:hd], pb[:, 0:bq])
            acc_ref[r0 + hd:r0 + 2 * hd] = (alpha[:, bq:2 * bq] * acc_ref[r0 + hd:r0 + 2 * hd]
                                            + _dot(vb[hd:2 * hd], pb[:, bq:2 * bq]))
            out.extend((m_new, l_new))
        return tuple(out)

    def double_step(m, carry):
        j = 2 * m
        scores(sb_ref, j + 1)
        carry = attend(sa_ref, j, carry, False)
        scores(sa_ref, j + 2)
        return attend(sb_ref, j + 1, carry, False)

    init = (jnp.full((1, 2 * bq), -jnp.inf, F32), jnp.zeros((1, 2 * bq), F32)) * ATT_PAIRS
    scores(sa_ref, 0)
    carry = lax.fori_loop(0, i // 2, double_step, init)
    j = 2 * (i // 2)
    scores(sb_ref, jnp.minimum(j + 1, nq - 1))
    carry = attend(sa_ref, j, carry, True)
    carry = attend(sb_ref, j + 1, carry, True)
    for pp in range(ATT_PAIRS):
        l = carry[2 * pp + 1]
        r0 = pp * LANES
        o_ref[r0:r0 + hd] = acc_ref[r0:r0 + hd] / l[:, 0:bq]
        o_ref[r0 + hd:r0 + 2 * hd] = acc_ref[r0 + hd:r0 + 2 * hd] / l[:, bq:2 * bq]


def _attn_p(qt, kaug, vt, batch, t):
    blk = ATT_BLK
    nq = t // blk
    ngrp = N_HEADS // 2 // ATT_PAIRS
    rows = ATT_PAIRS * LANES
    qt4 = qt.reshape(batch, nq, D_ATT, blk)
    vt4 = vt.reshape(batch, nq, D_ATT, blk)
    ka4 = kaug.reshape(batch, nq, blk, 2 * D_ATT)
    return pl.pallas_call(
        _attn_p_kernel,
        out_shape=jax.ShapeDtypeStruct((batch, D_ATT, t), F32),
        grid=(batch, ngrp, nq),
        in_specs=[pl.BlockSpec((None, None, rows, blk), lambda b, p, i: (b, i, p, 0)),
                  pl.BlockSpec((None, nq, blk, 2 * rows), lambda b, p, i: (b, 0, 0, p)),
                  pl.BlockSpec((None, nq, rows, blk), lambda b, p, i: (b, 0, p, 0))],
        out_specs=pl.BlockSpec((None, rows, blk), lambda b, p, i: (b, p, i)),
        scratch_shapes=[pltpu.VMEM((ATT_PAIRS, 2 * LANES, 2 * blk), BF16), pltpu.VMEM((rows, blk), F32),
                        pltpu.VMEM((ATT_PAIRS, blk, 2 * blk), F32), pltpu.VMEM((ATT_PAIRS, blk, 2 * blk), F32)],
        compiler_params=_cparams(("arbitrary", "arbitrary", "arbitrary")),
        name="attn_p",
    )(qt4, ka4, vt4)


def _lane_cumsum(x):
    n = x.shape[-1]
    lane = lax.broadcasted_iota(I32, x.shape, x.ndim - 1)
    s = 1
    while s < n:
        x = x + jnp.where(lane >= s, pltpu.roll(x, s, x.ndim - 1), 0.0)
        s *= 2
    return x


def _fprep_s_kernel(cl_ref, ln_ref, fc_ref, fn_ref):
    b, h, p = cl_ref.shape
    cum = _lane_cumsum(cl_ref[...].reshape(b * h, p))
    fc_ref[...] = (-cum).reshape(b, h, p)
    total = cum[:, p - 1:p]
    cn = _lane_cumsum(ln_ref[...].reshape(b * h, LANES))
    fn_ref[...] = (-(total + cn)).reshape(b, h, LANES)


def _fprep_s(cache_lf_t, new_lf_t):
    b, h, p = cache_lf_t.shape
    return pl.pallas_call(
        _fprep_s_kernel,
        out_shape=(jax.ShapeDtypeStruct((b, h, p), F32), jax.ShapeDtypeStruct((b, h, LANES), F32)),
        grid=(1,),
        in_specs=[pl.BlockSpec((b, h, p), lambda i: (0, 0, 0)), pl.BlockSpec((b, h, LANES), lambda i: (0, 0, 0))],
        out_specs=(pl.BlockSpec((b, h, p), lambda i: (0, 0, 0)), pl.BlockSpec((b, h, LANES), lambda i: (0, 0, 0))),
        compiler_params=_cparams(("arbitrary",)),
        name="fprep_s",
    )(cache_lf_t, new_lf_t)


def _attn_s_kernel(q_ref, ck_ref, cv_ref, kn_ref, vn_ref, fc_ref, fn_ref, o_ref, qbd_ref, m_ref, l_ref, acc_ref,
                   *, tq):
    j = pl.program_id(1)
    nkv = pl.num_programs(1)
    rows = N_HEADS * tq
    bk = ck_ref.shape[1]

    @pl.when(j == 0)
    def _():
        q = q_ref[...]
        qrep = jnp.broadcast_to(q[None], (N_HEADS, tq, D_ATT)).reshape(rows, D_ATT)
        rh = lax.broadcasted_iota(I32, (rows, D_ATT), 0) // tq
        ch = lax.broadcasted_iota(I32, (rows, D_ATT), 1) // HEAD_DIM
        qbd_ref[...] = jnp.where(rh == ch, qrep, jnp.zeros_like(qrep))
        m_ref[...] = jnp.full_like(m_ref, -jnp.inf)
        l_ref[...] = jnp.zeros_like(l_ref)
        acc_ref[...] = jnp.zeros_like(acc_ref)

    def update(s, vt):
        m_prev = m_ref[...]
        m_new = jnp.maximum(m_prev, jnp.max(s, axis=1, keepdims=True))
        alpha = jnp.exp(m_prev - m_new)
        p = jnp.exp(s - m_new)
        l_ref[...] = alpha * l_ref[...] + jnp.sum(p, axis=1, keepdims=True)
        m_ref[...] = m_new
        acc_ref[...] = alpha * acc_ref[...] + _dot_nt(p.astype(BF16), vt)

    def bias(f, width):
        return jnp.broadcast_to(f[:, None, :], (N_HEADS, tq, width)).reshape(rows, width)

    s = _dot(qbd_ref[...], ck_ref[...].astype(BF16)) + bias(fc_ref[...], bk)
    update(s, cv_ref[...].astype(BF16))

    @pl.when(j == nkv - 1)
    def _():
        s2 = _dot(qbd_ref[...], kn_ref[...].astype(BF16)) + bias(fn_ref[...][:, 0:tq], tq)
        key = lax.broadcasted_iota(I32, (rows, tq), 1)
        qq = lax.broadcasted_iota(I32, (rows, tq), 0) % tq
        update(jnp.where(key <= qq, s2, -jnp.inf), vn_ref[...].astype(BF16))
        o = acc_ref[...] / l_ref[...]
        rh = lax.broadcasted_iota(I32, (rows, D_ATT), 0) // tq
        ch = lax.broadcasted_iota(I32, (rows, D_ATT), 1) // HEAD_DIM
        o = jnp.where(rh == ch, o, 0.0).reshape(N_HEADS, tq, D_ATT)
        o_ref[...] = jnp.sum(o, axis=0)


def _attn_s(q, cache_k, cache_v, k_new, v_new, fc, fn):
    b, tq, _ = q.shape
    p = cache_k.shape[2]
    bk = min(SAMPLE_KV_BLK, p)
    nkv = p // bk
    rows = N_HEADS * tq
    tok = lambda: pl.BlockSpec((None, tq, D_ATT), lambda bi, j: (bi, 0, 0))
    new = lambda: pl.BlockSpec((None, D_ATT, tq), lambda bi, j: (bi, 0, 0))
    return pl.pallas_call(
        functools.partial(_attn_s_kernel, tq=tq),
        out_shape=jax.ShapeDtypeStruct((b, tq, D_ATT), F32),
        grid=(b, nkv),
        in_specs=[tok(),
                  pl.BlockSpec((None, D_ATT, bk), lambda bi, j: (bi, 0, j)),
                  pl.BlockSpec((None, D_ATT, bk), lambda bi, j: (bi, 0, j)),
                  new(), new(),
                  pl.BlockSpec((None, N_HEADS, bk), lambda bi, j: (bi, 0, j)),
                  pl.BlockSpec((None, N_HEADS, LANES), lambda bi, j: (bi, 0, 0))],
        out_specs=tok(),
        scratch_shapes=[pltpu.VMEM((rows, D_ATT), BF16), pltpu.VMEM((rows, 1), F32),
                        pltpu.VMEM((rows, 1), F32), pltpu.VMEM((rows, D_ATT), F32)],
        compiler_params=_cparams(("arbitrary", "arbitrary")),
        name="attn_s",
    )(q, cache_k, cache_v, k_new, v_new, fc, fn)


def _outproj_kernel(x_ref, y4_ref, att_ref, gt_ref, sh_ref, sc_ref, gf_ref, wglu_ref, bglu_ref, wout_ref,
                    wr_ref, br_ref, cin_ref,
                    x1_ref, h2_ref, idx_ref, gate_ref, rank_ref, cnt_ref, carry_ref, *, att_transposed):
    i = pl.program_id(0)
    nb, tt, d = x_ref.shape
    tm = nb * tt

    @pl.when(i == 0)
    def _():
        carry_ref[...] = cin_ref[...]

    ys = jnp.concatenate([y4_ref[o] for o in range(N_OCT)], axis=-1)
    ysf = ys.astype(F32)
    glu = ysf * jax.nn.sigmoid(_dot(ys, wglu_ref[...]) + bglu_ref[...])
    att = att_ref[...]
    if att_transposed:
        att = att.T
    mix = _dot(glu.astype(BF16), wout_ref[0:D_SSM, :]) + _dot(att.astype(BF16), wout_ref[D_SSM:, :])
    x1 = x_ref[...] + gt_ref[...] * mix.reshape(nb, tt, d)
    x1_ref[...] = x1
    h2 = _norm_mod(x1, gf_ref[...], sc_ref[...], sh_ref[...]).reshape(tm, d)
    h2_ref[...] = h2

    logits = lax.dot_general(wr_ref[...], h2, (((1,), (1,)), ((), ())), preferred_element_type=F32,
                             precision=lax.Precision.HIGHEST) + br_ref[...]
    sub = lax.broadcasted_iota(I32, logits.shape, 0)
    work = logits
    vals, idxs = [], []
    for _ in range(TOP_K):
        mx = jnp.max(work, axis=0, keepdims=True)
        ix = jnp.min(jnp.where(work == mx, sub, N_EXPERTS), axis=0, keepdims=True)
        vals.append(mx)
        idxs.append(ix)
        work = jnp.where(sub == ix, -jnp.inf, work)
    ex = [jnp.exp(v - vals[0]) for v in vals]
    den = ex[0] + ex[1] + ex[2] + ex[3]
    mh = jnp.where(work == -jnp.inf, 1.0, 0.0)
    r = lax.broadcasted_iota(I32, (tm, tm), 0)
    c = lax.broadcasted_iota(I32, (tm, tm), 1)
    earlier = jnp.where(r < c, 1.0, 0.0).astype(BF16)
    carry = carry_ref[...]
    before = _dot(mh.astype(BF16), earlier) + carry[:, 0:1]
    carry_ref[...] = carry + jnp.sum(mh, axis=1, keepdims=True)
    for kk in range(TOP_K):
        idx_ref[kk:kk + 1, :] = idxs[kk]
        gate_ref[kk:kk + 1, :] = ex[kk] / den
        rk = jnp.sum(jnp.where(sub == idxs[kk], before, 0.0), axis=0, keepdims=True)
        rank_ref[kk:kk + 1, :] = rk.astype(I32)
    cnt_ref[...] = carry_ref[...]


def _outproj(x, y4, att, gt, sh, sc, g_ffn, w, counts_in, *, att_transposed, tm):
    b, t, d = x.shape
    n = b * t
    tm = min(tm, n)
    tt = min(t, tm)
    nb = tm // tt
    tpb = t // tt

    def xmap(i):
        return (i // tpb, i % tpb, 0) if nb == 1 else (i, 0, 0)

    def bmap(i):
        return (i // tpb, 0, 0) if nb == 1 else (i, 0, 0)

    full = lambda shape: pl.BlockSpec(shape, lambda i: (0,) * len(shape))
    if att_transposed:
        att_spec = pl.BlockSpec((None, D_ATT, tm), lambda i: (i // tpb, 0, i % tpb))
    else:
        att_spec = pl.BlockSpec((tm, D_ATT), lambda i: (i, 0))
    choice = lambda dt: (jax.ShapeDtypeStruct((TOP_K, n), dt), pl.BlockSpec((TOP_K, tm), lambda i: (0, i)))
    outs = [(jax.ShapeDtypeStruct((b, t, d), F32), pl.BlockSpec((nb, tt, d), xmap)),
            (jax.ShapeDtypeStruct((n, d), F32), pl.BlockSpec((tm, d), lambda i: (i, 0))),
            choice(I32), choice(F32), choice(I32),
            (jax.ShapeDtypeStruct((N_EXPERTS, LANES), F32), full((N_EXPERTS, LANES)))]
    return pl.pallas_call(
        functools.partial(_outproj_kernel, att_transposed=att_transposed),
        out_shape=tuple(o[0] for o in outs),
        grid=(n // tm,),
        in_specs=[pl.BlockSpec((nb, tt, d), xmap),
                  pl.BlockSpec((N_OCT, tm, LANES), lambda i: (0, i, 0)),
                  att_spec,
                  pl.BlockSpec((nb, 1, d), bmap), pl.BlockSpec((nb, 1, d), bmap), pl.BlockSpec((nb, 1, d), bmap),
                  full((1, d)), full((D_SSM, D_SSM)), full((1, D_SSM)), full((d, d)),
                  full((N_EXPERTS, d)), full((N_EXPERTS, 1)), full((N_EXPERTS, LANES))],
        out_specs=tuple(o[1] for o in outs),
        scratch_shapes=[pltpu.VMEM((N_EXPERTS, LANES), F32)],
        compiler_params=_cparams(("arbitrary",)),
        name="outproj_t" if att_transposed else "outproj",
    )(x, y4, att, gt, sh, sc, g_ffn, w["wglu"], w["bglu"], w["wout"], w["wr"], w["br"], counts_in)


def _dispatch_kernel(zst_ref, nu_ref, dest_ref, *rest, tm, tiles):
    h_refs = rest[:len(tiles)]
    xs_ref, zero_ref, zsem, sem = rest[len(tiles):]
    i = pl.program_id(0)
    n_blocks = xs_ref.shape[0] // MOE_BLK

    @pl.when(i == 0)
    def _():
        zero_ref[...] = jnp.zeros_like(zero_ref)

        def zero_copy(start):
            start = pl.multiple_of(start, MOE_BLK)
            return pltpu.make_async_copy(zero_ref, xs_ref.at[pl.ds(start, MOE_BLK)], zsem)

        for e in range(N_EXPERTS):
            @pl.when(zst_ref[e] >= 0)
            def _():
                zero_copy(jnp.maximum(zst_ref[e], 0)).start()

        def tail_start(j, carry):
            zero_copy(j * MOE_BLK).start()
            return carry

        def tail_wait(j, carry):
            zero_copy(j * MOE_BLK).wait()
            return carry

        lax.fori_loop(nu_ref[0], n_blocks, tail_start, 0)
        for e in range(N_EXPERTS):
            @pl.when(zst_ref[e] >= 0)
            def _():
                zero_copy(jnp.maximum(zst_ref[e], 0)).wait()
        lax.fori_loop(nu_ref[0], n_blocks, tail_wait, 0)

    def scatter_rows(h_ref):
        def issue(t, carry):
            for kk in range(TOP_K):
                dst = dest_ref[0, 0, t * TOP_K + kk]
                pltpu.make_async_copy(h_ref.at[pl.ds(t, 1)], xs_ref.at[pl.ds(dst, 1)], sem).start()
            return carry

        lax.fori_loop(0, tm, issue, 0, unroll=8)

    first_tile = 0
    for h_ref, n_tiles in zip(h_refs, tiles):
        pl.when((i >= first_tile) & (i < first_tile + n_tiles))(functools.partial(scatter_rows, h_ref))
        first_tile += n_tiles
    n_copied = tm * TOP_K
    pltpu.make_async_copy(xs_ref.at[pl.ds(0, n_copied)], xs_ref.at[pl.ds(0, n_copied)], sem).wait()


def _dispatch(h2s, dest, zstart, n_used, n_rows, tm):
    d = h2s[0].shape[1]
    tiles = tuple(h.shape[0] // tm for h in h2s)
    nt = sum(tiles)
    dest3 = dest.reshape(nt, 1, tm * TOP_K)
    in_specs = [pl.BlockSpec((1, 1, tm * TOP_K), lambda i, *_: (i, 0, 0), memory_space=pltpu.SMEM)]
    first_tile = 0
    for n_tiles in tiles:
        in_specs.append(pl.BlockSpec(
            (tm, d), lambda i, *_, f=first_tile, m=n_tiles: (jnp.clip(i - f, 0, m - 1), 0)))
        first_tile += n_tiles
    return pl.pallas_call(
        functools.partial(_dispatch_kernel, tm=tm, tiles=tiles),
        out_shape=jax.ShapeDtypeStruct((n_rows, d), F32),
        grid_spec=pltpu.PrefetchScalarGridSpec(
            num_scalar_prefetch=2, grid=(nt,),
            in_specs=in_specs,
            out_specs=pl.BlockSpec(memory_space=pl.ANY),
            scratch_shapes=[pltpu.VMEM((MOE_BLK, d), F32), pltpu.SemaphoreType.DMA, pltpu.SemaphoreType.DMA]),
        compiler_params=_cparams(("arbitrary",)),
        name="moe_dispatch",
    )(zstart, n_used, dest3, *h2s)


def _expert_kernel(be_ref, nu_ref, xs_ref, wup_ref, bup_ref, wdn_ref, bdn_ref, ys_ref, wup_bf, wdn_bf):
    i = pl.program_id(0)
    e = be_ref[i]
    prev = be_ref[jnp.maximum(i - 1, 0)]

    @pl.when(i < nu_ref[0])
    def _():
        @pl.when((i == 0) | (prev != e))
        def _():
            wup_bf[...] = wup_ref[...].astype(BF16)
            wdn_bf[...] = wdn_ref[...].astype(BF16)

        up = _dot(xs_ref[...].astype(BF16), wup_bf[...]) + bup_ref[...]
        x_glu = jnp.minimum(up[:, :D_FF], SWIGLU_LIMIT)
        x_lin = jnp.clip(up[:, D_FF:], -SWIGLU_LIMIT, SWIGLU_LIMIT)
        act = (x_lin + 1.0) * (x_glu * jax.nn.sigmoid(SWIGLU_ALPHA * x_glu))
        ys_ref[...] = _dot(act.astype(BF16), wdn_bf[...]) + bdn_ref[...]

    @pl.when(i >= nu_ref[0])
    def _():
        ys_ref[...] = jnp.zeros_like(ys_ref)


def _experts(xs, block_e, n_used, w_up, b_up, w_down, b_down):
    n_rows, d = xs.shape
    nblk = n_rows // MOE_BLK
    rowmap = lambda i, be, nu: (jnp.minimum(i, nu[0] - 1), 0)
    emap = lambda i, be, nu: (be[i], 0, 0)
    return pl.pallas_call(
        _expert_kernel,
        out_shape=jax.ShapeDtypeStruct((n_rows, d), F32),
        grid_spec=pltpu.PrefetchScalarGridSpec(
            num_scalar_prefetch=2, grid=(nblk,),
            in_specs=[pl.BlockSpec((MOE_BLK, d), rowmap),
                      pl.BlockSpec((None, d, 2 * D_FF), emap), pl.BlockSpec((None, 1, 2 * D_FF), emap),
                      pl.BlockSpec((None, D_FF, d), emap), pl.BlockSpec((None, 1, d), emap)],
            out_specs=pl.BlockSpec((MOE_BLK, d), lambda i, be, nu: (i, 0)),
            scratch_shapes=[pltpu.VMEM((d, 2 * D_FF), BF16), pltpu.VMEM((D_FF, d), BF16)]),
        compiler_params=_cparams(("arbitrary",)),
        name="moe_experts",
    )(block_e, n_used, xs, w_up, b_up, w_down, b_down)


def _combine_kernel(meta_ref, metan_ref, x1_ref, col_ref, gate_ref, gt_ref, gfin_ref, ys_ref, y_ref, ybuf_ref, sems,
                    *, tm):
    i = pl.program_id(0)
    nt = pl.num_programs(0)
    nb, tt, d = x1_ref.shape
    rows = ybuf_ref.shape[1]

    def run_copies(m_ref, slot, start):
        def per_expert(e, carry):
            a = m_ref[0, 0, e]
            n_chunks = m_ref[0, 0, N_EXPERTS + e]
            bo = m_ref[0, 0, 2 * N_EXPERTS + e]

            def per_chunk(c, carry2):
                src = ys_ref.at[pl.ds(pl.multiple_of(a + 8 * c, 8), 8)]
                dst = ybuf_ref.at[slot, pl.ds(pl.multiple_of(bo + 8 * c, 8), 8)]
                copy = pltpu.make_async_copy(src, dst, sems.at[slot])
                if start:
                    copy.start()
                else:
                    copy.wait()
                return carry2

            lax.fori_loop(0, n_chunks, per_chunk, 0)
            return carry

        lax.fori_loop(0, N_EXPERTS, per_expert, 0)

    @pl.when(i == 0)
    def _():
        ybuf_ref[...] = jnp.zeros_like(ybuf_ref)
        run_copies(meta_ref, 0, True)

    @pl.when(i + 1 < nt)
    def _():
        run_copies(metan_ref, (i + 1) % 2, True)

    slot = i % 2
    run_copies(meta_ref, slot, False)
    col = col_ref[...]
    gate = gate_ref[...]
    lane = lax.broadcasted_iota(I32, (tm, rows), 1)
    pick = jnp.where(lane == col[:, 0:1], gate[:, 0:1], 0.0)
    for kk in range(1, TOP_K):
        pick = pick + jnp.where(lane == col[:, kk:kk + 1], gate[:, kk:kk + 1], 0.0)
    moe = _dot(pick.astype(BF16), ybuf_ref[slot].astype(BF16))
    x2 = x1_ref[...] + gt_ref[...] * moe.reshape(nb, tt, d)
    ms = jnp.mean(x2 * x2, axis=-1, keepdims=True)
    y_ref[...] = x2 * lax.rsqrt(ms + NORM_EPS) * gfin_ref[...]


def _combine_plan(idx, dest, off, before0, tm):
    n = idx.shape[0]
    nt = n // tm
    experts = jnp.arange(N_EXPERTS, dtype=I32)
    chose = jnp.any(idx[:, :, None] == experts, axis=1).astype(I32)
    cnt = chose.reshape(nt, tm, N_EXPERTS).sum(axis=1)
    before = before0[None, :] + jnp.cumsum(cnt, axis=0) - cnt
    start = off[None, :] + before
    a8 = start // 8 * 8
    b8 = jnp.where(cnt > 0, (start + cnt + 7) // 8 * 8, a8)
    span = b8 - a8
    boff = jnp.cumsum(span, axis=1) - span
    delta = jnp.repeat(boff - a8, tm, axis=0)
    col = dest + jnp.take_along_axis(delta, idx, axis=1)
    meta = jnp.concatenate([a8, span // 8, boff, jnp.zeros_like(a8)], axis=1).astype(I32)
    return meta.reshape(nt, 1, 4 * N_EXPERTS), col.astype(I32)


def _combine(x1, gate, idx, dest, off, before0, ys, gt, g_final, tm):
    b, t, d = x1.shape
    n = b * t
    tm = min(tm, n)
    tt = min(t, tm)
    nb = tm // tt
    tpb = t // tt
    nt = n // tm
    rows = -(-(tm * TOP_K + 14 * N_EXPERTS) // 256) * 256
    meta, col = _combine_plan(idx, dest, off, before0, tm)

    def xmap(i):
        return (i // tpb, i % tpb, 0) if nb == 1 else (i, 0, 0)

    def bmap(i):
        return (i // tpb, 0, 0) if nb == 1 else (i, 0, 0)

    smem = lambda imap: pl.BlockSpec((1, 1, 4 * N_EXPERTS), imap, memory_space=pltpu.SMEM)
    return pl.pallas_call(
        functools.partial(_combine_kernel, tm=tm),
        out_shape=jax.ShapeDtypeStruct((b, t, d), F32),
        grid=(nt,),
        in_specs=[smem(lambda i: (i, 0, 0)), smem(lambda i: (jnp.minimum(i + 1, nt - 1), 0, 0)),
                  pl.BlockSpec((nb, tt, d), xmap),
                  pl.BlockSpec((tm, TOP_K), lambda i: (i, 0)),
                  pl.BlockSpec((tm, TOP_K), lambda i: (i, 0)),
                  pl.BlockSpec((nb, 1, d), bmap),
                  pl.BlockSpec((1, d), lambda i: (0, 0)),
                  pl.BlockSpec(memory_space=pl.ANY)],
        out_specs=pl.BlockSpec((nb, tt, d), xmap),
        scratch_shapes=[pltpu.VMEM((2, rows, d), F32), pltpu.SemaphoreType.DMA((2,))],
        compiler_params=_cparams(("arbitrary",)),
        name="moe_combine",
    )(meta, meta, x1, col, gate, gt, g_final, ys)


def _moe(streams, counts, g_final, w_up, b_up, w_down, b_down):
    n_total = sum(s["h2"].shape[0] for s in streams)
    cnt = counts[:, 0].astype(I32)
    padded = (cnt + MOE_BLK - 1) // MOE_BLK * MOE_BLK
    pad_end = jnp.cumsum(padded)
    off = (pad_end - padded).astype(I32)
    n_blocks = -(-n_total * TOP_K // MOE_BLK) + N_EXPERTS
    n_rows = n_blocks * MOE_BLK
    starts = jnp.arange(n_blocks, dtype=I32) * MOE_BLK
    block_e = jnp.minimum(jnp.sum((pad_end[None, :] <= starts[:, None]).astype(I32), axis=1), N_EXPERTS - 1)
    n_used = (pad_end[-1:] // MOE_BLK).astype(I32)
    zstart = jnp.where(padded > 0, pad_end - MOE_BLK, -1).astype(I32)
    experts = jnp.arange(N_EXPERTS, dtype=I32)
    idxs, dests = [], []
    for s in streams:
        idx = s["idx"].T
        idxs.append(idx)
        dests.append((jnp.sum(jnp.where(idx[:, :, None] == experts, off, 0), axis=-1) + s["rank"].T).astype(I32))
    tm = min([256] + [s["h2"].shape[0] for s in streams])
    xs = _dispatch([s["h2"] for s in streams], jnp.concatenate(dests, axis=0), zstart, n_used, n_rows, tm)
    ys = _experts(xs, block_e, n_used, w_up, b_up, w_down, b_down)
    return [_combine(s["x1"], s["gate"].T, idx, dest, off, s["before"], ys, s["gt"], g_final, 256)
            for s, idx, dest in zip(streams, idxs, dests)]


def _stream(x, mod, w, ssm_ops, s0, cache, params, counts_in):
    b, t, d = x.shape
    n = b * t
    sh_mix, sc_mix, gt_mix, sh_ffn, sc_ffn, gt_ffn = mod
    prompt = cache is None
    tm = 512
    u4, q, kt, vt, lft, vtb, k_tok, lf_tok = _inproj(x, sh_mix, sc_mix, params["g_mix"], w, prompt=prompt, tm=tm)
    n_chunks = t // SSM_CHUNK
    y4, s_fin = _ssm(u4.reshape(N_OCT, b * n_chunks, SSM_CHUNK * LANES), ssm_ops, s0, b, n_chunks)
    y4 = y4.reshape(N_OCT, n, LANES)
    if prompt:
        kaug = _fprep_p(lf_tok, k_tok, b, t)
        att = _attn_p(q, kaug, vtb, b, t)
    else:
        cache_k, cache_v, cache_lf = cache
        p = cache_k.shape[1]
        ck_t = jnp.transpose(cache_k, (0, 2, 3, 1)).reshape(b, D_ATT, p)
        cv_t = jnp.transpose(cache_v, (0, 2, 3, 1)).reshape(b, D_ATT, p)
        cl_t = jnp.transpose(cache_lf, (0, 2, 1))
        ln_t = jnp.pad(lft, ((0, 0), (0, 0), (0, LANES - t)))
        fc, fn = _fprep_s(cl_t, ln_t)
        att = _attn_s(q.reshape(b, t, D_ATT), ck_t, cv_t, kt, vt, fc, fn).reshape(n, D_ATT)
    x1, h2, idx, gate, rank, counts = _outproj(x, y4, att, gt_mix, sh_ffn, sc_ffn, params["g_ffn"], w, counts_in,
                                               att_transposed=prompt, tm=512)
    routed = {"x1": x1, "h2": h2, "idx": idx, "gate": gate, "rank": rank, "gt": gt_ffn,
              "before": counts_in[:, 0].astype(I32)}
    heads = lambda a: jnp.transpose(a.reshape(b, N_HEADS, HEAD_DIM, t), (0, 3, 1, 2))[None]
    return routed, counts, heads(kt), heads(vt), jnp.transpose(lft, (0, 2, 1))[None], s_fin


def _state_in(re, im):
    b = re.shape[0]
    s = jnp.concatenate([re.reshape(b, N_OCT, OCT_STATE // 2), im.reshape(b, N_OCT, OCT_STATE // 2)], axis=-1)
    return jnp.transpose(s, (1, 0, 2))


def _state_out(s):
    b = s.shape[1]
    s = jnp.transpose(s, (1, 0, 2))
    half = OCT_STATE // 2
    re = s[:, :, :half].reshape(1, b, N_SSM_GROUPS, SSM_STATE)
    im = s[:, :, half:].reshape(1, b, N_SSM_GROUPS, SSM_STATE)
    return re, im


def kernel(x_prompt, x_sample, c_prompt, c_sample, cache_k, cache_v, cache_logf, state_ssm_re, state_ssm_im, w_ada, b_ada, g_mix, w_in, b_forget, ssm_log_dt, ssm_a_re, ssm_a_im, ssm_b_re, ssm_b_im, ssm_c_re, ssm_c_im, ssm_d, w_glu, b_glu, w_out, g_ffn, w_router, b_router, w_up, b_up, w_down, b_down, g_final):
    assert w_ada.shape[0] == 1, "single-layer trunk"
    d = D_MODEL
    bp, tp, _ = x_prompt.shape
    bs, ts, _ = x_sample.shape
    n_c = bp + bs
    c_all = jnp.pad(jnp.concatenate([c_prompt, c_sample], axis=0), ((0, -n_c % 8), (0, 0)))
    mod = _ada(c_all, w_ada[0], b_ada[0].reshape(1, 6 * d))
    mod_p = [mod[:bp, j * d:(j + 1) * d].reshape(bp, 1, d) for j in range(6)]
    mod_s = [mod[bp:n_c, j * d:(j + 1) * d].reshape(bs, 1, d) for j in range(6)]
    wi = w_in[0]
    o1, o2, o3, o4 = D_SSM, D_SSM + D_ATT, D_SSM + 2 * D_ATT, D_SSM + 3 * D_ATT
    w_q = wi[:, o1:o2].astype(BF16)
    w_k = wi[:, o2:o3].astype(BF16)
    w_f = wi[:, o4:].astype(BF16)
    unused = jnp.zeros((8, LANES), BF16)
    base = {"wu": wi[:, :o1].astype(BF16), "wkt": w_k.T, "wvt": wi[:, o3:o4].astype(BF16).T, "wft": w_f.T,
            "bfc": b_forget[0].reshape(N_HEADS, 1), "bfr": b_forget[0].reshape(1, N_HEADS),
            "wglu": w_glu[0].astype(BF16), "bglu": b_glu[0].reshape(1, D_SSM), "wout": w_out[0].astype(BF16),
            "wr": w_router[0].T, "br": b_router[0].reshape(N_EXPERTS, 1)}
    w_p = dict(base, wq=w_q.T, wk=w_k, wf=w_f)
    w_s = dict(base, wq=w_q, wk=unused, wf=unused)
    params = {"g_mix": g_mix[0].reshape(1, d), "g_ffn": g_ffn[0].reshape(1, d), "g_final": g_final.reshape(1, d),
              "w_up": w_up[0], "b_up": b_up[0].reshape(N_EXPERTS, 1, 2 * D_FF),
              "w_down": w_down[0], "b_down": b_down[0].reshape(N_EXPERTS, 1, d)}
    ssm_ops = _ssm_operators(ssm_log_dt[0], ssm_a_re[0], ssm_a_im[0], ssm_b_re[0], ssm_b_im[0],
                             ssm_c_re[0], ssm_c_im[0], ssm_d[0])
    zero_state = jnp.zeros((N_OCT, bp, OCT_STATE), F32)
    counts0 = jnp.zeros((N_EXPERTS, LANES), F32)
    r_p, counts_p, k_p, v_p, f_p, s_p = _stream(x_prompt, mod_p, w_p, ssm_ops, zero_state, None, params, counts0)
    s0 = _state_in(state_ssm_re[0], state_ssm_im[0])
    r_s, counts, k_s, v_s, f_s, s_s = _stream(x_sample, mod_s, w_s, ssm_ops, s0,
                                              (cache_k[0], cache_v[0], cache_logf[0]), params, counts_p)
    y_p, y_s = _moe([r_p, r_s], counts, params["g_final"],
                    params["w_up"], params["b_up"], params["w_down"], params["b_down"])
    sre_p, sim_p = _state_out(s_p)
    sre_s, sim_s = _state_out(s_s)
    return (y_p, y_s, k_p, v_p, f_p, sre_p, sim_p, k_s, v_s, f_s, sre_s, sim_s)
```

```python
import functools
import math

import jax
import jax.numpy as jnp
from jax import lax
from jax.experimental import pallas as pl
from jax.experimental.pallas import tpu as pltpu

F32 = jnp.float32
BF16 = jnp.bfloat16
I32 = jnp.int32

D_MODEL = 1024
D_SSM = 512
SSM_GROUP = 16
N_SSM_GROUPS = 32
SSM_STATE = 64
D_ATT = 512
HEAD_DIM = 64
N_HEADS = 8
N_EXPERTS = 32
TOP_K = 4
D_FF = 1024
SWIGLU_LIMIT = 7.0
SWIGLU_ALPHA = 1.702
NORM_EPS = 1e-5

LANES = 128
N_OCT = D_SSM // LANES
OCT_STATE = 2 * (N_SSM_GROUPS // N_OCT) * SSM_STATE
SSM_CHUNK = 8
SSM_ROWS = 256
ATT_BLK = 256
ATT_PAIRS = 4
LOG2E = math.log2(math.e)
SAMPLE_KV_BLK = 1024
MOE_BLK = 256
VMEM_LIMIT = 52 * 1024 * 1024


def _cparams(sem, vmem=VMEM_LIMIT):
    return pltpu.CompilerParams(dimension_semantics=sem, vmem_limit_bytes=vmem)


def _dot(a, b):
    return jnp.dot(a, b, preferred_element_type=F32)


def _dot_nt(a, b):
    return lax.dot_general(a, b, (((1,), (1,)), ((), ())), preferred_element_type=F32)


def _dot_hi(a, b):
    return jnp.dot(a, b, preferred_element_type=F32, precision=lax.Precision.HIGHEST)


def _ada_kernel(c_ref, w_ref, b_ref, o_ref):
    c = c_ref[...]
    s = c * jax.nn.sigmoid(c)
    o_ref[...] = _dot_hi(s, w_ref[...]) + b_ref[...]


def _ada(c_all, w_ada, b_ada):
    m, d = c_all.shape
    n = w_ada.shape[1]
    return pl.pallas_call(
        _ada_kernel,
        out_shape=jax.ShapeDtypeStruct((m, n), F32),
        grid=(n // d,),
        in_specs=[pl.BlockSpec((m, d), lambda j: (0, 0)),
                  pl.BlockSpec((d, d), lambda j: (0, j)),
                  pl.BlockSpec((1, d), lambda j: (0, j))],
        out_specs=pl.BlockSpec((m, d), lambda j: (0, j)),
        compiler_params=_cparams(("arbitrary",)),
        name="ada",
    )(c_all, w_ada, b_ada)


def _norm_mod(x, g, scale, shift):
    ms = jnp.mean(x * x, axis=-1, keepdims=True)
    y = x * lax.rsqrt(ms + NORM_EPS) * g
    return y * (1.0 + scale) + shift


def _log_sigmoid(z):
    return jnp.minimum(z, 0.0) - jnp.log1p(jnp.exp(-jnp.abs(z)))


def _inproj_kernel(x_ref, sh_ref, sc_ref, g_ref, wu_ref, wq_ref, wkt_ref, wvt_ref, wft_ref, bfc_ref,
                   wk_ref, wf_ref, bfr_ref,
                   u4_ref, q_ref, kt_ref, vt_ref, lft_ref, vtb_ref, ktok_ref, lftok_ref, us_ref, *, prompt):
    nb, tt, d = x_ref.shape
    tm = nb * tt
    h = _norm_mod(x_ref[...], g_ref[...], sc_ref[...], sh_ref[...]).reshape(tm, d).astype(BF16)
    u = _dot(h, wu_ref[...])
    for o in range(N_OCT):
        us_ref[o] = u[:, o * LANES:(o + 1) * LANES]
    for s in range(SSM_CHUNK):
        for o in range(N_OCT):
            piece = us_ref[o, pl.ds(s, tm // SSM_CHUNK, stride=SSM_CHUNK), :]
            u4_ref[o, :, s * LANES:(s + 1) * LANES] = piece.astype(BF16)
    kt = _dot_nt(wkt_ref[...], h)
    vt = _dot_nt(wvt_ref[...], h)
    lft = _log_sigmoid(_dot_nt(wft_ref[...], h) + bfc_ref[...])
    scale = HEAD_DIM ** -0.5
    if prompt:
        kt_ref[...] = kt
        vt_ref[...] = vt
        lft_ref[...] = lft
        qt = (_dot_nt(wq_ref[...], h) * (scale * LOG2E)).astype(BF16)
        vtb = vt.astype(BF16)
        for j in range(tm // ATT_BLK):
            q_ref[j] = qt[:, j * ATT_BLK:(j + 1) * ATT_BLK]
            vtb_ref[j] = vtb[:, j * ATT_BLK:(j + 1) * ATT_BLK]
        ktok_ref[...] = _dot(h, wk_ref[...]).astype(BF16)
        lftok_ref[...] = _log_sigmoid(_dot(h, wf_ref[...]) + bfr_ref[...])
    else:
        for bl in range(nb):
            kt_ref[bl] = kt[:, bl * tt:(bl + 1) * tt]
            vt_ref[bl] = vt[:, bl * tt:(bl + 1) * tt]
            lft_ref[bl] = lft[:, bl * tt:(bl + 1) * tt]
        q_ref[...] = (_dot(h, wq_ref[...]) * scale).astype(BF16)
        vtb_ref[...] = jnp.zeros_like(vtb_ref)
        ktok_ref[...] = jnp.zeros_like(ktok_ref)
        lftok_ref[...] = jnp.zeros_like(lftok_ref)


def _inproj(x, shift, scale, g_mix, w, *, prompt, tm):
    b, t, d = x.shape
    n = b * t
    tm = min(tm, n)
    tt = min(t, tm)
    nb = tm // tt
    tpb = t // tt
    assert (nb == 1) == prompt
    grid = (n // tm,)

    def xmap(i):
        return (i // tpb, i % tpb, 0) if nb == 1 else (i, 0, 0)

    def bmap(i):
        return (i // tpb, 0, 0) if nb == 1 else (i, 0, 0)

    def tmap(i):
        return (i // tpb, 0, i % tpb) if nb == 1 else (i, 0, 0)

    full = lambda a: pl.BlockSpec(a.shape, lambda i: (0,) * a.ndim)
    names = ("wu", "wq", "wkt", "wvt", "wft", "bfc", "wk", "wf", "bfr")
    in_specs = [pl.BlockSpec((nb, tt, d), xmap),
                pl.BlockSpec((nb, 1, d), bmap), pl.BlockSpec((nb, 1, d), bmap),
                pl.BlockSpec((1, d), lambda i: (0, 0))] + [full(w[k]) for k in names]
    tblk = lambda rows: pl.BlockSpec((None if nb == 1 else nb, rows, tt), tmap)
    dummy = (jax.ShapeDtypeStruct((8, LANES), F32), pl.BlockSpec((8, LANES), lambda i: (0, 0)))
    if prompt:
        nblk = n // ATT_BLK
        blocked = (jax.ShapeDtypeStruct((nblk, D_ATT, ATT_BLK), BF16),
                   pl.BlockSpec((tm // ATT_BLK, D_ATT, ATT_BLK), lambda i: (i, 0, 0)))
        q_out, vtb_out = blocked, blocked
        ktok_out = (jax.ShapeDtypeStruct((n, D_ATT), BF16), pl.BlockSpec((tm, D_ATT), lambda i: (i, 0)))
        lftok_out = (jax.ShapeDtypeStruct((n, N_HEADS), F32), pl.BlockSpec((tm, N_HEADS), lambda i: (i, 0)))
    else:
        q_out = (jax.ShapeDtypeStruct((n, D_ATT), BF16), pl.BlockSpec((tm, D_ATT), lambda i: (i, 0)))
        vtb_out = ktok_out = lftok_out = dummy
    kdim = SSM_CHUNK * LANES
    outs = [(jax.ShapeDtypeStruct((N_OCT, n // SSM_CHUNK, kdim), BF16),
             pl.BlockSpec((N_OCT, tm // SSM_CHUNK, kdim), lambda i: (0, i, 0))),
            q_out,
            (jax.ShapeDtypeStruct((b, D_ATT, t), F32), tblk(D_ATT)),
            (jax.ShapeDtypeStruct((b, D_ATT, t), F32), tblk(D_ATT)),
            (jax.ShapeDtypeStruct((b, N_HEADS, t), F32), tblk(N_HEADS)),
            vtb_out, ktok_out, lftok_out]
    return pl.pallas_call(
        functools.partial(_inproj_kernel, prompt=prompt),
        out_shape=tuple(o[0] for o in outs), grid=grid, in_specs=in_specs, out_specs=tuple(o[1] for o in outs),
        scratch_shapes=[pltpu.VMEM((N_OCT, tm, LANES), F32)],
        compiler_params=_cparams(("arbitrary",)),
        name="inproj_p" if prompt else "inproj_s",
    )(x, shift, scale, g_mix, *[w[k] for k in names])


def _ssm_operators(log_dt, a_re, a_im, b_re, b_im, c_re, c_im, d_skip):
    L, G, P, C = SSM_CHUNK, N_SSM_GROUPS, SSM_STATE, SSM_GROUP
    q = G // N_OCT
    lam = lax.complex(a_re, a_im)
    dt = jnp.exp(log_dt)[:, None]
    lam_dt = lam * dt
    lam_bar = jnp.exp(lam_dt)
    b_bar = ((lam_bar - 1.0) / lam)[:, :, None] * lax.complex(b_re, b_im)
    c_mat = lax.complex(c_re, c_im)
    pw = jnp.exp(lam_dt[None] * jnp.arange(L + 1, dtype=F32)[:, None, None])
    kern = jnp.real(jnp.einsum('gcp,dgp,gpk->dgck', c_mat, pw[:L], b_bar))
    ksm = jnp.transpose(kern, (1, 0, 3, 2)).reshape(N_OCT, q, L, C, C)
    ksm = jnp.transpose(ksm, (0, 2, 1, 3, 4)).reshape(N_OCT, L, q * C, C)
    pwr = pw[L - 1 - jnp.arange(L)]
    wb = jnp.einsum('jgp,gpk->jgkp', pwr, b_bar)
    wsm = jnp.stack([jnp.real(wb), jnp.imag(wb)], axis=0).reshape(2, L, N_OCT, q * C, P)
    wsm = jnp.transpose(wsm, (2, 1, 0, 3, 4))
    cl = c_mat[None] * pw[1:L + 1][:, :, None, :]
    vsm = jnp.stack([jnp.real(cl), -jnp.imag(cl)], axis=0).reshape(2, L, N_OCT, q, C, P)
    vsm = jnp.transpose(vsm, (2, 1, 0, 3, 5, 4)).reshape(N_OCT, L, OCT_STATE, C)
    tmat, wend, win = _ssm_prep(ksm, wsm, vsm, d_skip.reshape(N_OCT, 1, LANES))
    dec = pw[L].reshape(N_OCT, q * P)
    decay = jnp.concatenate([jnp.real(dec), jnp.imag(dec)], axis=-1).reshape(N_OCT, 1, OCT_STATE)
    return tmat, wend, win, decay.astype(F32)


def _ssm_prep_kernel(ksm_ref, wsm_ref, vsm_ref, d_ref, tmat_ref, wend_ref, win_ref):
    L, C, P = SSM_CHUNK, SSM_GROUP, SSM_STATE
    half = OCT_STATE // 2

    def spread(period, width):
        r = lax.broadcasted_iota(I32, (period, width), 0)
        c = lax.broadcasted_iota(I32, (period, width), 1)
        return jnp.where(c % period == r, 1.0, 0.0)

    def same_group(shape, row_div, row_mod, lane_div):
        r = lax.broadcasted_iota(I32, shape, 0)
        c = lax.broadcasted_iota(I32, shape, 1)
        return (r % row_mod) // row_div == c // lane_div

    e_c = spread(C, LANES)
    e_p = spread(P, half)
    grp = same_group((LANES, LANES), C, LANES, C)
    rr = lax.broadcasted_iota(I32, (LANES, LANES), 0)
    cc = lax.broadcasted_iota(I32, (LANES, LANES), 1)
    lag = []
    for d in range(L):
        blk = jnp.where(grp, _dot_hi(ksm_ref[d], e_c), 0.0)
        if d == 0:
            blk = blk + jnp.where(rr == cc, jnp.broadcast_to(d_ref[...], (LANES, LANES)), 0.0)
        lag.append(blk.astype(BF16))
    zero = jnp.zeros((LANES, LANES), BF16)
    for j in range(L):
        for t in range(L):
            tmat_ref[j * LANES:(j + 1) * LANES, t * LANES:(t + 1) * LANES] = lag[t - j] if t >= j else zero
    grp_e = same_group((LANES, half), C, LANES, P)
    for j in range(L):
        for part in range(2):
            blk = jnp.where(grp_e, _dot_hi(wsm_ref[j, part], e_p), 0.0)
            wend_ref[j * LANES:(j + 1) * LANES, part * half:(part + 1) * half] = blk.astype(BF16)
    grp_i = same_group((OCT_STATE, LANES), P, half, C)
    for t in range(L):
        blk = jnp.where(grp_i, _dot_hi(vsm_ref[t], e_c), 0.0)
        win_ref[:, t * LANES:(t + 1) * LANES] = blk.astype(BF16)


def _ssm_prep(ksm, wsm, vsm, dvec):
    kdim = SSM_CHUNK * LANES
    blk = lambda a: pl.BlockSpec((None,) + a.shape[1:], lambda o: (o,) + (0,) * (a.ndim - 1))
    out = lambda r, c: (jax.ShapeDtypeStruct((N_OCT, r, c), BF16), pl.BlockSpec((None, r, c), lambda o: (o, 0, 0)))
    outs = [out(kdim, kdim), out(kdim, OCT_STATE), out(OCT_STATE, kdim)]
    return pl.pallas_call(
        _ssm_prep_kernel,
        out_shape=tuple(o[0] for o in outs),
        grid=(N_OCT,),
        in_specs=[blk(ksm), blk(wsm), blk(vsm), blk(dvec)],
        out_specs=tuple(o[1] for o in outs),
        compiler_params=_cparams(("arbitrary",)),
        name="ssm_prep",
    )(ksm, wsm, vsm, dvec)


def _ssm_kernel(u_ref, tmat_ref, wend_ref, win_ref, dec_ref, s0_ref, y_ref, sfin_ref, e_ref, sp_ref, st_ref,
                *, nb, cpt):
    b = pl.program_id(1)
    i = pl.program_id(2)
    half = OCT_STATE // 2
    whole = nb == s0_ref.shape[0]

    @pl.when(i == 0)
    def _():
        st_ref[...] = s0_ref[...] if whole else s0_ref[pl.ds(b, 1), :]

    u = u_ref[...]
    e_ref[...] = _dot(u, wend_ref[...])
    dec = dec_ref[...]
    ar, ai = dec[:, :half], dec[:, half:]

    def per_batch(bb, carry):
        def per_chunk(r, st):
            row = bb * cpt + r
            sp_ref[pl.ds(row, 1), :] = st
            e = e_ref[pl.ds(row, 1), :]
            re, im = st[:, :half], st[:, half:]
            nre = ar * re - ai * im + e[:, :half]
            nim = ar * im + ai * re + e[:, half:]
            return jnp.concatenate([nre, nim], axis=-1)

        st = lax.fori_loop(0, cpt, per_chunk, st_ref[pl.ds(bb, 1), :])
        st_ref[pl.ds(bb, 1), :] = st
        return carry

    lax.fori_loop(0, nb, per_batch, 0)
    y = _dot(u, tmat_ref[...]) + _dot(sp_ref[...].astype(BF16), win_ref[...])
    y_ref[...] = jax.nn.gelu(y).astype(BF16)
    if whole:
        sfin_ref[...] = st_ref[...]
    else:
        sfin_ref[pl.ds(b, 1), :] = st_ref[...]


def _ssm(u4, ops, s0, batch, n_chunks):
    tmat, wend, win, decay = ops
    rows = batch * n_chunks
    kdim = SSM_CHUNK * LANES
    if n_chunks % SSM_ROWS == 0:
        nb, cpt = 1, SSM_ROWS
    else:
        nb, cpt = batch, n_chunks
    r = nb * cpt
    tiles = n_chunks // cpt
    grid = (N_OCT, batch // nb, tiles)
    wspec = lambda shape: pl.BlockSpec((None,) + shape, lambda o, b, i: (o, 0, 0))
    return pl.pallas_call(
        functools.partial(_ssm_kernel, nb=nb, cpt=cpt),
        out_shape=(jax.ShapeDtypeStruct((N_OCT, rows, kdim), BF16),
                   jax.ShapeDtypeStruct((N_OCT, batch, OCT_STATE), F32)),
        grid=grid,
        in_specs=[pl.BlockSpec((None, r, kdim), lambda o, b, i: (o, b * tiles + i, 0)),
                  wspec((kdim, kdim)), wspec((kdim, OCT_STATE)), wspec((OCT_STATE, kdim)), wspec((1, OCT_STATE)),
                  pl.BlockSpec((None, batch, OCT_STATE), lambda o, b, i: (o, 0, 0))],
        out_specs=(pl.BlockSpec((None, r, kdim), lambda o, b, i: (o, b * tiles + i, 0)),
                   pl.BlockSpec((None, batch, OCT_STATE), lambda o, b, i: (o, 0, 0))),
        scratch_shapes=[pltpu.VMEM((r, OCT_STATE), F32), pltpu.VMEM((r, OCT_STATE), F32),
                        pltpu.VMEM((nb, OCT_STATE), F32)],
        compiler_params=_cparams(("arbitrary", "arbitrary", "arbitrary")),
        name="ssm",
    )(u4, tmat, wend, win, decay, s0)


def _bias_placement():
    import numpy as np
    pm = np.zeros((3, N_HEADS, N_HEADS // 2 * LANES), np.float32)
    for piece in range(3):
        for h in range(N_HEADS):
            pm[piece, h, LANES * (h // 2) + 3 * (h % 2) + piece] = 1.0
    return jnp.asarray(pm)


def _fprep_p_kernel(lf_ref, k_ref, pm_ref, ka_ref, carry_ref):
    i = pl.program_id(1)

    @pl.when(i == 0)
    def _():
        carry_ref[...] = jnp.zeros_like(carry_ref)

    tile = lf_ref.shape[0]
    r = lax.broadcasted_iota(I32, (tile, tile), 0)
    c = lax.broadcasted_iota(I32, (tile, tile), 1)
    tri = (c <= r).astype(F32)
    cum = _dot_hi(tri, lf_ref[...]) + carry_ref[0:1, 0:N_HEADS]
    carry_ref[0:1, 0:N_HEADS] = cum[tile - 1:tile, :]
    nf = cum * (-LOG2E)
    hi = nf.astype(BF16).astype(F32)
    r1 = nf - hi
    mid = r1.astype(BF16).astype(F32)
    lo = (r1 - mid).astype(BF16).astype(F32)
    slab = _dot(hi, pm_ref[0]) + _dot(mid, pm_ref[1]) + _dot(lo, pm_ref[2])
    k = k_ref[...]
    for p in range(N_HEADS // 2):
        ka_ref[:, 2 * p * LANES:(2 * p + 1) * LANES] = k[:, p * LANES:(p + 1) * LANES].astype(BF16)
        ka_ref[:, (2 * p + 1) * LANES:(2 * p + 2) * LANES] = slab[:, p * LANES:(p + 1) * LANES].astype(BF16)


def _fprep_p(logf, k, batch, t):
    tile = ATT_BLK
    tiles = t // tile
    pm = _bias_placement()
    kaug = pl.pallas_call(
        _fprep_p_kernel,
        out_shape=jax.ShapeDtypeStruct((batch * t, 2 * D_ATT), BF16),
        grid=(batch, tiles),
        in_specs=[pl.BlockSpec((tile, N_HEADS), lambda b, i: (b * tiles + i, 0)),
                  pl.BlockSpec((tile, D_ATT), lambda b, i: (b * tiles + i, 0)),
                  pl.BlockSpec(pm.shape, lambda b, i: (0, 0, 0))],
        out_specs=pl.BlockSpec((tile, 2 * D_ATT), lambda b, i: (b * tiles + i, 0)),
        scratch_shapes=[pltpu.VMEM((8, LANES), F32)],
        compiler_params=_cparams(("arbitrary", "arbitrary")),
        name="fprep_p",
    )(logf, k, pm)
    return kaug.reshape(batch * tiles, tile, 2 * D_ATT)


def _attn_p_kernel(qt_ref, ka_ref, vt_ref, o_ref, qbd_ref, acc_ref, sa_ref, sb_ref):
    i = pl.program_id(2)
    bq = ATT_BLK
    hd = HEAD_DIM
    r = lax.broadcasted_iota(I32, (LANES, 2 * bq), 0)
    c = lax.broadcasted_iota(I32, (LANES, 2 * bq), 1)
    ones = jnp.where(((r < 3) & (c < bq)) | ((r >= 3) & (r < 6) & (c >= bq)), 1.0, 0.0).astype(BF16)
    zero = jnp.zeros((hd, bq), BF16)
    for pp in range(ATT_PAIRS):
        qb = qt_ref[pp * LANES:(pp + 1) * LANES, :]
        qbd_ref[pp, 0:hd, 0:bq] = qb[0:hd]
        qbd_ref[pp, 0:hd, bq:2 * bq] = zero
        qbd_ref[pp, hd:2 * hd, 0:bq] = zero
        qbd_ref[pp, hd:2 * hd, bq:2 * bq] = qb[hd:2 * hd]
        qbd_ref[pp, 2 * hd:2 * hd + LANES, :] = ones
    acc_ref[...] = jnp.zeros_like(acc_ref)

    nq = ka_ref.shape[0]

    def scores(s_ref, j):
        for pp in range(ATT_PAIRS):
            s_ref[pp] = _dot(ka_ref[j, :, pp * 2 * LANES:(pp + 1) * 2 * LANES], qbd_ref[pp])

    def attend(s_ref, j, carry, masked):
        out = []
        for pp in range(ATT_PAIRS):
            m_prev, l_prev = carry[2 * pp], carry[2 * pp + 1]
            s = s_ref[pp]
            if masked:
                key = j * bq + lax.broadcasted_iota(I32, s.shape, 0)
                qq = lax.broadcasted_iota(I32, s.shape, 1)
                qq = i * bq + jnp.where(qq >= bq, qq - bq, qq)
                s = jnp.where(key <= qq, s, -jnp.inf)
            m_new = jnp.maximum(m_prev, jnp.max(s, axis=0, keepdims=True))
            alpha = jnp.exp2(m_prev - m_new)
            p = jnp.exp2(s - m_new)
            l_new = alpha * l_prev + jnp.sum(p, axis=0, keepdims=True)
            pb = p.astype(BF16)
            vb = vt_ref[jnp.minimum(j, nq - 1), pp * LANES:(pp + 1) * LANES, :]
            r0 = pp * LANES
            acc_ref[r0:r0 + hd] = alpha[:, 0:bq] * acc_ref[r0:r0 + hd] + _dot(vb[0:hd], pb[:, 0:bq])
            acc_ref[r0 + hd:r0 + 2 * hd] = (alpha[:, bq:2 * bq] * acc_ref[r0 + hd:r0 + 2 * hd]
                                            + _dot(vb[hd:2 * hd], pb[:, bq:2 * bq]))
            out.extend((m_new, l_new))
        return tuple(out)

    def double_step(m, carry):
        j = 2 * m
        scores(sb_ref, j + 1)
        carry = attend(sa_ref, j, carry, False)
        scores(sa_ref, j + 2)
        return attend(sb_ref, j + 1, carry, False)

    init = (jnp.full((1, 2 * bq), -jnp.inf, F32), jnp.zeros((1, 2 * bq), F32)) * ATT_PAIRS
    scores(sa_ref, 0)
    carry = lax.fori_loop(0, i // 2, double_step, init)
    j = 2 * (i // 2)
    scores(sb_ref, jnp.minimum(j + 1, nq - 1))
    carry = attend(sa_ref, j, carry, True)
    carry = attend(sb_ref, j + 1, carry, True)
    for pp in range(ATT_PAIRS):
        l = carry[2 * pp + 1]
        r0 = pp * LANES
        o_ref[r0:r0 + hd] = acc_ref[r0:r0 + hd] / l[:, 0:bq]
        o_ref[r0 + hd:r0 + 2 * hd] = acc_ref[r0 + hd:r0 + 2 * hd] / l[:, bq:2 * bq]


def _attn_p(qt, kaug, vt, batch, t):
    blk = ATT_BLK
    nq = t // blk
    ngrp = N_HEADS // 2 // ATT_PAIRS
    rows = ATT_PAIRS * LANES
    qt4 = qt.reshape(batch, nq, D_ATT, blk)
    vt4 = vt.reshape(batch, nq, D_ATT, blk)
    ka4 = kaug.reshape(batch, nq, blk, 2 * D_ATT)
    return pl.pallas_call(
        _attn_p_kernel,
        out_shape=jax.ShapeDtypeStruct((batch, D_ATT, t), F32),
        grid=(batch, ngrp, nq),
        in_specs=[pl.BlockSpec((None, None, rows, blk), lambda b, p, i: (b, i, p, 0)),
                  pl.BlockSpec((None, nq, blk, 2 * rows), lambda b, p, i: (b, 0, 0, p), pipeline_mode=pl.Buffered(1)),
                  pl.BlockSpec((None, nq, rows, blk), lambda b, p, i: (b, 0, p, 0), pipeline_mode=pl.Buffered(1))],
        out_specs=pl.BlockSpec((None, rows, blk), lambda b, p, i: (b, p, i)),
        scratch_shapes=[pltpu.VMEM((ATT_PAIRS, 2 * LANES, 2 * blk), BF16), pltpu.VMEM((rows, blk), F32),
                        pltpu.VMEM((ATT_PAIRS, blk, 2 * blk), F32), pltpu.VMEM((ATT_PAIRS, blk, 2 * blk), F32)],
        compiler_params=_cparams(("arbitrary", "arbitrary", "arbitrary")),
        name="attn_p",
    )(qt4, ka4, vt4)


def _lane_cumsum(x):
    n = x.shape[-1]
    lane = lax.broadcasted_iota(I32, x.shape, x.ndim - 1)
    s = 1
    while s < n:
        x = x + jnp.where(lane >= s, pltpu.roll(x, s, x.ndim - 1), 0.0)
        s *= 2
    return x


def _fprep_s_kernel(cl_ref, ln_ref, fc_ref, fn_ref):
    b, h, p = cl_ref.shape
    cum = _lane_cumsum(cl_ref[...].reshape(b * h, p))
    fc_ref[...] = (-cum).reshape(b, h, p)
    total = cum[:, p - 1:p]
    cn = _lane_cumsum(ln_ref[...].reshape(b * h, LANES))
    fn_ref[...] = (-(total + cn)).reshape(b, h, LANES)


def _fprep_s(cache_lf_t, new_lf_t):
    b, h, p = cache_lf_t.shape
    return pl.pallas_call(
        _fprep_s_kernel,
        out_shape=(jax.ShapeDtypeStruct((b, h, p), F32), jax.ShapeDtypeStruct((b, h, LANES), F32)),
        grid=(1,),
        in_specs=[pl.BlockSpec((b, h, p), lambda i: (0, 0, 0)), pl.BlockSpec((b, h, LANES), lambda i: (0, 0, 0))],
        out_specs=(pl.BlockSpec((b, h, p), lambda i: (0, 0, 0)), pl.BlockSpec((b, h, LANES), lambda i: (0, 0, 0))),
        compiler_params=_cparams(("arbitrary",)),
        name="fprep_s",
    )(cache_lf_t, new_lf_t)


def _attn_s_kernel(q_ref, ck_ref, cv_ref, kn_ref, vn_ref, fc_ref, fn_ref, o_ref, qbd_ref, m_ref, l_ref, acc_ref,
                   *, tq):
    j = pl.program_id(1)
    nkv = pl.num_programs(1)
    rows = N_HEADS * tq
    bk = ck_ref.shape[1]

    @pl.when(j == 0)
    def _():
        q = q_ref[...]
        qrep = jnp.broadcast_to(q[None], (N_HEADS, tq, D_ATT)).reshape(rows, D_ATT)
        rh = lax.broadcasted_iota(I32, (rows, D_ATT), 0) // tq
        ch = lax.broadcasted_iota(I32, (rows, D_ATT), 1) // HEAD_DIM
        qbd_ref[...] = jnp.where(rh == ch, qrep, jnp.zeros_like(qrep))
        m_ref[...] = jnp.full_like(m_ref, -jnp.inf)
        l_ref[...] = jnp.zeros_like(l_ref)
        acc_ref[...] = jnp.zeros_like(acc_ref)

    def update(s, vt):
        m_prev = m_ref[...]
        m_new = jnp.maximum(m_prev, jnp.max(s, axis=1, keepdims=True))
        alpha = jnp.exp(m_prev - m_new)
        p = jnp.exp(s - m_new)
        l_ref[...] = alpha * l_ref[...] + jnp.sum(p, axis=1, keepdims=True)
        m_ref[...] = m_new
        acc_ref[...] = alpha * acc_ref[...] + _dot_nt(p.astype(BF16), vt)

    def bias(f, width):
        return jnp.broadcast_to(f[:, None, :], (N_HEADS, tq, width)).reshape(rows, width)

    s = _dot(qbd_ref[...], ck_ref[...].astype(BF16)) + bias(fc_ref[...], bk)
    update(s, cv_ref[...].astype(BF16))

    @pl.when(j == nkv - 1)
    def _():
        s2 = _dot(qbd_ref[...], kn_ref[...].astype(BF16)) + bias(fn_ref[...][:, 0:tq], tq)
        key = lax.broadcasted_iota(I32, (rows, tq), 1)
        qq = lax.broadcasted_iota(I32, (rows, tq), 0) % tq
        update(jnp.where(key <= qq, s2, -jnp.inf), vn_ref[...].astype(BF16))
        o = acc_ref[...] / l_ref[...]
        rh = lax.broadcasted_iota(I32, (rows, D_ATT), 0) // tq
        ch = lax.broadcasted_iota(I32, (rows, D_ATT), 1) // HEAD_DIM
        o = jnp.where(rh == ch, o, 0.0).reshape(N_HEADS, tq, D_ATT)
        o_ref[...] = jnp.sum(o, axis=0)


def _attn_s(q, cache_k, cache_v, k_new, v_new, fc, fn):
    b, tq, _ = q.shape
    p = cache_k.shape[2]
    bk = min(SAMPLE_KV_BLK, p)
    nkv = p // bk
    rows = N_HEADS * tq
    tok = lambda: pl.BlockSpec((None, tq, D_ATT), lambda bi, j: (bi, 0, 0))
    new = lambda: pl.BlockSpec((None, D_ATT, tq), lambda bi, j: (bi, 0, 0))
    return pl.pallas_call(
        functools.partial(_attn_s_kernel, tq=tq),
        out_shape=jax.ShapeDtypeStruct((b, tq, D_ATT), F32),
        grid=(b, nkv),
        in_specs=[tok(),
                  pl.BlockSpec((None, D_ATT, bk), lambda bi, j: (bi, 0, j)),
                  pl.BlockSpec((None, D_ATT, bk), lambda bi, j: (bi, 0, j)),
                  new(), new(),
                  pl.BlockSpec((None, N_HEADS, bk), lambda bi, j: (bi, 0, j)),
                  pl.BlockSpec((None, N_HEADS, LANES), lambda bi, j: (bi, 0, 0))],
        out_specs=tok(),
        scratch_shapes=[pltpu.VMEM((rows, D_ATT), BF16), pltpu.VMEM((rows, 1), F32),
                        pltpu.VMEM((rows, 1), F32), pltpu.VMEM((rows, D_ATT), F32)],
        compiler_params=_cparams(("arbitrary", "arbitrary")),
        name="attn_s",
    )(q, cache_k, cache_v, k_new, v_new, fc, fn)


def _outproj_kernel(x_ref, y4_ref, att_ref, gt_ref, sh_ref, sc_ref, gf_ref, wglu_ref, bglu_ref, wout_ref,
                    wr_ref, br_ref, cin_ref,
                    x1_ref, h2_ref, idx_ref, gate_ref, rank_ref, cnt_ref, carry_ref, ys_ref, *, att_transposed):
    i = pl.program_id(0)
    nb, tt, d = x_ref.shape
    tm = nb * tt

    @pl.when(i == 0)
    def _():
        carry_ref[...] = cin_ref[...]

    for s in range(SSM_CHUNK):
        for o in range(N_OCT):
            ys_ref[o, pl.ds(s, tm // SSM_CHUNK, stride=SSM_CHUNK), :] = (
                y4_ref[o, :, s * LANES:(s + 1) * LANES].astype(F32))
    ysf = jnp.concatenate([ys_ref[o] for o in range(N_OCT)], axis=-1)
    glu = ysf * jax.nn.sigmoid(_dot(ysf.astype(BF16), wglu_ref[...]) + bglu_ref[...])
    att = att_ref[...]
    if att_transposed:
        att = att.T
    mix = _dot(glu.astype(BF16), wout_ref[0:D_SSM, :]) + _dot(att.astype(BF16), wout_ref[D_SSM:, :])
    x1 = x_ref[...] + gt_ref[...] * mix.reshape(nb, tt, d)
    x1_ref[...] = x1
    h2 = _norm_mod(x1, gf_ref[...], sc_ref[...], sh_ref[...]).reshape(tm, d)
    h2_ref[...] = h2

    logits = lax.dot_general(wr_ref[...], h2, (((1,), (1,)), ((), ())), preferred_element_type=F32,
                             precision=lax.Precision.HIGHEST) + br_ref[...]
    sub = lax.broadcasted_iota(I32, logits.shape, 0)
    work = logits
    vals, idxs = [], []
    for _ in range(TOP_K):
        mx = jnp.max(work, axis=0, keepdims=True)
        ix = jnp.min(jnp.where(work == mx, sub, N_EXPERTS), axis=0, keepdims=True)
        vals.append(mx)
        idxs.append(ix)
        work = jnp.where(sub == ix, -jnp.inf, work)
    ex = [jnp.exp(v - vals[0]) for v in vals]
    den = ex[0] + ex[1] + ex[2] + ex[3]
    mh = jnp.where(work == -jnp.inf, 1.0, 0.0)
    r = lax.broadcasted_iota(I32, (tm, tm), 0)
    c = lax.broadcasted_iota(I32, (tm, tm), 1)
    earlier = jnp.where(r < c, 1.0, 0.0).astype(BF16)
    carry = carry_ref[...]
    before = _dot(mh.astype(BF16), earlier) + carry[:, 0:1]
    carry_ref[...] = carry + jnp.sum(mh, axis=1, keepdims=True)
    for kk in range(TOP_K):
        idx_ref[kk:kk + 1, :] = idxs[kk]
        gate_ref[kk:kk + 1, :] = ex[kk] / den
        rk = jnp.sum(jnp.where(sub == idxs[kk], before, 0.0), axis=0, keepdims=True)
        rank_ref[kk:kk + 1, :] = rk.astype(I32)
    cnt_ref[...] = carry_ref[...]


def _outproj(x, y4, att, gt, sh, sc, g_ffn, w, counts_in, *, att_transposed, tm):
    b, t, d = x.shape
    n = b * t
    tm = min(tm, n)
    tt = min(t, tm)
    nb = tm // tt
    tpb = t // tt

    def xmap(i):
        return (i // tpb, i % tpb, 0) if nb == 1 else (i, 0, 0)

    def bmap(i):
        return (i // tpb, 0, 0) if nb == 1 else (i, 0, 0)

    full = lambda shape: pl.BlockSpec(shape, lambda i: (0,) * len(shape))
    if att_transposed:
        att_spec = pl.BlockSpec((None, D_ATT, tm), lambda i: (i // tpb, 0, i % tpb))
    else:
        att_spec = pl.BlockSpec((tm, D_ATT), lambda i: (i, 0))
    choice = lambda dt: (jax.ShapeDtypeStruct((TOP_K, n), dt), pl.BlockSpec((TOP_K, tm), lambda i: (0, i)))
    outs = [(jax.ShapeDtypeStruct((b, t, d), F32), pl.BlockSpec((nb, tt, d), xmap)),
            (jax.ShapeDtypeStruct((n, d), F32), pl.BlockSpec((tm, d), lambda i: (i, 0))),
            choice(I32), choice(F32), choice(I32),
            (jax.ShapeDtypeStruct((N_EXPERTS, LANES), F32), full((N_EXPERTS, LANES)))]
    return pl.pallas_call(
        functools.partial(_outproj_kernel, att_transposed=att_transposed),
        out_shape=tuple(o[0] for o in outs),
        grid=(n // tm,),
        in_specs=[pl.BlockSpec((nb, tt, d), xmap),
                  pl.BlockSpec((N_OCT, tm // SSM_CHUNK, SSM_CHUNK * LANES), lambda i: (0, i, 0)),
                  att_spec,
                  pl.BlockSpec((nb, 1, d), bmap), pl.BlockSpec((nb, 1, d), bmap), pl.BlockSpec((nb, 1, d), bmap),
                  full((1, d)), full((D_SSM, D_SSM)), full((1, D_SSM)), full((d, d)),
                  full((N_EXPERTS, d)), full((N_EXPERTS, 1)), full((N_EXPERTS, LANES))],
        out_specs=tuple(o[1] for o in outs),
        scratch_shapes=[pltpu.VMEM((N_EXPERTS, LANES), F32), pltpu.VMEM((N_OCT, tm, LANES), F32)],
        compiler_params=_cparams(("arbitrary",)),
        name="outproj_t" if att_transposed else "outproj",
    )(x, y4, att, gt, sh, sc, g_ffn, w["wglu"], w["bglu"], w["wout"], w["wr"], w["br"], counts_in)


def _dispatch_kernel(zst_ref, nu_ref, dest_ref, *rest, tm, tiles):
    h_refs = rest[:len(tiles)]
    xs_ref, zero_ref, zsem, sem = rest[len(tiles):]
    i = pl.program_id(0)
    n_blocks = xs_ref.shape[0] // MOE_BLK

    @pl.when(i == 0)
    def _():
        zero_ref[...] = jnp.zeros_like(zero_ref)

        def zero_copy(start):
            start = pl.multiple_of(start, MOE_BLK)
            return pltpu.make_async_copy(zero_ref, xs_ref.at[pl.ds(start, MOE_BLK)], zsem)

        for e in range(N_EXPERTS):
            @pl.when(zst_ref[e] >= 0)
            def _():
                zero_copy(jnp.maximum(zst_ref[e], 0)).start()

        def tail_start(j, carry):
            zero_copy(j * MOE_BLK).start()
            return carry

        def tail_wait(j, carry):
            zero_copy(j * MOE_BLK).wait()
            return carry

        lax.fori_loop(nu_ref[0], n_blocks, tail_start, 0)
        for e in range(N_EXPERTS):
            @pl.when(zst_ref[e] >= 0)
            def _():
                zero_copy(jnp.maximum(zst_ref[e], 0)).wait()
        lax.fori_loop(nu_ref[0], n_blocks, tail_wait, 0)

    def scatter_rows(h_ref):
        def issue(t, carry):
            for kk in range(TOP_K):
                dst = dest_ref[0, 0, t * TOP_K + kk]
                pltpu.make_async_copy(h_ref.at[pl.ds(t, 1)], xs_ref.at[pl.ds(dst, 1)], sem).start()
            return carry

        lax.fori_loop(0, tm, issue, 0, unroll=8)

    first_tile = 0
    for h_ref, n_tiles in zip(h_refs, tiles):
        pl.when((i >= first_tile) & (i < first_tile + n_tiles))(functools.partial(scatter_rows, h_ref))
        first_tile += n_tiles
    n_copied = tm * TOP_K
    pltpu.make_async_copy(xs_ref.at[pl.ds(0, n_copied)], xs_ref.at[pl.ds(0, n_copied)], sem).wait()


def _dispatch(h2s, dest, zstart, n_used, n_rows, tm):
    d = h2s[0].shape[1]
    tiles = tuple(h.shape[0] // tm for h in h2s)
    nt = sum(tiles)
    dest3 = dest.reshape(nt, 1, tm * TOP_K)
    in_specs = [pl.BlockSpec((1, 1, tm * TOP_K), lambda i, *_: (i, 0, 0), memory_space=pltpu.SMEM)]
    first_tile = 0
    for n_tiles in tiles:
        in_specs.append(pl.BlockSpec(
            (tm, d), lambda i, *_, f=first_tile, m=n_tiles: (jnp.clip(i - f, 0, m - 1), 0)))
        first_tile += n_tiles
    return pl.pallas_call(
        functools.partial(_dispatch_kernel, tm=tm, tiles=tiles),
        out_shape=jax.ShapeDtypeStruct((n_rows, d), F32),
        grid_spec=pltpu.PrefetchScalarGridSpec(
            num_scalar_prefetch=2, grid=(nt,),
            in_specs=in_specs,
            out_specs=pl.BlockSpec(memory_space=pl.ANY),
            scratch_shapes=[pltpu.VMEM((MOE_BLK, d), F32), pltpu.SemaphoreType.DMA, pltpu.SemaphoreType.DMA]),
        compiler_params=_cparams(("arbitrary",)),
        name="moe_dispatch",
    )(zstart, n_used, dest3, *h2s)


def _expert_kernel(be_ref, nu_ref, xs_ref, wup_ref, bup_ref, wdn_ref, bdn_ref, ys_ref, wup_bf, wdn_bf):
    i = pl.program_id(0)
    e = be_ref[i]
    prev = be_ref[jnp.maximum(i - 1, 0)]

    @pl.when(i < nu_ref[0])
    def _():
        @pl.when((i == 0) | (prev != e))
        def _():
            wup_bf[...] = wup_ref[...].astype(BF16)
            wdn_bf[...] = wdn_ref[...].astype(BF16)

        up = _dot(xs_ref[...].astype(BF16), wup_bf[...]) + bup_ref[...]
        x_glu = jnp.minimum(up[:, :D_FF], SWIGLU_LIMIT)
        x_lin = jnp.clip(up[:, D_FF:], -SWIGLU_LIMIT, SWIGLU_LIMIT)
        act = (x_lin + 1.0) * (x_glu * jax.nn.sigmoid(SWIGLU_ALPHA * x_glu))
        ys_ref[...] = _dot(act.astype(BF16), wdn_bf[...]) + bdn_ref[...]

    @pl.when(i >= nu_ref[0])
    def _():
        ys_ref[...] = jnp.zeros_like(ys_ref)


def _experts(xs, block_e, n_used, w_up, b_up, w_down, b_down):
    n_rows, d = xs.shape
    nblk = n_rows // MOE_BLK
    rowmap = lambda i, be, nu: (jnp.minimum(i, nu[0] - 1), 0)
    emap = lambda i, be, nu: (be[i], 0, 0)
    return pl.pallas_call(
        _expert_kernel,
        out_shape=jax.ShapeDtypeStruct((n_rows, d), F32),
        grid_spec=pltpu.PrefetchScalarGridSpec(
            num_scalar_prefetch=2, grid=(nblk,),
            in_specs=[pl.BlockSpec((MOE_BLK, d), rowmap),
                      pl.BlockSpec((None, d, 2 * D_FF), emap), pl.BlockSpec((None, 1, 2 * D_FF), emap),
                      pl.BlockSpec((None, D_FF, d), emap), pl.BlockSpec((None, 1, d), emap)],
            out_specs=pl.BlockSpec((MOE_BLK, d), lambda i, be, nu: (i, 0)),
            scratch_shapes=[pltpu.VMEM((d, 2 * D_FF), BF16), pltpu.VMEM((D_FF, d), BF16)]),
        compiler_params=_cparams(("arbitrary",)),
        name="moe_experts",
    )(block_e, n_used, xs, w_up, b_up, w_down, b_down)


def _combine_kernel(meta_ref, metan_ref, x1_ref, col_ref, gate_ref, gt_ref, gfin_ref, ys_ref, y_ref, ybuf_ref, sems,
                    *, tm):
    i = pl.program_id(0)
    nt = pl.num_programs(0)
    nb, tt, d = x1_ref.shape
    rows = ybuf_ref.shape[1]

    def run_copies(m_ref, slot, start):
        def per_expert(e, carry):
            a = m_ref[0, 0, e]
            n_chunks = m_ref[0, 0, N_EXPERTS + e]
            bo = m_ref[0, 0, 2 * N_EXPERTS + e]

            def per_chunk(c, carry2):
                src = ys_ref.at[pl.ds(pl.multiple_of(a + 8 * c, 8), 8)]
                dst = ybuf_ref.at[slot, pl.ds(pl.multiple_of(bo + 8 * c, 8), 8)]
                copy = pltpu.make_async_copy(src, dst, sems.at[slot])
                if start:
                    copy.start()
                else:
                    copy.wait()
                return carry2

            lax.fori_loop(0, n_chunks, per_chunk, 0)
            return carry

        lax.fori_loop(0, N_EXPERTS, per_expert, 0)

    @pl.when(i == 0)
    def _():
        ybuf_ref[...] = jnp.zeros_like(ybuf_ref)
        run_copies(meta_ref, 0, True)

    @pl.when(i + 1 < nt)
    def _():
        run_copies(metan_ref, (i + 1) % 2, True)

    slot = i % 2
    run_copies(meta_ref, slot, False)
    col = col_ref[...]
    gate = gate_ref[...]
    lane = lax.broadcasted_iota(I32, (tm, rows), 1)
    pick = jnp.where(lane == col[:, 0:1], gate[:, 0:1], 0.0)
    for kk in range(1, TOP_K):
        pick = pick + jnp.where(lane == col[:, kk:kk + 1], gate[:, kk:kk + 1], 0.0)
    moe = _dot(pick.astype(BF16), ybuf_ref[slot].astype(BF16))
    x2 = x1_ref[...] + gt_ref[...] * moe.reshape(nb, tt, d)
    ms = jnp.mean(x2 * x2, axis=-1, keepdims=True)
    y_ref[...] = x2 * lax.rsqrt(ms + NORM_EPS) * gfin_ref[...]


def _combine_plan(idx, dest, off, before0, tm):
    n = idx.shape[0]
    nt = n // tm
    experts = jnp.arange(N_EXPERTS, dtype=I32)
    chose = jnp.any(idx[:, :, None] == experts, axis=1).astype(I32)
    cnt = chose.reshape(nt, tm, N_EXPERTS).sum(axis=1)
    before = before0[None, :] + jnp.cumsum(cnt, axis=0) - cnt
    start = off[None, :] + before
    a8 = start // 8 * 8
    b8 = jnp.where(cnt > 0, (start + cnt + 7) // 8 * 8, a8)
    span = b8 - a8
    boff = jnp.cumsum(span, axis=1) - span
    delta = jnp.repeat(boff - a8, tm, axis=0)
    col = dest + jnp.take_along_axis(delta, idx, axis=1)
    meta = jnp.concatenate([a8, span // 8, boff, jnp.zeros_like(a8)], axis=1).astype(I32)
    return meta.reshape(nt, 1, 4 * N_EXPERTS), col.astype(I32)


def _combine(x1, gate, idx, dest, off, before0, ys, gt, g_final, tm):
    b, t, d = x1.shape
    n = b * t
    tm = min(tm, n)
    tt = min(t, tm)
    nb = tm // tt
    tpb = t // tt
    nt = n // tm
    rows = -(-(tm * TOP_K + 14 * N_EXPERTS) // 256) * 256
    meta, col = _combine_plan(idx, dest, off, before0, tm)

    def xmap(i):
        return (i // tpb, i % tpb, 0) if nb == 1 else (i, 0, 0)

    def bmap(i):
        return (i // tpb, 0, 0) if nb == 1 else (i, 0, 0)

    smem = lambda imap: pl.BlockSpec((1, 1, 4 * N_EXPERTS), imap, memory_space=pltpu.SMEM)
    return pl.pallas_call(
        functools.partial(_combine_kernel, tm=tm),
        out_shape=jax.ShapeDtypeStruct((b, t, d), F32),
        grid=(nt,),
        in_specs=[smem(lambda i: (i, 0, 0)), smem(lambda i: (jnp.minimum(i + 1, nt - 1), 0, 0)),
                  pl.BlockSpec((nb, tt, d), xmap),
                  pl.BlockSpec((tm, TOP_K), lambda i: (i, 0)),
                  pl.BlockSpec((tm, TOP_K), lambda i: (i, 0)),
                  pl.BlockSpec((nb, 1, d), bmap),
                  pl.BlockSpec((1, d), lambda i: (0, 0)),
                  pl.BlockSpec(memory_space=pl.ANY)],
        out_specs=pl.BlockSpec((nb, tt, d), xmap),
        scratch_shapes=[pltpu.VMEM((2, rows, d), F32), pltpu.SemaphoreType.DMA((2,))],
        compiler_params=_cparams(("arbitrary",)),
        name="moe_combine",
    )(meta, meta, x1, col, gate, gt, g_final, ys)


def _moe(streams, counts, g_final, w_up, b_up, w_down, b_down):
    n_total = sum(s["h2"].shape[0] for s in streams)
    cnt = counts[:, 0].astype(I32)
    padded = (cnt + MOE_BLK - 1) // MOE_BLK * MOE_BLK
    pad_end = jnp.cumsum(padded)
    off = (pad_end - padded).astype(I32)
    n_blocks = -(-n_total * TOP_K // MOE_BLK) + N_EXPERTS
    n_rows = n_blocks * MOE_BLK
    starts = jnp.arange(n_blocks, dtype=I32) * MOE_BLK
    block_e = jnp.minimum(jnp.sum((pad_end[None, :] <= starts[:, None]).astype(I32), axis=1), N_EXPERTS - 1)
    n_used = (pad_end[-1:] // MOE_BLK).astype(I32)
    zstart = jnp.where(padded > 0, pad_end - MOE_BLK, -1).astype(I32)
    experts = jnp.arange(N_EXPERTS, dtype=I32)
    idxs, dests = [], []
    for s in streams:
        idx = s["idx"].T
        idxs.append(idx)
        dests.append((jnp.sum(jnp.where(idx[:, :, None] == experts, off, 0), axis=-1) + s["rank"].T).astype(I32))
    tm = min([512] + [s["h2"].shape[0] for s in streams])
    xs = _dispatch([s["h2"] for s in streams], jnp.concatenate(dests, axis=0), zstart, n_used, n_rows, tm)
    ys = _experts(xs, block_e, n_used, w_up, b_up, w_down, b_down)
    return [_combine(s["x1"], s["gate"].T, idx, dest, off, s["before"], ys, s["gt"], g_final, 256)
            for s, idx, dest in zip(streams, idxs, dests)]


def _stream(x, mod, w, ssm_ops, s0, cache, params, counts_in):
    b, t, d = x.shape
    n = b * t
    sh_mix, sc_mix, gt_mix, sh_ffn, sc_ffn, gt_ffn = mod
    prompt = cache is None
    tm = 512
    u4, q, kt, vt, lft, vtb, k_tok, lf_tok = _inproj(x, sh_mix, sc_mix, params["g_mix"], w, prompt=prompt, tm=tm)
    n_chunks = t // SSM_CHUNK
    y4, s_fin = _ssm(u4, ssm_ops, s0, b, n_chunks)
    if prompt:
        kaug = _fprep_p(lf_tok, k_tok, b, t)
        att = _attn_p(q, kaug, vtb, b, t)
    else:
        cache_k, cache_v, cache_lf = cache
        p = cache_k.shape[1]
        ck_t = jnp.transpose(cache_k, (0, 2, 3, 1)).reshape(b, D_ATT, p)
        cv_t = jnp.transpose(cache_v, (0, 2, 3, 1)).reshape(b, D_ATT, p)
        cl_t = jnp.transpose(cache_lf, (0, 2, 1))
        ln_t = jnp.pad(lft, ((0, 0), (0, 0), (0, LANES - t)))
        fc, fn = _fprep_s(cl_t, ln_t)
        att = _attn_s(q.reshape(b, t, D_ATT), ck_t, cv_t, kt, vt, fc, fn).reshape(n, D_ATT)
    x1, h2, idx, gate, rank, counts = _outproj(x, y4, att, gt_mix, sh_ffn, sc_ffn, params["g_ffn"], w, counts_in,
                                               att_transposed=prompt, tm=512)
    routed = {"x1": x1, "h2": h2, "idx": idx, "gate": gate, "rank": rank, "gt": gt_ffn,
              "before": counts_in[:, 0].astype(I32)}
    heads = lambda a: jnp.transpose(a.reshape(b, N_HEADS, HEAD_DIM, t), (0, 3, 1, 2))[None]
    return routed, counts, heads(kt), heads(vt), jnp.transpose(lft, (0, 2, 1))[None], s_fin


def _state_in(re, im):
    b = re.shape[0]
    s = jnp.concatenate([re.reshape(b, N_OCT, OCT_STATE // 2), im.reshape(b, N_OCT, OCT_STATE // 2)], axis=-1)
    return jnp.transpose(s, (1, 0, 2))


def _state_out(s):
    b = s.shape[1]
    s = jnp.transpose(s, (1, 0, 2))
    half = OCT_STATE // 2
    re = s[:, :, :half].reshape(1, b, N_SSM_GROUPS, SSM_STATE)
    im = s[:, :, half:].reshape(1, b, N_SSM_GROUPS, SSM_STATE)
    return re, im


def kernel(x_prompt, x_sample, c_prompt, c_sample, cache_k, cache_v, cache_logf, state_ssm_re, state_ssm_im, w_ada, b_ada, g_mix, w_in, b_forget, ssm_log_dt, ssm_a_re, ssm_a_im, ssm_b_re, ssm_b_im, ssm_c_re, ssm_c_im, ssm_d, w_glu, b_glu, w_out, g_ffn, w_router, b_router, w_up, b_up, w_down, b_down, g_final):
    assert w_ada.shape[0] == 1, "single-layer trunk"
    d = D_MODEL
    bp, tp, _ = x_prompt.shape
    bs, ts, _ = x_sample.shape
    n_c = bp + bs
    c_all = jnp.pad(jnp.concatenate([c_prompt, c_sample], axis=0), ((0, -n_c % 8), (0, 0)))
    mod = _ada(c_all, w_ada[0], b_ada[0].reshape(1, 6 * d))
    mod_p = [mod[:bp, j * d:(j + 1) * d].reshape(bp, 1, d) for j in range(6)]
    mod_s = [mod[bp:n_c, j * d:(j + 1) * d].reshape(bs, 1, d) for j in range(6)]
    wi = w_in[0]
    o1, o2, o3, o4 = D_SSM, D_SSM + D_ATT, D_SSM + 2 * D_ATT, D_SSM + 3 * D_ATT
    w_q = wi[:, o1:o2].astype(BF16)
    w_k = wi[:, o2:o3].astype(BF16)
    w_f = wi[:, o4:].astype(BF16)
    unused = jnp.zeros((8, LANES), BF16)
    base = {"wu": wi[:, :o1].astype(BF16), "wkt": w_k.T, "wvt": wi[:, o3:o4].astype(BF16).T, "wft": w_f.T,
            "bfc": b_forget[0].reshape(N_HEADS, 1), "bfr": b_forget[0].reshape(1, N_HEADS),
            "wglu": w_glu[0].astype(BF16), "bglu": b_glu[0].reshape(1, D_SSM), "wout": w_out[0].astype(BF16),
            "wr": w_router[0].T, "br": b_router[0].reshape(N_EXPERTS, 1)}
    w_p = dict(base, wq=w_q.T, wk=w_k, wf=w_f)
    w_s = dict(base, wq=w_q, wk=unused, wf=unused)
    params = {"g_mix": g_mix[0].reshape(1, d), "g_ffn": g_ffn[0].reshape(1, d), "g_final": g_final.reshape(1, d),
              "w_up": w_up[0], "b_up": b_up[0].reshape(N_EXPERTS, 1, 2 * D_FF),
              "w_down": w_down[0], "b_down": b_down[0].reshape(N_EXPERTS, 1, d)}
    ssm_ops = _ssm_operators(ssm_log_dt[0], ssm_a_re[0], ssm_a_im[0], ssm_b_re[0], ssm_b_im[0],
                             ssm_c_re[0], ssm_c_im[0], ssm_d[0])
    zero_state = jnp.zeros((N_OCT, bp, OCT_STATE), F32)
    counts0 = jnp.zeros((N_EXPERTS, LANES), F32)
    r_p, counts_p, k_p, v_p, f_p, s_p = _stream(x_prompt, mod_p, w_p, ssm_ops, zero_state, None, params, counts0)
    s0 = _state_in(state_ssm_re[0], state_ssm_im[0])
    r_s, counts, k_s, v_s, f_s, s_s = _stream(x_sample, mod_s, w_s, ssm_ops, s0,
                                              (cache_k[0], cache_v[0], cache_logf[0]), params, counts_p)
    y_p, y_s = _moe([r_p, r_s], counts, params["g_final"],
                    params["w_up"], params["b_up"], params["w_down"], params["b_down"])
    sre_p, sim_p = _state_out(s_p)
    sre_s, sim_s = _state_out(s_s)
    return (y_p, y_s, k_p, v_p, f_p, sre_p, sim_p, k_s, v_s, f_s, sre_s, sim_s)
```

```python
import functools
import math

import jax
import jax.numpy as jnp
from jax import lax
from jax.experimental import pallas as pl
from jax.experimental.pallas import tpu as pltpu

F32 = jnp.float32
BF16 = jnp.bfloat16
I32 = jnp.int32

D_MODEL = 1024
D_SSM = 512
SSM_GROUP = 16
N_SSM_GROUPS = 32
SSM_STATE = 64
D_ATT = 512
HEAD_DIM = 64
N_HEADS = 8
N_EXPERTS = 32
TOP_K = 4
D_FF = 1024
SWIGLU_LIMIT = 7.0
SWIGLU_ALPHA = 1.702
NORM_EPS = 1e-5

LANES = 128
N_OCT = D_SSM // LANES
OCT_STATE = 2 * (N_SSM_GROUPS // N_OCT) * SSM_STATE
SSM_CHUNK = 8
SSM_ROWS = 256
SSM_INTERLEAVE = 4
ATT_BLK = 256
ATT_PAIRS = 4
LOG2E = math.log2(math.e)
SAMPLE_KV_BLK = 2048
MOE_BLK = 256
VMEM_LIMIT = 52 * 1024 * 1024


def _cparams(sem, vmem=VMEM_LIMIT):
    return pltpu.CompilerParams(dimension_semantics=sem, vmem_limit_bytes=vmem)


def _dot(a, b):
    return jnp.dot(a, b, preferred_element_type=F32)


def _dot_nt(a, b):
    return lax.dot_general(a, b, (((1,), (1,)), ((), ())), preferred_element_type=F32)


def _dot_hi(a, b):
    return jnp.dot(a, b, preferred_element_type=F32, precision=lax.Precision.HIGHEST)


def _ada_kernel(c_ref, w_ref, b_ref, o_ref):
    c = c_ref[...]
    s = c * jax.nn.sigmoid(c)
    o_ref[...] = _dot_hi(s, w_ref[...]) + b_ref[...]


def _ada(c_all, w_ada, b_ada):
    m, d = c_all.shape
    n = w_ada.shape[1]
    return pl.pallas_call(
        _ada_kernel,
        out_shape=jax.ShapeDtypeStruct((m, n), F32),
        grid=(n // d,),
        in_specs=[pl.BlockSpec((m, d), lambda j: (0, 0)),
                  pl.BlockSpec((d, d), lambda j: (0, j)),
                  pl.BlockSpec((1, d), lambda j: (0, j))],
        out_specs=pl.BlockSpec((m, d), lambda j: (0, j)),
        compiler_params=_cparams(("arbitrary",)),
        name="ada",
    )(c_all, w_ada, b_ada)


def _norm_mod(x, g, scale, shift):
    ms = jnp.mean(x * x, axis=-1, keepdims=True)
    y = x * lax.rsqrt(ms + NORM_EPS) * g
    return y * (1.0 + scale) + shift


def _log_sigmoid(z):
    return jnp.minimum(z, 0.0) - jnp.log1p(jnp.exp(-jnp.abs(z)))


def _inproj_kernel(x_ref, sh_ref, sc_ref, g_ref, wu_ref, wq_ref, wkt_ref, wvt_ref, wft_ref, bfc_ref,
                   wk_ref, pm_ref,
                   u4_ref, q_ref, kt_ref, vt_ref, lft_ref, vtb_ref, ka_ref, us_ref, carry_ref, *, prompt, tpb):
    nb, tt, d = x_ref.shape
    tm = nb * tt
    h = _norm_mod(x_ref[...], g_ref[...], sc_ref[...], sh_ref[...]).reshape(tm, d).astype(BF16)
    u = _dot(h, wu_ref[...])
    for o in range(N_OCT):
        us_ref[o] = u[:, o * LANES:(o + 1) * LANES]
    for s in range(SSM_CHUNK):
        for o in range(N_OCT):
            piece = us_ref[o, pl.ds(s, tm // SSM_CHUNK, stride=SSM_CHUNK), :]
            u4_ref[o, :, s * LANES:(s + 1) * LANES] = piece.astype(BF16)
    kt = _dot_nt(wkt_ref[...], h)
    vt = _dot_nt(wvt_ref[...], h)
    lft = _log_sigmoid(_dot_nt(wft_ref[...], h) + bfc_ref[...])
    scale = HEAD_DIM ** -0.5
    if prompt:
        kt_ref[...] = kt
        vt_ref[...] = vt
        lft_ref[...] = lft
        qt = (_dot_nt(wq_ref[...], h) * (scale * LOG2E)).astype(BF16)
        vtb = vt.astype(BF16)
        for j in range(tm // ATT_BLK):
            q_ref[j] = qt[:, j * ATT_BLK:(j + 1) * ATT_BLK]
            vtb_ref[j] = vtb[:, j * ATT_BLK:(j + 1) * ATT_BLK]
        k_tok = _dot(h, wk_ref[...])

        @pl.when(pl.program_id(0) % tpb == 0)
        def _():
            carry_ref[...] = jnp.zeros_like(carry_ref)

        cum = _lane_cumsum(lft) + carry_ref[:, 0:1]
        carry_ref[...] = jnp.broadcast_to(cum[:, tm - 1:tm], carry_ref.shape)
        nf = cum * (-LOG2E)
        hi = nf.astype(BF16).astype(F32)
        r1 = nf - hi
        mid = r1.astype(BF16).astype(F32)
        lo = (r1 - mid).astype(BF16).astype(F32)
        place = lambda piece, pm: lax.dot_general(piece, pm, (((0,), (0,)), ((), ())), preferred_element_type=F32)
        slab = place(hi, pm_ref[0]) + place(mid, pm_ref[1]) + place(lo, pm_ref[2])
        for j in range(tm // ATT_BLK):
            rows = slice(j * ATT_BLK, (j + 1) * ATT_BLK)
            for p in range(N_HEADS // 2):
                ka_ref[j, :, 2 * p * LANES:(2 * p + 1) * LANES] = k_tok[rows, p * LANES:(p + 1) * LANES].astype(BF16)
                ka_ref[j, :, (2 * p + 1) * LANES:(2 * p + 2) * LANES] = (
                    slab[rows, p * LANES:(p + 1) * LANES].astype(BF16))
    else:
        for bl in range(nb):
            kt_ref[bl] = kt[:, bl * tt:(bl + 1) * tt]
            vt_ref[bl] = vt[:, bl * tt:(bl + 1) * tt]
            lft_ref[bl] = lft[:, bl * tt:(bl + 1) * tt]
        q_ref[...] = (_dot(h, wq_ref[...]) * scale).astype(BF16)
        vtb_ref[...] = jnp.zeros_like(vtb_ref)
        ka_ref[...] = jnp.zeros_like(ka_ref)


def _inproj(x, shift, scale, g_mix, w, *, prompt, tm):
    b, t, d = x.shape
    n = b * t
    tm = min(tm, n)
    tt = min(t, tm)
    nb = tm // tt
    tpb = t // tt
    assert (nb == 1) == prompt
    grid = (n // tm,)

    def xmap(i):
        return (i // tpb, i % tpb, 0) if nb == 1 else (i, 0, 0)

    def bmap(i):
        return (i // tpb, 0, 0) if nb == 1 else (i, 0, 0)

    def tmap(i):
        return (i // tpb, 0, i % tpb) if nb == 1 else (i, 0, 0)

    full = lambda a: pl.BlockSpec(a.shape, lambda i: (0,) * a.ndim)
    names = ("wu", "wq", "wkt", "wvt", "wft", "bfc", "wk", "pm")
    in_specs = [pl.BlockSpec((nb, tt, d), xmap),
                pl.BlockSpec((nb, 1, d), bmap), pl.BlockSpec((nb, 1, d), bmap),
                pl.BlockSpec((1, d), lambda i: (0, 0))] + [full(w[k]) for k in names]
    tblk = lambda rows: pl.BlockSpec((None if nb == 1 else nb, rows, tt), tmap)
    dummy = (jax.ShapeDtypeStruct((8, LANES), F32), pl.BlockSpec((8, LANES), lambda i: (0, 0)))
    if prompt:
        nblk = n // ATT_BLK
        blocked = (jax.ShapeDtypeStruct((nblk, D_ATT, ATT_BLK), BF16),
                   pl.BlockSpec((tm // ATT_BLK, D_ATT, ATT_BLK), lambda i: (i, 0, 0)))
        q_out, vtb_out = blocked, blocked
        ka_out = (jax.ShapeDtypeStruct((nblk, ATT_BLK, 2 * D_ATT), BF16),
                  pl.BlockSpec((tm // ATT_BLK, ATT_BLK, 2 * D_ATT), lambda i: (i, 0, 0)))
    else:
        q_out = (jax.ShapeDtypeStruct((n, D_ATT), BF16), pl.BlockSpec((tm, D_ATT), lambda i: (i, 0)))
        vtb_out = ka_out = dummy
    kdim = SSM_CHUNK * LANES
    outs = [(jax.ShapeDtypeStruct((N_OCT, n // SSM_CHUNK, kdim), BF16),
             pl.BlockSpec((N_OCT, tm // SSM_CHUNK, kdim), lambda i: (0, i, 0))),
            q_out,
            (jax.ShapeDtypeStruct((b, D_ATT, t), F32), tblk(D_ATT)),
            (jax.ShapeDtypeStruct((b, D_ATT, t), F32), tblk(D_ATT)),
            (jax.ShapeDtypeStruct((b, N_HEADS, t), F32), tblk(N_HEADS)),
            vtb_out, ka_out]
    return pl.pallas_call(
        functools.partial(_inproj_kernel, prompt=prompt, tpb=tpb),
        out_shape=tuple(o[0] for o in outs), grid=grid, in_specs=in_specs, out_specs=tuple(o[1] for o in outs),
        scratch_shapes=[pltpu.VMEM((N_OCT, tm, LANES), F32), pltpu.VMEM((8, LANES), F32)],
        compiler_params=_cparams(("arbitrary",)),
        name="inproj_p" if prompt else "inproj_s",
    )(x, shift, scale, g_mix, *[w[k] for k in names])


def _ssm_operators(log_dt, a_re, a_im, b_re, b_im, c_re, c_im, d_skip):
    L, G, P, C = SSM_CHUNK, N_SSM_GROUPS, SSM_STATE, SSM_GROUP
    q = G // N_OCT
    lam = lax.complex(a_re, a_im)
    dt = jnp.exp(log_dt)[:, None]
    lam_dt = lam * dt
    lam_bar = jnp.exp(lam_dt)
    b_bar = ((lam_bar - 1.0) / lam)[:, :, None] * lax.complex(b_re, b_im)
    c_mat = lax.complex(c_re, c_im)
    pw = jnp.exp(lam_dt[None] * jnp.arange(L + 1, dtype=F32)[:, None, None])
    kern = jnp.real(jnp.einsum('gcp,dgp,gpk->dgck', c_mat, pw[:L], b_bar))
    ksm = jnp.transpose(kern, (1, 0, 3, 2)).reshape(N_OCT, q, L, C, C)
    ksm = jnp.transpose(ksm, (0, 2, 1, 3, 4)).reshape(N_OCT, L, q * C, C)
    pwr = pw[L - 1 - jnp.arange(L)]
    wb = jnp.einsum('jgp,gpk->jgkp', pwr, b_bar)
    wsm = jnp.stack([jnp.real(wb), jnp.imag(wb)], axis=0).reshape(2, L, N_OCT, q * C, P)
    wsm = jnp.transpose(wsm, (2, 1, 0, 3, 4))
    cl = c_mat[None] * pw[1:L + 1][:, :, None, :]
    vsm = jnp.stack([jnp.real(cl), -jnp.imag(cl)], axis=0).reshape(2, L, N_OCT, q, C, P)
    vsm = jnp.transpose(vsm, (2, 1, 0, 3, 5, 4)).reshape(N_OCT, L, OCT_STATE, C)
    tmat, wend, win = _ssm_prep(ksm, wsm, vsm, d_skip.reshape(N_OCT, 1, LANES))
    dec = pw[L].reshape(N_OCT, q * P)
    decay = jnp.concatenate([jnp.real(dec), jnp.imag(dec)], axis=-1).reshape(N_OCT, 1, OCT_STATE)
    return tmat, wend, win, decay.astype(F32)


def _ssm_prep_kernel(ksm_ref, wsm_ref, vsm_ref, d_ref, tmat_ref, wend_ref, win_ref):
    L, C, P = SSM_CHUNK, SSM_GROUP, SSM_STATE
    half = OCT_STATE // 2

    def spread(period, width):
        r = lax.broadcasted_iota(I32, (period, width), 0)
        c = lax.broadcasted_iota(I32, (period, width), 1)
        return jnp.where(c % period == r, 1.0, 0.0)

    def same_group(shape, row_div, row_mod, lane_div):
        r = lax.broadcasted_iota(I32, shape, 0)
        c = lax.broadcasted_iota(I32, shape, 1)
        return (r % row_mod) // row_div == c // lane_div

    e_c = spread(C, LANES)
    e_p = spread(P, half)
    grp = same_group((LANES, LANES), C, LANES, C)
    rr = lax.broadcasted_iota(I32, (LANES, LANES), 0)
    cc = lax.broadcasted_iota(I32, (LANES, LANES), 1)
    lag = []
    for d in range(L):
        blk = jnp.where(grp, _dot_hi(ksm_ref[d], e_c), 0.0)
        if d == 0:
            blk = blk + jnp.where(rr == cc, jnp.broadcast_to(d_ref[...], (LANES, LANES)), 0.0)
        lag.append(blk.astype(BF16))
    zero = jnp.zeros((LANES, LANES), BF16)
    for j in range(L):
        for t in range(L):
            tmat_ref[j * LANES:(j + 1) * LANES, t * LANES:(t + 1) * LANES] = lag[t - j] if t >= j else zero
    grp_e = same_group((LANES, half), C, LANES, P)
    for j in range(L):
        for part in range(2):
            blk = jnp.where(grp_e, _dot_hi(wsm_ref[j, part], e_p), 0.0)
            wend_ref[j * LANES:(j + 1) * LANES, part * half:(part + 1) * half] = blk.astype(BF16)
    grp_i = same_group((OCT_STATE, LANES), P, half, C)
    for t in range(L):
        blk = jnp.where(grp_i, _dot_hi(vsm_ref[t], e_c), 0.0)
        win_ref[:, t * LANES:(t + 1) * LANES] = blk.astype(BF16)


def _ssm_prep(ksm, wsm, vsm, dvec):
    kdim = SSM_CHUNK * LANES
    blk = lambda a: pl.BlockSpec((None,) + a.shape[1:], lambda o: (o,) + (0,) * (a.ndim - 1))
    out = lambda r, c: (jax.ShapeDtypeStruct((N_OCT, r, c), BF16), pl.BlockSpec((None, r, c), lambda o: (o, 0, 0)))
    outs = [out(kdim, kdim), out(kdim, OCT_STATE), out(OCT_STATE, kdim)]
    return pl.pallas_call(
        _ssm_prep_kernel,
        out_shape=tuple(o[0] for o in outs),
        grid=(N_OCT,),
        in_specs=[blk(ksm), blk(wsm), blk(vsm), blk(dvec)],
        out_specs=tuple(o[1] for o in outs),
        compiler_params=_cparams(("arbitrary",)),
        name="ssm_prep",
    )(ksm, wsm, vsm, dvec)


def _ssm_kernel(u_ref, tmat_ref, wend_ref, win_ref, dec_ref, s0_ref, y_ref, sfin_ref, e_ref, sp_ref, st_ref,
                *, nb, cpt):
    i = pl.program_id(1)
    half = OCT_STATE // 2
    rows = nb * cpt

    @pl.when(i == 0)
    def _():
        st_ref[...] = s0_ref[...]

    u = u_ref[...].reshape(rows, u_ref.shape[-1])
    e_ref[...] = _dot(u, wend_ref[...])
    dec = dec_ref[...]
    ar, ai = dec[:, :half], dec[:, half:]

    def advance(row, st):
        sp_ref[pl.ds(row, 1), :] = st
        e = e_ref[pl.ds(row, 1), :]
        re, im = st[:, :half], st[:, half:]
        nre = ar * re - ai * im + e[:, :half]
        nim = ar * im + ai * re + e[:, half:]
        return jnp.concatenate([nre, nim], axis=-1)

    if nb <= SSM_INTERLEAVE:
        def per_chunk(r, sts):
            return tuple(advance(bb * cpt + r, sts[bb]) for bb in range(nb))

        sts = lax.fori_loop(0, cpt, per_chunk, tuple(st_ref[bb:bb + 1, :] for bb in range(nb)))
        for bb in range(nb):
            st_ref[bb:bb + 1, :] = sts[bb]
    else:
        def per_batch(bb, carry):
            st = lax.fori_loop(0, cpt, lambda r, st: advance(bb * cpt + r, st), st_ref[pl.ds(bb, 1), :])
            st_ref[pl.ds(bb, 1), :] = st
            return carry

        lax.fori_loop(0, nb, per_batch, 0)
    y = _dot(u, tmat_ref[...]) + _dot(sp_ref[...].astype(BF16), win_ref[...])
    y_ref[...] = jax.nn.gelu(y).astype(BF16).reshape(y_ref.shape)
    sfin_ref[...] = st_ref[...]


def _ssm(u4, ops, s0, batch, n_chunks):
    tmat, wend, win, decay = ops
    kdim = SSM_CHUNK * LANES
    cpt = SSM_ROWS if n_chunks % SSM_ROWS == 0 else n_chunks
    tiles = n_chunks // cpt
    if tiles > 1:
        u_in = u4.reshape(N_OCT, batch, n_chunks, kdim)
        u_spec = pl.BlockSpec((None, batch, cpt, kdim), lambda o, i: (o, 0, i, 0))
    else:
        u_in = u4
        u_spec = pl.BlockSpec((None, batch * n_chunks, kdim), lambda o, i: (o, 0, 0))
    r = batch * cpt
    wspec = lambda shape: pl.BlockSpec((None,) + shape, lambda o, i: (o, 0, 0))
    y, s_fin = pl.pallas_call(
        functools.partial(_ssm_kernel, nb=batch, cpt=cpt),
        out_shape=(jax.ShapeDtypeStruct(u_in.shape, BF16),
                   jax.ShapeDtypeStruct((N_OCT, batch, OCT_STATE), F32)),
        grid=(N_OCT, tiles),
        in_specs=[u_spec,
                  wspec((kdim, kdim)), wspec((kdim, OCT_STATE)), wspec((OCT_STATE, kdim)), wspec((1, OCT_STATE)),
                  wspec((batch, OCT_STATE))],
        out_specs=(u_spec, wspec((batch, OCT_STATE))),
        scratch_shapes=[pltpu.VMEM((r, OCT_STATE), F32), pltpu.VMEM((r, OCT_STATE), F32),
                        pltpu.VMEM((batch, OCT_STATE), F32)],
        compiler_params=_cparams(("arbitrary", "arbitrary")),
        name="ssm",
    )(u_in, tmat, wend, win, decay, s0)
    return y.reshape(u4.shape), s_fin


def _bias_placement():
    import numpy as np
    pm = np.zeros((3, N_HEADS, N_HEADS // 2 * LANES), np.float32)
    for piece in range(3):
        for h in range(N_HEADS):
            pm[piece, h, LANES * (h // 2) + 3 * (h % 2) + piece] = 1.0
    return jnp.asarray(pm)


def _attn_p_kernel(qt_ref, ka_ref, vt_ref, o_ref, qbd_ref, acc_ref, sa_ref, sb_ref):
    i = pl.program_id(2)
    bq = ATT_BLK
    hd = HEAD_DIM
    r = lax.broadcasted_iota(I32, (LANES, 2 * bq), 0)
    c = lax.broadcasted_iota(I32, (LANES, 2 * bq), 1)
    ones = jnp.where(((r < 3) & (c < bq)) | ((r >= 3) & (r < 6) & (c >= bq)), 1.0, 0.0).astype(BF16)
    zero = jnp.zeros((hd, bq), BF16)
    for pp in range(ATT_PAIRS):
        qb = qt_ref[pp * LANES:(pp + 1) * LANES, :]
        qbd_ref[pp, 0:hd, 0:bq] = qb[0:hd]
        qbd_ref[pp, 0:hd, bq:2 * bq] = zero
        qbd_ref[pp, hd:2 * hd, 0:bq] = zero
        qbd_ref[pp, hd:2 * hd, bq:2 * bq] = qb[hd:2 * hd]
        qbd_ref[pp, 2 * hd:2 * hd + LANES, :] = ones
    acc_ref[...] = jnp.zeros_like(acc_ref)

    nq = ka_ref.shape[0]

    def scores(s_ref, j):
        for pp in range(ATT_PAIRS):
            s_ref[pp] = _dot(ka_ref[j, :, pp * 2 * LANES:(pp + 1) * 2 * LANES], qbd_ref[pp])

    def attend(s_ref, j, carry, masked):
        out = []
        for pp in range(ATT_PAIRS):
            m_prev, l_prev = carry[2 * pp], carry[2 * pp + 1]
            s = s_ref[pp]
            if masked:
                key = j * bq + lax.broadcasted_iota(I32, s.shape, 0)
                qq = lax.broadcasted_iota(I32, s.shape, 1)
                qq = i * bq + jnp.where(qq >= bq, qq - bq, qq)
                s = jnp.where(key <= qq, s, -jnp.inf)
            m_new = jnp.maximum(m_prev, jnp.max(s, axis=0, keepdims=True))
            alpha = jnp.exp2(m_prev - m_new)
            p = jnp.exp2(s - m_new)
            l_new = alpha * l_prev + jnp.sum(p, axis=0, keepdims=True)
            pb = p.astype(BF16)
            vb = vt_ref[jnp.minimum(j, nq - 1), pp * LANES:(pp + 1) * LANES, :]
            r0 = pp * LANES
            acc_ref[r0:r0 + hd] = alpha[:, 0:bq] * acc_ref[r0:r0 + hd] + _dot(vb[0:hd], pb[:, 0:bq])
            acc_ref[r0 + hd:r0 + 2 * hd] = (alpha[:, bq:2 * bq] * acc_ref[r0 + hd:r0 + 2 * hd]
                                            + _dot(vb[hd:2 * hd], pb[:, bq:2 * bq]))
            out.extend((m_new, l_new))
        return tuple(out)

    def double_step(m, carry):
        j = 2 * m
        scores(sb_ref, j + 1)
        carry = attend(sa_ref, j, carry, False)
        scores(sa_ref, j + 2)
        return attend(sb_ref, j + 1, carry, False)

    init = (jnp.full((1, 2 * bq), -jnp.inf, F32), jnp.zeros((1, 2 * bq), F32)) * ATT_PAIRS
    scores(sa_ref, 0)
    carry = lax.fori_loop(0, i // 2, double_step, init)
    j = 2 * (i // 2)
    scores(sb_ref, jnp.minimum(j + 1, nq - 1))
    carry = attend(sa_ref, j, carry, True)
    carry = attend(sb_ref, j + 1, carry, True)
    for pp in range(ATT_PAIRS):
        l = carry[2 * pp + 1]
        r0 = pp * LANES
        o_ref[r0:r0 + hd] = acc_ref[r0:r0 + hd] / l[:, 0:bq]
        o_ref[r0 + hd:r0 + 2 * hd] = acc_ref[r0 + hd:r0 + 2 * hd] / l[:, bq:2 * bq]


def _attn_p(qt, kaug, vt, batch, t):
    blk = ATT_BLK
    nq = t // blk
    ngrp = N_HEADS // 2 // ATT_PAIRS
    rows = ATT_PAIRS * LANES
    qt4 = qt.reshape(batch, nq, D_ATT, blk)
    vt4 = vt.reshape(batch, nq, D_ATT, blk)
    ka4 = kaug.reshape(batch, nq, blk, 2 * D_ATT)
    return pl.pallas_call(
        _attn_p_kernel,
        out_shape=jax.ShapeDtypeStruct((batch, D_ATT, t), F32),
        grid=(batch, ngrp, nq),
        in_specs=[pl.BlockSpec((None, None, rows, blk), lambda b, p, i: (b, i, p, 0)),
                  pl.BlockSpec((None, nq, blk, 2 * rows), lambda b, p, i: (b, 0, 0, p), pipeline_mode=pl.Buffered(1)),
                  pl.BlockSpec((None, nq, rows, blk), lambda b, p, i: (b, 0, p, 0), pipeline_mode=pl.Buffered(1))],
        out_specs=pl.BlockSpec((None, rows, blk), lambda b, p, i: (b, p, i)),
        scratch_shapes=[pltpu.VMEM((ATT_PAIRS, 2 * LANES, 2 * blk), BF16), pltpu.VMEM((rows, blk), F32),
                        pltpu.VMEM((ATT_PAIRS, blk, 2 * blk), F32), pltpu.VMEM((ATT_PAIRS, blk, 2 * blk), F32)],
        compiler_params=_cparams(("arbitrary", "arbitrary", "arbitrary")),
        name="attn_p",
    )(qt4, ka4, vt4)


def _lane_cumsum(x):
    n = x.shape[-1]
    lane = lax.broadcasted_iota(I32, x.shape, x.ndim - 1)
    s = 1
    while s < n:
        x = x + jnp.where(lane >= s, pltpu.roll(x, s, x.ndim - 1), 0.0)
        s *= 2
    return x


def _fprep_s_kernel(cl_ref, ln_ref, fc_ref, fn_ref):
    b, h, p = cl_ref.shape
    cum = _lane_cumsum(cl_ref[...].reshape(b * h, p))
    fc_ref[...] = (-cum).reshape(b, h, p)
    total = cum[:, p - 1:p]
    cn = _lane_cumsum(ln_ref[...].reshape(b * h, LANES))
    fn_ref[...] = (-(total + cn)).reshape(b, h, LANES)


def _fprep_s(cache_lf_t, new_lf_t):
    b, h, p = cache_lf_t.shape
    return pl.pallas_call(
        _fprep_s_kernel,
        out_shape=(jax.ShapeDtypeStruct((b, h, p), F32), jax.ShapeDtypeStruct((b, h, LANES), F32)),
        grid=(1,),
        in_specs=[pl.BlockSpec((b, h, p), lambda i: (0, 0, 0)), pl.BlockSpec((b, h, LANES), lambda i: (0, 0, 0))],
        out_specs=(pl.BlockSpec((b, h, p), lambda i: (0, 0, 0)), pl.BlockSpec((b, h, LANES), lambda i: (0, 0, 0))),
        compiler_params=_cparams(("arbitrary",)),
        name="fprep_s",
    )(cache_lf_t, new_lf_t)


def _attn_s_kernel(q_ref, ck_ref, cv_ref, kn_ref, vn_ref, fc_ref, fn_ref, o_ref, qbd_ref, m_ref, l_ref, acc_ref,
                   *, tq):
    j = pl.program_id(1)
    nkv = pl.num_programs(1)
    rows = N_HEADS * tq
    bk = ck_ref.shape[1]

    @pl.when(j == 0)
    def _():
        q = q_ref[...]
        qrep = jnp.broadcast_to(q[None], (N_HEADS, tq, D_ATT)).reshape(rows, D_ATT)
        rh = lax.broadcasted_iota(I32, (rows, D_ATT), 0) // tq
        ch = lax.broadcasted_iota(I32, (rows, D_ATT), 1) // HEAD_DIM
        qbd_ref[...] = jnp.where(rh == ch, qrep, jnp.zeros_like(qrep))
        m_ref[...] = jnp.full_like(m_ref, -jnp.inf)
        l_ref[...] = jnp.zeros_like(l_ref)
        acc_ref[...] = jnp.zeros_like(acc_ref)

    def update(s, vt):
        m_prev = m_ref[...]
        m_new = jnp.maximum(m_prev, jnp.max(s, axis=1, keepdims=True))
        alpha = jnp.exp(m_prev - m_new)
        p = jnp.exp(s - m_new)
        l_ref[...] = alpha * l_ref[...] + jnp.sum(p, axis=1, keepdims=True)
        m_ref[...] = m_new
        acc_ref[...] = alpha * acc_ref[...] + _dot_nt(p.astype(BF16), vt)

    def bias(f, width):
        return jnp.broadcast_to(f[:, None, :], (N_HEADS, tq, width)).reshape(rows, width)

    s = _dot(qbd_ref[...], ck_ref[...].astype(BF16)) + bias(fc_ref[...], bk)
    update(s, cv_ref[...].astype(BF16))

    @pl.when(j == nkv - 1)
    def _():
        s2 = _dot(qbd_ref[...], kn_ref[...].astype(BF16)) + bias(fn_ref[...][:, 0:tq], tq)
        key = lax.broadcasted_iota(I32, (rows, tq), 1)
        qq = lax.broadcasted_iota(I32, (rows, tq), 0) % tq
        update(jnp.where(key <= qq, s2, -jnp.inf), vn_ref[...].astype(BF16))
        o = acc_ref[...] / l_ref[...]
        rh = lax.broadcasted_iota(I32, (rows, D_ATT), 0) // tq
        ch = lax.broadcasted_iota(I32, (rows, D_ATT), 1) // HEAD_DIM
        o = jnp.where(rh == ch, o, 0.0).reshape(N_HEADS, tq, D_ATT)
        o_ref[...] = jnp.sum(o, axis=0)


def _attn_s(q, cache_k, cache_v, k_new, v_new, fc, fn):
    b, tq, _ = q.shape
    p = cache_k.shape[2]
    bk = min(SAMPLE_KV_BLK, p)
    nkv = p // bk
    rows = N_HEADS * tq
    tok = lambda: pl.BlockSpec((None, tq, D_ATT), lambda bi, j: (bi, 0, 0))
    new = lambda: pl.BlockSpec((None, D_ATT, tq), lambda bi, j: (bi, 0, 0))
    return pl.pallas_call(
        functools.partial(_attn_s_kernel, tq=tq),
        out_shape=jax.ShapeDtypeStruct((b, tq, D_ATT), F32),
        grid=(b, nkv),
        in_specs=[tok(),
                  pl.BlockSpec((None, D_ATT, bk), lambda bi, j: (bi, 0, j)),
                  pl.BlockSpec((None, D_ATT, bk), lambda bi, j: (bi, 0, j)),
                  new(), new(),
                  pl.BlockSpec((None, N_HEADS, bk), lambda bi, j: (bi, 0, j)),
                  pl.BlockSpec((None, N_HEADS, LANES), lambda bi, j: (bi, 0, 0))],
        out_specs=tok(),
        scratch_shapes=[pltpu.VMEM((rows, D_ATT), BF16), pltpu.VMEM((rows, 1), F32),
                        pltpu.VMEM((rows, 1), F32), pltpu.VMEM((rows, D_ATT), F32)],
        compiler_params=_cparams(("arbitrary", "arbitrary")),
        name="attn_s",
    )(q, cache_k, cache_v, k_new, v_new, fc, fn)


def _outproj_kernel(x_ref, y4_ref, att_ref, gt_ref, sh_ref, sc_ref, gf_ref, wglu_ref, bglu_ref, wout_ref,
                    wr_ref, br_ref, cin_ref,
                    x1_ref, h2_ref, idx_ref, gate_ref, rank_ref, cnt_ref, carry_ref, ys_ref, *, att_transposed):
    i = pl.program_id(0)
    nb, tt, d = x_ref.shape
    tm = nb * tt

    @pl.when(i == 0)
    def _():
        carry_ref[...] = cin_ref[...]

    for s in range(SSM_CHUNK):
        for o in range(N_OCT):
            ys_ref[o, pl.ds(s, tm // SSM_CHUNK, stride=SSM_CHUNK), :] = (
                y4_ref[o, :, s * LANES:(s + 1) * LANES].astype(F32))
    ysf = jnp.concatenate([ys_ref[o] for o in range(N_OCT)], axis=-1)
    glu = ysf * jax.nn.sigmoid(_dot(ysf.astype(BF16), wglu_ref[...]) + bglu_ref[...])
    att = att_ref[...]
    if att_transposed:
        att = att.T
    mix = _dot(glu.astype(BF16), wout_ref[0:D_SSM, :]) + _dot(att.astype(BF16), wout_ref[D_SSM:, :])
    x1 = x_ref[...] + gt_ref[...] * mix.reshape(nb, tt, d)
    x1_ref[...] = x1
    h2 = _norm_mod(x1, gf_ref[...], sc_ref[...], sh_ref[...]).reshape(tm, d)
    h2_ref[...] = h2

    logits = lax.dot_general(wr_ref[...], h2, (((1,), (1,)), ((), ())), preferred_element_type=F32,
                             precision=lax.Precision.HIGHEST) + br_ref[...]
    sub = lax.broadcasted_iota(I32, logits.shape, 0)
    work = logits
    vals, idxs = [], []
    for _ in range(TOP_K):
        mx = jnp.max(work, axis=0, keepdims=True)
        ix = jnp.min(jnp.where(work == mx, sub, N_EXPERTS), axis=0, keepdims=True)
        vals.append(mx)
        idxs.append(ix)
        work = jnp.where(sub == ix, -jnp.inf, work)
    ex = [jnp.exp(v - vals[0]) for v in vals]
    den = ex[0] + ex[1] + ex[2] + ex[3]
    mh = jnp.where(work == -jnp.inf, 1.0, 0.0)
    r = lax.broadcasted_iota(I32, (tm, tm), 0)
    c = lax.broadcasted_iota(I32, (tm, tm), 1)
    earlier = jnp.where(r < c, 1.0, 0.0).astype(BF16)
    carry = carry_ref[...]
    before = _dot(mh.astype(BF16), earlier) + carry[:, 0:1]
    carry_ref[...] = carry + jnp.sum(mh, axis=1, keepdims=True)
    for kk in range(TOP_K):
        idx_ref[kk:kk + 1, :] = idxs[kk]
        gate_ref[kk:kk + 1, :] = ex[kk] / den
        rk = jnp.sum(jnp.where(sub == idxs[kk], before, 0.0), axis=0, keepdims=True)
        rank_ref[kk:kk + 1, :] = rk.astype(I32)
    cnt_ref[...] = carry_ref[...]


def _outproj(x, y4, att, gt, sh, sc, g_ffn, w, counts_in, *, att_transposed, tm):
    b, t, d = x.shape
    n = b * t
    tm = min(tm, n)
    tt = min(t, tm)
    nb = tm // tt
    tpb = t // tt

    def xmap(i):
        return (i // tpb, i % tpb, 0) if nb == 1 else (i, 0, 0)

    def bmap(i):
        return (i // tpb, 0, 0) if nb == 1 else (i, 0, 0)

    full = lambda shape: pl.BlockSpec(shape, lambda i: (0,) * len(shape))
    if att_transposed:
        att_spec = pl.BlockSpec((None, D_ATT, tm), lambda i: (i // tpb, 0, i % tpb))
    else:
        att_spec = pl.BlockSpec((tm, D_ATT), lambda i: (i, 0))
    choice = lambda dt: (jax.ShapeDtypeStruct((TOP_K, n), dt), pl.BlockSpec((TOP_K, tm), lambda i: (0, i)))
    outs = [(jax.ShapeDtypeStruct((b, t, d), F32), pl.BlockSpec((nb, tt, d), xmap)),
            (jax.ShapeDtypeStruct((n, d), F32), pl.BlockSpec((tm, d), lambda i: (i, 0))),
            choice(I32), choice(F32), choice(I32),
            (jax.ShapeDtypeStruct((N_EXPERTS, LANES), F32), full((N_EXPERTS, LANES)))]
    return pl.pallas_call(
        functools.partial(_outproj_kernel, att_transposed=att_transposed),
        out_shape=tuple(o[0] for o in outs),
        grid=(n // tm,),
        in_specs=[pl.BlockSpec((nb, tt, d), xmap),
                  pl.BlockSpec((N_OCT, tm // SSM_CHUNK, SSM_CHUNK * LANES), lambda i: (0, i, 0)),
                  att_spec,
                  pl.BlockSpec((nb, 1, d), bmap), pl.BlockSpec((nb, 1, d), bmap), pl.BlockSpec((nb, 1, d), bmap),
                  full((1, d)), full((D_SSM, D_SSM)), full((1, D_SSM)), full((d, d)),
                  full((N_EXPERTS, d)), full((N_EXPERTS, 1)), full((N_EXPERTS, LANES))],
        out_specs=tuple(o[1] for o in outs),
        scratch_shapes=[pltpu.VMEM((N_EXPERTS, LANES), F32), pltpu.VMEM((N_OCT, tm, LANES), F32)],
        compiler_params=_cparams(("arbitrary",)),
        name="outproj_t" if att_transposed else "outproj",
    )(x, y4, att, gt, sh, sc, g_ffn, w["wglu"], w["bglu"], w["wout"], w["wr"], w["br"], counts_in)


def _dispatch_kernel(zst_ref, nu_ref, dest_ref, *rest, tm, tiles):
    h_refs = rest[:len(tiles)]
    xs_ref, zero_ref, zsem, sem = rest[len(tiles):]
    i = pl.program_id(0)
    n_blocks = xs_ref.shape[0] // MOE_BLK

    @pl.when(i == 0)
    def _():
        zero_ref[...] = jnp.zeros_like(zero_ref)

        def zero_copy(start):
            start = pl.multiple_of(start, MOE_BLK)
            return pltpu.make_async_copy(zero_ref, xs_ref.at[pl.ds(start, MOE_BLK)], zsem)

        for e in range(N_EXPERTS):
            @pl.when(zst_ref[e] >= 0)
            def _():
                zero_copy(jnp.maximum(zst_ref[e], 0)).start()

        def tail_start(j, carry):
            zero_copy(j * MOE_BLK).start()
            return carry

        def tail_wait(j, carry):
            zero_copy(j * MOE_BLK).wait()
            return carry

        lax.fori_loop(nu_ref[0], n_blocks, tail_start, 0)
        for e in range(N_EXPERTS):
            @pl.when(zst_ref[e] >= 0)
            def _():
                zero_copy(jnp.maximum(zst_ref[e], 0)).wait()
        lax.fori_loop(nu_ref[0], n_blocks, tail_wait, 0)

    def scatter_rows(h_ref):
        def issue(t, carry):
            for kk in range(TOP_K):
                dst = dest_ref[0, 0, t * TOP_K + kk]
                pltpu.make_async_copy(h_ref.at[pl.ds(t, 1)], xs_ref.at[pl.ds(dst, 1)], sem).start()
            return carry

        lax.fori_loop(0, tm, issue, 0, unroll=8)

    first_tile = 0
    for h_ref, n_tiles in zip(h_refs, tiles):
        pl.when((i >= first_tile) & (i < first_tile + n_tiles))(functools.partial(scatter_rows, h_ref))
        first_tile += n_tiles
    n_copied = tm * TOP_K
    pltpu.make_async_copy(xs_ref.at[pl.ds(0, n_copied)], xs_ref.at[pl.ds(0, n_copied)], sem).wait()


def _dispatch(h2s, dest, zstart, n_used, n_rows, tm):
    d = h2s[0].shape[1]
    tiles = tuple(h.shape[0] // tm for h in h2s)
    nt = sum(tiles)
    dest3 = dest.reshape(nt, 1, tm * TOP_K)
    in_specs = [pl.BlockSpec((1, 1, tm * TOP_K), lambda i, *_: (i, 0, 0), memory_space=pltpu.SMEM)]
    first_tile = 0
    for n_tiles in tiles:
        in_specs.append(pl.BlockSpec(
            (tm, d), lambda i, *_, f=first_tile, m=n_tiles: (jnp.clip(i - f, 0, m - 1), 0)))
        first_tile += n_tiles
    return pl.pallas_call(
        functools.partial(_dispatch_kernel, tm=tm, tiles=tiles),
        out_shape=jax.ShapeDtypeStruct((n_rows, d), F32),
        grid_spec=pltpu.PrefetchScalarGridSpec(
            num_scalar_prefetch=2, grid=(nt,),
            in_specs=in_specs,
            out_specs=pl.BlockSpec(memory_space=pl.ANY),
            scratch_shapes=[pltpu.VMEM((MOE_BLK, d), F32), pltpu.SemaphoreType.DMA, pltpu.SemaphoreType.DMA]),
        compiler_params=_cparams(("arbitrary",)),
        name="moe_dispatch",
    )(zstart, n_used, dest3, *h2s)


def _expert_kernel(be_ref, nu_ref, xs_ref, wup_ref, bup_ref, wdn_ref, bdn_ref, ys_ref, wup_bf, wdn_bf):
    i = pl.program_id(0)
    e = be_ref[i]
    prev = be_ref[jnp.maximum(i - 1, 0)]

    @pl.when(i < nu_ref[0])
    def _():
        @pl.when((i == 0) | (prev != e))
        def _():
            wup_bf[...] = wup_ref[...].astype(BF16)
            wdn_bf[...] = wdn_ref[...].astype(BF16)

        up = _dot(xs_ref[...].astype(BF16), wup_bf[...]) + bup_ref[...]
        x_glu = jnp.minimum(up[:, :D_FF], SWIGLU_LIMIT)
        x_lin = jnp.clip(up[:, D_FF:], -SWIGLU_LIMIT, SWIGLU_LIMIT)
        act = (x_lin + 1.0) * (x_glu * jax.nn.sigmoid(SWIGLU_ALPHA * x_glu))
        ys_ref[...] = _dot(act.astype(BF16), wdn_bf[...]) + bdn_ref[...]

    @pl.when(i >= nu_ref[0])
    def _():
        ys_ref[...] = jnp.zeros_like(ys_ref)


def _experts(xs, block_e, n_used, w_up, b_up, w_down, b_down):
    n_rows, d = xs.shape
    nblk = n_rows // MOE_BLK
    rowmap = lambda i, be, nu: (jnp.minimum(i, nu[0] - 1), 0)
    emap = lambda i, be, nu: (be[i], 0, 0)
    return pl.pallas_call(
        _expert_kernel,
        out_shape=jax.ShapeDtypeStruct((n_rows, d), F32),
        grid_spec=pltpu.PrefetchScalarGridSpec(
            num_scalar_prefetch=2, grid=(nblk,),
            in_specs=[pl.BlockSpec((MOE_BLK, d), rowmap),
                      pl.BlockSpec((None, d, 2 * D_FF), emap), pl.BlockSpec((None, 1, 2 * D_FF), emap),
                      pl.BlockSpec((None, D_FF, d), emap), pl.BlockSpec((None, 1, d), emap)],
            out_specs=pl.BlockSpec((MOE_BLK, d), lambda i, be, nu: (i, 0)),
            scratch_shapes=[pltpu.VMEM((d, 2 * D_FF), BF16), pltpu.VMEM((D_FF, d), BF16)]),
        compiler_params=_cparams(("arbitrary",)),
        name="moe_experts",
    )(block_e, n_used, xs, w_up, b_up, w_down, b_down)


def _combine_kernel(meta_ref, metan_ref, x1_ref, col_ref, gate_ref, gt_ref, gfin_ref, ys_ref, y_ref, ybuf_ref, sems,
                    *, tm):
    i = pl.program_id(0)
    nt = pl.num_programs(0)
    nb, tt, d = x1_ref.shape
    rows = ybuf_ref.shape[1]

    def run_copies(m_ref, slot, start):
        def per_expert(e, carry):
            a = m_ref[0, 0, e]
            n_chunks = m_ref[0, 0, N_EXPERTS + e]
            bo = m_ref[0, 0, 2 * N_EXPERTS + e]

            def per_chunk(c, carry2):
                src = ys_ref.at[pl.ds(pl.multiple_of(a + 8 * c, 8), 8)]
                dst = ybuf_ref.at[slot, pl.ds(pl.multiple_of(bo + 8 * c, 8), 8)]
                copy = pltpu.make_async_copy(src, dst, sems.at[slot])
                if start:
                    copy.start()
                else:
                    copy.wait()
                return carry2

            lax.fori_loop(0, n_chunks, per_chunk, 0)
            return carry

        lax.fori_loop(0, N_EXPERTS, per_expert, 0)

    @pl.when(i == 0)
    def _():
        ybuf_ref[...] = jnp.zeros_like(ybuf_ref)
        run_copies(meta_ref, 0, True)

    @pl.when(i + 1 < nt)
    def _():
        run_copies(metan_ref, (i + 1) % 2, True)

    slot = i % 2
    run_copies(meta_ref, slot, False)
    col = col_ref[...]
    gate = gate_ref[...]
    lane = lax.broadcasted_iota(I32, (tm, rows), 1)
    pick = jnp.where(lane == col[:, 0:1], gate[:, 0:1], 0.0)
    for kk in range(1, TOP_K):
        pick = pick + jnp.where(lane == col[:, kk:kk + 1], gate[:, kk:kk + 1], 0.0)
    moe = _dot(pick.astype(BF16), ybuf_ref[slot].astype(BF16))
    x2 = x1_ref[...] + gt_ref[...] * moe.reshape(nb, tt, d)
    ms = jnp.mean(x2 * x2, axis=-1, keepdims=True)
    y_ref[...] = x2 * lax.rsqrt(ms + NORM_EPS) * gfin_ref[...]


def _combine_plan(idx, dest, off, before0, tm):
    n = idx.shape[0]
    nt = n // tm
    experts = jnp.arange(N_EXPERTS, dtype=I32)
    chose = jnp.any(idx[:, :, None] == experts, axis=1).astype(I32)
    cnt = chose.reshape(nt, tm, N_EXPERTS).sum(axis=1)
    before = before0[None, :] + jnp.cumsum(cnt, axis=0) - cnt
    start = off[None, :] + before
    a8 = start // 8 * 8
    b8 = jnp.where(cnt > 0, (start + cnt + 7) // 8 * 8, a8)
    span = b8 - a8
    boff = jnp.cumsum(span, axis=1) - span
    delta = jnp.repeat(boff - a8, tm, axis=0)
    col = dest + jnp.take_along_axis(delta, idx, axis=1)
    meta = jnp.concatenate([a8, span // 8, boff, jnp.zeros_like(a8)], axis=1).astype(I32)
    return meta.reshape(nt, 1, 4 * N_EXPERTS), col.astype(I32)


def _combine(x1, gate, idx, dest, off, before0, ys, gt, g_final, tm):
    b, t, d = x1.shape
    n = b * t
    tm = min(tm, n)
    tt = min(t, tm)
    nb = tm // tt
    tpb = t // tt
    nt = n // tm
    rows = -(-(tm * TOP_K + 14 * N_EXPERTS) // 256) * 256
    meta, col = _combine_plan(idx, dest, off, before0, tm)

    def xmap(i):
        return (i // tpb, i % tpb, 0) if nb == 1 else (i, 0, 0)

    def bmap(i):
        return (i // tpb, 0, 0) if nb == 1 else (i, 0, 0)

    smem = lambda imap: pl.BlockSpec((1, 1, 4 * N_EXPERTS), imap, memory_space=pltpu.SMEM)
    return pl.pallas_call(
        functools.partial(_combine_kernel, tm=tm),
        out_shape=jax.ShapeDtypeStruct((b, t, d), F32),
        grid=(nt,),
        in_specs=[smem(lambda i: (i, 0, 0)), smem(lambda i: (jnp.minimum(i + 1, nt - 1), 0, 0)),
                  pl.BlockSpec((nb, tt, d), xmap),
                  pl.BlockSpec((tm, TOP_K), lambda i: (i, 0)),
                  pl.BlockSpec((tm, TOP_K), lambda i: (i, 0)),
                  pl.BlockSpec((nb, 1, d), bmap),
                  pl.BlockSpec((1, d), lambda i: (0, 0)),
                  pl.BlockSpec(memory_space=pl.ANY)],
        out_specs=pl.BlockSpec((nb, tt, d), xmap),
        scratch_shapes=[pltpu.VMEM((2, rows, d), F32), pltpu.SemaphoreType.DMA((2,))],
        compiler_params=_cparams(("arbitrary",)),
        name="moe_combine",
    )(meta, meta, x1, col, gate, gt, g_final, ys)


def _moe(streams, counts, g_final, w_up, b_up, w_down, b_down):
    n_total = sum(s["h2"].shape[0] for s in streams)
    cnt = counts[:, 0].astype(I32)
    padded = (cnt + MOE_BLK - 1) // MOE_BLK * MOE_BLK
    pad_end = jnp.cumsum(padded)
    off = (pad_end - padded).astype(I32)
    n_blocks = -(-n_total * TOP_K // MOE_BLK) + N_EXPERTS
    n_rows = n_blocks * MOE_BLK
    starts = jnp.arange(n_blocks, dtype=I32) * MOE_BLK
    block_e = jnp.minimum(jnp.sum((pad_end[None, :] <= starts[:, None]).astype(I32), axis=1), N_EXPERTS - 1)
    n_used = (pad_end[-1:] // MOE_BLK).astype(I32)
    zstart = jnp.where(padded > 0, pad_end - MOE_BLK, -1).astype(I32)
    experts = jnp.arange(N_EXPERTS, dtype=I32)
    idxs, dests = [], []
    for s in streams:
        idx = s["idx"].T
        idxs.append(idx)
        dests.append((jnp.sum(jnp.where(idx[:, :, None] == experts, off, 0), axis=-1) + s["rank"].T).astype(I32))
    tm = min([512] + [s["h2"].shape[0] for s in streams])
    xs = _dispatch([s["h2"] for s in streams], jnp.concatenate(dests, axis=0), zstart, n_used, n_rows, tm)
    ys = _experts(xs, block_e, n_used, w_up, b_up, w_down, b_down)
    return [_combine(s["x1"], s["gate"].T, idx, dest, off, s["before"], ys, s["gt"], g_final, 256)
            for s, idx, dest in zip(streams, idxs, dests)]


def _stream(x, mod, w, ssm_ops, s0, cache, params, counts_in):
    b, t, d = x.shape
    n = b * t
    sh_mix, sc_mix, gt_mix, sh_ffn, sc_ffn, gt_ffn = mod
    prompt = cache is None
    tm = 512
    u4, q, kt, vt, lft, vtb, kaug = _inproj(x, sh_mix, sc_mix, params["g_mix"], w, prompt=prompt, tm=tm)
    n_chunks = t // SSM_CHUNK
    y4, s_fin = _ssm(u4, ssm_ops, s0, b, n_chunks)
    if prompt:
        att = _attn_p(q, kaug, vtb, b, t)
    else:
        cache_k, cache_v, cache_lf = cache
        p = cache_k.shape[1]
        ck_t = jnp.transpose(cache_k, (0, 2, 3, 1)).reshape(b, D_ATT, p)
        cv_t = jnp.transpose(cache_v, (0, 2, 3, 1)).reshape(b, D_ATT, p)
        cl_t = jnp.transpose(cache_lf, (0, 2, 1))
        ln_t = jnp.pad(lft, ((0, 0), (0, 0), (0, LANES - t)))
        fc, fn = _fprep_s(cl_t, ln_t)
        att = _attn_s(q.reshape(b, t, D_ATT), ck_t, cv_t, kt, vt, fc, fn).reshape(n, D_ATT)
    x1, h2, idx, gate, rank, counts = _outproj(x, y4, att, gt_mix, sh_ffn, sc_ffn, params["g_ffn"], w, counts_in,
                                               att_transposed=prompt, tm=512)
    routed = {"x1": x1, "h2": h2, "idx": idx, "gate": gate, "rank": rank, "gt": gt_ffn,
              "before": counts_in[:, 0].astype(I32)}
    heads = lambda a: jnp.transpose(a.reshape(b, N_HEADS, HEAD_DIM, t), (0, 3, 1, 2))[None]
    return routed, counts, heads(kt), heads(vt), jnp.transpose(lft, (0, 2, 1))[None], s_fin


def _state_in(re, im):
    b = re.shape[0]
    s = jnp.concatenate([re.reshape(b, N_OCT, OCT_STATE // 2), im.reshape(b, N_OCT, OCT_STATE // 2)], axis=-1)
    return jnp.transpose(s, (1, 0, 2))


def _state_out(s):
    b = s.shape[1]
    s = jnp.transpose(s, (1, 0, 2))
    half = OCT_STATE // 2
    re = s[:, :, :half].reshape(1, b, N_SSM_GROUPS, SSM_STATE)
    im = s[:, :, half:].reshape(1, b, N_SSM_GROUPS, SSM_STATE)
    return re, im


def kernel(x_prompt, x_sample, c_prompt, c_sample, cache_k, cache_v, cache_logf, state_ssm_re, state_ssm_im, w_ada, b_ada, g_mix, w_in, b_forget, ssm_log_dt, ssm_a_re, ssm_a_im, ssm_b_re, ssm_b_im, ssm_c_re, ssm_c_im, ssm_d, w_glu, b_glu, w_out, g_ffn, w_router, b_router, w_up, b_up, w_down, b_down, g_final):
    assert w_ada.shape[0] == 1, "single-layer trunk"
    d = D_MODEL
    bp, tp, _ = x_prompt.shape
    bs, ts, _ = x_sample.shape
    n_c = bp + bs
    c_all = jnp.pad(jnp.concatenate([c_prompt, c_sample], axis=0), ((0, -n_c % 8), (0, 0)))
    mod = _ada(c_all, w_ada[0], b_ada[0].reshape(1, 6 * d))
    mod_p = [mod[:bp, j * d:(j + 1) * d].reshape(bp, 1, d) for j in range(6)]
    mod_s = [mod[bp:n_c, j * d:(j + 1) * d].reshape(bs, 1, d) for j in range(6)]
    wi = w_in[0]
    o1, o2, o3, o4 = D_SSM, D_SSM + D_ATT, D_SSM + 2 * D_ATT, D_SSM + 3 * D_ATT
    w_q = wi[:, o1:o2].astype(BF16)
    w_k = wi[:, o2:o3].astype(BF16)
    unused = jnp.zeros((8, LANES), BF16)
    base = {"wu": wi[:, :o1].astype(BF16), "wkt": w_k.T, "wvt": wi[:, o3:o4].astype(BF16).T,
            "wft": wi[:, o4:].astype(BF16).T, "bfc": b_forget[0].reshape(N_HEADS, 1), "pm": _bias_placement(),
            "wglu": w_glu[0].astype(BF16), "bglu": b_glu[0].reshape(1, D_SSM), "wout": w_out[0].astype(BF16),
            "wr": w_router[0].T, "br": b_router[0].reshape(N_EXPERTS, 1)}
    w_p = dict(base, wq=w_q.T, wk=w_k)
    w_s = dict(base, wq=w_q, wk=unused)
    params = {"g_mix": g_mix[0].reshape(1, d), "g_ffn": g_ffn[0].reshape(1, d), "g_final": g_final.reshape(1, d),
              "w_up": w_up[0], "b_up": b_up[0].reshape(N_EXPERTS, 1, 2 * D_FF),
              "w_down": w_down[0], "b_down": b_down[0].reshape(N_EXPERTS, 1, d)}
    ssm_ops = _ssm_operators(ssm_log_dt[0], ssm_a_re[0], ssm_a_im[0], ssm_b_re[0], ssm_b_im[0],
                             ssm_c_re[0], ssm_c_im[0], ssm_d[0])
    zero_state = jnp.zeros((N_OCT, bp, OCT_STATE), F32)
    counts0 = jnp.zeros((N_EXPERTS, LANES), F32)
    r_p, counts_p, k_p, v_p, f_p, s_p = _stream(x_prompt, mod_p, w_p, ssm_ops, zero_state, None, params, counts0)
    s0 = _state_in(state_ssm_re[0], state_ssm_im[0])
    r_s, counts, k_s, v_s, f_s, s_s = _stream(x_sample, mod_s, w_s, ssm_ops, s0,
                                              (cache_k[0], cache_v[0], cache_logf[0]), params, counts_p)
    y_p, y_s = _moe([r_p, r_s], counts, params["g_final"],
                    params["w_up"], params["b_up"], params["w_down"], params["b_down"])
    sre_p, sim_p = _state_out(s_p)
    sre_s, sim_s = _state_out(s_s)
    return (y_p, y_s, k_p, v_p, f_p, sre_p, sim_p, k_s, v_s, f_s, sre_s, sim_s)
```

```python
import functools
import math

import jax
import jax.numpy as jnp
from jax import lax
from jax.experimental import pallas as pl
from jax.experimental.pallas import tpu as pltpu

F32 = jnp.float32
BF16 = jnp.bfloat16
I32 = jnp.int32

D_MODEL = 1024
D_SSM = 512
SSM_GROUP = 16
N_SSM_GROUPS = 32
SSM_STATE = 64
D_ATT = 512
HEAD_DIM = 64
N_HEADS = 8
N_EXPERTS = 32
TOP_K = 4
D_FF = 1024
SWIGLU_LIMIT = 7.0
SWIGLU_ALPHA = 1.702
NORM_EPS = 1e-5

LANES = 128
N_OCT = D_SSM // LANES
OCT_STATE = 2 * (N_SSM_GROUPS // N_OCT) * SSM_STATE
SSM_CHUNK = 8
SSM_ROWS = 256
SSM_INTERLEAVE = 4
ATT_BLK = 256
ATT_PAIRS = 4
LOG2E = math.log2(math.e)
SAMPLE_KV_BLK = 2048
MOE_BLK = 256
VMEM_LIMIT = 52 * 1024 * 1024


def _cparams(sem, vmem=VMEM_LIMIT):
    return pltpu.CompilerParams(dimension_semantics=sem, vmem_limit_bytes=vmem)


def _dot(a, b):
    return jnp.dot(a, b, preferred_element_type=F32)


def _dot_nt(a, b):
    return lax.dot_general(a, b, (((1,), (1,)), ((), ())), preferred_element_type=F32)


def _dot_hi(a, b):
    return jnp.dot(a, b, preferred_element_type=F32, precision=lax.Precision.HIGHEST)


def _ada_kernel(c_ref, w_ref, b_ref, o_ref):
    c = c_ref[...]
    s = c * jax.nn.sigmoid(c)
    o_ref[...] = _dot_hi(s, w_ref[...]) + b_ref[...]


def _ada(c_all, w_ada, b_ada):
    m, d = c_all.shape
    n = w_ada.shape[1]
    return pl.pallas_call(
        _ada_kernel,
        out_shape=jax.ShapeDtypeStruct((m, n), F32),
        grid=(n // d,),
        in_specs=[pl.BlockSpec((m, d), lambda j: (0, 0)),
                  pl.BlockSpec((d, d), lambda j: (0, j)),
                  pl.BlockSpec((1, d), lambda j: (0, j))],
        out_specs=pl.BlockSpec((m, d), lambda j: (0, j)),
        compiler_params=_cparams(("arbitrary",)),
        name="ada",
    )(c_all, w_ada, b_ada)


def _norm_mod(x, g, scale, shift):
    ms = jnp.mean(x * x, axis=-1, keepdims=True)
    y = x * lax.rsqrt(ms + NORM_EPS) * g
    return y * (1.0 + scale) + shift


def _log_sigmoid(z):
    return jnp.minimum(z, 0.0) - jnp.log1p(jnp.exp(-jnp.abs(z)))


def _inproj_kernel(x_ref, sh_ref, sc_ref, g_ref, wu_ref, wq_ref, wkt_ref, wvt_ref, wft_ref, bfc_ref,
                   wk_ref, pm_ref,
                   u4_ref, q_ref, kt_ref, vt_ref, lft_ref, vtb_ref, ka_ref, us_ref, carry_ref, *, prompt, tpb):
    nb, tt, d = x_ref.shape
    tm = nb * tt
    h = _norm_mod(x_ref[...], g_ref[...], sc_ref[...], sh_ref[...]).reshape(tm, d).astype(BF16)
    u = _dot(h, wu_ref[...])
    for o in range(N_OCT):
        us_ref[o] = u[:, o * LANES:(o + 1) * LANES]
    for s in range(SSM_CHUNK):
        for o in range(N_OCT):
            piece = us_ref[o, pl.ds(s, tm // SSM_CHUNK, stride=SSM_CHUNK), :]
            u4_ref[o, :, s * LANES:(s + 1) * LANES] = piece.astype(BF16)
    kt = _dot_nt(wkt_ref[...], h)
    vt = _dot_nt(wvt_ref[...], h)
    lft = _log_sigmoid(_dot_nt(wft_ref[...], h) + bfc_ref[...])
    scale = HEAD_DIM ** -0.5
    if prompt:
        kt_ref[...] = kt
        vt_ref[...] = vt
        lft_ref[...] = lft
        qt = (_dot_nt(wq_ref[...], h) * (scale * LOG2E)).astype(BF16)
        vtb = vt.astype(BF16)
        for j in range(tm // ATT_BLK):
            q_ref[j] = qt[:, j * ATT_BLK:(j + 1) * ATT_BLK]
            vtb_ref[j] = vtb[:, j * ATT_BLK:(j + 1) * ATT_BLK]
        k_tok = _dot(h, wk_ref[...])

        @pl.when(pl.program_id(0) % tpb == 0)
        def _():
            carry_ref[...] = jnp.zeros_like(carry_ref)

        cum = _lane_cumsum(lft) + carry_ref[:, 0:1]
        carry_ref[...] = jnp.broadcast_to(cum[:, tm - 1:tm], carry_ref.shape)
        nf = cum * (-LOG2E)
        hi = nf.astype(BF16).astype(F32)
        r1 = nf - hi
        mid = r1.astype(BF16).astype(F32)
        lo = (r1 - mid).astype(BF16).astype(F32)
        place = lambda piece, pm: lax.dot_general(piece, pm, (((0,), (0,)), ((), ())), preferred_element_type=F32)
        slab = place(hi, pm_ref[0]) + place(mid, pm_ref[1]) + place(lo, pm_ref[2])
        for j in range(tm // ATT_BLK):
            rows = slice(j * ATT_BLK, (j + 1) * ATT_BLK)
            for p in range(N_HEADS // 2):
                ka_ref[j, :, 2 * p * LANES:(2 * p + 1) * LANES] = k_tok[rows, p * LANES:(p + 1) * LANES].astype(BF16)
                ka_ref[j, :, (2 * p + 1) * LANES:(2 * p + 2) * LANES] = (
                    slab[rows, p * LANES:(p + 1) * LANES].astype(BF16))
    else:
        for bl in range(nb):
            kt_ref[bl] = kt[:, bl * tt:(bl + 1) * tt]
            vt_ref[bl] = vt[:, bl * tt:(bl + 1) * tt]
            lft_ref[bl] = lft[:, bl * tt:(bl + 1) * tt]
        q_ref[...] = (_dot(h, wq_ref[...]) * scale).astype(BF16)
        vtb_ref[...] = jnp.zeros_like(vtb_ref)
        ka_ref[...] = jnp.zeros_like(ka_ref)


def _inproj(x, shift, scale, g_mix, w, *, prompt, tm):
    b, t, d = x.shape
    n = b * t
    tm = min(tm, n)
    tt = min(t, tm)
    nb = tm // tt
    tpb = t // tt
    assert (nb == 1) == prompt
    grid = (n // tm,)

    def xmap(i):
        return (i // tpb, i % tpb, 0) if nb == 1 else (i, 0, 0)

    def bmap(i):
        return (i // tpb, 0, 0) if nb == 1 else (i, 0, 0)

    def tmap(i):
        return (i // tpb, 0, i % tpb) if nb == 1 else (i, 0, 0)

    full = lambda a: pl.BlockSpec(a.shape, lambda i: (0,) * a.ndim)
    names = ("wu", "wq", "wkt", "wvt", "wft", "bfc", "wk", "pm")
    in_specs = [pl.BlockSpec((nb, tt, d), xmap),
                pl.BlockSpec((nb, 1, d), bmap), pl.BlockSpec((nb, 1, d), bmap),
                pl.BlockSpec((1, d), lambda i: (0, 0))] + [full(w[k]) for k in names]
    tblk = lambda rows: pl.BlockSpec((None if nb == 1 else nb, rows, tt), tmap)
    dummy = (jax.ShapeDtypeStruct((8, LANES), F32), pl.BlockSpec((8, LANES), lambda i: (0, 0)))
    if prompt:
        nblk = n // ATT_BLK
        blocked = (jax.ShapeDtypeStruct((nblk, D_ATT, ATT_BLK), BF16),
                   pl.BlockSpec((tm // ATT_BLK, D_ATT, ATT_BLK), lambda i: (i, 0, 0)))
        q_out, vtb_out = blocked, blocked
        ka_out = (jax.ShapeDtypeStruct((nblk, ATT_BLK, 2 * D_ATT), BF16),
                  pl.BlockSpec((tm // ATT_BLK, ATT_BLK, 2 * D_ATT), lambda i: (i, 0, 0)))
    else:
        q_out = (jax.ShapeDtypeStruct((n, D_ATT), BF16), pl.BlockSpec((tm, D_ATT), lambda i: (i, 0)))
        vtb_out = ka_out = dummy
    kdim = SSM_CHUNK * LANES
    outs = [(jax.ShapeDtypeStruct((N_OCT, n // SSM_CHUNK, kdim), BF16),
             pl.BlockSpec((N_OCT, tm // SSM_CHUNK, kdim), lambda i: (0, i, 0))),
            q_out,
            (jax.ShapeDtypeStruct((b, D_ATT, t), F32), tblk(D_ATT)),
            (jax.ShapeDtypeStruct((b, D_ATT, t), F32), tblk(D_ATT)),
            (jax.ShapeDtypeStruct((b, N_HEADS, t), F32), tblk(N_HEADS)),
            vtb_out, ka_out]
    return pl.pallas_call(
        functools.partial(_inproj_kernel, prompt=prompt, tpb=tpb),
        out_shape=tuple(o[0] for o in outs), grid=grid, in_specs=in_specs, out_specs=tuple(o[1] for o in outs),
        scratch_shapes=[pltpu.VMEM((N_OCT, tm, LANES), F32), pltpu.VMEM((8, LANES), F32)],
        compiler_params=_cparams(("arbitrary",)),
        name="inproj_p" if prompt else "inproj_s",
    )(x, shift, scale, g_mix, *[w[k] for k in names])


def _ssm_operators(log_dt, a_re, a_im, b_re, b_im, c_re, c_im, d_skip):
    L, G, P, C = SSM_CHUNK, N_SSM_GROUPS, SSM_STATE, SSM_GROUP
    q = G // N_OCT
    lam = lax.complex(a_re, a_im)
    dt = jnp.exp(log_dt)[:, None]
    lam_dt = lam * dt
    lam_bar = jnp.exp(lam_dt)
    b_bar = ((lam_bar - 1.0) / lam)[:, :, None] * lax.complex(b_re, b_im)
    c_mat = lax.complex(c_re, c_im)
    pw = jnp.exp(lam_dt[None] * jnp.arange(L + 1, dtype=F32)[:, None, None])
    kern = jnp.real(jnp.einsum('gcp,dgp,gpk->dgck', c_mat, pw[:L], b_bar))
    ksm = jnp.transpose(kern, (1, 0, 3, 2)).reshape(N_OCT, q, L, C, C)
    ksm = jnp.transpose(ksm, (0, 2, 1, 3, 4)).reshape(N_OCT, L, q * C, C)
    pwr = pw[L - 1 - jnp.arange(L)]
    wb = jnp.einsum('jgp,gpk->jgkp', pwr, b_bar)
    wsm = jnp.stack([jnp.real(wb), jnp.imag(wb)], axis=0).reshape(2, L, N_OCT, q * C, P)
    wsm = jnp.transpose(wsm, (2, 1, 0, 3, 4))
    cl = c_mat[None] * pw[1:L + 1][:, :, None, :]
    vsm = jnp.stack([jnp.real(cl), -jnp.imag(cl)], axis=0).reshape(2, L, N_OCT, q, C, P)
    vsm = jnp.transpose(vsm, (2, 1, 0, 3, 5, 4)).reshape(N_OCT, L, OCT_STATE, C)
    tmat, wend, win = _ssm_prep(ksm, wsm, vsm, d_skip.reshape(N_OCT, 1, LANES))
    dec = pw[L].reshape(N_OCT, q * P)
    decay = jnp.concatenate([jnp.real(dec), jnp.imag(dec)], axis=-1).reshape(N_OCT, 1, OCT_STATE)
    return tmat, wend, win, decay.astype(F32)


def _ssm_prep_kernel(ksm_ref, wsm_ref, vsm_ref, d_ref, tmat_ref, wend_ref, win_ref):
    L, C, P = SSM_CHUNK, SSM_GROUP, SSM_STATE
    half = OCT_STATE // 2

    def spread(period, width):
        r = lax.broadcasted_iota(I32, (period, width), 0)
        c = lax.broadcasted_iota(I32, (period, width), 1)
        return jnp.where(c % period == r, 1.0, 0.0)

    def same_group(shape, row_div, row_mod, lane_div):
        r = lax.broadcasted_iota(I32, shape, 0)
        c = lax.broadcasted_iota(I32, shape, 1)
        return (r % row_mod) // row_div == c // lane_div

    e_c = spread(C, LANES)
    e_p = spread(P, half)
    grp = same_group((LANES, LANES), C, LANES, C)
    rr = lax.broadcasted_iota(I32, (LANES, LANES), 0)
    cc = lax.broadcasted_iota(I32, (LANES, LANES), 1)
    lag = []
    for d in range(L):
        blk = jnp.where(grp, _dot_hi(ksm_ref[d], e_c), 0.0)
        if d == 0:
            blk = blk + jnp.where(rr == cc, jnp.broadcast_to(d_ref[...], (LANES, LANES)), 0.0)
        lag.append(blk.astype(BF16))
    zero = jnp.zeros((LANES, LANES), BF16)
    for j in range(L):
        for t in range(L):
            tmat_ref[j * LANES:(j + 1) * LANES, t * LANES:(t + 1) * LANES] = lag[t - j] if t >= j else zero
    grp_e = same_group((LANES, half), C, LANES, P)
    for j in range(L):
        for part in range(2):
            blk = jnp.where(grp_e, _dot_hi(wsm_ref[j, part], e_p), 0.0)
            wend_ref[j * LANES:(j + 1) * LANES, part * half:(part + 1) * half] = blk.astype(BF16)
    grp_i = same_group((OCT_STATE, LANES), P, half, C)
    for t in range(L):
        blk = jnp.where(grp_i, _dot_hi(vsm_ref[t], e_c), 0.0)
        win_ref[:, t * LANES:(t + 1) * LANES] = blk.astype(BF16)


def _ssm_prep(ksm, wsm, vsm, dvec):
    kdim = SSM_CHUNK * LANES
    blk = lambda a: pl.BlockSpec((None,) + a.shape[1:], lambda o: (o,) + (0,) * (a.ndim - 1))
    out = lambda r, c: (jax.ShapeDtypeStruct((N_OCT, r, c), BF16), pl.BlockSpec((None, r, c), lambda o: (o, 0, 0)))
    outs = [out(kdim, kdim), out(kdim, OCT_STATE), out(OCT_STATE, kdim)]
    return pl.pallas_call(
        _ssm_prep_kernel,
        out_shape=tuple(o[0] for o in outs),
        grid=(N_OCT,),
        in_specs=[blk(ksm), blk(wsm), blk(vsm), blk(dvec)],
        out_specs=tuple(o[1] for o in outs),
        compiler_params=_cparams(("arbitrary",)),
        name="ssm_prep",
    )(ksm, wsm, vsm, dvec)


def _ssm_kernel(u_ref, tmat_ref, wend_ref, win_ref, dec_ref, s0_ref, y_ref, sfin_ref, e_ref, sp_ref, st_ref,
                *, nb, cpt):
    i = pl.program_id(1)
    half = OCT_STATE // 2
    rows = nb * cpt

    @pl.when(i == 0)
    def _():
        st_ref[...] = s0_ref[...]

    u = u_ref[...].reshape(rows, u_ref.shape[-1])
    e_ref[...] = _dot(u, wend_ref[...])
    dec = dec_ref[...]
    ar, ai = dec[:, :half], dec[:, half:]

    def advance(row, st):
        sp_ref[pl.ds(row, 1), :] = st
        e = e_ref[pl.ds(row, 1), :]
        re, im = st[:, :half], st[:, half:]
        nre = ar * re - ai * im + e[:, :half]
        nim = ar * im + ai * re + e[:, half:]
        return jnp.concatenate([nre, nim], axis=-1)

    if nb <= SSM_INTERLEAVE:
        def per_chunk(r, sts):
            return tuple(advance(bb * cpt + r, sts[bb]) for bb in range(nb))

        sts = lax.fori_loop(0, cpt, per_chunk, tuple(st_ref[bb:bb + 1, :] for bb in range(nb)))
        for bb in range(nb):
            st_ref[bb:bb + 1, :] = sts[bb]
    else:
        def per_batch(bb, carry):
            st = lax.fori_loop(0, cpt, lambda r, st: advance(bb * cpt + r, st), st_ref[pl.ds(bb, 1), :])
            st_ref[pl.ds(bb, 1), :] = st
            return carry

        lax.fori_loop(0, nb, per_batch, 0)
    y = _dot(u, tmat_ref[...]) + _dot(sp_ref[...].astype(BF16), win_ref[...])
    y_ref[...] = jax.nn.gelu(y).astype(BF16).reshape(y_ref.shape)
    sfin_ref[...] = st_ref[...]


def _ssm(u4, ops, s0, batch, n_chunks):
    tmat, wend, win, decay = ops
    kdim = SSM_CHUNK * LANES
    cpt = SSM_ROWS if n_chunks % SSM_ROWS == 0 else n_chunks
    tiles = n_chunks // cpt
    if tiles > 1:
        u_in = u4.reshape(N_OCT, batch, n_chunks, kdim)
        u_spec = pl.BlockSpec((None, batch, cpt, kdim), lambda o, i: (o, 0, i, 0))
    else:
        u_in = u4
        u_spec = pl.BlockSpec((None, batch * n_chunks, kdim), lambda o, i: (o, 0, 0))
    r = batch * cpt
    wspec = lambda shape: pl.BlockSpec((None,) + shape, lambda o, i: (o, 0, 0))
    y, s_fin = pl.pallas_call(
        functools.partial(_ssm_kernel, nb=batch, cpt=cpt),
        out_shape=(jax.ShapeDtypeStruct(u_in.shape, BF16),
                   jax.ShapeDtypeStruct((N_OCT, batch, OCT_STATE), F32)),
        grid=(N_OCT, tiles),
        in_specs=[u_spec,
                  wspec((kdim, kdim)), wspec((kdim, OCT_STATE)), wspec((OCT_STATE, kdim)), wspec((1, OCT_STATE)),
                  wspec((batch, OCT_STATE))],
        out_specs=(u_spec, wspec((batch, OCT_STATE))),
        scratch_shapes=[pltpu.VMEM((r, OCT_STATE), F32), pltpu.VMEM((r, OCT_STATE), F32),
                        pltpu.VMEM((batch, OCT_STATE), F32)],
        compiler_params=_cparams(("arbitrary", "arbitrary")),
        name="ssm",
    )(u_in, tmat, wend, win, decay, s0)
    return y.reshape(u4.shape), s_fin


def _bias_placement():
    import numpy as np
    pm = np.zeros((3, N_HEADS, N_HEADS // 2 * LANES), np.float32)
    for piece in range(3):
        for h in range(N_HEADS):
            pm[piece, h, LANES * (h // 2) + 3 * (h % 2) + piece] = 1.0
    return jnp.asarray(pm)


def _attn_p_kernel(qt_ref, ka_ref, vt_ref, o_ref, qbd_ref, acc_ref, sa_ref, sb_ref):
    i = pl.program_id(2)
    bq = ATT_BLK
    hd = HEAD_DIM
    r = lax.broadcasted_iota(I32, (LANES, 2 * bq), 0)
    c = lax.broadcasted_iota(I32, (LANES, 2 * bq), 1)
    ones = jnp.where(((r < 3) & (c < bq)) | ((r >= 3) & (r < 6) & (c >= bq)), 1.0, 0.0).astype(BF16)
    zero = jnp.zeros((hd, bq), BF16)
    for pp in range(ATT_PAIRS):
        qb = qt_ref[pp * LANES:(pp + 1) * LANES, :]
        qbd_ref[pp, 0:hd, 0:bq] = qb[0:hd]
        qbd_ref[pp, 0:hd, bq:2 * bq] = zero
        qbd_ref[pp, hd:2 * hd, 0:bq] = zero
        qbd_ref[pp, hd:2 * hd, bq:2 * bq] = qb[hd:2 * hd]
        qbd_ref[pp, 2 * hd:2 * hd + LANES, :] = ones
    acc_ref[...] = jnp.zeros_like(acc_ref)

    nq = ka_ref.shape[0]

    def scores(s_ref, j):
        for pp in range(ATT_PAIRS):
            s_ref[pp] = _dot(ka_ref[j, :, pp * 2 * LANES:(pp + 1) * 2 * LANES], qbd_ref[pp])

    def attend(s_ref, j, carry, masked):
        out = []
        for pp in range(ATT_PAIRS):
            m_prev, l_prev = carry[2 * pp], carry[2 * pp + 1]
            s = s_ref[pp]
            if masked:
                key = j * bq + lax.broadcasted_iota(I32, s.shape, 0)
                qq = lax.broadcasted_iota(I32, s.shape, 1)
                qq = i * bq + jnp.where(qq >= bq, qq - bq, qq)
                s = jnp.where(key <= qq, s, -jnp.inf)
            m_new = jnp.maximum(m_prev, jnp.max(s, axis=0, keepdims=True))
            alpha = jnp.exp2(m_prev - m_new)
            p = jnp.exp2(s - m_new)
            l_new = alpha * l_prev + jnp.sum(p, axis=0, keepdims=True)
            pb = p.astype(BF16)
            vb = vt_ref[jnp.minimum(j, nq - 1), pp * LANES:(pp + 1) * LANES, :]
            r0 = pp * LANES
            acc_ref[r0:r0 + hd] = alpha[:, 0:bq] * acc_ref[r0:r0 + hd] + _dot(vb[0:hd], pb[:, 0:bq])
            acc_ref[r0 + hd:r0 + 2 * hd] = (alpha[:, bq:2 * bq] * acc_ref[r0 + hd:r0 + 2 * hd]
                                            + _dot(vb[hd:2 * hd], pb[:, bq:2 * bq]))
            out.extend((m_new, l_new))
        return tuple(out)

    def double_step(m, carry):
        j = 2 * m
        scores(sb_ref, j + 1)
        carry = attend(sa_ref, j, carry, False)
        scores(sa_ref, j + 2)
        return attend(sb_ref, j + 1, carry, False)

    init = (jnp.full((1, 2 * bq), -jnp.inf, F32), jnp.zeros((1, 2 * bq), F32)) * ATT_PAIRS
    scores(sa_ref, 0)
    carry = lax.fori_loop(0, i // 2, double_step, init)
    j = 2 * (i // 2)
    scores(sb_ref, jnp.minimum(j + 1, nq - 1))
    carry = attend(sa_ref, j, carry, True)
    carry = attend(sb_ref, j + 1, carry, True)
    for pp in range(ATT_PAIRS):
        l = carry[2 * pp + 1]
        r0 = pp * LANES
        o_ref[r0:r0 + hd] = acc_ref[r0:r0 + hd] / l[:, 0:bq]
        o_ref[r0 + hd:r0 + 2 * hd] = acc_ref[r0 + hd:r0 + 2 * hd] / l[:, bq:2 * bq]


def _attn_p(qt, kaug, vt, batch, t):
    blk = ATT_BLK
    nq = t // blk
    ngrp = N_HEADS // 2 // ATT_PAIRS
    rows = ATT_PAIRS * LANES
    qt4 = qt.reshape(batch, nq, D_ATT, blk)
    vt4 = vt.reshape(batch, nq, D_ATT, blk)
    ka4 = kaug.reshape(batch, nq, blk, 2 * D_ATT)
    return pl.pallas_call(
        _attn_p_kernel,
        out_shape=jax.ShapeDtypeStruct((batch, D_ATT, t), F32),
        grid=(batch, ngrp, nq),
        in_specs=[pl.BlockSpec((None, None, rows, blk), lambda b, p, i: (b, i, p, 0)),
                  pl.BlockSpec((None, nq, blk, 2 * rows), lambda b, p, i: (b, 0, 0, p), pipeline_mode=pl.Buffered(1)),
                  pl.BlockSpec((None, nq, rows, blk), lambda b, p, i: (b, 0, p, 0), pipeline_mode=pl.Buffered(1))],
        out_specs=pl.BlockSpec((None, rows, blk), lambda b, p, i: (b, p, i)),
        scratch_shapes=[pltpu.VMEM((ATT_PAIRS, 2 * LANES, 2 * blk), BF16), pltpu.VMEM((rows, blk), F32),
                        pltpu.VMEM((ATT_PAIRS, blk, 2 * blk), F32), pltpu.VMEM((ATT_PAIRS, blk, 2 * blk), F32)],
        compiler_params=_cparams(("arbitrary", "arbitrary", "arbitrary")),
        name="attn_p",
    )(qt4, ka4, vt4)


def _lane_cumsum(x):
    n = x.shape[-1]
    lane = lax.broadcasted_iota(I32, x.shape, x.ndim - 1)
    s = 1
    while s < n:
        x = x + jnp.where(lane >= s, pltpu.roll(x, s, x.ndim - 1), 0.0)
        s *= 2
    return x


def _fprep_s_kernel(cl_ref, ln_ref, fc_ref, fn_ref):
    b, h, p = cl_ref.shape
    cum = _lane_cumsum(cl_ref[...].reshape(b * h, p))
    fc_ref[...] = (-cum).reshape(b, h, p)
    total = cum[:, p - 1:p]
    cn = _lane_cumsum(ln_ref[...].reshape(b * h, LANES))
    fn_ref[...] = (-(total + cn)).reshape(b, h, LANES)


def _fprep_s(cache_lf_t, new_lf_t):
    b, h, p = cache_lf_t.shape
    return pl.pallas_call(
        _fprep_s_kernel,
        out_shape=(jax.ShapeDtypeStruct((b, h, p), F32), jax.ShapeDtypeStruct((b, h, LANES), F32)),
        grid=(1,),
        in_specs=[pl.BlockSpec((b, h, p), lambda i: (0, 0, 0)), pl.BlockSpec((b, h, LANES), lambda i: (0, 0, 0))],
        out_specs=(pl.BlockSpec((b, h, p), lambda i: (0, 0, 0)), pl.BlockSpec((b, h, LANES), lambda i: (0, 0, 0))),
        compiler_params=_cparams(("arbitrary",)),
        name="fprep_s",
    )(cache_lf_t, new_lf_t)


def _attn_s_kernel(q_ref, ck_ref, cv_ref, kn_ref, vn_ref, fc_ref, fn_ref, o_ref, qbd_ref, m_ref, l_ref, acc_ref,
                   *, tq):
    j = pl.program_id(1)
    nkv = pl.num_programs(1)
    rows = N_HEADS * tq
    bk = ck_ref.shape[1]

    @pl.when(j == 0)
    def _():
        q = q_ref[...]
        qrep = jnp.broadcast_to(q[None], (N_HEADS, tq, D_ATT)).reshape(rows, D_ATT)
        rh = lax.broadcasted_iota(I32, (rows, D_ATT), 0) // tq
        ch = lax.broadcasted_iota(I32, (rows, D_ATT), 1) // HEAD_DIM
        qbd_ref[...] = jnp.where(rh == ch, qrep, jnp.zeros_like(qrep))
        m_ref[...] = jnp.full_like(m_ref, -jnp.inf)
        l_ref[...] = jnp.zeros_like(l_ref)
        acc_ref[...] = jnp.zeros_like(acc_ref)

    def update(s, vt):
        m_prev = m_ref[...]
        m_new = jnp.maximum(m_prev, jnp.max(s, axis=1, keepdims=True))
        alpha = jnp.exp(m_prev - m_new)
        p = jnp.exp(s - m_new)
        l_ref[...] = alpha * l_ref[...] + jnp.sum(p, axis=1, keepdims=True)
        m_ref[...] = m_new
        acc_ref[...] = alpha * acc_ref[...] + _dot_nt(p.astype(BF16), vt)

    def bias(f, width):
        return jnp.broadcast_to(f[:, None, :], (N_HEADS, tq, width)).reshape(rows, width)

    s = _dot(qbd_ref[...], ck_ref[...].astype(BF16)) + bias(fc_ref[...], bk)
    update(s, cv_ref[...].astype(BF16))

    @pl.when(j == nkv - 1)
    def _():
        s2 = _dot(qbd_ref[...], kn_ref[...].astype(BF16)) + bias(fn_ref[...][:, 0:tq], tq)
        key = lax.broadcasted_iota(I32, (rows, tq), 1)
        qq = lax.broadcasted_iota(I32, (rows, tq), 0) % tq
        update(jnp.where(key <= qq, s2, -jnp.inf), vn_ref[...].astype(BF16))
        o = acc_ref[...] / l_ref[...]
        rh = lax.broadcasted_iota(I32, (rows, D_ATT), 0) // tq
        ch = lax.broadcasted_iota(I32, (rows, D_ATT), 1) // HEAD_DIM
        o = jnp.where(rh == ch, o, 0.0).reshape(N_HEADS, tq, D_ATT)
        o_ref[...] = jnp.sum(o, axis=0)


def _attn_s(q, cache_k, cache_v, k_new, v_new, fc, fn):
    b, tq, _ = q.shape
    p = cache_k.shape[2]
    bk = min(SAMPLE_KV_BLK, p)
    nkv = p // bk
    rows = N_HEADS * tq
    tok = lambda: pl.BlockSpec((None, tq, D_ATT), lambda bi, j: (bi, 0, 0))
    new = lambda: pl.BlockSpec((None, D_ATT, tq), lambda bi, j: (bi, 0, 0))
    return pl.pallas_call(
        functools.partial(_attn_s_kernel, tq=tq),
        out_shape=jax.ShapeDtypeStruct((b, tq, D_ATT), F32),
        grid=(b, nkv),
        in_specs=[tok(),
                  pl.BlockSpec((None, D_ATT, bk), lambda bi, j: (bi, 0, j)),
                  pl.BlockSpec((None, D_ATT, bk), lambda bi, j: (bi, 0, j)),
                  new(), new(),
                  pl.BlockSpec((None, N_HEADS, bk), lambda bi, j: (bi, 0, j)),
                  pl.BlockSpec((None, N_HEADS, LANES), lambda bi, j: (bi, 0, 0))],
        out_specs=tok(),
        scratch_shapes=[pltpu.VMEM((rows, D_ATT), BF16), pltpu.VMEM((rows, 1), F32),
                        pltpu.VMEM((rows, 1), F32), pltpu.VMEM((rows, D_ATT), F32)],
        compiler_params=_cparams(("arbitrary", "arbitrary")),
        name="attn_s",
    )(q, cache_k, cache_v, k_new, v_new, fc, fn)


def _outproj_kernel(x_ref, y4_ref, att_ref, gt_ref, sh_ref, sc_ref, gf_ref, wglu_ref, bglu_ref, wout_ref,
                    wr_ref, br_ref, cin_ref,
                    x1_ref, h2_ref, idx_ref, gate_ref, rank_ref, cnt_ref, carry_ref, ys_ref, *, att_transposed):
    i = pl.program_id(0)
    nb, tt, d = x_ref.shape
    tm = nb * tt

    @pl.when(i == 0)
    def _():
        carry_ref[...] = cin_ref[...]

    for s in range(SSM_CHUNK):
        for o in range(N_OCT):
            ys_ref[o, pl.ds(s, tm // SSM_CHUNK, stride=SSM_CHUNK), :] = (
                y4_ref[o, :, s * LANES:(s + 1) * LANES].astype(F32))
    ysf = jnp.concatenate([ys_ref[o] for o in range(N_OCT)], axis=-1)
    glu = ysf * jax.nn.sigmoid(_dot(ysf.astype(BF16), wglu_ref[...]) + bglu_ref[...])
    att = att_ref[...]
    if att_transposed:
        att = att.T
    mix = _dot(glu.astype(BF16), wout_ref[0:D_SSM, :]) + _dot(att.astype(BF16), wout_ref[D_SSM:, :])
    x1 = x_ref[...] + gt_ref[...] * mix.reshape(nb, tt, d)
    x1_ref[...] = x1
    h2 = _norm_mod(x1, gf_ref[...], sc_ref[...], sh_ref[...]).reshape(tm, d)
    h2_ref[...] = h2

    logits = lax.dot_general(wr_ref[...], h2, (((1,), (1,)), ((), ())), preferred_element_type=F32,
                             precision=lax.Precision.HIGHEST) + br_ref[...]
    sub = lax.broadcasted_iota(I32, logits.shape, 0)
    work = logits
    vals, idxs = [], []
    for _ in range(TOP_K):
        mx = jnp.max(work, axis=0, keepdims=True)
        ix = jnp.min(jnp.where(work == mx, sub, N_EXPERTS), axis=0, keepdims=True)
        vals.append(mx)
        idxs.append(ix)
        work = jnp.where(sub == ix, -jnp.inf, work)
    ex = [jnp.exp(v - vals[0]) for v in vals]
    den = ex[0] + ex[1] + ex[2] + ex[3]
    mh = jnp.where(work == -jnp.inf, 1.0, 0.0)
    r = lax.broadcasted_iota(I32, (tm, tm), 0)
    c = lax.broadcasted_iota(I32, (tm, tm), 1)
    earlier = jnp.where(r < c, 1.0, 0.0).astype(BF16)
    carry = carry_ref[...]
    before = _dot(mh.astype(BF16), earlier) + carry[:, 0:1]
    carry_ref[...] = carry + jnp.sum(mh, axis=1, keepdims=True)
    for kk in range(TOP_K):
        idx_ref[kk:kk + 1, :] = idxs[kk]
        gate_ref[kk:kk + 1, :] = ex[kk] / den
        rk = jnp.sum(jnp.where(sub == idxs[kk], before, 0.0), axis=0, keepdims=True)
        rank_ref[kk:kk + 1, :] = rk.astype(I32)
    cnt_ref[...] = carry_ref[...]


def _outproj(x, y4, att, gt, sh, sc, g_ffn, w, counts_in, *, att_transposed, tm):
    b, t, d = x.shape
    n = b * t
    tm = min(tm, n)
    tt = min(t, tm)
    nb = tm // tt
    tpb = t // tt

    def xmap(i):
        return (i // tpb, i % tpb, 0) if nb == 1 else (i, 0, 0)

    def bmap(i):
        return (i // tpb, 0, 0) if nb == 1 else (i, 0, 0)

    full = lambda shape: pl.BlockSpec(shape, lambda i: (0,) * len(shape))
    if att_transposed:
        att_spec = pl.BlockSpec((None, D_ATT, tm), lambda i: (i // tpb, 0, i % tpb))
    else:
        att_spec = pl.BlockSpec((tm, D_ATT), lambda i: (i, 0))
    choice = lambda dt: (jax.ShapeDtypeStruct((TOP_K, n), dt), pl.BlockSpec((TOP_K, tm), lambda i: (0, i)))
    outs = [(jax.ShapeDtypeStruct((b, t, d), F32), pl.BlockSpec((nb, tt, d), xmap)),
            (jax.ShapeDtypeStruct((n, d), F32), pl.BlockSpec((tm, d), lambda i: (i, 0))),
            choice(I32), choice(F32), choice(I32),
            (jax.ShapeDtypeStruct((N_EXPERTS, LANES), F32), full((N_EXPERTS, LANES)))]
    return pl.pallas_call(
        functools.partial(_outproj_kernel, att_transposed=att_transposed),
        out_shape=tuple(o[0] for o in outs),
        grid=(n // tm,),
        in_specs=[pl.BlockSpec((nb, tt, d), xmap),
                  pl.BlockSpec((N_OCT, tm // SSM_CHUNK, SSM_CHUNK * LANES), lambda i: (0, i, 0)),
                  att_spec,
                  pl.BlockSpec((nb, 1, d), bmap), pl.BlockSpec((nb, 1, d), bmap), pl.BlockSpec((nb, 1, d), bmap),
                  full((1, d)), full((D_SSM, D_SSM)), full((1, D_SSM)), full((d, d)),
                  full((N_EXPERTS, d)), full((N_EXPERTS, 1)), full((N_EXPERTS, LANES))],
        out_specs=tuple(o[1] for o in outs),
        scratch_shapes=[pltpu.VMEM((N_EXPERTS, LANES), F32), pltpu.VMEM((N_OCT, tm, LANES), F32)],
        compiler_params=_cparams(("arbitrary",)),
        name="outproj_t" if att_transposed else "outproj",
    )(x, y4, att, gt, sh, sc, g_ffn, w["wglu"], w["bglu"], w["wout"], w["wr"], w["br"], counts_in)


def _dispatch_kernel(zst_ref, nu_ref, dest_ref, *rest, tm, tiles):
    h_refs = rest[:len(tiles)]
    xs_ref, zero_ref, zsem, sem = rest[len(tiles):]
    i = pl.program_id(0)
    n_blocks = xs_ref.shape[0] // MOE_BLK

    @pl.when(i == 0)
    def _():
        zero_ref[...] = jnp.zeros_like(zero_ref)

        def zero_copy(start):
            start = pl.multiple_of(start, MOE_BLK)
            return pltpu.make_async_copy(zero_ref, xs_ref.at[pl.ds(start, MOE_BLK)], zsem)

        for e in range(N_EXPERTS):
            @pl.when(zst_ref[e] >= 0)
            def _():
                zero_copy(jnp.maximum(zst_ref[e], 0)).start()

        def tail_start(j, carry):
            zero_copy(j * MOE_BLK).start()
            return carry

        def tail_wait(j, carry):
            zero_copy(j * MOE_BLK).wait()
            return carry

        lax.fori_loop(nu_ref[0], n_blocks, tail_start, 0)
        for e in range(N_EXPERTS):
            @pl.when(zst_ref[e] >= 0)
            def _():
                zero_copy(jnp.maximum(zst_ref[e], 0)).wait()
        lax.fori_loop(nu_ref[0], n_blocks, tail_wait, 0)

    def scatter_rows(h_ref):
        def issue(t, carry):
            for kk in range(TOP_K):
                dst = dest_ref[0, 0, t * TOP_K + kk]
                pltpu.make_async_copy(h_ref.at[pl.ds(t, 1)], xs_ref.at[pl.ds(dst, 1)], sem).start()
            return carry

        lax.fori_loop(0, tm, issue, 0, unroll=8)

    first_tile = 0
    for h_ref, n_tiles in zip(h_refs, tiles):
        pl.when((i >= first_tile) & (i < first_tile + n_tiles))(functools.partial(scatter_rows, h_ref))
        first_tile += n_tiles
    n_copied = tm * TOP_K
    pltpu.make_async_copy(xs_ref.at[pl.ds(0, n_copied)], xs_ref.at[pl.ds(0, n_copied)], sem).wait()


def _dispatch(h2s, dest, zstart, n_used, n_rows, tm):
    d = h2s[0].shape[1]
    tiles = tuple(h.shape[0] // tm for h in h2s)
    nt = sum(tiles)
    dest3 = dest.reshape(nt, 1, tm * TOP_K)
    in_specs = [pl.BlockSpec((1, 1, tm * TOP_K), lambda i, *_: (i, 0, 0), memory_space=pltpu.SMEM)]
    first_tile = 0
    for n_tiles in tiles:
        in_specs.append(pl.BlockSpec(
            (tm, d), lambda i, *_, f=first_tile, m=n_tiles: (jnp.clip(i - f, 0, m - 1), 0)))
        first_tile += n_tiles
    return pl.pallas_call(
        functools.partial(_dispatch_kernel, tm=tm, tiles=tiles),
        out_shape=jax.ShapeDtypeStruct((n_rows, d), F32),
        grid_spec=pltpu.PrefetchScalarGridSpec(
            num_scalar_prefetch=2, grid=(nt,),
            in_specs=in_specs,
            out_specs=pl.BlockSpec(memory_space=pl.ANY),
            scratch_shapes=[pltpu.VMEM((MOE_BLK, d), F32), pltpu.SemaphoreType.DMA, pltpu.SemaphoreType.DMA]),
        compiler_params=_cparams(("arbitrary",)),
        name="moe_dispatch",
    )(zstart, n_used, dest3, *h2s)


def _expert_kernel(be_ref, nxt_ref, slot_ref, first_ref, nu_ref, xs_ref, wup_hbm, bup_ref, wdn_hbm, bdn_ref,
                   ys_ref, wup_f32, wdn_f32, wup_bf, wdn_bf, sems):
    i = pl.program_id(0)
    e = be_ref[i]

    def fetch(expert, slot):
        return (pltpu.make_async_copy(wup_hbm.at[expert], wup_f32.at[slot], sems.at[0, slot]),
                pltpu.make_async_copy(wdn_hbm.at[expert], wdn_f32.at[slot], sems.at[1, slot]))

    @pl.when(i < nu_ref[0])
    def _():
        @pl.when(first_ref[i] == 1)
        def _():
            slot = slot_ref[i]

            @pl.when(i == 0)
            def _():
                for copy in fetch(e, slot):
                    copy.start()

            for copy in fetch(e, slot):
                copy.wait()

            @pl.when(nxt_ref[i] >= 0)
            def _():
                for copy in fetch(jnp.maximum(nxt_ref[i], 0), 1 - slot):
                    copy.start()

            wup_bf[...] = wup_f32[slot].astype(BF16)
            wdn_bf[...] = wdn_f32[slot].astype(BF16)

        up = _dot(xs_ref[...].astype(BF16), wup_bf[...]) + bup_ref[...]
        x_glu = jnp.minimum(up[:, :D_FF], SWIGLU_LIMIT)
        x_lin = jnp.clip(up[:, D_FF:], -SWIGLU_LIMIT, SWIGLU_LIMIT)
        act = (x_lin + 1.0) * (x_glu * jax.nn.sigmoid(SWIGLU_ALPHA * x_glu))
        ys_ref[...] = _dot(act.astype(BF16), wdn_bf[...]) + bdn_ref[...]

    @pl.when(i >= nu_ref[0])
    def _():
        ys_ref[...] = jnp.zeros_like(ys_ref)


def _experts(xs, block_e, n_used, has_rows, w_up, b_up, w_down, b_down):
    n_rows, d = xs.shape
    nblk = n_rows // MOE_BLK
    blocks = jnp.arange(nblk, dtype=I32)
    used = blocks < n_used[0]
    first = (used & ((blocks == 0) | (block_e != jnp.roll(block_e, 1)))).astype(I32)
    slot = ((jnp.cumsum(first) - 1) % 2).astype(I32)
    experts = jnp.arange(N_EXPERTS, dtype=I32)
    later = jnp.where(has_rows[None, :] & (experts[None, :] > experts[:, None]), experts[None, :], N_EXPERTS)
    nxt_e = jnp.min(later, axis=1)
    nxt_e = jnp.where(nxt_e < N_EXPERTS, nxt_e, -1).astype(I32)
    nxt = jnp.sum(jnp.where(block_e[:, None] == experts[None, :], nxt_e[None, :], 0), axis=1).astype(I32)
    rowmap = lambda i, be, nx, sl, fi, nu: (jnp.minimum(i, nu[0] - 1), 0)
    emap = lambda i, be, nx, sl, fi, nu: (be[i], 0, 0)
    return pl.pallas_call(
        _expert_kernel,
        out_shape=jax.ShapeDtypeStruct((n_rows, d), F32),
        grid_spec=pltpu.PrefetchScalarGridSpec(
            num_scalar_prefetch=5, grid=(nblk,),
            in_specs=[pl.BlockSpec((MOE_BLK, d), rowmap),
                      pl.BlockSpec(memory_space=pl.ANY), pl.BlockSpec((None, 1, 2 * D_FF), emap),
                      pl.BlockSpec(memory_space=pl.ANY), pl.BlockSpec((None, 1, d), emap)],
            out_specs=pl.BlockSpec((MOE_BLK, d), lambda i, *_: (i, 0)),
            scratch_shapes=[pltpu.VMEM((2, d, 2 * D_FF), F32), pltpu.VMEM((2, D_FF, d), F32),
                            pltpu.VMEM((d, 2 * D_FF), BF16), pltpu.VMEM((D_FF, d), BF16),
                            pltpu.SemaphoreType.DMA((2, 2))]),
        compiler_params=_cparams(("arbitrary",)),
        name="moe_experts",
    )(block_e, nxt, slot, first, n_used, xs, w_up, b_up, w_down, b_down)


def _combine_kernel(meta_ref, metan_ref, x1_ref, col_ref, gate_ref, gt_ref, gfin_ref, ys_ref, y_ref, ybuf_ref, sems,
                    *, tm):
    i = pl.program_id(0)
    nt = pl.num_programs(0)
    nb, tt, d = x1_ref.shape
    rows = ybuf_ref.shape[1]

    def run_copies(m_ref, slot, start):
        def per_expert(e, carry):
            a = m_ref[0, 0, e]
            n_chunks = m_ref[0, 0, N_EXPERTS + e]
            bo = m_ref[0, 0, 2 * N_EXPERTS + e]

            def per_chunk(c, carry2):
                src = ys_ref.at[pl.ds(pl.multiple_of(a + 8 * c, 8), 8)]
                dst = ybuf_ref.at[slot, pl.ds(pl.multiple_of(bo + 8 * c, 8), 8)]
                copy = pltpu.make_async_copy(src, dst, sems.at[slot])
                if start:
                    copy.start()
                else:
                    copy.wait()
                return carry2

            lax.fori_loop(0, n_chunks, per_chunk, 0)
            return carry

        lax.fori_loop(0, N_EXPERTS, per_expert, 0)

    @pl.when(i == 0)
    def _():
        ybuf_ref[...] = jnp.zeros_like(ybuf_ref)
        run_copies(meta_ref, 0, True)

    @pl.when(i + 1 < nt)
    def _():
        run_copies(metan_ref, (i + 1) % 2, True)

    slot = i % 2
    run_copies(meta_ref, slot, False)
    col = col_ref[...]
    gate = gate_ref[...]
    lane = lax.broadcasted_iota(I32, (tm, rows), 1)
    pick = jnp.where(lane == col[:, 0:1], gate[:, 0:1], 0.0)
    for kk in range(1, TOP_K):
        pick = pick + jnp.where(lane == col[:, kk:kk + 1], gate[:, kk:kk + 1], 0.0)
    moe = _dot(pick.astype(BF16), ybuf_ref[slot].astype(BF16))
    x2 = x1_ref[...] + gt_ref[...] * moe.reshape(nb, tt, d)
    ms = jnp.mean(x2 * x2, axis=-1, keepdims=True)
    y_ref[...] = x2 * lax.rsqrt(ms + NORM_EPS) * gfin_ref[...]


def _combine_plan(idx, dest, off, before0, tm):
    n = idx.shape[1]
    nt = n // tm
    is_e = idx[None, :, :] == jnp.arange(N_EXPERTS, dtype=I32)[:, None, None]
    cnt = jnp.any(is_e, axis=1).astype(I32).reshape(N_EXPERTS, nt, tm).sum(axis=2).T
    before = before0[None, :] + jnp.cumsum(cnt, axis=0) - cnt
    start = off[None, :] + before
    a8 = start // 8 * 8
    b8 = jnp.where(cnt > 0, (start + cnt + 7) // 8 * 8, a8)
    span = b8 - a8
    boff = jnp.cumsum(span, axis=1) - span
    delta = jnp.repeat((boff - a8).T, tm, axis=1)
    col = dest + jnp.sum(jnp.where(is_e, delta[:, None, :], 0), axis=0)
    meta = jnp.concatenate([a8, span // 8, boff, jnp.zeros_like(a8)], axis=1).astype(I32)
    return meta.reshape(nt, 1, 4 * N_EXPERTS), col.T.astype(I32)


def _combine(x1, gate, idx, dest, off, before0, ys, gt, g_final, tm):
    b, t, d = x1.shape
    n = b * t
    tm = min(tm, n)
    tt = min(t, tm)
    nb = tm // tt
    tpb = t // tt
    nt = n // tm
    rows = -(-(tm * TOP_K + 14 * N_EXPERTS) // 256) * 256
    meta, col = _combine_plan(idx, dest, off, before0, tm)

    def xmap(i):
        return (i // tpb, i % tpb, 0) if nb == 1 else (i, 0, 0)

    def bmap(i):
        return (i // tpb, 0, 0) if nb == 1 else (i, 0, 0)

    smem = lambda imap: pl.BlockSpec((1, 1, 4 * N_EXPERTS), imap, memory_space=pltpu.SMEM)
    return pl.pallas_call(
        functools.partial(_combine_kernel, tm=tm),
        out_shape=jax.ShapeDtypeStruct((b, t, d), F32),
        grid=(nt,),
        in_specs=[smem(lambda i: (i, 0, 0)), smem(lambda i: (jnp.minimum(i + 1, nt - 1), 0, 0)),
                  pl.BlockSpec((nb, tt, d), xmap),
                  pl.BlockSpec((tm, TOP_K), lambda i: (i, 0)),
                  pl.BlockSpec((tm, TOP_K), lambda i: (i, 0)),
                  pl.BlockSpec((nb, 1, d), bmap),
                  pl.BlockSpec((1, d), lambda i: (0, 0)),
                  pl.BlockSpec(memory_space=pl.ANY)],
        out_specs=pl.BlockSpec((nb, tt, d), xmap),
        scratch_shapes=[pltpu.VMEM((2, rows, d), F32), pltpu.SemaphoreType.DMA((2,))],
        compiler_params=_cparams(("arbitrary",)),
        name="moe_combine",
    )(meta, meta, x1, col, gate, gt, g_final, ys)


def _moe(streams, counts, g_final, w_up, b_up, w_down, b_down):
    n_total = sum(s["h2"].shape[0] for s in streams)
    cnt = counts[:, 0].astype(I32)
    padded = (cnt + MOE_BLK - 1) // MOE_BLK * MOE_BLK
    pad_end = jnp.cumsum(padded)
    off = (pad_end - padded).astype(I32)
    n_blocks = -(-n_total * TOP_K // MOE_BLK) + N_EXPERTS
    n_rows = n_blocks * MOE_BLK
    starts = jnp.arange(n_blocks, dtype=I32) * MOE_BLK
    block_e = jnp.minimum(jnp.sum((pad_end[None, :] <= starts[:, None]).astype(I32), axis=1), N_EXPERTS - 1)
    n_used = (pad_end[-1:] // MOE_BLK).astype(I32)
    zstart = jnp.where(padded > 0, pad_end - MOE_BLK, -1).astype(I32)
    experts = jnp.arange(N_EXPERTS, dtype=I32)
    dests = []
    for s in streams:
        seg = jnp.sum(jnp.where(s["idx"][None] == experts[:, None, None], off[:, None, None], 0), axis=0)
        dests.append((seg + s["rank"]).astype(I32))
    tm = min([512] + [s["h2"].shape[0] for s in streams])
    dest_rows = jnp.concatenate([dest.T for dest in dests], axis=0)
    xs = _dispatch([s["h2"] for s in streams], dest_rows, zstart, n_used, n_rows, tm)
    ys = _experts(xs, block_e, n_used, padded > 0, w_up, b_up, w_down, b_down)
    return [_combine(s["x1"], s["gate"].T, s["idx"], dest, off, s["before"], ys, s["gt"], g_final, 256)
            for s, dest in zip(streams, dests)]


def _stream(x, mod, w, ssm_ops, s0, cache, params, counts_in):
    b, t, d = x.shape
    n = b * t
    sh_mix, sc_mix, gt_mix, sh_ffn, sc_ffn, gt_ffn = mod
    prompt = cache is None
    tm = 512
    u4, q, kt, vt, lft, vtb, kaug = _inproj(x, sh_mix, sc_mix, params["g_mix"], w, prompt=prompt, tm=tm)
    n_chunks = t // SSM_CHUNK
    y4, s_fin = _ssm(u4, ssm_ops, s0, b, n_chunks)
    if prompt:
        att = _attn_p(q, kaug, vtb, b, t)
    else:
        cache_k, cache_v, cache_lf = cache
        p = cache_k.shape[1]
        ck_t = jnp.transpose(cache_k, (0, 2, 3, 1)).reshape(b, D_ATT, p)
        cv_t = jnp.transpose(cache_v, (0, 2, 3, 1)).reshape(b, D_ATT, p)
        cl_t = jnp.transpose(cache_lf, (0, 2, 1))
        ln_t = jnp.pad(lft, ((0, 0), (0, 0), (0, LANES - t)))
        fc, fn = _fprep_s(cl_t, ln_t)
        att = _attn_s(q.reshape(b, t, D_ATT), ck_t, cv_t, kt, vt, fc, fn).reshape(n, D_ATT)
    x1, h2, idx, gate, rank, counts = _outproj(x, y4, att, gt_mix, sh_ffn, sc_ffn, params["g_ffn"], w, counts_in,
                                               att_transposed=prompt, tm=512)
    routed = {"x1": x1, "h2": h2, "idx": idx, "gate": gate, "rank": rank, "gt": gt_ffn,
              "before": counts_in[:, 0].astype(I32)}
    heads = lambda a: jnp.transpose(a.reshape(b, N_HEADS, HEAD_DIM, t), (0, 3, 1, 2))[None]
    return routed, counts, heads(kt), heads(vt), jnp.transpose(lft, (0, 2, 1))[None], s_fin


def _state_in(re, im):
    b = re.shape[0]
    s = jnp.concatenate([re.reshape(b, N_OCT, OCT_STATE // 2), im.reshape(b, N_OCT, OCT_STATE // 2)], axis=-1)
    return jnp.transpose(s, (1, 0, 2))


def _state_out(s):
    b = s.shape[1]
    s = jnp.transpose(s, (1, 0, 2))
    half = OCT_STATE // 2
    re = s[:, :, :half].reshape(1, b, N_SSM_GROUPS, SSM_STATE)
    im = s[:, :, half:].reshape(1, b, N_SSM_GROUPS, SSM_STATE)
    return re, im


def kernel(x_prompt, x_sample, c_prompt, c_sample, cache_k, cache_v, cache_logf, state_ssm_re, state_ssm_im, w_ada, b_ada, g_mix, w_in, b_forget, ssm_log_dt, ssm_a_re, ssm_a_im, ssm_b_re, ssm_b_im, ssm_c_re, ssm_c_im, ssm_d, w_glu, b_glu, w_out, g_ffn, w_router, b_router, w_up, b_up, w_down, b_down, g_final):
    assert w_ada.shape[0] == 1, "single-layer trunk"
    d = D_MODEL
    bp, tp, _ = x_prompt.shape
    bs, ts, _ = x_sample.shape
    n_c = bp + bs
    c_all = jnp.pad(jnp.concatenate([c_prompt, c_sample], axis=0), ((0, -n_c % 8), (0, 0)))
    mod = _ada(c_all, w_ada[0], b_ada[0].reshape(1, 6 * d))
    mod_p = [mod[:bp, j * d:(j + 1) * d].reshape(bp, 1, d) for j in range(6)]
    mod_s = [mod[bp:n_c, j * d:(j + 1) * d].reshape(bs, 1, d) for j in range(6)]
    wi = w_in[0]
    o1, o2, o3, o4 = D_SSM, D_SSM + D_ATT, D_SSM + 2 * D_ATT, D_SSM + 3 * D_ATT
    w_q = wi[:, o1:o2].astype(BF16)
    w_k = wi[:, o2:o3].astype(BF16)
    unused = jnp.zeros((8, LANES), BF16)
    base = {"wu": wi[:, :o1].astype(BF16), "wkt": w_k.T, "wvt": wi[:, o3:o4].astype(BF16).T,
            "wft": wi[:, o4:].astype(BF16).T, "bfc": b_forget[0].reshape(N_HEADS, 1), "pm": _bias_placement(),
            "wglu": w_glu[0].astype(BF16), "bglu": b_glu[0].reshape(1, D_SSM), "wout": w_out[0].astype(BF16),
            "wr": w_router[0].T, "br": b_router[0].reshape(N_EXPERTS, 1)}
    w_p = dict(base, wq=w_q.T, wk=w_k)
    w_s = dict(base, wq=w_q, wk=unused)
    params = {"g_mix": g_mix[0].reshape(1, d), "g_ffn": g_ffn[0].reshape(1, d), "g_final": g_final.reshape(1, d),
              "w_up": w_up[0], "b_up": b_up[0].reshape(N_EXPERTS, 1, 2 * D_FF),
              "w_down": w_down[0], "b_down": b_down[0].reshape(N_EXPERTS, 1, d)}
    ssm_ops = _ssm_operators(ssm_log_dt[0], ssm_a_re[0], ssm_a_im[0], ssm_b_re[0], ssm_b_im[0],
                             ssm_c_re[0], ssm_c_im[0], ssm_d[0])
    zero_state = jnp.zeros((N_OCT, bp, OCT_STATE), F32)
    counts0 = jnp.zeros((N_EXPERTS, LANES), F32)
    r_p, counts_p, k_p, v_p, f_p, s_p = _stream(x_prompt, mod_p, w_p, ssm_ops, zero_state, None, params, counts0)
    s0 = _state_in(state_ssm_re[0], state_ssm_im[0])
    r_s, counts, k_s, v_s, f_s, s_s = _stream(x_sample, mod_s, w_s, ssm_ops, s0,
                                              (cache_k[0], cache_v[0], cache_logf[0]), params, counts_p)
    y_p, y_s = _moe([r_p, r_s], counts, params["g_final"],
                    params["w_up"], params["b_up"], params["w_down"], params["b_down"])
    sre_p, sim_p = _state_out(s_p)
    sre_s, sim_s = _state_out(s_s)
    return (y_p, y_s, k_p, v_p, f_p, sre_p, sim_p, k_s, v_s, f_s, sre_s, sim_s)
```

```python
import functools
import math

import jax
import jax.numpy as jnp
from jax import lax
from jax.experimental import pallas as pl
from jax.experimental.pallas import tpu as pltpu

F32 = jnp.float32
BF16 = jnp.bfloat16
I32 = jnp.int32

D_MODEL = 1024
D_SSM = 512
SSM_GROUP = 16
N_SSM_GROUPS = 32
SSM_STATE = 64
D_ATT = 512
HEAD_DIM = 64
N_HEADS = 8
N_EXPERTS = 32
TOP_K = 4
D_FF = 1024
SWIGLU_LIMIT = 7.0
SWIGLU_ALPHA = 1.702
NORM_EPS = 1e-5

LANES = 128
N_OCT = D_SSM // LANES
OCT_STATE = 2 * (N_SSM_GROUPS // N_OCT) * SSM_STATE
SSM_CHUNK = 8
SSM_ROWS = 512
SSM_INTERLEAVE = 4
ATT_BLK = 256
ATT_PAIRS = 4
LOG2E = math.log2(math.e)
SAMPLE_KV_BLK = 2048
MOE_BLK = 256
VMEM_LIMIT = 52 * 1024 * 1024


def _cparams(sem, vmem=VMEM_LIMIT):
    return pltpu.CompilerParams(dimension_semantics=sem, vmem_limit_bytes=vmem)


def _dot(a, b):
    return jnp.dot(a, b, preferred_element_type=F32)


def _dot_nt(a, b):
    return lax.dot_general(a, b, (((1,), (1,)), ((), ())), preferred_element_type=F32)


def _dot_hi(a, b):
    return jnp.dot(a, b, preferred_element_type=F32, precision=lax.Precision.HIGHEST)


def _ada_kernel(c_ref, w_ref, b_ref, o_ref):
    c = c_ref[...]
    s = c * jax.nn.sigmoid(c)
    o_ref[...] = _dot_hi(s, w_ref[...]) + b_ref[...]


def _ada(c_all, w_ada, b_ada):
    m, d = c_all.shape
    n = w_ada.shape[1]
    return pl.pallas_call(
        _ada_kernel,
        out_shape=jax.ShapeDtypeStruct((m, n), F32),
        grid=(n // d,),
        in_specs=[pl.BlockSpec((m, d), lambda j: (0, 0)),
                  pl.BlockSpec((d, d), lambda j: (0, j)),
                  pl.BlockSpec((1, d), lambda j: (0, j))],
        out_specs=pl.BlockSpec((m, d), lambda j: (0, j)),
        compiler_params=_cparams(("arbitrary",)),
        name="ada",
    )(c_all, w_ada, b_ada)


def _norm_mod(x, g, scale, shift):
    ms = jnp.mean(x * x, axis=-1, keepdims=True)
    y = x * lax.rsqrt(ms + NORM_EPS) * g
    return y * (1.0 + scale) + shift


def _log_sigmoid(z):
    return jnp.minimum(z, 0.0) - jnp.log1p(jnp.exp(-jnp.abs(z)))


def _inproj_kernel(x_ref, sh_ref, sc_ref, g_ref, wu_ref, wq_ref, wkt_ref, wvt_ref, wft_ref, bfc_ref,
                   wk_ref, pm_ref,
                   u4_ref, q_ref, kt_ref, vt_ref, lft_ref, vtb_ref, ka_ref, us_ref, carry_ref, *, prompt, tpb):
    nb, tt, d = x_ref.shape
    tm = nb * tt
    h = _norm_mod(x_ref[...], g_ref[...], sc_ref[...], sh_ref[...]).reshape(tm, d).astype(BF16)
    u = _dot(h, wu_ref[...])
    for o in range(N_OCT):
        us_ref[o] = u[:, o * LANES:(o + 1) * LANES]
    for s in range(SSM_CHUNK):
        for o in range(N_OCT):
            piece = us_ref[o, pl.ds(s, tm // SSM_CHUNK, stride=SSM_CHUNK), :]
            u4_ref[o, :, s * LANES:(s + 1) * LANES] = piece.astype(BF16)
    kt = _dot_nt(wkt_ref[...], h)
    vt = _dot_nt(wvt_ref[...], h)
    lft = _log_sigmoid(_dot_nt(wft_ref[...], h) + bfc_ref[...])
    scale = HEAD_DIM ** -0.5
    if prompt:
        kt_ref[...] = kt
        vt_ref[...] = vt
        lft_ref[...] = lft
        qt = (_dot_nt(wq_ref[...], h) * (scale * LOG2E)).astype(BF16)
        vtb = vt.astype(BF16)
        for j in range(tm // ATT_BLK):
            q_ref[j] = qt[:, j * ATT_BLK:(j + 1) * ATT_BLK]
            vtb_ref[j] = vtb[:, j * ATT_BLK:(j + 1) * ATT_BLK]
        k_tok = _dot(h, wk_ref[...])

        @pl.when(pl.program_id(0) % tpb == 0)
        def _():
            carry_ref[...] = jnp.zeros_like(carry_ref)

        cum = _lane_cumsum(lft) + carry_ref[:, 0:1]
        carry_ref[...] = jnp.broadcast_to(cum[:, tm - 1:tm], carry_ref.shape)
        nf = cum * (-LOG2E)
        hi = nf.astype(BF16).astype(F32)
        r1 = nf - hi
        mid = r1.astype(BF16).astype(F32)
        lo = (r1 - mid).astype(BF16).astype(F32)
        place = lambda piece, pm: lax.dot_general(piece, pm, (((0,), (0,)), ((), ())), preferred_element_type=F32)
        slab = place(hi, pm_ref[0]) + place(mid, pm_ref[1]) + place(lo, pm_ref[2])
        for j in range(tm // ATT_BLK):
            rows = slice(j * ATT_BLK, (j + 1) * ATT_BLK)
            for p in range(N_HEADS // 2):
                ka_ref[j, :, 2 * p * LANES:(2 * p + 1) * LANES] = k_tok[rows, p * LANES:(p + 1) * LANES].astype(BF16)
                ka_ref[j, :, (2 * p + 1) * LANES:(2 * p + 2) * LANES] = (
                    slab[rows, p * LANES:(p + 1) * LANES].astype(BF16))
    else:
        for bl in range(nb):
            kt_ref[bl] = kt[:, bl * tt:(bl + 1) * tt]
            vt_ref[bl] = vt[:, bl * tt:(bl + 1) * tt]
            lft_ref[bl] = lft[:, bl * tt:(bl + 1) * tt]
        q_ref[...] = (_dot(h, wq_ref[...]) * scale).astype(BF16)
        vtb_ref[...] = jnp.zeros_like(vtb_ref)
        ka_ref[...] = jnp.zeros_like(ka_ref)


def _inproj(x, shift, scale, g_mix, w, *, prompt, tm):
    b, t, d = x.shape
    n = b * t
    tm = min(tm, n)
    tt = min(t, tm)
    nb = tm // tt
    tpb = t // tt
    assert (nb == 1) == prompt
    grid = (n // tm,)

    def xmap(i):
        return (i // tpb, i % tpb, 0) if nb == 1 else (i, 0, 0)

    def bmap(i):
        return (i // tpb, 0, 0) if nb == 1 else (i, 0, 0)

    def tmap(i):
        return (i // tpb, 0, i % tpb) if nb == 1 else (i, 0, 0)

    full = lambda a: pl.BlockSpec(a.shape, lambda i: (0,) * a.ndim)
    names = ("wu", "wq", "wkt", "wvt", "wft", "bfc", "wk", "pm")
    in_specs = [pl.BlockSpec((nb, tt, d), xmap),
                pl.BlockSpec((nb, 1, d), bmap), pl.BlockSpec((nb, 1, d), bmap),
                pl.BlockSpec((1, d), lambda i: (0, 0))] + [full(w[k]) for k in names]
    tblk = lambda rows: pl.BlockSpec((None if nb == 1 else nb, rows, tt), tmap)
    dummy = (jax.ShapeDtypeStruct((8, LANES), F32), pl.BlockSpec((8, LANES), lambda i: (0, 0)))
    if prompt:
        nblk = n // ATT_BLK
        blocked = (jax.ShapeDtypeStruct((nblk, D_ATT, ATT_BLK), BF16),
                   pl.BlockSpec((tm // ATT_BLK, D_ATT, ATT_BLK), lambda i: (i, 0, 0)))
        q_out, vtb_out = blocked, blocked
        ka_out = (jax.ShapeDtypeStruct((nblk, ATT_BLK, 2 * D_ATT), BF16),
                  pl.BlockSpec((tm // ATT_BLK, ATT_BLK, 2 * D_ATT), lambda i: (i, 0, 0)))
    else:
        q_out = (jax.ShapeDtypeStruct((n, D_ATT), BF16), pl.BlockSpec((tm, D_ATT), lambda i: (i, 0)))
        vtb_out = ka_out = dummy
    kdim = SSM_CHUNK * LANES
    outs = [(jax.ShapeDtypeStruct((N_OCT, n // SSM_CHUNK, kdim), BF16),
             pl.BlockSpec((N_OCT, tm // SSM_CHUNK, kdim), lambda i: (0, i, 0))),
            q_out,
            (jax.ShapeDtypeStruct((b, D_ATT, t), F32), tblk(D_ATT)),
            (jax.ShapeDtypeStruct((b, D_ATT, t), F32), tblk(D_ATT)),
            (jax.ShapeDtypeStruct((b, N_HEADS, t), F32), tblk(N_HEADS)),
            vtb_out, ka_out]
    return pl.pallas_call(
        functools.partial(_inproj_kernel, prompt=prompt, tpb=tpb),
        out_shape=tuple(o[0] for o in outs), grid=grid, in_specs=in_specs, out_specs=tuple(o[1] for o in outs),
        scratch_shapes=[pltpu.VMEM((N_OCT, tm, LANES), F32), pltpu.VMEM((8, LANES), F32)],
        compiler_params=_cparams(("arbitrary",)),
        name="inproj_p" if prompt else "inproj_s",
    )(x, shift, scale, g_mix, *[w[k] for k in names])


def _ssm_operators(log_dt, a_re, a_im, b_re, b_im, c_re, c_im, d_skip):
    L, G, P, C = SSM_CHUNK, N_SSM_GROUPS, SSM_STATE, SSM_GROUP
    q = G // N_OCT
    lam = lax.complex(a_re, a_im)
    dt = jnp.exp(log_dt)[:, None]
    lam_dt = lam * dt
    lam_bar = jnp.exp(lam_dt)
    b_bar = ((lam_bar - 1.0) / lam)[:, :, None] * lax.complex(b_re, b_im)
    c_mat = lax.complex(c_re, c_im)
    pw = jnp.exp(lam_dt[None] * jnp.arange(L + 1, dtype=F32)[:, None, None])
    kern = jnp.real(jnp.einsum('gcp,dgp,gpk->dgck', c_mat, pw[:L], b_bar))
    ksm = jnp.transpose(kern, (1, 0, 3, 2)).reshape(N_OCT, q, L, C, C)
    ksm = jnp.transpose(ksm, (0, 2, 1, 3, 4)).reshape(N_OCT, L, q * C, C)
    pwr = pw[L - 1 - jnp.arange(L)]
    wb = jnp.einsum('jgp,gpk->jgkp', pwr, b_bar)
    wsm = jnp.stack([jnp.real(wb), jnp.imag(wb)], axis=0).reshape(2, L, N_OCT, q * C, P)
    wsm = jnp.transpose(wsm, (2, 1, 0, 3, 4))
    cl = c_mat[None] * pw[1:L + 1][:, :, None, :]
    vsm = jnp.stack([jnp.real(cl), -jnp.imag(cl)], axis=0).reshape(2, L, N_OCT, q, C, P)
    vsm = jnp.transpose(vsm, (2, 1, 0, 3, 5, 4)).reshape(N_OCT, L, OCT_STATE, C)
    tmat, wend, win = _ssm_prep(ksm, wsm, vsm, d_skip.reshape(N_OCT, 1, LANES))
    dec = pw[L].reshape(N_OCT, q * P)
    decay = jnp.concatenate([jnp.real(dec), jnp.imag(dec)], axis=-1).reshape(N_OCT, 1, OCT_STATE)
    return tmat, wend, win, decay.astype(F32)


def _ssm_prep_kernel(ksm_ref, wsm_ref, vsm_ref, d_ref, tmat_ref, wend_ref, win_ref):
    L, C, P = SSM_CHUNK, SSM_GROUP, SSM_STATE
    half = OCT_STATE // 2

    def spread(period, width):
        r = lax.broadcasted_iota(I32, (period, width), 0)
        c = lax.broadcasted_iota(I32, (period, width), 1)
        return jnp.where(c % period == r, 1.0, 0.0)

    def same_group(shape, row_div, row_mod, lane_div):
        r = lax.broadcasted_iota(I32, shape, 0)
        c = lax.broadcasted_iota(I32, shape, 1)
        return (r % row_mod) // row_div == c // lane_div

    e_c = spread(C, LANES)
    e_p = spread(P, half)
    grp = same_group((LANES, LANES), C, LANES, C)
    rr = lax.broadcasted_iota(I32, (LANES, LANES), 0)
    cc = lax.broadcasted_iota(I32, (LANES, LANES), 1)
    lag = []
    for d in range(L):
        blk = jnp.where(grp, _dot_hi(ksm_ref[d], e_c), 0.0)
        if d == 0:
            blk = blk + jnp.where(rr == cc, jnp.broadcast_to(d_ref[...], (LANES, LANES)), 0.0)
        lag.append(blk.astype(BF16))
    zero = jnp.zeros((LANES, LANES), BF16)
    for j in range(L):
        for t in range(L):
            tmat_ref[j * LANES:(j + 1) * LANES, t * LANES:(t + 1) * LANES] = lag[t - j] if t >= j else zero
    grp_e = same_group((LANES, half), C, LANES, P)
    for j in range(L):
        for part in range(2):
            blk = jnp.where(grp_e, _dot_hi(wsm_ref[j, part], e_p), 0.0)
            wend_ref[j * LANES:(j + 1) * LANES, part * half:(part + 1) * half] = blk.astype(BF16)
    grp_i = same_group((OCT_STATE, LANES), P, half, C)
    for t in range(L):
        blk = jnp.where(grp_i, _dot_hi(vsm_ref[t], e_c), 0.0)
        win_ref[:, t * LANES:(t + 1) * LANES] = blk.astype(BF16)


def _ssm_prep(ksm, wsm, vsm, dvec):
    kdim = SSM_CHUNK * LANES
    blk = lambda a: pl.BlockSpec((None,) + a.shape[1:], lambda o: (o,) + (0,) * (a.ndim - 1))
    out = lambda r, c: (jax.ShapeDtypeStruct((N_OCT, r, c), BF16), pl.BlockSpec((None, r, c), lambda o: (o, 0, 0)))
    outs = [out(kdim, kdim), out(kdim, OCT_STATE), out(OCT_STATE, kdim)]
    return pl.pallas_call(
        _ssm_prep_kernel,
        out_shape=tuple(o[0] for o in outs),
        grid=(N_OCT,),
        in_specs=[blk(ksm), blk(wsm), blk(vsm), blk(dvec)],
        out_specs=tuple(o[1] for o in outs),
        compiler_params=_cparams(("arbitrary",)),
        name="ssm_prep",
    )(ksm, wsm, vsm, dvec)


def _ssm_kernel(u_ref, tmat_ref, wend_ref, win_ref, dec_ref, s0_ref, y_ref, sfin_ref, e_ref, sp_ref, st_ref,
                *, nb, cpt):
    i = pl.program_id(1)
    half = OCT_STATE // 2
    rows = nb * cpt

    @pl.when(i == 0)
    def _():
        st_ref[...] = s0_ref[...]

    u = u_ref[...].reshape(rows, u_ref.shape[-1])
    e_ref[...] = _dot(u, wend_ref[...])
    dec = dec_ref[...]
    ar, ai = dec[:, :half], dec[:, half:]

    def advance(row, st):
        sp_ref[pl.ds(row, 1), :] = st
        e = e_ref[pl.ds(row, 1), :]
        re, im = st[:, :half], st[:, half:]
        nre = ar * re - ai * im + e[:, :half]
        nim = ar * im + ai * re + e[:, half:]
        return jnp.concatenate([nre, nim], axis=-1)

    if nb <= SSM_INTERLEAVE:
        def per_chunk(r, sts):
            return tuple(advance(bb * cpt + r, sts[bb]) for bb in range(nb))

        sts = lax.fori_loop(0, cpt, per_chunk, tuple(st_ref[bb:bb + 1, :] for bb in range(nb)))
        for bb in range(nb):
            st_ref[bb:bb + 1, :] = sts[bb]
    else:
        def per_batch(bb, carry):
            st = lax.fori_loop(0, cpt, lambda r, st: advance(bb * cpt + r, st), st_ref[pl.ds(bb, 1), :])
            st_ref[pl.ds(bb, 1), :] = st
            return carry

        lax.fori_loop(0, nb, per_batch, 0)
    y = _dot(u, tmat_ref[...]) + _dot(sp_ref[...].astype(BF16), win_ref[...])
    y_ref[...] = jax.nn.gelu(y).astype(BF16).reshape(y_ref.shape)
    sfin_ref[...] = st_ref[...]


def _ssm(u4, ops, s0, batch, n_chunks):
    tmat, wend, win, decay = ops
    kdim = SSM_CHUNK * LANES
    cpt = SSM_ROWS if n_chunks % SSM_ROWS == 0 else n_chunks
    tiles = n_chunks // cpt
    if tiles > 1:
        u_in = u4.reshape(N_OCT, batch, n_chunks, kdim)
        u_spec = pl.BlockSpec((None, batch, cpt, kdim), lambda o, i: (o, 0, i, 0))
    else:
        u_in = u4
        u_spec = pl.BlockSpec((None, batch * n_chunks, kdim), lambda o, i: (o, 0, 0))
    r = batch * cpt
    wspec = lambda shape: pl.BlockSpec((None,) + shape, lambda o, i: (o, 0, 0))
    y, s_fin = pl.pallas_call(
        functools.partial(_ssm_kernel, nb=batch, cpt=cpt),
        out_shape=(jax.ShapeDtypeStruct(u_in.shape, BF16),
                   jax.ShapeDtypeStruct((N_OCT, batch, OCT_STATE), F32)),
        grid=(N_OCT, tiles),
        in_specs=[u_spec,
                  wspec((kdim, kdim)), wspec((kdim, OCT_STATE)), wspec((OCT_STATE, kdim)), wspec((1, OCT_STATE)),
                  wspec((batch, OCT_STATE))],
        out_specs=(u_spec, wspec((batch, OCT_STATE))),
        scratch_shapes=[pltpu.VMEM((r, OCT_STATE), F32), pltpu.VMEM((r, OCT_STATE), F32),
                        pltpu.VMEM((batch, OCT_STATE), F32)],
        compiler_params=_cparams(("arbitrary", "arbitrary")),
        name="ssm",
    )(u_in, tmat, wend, win, decay, s0)
    return y.reshape(u4.shape), s_fin


def _bias_placement():
    import numpy as np
    pm = np.zeros((3, N_HEADS, N_HEADS // 2 * LANES), np.float32)
    for piece in range(3):
        for h in range(N_HEADS):
            pm[piece, h, LANES * (h // 2) + 3 * (h % 2) + piece] = 1.0
    return jnp.asarray(pm)


def _attn_p_kernel(qt_ref, ka_ref, vt_ref, o_ref, qbd_ref, acc_ref, sa_ref, sb_ref):
    i = pl.program_id(2)
    bq = ATT_BLK
    hd = HEAD_DIM
    r = lax.broadcasted_iota(I32, (LANES, 2 * bq), 0)
    c = lax.broadcasted_iota(I32, (LANES, 2 * bq), 1)
    ones = jnp.where(((r < 3) & (c < bq)) | ((r >= 3) & (r < 6) & (c >= bq)), 1.0, 0.0).astype(BF16)
    zero = jnp.zeros((hd, bq), BF16)
    for pp in range(ATT_PAIRS):
        qb = qt_ref[pp * LANES:(pp + 1) * LANES, :]
        qbd_ref[pp, 0:hd, 0:bq] = qb[0:hd]
        qbd_ref[pp, 0:hd, bq:2 * bq] = zero
        qbd_ref[pp, hd:2 * hd, 0:bq] = zero
        qbd_ref[pp, hd:2 * hd, bq:2 * bq] = qb[hd:2 * hd]
        qbd_ref[pp, 2 * hd:2 * hd + LANES, :] = ones
    acc_ref[...] = jnp.zeros_like(acc_ref)

    nq = ka_ref.shape[0]

    def scores(s_ref, j):
        for pp in range(ATT_PAIRS):
            s_ref[pp] = _dot(ka_ref[j, :, pp * 2 * LANES:(pp + 1) * 2 * LANES], qbd_ref[pp])

    def attend(s_ref, j, carry, masked):
        out = []
        for pp in range(ATT_PAIRS):
            m_prev, l_prev = carry[2 * pp], carry[2 * pp + 1]
            s = s_ref[pp]
            if masked:
                key = j * bq + lax.broadcasted_iota(I32, s.shape, 0)
                qq = lax.broadcasted_iota(I32, s.shape, 1)
                qq = i * bq + jnp.where(qq >= bq, qq - bq, qq)
                s = jnp.where(key <= qq, s, -jnp.inf)
            m_new = jnp.maximum(m_prev, jnp.max(s, axis=0, keepdims=True))
            alpha = jnp.exp2(m_prev - m_new)
            p = jnp.exp2(s - m_new)
            l_new = alpha * l_prev + jnp.sum(p, axis=0, keepdims=True)
            pb = p.astype(BF16)
            vb = vt_ref[jnp.minimum(j, nq - 1), pp * LANES:(pp + 1) * LANES, :]
            r0 = pp * LANES
            acc_ref[r0:r0 + hd] = alpha[:, 0:bq] * acc_ref[r0:r0 + hd] + _dot(vb[0:hd], pb[:, 0:bq])
            acc_ref[r0 + hd:r0 + 2 * hd] = (alpha[:, bq:2 * bq] * acc_ref[r0 + hd:r0 + 2 * hd]
                                            + _dot(vb[hd:2 * hd], pb[:, bq:2 * bq]))
            out.extend((m_new, l_new))
        return tuple(out)

    def double_step(m, carry):
        j = 2 * m
        scores(sb_ref, j + 1)
        carry = attend(sa_ref, j, carry, False)
        scores(sa_ref, j + 2)
        return attend(sb_ref, j + 1, carry, False)

    init = (jnp.full((1, 2 * bq), -jnp.inf, F32), jnp.zeros((1, 2 * bq), F32)) * ATT_PAIRS
    scores(sa_ref, 0)
    carry = lax.fori_loop(0, i // 2, double_step, init)
    j = 2 * (i // 2)
    scores(sb_ref, jnp.minimum(j + 1, nq - 1))
    carry = attend(sa_ref, j, carry, True)
    carry = attend(sb_ref, j + 1, carry, True)
    for pp in range(ATT_PAIRS):
        l = carry[2 * pp + 1]
        r0 = pp * LANES
        o_ref[r0:r0 + hd] = acc_ref[r0:r0 + hd] / l[:, 0:bq]
        o_ref[r0 + hd:r0 + 2 * hd] = acc_ref[r0 + hd:r0 + 2 * hd] / l[:, bq:2 * bq]


def _attn_p(qt, kaug, vt, batch, t):
    blk = ATT_BLK
    nq = t // blk
    ngrp = N_HEADS // 2 // ATT_PAIRS
    rows = ATT_PAIRS * LANES
    qt4 = qt.reshape(batch, nq, D_ATT, blk)
    vt4 = vt.reshape(batch, nq, D_ATT, blk)
    ka4 = kaug.reshape(batch, nq, blk, 2 * D_ATT)
    return pl.pallas_call(
        _attn_p_kernel,
        out_shape=jax.ShapeDtypeStruct((batch, D_ATT, t), F32),
        grid=(batch, ngrp, nq),
        in_specs=[pl.BlockSpec((None, None, rows, blk), lambda b, p, i: (b, i, p, 0)),
                  pl.BlockSpec((None, nq, blk, 2 * rows), lambda b, p, i: (b, 0, 0, p), pipeline_mode=pl.Buffered(1)),
                  pl.BlockSpec((None, nq, rows, blk), lambda b, p, i: (b, 0, p, 0), pipeline_mode=pl.Buffered(1))],
        out_specs=pl.BlockSpec((None, rows, blk), lambda b, p, i: (b, p, i)),
        scratch_shapes=[pltpu.VMEM((ATT_PAIRS, 2 * LANES, 2 * blk), BF16), pltpu.VMEM((rows, blk), F32),
                        pltpu.VMEM((ATT_PAIRS, blk, 2 * blk), F32), pltpu.VMEM((ATT_PAIRS, blk, 2 * blk), F32)],
        compiler_params=_cparams(("arbitrary", "arbitrary", "arbitrary")),
        name="attn_p",
    )(qt4, ka4, vt4)


def _lane_cumsum(x):
    n = x.shape[-1]
    lane = lax.broadcasted_iota(I32, x.shape, x.ndim - 1)
    s = 1
    while s < n:
        x = x + jnp.where(lane >= s, pltpu.roll(x, s, x.ndim - 1), 0.0)
        s *= 2
    return x


def _fprep_s_kernel(cl_ref, ln_ref, fc_ref, fn_ref):
    b, h, p = cl_ref.shape
    cum = _lane_cumsum(cl_ref[...].reshape(b * h, p))
    fc_ref[...] = (-cum).reshape(b, h, p)
    total = cum[:, p - 1:p]
    cn = _lane_cumsum(ln_ref[...].reshape(b * h, LANES))
    fn_ref[...] = (-(total + cn)).reshape(b, h, LANES)


def _fprep_s(cache_lf_t, new_lf_t):
    b, h, p = cache_lf_t.shape
    return pl.pallas_call(
        _fprep_s_kernel,
        out_shape=(jax.ShapeDtypeStruct((b, h, p), F32), jax.ShapeDtypeStruct((b, h, LANES), F32)),
        grid=(1,),
        in_specs=[pl.BlockSpec((b, h, p), lambda i: (0, 0, 0)), pl.BlockSpec((b, h, LANES), lambda i: (0, 0, 0))],
        out_specs=(pl.BlockSpec((b, h, p), lambda i: (0, 0, 0)), pl.BlockSpec((b, h, LANES), lambda i: (0, 0, 0))),
        compiler_params=_cparams(("arbitrary",)),
        name="fprep_s",
    )(cache_lf_t, new_lf_t)


def _attn_s_kernel(q_ref, ck_ref, cv_ref, kn_ref, vn_ref, fc_ref, fn_ref, o_ref, qbd_ref, m_ref, l_ref, acc_ref,
                   *, tq):
    j = pl.program_id(1)
    nkv = pl.num_programs(1)
    rows = N_HEADS * tq
    bk = ck_ref.shape[1]

    @pl.when(j == 0)
    def _():
        q = q_ref[...]
        qrep = jnp.broadcast_to(q[None], (N_HEADS, tq, D_ATT)).reshape(rows, D_ATT)
        rh = lax.broadcasted_iota(I32, (rows, D_ATT), 0) // tq
        ch = lax.broadcasted_iota(I32, (rows, D_ATT), 1) // HEAD_DIM
        qbd_ref[...] = jnp.where(rh == ch, qrep, jnp.zeros_like(qrep))
        m_ref[...] = jnp.full_like(m_ref, -jnp.inf)
        l_ref[...] = jnp.zeros_like(l_ref)
        acc_ref[...] = jnp.zeros_like(acc_ref)

    def update(s, vt):
        m_prev = m_ref[...]
        m_new = jnp.maximum(m_prev, jnp.max(s, axis=1, keepdims=True))
        alpha = jnp.exp(m_prev - m_new)
        p = jnp.exp(s - m_new)
        l_ref[...] = alpha * l_ref[...] + jnp.sum(p, axis=1, keepdims=True)
        m_ref[...] = m_new
        acc_ref[...] = alpha * acc_ref[...] + _dot_nt(p.astype(BF16), vt)

    def bias(f, width):
        return jnp.broadcast_to(f[:, None, :], (N_HEADS, tq, width)).reshape(rows, width)

    s = _dot(qbd_ref[...], ck_ref[...].astype(BF16)) + bias(fc_ref[...], bk)
    update(s, cv_ref[...].astype(BF16))

    @pl.when(j == nkv - 1)
    def _():
        s2 = _dot(qbd_ref[...], kn_ref[...].astype(BF16)) + bias(fn_ref[...][:, 0:tq], tq)
        key = lax.broadcasted_iota(I32, (rows, tq), 1)
        qq = lax.broadcasted_iota(I32, (rows, tq), 0) % tq
        update(jnp.where(key <= qq, s2, -jnp.inf), vn_ref[...].astype(BF16))
        o = acc_ref[...] / l_ref[...]
        rh = lax.broadcasted_iota(I32, (rows, D_ATT), 0) // tq
        ch = lax.broadcasted_iota(I32, (rows, D_ATT), 1) // HEAD_DIM
        o = jnp.where(rh == ch, o, 0.0).reshape(N_HEADS, tq, D_ATT)
        o_ref[...] = jnp.sum(o, axis=0)


def _attn_s(q, cache_k, cache_v, k_new, v_new, fc, fn):
    b, tq, _ = q.shape
    p = cache_k.shape[2]
    bk = min(SAMPLE_KV_BLK, p)
    nkv = p // bk
    rows = N_HEADS * tq
    tok = lambda: pl.BlockSpec((None, tq, D_ATT), lambda bi, j: (bi, 0, 0))
    new = lambda: pl.BlockSpec((None, D_ATT, tq), lambda bi, j: (bi, 0, 0))
    return pl.pallas_call(
        functools.partial(_attn_s_kernel, tq=tq),
        out_shape=jax.ShapeDtypeStruct((b, tq, D_ATT), F32),
        grid=(b, nkv),
        in_specs=[tok(),
                  pl.BlockSpec((None, D_ATT, bk), lambda bi, j: (bi, 0, j)),
                  pl.BlockSpec((None, D_ATT, bk), lambda bi, j: (bi, 0, j)),
                  new(), new(),
                  pl.BlockSpec((None, N_HEADS, bk), lambda bi, j: (bi, 0, j)),
                  pl.BlockSpec((None, N_HEADS, LANES), lambda bi, j: (bi, 0, 0))],
        out_specs=tok(),
        scratch_shapes=[pltpu.VMEM((rows, D_ATT), BF16), pltpu.VMEM((rows, 1), F32),
                        pltpu.VMEM((rows, 1), F32), pltpu.VMEM((rows, D_ATT), F32)],
        compiler_params=_cparams(("arbitrary", "arbitrary")),
        name="attn_s",
    )(q, cache_k, cache_v, k_new, v_new, fc, fn)


def _outproj_kernel(x_ref, y4_ref, att_ref, gt_ref, sh_ref, sc_ref, gf_ref, wglu_ref, bglu_ref, wout_ref,
                    wr_ref, br_ref, cin_ref,
                    x1_ref, h2_ref, idx_ref, gate_ref, rank_ref, cnt_ref, carry_ref, ys_ref, *, att_transposed):
    i = pl.program_id(0)
    nb, tt, d = x_ref.shape
    tm = nb * tt

    @pl.when(i == 0)
    def _():
        carry_ref[...] = cin_ref[...]

    for s in range(SSM_CHUNK):
        for o in range(N_OCT):
            ys_ref[o, pl.ds(s, tm // SSM_CHUNK, stride=SSM_CHUNK), :] = (
                y4_ref[o, :, s * LANES:(s + 1) * LANES].astype(F32))
    ysf = jnp.concatenate([ys_ref[o] for o in range(N_OCT)], axis=-1)
    glu = ysf * jax.nn.sigmoid(_dot(ysf.astype(BF16), wglu_ref[...]) + bglu_ref[...])
    att = att_ref[...]
    if att_transposed:
        att = att.T
    mix = _dot(glu.astype(BF16), wout_ref[0:D_SSM, :]) + _dot(att.astype(BF16), wout_ref[D_SSM:, :])
    x1 = x_ref[...] + gt_ref[...] * mix.reshape(nb, tt, d)
    x1_ref[...] = x1
    h2 = _norm_mod(x1, gf_ref[...], sc_ref[...], sh_ref[...]).reshape(tm, d)
    h2_ref[...] = h2

    logits = lax.dot_general(wr_ref[...], h2, (((1,), (1,)), ((), ())), preferred_element_type=F32,
                             precision=lax.Precision.HIGHEST) + br_ref[...]
    sub = lax.broadcasted_iota(I32, logits.shape, 0)
    work = logits
    vals, idxs = [], []
    for _ in range(TOP_K):
        mx = jnp.max(work, axis=0, keepdims=True)
        ix = jnp.min(jnp.where(work == mx, sub, N_EXPERTS), axis=0, keepdims=True)
        vals.append(mx)
        idxs.append(ix)
        work = jnp.where(sub == ix, -jnp.inf, work)
    ex = [jnp.exp(v - vals[0]) for v in vals]
    den = ex[0] + ex[1] + ex[2] + ex[3]
    mh = jnp.where(work == -jnp.inf, 1.0, 0.0)
    r = lax.broadcasted_iota(I32, (tm, tm), 0)
    c = lax.broadcasted_iota(I32, (tm, tm), 1)
    earlier = jnp.where(r < c, 1.0, 0.0).astype(BF16)
    carry = carry_ref[...]
    before = _dot(mh.astype(BF16), earlier) + carry[:, 0:1]
    carry_ref[...] = carry + jnp.sum(mh, axis=1, keepdims=True)
    for kk in range(TOP_K):
        idx_ref[kk:kk + 1, :] = idxs[kk]
        gate_ref[kk:kk + 1, :] = ex[kk] / den
        rk = jnp.sum(jnp.where(sub == idxs[kk], before, 0.0), axis=0, keepdims=True)
        rank_ref[kk:kk + 1, :] = rk.astype(I32)
    cnt_ref[...] = carry_ref[...]


def _outproj(x, y4, att, gt, sh, sc, g_ffn, w, counts_in, *, att_transposed, tm):
    b, t, d = x.shape
    n = b * t
    tm = min(tm, n)
    tt = min(t, tm)
    nb = tm // tt
    tpb = t // tt

    def xmap(i):
        return (i // tpb, i % tpb, 0) if nb == 1 else (i, 0, 0)

    def bmap(i):
        return (i // tpb, 0, 0) if nb == 1 else (i, 0, 0)

    full = lambda shape: pl.BlockSpec(shape, lambda i: (0,) * len(shape))
    if att_transposed:
        att_spec = pl.BlockSpec((None, D_ATT, tm), lambda i: (i // tpb, 0, i % tpb))
    else:
        att_spec = pl.BlockSpec((tm, D_ATT), lambda i: (i, 0))
    choice = lambda dt: (jax.ShapeDtypeStruct((TOP_K, n), dt), pl.BlockSpec((TOP_K, tm), lambda i: (0, i)))
    outs = [(jax.ShapeDtypeStruct((b, t, d), F32), pl.BlockSpec((nb, tt, d), xmap)),
            (jax.ShapeDtypeStruct((n, d), F32), pl.BlockSpec((tm, d), lambda i: (i, 0))),
            choice(I32), choice(F32), choice(I32),
            (jax.ShapeDtypeStruct((N_EXPERTS, LANES), F32), full((N_EXPERTS, LANES)))]
    return pl.pallas_call(
        functools.partial(_outproj_kernel, att_transposed=att_transposed),
        out_shape=tuple(o[0] for o in outs),
        grid=(n // tm,),
        in_specs=[pl.BlockSpec((nb, tt, d), xmap),
                  pl.BlockSpec((N_OCT, tm // SSM_CHUNK, SSM_CHUNK * LANES), lambda i: (0, i, 0)),
                  att_spec,
                  pl.BlockSpec((nb, 1, d), bmap), pl.BlockSpec((nb, 1, d), bmap), pl.BlockSpec((nb, 1, d), bmap),
                  full((1, d)), full((D_SSM, D_SSM)), full((1, D_SSM)), full((d, d)),
                  full((N_EXPERTS, d)), full((N_EXPERTS, 1)), full((N_EXPERTS, LANES))],
        out_specs=tuple(o[1] for o in outs),
        scratch_shapes=[pltpu.VMEM((N_EXPERTS, LANES), F32), pltpu.VMEM((N_OCT, tm, LANES), F32)],
        compiler_params=_cparams(("arbitrary",)),
        name="outproj_t" if att_transposed else "outproj",
    )(x, y4, att, gt, sh, sc, g_ffn, w["wglu"], w["bglu"], w["wout"], w["wr"], w["br"], counts_in)


def _dispatch_kernel(zst_ref, nu_ref, dest_ref, *rest, tm, tiles):
    h_refs = rest[:len(tiles)]
    xs_ref, zero_ref, zsem, sem = rest[len(tiles):]
    i = pl.program_id(0)
    n_blocks = xs_ref.shape[0] // MOE_BLK

    @pl.when(i == 0)
    def _():
        zero_ref[...] = jnp.zeros_like(zero_ref)

        def zero_copy(start):
            start = pl.multiple_of(start, MOE_BLK)
            return pltpu.make_async_copy(zero_ref, xs_ref.at[pl.ds(start, MOE_BLK)], zsem)

        for e in range(N_EXPERTS):
            @pl.when(zst_ref[e] >= 0)
            def _():
                zero_copy(jnp.maximum(zst_ref[e], 0)).start()

        def tail_start(j, carry):
            zero_copy(j * MOE_BLK).start()
            return carry

        def tail_wait(j, carry):
            zero_copy(j * MOE_BLK).wait()
            return carry

        lax.fori_loop(nu_ref[0], n_blocks, tail_start, 0)
        for e in range(N_EXPERTS):
            @pl.when(zst_ref[e] >= 0)
            def _():
                zero_copy(jnp.maximum(zst_ref[e], 0)).wait()
        lax.fori_loop(nu_ref[0], n_blocks, tail_wait, 0)

    def scatter_rows(h_ref):
        def issue(t, carry):
            for kk in range(TOP_K):
                dst = dest_ref[0, 0, t * TOP_K + kk]
                pltpu.make_async_copy(h_ref.at[pl.ds(t, 1)], xs_ref.at[pl.ds(dst, 1)], sem).start()
            return carry

        lax.fori_loop(0, tm, issue, 0, unroll=8)

    first_tile = 0
    for h_ref, n_tiles in zip(h_refs, tiles):
        pl.when((i >= first_tile) & (i < first_tile + n_tiles))(functools.partial(scatter_rows, h_ref))
        first_tile += n_tiles
    n_copied = tm * TOP_K
    pltpu.make_async_copy(xs_ref.at[pl.ds(0, n_copied)], xs_ref.at[pl.ds(0, n_copied)], sem).wait()


def _dispatch(h2s, dest, zstart, n_used, n_rows, tm):
    d = h2s[0].shape[1]
    tiles = tuple(h.shape[0] // tm for h in h2s)
    nt = sum(tiles)
    dest3 = dest.reshape(nt, 1, tm * TOP_K)
    in_specs = [pl.BlockSpec((1, 1, tm * TOP_K), lambda i, *_: (i, 0, 0), memory_space=pltpu.SMEM)]
    first_tile = 0
    for n_tiles in tiles:
        in_specs.append(pl.BlockSpec(
            (tm, d), lambda i, *_, f=first_tile, m=n_tiles: (jnp.clip(i - f, 0, m - 1), 0)))
        first_tile += n_tiles
    return pl.pallas_call(
        functools.partial(_dispatch_kernel, tm=tm, tiles=tiles),
        out_shape=jax.ShapeDtypeStruct((n_rows, d), F32),
        grid_spec=pltpu.PrefetchScalarGridSpec(
            num_scalar_prefetch=2, grid=(nt,),
            in_specs=in_specs,
            out_specs=pl.BlockSpec(memory_space=pl.ANY),
            scratch_shapes=[pltpu.VMEM((MOE_BLK, d), F32), pltpu.SemaphoreType.DMA, pltpu.SemaphoreType.DMA]),
        compiler_params=_cparams(("arbitrary",)),
        name="moe_dispatch",
    )(zstart, n_used, dest3, *h2s)


def _expert_kernel(be_ref, nxt_ref, slot_ref, first_ref, nu_ref, xs_ref, wup_hbm, bup_ref, wdn_hbm, bdn_ref,
                   ys_ref, wup_f32, wdn_f32, wup_bf, wdn_bf, sems):
    i = pl.program_id(0)
    e = be_ref[i]

    def fetch(expert, slot):
        return (pltpu.make_async_copy(wup_hbm.at[expert], wup_f32.at[slot], sems.at[0, slot]),
                pltpu.make_async_copy(wdn_hbm.at[expert], wdn_f32.at[slot], sems.at[1, slot]))

    @pl.when(i < nu_ref[0])
    def _():
        @pl.when(first_ref[i] == 1)
        def _():
            slot = slot_ref[i]

            @pl.when(i == 0)
            def _():
                for copy in fetch(e, slot):
                    copy.start()

            for copy in fetch(e, slot):
                copy.wait()

            @pl.when(nxt_ref[i] >= 0)
            def _():
                for copy in fetch(jnp.maximum(nxt_ref[i], 0), 1 - slot):
                    copy.start()

            wup_bf[...] = wup_f32[slot].astype(BF16)
            wdn_bf[...] = wdn_f32[slot].astype(BF16)

        up = _dot(xs_ref[...].astype(BF16), wup_bf[...]) + bup_ref[...]
        x_glu = jnp.minimum(up[:, :D_FF], SWIGLU_LIMIT)
        x_lin = jnp.clip(up[:, D_FF:], -SWIGLU_LIMIT, SWIGLU_LIMIT)
        act = (x_lin + 1.0) * (x_glu * jax.nn.sigmoid(SWIGLU_ALPHA * x_glu))
        ys_ref[...] = _dot(act.astype(BF16), wdn_bf[...]) + bdn_ref[...]

    @pl.when(i >= nu_ref[0])
    def _():
        ys_ref[...] = jnp.zeros_like(ys_ref)


def _experts(xs, block_e, n_used, has_rows, w_up, b_up, w_down, b_down):
    n_rows, d = xs.shape
    nblk = n_rows // MOE_BLK
    blocks = jnp.arange(nblk, dtype=I32)
    used = blocks < n_used[0]
    first = (used & ((blocks == 0) | (block_e != jnp.roll(block_e, 1)))).astype(I32)
    slot = ((jnp.cumsum(first) - 1) % 2).astype(I32)
    experts = jnp.arange(N_EXPERTS, dtype=I32)
    later = jnp.where(has_rows[None, :] & (experts[None, :] > experts[:, None]), experts[None, :], N_EXPERTS)
    nxt_e = jnp.min(later, axis=1)
    nxt_e = jnp.where(nxt_e < N_EXPERTS, nxt_e, -1).astype(I32)
    nxt = jnp.sum(jnp.where(block_e[:, None] == experts[None, :], nxt_e[None, :], 0), axis=1).astype(I32)
    rowmap = lambda i, be, nx, sl, fi, nu: (jnp.minimum(i, nu[0] - 1), 0)
    emap = lambda i, be, nx, sl, fi, nu: (be[i], 0, 0)
    return pl.pallas_call(
        _expert_kernel,
        out_shape=jax.ShapeDtypeStruct((n_rows, d), F32),
        grid_spec=pltpu.PrefetchScalarGridSpec(
            num_scalar_prefetch=5, grid=(nblk,),
            in_specs=[pl.BlockSpec((MOE_BLK, d), rowmap),
                      pl.BlockSpec(memory_space=pl.ANY), pl.BlockSpec((None, 1, 2 * D_FF), emap),
                      pl.BlockSpec(memory_space=pl.ANY), pl.BlockSpec((None, 1, d), emap)],
            out_specs=pl.BlockSpec((MOE_BLK, d), lambda i, *_: (i, 0)),
            scratch_shapes=[pltpu.VMEM((2, d, 2 * D_FF), F32), pltpu.VMEM((2, D_FF, d), F32),
                            pltpu.VMEM((d, 2 * D_FF), BF16), pltpu.VMEM((D_FF, d), BF16),
                            pltpu.SemaphoreType.DMA((2, 2))]),
        compiler_params=_cparams(("arbitrary",)),
        name="moe_experts",
    )(block_e, nxt, slot, first, n_used, xs, w_up, b_up, w_down, b_down)


def _combine_kernel(meta_ref, metan_ref, x1_ref, col_ref, gate_ref, gt_ref, gfin_ref, ys_ref, y_ref, ybuf_ref, sems,
                    *, tm):
    i = pl.program_id(0)
    nt = pl.num_programs(0)
    nb, tt, d = x1_ref.shape
    rows = ybuf_ref.shape[1]

    def run_copies(m_ref, slot, start):
        def per_expert(e, carry):
            a = m_ref[0, 0, e]
            n_chunks = m_ref[0, 0, N_EXPERTS + e]
            bo = m_ref[0, 0, 2 * N_EXPERTS + e]

            def per_chunk(c, carry2):
                src = ys_ref.at[pl.ds(pl.multiple_of(a + 8 * c, 8), 8)]
                dst = ybuf_ref.at[slot, pl.ds(pl.multiple_of(bo + 8 * c, 8), 8)]
                copy = pltpu.make_async_copy(src, dst, sems.at[slot])
                if start:
                    copy.start()
                else:
                    copy.wait()
                return carry2

            lax.fori_loop(0, n_chunks, per_chunk, 0)
            return carry

        lax.fori_loop(0, N_EXPERTS, per_expert, 0)

    @pl.when(i == 0)
    def _():
        ybuf_ref[...] = jnp.zeros_like(ybuf_ref)
        run_copies(meta_ref, 0, True)

    @pl.when(i + 1 < nt)
    def _():
        run_copies(metan_ref, (i + 1) % 2, True)

    slot = i % 2
    run_copies(meta_ref, slot, False)
    col = col_ref[...]
    gate = gate_ref[...]
    lane = lax.broadcasted_iota(I32, (tm, rows), 1)
    pick = jnp.where(lane == col[:, 0:1], gate[:, 0:1], 0.0)
    for kk in range(1, TOP_K):
        pick = pick + jnp.where(lane == col[:, kk:kk + 1], gate[:, kk:kk + 1], 0.0)
    moe = _dot(pick.astype(BF16), ybuf_ref[slot].astype(BF16))
    x2 = x1_ref[...] + gt_ref[...] * moe.reshape(nb, tt, d)
    ms = jnp.mean(x2 * x2, axis=-1, keepdims=True)
    y_ref[...] = x2 * lax.rsqrt(ms + NORM_EPS) * gfin_ref[...]


def _combine_plan(idx, dest, off, before0, tm):
    n = idx.shape[1]
    nt = n // tm
    is_e = idx[None, :, :] == jnp.arange(N_EXPERTS, dtype=I32)[:, None, None]
    cnt = jnp.any(is_e, axis=1).astype(I32).reshape(N_EXPERTS, nt, tm).sum(axis=2).T
    before = before0[None, :] + jnp.cumsum(cnt, axis=0) - cnt
    start = off[None, :] + before
    a8 = start // 8 * 8
    b8 = jnp.where(cnt > 0, (start + cnt + 7) // 8 * 8, a8)
    span = b8 - a8
    boff = jnp.cumsum(span, axis=1) - span
    delta = jnp.repeat((boff - a8).T, tm, axis=1)
    col = dest + jnp.sum(jnp.where(is_e, delta[:, None, :], 0), axis=0)
    meta = jnp.concatenate([a8, span // 8, boff, jnp.zeros_like(a8)], axis=1).astype(I32)
    return meta.reshape(nt, 1, 4 * N_EXPERTS), col.T.astype(I32)


def _combine(x1, gate, idx, dest, off, before0, ys, gt, g_final, tm):
    b, t, d = x1.shape
    n = b * t
    tm = min(tm, n)
    tt = min(t, tm)
    nb = tm // tt
    tpb = t // tt
    nt = n // tm
    rows = -(-(tm * TOP_K + 14 * N_EXPERTS) // 256) * 256
    meta, col = _combine_plan(idx, dest, off, before0, tm)

    def xmap(i):
        return (i // tpb, i % tpb, 0) if nb == 1 else (i, 0, 0)

    def bmap(i):
        return (i // tpb, 0, 0) if nb == 1 else (i, 0, 0)

    smem = lambda imap: pl.BlockSpec((1, 1, 4 * N_EXPERTS), imap, memory_space=pltpu.SMEM)
    return pl.pallas_call(
        functools.partial(_combine_kernel, tm=tm),
        out_shape=jax.ShapeDtypeStruct((b, t, d), F32),
        grid=(nt,),
        in_specs=[smem(lambda i: (i, 0, 0)), smem(lambda i: (jnp.minimum(i + 1, nt - 1), 0, 0)),
                  pl.BlockSpec((nb, tt, d), xmap),
                  pl.BlockSpec((tm, TOP_K), lambda i: (i, 0)),
                  pl.BlockSpec((tm, TOP_K), lambda i: (i, 0)),
                  pl.BlockSpec((nb, 1, d), bmap),
                  pl.BlockSpec((1, d), lambda i: (0, 0)),
                  pl.BlockSpec(memory_space=pl.ANY)],
        out_specs=pl.BlockSpec((nb, tt, d), xmap),
        scratch_shapes=[pltpu.VMEM((2, rows, d), F32), pltpu.SemaphoreType.DMA((2,))],
        compiler_params=_cparams(("arbitrary",)),
        name="moe_combine",
    )(meta, meta, x1, col, gate, gt, g_final, ys)


def _moe(streams, counts, g_final, w_up, b_up, w_down, b_down):
    n_total = sum(s["h2"].shape[0] for s in streams)
    cnt = counts[:, 0].astype(I32)
    padded = (cnt + MOE_BLK - 1) // MOE_BLK * MOE_BLK
    pad_end = jnp.cumsum(padded)
    off = (pad_end - padded).astype(I32)
    n_blocks = -(-n_total * TOP_K // MOE_BLK) + N_EXPERTS
    n_rows = n_blocks * MOE_BLK
    starts = jnp.arange(n_blocks, dtype=I32) * MOE_BLK
    block_e = jnp.minimum(jnp.sum((pad_end[None, :] <= starts[:, None]).astype(I32), axis=1), N_EXPERTS - 1)
    n_used = (pad_end[-1:] // MOE_BLK).astype(I32)
    zstart = jnp.where(padded > 0, pad_end - MOE_BLK, -1).astype(I32)
    experts = jnp.arange(N_EXPERTS, dtype=I32)
    dests = []
    for s in streams:
        seg = jnp.sum(jnp.where(s["idx"][None] == experts[:, None, None], off[:, None, None], 0), axis=0)
        dests.append((seg + s["rank"]).astype(I32))
    tm = min([512] + [s["h2"].shape[0] for s in streams])
    dest_rows = jnp.concatenate([dest.T for dest in dests], axis=0)
    xs = _dispatch([s["h2"] for s in streams], dest_rows, zstart, n_used, n_rows, tm)
    ys = _experts(xs, block_e, n_used, padded > 0, w_up, b_up, w_down, b_down)
    return [_combine(s["x1"], s["gate"].T, s["idx"], dest, off, s["before"], ys, s["gt"], g_final, 256)
            for s, dest in zip(streams, dests)]


def _stream(x, mod, w, ssm_ops, s0, cache, params, counts_in):
    b, t, d = x.shape
    n = b * t
    sh_mix, sc_mix, gt_mix, sh_ffn, sc_ffn, gt_ffn = mod
    prompt = cache is None
    tm = 1024 if prompt else 512
    u4, q, kt, vt, lft, vtb, kaug = _inproj(x, sh_mix, sc_mix, params["g_mix"], w, prompt=prompt, tm=tm)
    n_chunks = t // SSM_CHUNK
    y4, s_fin = _ssm(u4, ssm_ops, s0, b, n_chunks)
    if prompt:
        att = _attn_p(q, kaug, vtb, b, t)
    else:
        cache_k, cache_v, cache_lf = cache
        p = cache_k.shape[1]
        ck_t = jnp.transpose(cache_k, (0, 2, 3, 1)).reshape(b, D_ATT, p)
        cv_t = jnp.transpose(cache_v, (0, 2, 3, 1)).reshape(b, D_ATT, p)
        cl_t = jnp.transpose(cache_lf, (0, 2, 1))
        ln_t = jnp.pad(lft, ((0, 0), (0, 0), (0, LANES - t)))
        fc, fn = _fprep_s(cl_t, ln_t)
        att = _attn_s(q.reshape(b, t, D_ATT), ck_t, cv_t, kt, vt, fc, fn).reshape(n, D_ATT)
    x1, h2, idx, gate, rank, counts = _outproj(x, y4, att, gt_mix, sh_ffn, sc_ffn, params["g_ffn"], w, counts_in,
                                               att_transposed=prompt, tm=tm)
    routed = {"x1": x1, "h2": h2, "idx": idx, "gate": gate, "rank": rank, "gt": gt_ffn,
              "before": counts_in[:, 0].astype(I32)}
    heads = lambda a: jnp.transpose(a.reshape(b, N_HEADS, HEAD_DIM, t), (0, 3, 1, 2))[None]
    return routed, counts, heads(kt), heads(vt), jnp.transpose(lft, (0, 2, 1))[None], s_fin


def _state_in(re, im):
    b = re.shape[0]
    s = jnp.concatenate([re.reshape(b, N_OCT, OCT_STATE // 2), im.reshape(b, N_OCT, OCT_STATE // 2)], axis=-1)
    return jnp.transpose(s, (1, 0, 2))


def _state_out(s):
    b = s.shape[1]
    s = jnp.transpose(s, (1, 0, 2))
    half = OCT_STATE // 2
    re = s[:, :, :half].reshape(1, b, N_SSM_GROUPS, SSM_STATE)
    im = s[:, :, half:].reshape(1, b, N_SSM_GROUPS, SSM_STATE)
    return re, im


def kernel(x_prompt, x_sample, c_prompt, c_sample, cache_k, cache_v, cache_logf, state_ssm_re, state_ssm_im, w_ada, b_ada, g_mix, w_in, b_forget, ssm_log_dt, ssm_a_re, ssm_a_im, ssm_b_re, ssm_b_im, ssm_c_re, ssm_c_im, ssm_d, w_glu, b_glu, w_out, g_ffn, w_router, b_router, w_up, b_up, w_down, b_down, g_final):
    assert w_ada.shape[0] == 1, "single-layer trunk"
    d = D_MODEL
    bp, tp, _ = x_prompt.shape
    bs, ts, _ = x_sample.shape
    n_c = bp + bs
    c_all = jnp.pad(jnp.concatenate([c_prompt, c_sample], axis=0), ((0, -n_c % 8), (0, 0)))
    mod = _ada(c_all, w_ada[0], b_ada[0].reshape(1, 6 * d))
    mod_p = [mod[:bp, j * d:(j + 1) * d].reshape(bp, 1, d) for j in range(6)]
    mod_s = [mod[bp:n_c, j * d:(j + 1) * d].reshape(bs, 1, d) for j in range(6)]
    wi = w_in[0]
    o1, o2, o3, o4 = D_SSM, D_SSM + D_ATT, D_SSM + 2 * D_ATT, D_SSM + 3 * D_ATT
    w_q = wi[:, o1:o2].astype(BF16)
    w_k = wi[:, o2:o3].astype(BF16)
    unused = jnp.zeros((8, LANES), BF16)
    base = {"wu": wi[:, :o1].astype(BF16), "wkt": w_k.T, "wvt": wi[:, o3:o4].astype(BF16).T,
            "wft": wi[:, o4:].astype(BF16).T, "bfc": b_forget[0].reshape(N_HEADS, 1), "pm": _bias_placement(),
            "wglu": w_glu[0].astype(BF16), "bglu": b_glu[0].reshape(1, D_SSM), "wout": w_out[0].astype(BF16),
            "wr": w_router[0].T, "br": b_router[0].reshape(N_EXPERTS, 1)}
    w_p = dict(base, wq=w_q.T, wk=w_k)
    w_s = dict(base, wq=w_q, wk=unused)
    params = {"g_mix": g_mix[0].reshape(1, d), "g_ffn": g_ffn[0].reshape(1, d), "g_final": g_final.reshape(1, d),
              "w_up": w_up[0], "b_up": b_up[0].reshape(N_EXPERTS, 1, 2 * D_FF),
              "w_down": w_down[0], "b_down": b_down[0].reshape(N_EXPERTS, 1, d)}
    ssm_ops = _ssm_operators(ssm_log_dt[0], ssm_a_re[0], ssm_a_im[0], ssm_b_re[0], ssm_b_im[0],
                             ssm_c_re[0], ssm_c_im[0], ssm_d[0])
    zero_state = jnp.zeros((N_OCT, bp, OCT_STATE), F32)
    counts0 = jnp.zeros((N_EXPERTS, LANES), F32)
    r_p, counts_p, k_p, v_p, f_p, s_p = _stream(x_prompt, mod_p, w_p, ssm_ops, zero_state, None, params, counts0)
    s0 = _state_in(state_ssm_re[0], state_ssm_im[0])
    r_s, counts, k_s, v_s, f_s, s_s = _stream(x_sample, mod_s, w_s, ssm_ops, s0,
                                              (cache_k[0], cache_v[0], cache_logf[0]), params, counts_p)
    y_p, y_s = _moe([r_p, r_s], counts, params["g_final"],
                    params["w_up"], params["b_up"], params["w_down"], params["b_down"])
    sre_p, sim_p = _state_out(s_p)
    sre_s, sim_s = _state_out(s_s)
    return (y_p, y_s, k_p, v_p, f_p, sre_p, sim_p, k_s, v_s, f_s, sre_s, sim_s)
```

```python
import functools
import math

import jax
import jax.numpy as jnp
from jax import lax
from jax.experimental import pallas as pl
from jax.experimental.pallas import tpu as pltpu

F32 = jnp.float32
BF16 = jnp.bfloat16
I32 = jnp.int32

D_MODEL = 1024
D_SSM = 512
SSM_GROUP = 16
N_SSM_GROUPS = 32
SSM_STATE = 64
D_ATT = 512
HEAD_DIM = 64
N_HEADS = 8
N_EXPERTS = 32
TOP_K = 4
D_FF = 1024
SWIGLU_LIMIT = 7.0
SWIGLU_ALPHA = 1.702
NORM_EPS = 1e-5

LANES = 128
N_OCT = D_SSM // LANES
OCT_STATE = 2 * (N_SSM_GROUPS // N_OCT) * SSM_STATE
SSM_CHUNK = 8
SSM_ROWS = 512
SSM_INTERLEAVE = 4
ATT_BLK = 256
ATT_PAIRS = 4
LOG2E = math.log2(math.e)
SAMPLE_KV_BLK = 2048
MOE_BLK = 256
VMEM_LIMIT = 52 * 1024 * 1024


def _cparams(sem, vmem=VMEM_LIMIT):
    return pltpu.CompilerParams(dimension_semantics=sem, vmem_limit_bytes=vmem)


def _dot(a, b):
    return jnp.dot(a, b, preferred_element_type=F32)


def _dot_nt(a, b):
    return lax.dot_general(a, b, (((1,), (1,)), ((), ())), preferred_element_type=F32)


def _dot_hi(a, b):
    return jnp.dot(a, b, preferred_element_type=F32, precision=lax.Precision.HIGHEST)


def _ada_kernel(c_ref, w_ref, b_ref, o_ref):
    c = c_ref[...]
    s = c * jax.nn.sigmoid(c)
    o_ref[...] = _dot_hi(s, w_ref[...]) + b_ref[...]


def _ada(c_all, w_ada, b_ada):
    m, d = c_all.shape
    n = w_ada.shape[1]
    return pl.pallas_call(
        _ada_kernel,
        out_shape=jax.ShapeDtypeStruct((m, n), F32),
        grid=(n // d,),
        in_specs=[pl.BlockSpec((m, d), lambda j: (0, 0)),
                  pl.BlockSpec((d, d), lambda j: (0, j)),
                  pl.BlockSpec((1, d), lambda j: (0, j))],
        out_specs=pl.BlockSpec((m, d), lambda j: (0, j)),
        compiler_params=_cparams(("arbitrary",)),
        name="ada",
    )(c_all, w_ada, b_ada)


def _norm_mod(x, g, scale, shift):
    ms = jnp.mean(x * x, axis=-1, keepdims=True)
    y = x * lax.rsqrt(ms + NORM_EPS) * g
    return y * (1.0 + scale) + shift


def _log_sigmoid(z):
    return jnp.minimum(z, 0.0) - jnp.log1p(jnp.exp(-jnp.abs(z)))


def _inproj_kernel(x_ref, sh_ref, sc_ref, g_ref, wu_ref, wq_ref, wkt_ref, wvt_ref, wft_ref, bfc_ref,
                   wk_ref, pm_ref,
                   u4_ref, q_ref, kt_ref, vt_ref, lft_ref, vtb_ref, ka_ref, us_ref, carry_ref, *, prompt, tpb):
    nb, tt, d = x_ref.shape
    tm = nb * tt
    h = _norm_mod(x_ref[...], g_ref[...], sc_ref[...], sh_ref[...]).reshape(tm, d).astype(BF16)
    u = _dot(h, wu_ref[...])
    for o in range(N_OCT):
        us_ref[o] = u[:, o * LANES:(o + 1) * LANES]
    for s in range(SSM_CHUNK):
        for o in range(N_OCT):
            piece = us_ref[o, pl.ds(s, tm // SSM_CHUNK, stride=SSM_CHUNK), :]
            u4_ref[o, :, s * LANES:(s + 1) * LANES] = piece.astype(BF16)
    kt = _dot_nt(wkt_ref[...], h)
    vt = _dot_nt(wvt_ref[...], h)
    lft = _log_sigmoid(_dot_nt(wft_ref[...], h) + bfc_ref[...])
    scale = HEAD_DIM ** -0.5
    if prompt:
        kt_ref[...] = kt
        vt_ref[...] = vt
        lft_ref[...] = lft
        qt = (_dot_nt(wq_ref[...], h) * (scale * LOG2E)).astype(BF16)
        vtb = vt.astype(BF16)
        for j in range(tm // ATT_BLK):
            q_ref[j] = qt[:, j * ATT_BLK:(j + 1) * ATT_BLK]
            vtb_ref[j] = vtb[:, j * ATT_BLK:(j + 1) * ATT_BLK]
        k_tok = _dot(h, wk_ref[...])

        @pl.when(pl.program_id(0) % tpb == 0)
        def _():
            carry_ref[...] = jnp.zeros_like(carry_ref)

        cum = _lane_cumsum(lft) + carry_ref[:, 0:1]
        carry_ref[...] = jnp.broadcast_to(cum[:, tm - 1:tm], carry_ref.shape)
        nf = cum * (-LOG2E)
        hi = nf.astype(BF16).astype(F32)
        r1 = nf - hi
        mid = r1.astype(BF16).astype(F32)
        lo = (r1 - mid).astype(BF16).astype(F32)
        place = lambda piece, pm: lax.dot_general(piece, pm, (((0,), (0,)), ((), ())), preferred_element_type=F32)
        slab = place(hi, pm_ref[0]) + place(mid, pm_ref[1]) + place(lo, pm_ref[2])
        for j in range(tm // ATT_BLK):
            rows = slice(j * ATT_BLK, (j + 1) * ATT_BLK)
            for p in range(N_HEADS // 2):
                ka_ref[j, :, 2 * p * LANES:(2 * p + 1) * LANES] = k_tok[rows, p * LANES:(p + 1) * LANES].astype(BF16)
                ka_ref[j, :, (2 * p + 1) * LANES:(2 * p + 2) * LANES] = (
                    slab[rows, p * LANES:(p + 1) * LANES].astype(BF16))
    else:
        for bl in range(nb):
            kt_ref[bl] = kt[:, bl * tt:(bl + 1) * tt]
            vt_ref[bl] = vt[:, bl * tt:(bl + 1) * tt]
            lft_ref[bl] = lft[:, bl * tt:(bl + 1) * tt]
        q_ref[...] = (_dot(h, wq_ref[...]) * scale).astype(BF16)
        vtb_ref[...] = jnp.zeros_like(vtb_ref)
        ka_ref[...] = jnp.zeros_like(ka_ref)


def _inproj(x, shift, scale, g_mix, w, *, prompt, tm):
    b, t, d = x.shape
    n = b * t
    tm = min(tm, n)
    tt = min(t, tm)
    nb = tm // tt
    tpb = t // tt
    assert (nb == 1) == prompt
    grid = (n // tm,)

    def xmap(i):
        return (i // tpb, i % tpb, 0) if nb == 1 else (i, 0, 0)

    def bmap(i):
        return (i // tpb, 0, 0) if nb == 1 else (i, 0, 0)

    def tmap(i):
        return (i // tpb, 0, i % tpb) if nb == 1 else (i, 0, 0)

    full = lambda a: pl.BlockSpec(a.shape, lambda i: (0,) * a.ndim)
    names = ("wu", "wq", "wkt", "wvt", "wft", "bfc", "wk", "pm")
    in_specs = [pl.BlockSpec((nb, tt, d), xmap),
                pl.BlockSpec((nb, 1, d), bmap), pl.BlockSpec((nb, 1, d), bmap),
                pl.BlockSpec((1, d), lambda i: (0, 0))] + [full(w[k]) for k in names]
    tblk = lambda rows: pl.BlockSpec((None if nb == 1 else nb, rows, tt), tmap)
    dummy = (jax.ShapeDtypeStruct((8, LANES), F32), pl.BlockSpec((8, LANES), lambda i: (0, 0)))
    if prompt:
        nblk = n // ATT_BLK
        blocked = (jax.ShapeDtypeStruct((nblk, D_ATT, ATT_BLK), BF16),
                   pl.BlockSpec((tm // ATT_BLK, D_ATT, ATT_BLK), lambda i: (i, 0, 0)))
        q_out, vtb_out = blocked, blocked
        ka_out = (jax.ShapeDtypeStruct((nblk, ATT_BLK, 2 * D_ATT), BF16),
                  pl.BlockSpec((tm // ATT_BLK, ATT_BLK, 2 * D_ATT), lambda i: (i, 0, 0)))
    else:
        q_out = (jax.ShapeDtypeStruct((n, D_ATT), BF16), pl.BlockSpec((tm, D_ATT), lambda i: (i, 0)))
        vtb_out = ka_out = dummy
    kdim = SSM_CHUNK * LANES
    outs = [(jax.ShapeDtypeStruct((N_OCT, n // SSM_CHUNK, kdim), BF16),
             pl.BlockSpec((N_OCT, tm // SSM_CHUNK, kdim), lambda i: (0, i, 0))),
            q_out,
            (jax.ShapeDtypeStruct((b, D_ATT, t), F32), tblk(D_ATT)),
            (jax.ShapeDtypeStruct((b, D_ATT, t), F32), tblk(D_ATT)),
            (jax.ShapeDtypeStruct((b, N_HEADS, t), F32), tblk(N_HEADS)),
            vtb_out, ka_out]
    return pl.pallas_call(
        functools.partial(_inproj_kernel, prompt=prompt, tpb=tpb),
        out_shape=tuple(o[0] for o in outs), grid=grid, in_specs=in_specs, out_specs=tuple(o[1] for o in outs),
        scratch_shapes=[pltpu.VMEM((N_OCT, tm, LANES), F32), pltpu.VMEM((8, LANES), F32)],
        compiler_params=_cparams(("arbitrary",)),
        name="inproj_p" if prompt else "inproj_s",
    )(x, shift, scale, g_mix, *[w[k] for k in names])


def _ssm_operators(log_dt, a_re, a_im, b_re, b_im, c_re, c_im, d_skip):
    L, G, P, C = SSM_CHUNK, N_SSM_GROUPS, SSM_STATE, SSM_GROUP
    q = G // N_OCT
    lam = lax.complex(a_re, a_im)
    dt = jnp.exp(log_dt)[:, None]
    lam_dt = lam * dt
    lam_bar = jnp.exp(lam_dt)
    b_bar = ((lam_bar - 1.0) / lam)[:, :, None] * lax.complex(b_re, b_im)
    c_mat = lax.complex(c_re, c_im)
    pw = jnp.exp(lam_dt[None] * jnp.arange(L + 1, dtype=F32)[:, None, None])
    kern = jnp.real(jnp.einsum('gcp,dgp,gpk->dgck', c_mat, pw[:L], b_bar))
    ksm = jnp.transpose(kern, (1, 0, 3, 2)).reshape(N_OCT, q, L, C, C)
    ksm = jnp.transpose(ksm, (0, 2, 1, 3, 4)).reshape(N_OCT, L, q * C, C)
    pwr = pw[L - 1 - jnp.arange(L)]
    wb = jnp.einsum('jgp,gpk->jgkp', pwr, b_bar)
    wsm = jnp.stack([jnp.real(wb), jnp.imag(wb)], axis=0).reshape(2, L, N_OCT, q * C, P)
    wsm = jnp.transpose(wsm, (2, 1, 0, 3, 4))
    cl = c_mat[None] * pw[1:L + 1][:, :, None, :]
    vsm = jnp.stack([jnp.real(cl), -jnp.imag(cl)], axis=0).reshape(2, L, N_OCT, q, C, P)
    vsm = jnp.transpose(vsm, (2, 1, 0, 3, 5, 4)).reshape(N_OCT, L, OCT_STATE, C)
    tmat, wend, win = _ssm_prep(ksm, wsm, vsm, d_skip.reshape(N_OCT, 1, LANES))
    dec = pw[L].reshape(N_OCT, q * P)
    decay = jnp.concatenate([jnp.real(dec), jnp.imag(dec)], axis=-1).reshape(N_OCT, 1, OCT_STATE)
    return tmat, wend, win, decay.astype(F32)


def _ssm_prep_kernel(ksm_ref, wsm_ref, vsm_ref, d_ref, tmat_ref, wend_ref, win_ref):
    L, C, P = SSM_CHUNK, SSM_GROUP, SSM_STATE
    half = OCT_STATE // 2

    def spread(period, width):
        r = lax.broadcasted_iota(I32, (period, width), 0)
        c = lax.broadcasted_iota(I32, (period, width), 1)
        return jnp.where(c % period == r, 1.0, 0.0)

    def same_group(shape, row_div, row_mod, lane_div):
        r = lax.broadcasted_iota(I32, shape, 0)
        c = lax.broadcasted_iota(I32, shape, 1)
        return (r % row_mod) // row_div == c // lane_div

    e_c = spread(C, LANES)
    e_p = spread(P, half)
    grp = same_group((LANES, LANES), C, LANES, C)
    rr = lax.broadcasted_iota(I32, (LANES, LANES), 0)
    cc = lax.broadcasted_iota(I32, (LANES, LANES), 1)
    lag = []
    for d in range(L):
        blk = jnp.where(grp, _dot_hi(ksm_ref[d], e_c), 0.0)
        if d == 0:
            blk = blk + jnp.where(rr == cc, jnp.broadcast_to(d_ref[...], (LANES, LANES)), 0.0)
        lag.append(blk.astype(BF16))
    zero = jnp.zeros((LANES, LANES), BF16)
    for j in range(L):
        for t in range(L):
            tmat_ref[j * LANES:(j + 1) * LANES, t * LANES:(t + 1) * LANES] = lag[t - j] if t >= j else zero
    grp_e = same_group((LANES, half), C, LANES, P)
    for j in range(L):
        for part in range(2):
            blk = jnp.where(grp_e, _dot_hi(wsm_ref[j, part], e_p), 0.0)
            wend_ref[j * LANES:(j + 1) * LANES, part * half:(part + 1) * half] = blk.astype(BF16)
    grp_i = same_group((OCT_STATE, LANES), P, half, C)
    for t in range(L):
        blk = jnp.where(grp_i, _dot_hi(vsm_ref[t], e_c), 0.0)
        win_ref[:, t * LANES:(t + 1) * LANES] = blk.astype(BF16)


def _ssm_prep(ksm, wsm, vsm, dvec):
    kdim = SSM_CHUNK * LANES
    blk = lambda a: pl.BlockSpec((None,) + a.shape[1:], lambda o: (o,) + (0,) * (a.ndim - 1))
    out = lambda r, c: (jax.ShapeDtypeStruct((N_OCT, r, c), BF16), pl.BlockSpec((None, r, c), lambda o: (o, 0, 0)))
    outs = [out(kdim, kdim), out(kdim, OCT_STATE), out(OCT_STATE, kdim)]
    return pl.pallas_call(
        _ssm_prep_kernel,
        out_shape=tuple(o[0] for o in outs),
        grid=(N_OCT,),
        in_specs=[blk(ksm), blk(wsm), blk(vsm), blk(dvec)],
        out_specs=tuple(o[1] for o in outs),
        compiler_params=_cparams(("arbitrary",)),
        name="ssm_prep",
    )(ksm, wsm, vsm, dvec)


def _ssm_kernel(u_ref, tmat_ref, wend_ref, win_ref, dec_ref, s0_ref, y_ref, sfin_ref, e_ref, sp_ref, st_ref,
                *, nb, cpt):
    i = pl.program_id(1)
    half = OCT_STATE // 2
    rows = nb * cpt

    @pl.when(i == 0)
    def _():
        st_ref[...] = s0_ref[...]

    u = u_ref[...].reshape(rows, u_ref.shape[-1])
    e_ref[...] = _dot(u, wend_ref[...])
    dec = dec_ref[...]
    ar, ai = dec[:, :half], dec[:, half:]

    def advance(row, st):
        sp_ref[pl.ds(row, 1), :] = st
        e = e_ref[pl.ds(row, 1), :]
        re, im = st[:, :half], st[:, half:]
        nre = ar * re - ai * im + e[:, :half]
        nim = ar * im + ai * re + e[:, half:]
        return jnp.concatenate([nre, nim], axis=-1)

    if nb <= SSM_INTERLEAVE:
        def per_chunk(r, sts):
            return tuple(advance(bb * cpt + r, sts[bb]) for bb in range(nb))

        sts = lax.fori_loop(0, cpt, per_chunk, tuple(st_ref[bb:bb + 1, :] for bb in range(nb)))
        for bb in range(nb):
            st_ref[bb:bb + 1, :] = sts[bb]
    else:
        def per_batch(bb, carry):
            st = lax.fori_loop(0, cpt, lambda r, st: advance(bb * cpt + r, st), st_ref[pl.ds(bb, 1), :])
            st_ref[pl.ds(bb, 1), :] = st
            return carry

        lax.fori_loop(0, nb, per_batch, 0)
    y = _dot(u, tmat_ref[...]) + _dot(sp_ref[...].astype(BF16), win_ref[...])
    y_ref[...] = jax.nn.gelu(y).astype(BF16).reshape(y_ref.shape)
    sfin_ref[...] = st_ref[...]


def _ssm(u4, ops, s0, batch, n_chunks):
    tmat, wend, win, decay = ops
    kdim = SSM_CHUNK * LANES
    cpt = SSM_ROWS if n_chunks % SSM_ROWS == 0 else n_chunks
    tiles = n_chunks // cpt
    if tiles > 1:
        u_in = u4.reshape(N_OCT, batch, n_chunks, kdim)
        u_spec = pl.BlockSpec((None, batch, cpt, kdim), lambda o, i: (o, 0, i, 0))
    else:
        u_in = u4
        u_spec = pl.BlockSpec((None, batch * n_chunks, kdim), lambda o, i: (o, 0, 0))
    r = batch * cpt
    wspec = lambda shape: pl.BlockSpec((None,) + shape, lambda o, i: (o, 0, 0))
    y, s_fin = pl.pallas_call(
        functools.partial(_ssm_kernel, nb=batch, cpt=cpt),
        out_shape=(jax.ShapeDtypeStruct(u_in.shape, BF16),
                   jax.ShapeDtypeStruct((N_OCT, batch, OCT_STATE), F32)),
        grid=(N_OCT, tiles),
        in_specs=[u_spec,
                  wspec((kdim, kdim)), wspec((kdim, OCT_STATE)), wspec((OCT_STATE, kdim)), wspec((1, OCT_STATE)),
                  wspec((batch, OCT_STATE))],
        out_specs=(u_spec, wspec((batch, OCT_STATE))),
        scratch_shapes=[pltpu.VMEM((r, OCT_STATE), F32), pltpu.VMEM((r, OCT_STATE), F32),
                        pltpu.VMEM((batch, OCT_STATE), F32)],
        compiler_params=_cparams(("arbitrary", "arbitrary")),
        name="ssm",
    )(u_in, tmat, wend, win, decay, s0)
    return y.reshape(u4.shape), s_fin


def _bias_placement():
    import numpy as np
    pm = np.zeros((3, N_HEADS, N_HEADS // 2 * LANES), np.float32)
    for piece in range(3):
        for h in range(N_HEADS):
            pm[piece, h, LANES * (h // 2) + 3 * (h % 2) + piece] = 1.0
    return jnp.asarray(pm)


def _attn_p_kernel(qt_ref, ka_ref, vt_ref, o_ref, qbd_ref, acc_ref, sa_ref, sb_ref):
    i = pl.program_id(2)
    bq = ATT_BLK
    hd = HEAD_DIM
    r = lax.broadcasted_iota(I32, (LANES, 2 * bq), 0)
    c = lax.broadcasted_iota(I32, (LANES, 2 * bq), 1)
    ones = jnp.where(((r < 3) & (c < bq)) | ((r >= 3) & (r < 6) & (c >= bq)), 1.0, 0.0).astype(BF16)
    zero = jnp.zeros((hd, bq), BF16)
    for pp in range(ATT_PAIRS):
        qb = qt_ref[pp * LANES:(pp + 1) * LANES, :]
        qbd_ref[pp, 0:hd, 0:bq] = qb[0:hd]
        qbd_ref[pp, 0:hd, bq:2 * bq] = zero
        qbd_ref[pp, hd:2 * hd, 0:bq] = zero
        qbd_ref[pp, hd:2 * hd, bq:2 * bq] = qb[hd:2 * hd]
        qbd_ref[pp, 2 * hd:2 * hd + LANES, :] = ones
    acc_ref[...] = jnp.zeros_like(acc_ref)

    nq = ka_ref.shape[0]

    def scores(s_ref, j):
        for pp in range(ATT_PAIRS):
            s_ref[pp] = _dot(ka_ref[j, :, pp * 2 * LANES:(pp + 1) * 2 * LANES], qbd_ref[pp])

    def attend(s_ref, j, carry, masked):
        out = []
        for pp in range(ATT_PAIRS):
            m_prev, l_prev = carry[2 * pp], carry[2 * pp + 1]
            s = s_ref[pp]
            if masked:
                key = j * bq + lax.broadcasted_iota(I32, s.shape, 0)
                qq = lax.broadcasted_iota(I32, s.shape, 1)
                qq = i * bq + jnp.where(qq >= bq, qq - bq, qq)
                s = jnp.where(key <= qq, s, -jnp.inf)
            m_new = jnp.maximum(m_prev, jnp.max(s, axis=0, keepdims=True))
            alpha = jnp.exp2(m_prev - m_new)
            p = jnp.exp2(s - m_new)
            l_new = alpha * l_prev + jnp.sum(p, axis=0, keepdims=True)
            pb = p.astype(BF16)
            vb = vt_ref[jnp.minimum(j, nq - 1), pp * LANES:(pp + 1) * LANES, :]
            r0 = pp * LANES
            acc_ref[r0:r0 + hd] = alpha[:, 0:bq] * acc_ref[r0:r0 + hd] + _dot(vb[0:hd], pb[:, 0:bq])
            acc_ref[r0 + hd:r0 + 2 * hd] = (alpha[:, bq:2 * bq] * acc_ref[r0 + hd:r0 + 2 * hd]
                                            + _dot(vb[hd:2 * hd], pb[:, bq:2 * bq]))
            out.extend((m_new, l_new))
        return tuple(out)

    def double_step(m, carry):
        j = 2 * m
        scores(sb_ref, j + 1)
        carry = attend(sa_ref, j, carry, False)
        scores(sa_ref, j + 2)
        return attend(sb_ref, j + 1, carry, False)

    init = (jnp.full((1, 2 * bq), -jnp.inf, F32), jnp.zeros((1, 2 * bq), F32)) * ATT_PAIRS
    scores(sa_ref, 0)
    carry = lax.fori_loop(0, i // 2, double_step, init)
    j = 2 * (i // 2)
    scores(sb_ref, jnp.minimum(j + 1, nq - 1))
    carry = attend(sa_ref, j, carry, True)
    carry = attend(sb_ref, j + 1, carry, True)
    for pp in range(ATT_PAIRS):
        l = carry[2 * pp + 1]
        r0 = pp * LANES
        o_ref[r0:r0 + hd] = acc_ref[r0:r0 + hd] / l[:, 0:bq]
        o_ref[r0 + hd:r0 + 2 * hd] = acc_ref[r0 + hd:r0 + 2 * hd] / l[:, bq:2 * bq]


def _attn_p(qt, kaug, vt, batch, t):
    blk = ATT_BLK
    nq = t // blk
    ngrp = N_HEADS // 2 // ATT_PAIRS
    rows = ATT_PAIRS * LANES
    qt4 = qt.reshape(batch, nq, D_ATT, blk)
    vt4 = vt.reshape(batch, nq, D_ATT, blk)
    ka4 = kaug.reshape(batch, nq, blk, 2 * D_ATT)
    return pl.pallas_call(
        _attn_p_kernel,
        out_shape=jax.ShapeDtypeStruct((batch, D_ATT, t), F32),
        grid=(batch, ngrp, nq),
        in_specs=[pl.BlockSpec((None, None, rows, blk), lambda b, p, i: (b, i, p, 0)),
                  pl.BlockSpec((None, nq, blk, 2 * rows), lambda b, p, i: (b, 0, 0, p), pipeline_mode=pl.Buffered(1)),
                  pl.BlockSpec((None, nq, rows, blk), lambda b, p, i: (b, 0, p, 0), pipeline_mode=pl.Buffered(1))],
        out_specs=pl.BlockSpec((None, rows, blk), lambda b, p, i: (b, p, i)),
        scratch_shapes=[pltpu.VMEM((ATT_PAIRS, 2 * LANES, 2 * blk), BF16), pltpu.VMEM((rows, blk), F32),
                        pltpu.VMEM((ATT_PAIRS, blk, 2 * blk), F32), pltpu.VMEM((ATT_PAIRS, blk, 2 * blk), F32)],
        compiler_params=_cparams(("arbitrary", "arbitrary", "arbitrary")),
        name="attn_p",
    )(qt4, ka4, vt4)


def _lane_cumsum(x):
    n = x.shape[-1]
    lane = lax.broadcasted_iota(I32, x.shape, x.ndim - 1)
    s = 1
    while s < n:
        x = x + jnp.where(lane >= s, pltpu.roll(x, s, x.ndim - 1), 0.0)
        s *= 2
    return x


def _fprep_s_kernel(cl_ref, ln_ref, fc_ref, fn_ref):
    b, h, p = cl_ref.shape
    cum = _lane_cumsum(cl_ref[...].reshape(b * h, p))
    fc_ref[...] = (-cum).reshape(b, h, p)
    total = cum[:, p - 1:p]
    cn = _lane_cumsum(ln_ref[...].reshape(b * h, LANES))
    fn_ref[...] = (-(total + cn)).reshape(b, h, LANES)


def _fprep_s(cache_lf_t, new_lf_t):
    b, h, p = cache_lf_t.shape
    return pl.pallas_call(
        _fprep_s_kernel,
        out_shape=(jax.ShapeDtypeStruct((b, h, p), F32), jax.ShapeDtypeStruct((b, h, LANES), F32)),
        grid=(1,),
        in_specs=[pl.BlockSpec((b, h, p), lambda i: (0, 0, 0)), pl.BlockSpec((b, h, LANES), lambda i: (0, 0, 0))],
        out_specs=(pl.BlockSpec((b, h, p), lambda i: (0, 0, 0)), pl.BlockSpec((b, h, LANES), lambda i: (0, 0, 0))),
        compiler_params=_cparams(("arbitrary",)),
        name="fprep_s",
    )(cache_lf_t, new_lf_t)


def _attn_s_kernel(q_ref, ck_ref, cv_ref, kn_ref, vn_ref, fc_ref, fn_ref, o_ref, qbd_ref, m_ref, l_ref, acc_ref,
                   *, tq):
    j = pl.program_id(1)
    nkv = pl.num_programs(1)
    rows = N_HEADS * tq
    bk = ck_ref.shape[1]

    @pl.when(j == 0)
    def _():
        q = q_ref[...]
        qrep = jnp.broadcast_to(q[None], (N_HEADS, tq, D_ATT)).reshape(rows, D_ATT)
        rh = lax.broadcasted_iota(I32, (rows, D_ATT), 0) // tq
        ch = lax.broadcasted_iota(I32, (rows, D_ATT), 1) // HEAD_DIM
        qbd_ref[...] = jnp.where(rh == ch, qrep, jnp.zeros_like(qrep))
        m_ref[...] = jnp.full_like(m_ref, -jnp.inf)
        l_ref[...] = jnp.zeros_like(l_ref)
        acc_ref[...] = jnp.zeros_like(acc_ref)

    def update(s, vt):
        m_prev = m_ref[...]
        m_new = jnp.maximum(m_prev, jnp.max(s, axis=1, keepdims=True))
        alpha = jnp.exp(m_prev - m_new)
        p = jnp.exp(s - m_new)
        l_ref[...] = alpha * l_ref[...] + jnp.sum(p, axis=1, keepdims=True)
        m_ref[...] = m_new
        acc_ref[...] = alpha * acc_ref[...] + _dot_nt(p.astype(BF16), vt)

    def bias(f, width):
        return jnp.broadcast_to(f[:, None, :], (N_HEADS, tq, width)).reshape(rows, width)

    s = _dot(qbd_ref[...], ck_ref[...].astype(BF16)) + bias(fc_ref[...], bk)
    update(s, cv_ref[...].astype(BF16))

    @pl.when(j == nkv - 1)
    def _():
        s2 = _dot(qbd_ref[...], kn_ref[...].astype(BF16)) + bias(fn_ref[...][:, 0:tq], tq)
        key = lax.broadcasted_iota(I32, (rows, tq), 1)
        qq = lax.broadcasted_iota(I32, (rows, tq), 0) % tq
        update(jnp.where(key <= qq, s2, -jnp.inf), vn_ref[...].astype(BF16))
        o = acc_ref[...] / l_ref[...]
        rh = lax.broadcasted_iota(I32, (rows, D_ATT), 0) // tq
        ch = lax.broadcasted_iota(I32, (rows, D_ATT), 1) // HEAD_DIM
        o = jnp.where(rh == ch, o, 0.0).reshape(N_HEADS, tq, D_ATT)
        o_ref[...] = jnp.sum(o, axis=0)


def _attn_s(q, cache_k, cache_v, k_new, v_new, fc, fn):
    b, tq, _ = q.shape
    p = cache_k.shape[2]
    bk = min(SAMPLE_KV_BLK, p)
    nkv = p // bk
    rows = N_HEADS * tq
    tok = lambda: pl.BlockSpec((None, tq, D_ATT), lambda bi, j: (bi, 0, 0))
    new = lambda: pl.BlockSpec((None, D_ATT, tq), lambda bi, j: (bi, 0, 0))
    return pl.pallas_call(
        functools.partial(_attn_s_kernel, tq=tq),
        out_shape=jax.ShapeDtypeStruct((b, tq, D_ATT), F32),
        grid=(b, nkv),
        in_specs=[tok(),
                  pl.BlockSpec((None, D_ATT, bk), lambda bi, j: (bi, 0, j)),
                  pl.BlockSpec((None, D_ATT, bk), lambda bi, j: (bi, 0, j)),
                  new(), new(),
                  pl.BlockSpec((None, N_HEADS, bk), lambda bi, j: (bi, 0, j)),
                  pl.BlockSpec((None, N_HEADS, LANES), lambda bi, j: (bi, 0, 0))],
        out_specs=tok(),
        scratch_shapes=[pltpu.VMEM((rows, D_ATT), BF16), pltpu.VMEM((rows, 1), F32),
                        pltpu.VMEM((rows, 1), F32), pltpu.VMEM((rows, D_ATT), F32)],
        compiler_params=_cparams(("arbitrary", "arbitrary")),
        name="attn_s",
    )(q, cache_k, cache_v, k_new, v_new, fc, fn)


def _outproj_kernel(x_ref, y4_ref, att_ref, gt_ref, sh_ref, sc_ref, gf_ref, wglu_ref, bglu_ref, wout_ref,
                    wr_ref, br_ref, cin_ref,
                    x1_ref, h2_ref, idx_ref, gate_ref, rank_ref, cnt_ref, carry_ref, ys_ref, *, att_transposed):
    i = pl.program_id(0)
    nb, tt, d = x_ref.shape
    tm = nb * tt

    @pl.when(i == 0)
    def _():
        carry_ref[...] = cin_ref[...]

    for s in range(SSM_CHUNK):
        for o in range(N_OCT):
            ys_ref[o, pl.ds(s, tm // SSM_CHUNK, stride=SSM_CHUNK), :] = (
                y4_ref[o, :, s * LANES:(s + 1) * LANES].astype(F32))
    ysf = jnp.concatenate([ys_ref[o] for o in range(N_OCT)], axis=-1)
    glu = ysf * jax.nn.sigmoid(_dot(ysf.astype(BF16), wglu_ref[...]) + bglu_ref[...])
    att = att_ref[...]
    if att_transposed:
        att = att.T
    mix = _dot(glu.astype(BF16), wout_ref[0:D_SSM, :]) + _dot(att.astype(BF16), wout_ref[D_SSM:, :])
    x1 = x_ref[...] + gt_ref[...] * mix.reshape(nb, tt, d)
    x1_ref[...] = x1
    h2 = _norm_mod(x1, gf_ref[...], sc_ref[...], sh_ref[...]).reshape(tm, d)
    h2_ref[...] = h2

    logits = lax.dot_general(wr_ref[...], h2, (((1,), (1,)), ((), ())), preferred_element_type=F32,
                             precision=lax.Precision.HIGHEST) + br_ref[...]
    sub = lax.broadcasted_iota(I32, logits.shape, 0)
    work = logits
    vals, idxs = [], []
    for _ in range(TOP_K):
        mx = jnp.max(work, axis=0, keepdims=True)
        ix = jnp.min(jnp.where(work == mx, sub, N_EXPERTS), axis=0, keepdims=True)
        vals.append(mx)
        idxs.append(ix)
        work = jnp.where(sub == ix, -jnp.inf, work)
    ex = [jnp.exp(v - vals[0]) for v in vals]
    den = ex[0] + ex[1] + ex[2] + ex[3]
    mh = jnp.where(work == -jnp.inf, 1.0, 0.0)
    r = lax.broadcasted_iota(I32, (tm, tm), 0)
    c = lax.broadcasted_iota(I32, (tm, tm), 1)
    earlier = jnp.where(r < c, 1.0, 0.0).astype(BF16)
    carry = carry_ref[...]
    before = _dot(mh.astype(BF16), earlier) + carry[:, 0:1]
    carry_ref[...] = carry + jnp.sum(mh, axis=1, keepdims=True)
    for kk in range(TOP_K):
        idx_ref[kk:kk + 1, :] = idxs[kk]
        gate_ref[kk:kk + 1, :] = ex[kk] / den
        rk = jnp.sum(jnp.where(sub == idxs[kk], before, 0.0), axis=0, keepdims=True)
        rank_ref[kk:kk + 1, :] = rk.astype(I32)
    cnt_ref[...] = carry_ref[...]


def _outproj(x, y4, att, gt, sh, sc, g_ffn, w, counts_in, *, att_transposed, tm):
    b, t, d = x.shape
    n = b * t
    tm = min(tm, n)
    tt = min(t, tm)
    nb = tm // tt
    tpb = t // tt

    def xmap(i):
        return (i // tpb, i % tpb, 0) if nb == 1 else (i, 0, 0)

    def bmap(i):
        return (i // tpb, 0, 0) if nb == 1 else (i, 0, 0)

    full = lambda shape: pl.BlockSpec(shape, lambda i: (0,) * len(shape))
    if att_transposed:
        att_spec = pl.BlockSpec((None, D_ATT, tm), lambda i: (i // tpb, 0, i % tpb))
    else:
        att_spec = pl.BlockSpec((tm, D_ATT), lambda i: (i, 0))
    choice = lambda dt: (jax.ShapeDtypeStruct((TOP_K, n), dt), pl.BlockSpec((TOP_K, tm), lambda i: (0, i)))
    outs = [(jax.ShapeDtypeStruct((b, t, d), F32), pl.BlockSpec((nb, tt, d), xmap)),
            (jax.ShapeDtypeStruct((n, d), F32), pl.BlockSpec((tm, d), lambda i: (i, 0))),
            choice(I32), choice(F32), choice(I32),
            (jax.ShapeDtypeStruct((N_EXPERTS, LANES), F32), full((N_EXPERTS, LANES)))]
    return pl.pallas_call(
        functools.partial(_outproj_kernel, att_transposed=att_transposed),
        out_shape=tuple(o[0] for o in outs),
        grid=(n // tm,),
        in_specs=[pl.BlockSpec((nb, tt, d), xmap),
                  pl.BlockSpec((N_OCT, tm // SSM_CHUNK, SSM_CHUNK * LANES), lambda i: (0, i, 0)),
                  att_spec,
                  pl.BlockSpec((nb, 1, d), bmap), pl.BlockSpec((nb, 1, d), bmap), pl.BlockSpec((nb, 1, d), bmap),
                  full((1, d)), full((D_SSM, D_SSM)), full((1, D_SSM)), full((d, d)),
                  full((N_EXPERTS, d)), full((N_EXPERTS, 1)), full((N_EXPERTS, LANES))],
        out_specs=tuple(o[1] for o in outs),
        scratch_shapes=[pltpu.VMEM((N_EXPERTS, LANES), F32), pltpu.VMEM((N_OCT, tm, LANES), F32)],
        compiler_params=_cparams(("arbitrary",)),
        name="outproj_t" if att_transposed else "outproj",
    )(x, y4, att, gt, sh, sc, g_ffn, w["wglu"], w["bglu"], w["wout"], w["wr"], w["br"], counts_in)


def _dispatch_kernel(zst_ref, nu_ref, dest_ref, *rest, tm, tiles):
    h_refs = rest[:len(tiles)]
    xs_ref, zero_ref, hbuf_ref, zsem, sems = rest[len(tiles):]
    i = pl.program_id(0)
    n_blocks = xs_ref.shape[0] // MOE_BLK

    @pl.when(i == 0)
    def _():
        zero_ref[...] = jnp.zeros_like(zero_ref)

        def zero_copy(start):
            start = pl.multiple_of(start, MOE_BLK)
            return pltpu.make_async_copy(zero_ref, xs_ref.at[pl.ds(start, MOE_BLK)], zsem)

        for e in range(N_EXPERTS):
            @pl.when(zst_ref[e] >= 0)
            def _():
                zero_copy(jnp.maximum(zst_ref[e], 0)).start()

        def tail_start(j, carry):
            zero_copy(j * MOE_BLK).start()
            return carry

        def tail_wait(j, carry):
            zero_copy(j * MOE_BLK).wait()
            return carry

        lax.fori_loop(nu_ref[0], n_blocks, tail_start, 0)
        for e in range(N_EXPERTS):
            @pl.when(zst_ref[e] >= 0)
            def _():
                zero_copy(jnp.maximum(zst_ref[e], 0)).wait()
        lax.fori_loop(nu_ref[0], n_blocks, tail_wait, 0)

    slot = i % 2

    def scatter_rows(h_ref):
        hbuf_ref[slot] = h_ref[...]

        def issue(t, carry):
            for kk in range(TOP_K):
                dst = dest_ref[0, 0, t * TOP_K + kk]
                pltpu.make_async_copy(hbuf_ref.at[slot, pl.ds(t, 1)], xs_ref.at[pl.ds(dst, 1)], sems.at[slot]).start()
            return carry

        lax.fori_loop(0, tm, issue, 0, unroll=8)

    first_tile = 0
    for h_ref, n_tiles in zip(h_refs, tiles):
        pl.when((i >= first_tile) & (i < first_tile + n_tiles))(functools.partial(scatter_rows, h_ref))
        first_tile += n_tiles

    def wait_tile(s):
        n_copied = tm * TOP_K
        pltpu.make_async_copy(xs_ref.at[pl.ds(0, n_copied)], xs_ref.at[pl.ds(0, n_copied)], sems.at[s]).wait()

    @pl.when(i > 0)
    def _():
        wait_tile(1 - slot)

    @pl.when(i == pl.num_programs(0) - 1)
    def _():
        wait_tile(slot)


def _dispatch(h2s, dest, zstart, n_used, n_rows, tm):
    d = h2s[0].shape[1]
    tiles = tuple(h.shape[0] // tm for h in h2s)
    nt = sum(tiles)
    dest3 = dest.reshape(nt, 1, tm * TOP_K)
    in_specs = [pl.BlockSpec((1, 1, tm * TOP_K), lambda i, *_: (i, 0, 0), memory_space=pltpu.SMEM)]
    first_tile = 0
    for n_tiles in tiles:
        in_specs.append(pl.BlockSpec(
            (tm, d), lambda i, *_, f=first_tile, m=n_tiles: (jnp.clip(i - f, 0, m - 1), 0)))
        first_tile += n_tiles
    return pl.pallas_call(
        functools.partial(_dispatch_kernel, tm=tm, tiles=tiles),
        out_shape=jax.ShapeDtypeStruct((n_rows, d), F32),
        grid_spec=pltpu.PrefetchScalarGridSpec(
            num_scalar_prefetch=2, grid=(nt,),
            in_specs=in_specs,
            out_specs=pl.BlockSpec(memory_space=pl.ANY),
            scratch_shapes=[pltpu.VMEM((MOE_BLK, d), F32), pltpu.VMEM((2, tm, d), F32),
                            pltpu.SemaphoreType.DMA, pltpu.SemaphoreType.DMA((2,))]),
        compiler_params=_cparams(("arbitrary",)),
        name="moe_dispatch",
    )(zstart, n_used, dest3, *h2s)


def _expert_kernel(be_ref, nxt_ref, slot_ref, first_ref, nu_ref, xs_ref, wup_hbm, bup_ref, wdn_hbm, bdn_ref,
                   ys_ref, wup_f32, wdn_f32, wup_bf, wdn_bf, sems):
    i = pl.program_id(0)
    e = be_ref[i]

    def fetch(expert, slot):
        return (pltpu.make_async_copy(wup_hbm.at[expert], wup_f32.at[slot], sems.at[0, slot]),
                pltpu.make_async_copy(wdn_hbm.at[expert], wdn_f32.at[slot], sems.at[1, slot]))

    @pl.when(i < nu_ref[0])
    def _():
        @pl.when(first_ref[i] == 1)
        def _():
            slot = slot_ref[i]

            @pl.when(i == 0)
            def _():
                for copy in fetch(e, slot):
                    copy.start()

            for copy in fetch(e, slot):
                copy.wait()

            @pl.when(nxt_ref[i] >= 0)
            def _():
                for copy in fetch(jnp.maximum(nxt_ref[i], 0), 1 - slot):
                    copy.start()

            wup_bf[...] = wup_f32[slot].astype(BF16)
            wdn_bf[...] = wdn_f32[slot].astype(BF16)

        up = _dot(xs_ref[...].astype(BF16), wup_bf[...]) + bup_ref[...]
        x_glu = jnp.minimum(up[:, :D_FF], SWIGLU_LIMIT)
        x_lin = jnp.clip(up[:, D_FF:], -SWIGLU_LIMIT, SWIGLU_LIMIT)
        act = (x_lin + 1.0) * (x_glu * jax.nn.sigmoid(SWIGLU_ALPHA * x_glu))
        ys_ref[...] = _dot(act.astype(BF16), wdn_bf[...]) + bdn_ref[...]

    @pl.when(i >= nu_ref[0])
    def _():
        ys_ref[...] = jnp.zeros_like(ys_ref)


def _experts(xs, block_e, n_used, has_rows, w_up, b_up, w_down, b_down):
    n_rows, d = xs.shape
    nblk = n_rows // MOE_BLK
    blocks = jnp.arange(nblk, dtype=I32)
    used = blocks < n_used[0]
    first = (used & ((blocks == 0) | (block_e != jnp.roll(block_e, 1)))).astype(I32)
    slot = ((jnp.cumsum(first) - 1) % 2).astype(I32)
    experts = jnp.arange(N_EXPERTS, dtype=I32)
    later = jnp.where(has_rows[None, :] & (experts[None, :] > experts[:, None]), experts[None, :], N_EXPERTS)
    nxt_e = jnp.min(later, axis=1)
    nxt_e = jnp.where(nxt_e < N_EXPERTS, nxt_e, -1).astype(I32)
    nxt = jnp.sum(jnp.where(block_e[:, None] == experts[None, :], nxt_e[None, :], 0), axis=1).astype(I32)
    rowmap = lambda i, be, nx, sl, fi, nu: (jnp.minimum(i, nu[0] - 1), 0)
    emap = lambda i, be, nx, sl, fi, nu: (be[i], 0, 0)
    return pl.pallas_call(
        _expert_kernel,
        out_shape=jax.ShapeDtypeStruct((n_rows, d), F32),
        grid_spec=pltpu.PrefetchScalarGridSpec(
            num_scalar_prefetch=5, grid=(nblk,),
            in_specs=[pl.BlockSpec((MOE_BLK, d), rowmap),
                      pl.BlockSpec(memory_space=pl.ANY), pl.BlockSpec((None, 1, 2 * D_FF), emap),
                      pl.BlockSpec(memory_space=pl.ANY), pl.BlockSpec((None, 1, d), emap)],
            out_specs=pl.BlockSpec((MOE_BLK, d), lambda i, *_: (i, 0)),
            scratch_shapes=[pltpu.VMEM((2, d, 2 * D_FF), F32), pltpu.VMEM((2, D_FF, d), F32),
                            pltpu.VMEM((d, 2 * D_FF), BF16), pltpu.VMEM((D_FF, d), BF16),
                            pltpu.SemaphoreType.DMA((2, 2))]),
        compiler_params=_cparams(("arbitrary",)),
        name="moe_experts",
    )(block_e, nxt, slot, first, n_used, xs, w_up, b_up, w_down, b_down)


def _combine_kernel(meta_ref, metan_ref, x1_ref, col_ref, gate_ref, gt_ref, gfin_ref, ys_ref, y_ref, ybuf_ref, sems,
                    *, tm):
    i = pl.program_id(0)
    nt = pl.num_programs(0)
    nb, tt, d = x1_ref.shape
    rows = ybuf_ref.shape[1]

    def run_copies(m_ref, slot, start):
        def per_expert(e, carry):
            a = m_ref[0, 0, e]
            n_chunks = m_ref[0, 0, N_EXPERTS + e]
            bo = m_ref[0, 0, 2 * N_EXPERTS + e]

            def per_chunk(c, carry2):
                src = ys_ref.at[pl.ds(pl.multiple_of(a + 8 * c, 8), 8)]
                dst = ybuf_ref.at[slot, pl.ds(pl.multiple_of(bo + 8 * c, 8), 8)]
                copy = pltpu.make_async_copy(src, dst, sems.at[slot])
                if start:
                    copy.start()
                else:
                    copy.wait()
                return carry2

            lax.fori_loop(0, n_chunks, per_chunk, 0)
            return carry

        lax.fori_loop(0, N_EXPERTS, per_expert, 0)

    @pl.when(i == 0)
    def _():
        ybuf_ref[...] = jnp.zeros_like(ybuf_ref)
        run_copies(meta_ref, 0, True)

    @pl.when(i + 1 < nt)
    def _():
        run_copies(metan_ref, (i + 1) % 2, True)

    slot = i % 2
    run_copies(meta_ref, slot, False)
    col = col_ref[...]
    gate = gate_ref[...]
    lane = lax.broadcasted_iota(I32, (tm, rows), 1)
    pick = jnp.where(lane == col[:, 0:1], gate[:, 0:1], 0.0)
    for kk in range(1, TOP_K):
        pick = pick + jnp.where(lane == col[:, kk:kk + 1], gate[:, kk:kk + 1], 0.0)
    moe = _dot(pick.astype(BF16), ybuf_ref[slot].astype(BF16))
    x2 = x1_ref[...] + gt_ref[...] * moe.reshape(nb, tt, d)
    ms = jnp.mean(x2 * x2, axis=-1, keepdims=True)
    y_ref[...] = x2 * lax.rsqrt(ms + NORM_EPS) * gfin_ref[...]


def _combine_plan(idx, dest, off, before0, tm):
    n = idx.shape[1]
    nt = n // tm
    is_e = idx[None, :, :] == jnp.arange(N_EXPERTS, dtype=I32)[:, None, None]
    cnt = jnp.any(is_e, axis=1).astype(I32).reshape(N_EXPERTS, nt, tm).sum(axis=2).T
    before = before0[None, :] + jnp.cumsum(cnt, axis=0) - cnt
    start = off[None, :] + before
    a8 = start // 8 * 8
    b8 = jnp.where(cnt > 0, (start + cnt + 7) // 8 * 8, a8)
    span = b8 - a8
    boff = jnp.cumsum(span, axis=1) - span
    delta = jnp.repeat((boff - a8).T, tm, axis=1)
    col = dest + jnp.sum(jnp.where(is_e, delta[:, None, :], 0), axis=0)
    meta = jnp.concatenate([a8, span // 8, boff, jnp.zeros_like(a8)], axis=1).astype(I32)
    return meta.reshape(nt, 1, 4 * N_EXPERTS), col.T.astype(I32)


def _combine(x1, gate, idx, dest, off, before0, ys, gt, g_final, tm):
    b, t, d = x1.shape
    n = b * t
    tm = min(tm, n)
    tt = min(t, tm)
    nb = tm // tt
    tpb = t // tt
    nt = n // tm
    rows = -(-(tm * TOP_K + 14 * N_EXPERTS) // 256) * 256
    meta, col = _combine_plan(idx, dest, off, before0, tm)

    def xmap(i):
        return (i // tpb, i % tpb, 0) if nb == 1 else (i, 0, 0)

    def bmap(i):
        return (i // tpb, 0, 0) if nb == 1 else (i, 0, 0)

    smem = lambda imap: pl.BlockSpec((1, 1, 4 * N_EXPERTS), imap, memory_space=pltpu.SMEM)
    return pl.pallas_call(
        functools.partial(_combine_kernel, tm=tm),
        out_shape=jax.ShapeDtypeStruct((b, t, d), F32),
        grid=(nt,),
        in_specs=[smem(lambda i: (i, 0, 0)), smem(lambda i: (jnp.minimum(i + 1, nt - 1), 0, 0)),
                  pl.BlockSpec((nb, tt, d), xmap),
                  pl.BlockSpec((tm, TOP_K), lambda i: (i, 0)),
                  pl.BlockSpec((tm, TOP_K), lambda i: (i, 0)),
                  pl.BlockSpec((nb, 1, d), bmap),
                  pl.BlockSpec((1, d), lambda i: (0, 0)),
                  pl.BlockSpec(memory_space=pl.ANY)],
        out_specs=pl.BlockSpec((nb, tt, d), xmap),
        scratch_shapes=[pltpu.VMEM((2, rows, d), F32), pltpu.SemaphoreType.DMA((2,))],
        compiler_params=_cparams(("arbitrary",)),
        name="moe_combine",
    )(meta, meta, x1, col, gate, gt, g_final, ys)


def _moe(streams, counts, g_final, w_up, b_up, w_down, b_down):
    n_total = sum(s["h2"].shape[0] for s in streams)
    cnt = counts[:, 0].astype(I32)
    padded = (cnt + MOE_BLK - 1) // MOE_BLK * MOE_BLK
    pad_end = jnp.cumsum(padded)
    off = (pad_end - padded).astype(I32)
    n_blocks = -(-n_total * TOP_K // MOE_BLK) + N_EXPERTS
    n_rows = n_blocks * MOE_BLK
    starts = jnp.arange(n_blocks, dtype=I32) * MOE_BLK
    block_e = jnp.minimum(jnp.sum((pad_end[None, :] <= starts[:, None]).astype(I32), axis=1), N_EXPERTS - 1)
    n_used = (pad_end[-1:] // MOE_BLK).astype(I32)
    zstart = jnp.where(padded > 0, pad_end - MOE_BLK, -1).astype(I32)
    experts = jnp.arange(N_EXPERTS, dtype=I32)
    dests = []
    for s in streams:
        seg = jnp.sum(jnp.where(s["idx"][None] == experts[:, None, None], off[:, None, None], 0), axis=0)
        dests.append((seg + s["rank"]).astype(I32))
    tm = min([512] + [s["h2"].shape[0] for s in streams])
    dest_rows = jnp.concatenate([dest.T for dest in dests], axis=0)
    xs = _dispatch([s["h2"] for s in streams], dest_rows, zstart, n_used, n_rows, tm)
    ys = _experts(xs, block_e, n_used, padded > 0, w_up, b_up, w_down, b_down)
    return [_combine(s["x1"], s["gate"].T, s["idx"], dest, off, s["before"], ys, s["gt"], g_final, 256)
            for s, dest in zip(streams, dests)]


def _stream(x, mod, w, ssm_ops, s0, cache, params, counts_in):
    b, t, d = x.shape
    n = b * t
    sh_mix, sc_mix, gt_mix, sh_ffn, sc_ffn, gt_ffn = mod
    prompt = cache is None
    tm = 1024 if prompt else 512
    u4, q, kt, vt, lft, vtb, kaug = _inproj(x, sh_mix, sc_mix, params["g_mix"], w, prompt=prompt, tm=tm)
    n_chunks = t // SSM_CHUNK
    y4, s_fin = _ssm(u4, ssm_ops, s0, b, n_chunks)
    if prompt:
        att = _attn_p(q, kaug, vtb, b, t)
    else:
        cache_k, cache_v, cache_lf = cache
        p = cache_k.shape[1]
        ck_t = jnp.transpose(cache_k, (0, 2, 3, 1)).reshape(b, D_ATT, p)
        cv_t = jnp.transpose(cache_v, (0, 2, 3, 1)).reshape(b, D_ATT, p)
        cl_t = jnp.transpose(cache_lf, (0, 2, 1))
        ln_t = jnp.pad(lft, ((0, 0), (0, 0), (0, LANES - t)))
        fc, fn = _fprep_s(cl_t, ln_t)
        att = _attn_s(q.reshape(b, t, D_ATT), ck_t, cv_t, kt, vt, fc, fn).reshape(n, D_ATT)
    x1, h2, idx, gate, rank, counts = _outproj(x, y4, att, gt_mix, sh_ffn, sc_ffn, params["g_ffn"], w, counts_in,
                                               att_transposed=prompt, tm=tm)
    routed = {"x1": x1, "h2": h2, "idx": idx, "gate": gate, "rank": rank, "gt": gt_ffn,
              "before": counts_in[:, 0].astype(I32)}
    heads = lambda a: jnp.transpose(a.reshape(b, N_HEADS, HEAD_DIM, t), (0, 3, 1, 2))[None]
    return routed, counts, heads(kt), heads(vt), jnp.transpose(lft, (0, 2, 1))[None], s_fin


def _state_in(re, im):
    b = re.shape[0]
    s = jnp.concatenate([re.reshape(b, N_OCT, OCT_STATE // 2), im.reshape(b, N_OCT, OCT_STATE // 2)], axis=-1)
    return jnp.transpose(s, (1, 0, 2))


def _state_out(s):
    b = s.shape[1]
    s = jnp.transpose(s, (1, 0, 2))
    half = OCT_STATE // 2
    re = s[:, :, :half].reshape(1, b, N_SSM_GROUPS, SSM_STATE)
    im = s[:, :, half:].reshape(1, b, N_SSM_GROUPS, SSM_STATE)
    return re, im


def kernel(x_prompt, x_sample, c_prompt, c_sample, cache_k, cache_v, cache_logf, state_ssm_re, state_ssm_im, w_ada, b_ada, g_mix, w_in, b_forget, ssm_log_dt, ssm_a_re, ssm_a_im, ssm_b_re, ssm_b_im, ssm_c_re, ssm_c_im, ssm_d, w_glu, b_glu, w_out, g_ffn, w_router, b_router, w_up, b_up, w_down, b_down, g_final):
    assert w_ada.shape[0] == 1, "single-layer trunk"
    d = D_MODEL
    bp, tp, _ = x_prompt.shape
    bs, ts, _ = x_sample.shape
    n_c = bp + bs
    c_all = jnp.pad(jnp.concatenate([c_prompt, c_sample], axis=0), ((0, -n_c % 8), (0, 0)))
    mod = _ada(c_all, w_ada[0], b_ada[0].reshape(1, 6 * d))
    mod_p = [mod[:bp, j * d:(j + 1) * d].reshape(bp, 1, d) for j in range(6)]
    mod_s = [mod[bp:n_c, j * d:(j + 1) * d].reshape(bs, 1, d) for j in range(6)]
    wi = w_in[0]
    o1, o2, o3, o4 = D_SSM, D_SSM + D_ATT, D_SSM + 2 * D_ATT, D_SSM + 3 * D_ATT
    w_q = wi[:, o1:o2].astype(BF16)
    w_k = wi[:, o2:o3].astype(BF16)
    unused = jnp.zeros((8, LANES), BF16)
    base = {"wu": wi[:, :o1].astype(BF16), "wkt": w_k.T, "wvt": wi[:, o3:o4].astype(BF16).T,
            "wft": wi[:, o4:].astype(BF16).T, "bfc": b_forget[0].reshape(N_HEADS, 1), "pm": _bias_placement(),
            "wglu": w_glu[0].astype(BF16), "bglu": b_glu[0].reshape(1, D_SSM), "wout": w_out[0].astype(BF16),
            "wr": w_router[0].T, "br": b_router[0].reshape(N_EXPERTS, 1)}
    w_p = dict(base, wq=w_q.T, wk=w_k)
    w_s = dict(base, wq=w_q, wk=unused)
    params = {"g_mix": g_mix[0].reshape(1, d), "g_ffn": g_ffn[0].reshape(1, d), "g_final": g_final.reshape(1, d),
              "w_up": w_up[0], "b_up": b_up[0].reshape(N_EXPERTS, 1, 2 * D_FF),
              "w_down": w_down[0], "b_down": b_down[0].reshape(N_EXPERTS, 1, d)}
    ssm_ops = _ssm_operators(ssm_log_dt[0], ssm_a_re[0], ssm_a_im[0], ssm_b_re[0], ssm_b_im[0],
                             ssm_c_re[0], ssm_c_im[0], ssm_d[0])
    zero_state = jnp.zeros((N_OCT, bp, OCT_STATE), F32)
    counts0 = jnp.zeros((N_EXPERTS, LANES), F32)
    r_p, counts_p, k_p, v_p, f_p, s_p = _stream(x_prompt, mod_p, w_p, ssm_ops, zero_state, None, params, counts0)
    s0 = _state_in(state_ssm_re[0], state_ssm_im[0])
    r_s, counts, k_s, v_s, f_s, s_s = _stream(x_sample, mod_s, w_s, ssm_ops, s0,
                                              (cache_k[0], cache_v[0], cache_logf[0]), params, counts_p)
    y_p, y_s = _moe([r_p, r_s], counts, params["g_final"],
                    params["w_up"], params["b_up"], params["w_down"], params["b_down"])
    sre_p, sim_p = _state_out(s_p)
    sre_s, sim_s = _state_out(s_s)
    return (y_p, y_s, k_p, v_p, f_p, sre_p, sim_p, k_s, v_s, f_s, sre_s, sim_s)
```

```python
import functools
import math

import jax
import jax.numpy as jnp
from jax import lax
from jax.experimental import pallas as pl
from jax.experimental.pallas import tpu as pltpu

F32 = jnp.float32
BF16 = jnp.bfloat16
I32 = jnp.int32

D_MODEL = 1024
D_SSM = 512
SSM_GROUP = 16
N_SSM_GROUPS = 32
SSM_STATE = 64
D_ATT = 512
HEAD_DIM = 64
N_HEADS = 8
N_EXPERTS = 32
TOP_K = 4
D_FF = 1024
SWIGLU_LIMIT = 7.0
SWIGLU_ALPHA = 1.702
NORM_EPS = 1e-5

LANES = 128
N_OCT = D_SSM // LANES
OCT_STATE = 2 * (N_SSM_GROUPS // N_OCT) * SSM_STATE
SSM_CHUNK = 8
SSM_ROWS = 512
SSM_INTERLEAVE = 4
ATT_BLK = 256
ATT_PAIRS = 4
LOG2E = math.log2(math.e)
SAMPLE_KV_BLK = 2048
MOE_BLK = 256
RUN_ALIGN = 16
WAIT_GROUP = 16
VMEM_LIMIT = 52 * 1024 * 1024


def _cparams(sem, vmem=VMEM_LIMIT):
    return pltpu.CompilerParams(dimension_semantics=sem, vmem_limit_bytes=vmem)


def _dot(a, b):
    return jnp.dot(a, b, preferred_element_type=F32)


def _dot_nt(a, b):
    return lax.dot_general(a, b, (((1,), (1,)), ((), ())), preferred_element_type=F32)


def _dot_hi(a, b):
    return jnp.dot(a, b, preferred_element_type=F32, precision=lax.Precision.HIGHEST)


def _ada_kernel(c_ref, w_ref, b_ref, o_ref):
    c = c_ref[...]
    s = c * jax.nn.sigmoid(c)
    o_ref[...] = _dot_hi(s, w_ref[...]) + b_ref[...]


def _ada(c_all, w_ada, b_ada):
    m, d = c_all.shape
    n = w_ada.shape[1]
    return pl.pallas_call(
        _ada_kernel,
        out_shape=jax.ShapeDtypeStruct((m, n), F32),
        grid=(n // d,),
        in_specs=[pl.BlockSpec((m, d), lambda j: (0, 0)),
                  pl.BlockSpec((d, d), lambda j: (0, j)),
                  pl.BlockSpec((1, d), lambda j: (0, j))],
        out_specs=pl.BlockSpec((m, d), lambda j: (0, j)),
        compiler_params=_cparams(("arbitrary",)),
        name="ada",
    )(c_all, w_ada, b_ada)


def _norm_mod(x, g, scale, shift):
    ms = jnp.mean(x * x, axis=-1, keepdims=True)
    y = x * lax.rsqrt(ms + NORM_EPS) * g
    return y * (1.0 + scale) + shift


def _log_sigmoid(z):
    return jnp.minimum(z, 0.0) - jnp.log1p(jnp.exp(-jnp.abs(z)))


def _inproj_kernel(x_ref, sh_ref, sc_ref, g_ref, wu_ref, wq_ref, wkt_ref, wvt_ref, wft_ref, bfc_ref,
                   wk_ref, pm_ref,
                   u4_ref, q_ref, kt_ref, vt_ref, lft_ref, vtb_ref, ka_ref, us_ref, carry_ref, *, prompt, tpb):
    nb, tt, d = x_ref.shape
    tm = nb * tt
    h = _norm_mod(x_ref[...], g_ref[...], sc_ref[...], sh_ref[...]).reshape(tm, d).astype(BF16)
    u = _dot(h, wu_ref[...])
    for o in range(N_OCT):
        us_ref[o] = u[:, o * LANES:(o + 1) * LANES]
    for s in range(SSM_CHUNK):
        for o in range(N_OCT):
            piece = us_ref[o, pl.ds(s, tm // SSM_CHUNK, stride=SSM_CHUNK), :]
            u4_ref[o, :, s * LANES:(s + 1) * LANES] = piece.astype(BF16)
    kt = _dot_nt(wkt_ref[...], h)
    vt = _dot_nt(wvt_ref[...], h)
    lft = _log_sigmoid(_dot_nt(wft_ref[...], h) + bfc_ref[...])
    scale = HEAD_DIM ** -0.5
    if prompt:
        kt_ref[...] = kt
        vt_ref[...] = vt
        lft_ref[...] = lft
        qt = (_dot_nt(wq_ref[...], h) * (scale * LOG2E)).astype(BF16)
        vtb = vt.astype(BF16)
        for j in range(tm // ATT_BLK):
            q_ref[j] = qt[:, j * ATT_BLK:(j + 1) * ATT_BLK]
            vtb_ref[j] = vtb[:, j * ATT_BLK:(j + 1) * ATT_BLK]
        k_tok = _dot(h, wk_ref[...])

        @pl.when(pl.program_id(0) % tpb == 0)
        def _():
            carry_ref[...] = jnp.zeros_like(carry_ref)

        cum = _lane_cumsum(lft) + carry_ref[:, 0:1]
        carry_ref[...] = jnp.broadcast_to(cum[:, tm - 1:tm], carry_ref.shape)
        nf = cum * (-LOG2E)
        hi = nf.astype(BF16).astype(F32)
        r1 = nf - hi
        mid = r1.astype(BF16).astype(F32)
        lo = (r1 - mid).astype(BF16).astype(F32)
        place = lambda piece, pm: lax.dot_general(piece, pm, (((0,), (0,)), ((), ())), preferred_element_type=F32)
        slab = place(hi, pm_ref[0]) + place(mid, pm_ref[1]) + place(lo, pm_ref[2])
        for j in range(tm // ATT_BLK):
            rows = slice(j * ATT_BLK, (j + 1) * ATT_BLK)
            for p in range(N_HEADS // 2):
                ka_ref[j, :, 2 * p * LANES:(2 * p + 1) * LANES] = k_tok[rows, p * LANES:(p + 1) * LANES].astype(BF16)
                ka_ref[j, :, (2 * p + 1) * LANES:(2 * p + 2) * LANES] = (
                    slab[rows, p * LANES:(p + 1) * LANES].astype(BF16))
    else:
        for bl in range(nb):
            kt_ref[bl] = kt[:, bl * tt:(bl + 1) * tt]
            vt_ref[bl] = vt[:, bl * tt:(bl + 1) * tt]
            lft_ref[bl] = lft[:, bl * tt:(bl + 1) * tt]
        q_ref[...] = (_dot(h, wq_ref[...]) * scale).astype(BF16)
        vtb_ref[...] = jnp.zeros_like(vtb_ref)
        ka_ref[...] = jnp.zeros_like(ka_ref)


def _inproj(x, shift, scale, g_mix, w, *, prompt, tm):
    b, t, d = x.shape
    n = b * t
    tm = min(tm, n)
    tt = min(t, tm)
    nb = tm // tt
    tpb = t // tt
    assert (nb == 1) == prompt
    grid = (n // tm,)

    def xmap(i):
        return (i // tpb, i % tpb, 0) if nb == 1 else (i, 0, 0)

    def bmap(i):
        return (i // tpb, 0, 0) if nb == 1 else (i, 0, 0)

    def tmap(i):
        return (i // tpb, 0, i % tpb) if nb == 1 else (i, 0, 0)

    full = lambda a: pl.BlockSpec(a.shape, lambda i: (0,) * a.ndim)
    names = ("wu", "wq", "wkt", "wvt", "wft", "bfc", "wk", "pm")
    in_specs = [pl.BlockSpec((nb, tt, d), xmap),
                pl.BlockSpec((nb, 1, d), bmap), pl.BlockSpec((nb, 1, d), bmap),
                pl.BlockSpec((1, d), lambda i: (0, 0))] + [full(w[k]) for k in names]
    tblk = lambda rows: pl.BlockSpec((None if nb == 1 else nb, rows, tt), tmap)
    dummy = (jax.ShapeDtypeStruct((8, LANES), F32), pl.BlockSpec((8, LANES), lambda i: (0, 0)))
    if prompt:
        nblk = n // ATT_BLK
        blocked = (jax.ShapeDtypeStruct((nblk, D_ATT, ATT_BLK), BF16),
                   pl.BlockSpec((tm // ATT_BLK, D_ATT, ATT_BLK), lambda i: (i, 0, 0)))
        q_out, vtb_out = blocked, blocked
        ka_out = (jax.ShapeDtypeStruct((nblk, ATT_BLK, 2 * D_ATT), BF16),
                  pl.BlockSpec((tm // ATT_BLK, ATT_BLK, 2 * D_ATT), lambda i: (i, 0, 0)))
    else:
        q_out = (jax.ShapeDtypeStruct((n, D_ATT), BF16), pl.BlockSpec((tm, D_ATT), lambda i: (i, 0)))
        vtb_out = ka_out = dummy
    kdim = SSM_CHUNK * LANES
    outs = [(jax.ShapeDtypeStruct((N_OCT, n // SSM_CHUNK, kdim), BF16),
             pl.BlockSpec((N_OCT, tm // SSM_CHUNK, kdim), lambda i: (0, i, 0))),
            q_out,
            (jax.ShapeDtypeStruct((b, D_ATT, t), F32), tblk(D_ATT)),
            (jax.ShapeDtypeStruct((b, D_ATT, t), F32), tblk(D_ATT)),
            (jax.ShapeDtypeStruct((b, N_HEADS, t), F32), tblk(N_HEADS)),
            vtb_out, ka_out]
    return pl.pallas_call(
        functools.partial(_inproj_kernel, prompt=prompt, tpb=tpb),
        out_shape=tuple(o[0] for o in outs), grid=grid, in_specs=in_specs, out_specs=tuple(o[1] for o in outs),
        scratch_shapes=[pltpu.VMEM((N_OCT, tm, LANES), F32), pltpu.VMEM((8, LANES), F32)],
        compiler_params=_cparams(("arbitrary",)),
        name="inproj_p" if prompt else "inproj_s",
    )(x, shift, scale, g_mix, *[w[k] for k in names])


def _ssm_operators(log_dt, a_re, a_im, b_re, b_im, c_re, c_im, d_skip):
    L, G, P, C = SSM_CHUNK, N_SSM_GROUPS, SSM_STATE, SSM_GROUP
    q = G // N_OCT
    lam = lax.complex(a_re, a_im)
    dt = jnp.exp(log_dt)[:, None]
    lam_dt = lam * dt
    lam_bar = jnp.exp(lam_dt)
    b_bar = ((lam_bar - 1.0) / lam)[:, :, None] * lax.complex(b_re, b_im)
    c_mat = lax.complex(c_re, c_im)
    pw = jnp.exp(lam_dt[None] * jnp.arange(L + 1, dtype=F32)[:, None, None])
    kern = jnp.real(jnp.einsum('gcp,dgp,gpk->dgck', c_mat, pw[:L], b_bar))
    ksm = jnp.transpose(kern, (1, 0, 3, 2)).reshape(N_OCT, q, L, C, C)
    ksm = jnp.transpose(ksm, (0, 2, 1, 3, 4)).reshape(N_OCT, L, q * C, C)
    pwr = pw[L - 1 - jnp.arange(L)]
    wb = jnp.einsum('jgp,gpk->jgkp', pwr, b_bar)
    wsm = jnp.stack([jnp.real(wb), jnp.imag(wb)], axis=0).reshape(2, L, N_OCT, q * C, P)
    wsm = jnp.transpose(wsm, (2, 1, 0, 3, 4))
    cl = c_mat[None] * pw[1:L + 1][:, :, None, :]
    vsm = jnp.stack([jnp.real(cl), -jnp.imag(cl)], axis=0).reshape(2, L, N_OCT, q, C, P)
    vsm = jnp.transpose(vsm, (2, 1, 0, 3, 5, 4)).reshape(N_OCT, L, OCT_STATE, C)
    tmat, wend, win = _ssm_prep(ksm, wsm, vsm, d_skip.reshape(N_OCT, 1, LANES))
    dec = pw[L].reshape(N_OCT, q * P)
    decay = jnp.concatenate([jnp.real(dec), jnp.imag(dec)], axis=-1).reshape(N_OCT, 1, OCT_STATE)
    return tmat, wend, win, decay.astype(F32)


def _ssm_prep_kernel(ksm_ref, wsm_ref, vsm_ref, d_ref, tmat_ref, wend_ref, win_ref):
    L, C, P = SSM_CHUNK, SSM_GROUP, SSM_STATE
    half = OCT_STATE // 2

    def spread(period, width):
        r = lax.broadcasted_iota(I32, (period, width), 0)
        c = lax.broadcasted_iota(I32, (period, width), 1)
        return jnp.where(c % period == r, 1.0, 0.0)

    def same_group(shape, row_div, row_mod, lane_div):
        r = lax.broadcasted_iota(I32, shape, 0)
        c = lax.broadcasted_iota(I32, shape, 1)
        return (r % row_mod) // row_div == c // lane_div

    e_c = spread(C, LANES)
    e_p = spread(P, half)
    grp = same_group((LANES, LANES), C, LANES, C)
    rr = lax.broadcasted_iota(I32, (LANES, LANES), 0)
    cc = lax.broadcasted_iota(I32, (LANES, LANES), 1)
    lag = []
    for d in range(L):
        blk = jnp.where(grp, _dot_hi(ksm_ref[d], e_c), 0.0)
        if d == 0:
            blk = blk + jnp.where(rr == cc, jnp.broadcast_to(d_ref[...], (LANES, LANES)), 0.0)
        lag.append(blk.astype(BF16))
    zero = jnp.zeros((LANES, LANES), BF16)
    for j in range(L):
        for t in range(L):
            tmat_ref[j * LANES:(j + 1) * LANES, t * LANES:(t + 1) * LANES] = lag[t - j] if t >= j else zero
    grp_e = same_group((LANES, half), C, LANES, P)
    for j in range(L):
        for part in range(2):
            blk = jnp.where(grp_e, _dot_hi(wsm_ref[j, part], e_p), 0.0)
            wend_ref[j * LANES:(j + 1) * LANES, part * half:(part + 1) * half] = blk.astype(BF16)
    grp_i = same_group((OCT_STATE, LANES), P, half, C)
    for t in range(L):
        blk = jnp.where(grp_i, _dot_hi(vsm_ref[t], e_c), 0.0)
        win_ref[:, t * LANES:(t + 1) * LANES] = blk.astype(BF16)


def _ssm_prep(ksm, wsm, vsm, dvec):
    kdim = SSM_CHUNK * LANES
    blk = lambda a: pl.BlockSpec((None,) + a.shape[1:], lambda o: (o,) + (0,) * (a.ndim - 1))
    out = lambda r, c: (jax.ShapeDtypeStruct((N_OCT, r, c), BF16), pl.BlockSpec((None, r, c), lambda o: (o, 0, 0)))
    outs = [out(kdim, kdim), out(kdim, OCT_STATE), out(OCT_STATE, kdim)]
    return pl.pallas_call(
        _ssm_prep_kernel,
        out_shape=tuple(o[0] for o in outs),
        grid=(N_OCT,),
        in_specs=[blk(ksm), blk(wsm), blk(vsm), blk(dvec)],
        out_specs=tuple(o[1] for o in outs),
        compiler_params=_cparams(("arbitrary",)),
        name="ssm_prep",
    )(ksm, wsm, vsm, dvec)


def _ssm_kernel(u_ref, tmat_ref, wend_ref, win_ref, dec_ref, s0_ref, y_ref, sfin_ref, e_ref, sp_ref, st_ref,
                *, nb, cpt):
    i = pl.program_id(1)
    half = OCT_STATE // 2
    rows = nb * cpt

    @pl.when(i == 0)
    def _():
        st_ref[...] = s0_ref[...]

    u = u_ref[...].reshape(rows, u_ref.shape[-1])
    e_ref[...] = _dot(u, wend_ref[...])
    dec = dec_ref[...]
    ar, ai = dec[:, :half], dec[:, half:]

    def advance(row, st):
        sp_ref[pl.ds(row, 1), :] = st
        e = e_ref[pl.ds(row, 1), :]
        re, im = st[:, :half], st[:, half:]
        nre = ar * re - ai * im + e[:, :half]
        nim = ar * im + ai * re + e[:, half:]
        return jnp.concatenate([nre, nim], axis=-1)

    if nb <= SSM_INTERLEAVE:
        def per_chunk(r, sts):
            return tuple(advance(bb * cpt + r, sts[bb]) for bb in range(nb))

        sts = lax.fori_loop(0, cpt, per_chunk, tuple(st_ref[bb:bb + 1, :] for bb in range(nb)))
        for bb in range(nb):
            st_ref[bb:bb + 1, :] = sts[bb]
    else:
        def per_batch(bb, carry):
            st = lax.fori_loop(0, cpt, lambda r, st: advance(bb * cpt + r, st), st_ref[pl.ds(bb, 1), :])
            st_ref[pl.ds(bb, 1), :] = st
            return carry

        lax.fori_loop(0, nb, per_batch, 0)
    y = _dot(u, tmat_ref[...]) + _dot(sp_ref[...].astype(BF16), win_ref[...])
    y_ref[...] = jax.nn.gelu(y).astype(BF16).reshape(y_ref.shape)
    sfin_ref[...] = st_ref[...]


def _ssm(u4, ops, s0, batch, n_chunks):
    tmat, wend, win, decay = ops
    kdim = SSM_CHUNK * LANES
    cpt = SSM_ROWS if n_chunks % SSM_ROWS == 0 else n_chunks
    tiles = n_chunks // cpt
    if tiles > 1:
        u_in = u4.reshape(N_OCT, batch, n_chunks, kdim)
        u_spec = pl.BlockSpec((None, batch, cpt, kdim), lambda o, i: (o, 0, i, 0))
    else:
        u_in = u4
        u_spec = pl.BlockSpec((None, batch * n_chunks, kdim), lambda o, i: (o, 0, 0))
    r = batch * cpt
    wspec = lambda shape: pl.BlockSpec((None,) + shape, lambda o, i: (o, 0, 0))
    y, s_fin = pl.pallas_call(
        functools.partial(_ssm_kernel, nb=batch, cpt=cpt),
        out_shape=(jax.ShapeDtypeStruct(u_in.shape, BF16),
                   jax.ShapeDtypeStruct((N_OCT, batch, OCT_STATE), F32)),
        grid=(N_OCT, tiles),
        in_specs=[u_spec,
                  wspec((kdim, kdim)), wspec((kdim, OCT_STATE)), wspec((OCT_STATE, kdim)), wspec((1, OCT_STATE)),
                  wspec((batch, OCT_STATE))],
        out_specs=(u_spec, wspec((batch, OCT_STATE))),
        scratch_shapes=[pltpu.VMEM((r, OCT_STATE), F32), pltpu.VMEM((r, OCT_STATE), F32),
                        pltpu.VMEM((batch, OCT_STATE), F32)],
        compiler_params=_cparams(("arbitrary", "arbitrary")),
        name="ssm",
    )(u_in, tmat, wend, win, decay, s0)
    return y.reshape(u4.shape), s_fin


def _bias_placement():
    import numpy as np
    pm = np.zeros((3, N_HEADS, N_HEADS // 2 * LANES), np.float32)
    for piece in range(3):
        for h in range(N_HEADS):
            pm[piece, h, LANES * (h // 2) + 3 * (h % 2) + piece] = 1.0
    return jnp.asarray(pm)


def _attn_p_kernel(qt_ref, ka_ref, vt_ref, o_ref, qbd_ref, acc_ref, sa_ref, sb_ref):
    i = pl.program_id(2)
    bq = ATT_BLK
    hd = HEAD_DIM
    r = lax.broadcasted_iota(I32, (LANES, 2 * bq), 0)
    c = lax.broadcasted_iota(I32, (LANES, 2 * bq), 1)
    ones = jnp.where(((r < 3) & (c < bq)) | ((r >= 3) & (r < 6) & (c >= bq)), 1.0, 0.0).astype(BF16)
    zero = jnp.zeros((hd, bq), BF16)
    for pp in range(ATT_PAIRS):
        qb = qt_ref[pp * LANES:(pp + 1) * LANES, :]
        qbd_ref[pp, 0:hd, 0:bq] = qb[0:hd]
        qbd_ref[pp, 0:hd, bq:2 * bq] = zero
        qbd_ref[pp, hd:2 * hd, 0:bq] = zero
        qbd_ref[pp, hd:2 * hd, bq:2 * bq] = qb[hd:2 * hd]
        qbd_ref[pp, 2 * hd:2 * hd + LANES, :] = ones
    acc_ref[...] = jnp.zeros_like(acc_ref)

    nq = ka_ref.shape[0]

    def scores(s_ref, j):
        for pp in range(ATT_PAIRS):
            s_ref[pp] = _dot(ka_ref[j, :, pp * 2 * LANES:(pp + 1) * 2 * LANES], qbd_ref[pp])

    def attend(s_ref, j, carry, masked):
        out = []
        for pp in range(ATT_PAIRS):
            m_prev, l_prev = carry[2 * pp], carry[2 * pp + 1]
            s = s_ref[pp]
            if masked:
                key = j * bq + lax.broadcasted_iota(I32, s.shape, 0)
                qq = lax.broadcasted_iota(I32, s.shape, 1)
                qq = i * bq + jnp.where(qq >= bq, qq - bq, qq)
                s = jnp.where(key <= qq, s, -jnp.inf)
            m_new = jnp.maximum(m_prev, jnp.max(s, axis=0, keepdims=True))
            alpha = jnp.exp2(m_prev - m_new)
            p = jnp.exp2(s - m_new)
            l_new = alpha * l_prev + jnp.sum(p, axis=0, keepdims=True)
            pb = p.astype(BF16)
            vb = vt_ref[jnp.minimum(j, nq - 1), pp * LANES:(pp + 1) * LANES, :]
            r0 = pp * LANES
            acc_ref[r0:r0 + hd] = alpha[:, 0:bq] * acc_ref[r0:r0 + hd] + _dot(vb[0:hd], pb[:, 0:bq])
            acc_ref[r0 + hd:r0 + 2 * hd] = (alpha[:, bq:2 * bq] * acc_ref[r0 + hd:r0 + 2 * hd]
                                            + _dot(vb[hd:2 * hd], pb[:, bq:2 * bq]))
            out.extend((m_new, l_new))
        return tuple(out)

    def double_step(m, carry):
        j = 2 * m
        scores(sb_ref, j + 1)
        carry = attend(sa_ref, j, carry, False)
        scores(sa_ref, j + 2)
        return attend(sb_ref, j + 1, carry, False)

    init = (jnp.full((1, 2 * bq), -jnp.inf, F32), jnp.zeros((1, 2 * bq), F32)) * ATT_PAIRS
    scores(sa_ref, 0)
    carry = lax.fori_loop(0, i // 2, double_step, init)
    j = 2 * (i // 2)
    scores(sb_ref, jnp.minimum(j + 1, nq - 1))
    carry = attend(sa_ref, j, carry, True)
    carry = attend(sb_ref, j + 1, carry, True)
    for pp in range(ATT_PAIRS):
        l = carry[2 * pp + 1]
        r0 = pp * LANES
        o_ref[r0:r0 + hd] = acc_ref[r0:r0 + hd] / l[:, 0:bq]
        o_ref[r0 + hd:r0 + 2 * hd] = acc_ref[r0 + hd:r0 + 2 * hd] / l[:, bq:2 * bq]


def _attn_p(qt, kaug, vt, batch, t):
    blk = ATT_BLK
    nq = t // blk
    ngrp = N_HEADS // 2 // ATT_PAIRS
    rows = ATT_PAIRS * LANES
    qt4 = qt.reshape(batch, nq, D_ATT, blk)
    vt4 = vt.reshape(batch, nq, D_ATT, blk)
    ka4 = kaug.reshape(batch, nq, blk, 2 * D_ATT)
    return pl.pallas_call(
        _attn_p_kernel,
        out_shape=jax.ShapeDtypeStruct((batch, D_ATT, t), F32),
        grid=(batch, ngrp, nq),
        in_specs=[pl.BlockSpec((None, None, rows, blk), lambda b, p, i: (b, i, p, 0)),
                  pl.BlockSpec((None, nq, blk, 2 * rows), lambda b, p, i: (b, 0, 0, p), pipeline_mode=pl.Buffered(1)),
                  pl.BlockSpec((None, nq, rows, blk), lambda b, p, i: (b, 0, p, 0), pipeline_mode=pl.Buffered(1))],
        out_specs=pl.BlockSpec((None, rows, blk), lambda b, p, i: (b, p, i)),
        scratch_shapes=[pltpu.VMEM((ATT_PAIRS, 2 * LANES, 2 * blk), BF16), pltpu.VMEM((rows, blk), F32),
                        pltpu.VMEM((ATT_PAIRS, blk, 2 * blk), F32), pltpu.VMEM((ATT_PAIRS, blk, 2 * blk), F32)],
        compiler_params=_cparams(("arbitrary", "arbitrary", "arbitrary")),
        name="attn_p",
    )(qt4, ka4, vt4)


def _lane_cumsum(x):
    n = x.shape[-1]
    lane = lax.broadcasted_iota(I32, x.shape, x.ndim - 1)
    s = 1
    while s < n:
        x = x + jnp.where(lane >= s, pltpu.roll(x, s, x.ndim - 1), 0.0)
        s *= 2
    return x


def _fprep_s_kernel(cl_ref, ln_ref, fc_ref, fn_ref):
    b, h, p = cl_ref.shape
    cum = _lane_cumsum(cl_ref[...].reshape(b * h, p))
    fc_ref[...] = (-cum).reshape(b, h, p)
    total = cum[:, p - 1:p]
    cn = _lane_cumsum(ln_ref[...].reshape(b * h, LANES))
    fn_ref[...] = (-(total + cn)).reshape(b, h, LANES)


def _fprep_s(cache_lf_t, new_lf_t):
    b, h, p = cache_lf_t.shape
    return pl.pallas_call(
        _fprep_s_kernel,
        out_shape=(jax.ShapeDtypeStruct((b, h, p), F32), jax.ShapeDtypeStruct((b, h, LANES), F32)),
        grid=(1,),
        in_specs=[pl.BlockSpec((b, h, p), lambda i: (0, 0, 0)), pl.BlockSpec((b, h, LANES), lambda i: (0, 0, 0))],
        out_specs=(pl.BlockSpec((b, h, p), lambda i: (0, 0, 0)), pl.BlockSpec((b, h, LANES), lambda i: (0, 0, 0))),
        compiler_params=_cparams(("arbitrary",)),
        name="fprep_s",
    )(cache_lf_t, new_lf_t)


def _attn_s_kernel(q_ref, ck_ref, cv_ref, kn_ref, vn_ref, fc_ref, fn_ref, o_ref, qbd_ref, m_ref, l_ref, acc_ref,
                   *, tq):
    j = pl.program_id(1)
    nkv = pl.num_programs(1)
    rows = N_HEADS * tq
    bk = ck_ref.shape[1]

    @pl.when(j == 0)
    def _():
        q = q_ref[...]
        qrep = jnp.broadcast_to(q[None], (N_HEADS, tq, D_ATT)).reshape(rows, D_ATT)
        rh = lax.broadcasted_iota(I32, (rows, D_ATT), 0) // tq
        ch = lax.broadcasted_iota(I32, (rows, D_ATT), 1) // HEAD_DIM
        qbd_ref[...] = jnp.where(rh == ch, qrep, jnp.zeros_like(qrep))
        m_ref[...] = jnp.full_like(m_ref, -jnp.inf)
        l_ref[...] = jnp.zeros_like(l_ref)
        acc_ref[...] = jnp.zeros_like(acc_ref)

    def update(s, vt):
        m_prev = m_ref[...]
        m_new = jnp.maximum(m_prev, jnp.max(s, axis=1, keepdims=True))
        alpha = jnp.exp(m_prev - m_new)
        p = jnp.exp(s - m_new)
        l_ref[...] = alpha * l_ref[...] + jnp.sum(p, axis=1, keepdims=True)
        m_ref[...] = m_new
        acc_ref[...] = alpha * acc_ref[...] + _dot_nt(p.astype(BF16), vt)

    def bias(f, width):
        return jnp.broadcast_to(f[:, None, :], (N_HEADS, tq, width)).reshape(rows, width)

    s = _dot(qbd_ref[...], ck_ref[...].astype(BF16)) + bias(fc_ref[...], bk)
    update(s, cv_ref[...].astype(BF16))

    @pl.when(j == nkv - 1)
    def _():
        s2 = _dot(qbd_ref[...], kn_ref[...].astype(BF16)) + bias(fn_ref[...][:, 0:tq], tq)
        key = lax.broadcasted_iota(I32, (rows, tq), 1)
        qq = lax.broadcasted_iota(I32, (rows, tq), 0) % tq
        update(jnp.where(key <= qq, s2, -jnp.inf), vn_ref[...].astype(BF16))
        o = acc_ref[...] / l_ref[...]
        rh = lax.broadcasted_iota(I32, (rows, D_ATT), 0) // tq
        ch = lax.broadcasted_iota(I32, (rows, D_ATT), 1) // HEAD_DIM
        o = jnp.where(rh == ch, o, 0.0).reshape(N_HEADS, tq, D_ATT)
        o_ref[...] = jnp.sum(o, axis=0)


def _attn_s(q, cache_k, cache_v, k_new, v_new, fc, fn):
    b, tq, _ = q.shape
    p = cache_k.shape[2]
    bk = min(SAMPLE_KV_BLK, p)
    nkv = p // bk
    rows = N_HEADS * tq
    tok = lambda: pl.BlockSpec((None, tq, D_ATT), lambda bi, j: (bi, 0, 0))
    new = lambda: pl.BlockSpec((None, D_ATT, tq), lambda bi, j: (bi, 0, 0))
    return pl.pallas_call(
        functools.partial(_attn_s_kernel, tq=tq),
        out_shape=jax.ShapeDtypeStruct((b, tq, D_ATT), F32),
        grid=(b, nkv),
        in_specs=[tok(),
                  pl.BlockSpec((None, D_ATT, bk), lambda bi, j: (bi, 0, j)),
                  pl.BlockSpec((None, D_ATT, bk), lambda bi, j: (bi, 0, j)),
                  new(), new(),
                  pl.BlockSpec((None, N_HEADS, bk), lambda bi, j: (bi, 0, j)),
                  pl.BlockSpec((None, N_HEADS, LANES), lambda bi, j: (bi, 0, 0))],
        out_specs=tok(),
        scratch_shapes=[pltpu.VMEM((rows, D_ATT), BF16), pltpu.VMEM((rows, 1), F32),
                        pltpu.VMEM((rows, 1), F32), pltpu.VMEM((rows, D_ATT), F32)],
        compiler_params=_cparams(("arbitrary", "arbitrary")),
        name="attn_s",
    )(q, cache_k, cache_v, k_new, v_new, fc, fn)


def _outproj_kernel(x_ref, y4_ref, att_ref, gt_ref, sh_ref, sc_ref, gf_ref, wglu_ref, bglu_ref, wout_ref,
                    wr_ref, br_ref, cin_ref,
                    x1_ref, h2_ref, idx_ref, gate_ref, rank_ref, cnt_ref, carry_ref, ys_ref, *, att_transposed):
    i = pl.program_id(0)
    nb, tt, d = x_ref.shape
    tm = nb * tt

    @pl.when(i == 0)
    def _():
        carry_ref[...] = cin_ref[...]

    for s in range(SSM_CHUNK):
        for o in range(N_OCT):
            ys_ref[o, pl.ds(s, tm // SSM_CHUNK, stride=SSM_CHUNK), :] = (
                y4_ref[o, :, s * LANES:(s + 1) * LANES].astype(F32))
    ysf = jnp.concatenate([ys_ref[o] for o in range(N_OCT)], axis=-1)
    glu = ysf * jax.nn.sigmoid(_dot(ysf.astype(BF16), wglu_ref[...]) + bglu_ref[...])
    att = att_ref[...]
    if att_transposed:
        att = att.T
    mix = _dot(glu.astype(BF16), wout_ref[0:D_SSM, :]) + _dot(att.astype(BF16), wout_ref[D_SSM:, :])
    x1 = x_ref[...] + gt_ref[...] * mix.reshape(nb, tt, d)
    x1_ref[...] = x1
    h2 = _norm_mod(x1, gf_ref[...], sc_ref[...], sh_ref[...]).reshape(tm, d)
    h2_ref[...] = h2

    logits = lax.dot_general(wr_ref[...], h2, (((1,), (1,)), ((), ())), preferred_element_type=F32,
                             precision=lax.Precision.HIGHEST) + br_ref[...]
    sub = lax.broadcasted_iota(I32, logits.shape, 0)
    work = logits
    vals, idxs = [], []
    for _ in range(TOP_K):
        mx = jnp.max(work, axis=0, keepdims=True)
        ix = jnp.min(jnp.where(work == mx, sub, N_EXPERTS), axis=0, keepdims=True)
        vals.append(mx)
        idxs.append(ix)
        work = jnp.where(sub == ix, -jnp.inf, work)
    ex = [jnp.exp(v - vals[0]) for v in vals]
    den = ex[0] + ex[1] + ex[2] + ex[3]
    mh = jnp.where(work == -jnp.inf, 1.0, 0.0)
    r = lax.broadcasted_iota(I32, (tm, tm), 0)
    c = lax.broadcasted_iota(I32, (tm, tm), 1)
    earlier = jnp.where(r < c, 1.0, 0.0).astype(BF16)
    carry = carry_ref[...]
    before = _dot(mh.astype(BF16), earlier) + carry[:, 0:1]
    carry_ref[...] = carry + jnp.sum(mh, axis=1, keepdims=True)
    for kk in range(TOP_K):
        idx_ref[kk:kk + 1, :] = idxs[kk]
        gate_ref[kk:kk + 1, :] = ex[kk] / den
        rk = jnp.sum(jnp.where(sub == idxs[kk], before, 0.0), axis=0, keepdims=True)
        rank_ref[kk:kk + 1, :] = rk.astype(I32)
    cnt_ref[...] = carry_ref[...]


def _outproj(x, y4, att, gt, sh, sc, g_ffn, w, counts_in, *, att_transposed, tm):
    b, t, d = x.shape
    n = b * t
    tm = min(tm, n)
    tt = min(t, tm)
    nb = tm // tt
    tpb = t // tt

    def xmap(i):
        return (i // tpb, i % tpb, 0) if nb == 1 else (i, 0, 0)

    def bmap(i):
        return (i // tpb, 0, 0) if nb == 1 else (i, 0, 0)

    full = lambda shape: pl.BlockSpec(shape, lambda i: (0,) * len(shape))
    if att_transposed:
        att_spec = pl.BlockSpec((None, D_ATT, tm), lambda i: (i // tpb, 0, i % tpb))
    else:
        att_spec = pl.BlockSpec((tm, D_ATT), lambda i: (i, 0))
    choice = lambda dt: (jax.ShapeDtypeStruct((TOP_K, n), dt), pl.BlockSpec((TOP_K, tm), lambda i: (0, i)))
    outs = [(jax.ShapeDtypeStruct((b, t, d), F32), pl.BlockSpec((nb, tt, d), xmap)),
            (jax.ShapeDtypeStruct((n, d), F32), pl.BlockSpec((tm, d), lambda i: (i, 0))),
            choice(I32), choice(F32), choice(I32),
            (jax.ShapeDtypeStruct((N_EXPERTS, LANES), F32), full((N_EXPERTS, LANES)))]
    return pl.pallas_call(
        functools.partial(_outproj_kernel, att_transposed=att_transposed),
        out_shape=tuple(o[0] for o in outs),
        grid=(n // tm,),
        in_specs=[pl.BlockSpec((nb, tt, d), xmap),
                  pl.BlockSpec((N_OCT, tm // SSM_CHUNK, SSM_CHUNK * LANES), lambda i: (0, i, 0)),
                  att_spec,
                  pl.BlockSpec((nb, 1, d), bmap), pl.BlockSpec((nb, 1, d), bmap), pl.BlockSpec((nb, 1, d), bmap),
                  full((1, d)), full((D_SSM, D_SSM)), full((1, D_SSM)), full((d, d)),
                  full((N_EXPERTS, d)), full((N_EXPERTS, 1)), full((N_EXPERTS, LANES))],
        out_specs=tuple(o[1] for o in outs),
        scratch_shapes=[pltpu.VMEM((N_EXPERTS, LANES), F32), pltpu.VMEM((N_OCT, tm, LANES), F32)],
        compiler_params=_cparams(("arbitrary",)),
        name="outproj_t" if att_transposed else "outproj",
    )(x, y4, att, gt, sh, sc, g_ffn, w["wglu"], w["bglu"], w["wout"], w["wr"], w["br"], counts_in)


def _dispatch_kernel(zst_ref, nu_ref, dest_ref, *rest, tm, tiles):
    h_refs = rest[:len(tiles)]
    xs_ref, zero_ref, hbuf_ref, zsem, sems = rest[len(tiles):]
    i = pl.program_id(0)
    n_blocks = xs_ref.shape[0] // MOE_BLK

    @pl.when(i == 0)
    def _():
        zero_ref[...] = jnp.zeros_like(zero_ref)

        def zero_copy(start):
            start = pl.multiple_of(start, MOE_BLK)
            return pltpu.make_async_copy(zero_ref, xs_ref.at[pl.ds(start, MOE_BLK)], zsem)

        for e in range(N_EXPERTS):
            @pl.when(zst_ref[e] >= 0)
            def _():
                zero_copy(jnp.maximum(zst_ref[e], 0)).start()

        def tail_start(j, carry):
            zero_copy(j * MOE_BLK).start()
            return carry

        def tail_wait(j, carry):
            zero_copy(j * MOE_BLK).wait()
            return carry

        lax.fori_loop(nu_ref[0], n_blocks, tail_start, 0)
        for e in range(N_EXPERTS):
            @pl.when(zst_ref[e] >= 0)
            def _():
                zero_copy(jnp.maximum(zst_ref[e], 0)).wait()
        lax.fori_loop(nu_ref[0], n_blocks, tail_wait, 0)

    slot = i % 2

    def scatter_rows(h_ref):
        hbuf_ref[slot] = h_ref[...]

        def issue(t, carry):
            for kk in range(TOP_K):
                dst = dest_ref[0, 0, t * TOP_K + kk]
                pltpu.make_async_copy(hbuf_ref.at[slot, pl.ds(t, 1)], xs_ref.at[pl.ds(dst, 1)], sems.at[slot]).start()
            return carry

        lax.fori_loop(0, tm, issue, 0, unroll=8)

    first_tile = 0
    for h_ref, n_tiles in zip(h_refs, tiles):
        pl.when((i >= first_tile) & (i < first_tile + n_tiles))(functools.partial(scatter_rows, h_ref))
        first_tile += n_tiles

    def wait_tile(s):
        n_copied = tm * TOP_K
        pltpu.make_async_copy(xs_ref.at[pl.ds(0, n_copied)], xs_ref.at[pl.ds(0, n_copied)], sems.at[s]).wait()

    @pl.when(i > 0)
    def _():
        wait_tile(1 - slot)

    @pl.when(i == pl.num_programs(0) - 1)
    def _():
        wait_tile(slot)


def _dispatch(h2s, dest, zstart, n_used, n_rows, tm):
    d = h2s[0].shape[1]
    tiles = tuple(h.shape[0] // tm for h in h2s)
    nt = sum(tiles)
    dest3 = dest.reshape(nt, 1, tm * TOP_K)
    in_specs = [pl.BlockSpec((1, 1, tm * TOP_K), lambda i, *_: (i, 0, 0), memory_space=pltpu.SMEM)]
    first_tile = 0
    for n_tiles in tiles:
        in_specs.append(pl.BlockSpec(
            (tm, d), lambda i, *_, f=first_tile, m=n_tiles: (jnp.clip(i - f, 0, m - 1), 0)))
        first_tile += n_tiles
    return pl.pallas_call(
        functools.partial(_dispatch_kernel, tm=tm, tiles=tiles),
        out_shape=jax.ShapeDtypeStruct((n_rows, d), F32),
        grid_spec=pltpu.PrefetchScalarGridSpec(
            num_scalar_prefetch=2, grid=(nt,),
            in_specs=in_specs,
            out_specs=pl.BlockSpec(memory_space=pl.ANY),
            scratch_shapes=[pltpu.VMEM((MOE_BLK, d), F32), pltpu.VMEM((2, tm, d), F32),
                            pltpu.SemaphoreType.DMA, pltpu.SemaphoreType.DMA((2,))]),
        compiler_params=_cparams(("arbitrary",)),
        name="moe_dispatch",
    )(zstart, n_used, dest3, *h2s)


def _expert_kernel(be_ref, nxt_ref, slot_ref, first_ref, nu_ref, xs_ref, wup_hbm, bup_ref, wdn_hbm, bdn_ref,
                   ys_ref, wup_f32, wdn_f32, wup_bf, wdn_bf, sems):
    i = pl.program_id(0)
    e = be_ref[i]

    def fetch(expert, slot):
        return (pltpu.make_async_copy(wup_hbm.at[expert], wup_f32.at[slot], sems.at[0, slot]),
                pltpu.make_async_copy(wdn_hbm.at[expert], wdn_f32.at[slot], sems.at[1, slot]))

    @pl.when(i < nu_ref[0])
    def _():
        @pl.when(first_ref[i] == 1)
        def _():
            slot = slot_ref[i]

            @pl.when(i == 0)
            def _():
                for copy in fetch(e, slot):
                    copy.start()

            for copy in fetch(e, slot):
                copy.wait()

            @pl.when(nxt_ref[i] >= 0)
            def _():
                for copy in fetch(jnp.maximum(nxt_ref[i], 0), 1 - slot):
                    copy.start()

            wup_bf[...] = wup_f32[slot].astype(BF16)
            wdn_bf[...] = wdn_f32[slot].astype(BF16)

        up = _dot(xs_ref[...].astype(BF16), wup_bf[...]) + bup_ref[...]
        x_glu = jnp.minimum(up[:, :D_FF], SWIGLU_LIMIT)
        x_lin = jnp.clip(up[:, D_FF:], -SWIGLU_LIMIT, SWIGLU_LIMIT)
        act = (x_lin + 1.0) * (x_glu * jax.nn.sigmoid(SWIGLU_ALPHA * x_glu))
        ys_ref[...] = (_dot(act.astype(BF16), wdn_bf[...]) + bdn_ref[...]).astype(ys_ref.dtype)

    @pl.when(i >= nu_ref[0])
    def _():
        ys_ref[...] = jnp.zeros_like(ys_ref)


def _experts(xs, block_e, n_used, has_rows, w_up, b_up, w_down, b_down):
    n_rows, d = xs.shape
    nblk = n_rows // MOE_BLK
    blocks = jnp.arange(nblk, dtype=I32)
    used = blocks < n_used[0]
    first = (used & ((blocks == 0) | (block_e != jnp.roll(block_e, 1)))).astype(I32)
    slot = ((jnp.cumsum(first) - 1) % 2).astype(I32)
    experts = jnp.arange(N_EXPERTS, dtype=I32)
    later = jnp.where(has_rows[None, :] & (experts[None, :] > experts[:, None]), experts[None, :], N_EXPERTS)
    nxt_e = jnp.min(later, axis=1)
    nxt_e = jnp.where(nxt_e < N_EXPERTS, nxt_e, -1).astype(I32)
    nxt = jnp.sum(jnp.where(block_e[:, None] == experts[None, :], nxt_e[None, :], 0), axis=1).astype(I32)
    rowmap = lambda i, be, nx, sl, fi, nu: (jnp.minimum(i, nu[0] - 1), 0)
    emap = lambda i, be, nx, sl, fi, nu: (be[i], 0, 0)
    return pl.pallas_call(
        _expert_kernel,
        out_shape=jax.ShapeDtypeStruct((n_rows, d), BF16),
        grid_spec=pltpu.PrefetchScalarGridSpec(
            num_scalar_prefetch=5, grid=(nblk,),
            in_specs=[pl.BlockSpec((MOE_BLK, d), rowmap),
                      pl.BlockSpec(memory_space=pl.ANY), pl.BlockSpec((None, 1, 2 * D_FF), emap),
                      pl.BlockSpec(memory_space=pl.ANY), pl.BlockSpec((None, 1, d), emap)],
            out_specs=pl.BlockSpec((MOE_BLK, d), lambda i, *_: (i, 0)),
            scratch_shapes=[pltpu.VMEM((2, d, 2 * D_FF), F32), pltpu.VMEM((2, D_FF, d), F32),
                            pltpu.VMEM((d, 2 * D_FF), BF16), pltpu.VMEM((D_FF, d), BF16),
                            pltpu.SemaphoreType.DMA((2, 2))]),
        compiler_params=_cparams(("arbitrary",)),
        name="moe_experts",
    )(block_e, nxt, slot, first, n_used, xs, w_up, b_up, w_down, b_down)


def _combine_kernel(meta_ref, metan_ref, x1_ref, col_ref, gate_ref, gt_ref, gfin_ref, ys_ref, y_ref, ybuf_ref, sems,
                    *, tm):
    i = pl.program_id(0)
    nt = pl.num_programs(0)
    nb, tt, d = x1_ref.shape
    rows = ybuf_ref.shape[1]

    ra = RUN_ALIGN

    def start_copies(m_ref, slot):
        def per_expert(e, carry):
            a = m_ref[0, 0, e]
            n_chunks = m_ref[0, 0, N_EXPERTS + e]
            bo = m_ref[0, 0, 2 * N_EXPERTS + e]

            def per_chunk(c, carry2):
                src = ys_ref.at[pl.ds(pl.multiple_of(a + ra * c, ra), ra)]
                dst = ybuf_ref.at[slot, pl.ds(pl.multiple_of(bo + ra * c, ra), ra)]
                pltpu.make_async_copy(src, dst, sems.at[slot]).start()
                return carry2

            lax.fori_loop(0, n_chunks, per_chunk, 0)
            return carry

        lax.fori_loop(0, N_EXPERTS, per_expert, 0)

    def wait_copies(m_ref, slot):
        total = m_ref[0, 0, 3 * N_EXPERTS]

        def wait_rows(n_rows):
            pltpu.make_async_copy(ys_ref.at[pl.ds(0, n_rows)], ybuf_ref.at[slot, pl.ds(0, n_rows)],
                                  sems.at[slot]).wait()

        lax.fori_loop(0, total // WAIT_GROUP, lambda c, carry: (wait_rows(WAIT_GROUP * ra), carry)[1], 0)
        lax.fori_loop(0, total % WAIT_GROUP, lambda c, carry: (wait_rows(ra), carry)[1], 0)

    @pl.when(i == 0)
    def _():
        ybuf_ref[...] = jnp.zeros_like(ybuf_ref)
        start_copies(meta_ref, 0)

    @pl.when(i + 1 < nt)
    def _():
        start_copies(metan_ref, (i + 1) % 2)

    slot = i % 2
    wait_copies(meta_ref, slot)
    col = col_ref[...]
    gate = gate_ref[...]
    lane = lax.broadcasted_iota(I32, (tm, rows), 1)
    pick = jnp.where(lane == col[:, 0:1], gate[:, 0:1], 0.0)
    for kk in range(1, TOP_K):
        pick = pick + jnp.where(lane == col[:, kk:kk + 1], gate[:, kk:kk + 1], 0.0)
    moe = _dot(pick.astype(BF16), ybuf_ref[slot])
    x2 = x1_ref[...] + gt_ref[...] * moe.reshape(nb, tt, d)
    ms = jnp.mean(x2 * x2, axis=-1, keepdims=True)
    y_ref[...] = x2 * lax.rsqrt(ms + NORM_EPS) * gfin_ref[...]


def _combine_plan(idx, dest, off, before0, tm):
    n = idx.shape[1]
    nt = n // tm
    is_e = idx[None, :, :] == jnp.arange(N_EXPERTS, dtype=I32)[:, None, None]
    cnt = jnp.any(is_e, axis=1).astype(I32).reshape(N_EXPERTS, nt, tm).sum(axis=2).T
    before = before0[None, :] + jnp.cumsum(cnt, axis=0) - cnt
    start = off[None, :] + before
    ra = RUN_ALIGN
    lo = start // ra * ra
    hi = jnp.where(cnt > 0, (start + cnt + ra - 1) // ra * ra, lo)
    span = hi - lo
    boff = jnp.cumsum(span, axis=1) - span
    delta = jnp.repeat((boff - lo).T, tm, axis=1)
    col = dest + jnp.sum(jnp.where(is_e, delta[:, None, :], 0), axis=0)
    total = jnp.sum(span // ra, axis=1, keepdims=True)
    meta = jnp.concatenate([lo, span // ra, boff, jnp.broadcast_to(total, lo.shape)], axis=1).astype(I32)
    return meta.reshape(nt, 1, 4 * N_EXPERTS), col.T.astype(I32)


def _combine(x1, gate, idx, dest, off, before0, ys, gt, g_final, tm):
    b, t, d = x1.shape
    n = b * t
    tm = min(tm, n)
    tt = min(t, tm)
    nb = tm // tt
    tpb = t // tt
    nt = n // tm
    rows = -(-(tm * TOP_K + 2 * (RUN_ALIGN - 1) * N_EXPERTS) // 256) * 256
    meta, col = _combine_plan(idx, dest, off, before0, tm)

    def xmap(i):
        return (i // tpb, i % tpb, 0) if nb == 1 else (i, 0, 0)

    def bmap(i):
        return (i // tpb, 0, 0) if nb == 1 else (i, 0, 0)

    smem = lambda imap: pl.BlockSpec((1, 1, 4 * N_EXPERTS), imap, memory_space=pltpu.SMEM)
    return pl.pallas_call(
        functools.partial(_combine_kernel, tm=tm),
        out_shape=jax.ShapeDtypeStruct((b, t, d), F32),
        grid=(nt,),
        in_specs=[smem(lambda i: (i, 0, 0)), smem(lambda i: (jnp.minimum(i + 1, nt - 1), 0, 0)),
                  pl.BlockSpec((nb, tt, d), xmap),
                  pl.BlockSpec((tm, TOP_K), lambda i: (i, 0)),
                  pl.BlockSpec((tm, TOP_K), lambda i: (i, 0)),
                  pl.BlockSpec((nb, 1, d), bmap),
                  pl.BlockSpec((1, d), lambda i: (0, 0)),
                  pl.BlockSpec(memory_space=pl.ANY)],
        out_specs=pl.BlockSpec((nb, tt, d), xmap),
        scratch_shapes=[pltpu.VMEM((2, rows, d), BF16), pltpu.SemaphoreType.DMA((2,))],
        compiler_params=_cparams(("arbitrary",)),
        name="moe_combine",
    )(meta, meta, x1, col, gate, gt, g_final, ys)


def _moe(streams, counts, g_final, w_up, b_up, w_down, b_down):
    n_total = sum(s["h2"].shape[0] for s in streams)
    cnt = counts[:, 0].astype(I32)
    padded = (cnt + MOE_BLK - 1) // MOE_BLK * MOE_BLK
    pad_end = jnp.cumsum(padded)
    off = (pad_end - padded).astype(I32)
    n_blocks = -(-n_total * TOP_K // MOE_BLK) + N_EXPERTS
    n_rows = n_blocks * MOE_BLK
    starts = jnp.arange(n_blocks, dtype=I32) * MOE_BLK
    block_e = jnp.minimum(jnp.sum((pad_end[None, :] <= starts[:, None]).astype(I32), axis=1), N_EXPERTS - 1)
    n_used = (pad_end[-1:] // MOE_BLK).astype(I32)
    zstart = jnp.where(padded > 0, pad_end - MOE_BLK, -1).astype(I32)
    experts = jnp.arange(N_EXPERTS, dtype=I32)
    dests = []
    for s in streams:
        seg = jnp.sum(jnp.where(s["idx"][None] == experts[:, None, None], off[:, None, None], 0), axis=0)
        dests.append((seg + s["rank"]).astype(I32))
    tm = min([512] + [s["h2"].shape[0] for s in streams])
    dest_rows = jnp.concatenate([dest.T for dest in dests], axis=0)
    xs = _dispatch([s["h2"] for s in streams], dest_rows, zstart, n_used, n_rows, tm)
    ys = _experts(xs, block_e, n_used, padded > 0, w_up, b_up, w_down, b_down)
    return [_combine(s["x1"], s["gate"].T, s["idx"], dest, off, s["before"], ys, s["gt"], g_final, 256)
            for s, dest in zip(streams, dests)]


def _stream(x, mod, w, ssm_ops, s0, cache, params, counts_in):
    b, t, d = x.shape
    n = b * t
    sh_mix, sc_mix, gt_mix, sh_ffn, sc_ffn, gt_ffn = mod
    prompt = cache is None
    tm = 1024 if prompt else 512
    u4, q, kt, vt, lft, vtb, kaug = _inproj(x, sh_mix, sc_mix, params["g_mix"], w, prompt=prompt, tm=tm)
    n_chunks = t // SSM_CHUNK
    y4, s_fin = _ssm(u4, ssm_ops, s0, b, n_chunks)
    if prompt:
        att = _attn_p(q, kaug, vtb, b, t)
    else:
        cache_k, cache_v, cache_lf = cache
        p = cache_k.shape[1]
        ck_t = jnp.transpose(cache_k, (0, 2, 3, 1)).reshape(b, D_ATT, p)
        cv_t = jnp.transpose(cache_v, (0, 2, 3, 1)).reshape(b, D_ATT, p)
        cl_t = jnp.transpose(cache_lf, (0, 2, 1))
        ln_t = jnp.pad(lft, ((0, 0), (0, 0), (0, LANES - t)))
        fc, fn = _fprep_s(cl_t, ln_t)
        att = _attn_s(q.reshape(b, t, D_ATT), ck_t, cv_t, kt, vt, fc, fn).reshape(n, D_ATT)
    x1, h2, idx, gate, rank, counts = _outproj(x, y4, att, gt_mix, sh_ffn, sc_ffn, params["g_ffn"], w, counts_in,
                                               att_transposed=prompt, tm=tm)
    routed = {"x1": x1, "h2": h2, "idx": idx, "gate": gate, "rank": rank, "gt": gt_ffn,
              "before": counts_in[:, 0].astype(I32)}
    heads = lambda a: jnp.transpose(a.reshape(b, N_HEADS, HEAD_DIM, t), (0, 3, 1, 2))[None]
    return routed, counts, heads(kt), heads(vt), jnp.transpose(lft, (0, 2, 1))[None], s_fin


def _state_in(re, im):
    b = re.shape[0]
    s = jnp.concatenate([re.reshape(b, N_OCT, OCT_STATE // 2), im.reshape(b, N_OCT, OCT_STATE // 2)], axis=-1)
    return jnp.transpose(s, (1, 0, 2))


def _state_out(s):
    b = s.shape[1]
    s = jnp.transpose(s, (1, 0, 2))
    half = OCT_STATE // 2
    re = s[:, :, :half].reshape(1, b, N_SSM_GROUPS, SSM_STATE)
    im = s[:, :, half:].reshape(1, b, N_SSM_GROUPS, SSM_STATE)
    return re, im


def kernel(x_prompt, x_sample, c_prompt, c_sample, cache_k, cache_v, cache_logf, state_ssm_re, state_ssm_im, w_ada, b_ada, g_mix, w_in, b_forget, ssm_log_dt, ssm_a_re, ssm_a_im, ssm_b_re, ssm_b_im, ssm_c_re, ssm_c_im, ssm_d, w_glu, b_glu, w_out, g_ffn, w_router, b_router, w_up, b_up, w_down, b_down, g_final):
    assert w_ada.shape[0] == 1, "single-layer trunk"
    d = D_MODEL
    bp, tp, _ = x_prompt.shape
    bs, ts, _ = x_sample.shape
    n_c = bp + bs
    c_all = jnp.pad(jnp.concatenate([c_prompt, c_sample], axis=0), ((0, -n_c % 8), (0, 0)))
    mod = _ada(c_all, w_ada[0], b_ada[0].reshape(1, 6 * d))
    mod_p = [mod[:bp, j * d:(j + 1) * d].reshape(bp, 1, d) for j in range(6)]
    mod_s = [mod[bp:n_c, j * d:(j + 1) * d].reshape(bs, 1, d) for j in range(6)]
    wi = w_in[0]
    o1, o2, o3, o4 = D_SSM, D_SSM + D_ATT, D_SSM + 2 * D_ATT, D_SSM + 3 * D_ATT
    w_q = wi[:, o1:o2].astype(BF16)
    w_k = wi[:, o2:o3].astype(BF16)
    unused = jnp.zeros((8, LANES), BF16)
    base = {"wu": wi[:, :o1].astype(BF16), "wkt": w_k.T, "wvt": wi[:, o3:o4].astype(BF16).T,
            "wft": wi[:, o4:].astype(BF16).T, "bfc": b_forget[0].reshape(N_HEADS, 1), "pm": _bias_placement(),
            "wglu": w_glu[0].astype(BF16), "bglu": b_glu[0].reshape(1, D_SSM), "wout": w_out[0].astype(BF16),
            "wr": w_router[0].T, "br": b_router[0].reshape(N_EXPERTS, 1)}
    w_p = dict(base, wq=w_q.T, wk=w_k)
    w_s = dict(base, wq=w_q, wk=unused)
    params = {"g_mix": g_mix[0].reshape(1, d), "g_ffn": g_ffn[0].reshape(1, d), "g_final": g_final.reshape(1, d),
              "w_up": w_up[0], "b_up": b_up[0].reshape(N_EXPERTS, 1, 2 * D_FF),
              "w_down": w_down[0], "b_down": b_down[0].reshape(N_EXPERTS, 1, d)}
    ssm_ops = _ssm_operators(ssm_log_dt[0], ssm_a_re[0], ssm_a_im[0], ssm_b_re[0], ssm_b_im[0],
                             ssm_c_re[0], ssm_c_im[0], ssm_d[0])
    zero_state = jnp.zeros((N_OCT, bp, OCT_STATE), F32)
    counts0 = jnp.zeros((N_EXPERTS, LANES), F32)
    r_p, counts_p, k_p, v_p, f_p, s_p = _stream(x_prompt, mod_p, w_p, ssm_ops, zero_state, None, params, counts0)
    s0 = _state_in(state_ssm_re[0], state_ssm_im[0])
    r_s, counts, k_s, v_s, f_s, s_s = _stream(x_sample, mod_s, w_s, ssm_ops, s0,
                                              (cache_k[0], cache_v[0], cache_logf[0]), params, counts_p)
    y_p, y_s = _moe([r_p, r_s], counts, params["g_final"],
                    params["w_up"], params["b_up"], params["w_down"], params["b_down"])
    sre_p, sim_p = _state_out(s_p)
    sre_s, sim_s = _state_out(s_s)
    return (y_p, y_s, k_p, v_p, f_p, sre_p, sim_p, k_s, v_s, f_s, sre_s, sim_s)
```

```python
import functools
import math

import jax
import jax.numpy as jnp
from jax import lax
from jax.experimental import pallas as pl
from jax.experimental.pallas import tpu as pltpu

F32 = jnp.float32
BF16 = jnp.bfloat16
I32 = jnp.int32

D_MODEL = 1024
D_SSM = 512
SSM_GROUP = 16
N_SSM_GROUPS = 32
SSM_STATE = 64
D_ATT = 512
HEAD_DIM = 64
N_HEADS = 8
N_EXPERTS = 32
TOP_K = 4
D_FF = 1024
SWIGLU_LIMIT = 7.0
SWIGLU_ALPHA = 1.702
NORM_EPS = 1e-5

LANES = 128
N_OCT = D_SSM // LANES
OCT_STATE = 2 * (N_SSM_GROUPS // N_OCT) * SSM_STATE
SSM_CHUNK = 8
SSM_ROWS = 512
SSM_INTERLEAVE = 4
ATT_BLK = 256
ATT_PAIRS = 4
LOG2E = math.log2(math.e)
SAMPLE_KV_BLK = 2048
MOE_BLK = 256
RUN_ALIGN = 16
WAIT_GROUP = 16
VMEM_LIMIT = 52 * 1024 * 1024


def _cparams(sem, vmem=VMEM_LIMIT):
    return pltpu.CompilerParams(dimension_semantics=sem, vmem_limit_bytes=vmem)


def _dot(a, b):
    return jnp.dot(a, b, preferred_element_type=F32)


def _dot_nt(a, b):
    return lax.dot_general(a, b, (((1,), (1,)), ((), ())), preferred_element_type=F32)


def _dot_hi(a, b):
    return jnp.dot(a, b, preferred_element_type=F32, precision=lax.Precision.HIGHEST)


def _ada_kernel(c_ref, w_ref, b_ref, o_ref):
    c = c_ref[...]
    s = c * jax.nn.sigmoid(c)
    o_ref[...] = _dot_hi(s, w_ref[...]) + b_ref[...]


def _ada(c_all, w_ada, b_ada):
    m, d = c_all.shape
    n = w_ada.shape[1]
    return pl.pallas_call(
        _ada_kernel,
        out_shape=jax.ShapeDtypeStruct((m, n), F32),
        grid=(n // d,),
        in_specs=[pl.BlockSpec((m, d), lambda j: (0, 0)),
                  pl.BlockSpec((d, d), lambda j: (0, j)),
                  pl.BlockSpec((1, d), lambda j: (0, j))],
        out_specs=pl.BlockSpec((m, d), lambda j: (0, j)),
        compiler_params=_cparams(("arbitrary",)),
        name="ada",
    )(c_all, w_ada, b_ada)


def _norm_mod(x, g, scale, shift):
    ms = jnp.mean(x * x, axis=-1, keepdims=True)
    y = x * lax.rsqrt(ms + NORM_EPS) * g
    return y * (1.0 + scale) + shift


def _log_sigmoid(z):
    return jnp.minimum(z, 0.0) - jnp.log1p(jnp.exp(-jnp.abs(z)))


def _inproj_kernel(x_ref, sh_ref, sc_ref, g_ref, wu_ref, wq_ref, wkt_ref, wvt_ref, wft_ref, bfc_ref,
                   wk_ref, pm_ref,
                   u4_ref, q_ref, kt_ref, vt_ref, lft_ref, vtb_ref, ka_ref, us_ref, carry_ref, *, prompt, tpb):
    nb, tt, d = x_ref.shape
    tm = nb * tt
    h = _norm_mod(x_ref[...], g_ref[...], sc_ref[...], sh_ref[...]).reshape(tm, d).astype(BF16)
    u = _dot(h, wu_ref[...])
    for o in range(N_OCT):
        us_ref[o] = u[:, o * LANES:(o + 1) * LANES]
    for s in range(SSM_CHUNK):
        for o in range(N_OCT):
            piece = us_ref[o, pl.ds(s, tm // SSM_CHUNK, stride=SSM_CHUNK), :]
            u4_ref[o, :, s * LANES:(s + 1) * LANES] = piece.astype(BF16)
    kt = _dot_nt(wkt_ref[...], h)
    vt = _dot_nt(wvt_ref[...], h)
    lft = _log_sigmoid(_dot_nt(wft_ref[...], h) + bfc_ref[...])
    scale = HEAD_DIM ** -0.5
    if prompt:
        kt_ref[...] = kt
        vt_ref[...] = vt
        lft_ref[...] = lft
        qt = (_dot_nt(wq_ref[...], h) * (scale * LOG2E)).astype(BF16)
        vtb = vt.astype(BF16)
        for j in range(tm // ATT_BLK):
            q_ref[j] = qt[:, j * ATT_BLK:(j + 1) * ATT_BLK]
            vtb_ref[j] = vtb[:, j * ATT_BLK:(j + 1) * ATT_BLK]
        k_tok = _dot(h, wk_ref[...])

        @pl.when(pl.program_id(0) % tpb == 0)
        def _():
            carry_ref[...] = jnp.zeros_like(carry_ref)

        cum = _lane_cumsum(lft) + carry_ref[:, 0:1]
        carry_ref[...] = jnp.broadcast_to(cum[:, tm - 1:tm], carry_ref.shape)
        nf = cum * (-LOG2E)
        hi = nf.astype(BF16).astype(F32)
        r1 = nf - hi
        mid = r1.astype(BF16).astype(F32)
        lo = (r1 - mid).astype(BF16).astype(F32)
        place = lambda piece, pm: lax.dot_general(piece, pm, (((0,), (0,)), ((), ())), preferred_element_type=F32)
        slab = place(hi, pm_ref[0]) + place(mid, pm_ref[1]) + place(lo, pm_ref[2])
        for j in range(tm // ATT_BLK):
            rows = slice(j * ATT_BLK, (j + 1) * ATT_BLK)
            for p in range(N_HEADS // 2):
                ka_ref[j, :, 2 * p * LANES:(2 * p + 1) * LANES] = k_tok[rows, p * LANES:(p + 1) * LANES].astype(BF16)
                ka_ref[j, :, (2 * p + 1) * LANES:(2 * p + 2) * LANES] = (
                    slab[rows, p * LANES:(p + 1) * LANES].astype(BF16))
    else:
        for bl in range(nb):
            kt_ref[bl] = kt[:, bl * tt:(bl + 1) * tt]
            vt_ref[bl] = vt[:, bl * tt:(bl + 1) * tt]
            lft_ref[bl] = lft[:, bl * tt:(bl + 1) * tt]
        q_ref[...] = (_dot(h, wq_ref[...]) * scale).astype(BF16)
        vtb_ref[...] = jnp.zeros_like(vtb_ref)
        ka_ref[...] = jnp.zeros_like(ka_ref)


def _inproj(x, shift, scale, g_mix, w, *, prompt, tm):
    b, t, d = x.shape
    n = b * t
    tm = min(tm, n)
    tt = min(t, tm)
    nb = tm // tt
    tpb = t // tt
    assert (nb == 1) == prompt
    grid = (n // tm,)

    def xmap(i):
        return (i // tpb, i % tpb, 0) if nb == 1 else (i, 0, 0)

    def bmap(i):
        return (i // tpb, 0, 0) if nb == 1 else (i, 0, 0)

    def tmap(i):
        return (i // tpb, 0, i % tpb) if nb == 1 else (i, 0, 0)

    full = lambda a: pl.BlockSpec(a.shape, lambda i: (0,) * a.ndim)
    names = ("wu", "wq", "wkt", "wvt", "wft", "bfc", "wk", "pm")
    in_specs = [pl.BlockSpec((nb, tt, d), xmap),
                pl.BlockSpec((nb, 1, d), bmap), pl.BlockSpec((nb, 1, d), bmap),
                pl.BlockSpec((1, d), lambda i: (0, 0))] + [full(w[k]) for k in names]
    tblk = lambda rows: pl.BlockSpec((None if nb == 1 else nb, rows, tt), tmap)
    dummy = (jax.ShapeDtypeStruct((8, LANES), F32), pl.BlockSpec((8, LANES), lambda i: (0, 0)))
    if prompt:
        nblk = n // ATT_BLK
        blocked = (jax.ShapeDtypeStruct((nblk, D_ATT, ATT_BLK), BF16),
                   pl.BlockSpec((tm // ATT_BLK, D_ATT, ATT_BLK), lambda i: (i, 0, 0)))
        q_out, vtb_out = blocked, blocked
        ka_out = (jax.ShapeDtypeStruct((nblk, ATT_BLK, 2 * D_ATT), BF16),
                  pl.BlockSpec((tm // ATT_BLK, ATT_BLK, 2 * D_ATT), lambda i: (i, 0, 0)))
    else:
        q_out = (jax.ShapeDtypeStruct((n, D_ATT), BF16), pl.BlockSpec((tm, D_ATT), lambda i: (i, 0)))
        vtb_out = ka_out = dummy
    kdim = SSM_CHUNK * LANES
    outs = [(jax.ShapeDtypeStruct((N_OCT, n // SSM_CHUNK, kdim), BF16),
             pl.BlockSpec((N_OCT, tm // SSM_CHUNK, kdim), lambda i: (0, i, 0))),
            q_out,
            (jax.ShapeDtypeStruct((b, D_ATT, t), F32), tblk(D_ATT)),
            (jax.ShapeDtypeStruct((b, D_ATT, t), F32), tblk(D_ATT)),
            (jax.ShapeDtypeStruct((b, N_HEADS, t), F32), tblk(N_HEADS)),
            vtb_out, ka_out]
    return pl.pallas_call(
        functools.partial(_inproj_kernel, prompt=prompt, tpb=tpb),
        out_shape=tuple(o[0] for o in outs), grid=grid, in_specs=in_specs, out_specs=tuple(o[1] for o in outs),
        scratch_shapes=[pltpu.VMEM((N_OCT, tm, LANES), F32), pltpu.VMEM((8, LANES), F32)],
        compiler_params=_cparams(("arbitrary",)),
        name="inproj_p" if prompt else "inproj_s",
    )(x, shift, scale, g_mix, *[w[k] for k in names])


def _ssm_operators(log_dt, a_re, a_im, b_re, b_im, c_re, c_im, d_skip):
    L, G, P, C = SSM_CHUNK, N_SSM_GROUPS, SSM_STATE, SSM_GROUP
    q = G // N_OCT
    lam = lax.complex(a_re, a_im)
    dt = jnp.exp(log_dt)[:, None]
    lam_dt = lam * dt
    lam_bar = jnp.exp(lam_dt)
    b_bar = ((lam_bar - 1.0) / lam)[:, :, None] * lax.complex(b_re, b_im)
    c_mat = lax.complex(c_re, c_im)
    pw = jnp.exp(lam_dt[None] * jnp.arange(L + 1, dtype=F32)[:, None, None])
    kern = jnp.real(jnp.einsum('gcp,dgp,gpk->dgck', c_mat, pw[:L], b_bar))
    ksm = jnp.transpose(kern, (1, 0, 3, 2)).reshape(N_OCT, q, L, C, C)
    ksm = jnp.transpose(ksm, (0, 2, 1, 3, 4)).reshape(N_OCT, L, q * C, C)
    pwr = pw[L - 1 - jnp.arange(L)]
    wb = jnp.einsum('jgp,gpk->jgkp', pwr, b_bar)
    wsm = jnp.stack([jnp.real(wb), jnp.imag(wb)], axis=0).reshape(2, L, N_OCT, q * C, P)
    wsm = jnp.transpose(wsm, (2, 1, 0, 3, 4))
    cl = c_mat[None] * pw[1:L + 1][:, :, None, :]
    vsm = jnp.stack([jnp.real(cl), -jnp.imag(cl)], axis=0).reshape(2, L, N_OCT, q, C, P)
    vsm = jnp.transpose(vsm, (2, 1, 0, 3, 5, 4)).reshape(N_OCT, L, OCT_STATE, C)
    tmat, wend, win = _ssm_prep(ksm, wsm, vsm, d_skip.reshape(N_OCT, 1, LANES))
    dec = pw[L].reshape(N_OCT, q * P)
    decay = jnp.concatenate([jnp.real(dec), jnp.imag(dec)], axis=-1).reshape(N_OCT, 1, OCT_STATE)
    return tmat, wend, win, decay.astype(F32)


def _ssm_prep_kernel(ksm_ref, wsm_ref, vsm_ref, d_ref, tmat_ref, wend_ref, win_ref):
    L, C, P = SSM_CHUNK, SSM_GROUP, SSM_STATE
    half = OCT_STATE // 2

    def spread(period, width):
        r = lax.broadcasted_iota(I32, (period, width), 0)
        c = lax.broadcasted_iota(I32, (period, width), 1)
        return jnp.where(c % period == r, 1.0, 0.0)

    def same_group(shape, row_div, row_mod, lane_div):
        r = lax.broadcasted_iota(I32, shape, 0)
        c = lax.broadcasted_iota(I32, shape, 1)
        return (r % row_mod) // row_div == c // lane_div

    e_c = spread(C, LANES)
    e_p = spread(P, half)
    grp = same_group((LANES, LANES), C, LANES, C)
    rr = lax.broadcasted_iota(I32, (LANES, LANES), 0)
    cc = lax.broadcasted_iota(I32, (LANES, LANES), 1)
    lag = []
    for d in range(L):
        blk = jnp.where(grp, _dot_hi(ksm_ref[d], e_c), 0.0)
        if d == 0:
            blk = blk + jnp.where(rr == cc, jnp.broadcast_to(d_ref[...], (LANES, LANES)), 0.0)
        lag.append(blk.astype(BF16))
    zero = jnp.zeros((LANES, LANES), BF16)
    for j in range(L):
        for t in range(L):
            tmat_ref[j * LANES:(j + 1) * LANES, t * LANES:(t + 1) * LANES] = lag[t - j] if t >= j else zero
    grp_e = same_group((LANES, half), C, LANES, P)
    for j in range(L):
        for part in range(2):
            blk = jnp.where(grp_e, _dot_hi(wsm_ref[j, part], e_p), 0.0)
            wend_ref[j * LANES:(j + 1) * LANES, part * half:(part + 1) * half] = blk.astype(BF16)
    grp_i = same_group((OCT_STATE, LANES), P, half, C)
    for t in range(L):
        blk = jnp.where(grp_i, _dot_hi(vsm_ref[t], e_c), 0.0)
        win_ref[:, t * LANES:(t + 1) * LANES] = blk.astype(BF16)


def _ssm_prep(ksm, wsm, vsm, dvec):
    kdim = SSM_CHUNK * LANES
    blk = lambda a: pl.BlockSpec((None,) + a.shape[1:], lambda o: (o,) + (0,) * (a.ndim - 1))
    out = lambda r, c: (jax.ShapeDtypeStruct((N_OCT, r, c), BF16), pl.BlockSpec((None, r, c), lambda o: (o, 0, 0)))
    outs = [out(kdim, kdim), out(kdim, OCT_STATE), out(OCT_STATE, kdim)]
    return pl.pallas_call(
        _ssm_prep_kernel,
        out_shape=tuple(o[0] for o in outs),
        grid=(N_OCT,),
        in_specs=[blk(ksm), blk(wsm), blk(vsm), blk(dvec)],
        out_specs=tuple(o[1] for o in outs),
        compiler_params=_cparams(("arbitrary",)),
        name="ssm_prep",
    )(ksm, wsm, vsm, dvec)


def _ssm_kernel(u_ref, tmat_ref, wend_ref, win_ref, dec_ref, s0_ref, y_ref, sfin_ref, e_ref, sp_ref, st_ref,
                *, nb, cpt):
    i = pl.program_id(1)
    half = OCT_STATE // 2
    rows = nb * cpt

    @pl.when(i == 0)
    def _():
        st_ref[...] = s0_ref[...]

    u = u_ref[...].reshape(rows, u_ref.shape[-1])
    e_ref[...] = _dot(u, wend_ref[...])
    dec = dec_ref[...]
    ar, ai = dec[:, :half], dec[:, half:]

    def advance(row, st):
        sp_ref[pl.ds(row, 1), :] = st
        e = e_ref[pl.ds(row, 1), :]
        re, im = st[:, :half], st[:, half:]
        nre = ar * re - ai * im + e[:, :half]
        nim = ar * im + ai * re + e[:, half:]
        return jnp.concatenate([nre, nim], axis=-1)

    if nb <= SSM_INTERLEAVE:
        def per_chunk(r, sts):
            return tuple(advance(bb * cpt + r, sts[bb]) for bb in range(nb))

        sts = lax.fori_loop(0, cpt, per_chunk, tuple(st_ref[bb:bb + 1, :] for bb in range(nb)))
        for bb in range(nb):
            st_ref[bb:bb + 1, :] = sts[bb]
    else:
        def per_batch(bb, carry):
            st = lax.fori_loop(0, cpt, lambda r, st: advance(bb * cpt + r, st), st_ref[pl.ds(bb, 1), :])
            st_ref[pl.ds(bb, 1), :] = st
            return carry

        lax.fori_loop(0, nb, per_batch, 0)
    y = _dot(u, tmat_ref[...]) + _dot(sp_ref[...].astype(BF16), win_ref[...])
    y_ref[...] = jax.nn.gelu(y).astype(BF16).reshape(y_ref.shape)
    sfin_ref[...] = st_ref[...]


def _ssm(u4, ops, s0, batch, n_chunks):
    tmat, wend, win, decay = ops
    kdim = SSM_CHUNK * LANES
    cpt = SSM_ROWS if n_chunks % SSM_ROWS == 0 else n_chunks
    tiles = n_chunks // cpt
    if tiles > 1:
        u_in = u4.reshape(N_OCT, batch, n_chunks, kdim)
        u_spec = pl.BlockSpec((None, batch, cpt, kdim), lambda o, i: (o, 0, i, 0))
    else:
        u_in = u4
        u_spec = pl.BlockSpec((None, batch * n_chunks, kdim), lambda o, i: (o, 0, 0))
    r = batch * cpt
    wspec = lambda shape: pl.BlockSpec((None,) + shape, lambda o, i: (o, 0, 0))
    y, s_fin = pl.pallas_call(
        functools.partial(_ssm_kernel, nb=batch, cpt=cpt),
        out_shape=(jax.ShapeDtypeStruct(u_in.shape, BF16),
                   jax.ShapeDtypeStruct((N_OCT, batch, OCT_STATE), F32)),
        grid=(N_OCT, tiles),
        in_specs=[u_spec,
                  wspec((kdim, kdim)), wspec((kdim, OCT_STATE)), wspec((OCT_STATE, kdim)), wspec((1, OCT_STATE)),
                  wspec((batch, OCT_STATE))],
        out_specs=(u_spec, wspec((batch, OCT_STATE))),
        scratch_shapes=[pltpu.VMEM((r, OCT_STATE), F32), pltpu.VMEM((r, OCT_STATE), F32),
                        pltpu.VMEM((batch, OCT_STATE), F32)],
        compiler_params=_cparams(("arbitrary", "arbitrary")),
        name="ssm",
    )(u_in, tmat, wend, win, decay, s0)
    return y.reshape(u4.shape), s_fin


def _bias_placement():
    import numpy as np
    pm = np.zeros((3, N_HEADS, N_HEADS // 2 * LANES), np.float32)
    for piece in range(3):
        for h in range(N_HEADS):
            pm[piece, h, LANES * (h // 2) + 3 * (h % 2) + piece] = 1.0
    return jnp.asarray(pm)


def _attn_p_kernel(qt_ref, ka_ref, vt_ref, o_ref, qbd_ref, acc_ref, sa_ref, sb_ref):
    i = pl.program_id(2)
    bq = ATT_BLK
    hd = HEAD_DIM
    r = lax.broadcasted_iota(I32, (LANES, 2 * bq), 0)
    c = lax.broadcasted_iota(I32, (LANES, 2 * bq), 1)
    ones = jnp.where(((r < 3) & (c < bq)) | ((r >= 3) & (r < 6) & (c >= bq)), 1.0, 0.0).astype(BF16)
    zero = jnp.zeros((hd, bq), BF16)
    for pp in range(ATT_PAIRS):
        qb = qt_ref[pp * LANES:(pp + 1) * LANES, :]
        qbd_ref[pp, 0:hd, 0:bq] = qb[0:hd]
        qbd_ref[pp, 0:hd, bq:2 * bq] = zero
        qbd_ref[pp, hd:2 * hd, 0:bq] = zero
        qbd_ref[pp, hd:2 * hd, bq:2 * bq] = qb[hd:2 * hd]
        qbd_ref[pp, 2 * hd:2 * hd + LANES, :] = ones
    acc_ref[...] = jnp.zeros_like(acc_ref)

    nq = ka_ref.shape[0]

    def scores(s_ref, j):
        for pp in range(ATT_PAIRS):
            s_ref[pp] = _dot(ka_ref[j, :, pp * 2 * LANES:(pp + 1) * 2 * LANES], qbd_ref[pp])

    def attend(s_ref, j, carry, masked):
        out = []
        for pp in range(ATT_PAIRS):
            m_prev, l_prev = carry[2 * pp], carry[2 * pp + 1]
            s = s_ref[pp]
            if masked:
                key = j * bq + lax.broadcasted_iota(I32, s.shape, 0)
                qq = lax.broadcasted_iota(I32, s.shape, 1)
                qq = i * bq + jnp.where(qq >= bq, qq - bq, qq)
                s = jnp.where(key <= qq, s, -jnp.inf)
            m_new = jnp.maximum(m_prev, jnp.max(s, axis=0, keepdims=True))
            alpha = jnp.exp2(m_prev - m_new)
            p = jnp.exp2(s - m_new)
            l_new = alpha * l_prev + jnp.sum(p, axis=0, keepdims=True)
            pb = p.astype(BF16)
            vb = vt_ref[jnp.minimum(j, nq - 1), pp * LANES:(pp + 1) * LANES, :]
            r0 = pp * LANES
            acc_ref[r0:r0 + hd] = alpha[:, 0:bq] * acc_ref[r0:r0 + hd] + _dot(vb[0:hd], pb[:, 0:bq])
            acc_ref[r0 + hd:r0 + 2 * hd] = (alpha[:, bq:2 * bq] * acc_ref[r0 + hd:r0 + 2 * hd]
                                            + _dot(vb[hd:2 * hd], pb[:, bq:2 * bq]))
            out.extend((m_new, l_new))
        return tuple(out)

    def double_step(m, carry):
        j = 2 * m
        scores(sb_ref, j + 1)
        carry = attend(sa_ref, j, carry, False)
        scores(sa_ref, j + 2)
        return attend(sb_ref, j + 1, carry, False)

    init = (jnp.full((1, 2 * bq), -jnp.inf, F32), jnp.zeros((1, 2 * bq), F32)) * ATT_PAIRS
    scores(sa_ref, 0)
    carry = lax.fori_loop(0, i // 2, double_step, init)
    j = 2 * (i // 2)
    scores(sb_ref, jnp.minimum(j + 1, nq - 1))
    carry = attend(sa_ref, j, carry, True)
    carry = attend(sb_ref, j + 1, carry, True)
    for pp in range(ATT_PAIRS):
        l = carry[2 * pp + 1]
        r0 = pp * LANES
        o_ref[r0:r0 + hd] = acc_ref[r0:r0 + hd] / l[:, 0:bq]
        o_ref[r0 + hd:r0 + 2 * hd] = acc_ref[r0 + hd:r0 + 2 * hd] / l[:, bq:2 * bq]


def _attn_p(qt, kaug, vt, batch, t):
    blk = ATT_BLK
    nq = t // blk
    ngrp = N_HEADS // 2 // ATT_PAIRS
    rows = ATT_PAIRS * LANES
    qt4 = qt.reshape(batch, nq, D_ATT, blk)
    vt4 = vt.reshape(batch, nq, D_ATT, blk)
    ka4 = kaug.reshape(batch, nq, blk, 2 * D_ATT)
    return pl.pallas_call(
        _attn_p_kernel,
        out_shape=jax.ShapeDtypeStruct((batch, D_ATT, t), F32),
        grid=(batch, ngrp, nq),
        in_specs=[pl.BlockSpec((None, None, rows, blk), lambda b, p, i: (b, i, p, 0)),
                  pl.BlockSpec((None, nq, blk, 2 * rows), lambda b, p, i: (b, 0, 0, p), pipeline_mode=pl.Buffered(1)),
                  pl.BlockSpec((None, nq, rows, blk), lambda b, p, i: (b, 0, p, 0), pipeline_mode=pl.Buffered(1))],
        out_specs=pl.BlockSpec((None, rows, blk), lambda b, p, i: (b, p, i)),
        scratch_shapes=[pltpu.VMEM((ATT_PAIRS, 2 * LANES, 2 * blk), BF16), pltpu.VMEM((rows, blk), F32),
                        pltpu.VMEM((ATT_PAIRS, blk, 2 * blk), F32), pltpu.VMEM((ATT_PAIRS, blk, 2 * blk), F32)],
        compiler_params=_cparams(("arbitrary", "arbitrary", "arbitrary")),
        name="attn_p",
    )(qt4, ka4, vt4)


def _lane_cumsum(x):
    n = x.shape[-1]
    lane = lax.broadcasted_iota(I32, x.shape, x.ndim - 1)
    s = 1
    while s < n:
        x = x + jnp.where(lane >= s, pltpu.roll(x, s, x.ndim - 1), 0.0)
        s *= 2
    return x


def _fprep_s_kernel(cl_ref, ln_ref, fc_ref, fn_ref):
    b, h, p = cl_ref.shape
    cum = _lane_cumsum(cl_ref[...].reshape(b * h, p))
    fc_ref[...] = (-cum).reshape(b, h, p)
    total = cum[:, p - 1:p]
    cn = _lane_cumsum(ln_ref[...].reshape(b * h, LANES))
    fn_ref[...] = (-(total + cn)).reshape(b, h, LANES)


def _fprep_s(cache_lf_t, new_lf_t):
    b, h, p = cache_lf_t.shape
    return pl.pallas_call(
        _fprep_s_kernel,
        out_shape=(jax.ShapeDtypeStruct((b, h, p), F32), jax.ShapeDtypeStruct((b, h, LANES), F32)),
        grid=(1,),
        in_specs=[pl.BlockSpec((b, h, p), lambda i: (0, 0, 0)), pl.BlockSpec((b, h, LANES), lambda i: (0, 0, 0))],
        out_specs=(pl.BlockSpec((b, h, p), lambda i: (0, 0, 0)), pl.BlockSpec((b, h, LANES), lambda i: (0, 0, 0))),
        compiler_params=_cparams(("arbitrary",)),
        name="fprep_s",
    )(cache_lf_t, new_lf_t)


def _attn_s_kernel(q_ref, ck_ref, cv_ref, kn_ref, vn_ref, fc_ref, fn_ref, o_ref, qbd_ref, m_ref, l_ref, acc_ref,
                   *, tq):
    j = pl.program_id(1)
    nkv = pl.num_programs(1)
    rows = N_HEADS * tq
    bk = ck_ref.shape[1]

    @pl.when(j == 0)
    def _():
        q = q_ref[...]
        qrep = jnp.broadcast_to(q[None], (N_HEADS, tq, D_ATT)).reshape(rows, D_ATT)
        rh = lax.broadcasted_iota(I32, (rows, D_ATT), 0) // tq
        ch = lax.broadcasted_iota(I32, (rows, D_ATT), 1) // HEAD_DIM
        qbd_ref[...] = jnp.where(rh == ch, qrep, jnp.zeros_like(qrep))
        m_ref[...] = jnp.full_like(m_ref, -jnp.inf)
        l_ref[...] = jnp.zeros_like(l_ref)
        acc_ref[...] = jnp.zeros_like(acc_ref)

    def update(s, vt):
        m_prev = m_ref[...]
        m_new = jnp.maximum(m_prev, jnp.max(s, axis=1, keepdims=True))
        alpha = jnp.exp(m_prev - m_new)
        p = jnp.exp(s - m_new)
        l_ref[...] = alpha * l_ref[...] + jnp.sum(p, axis=1, keepdims=True)
        m_ref[...] = m_new
        acc_ref[...] = alpha * acc_ref[...] + _dot_nt(p.astype(BF16), vt)

    def bias(f, width):
        return jnp.broadcast_to(f[:, None, :], (N_HEADS, tq, width)).reshape(rows, width)

    s = _dot(qbd_ref[...], ck_ref[...].astype(BF16)) + bias(fc_ref[...], bk)
    update(s, cv_ref[...].astype(BF16))

    @pl.when(j == nkv - 1)
    def _():
        s2 = _dot(qbd_ref[...], kn_ref[...].astype(BF16)) + bias(fn_ref[...][:, 0:tq], tq)
        key = lax.broadcasted_iota(I32, (rows, tq), 1)
        qq = lax.broadcasted_iota(I32, (rows, tq), 0) % tq
        update(jnp.where(key <= qq, s2, -jnp.inf), vn_ref[...].astype(BF16))
        o = acc_ref[...] / l_ref[...]
        rh = lax.broadcasted_iota(I32, (rows, D_ATT), 0) // tq
        ch = lax.broadcasted_iota(I32, (rows, D_ATT), 1) // HEAD_DIM
        o = jnp.where(rh == ch, o, 0.0).reshape(N_HEADS, tq, D_ATT)
        o_ref[...] = jnp.sum(o, axis=0)


def _attn_s(q, cache_k, cache_v, k_new, v_new, fc, fn):
    b, tq, _ = q.shape
    p = cache_k.shape[2]
    bk = min(SAMPLE_KV_BLK, p)
    nkv = p // bk
    rows = N_HEADS * tq
    tok = lambda: pl.BlockSpec((None, tq, D_ATT), lambda bi, j: (bi, 0, 0))
    new = lambda: pl.BlockSpec((None, D_ATT, tq), lambda bi, j: (bi, 0, 0))
    return pl.pallas_call(
        functools.partial(_attn_s_kernel, tq=tq),
        out_shape=jax.ShapeDtypeStruct((b, tq, D_ATT), F32),
        grid=(b, nkv),
        in_specs=[tok(),
                  pl.BlockSpec((None, D_ATT, bk), lambda bi, j: (bi, 0, j)),
                  pl.BlockSpec((None, D_ATT, bk), lambda bi, j: (bi, 0, j)),
                  new(), new(),
                  pl.BlockSpec((None, N_HEADS, bk), lambda bi, j: (bi, 0, j)),
                  pl.BlockSpec((None, N_HEADS, LANES), lambda bi, j: (bi, 0, 0))],
        out_specs=tok(),
        scratch_shapes=[pltpu.VMEM((rows, D_ATT), BF16), pltpu.VMEM((rows, 1), F32),
                        pltpu.VMEM((rows, 1), F32), pltpu.VMEM((rows, D_ATT), F32)],
        compiler_params=_cparams(("arbitrary", "arbitrary")),
        name="attn_s",
    )(q, cache_k, cache_v, k_new, v_new, fc, fn)


def _outproj_kernel(x_ref, y4_ref, att_ref, gt_ref, sh_ref, sc_ref, gf_ref, wglu_ref, bglu_ref, wout_ref,
                    wr_ref, br_ref, cin_ref,
                    x1_ref, h2_ref, idx_ref, gate_ref, rank_ref, cnt_ref, carry_ref, ys_ref, *, att_transposed):
    i = pl.program_id(0)
    nb, tt, d = x_ref.shape
    tm = nb * tt

    @pl.when(i == 0)
    def _():
        carry_ref[...] = cin_ref[...]

    for s in range(SSM_CHUNK):
        for o in range(N_OCT):
            ys_ref[o, pl.ds(s, tm // SSM_CHUNK, stride=SSM_CHUNK), :] = (
                y4_ref[o, :, s * LANES:(s + 1) * LANES].astype(F32))
    ysf = jnp.concatenate([ys_ref[o] for o in range(N_OCT)], axis=-1)
    glu = ysf * jax.nn.sigmoid(_dot(ysf.astype(BF16), wglu_ref[...]) + bglu_ref[...])
    att = att_ref[...]
    if att_transposed:
        att = att.T
    mix = _dot(glu.astype(BF16), wout_ref[0:D_SSM, :]) + _dot(att.astype(BF16), wout_ref[D_SSM:, :])
    x1 = x_ref[...] + gt_ref[...] * mix.reshape(nb, tt, d)
    x1_ref[...] = x1
    h2 = _norm_mod(x1, gf_ref[...], sc_ref[...], sh_ref[...]).reshape(tm, d)
    bits = lax.bitcast_convert_type(h2.astype(BF16).astype(F32), I32)
    h2_ref[...] = (bits[:, d // 2:] & jnp.int32(-65536)) | lax.shift_right_logical(bits[:, :d // 2], 16)

    logits = lax.dot_general(wr_ref[...], h2, (((1,), (1,)), ((), ())), preferred_element_type=F32,
                             precision=lax.Precision.HIGHEST) + br_ref[...]
    sub = lax.broadcasted_iota(I32, logits.shape, 0)
    work = logits
    vals, idxs = [], []
    for _ in range(TOP_K):
        mx = jnp.max(work, axis=0, keepdims=True)
        ix = jnp.min(jnp.where(work == mx, sub, N_EXPERTS), axis=0, keepdims=True)
        vals.append(mx)
        idxs.append(ix)
        work = jnp.where(sub == ix, -jnp.inf, work)
    ex = [jnp.exp(v - vals[0]) for v in vals]
    den = ex[0] + ex[1] + ex[2] + ex[3]
    mh = jnp.where(work == -jnp.inf, 1.0, 0.0)
    r = lax.broadcasted_iota(I32, (tm, tm), 0)
    c = lax.broadcasted_iota(I32, (tm, tm), 1)
    earlier = jnp.where(r < c, 1.0, 0.0).astype(BF16)
    carry = carry_ref[...]
    before = _dot(mh.astype(BF16), earlier) + carry[:, 0:1]
    carry_ref[...] = carry + jnp.sum(mh, axis=1, keepdims=True)
    for kk in range(TOP_K):
        idx_ref[kk:kk + 1, :] = idxs[kk]
        gate_ref[kk:kk + 1, :] = ex[kk] / den
        rk = jnp.sum(jnp.where(sub == idxs[kk], before, 0.0), axis=0, keepdims=True)
        rank_ref[kk:kk + 1, :] = rk.astype(I32)
    cnt_ref[...] = carry_ref[...]


def _outproj(x, y4, att, gt, sh, sc, g_ffn, w, counts_in, *, att_transposed, tm):
    b, t, d = x.shape
    n = b * t
    tm = min(tm, n)
    tt = min(t, tm)
    nb = tm // tt
    tpb = t // tt

    def xmap(i):
        return (i // tpb, i % tpb, 0) if nb == 1 else (i, 0, 0)

    def bmap(i):
        return (i // tpb, 0, 0) if nb == 1 else (i, 0, 0)

    full = lambda shape: pl.BlockSpec(shape, lambda i: (0,) * len(shape))
    if att_transposed:
        att_spec = pl.BlockSpec((None, D_ATT, tm), lambda i: (i // tpb, 0, i % tpb))
    else:
        att_spec = pl.BlockSpec((tm, D_ATT), lambda i: (i, 0))
    choice = lambda dt: (jax.ShapeDtypeStruct((TOP_K, n), dt), pl.BlockSpec((TOP_K, tm), lambda i: (0, i)))
    outs = [(jax.ShapeDtypeStruct((b, t, d), F32), pl.BlockSpec((nb, tt, d), xmap)),
            (jax.ShapeDtypeStruct((n, d // 2), I32), pl.BlockSpec((tm, d // 2), lambda i: (i, 0))),
            choice(I32), choice(F32), choice(I32),
            (jax.ShapeDtypeStruct((N_EXPERTS, LANES), F32), full((N_EXPERTS, LANES)))]
    return pl.pallas_call(
        functools.partial(_outproj_kernel, att_transposed=att_transposed),
        out_shape=tuple(o[0] for o in outs),
        grid=(n // tm,),
        in_specs=[pl.BlockSpec((nb, tt, d), xmap),
                  pl.BlockSpec((N_OCT, tm // SSM_CHUNK, SSM_CHUNK * LANES), lambda i: (0, i, 0)),
                  att_spec,
                  pl.BlockSpec((nb, 1, d), bmap), pl.BlockSpec((nb, 1, d), bmap), pl.BlockSpec((nb, 1, d), bmap),
                  full((1, d)), full((D_SSM, D_SSM)), full((1, D_SSM)), full((d, d)),
                  full((N_EXPERTS, d)), full((N_EXPERTS, 1)), full((N_EXPERTS, LANES))],
        out_specs=tuple(o[1] for o in outs),
        scratch_shapes=[pltpu.VMEM((N_EXPERTS, LANES), F32), pltpu.VMEM((N_OCT, tm, LANES), F32)],
        compiler_params=_cparams(("arbitrary",)),
        name="outproj_t" if att_transposed else "outproj",
    )(x, y4, att, gt, sh, sc, g_ffn, w["wglu"], w["bglu"], w["wout"], w["wr"], w["br"], counts_in)


def _dispatch_kernel(zst_ref, nu_ref, dest_ref, *rest, tm, tiles):
    h_refs = rest[:len(tiles)]
    xs_ref, zero_ref, hbuf_ref, zsem, sems = rest[len(tiles):]
    i = pl.program_id(0)
    n_blocks = xs_ref.shape[0] // MOE_BLK

    @pl.when(i == 0)
    def _():
        zero_ref[...] = jnp.zeros_like(zero_ref)

        def zero_copy(start):
            start = pl.multiple_of(start, MOE_BLK)
            return pltpu.make_async_copy(zero_ref, xs_ref.at[pl.ds(start, MOE_BLK)], zsem)

        for e in range(N_EXPERTS):
            @pl.when(zst_ref[e] >= 0)
            def _():
                zero_copy(jnp.maximum(zst_ref[e], 0)).start()

        def tail_start(j, carry):
            zero_copy(j * MOE_BLK).start()
            return carry

        def tail_wait(j, carry):
            zero_copy(j * MOE_BLK).wait()
            return carry

        lax.fori_loop(nu_ref[0], n_blocks, tail_start, 0)
        for e in range(N_EXPERTS):
            @pl.when(zst_ref[e] >= 0)
            def _():
                zero_copy(jnp.maximum(zst_ref[e], 0)).wait()
        lax.fori_loop(nu_ref[0], n_blocks, tail_wait, 0)

    slot = i % 2

    def scatter_rows(h_ref):
        hbuf_ref[slot] = h_ref[...]

        def issue(t, carry):
            for kk in range(TOP_K):
                dst = dest_ref[0, 0, t * TOP_K + kk]
                pltpu.make_async_copy(hbuf_ref.at[slot, pl.ds(t, 1)], xs_ref.at[pl.ds(dst, 1)], sems.at[slot]).start()
            return carry

        lax.fori_loop(0, tm, issue, 0, unroll=8)

    first_tile = 0
    for h_ref, n_tiles in zip(h_refs, tiles):
        pl.when((i >= first_tile) & (i < first_tile + n_tiles))(functools.partial(scatter_rows, h_ref))
        first_tile += n_tiles

    def wait_tile(s):
        n_copied = tm * TOP_K
        pltpu.make_async_copy(xs_ref.at[pl.ds(0, n_copied)], xs_ref.at[pl.ds(0, n_copied)], sems.at[s]).wait()

    @pl.when(i > 0)
    def _():
        wait_tile(1 - slot)

    @pl.when(i == pl.num_programs(0) - 1)
    def _():
        wait_tile(slot)


def _dispatch(h2s, dest, zstart, n_used, n_rows, tm):
    d = h2s[0].shape[1]
    dt = h2s[0].dtype
    tiles = tuple(h.shape[0] // tm for h in h2s)
    nt = sum(tiles)
    dest3 = dest.reshape(nt, 1, tm * TOP_K)
    in_specs = [pl.BlockSpec((1, 1, tm * TOP_K), lambda i, *_: (i, 0, 0), memory_space=pltpu.SMEM)]
    first_tile = 0
    for n_tiles in tiles:
        in_specs.append(pl.BlockSpec(
            (tm, d), lambda i, *_, f=first_tile, m=n_tiles: (jnp.clip(i - f, 0, m - 1), 0)))
        first_tile += n_tiles
    return pl.pallas_call(
        functools.partial(_dispatch_kernel, tm=tm, tiles=tiles),
        out_shape=jax.ShapeDtypeStruct((n_rows, d), dt),
        grid_spec=pltpu.PrefetchScalarGridSpec(
            num_scalar_prefetch=2, grid=(nt,),
            in_specs=in_specs,
            out_specs=pl.BlockSpec(memory_space=pl.ANY),
            scratch_shapes=[pltpu.VMEM((MOE_BLK, d), dt), pltpu.VMEM((2, tm, d), dt),
                            pltpu.SemaphoreType.DMA, pltpu.SemaphoreType.DMA((2,))]),
        compiler_params=_cparams(("arbitrary",)),
        name="moe_dispatch",
    )(zstart, n_used, dest3, *h2s)


def _expert_kernel(be_ref, nxt_ref, slot_ref, first_ref, nu_ref, xs_ref, wup_hbm, bup_ref, wdn_hbm, bdn_ref,
                   ys_ref, wup_f32, wdn_f32, wup_bf, wdn_bf, sems):
    i = pl.program_id(0)
    e = be_ref[i]

    def fetch(expert, slot):
        return (pltpu.make_async_copy(wup_hbm.at[expert], wup_f32.at[slot], sems.at[0, slot]),
                pltpu.make_async_copy(wdn_hbm.at[expert], wdn_f32.at[slot], sems.at[1, slot]))

    @pl.when(i < nu_ref[0])
    def _():
        @pl.when(first_ref[i] == 1)
        def _():
            slot = slot_ref[i]

            @pl.when(i == 0)
            def _():
                for copy in fetch(e, slot):
                    copy.start()

            for copy in fetch(e, slot):
                copy.wait()

            @pl.when(nxt_ref[i] >= 0)
            def _():
                for copy in fetch(jnp.maximum(nxt_ref[i], 0), 1 - slot):
                    copy.start()

            wup_bf[...] = wup_f32[slot].astype(BF16)
            wdn_bf[...] = wdn_f32[slot].astype(BF16)

        words = xs_ref[...]
        half = words.shape[1]
        x_lo = lax.bitcast_convert_type(words << 16, F32).astype(BF16)
        x_hi = lax.bitcast_convert_type(words & jnp.int32(-65536), F32).astype(BF16)
        up = _dot(x_lo, wup_bf[0:half, :]) + _dot(x_hi, wup_bf[half:, :]) + bup_ref[...]
        x_glu = jnp.minimum(up[:, :D_FF], SWIGLU_LIMIT)
        x_lin = jnp.clip(up[:, D_FF:], -SWIGLU_LIMIT, SWIGLU_LIMIT)
        act = (x_lin + 1.0) * (x_glu * jax.nn.sigmoid(SWIGLU_ALPHA * x_glu))
        ys_ref[...] = (_dot(act.astype(BF16), wdn_bf[...]) + bdn_ref[...]).astype(ys_ref.dtype)

    @pl.when(i >= nu_ref[0])
    def _():
        ys_ref[...] = jnp.zeros_like(ys_ref)


def _experts(xs, block_e, n_used, has_rows, w_up, b_up, w_down, b_down):
    n_rows, packed = xs.shape
    d = 2 * packed
    nblk = n_rows // MOE_BLK
    blocks = jnp.arange(nblk, dtype=I32)
    used = blocks < n_used[0]
    first = (used & ((blocks == 0) | (block_e != jnp.roll(block_e, 1)))).astype(I32)
    slot = ((jnp.cumsum(first) - 1) % 2).astype(I32)
    experts = jnp.arange(N_EXPERTS, dtype=I32)
    later = jnp.where(has_rows[None, :] & (experts[None, :] > experts[:, None]), experts[None, :], N_EXPERTS)
    nxt_e = jnp.min(later, axis=1)
    nxt_e = jnp.where(nxt_e < N_EXPERTS, nxt_e, -1).astype(I32)
    nxt = jnp.sum(jnp.where(block_e[:, None] == experts[None, :], nxt_e[None, :], 0), axis=1).astype(I32)
    rowmap = lambda i, be, nx, sl, fi, nu: (jnp.minimum(i, nu[0] - 1), 0)
    emap = lambda i, be, nx, sl, fi, nu: (be[i], 0, 0)
    return pl.pallas_call(
        _expert_kernel,
        out_shape=jax.ShapeDtypeStruct((n_rows, d), BF16),
        grid_spec=pltpu.PrefetchScalarGridSpec(
            num_scalar_prefetch=5, grid=(nblk,),
            in_specs=[pl.BlockSpec((MOE_BLK, packed), rowmap),
                      pl.BlockSpec(memory_space=pl.ANY), pl.BlockSpec((None, 1, 2 * D_FF), emap),
                      pl.BlockSpec(memory_space=pl.ANY), pl.BlockSpec((None, 1, d), emap)],
            out_specs=pl.BlockSpec((MOE_BLK, d), lambda i, *_: (i, 0)),
            scratch_shapes=[pltpu.VMEM((2, d, 2 * D_FF), F32), pltpu.VMEM((2, D_FF, d), F32),
                            pltpu.VMEM((d, 2 * D_FF), BF16), pltpu.VMEM((D_FF, d), BF16),
                            pltpu.SemaphoreType.DMA((2, 2))]),
        compiler_params=_cparams(("arbitrary",)),
        name="moe_experts",
    )(block_e, nxt, slot, first, n_used, xs, w_up, b_up, w_down, b_down)


def _combine_kernel(meta_ref, metan_ref, x1_ref, col_ref, gate_ref, gt_ref, gfin_ref, ys_ref, y_ref, ybuf_ref, sems,
                    *, tm):
    i = pl.program_id(0)
    nt = pl.num_programs(0)
    nb, tt, d = x1_ref.shape
    rows = ybuf_ref.shape[1]

    ra = RUN_ALIGN

    def start_copies(m_ref, slot):
        def per_expert(e, carry):
            a = m_ref[0, 0, e]
            n_chunks = m_ref[0, 0, N_EXPERTS + e]
            bo = m_ref[0, 0, 2 * N_EXPERTS + e]

            def per_chunk(c, carry2):
                src = ys_ref.at[pl.ds(pl.multiple_of(a + ra * c, ra), ra)]
                dst = ybuf_ref.at[slot, pl.ds(pl.multiple_of(bo + ra * c, ra), ra)]
                pltpu.make_async_copy(src, dst, sems.at[slot]).start()
                return carry2

            lax.fori_loop(0, n_chunks, per_chunk, 0)
            return carry

        lax.fori_loop(0, N_EXPERTS, per_expert, 0)

    def wait_copies(m_ref, slot):
        total = m_ref[0, 0, 3 * N_EXPERTS]

        def wait_rows(n_rows):
            pltpu.make_async_copy(ys_ref.at[pl.ds(0, n_rows)], ybuf_ref.at[slot, pl.ds(0, n_rows)],
                                  sems.at[slot]).wait()

        lax.fori_loop(0, total // WAIT_GROUP, lambda c, carry: (wait_rows(WAIT_GROUP * ra), carry)[1], 0)
        lax.fori_loop(0, total % WAIT_GROUP, lambda c, carry: (wait_rows(ra), carry)[1], 0)

    @pl.when(i == 0)
    def _():
        ybuf_ref[...] = jnp.zeros_like(ybuf_ref)
        start_copies(meta_ref, 0)

    @pl.when(i + 1 < nt)
    def _():
        start_copies(metan_ref, (i + 1) % 2)

    slot = i % 2
    wait_copies(meta_ref, slot)
    col = col_ref[...]
    gate = gate_ref[...]
    lane = lax.broadcasted_iota(I32, (tm, rows), 1)
    pick = jnp.where(lane == col[:, 0:1], gate[:, 0:1], 0.0)
    for kk in range(1, TOP_K):
        pick = pick + jnp.where(lane == col[:, kk:kk + 1], gate[:, kk:kk + 1], 0.0)
    moe = _dot(pick.astype(BF16), ybuf_ref[slot])
    x2 = x1_ref[...] + gt_ref[...] * moe.reshape(nb, tt, d)
    ms = jnp.mean(x2 * x2, axis=-1, keepdims=True)
    y_ref[...] = x2 * lax.rsqrt(ms + NORM_EPS) * gfin_ref[...]


def _combine_plan(idx, dest, off, before0, tm):
    n = idx.shape[1]
    nt = n // tm
    is_e = idx[None, :, :] == jnp.arange(N_EXPERTS, dtype=I32)[:, None, None]
    cnt = jnp.any(is_e, axis=1).astype(I32).reshape(N_EXPERTS, nt, tm).sum(axis=2).T
    before = before0[None, :] + jnp.cumsum(cnt, axis=0) - cnt
    start = off[None, :] + before
    ra = RUN_ALIGN
    lo = start // ra * ra
    hi = jnp.where(cnt > 0, (start + cnt + ra - 1) // ra * ra, lo)
    span = hi - lo
    boff = jnp.cumsum(span, axis=1) - span
    delta = jnp.repeat((boff - lo).T, tm, axis=1)
    col = dest + jnp.sum(jnp.where(is_e, delta[:, None, :], 0), axis=0)
    total = jnp.sum(span // ra, axis=1, keepdims=True)
    meta = jnp.concatenate([lo, span // ra, boff, jnp.broadcast_to(total, lo.shape)], axis=1).astype(I32)
    return meta.reshape(nt, 1, 4 * N_EXPERTS), col.T.astype(I32)


def _combine(x1, gate, idx, dest, off, before0, ys, gt, g_final, tm):
    b, t, d = x1.shape
    n = b * t
    tm = min(tm, n)
    tt = min(t, tm)
    nb = tm // tt
    tpb = t // tt
    nt = n // tm
    rows = -(-(tm * TOP_K + 2 * (RUN_ALIGN - 1) * N_EXPERTS) // 256) * 256
    meta, col = _combine_plan(idx, dest, off, before0, tm)

    def xmap(i):
        return (i // tpb, i % tpb, 0) if nb == 1 else (i, 0, 0)

    def bmap(i):
        return (i // tpb, 0, 0) if nb == 1 else (i, 0, 0)

    smem = lambda imap: pl.BlockSpec((1, 1, 4 * N_EXPERTS), imap, memory_space=pltpu.SMEM)
    return pl.pallas_call(
        functools.partial(_combine_kernel, tm=tm),
        out_shape=jax.ShapeDtypeStruct((b, t, d), F32),
        grid=(nt,),
        in_specs=[smem(lambda i: (i, 0, 0)), smem(lambda i: (jnp.minimum(i + 1, nt - 1), 0, 0)),
                  pl.BlockSpec((nb, tt, d), xmap),
                  pl.BlockSpec((tm, TOP_K), lambda i: (i, 0)),
                  pl.BlockSpec((tm, TOP_K), lambda i: (i, 0)),
                  pl.BlockSpec((nb, 1, d), bmap),
                  pl.BlockSpec((1, d), lambda i: (0, 0)),
                  pl.BlockSpec(memory_space=pl.ANY)],
        out_specs=pl.BlockSpec((nb, tt, d), xmap),
        scratch_shapes=[pltpu.VMEM((2, rows, d), BF16), pltpu.SemaphoreType.DMA((2,))],
        compiler_params=_cparams(("arbitrary",)),
        name="moe_combine",
    )(meta, meta, x1, col, gate, gt, g_final, ys)


def _moe(streams, counts, g_final, w_up, b_up, w_down, b_down):
    n_total = sum(s["h2"].shape[0] for s in streams)
    cnt = counts[:, 0].astype(I32)
    padded = (cnt + MOE_BLK - 1) // MOE_BLK * MOE_BLK
    pad_end = jnp.cumsum(padded)
    off = (pad_end - padded).astype(I32)
    n_blocks = -(-n_total * TOP_K // MOE_BLK) + N_EXPERTS
    n_rows = n_blocks * MOE_BLK
    starts = jnp.arange(n_blocks, dtype=I32) * MOE_BLK
    block_e = jnp.minimum(jnp.sum((pad_end[None, :] <= starts[:, None]).astype(I32), axis=1), N_EXPERTS - 1)
    n_used = (pad_end[-1:] // MOE_BLK).astype(I32)
    zstart = jnp.where(padded > 0, pad_end - MOE_BLK, -1).astype(I32)
    experts = jnp.arange(N_EXPERTS, dtype=I32)
    dests = []
    for s in streams:
        seg = jnp.sum(jnp.where(s["idx"][None] == experts[:, None, None], off[:, None, None], 0), axis=0)
        dests.append((seg + s["rank"]).astype(I32))
    tm = min([512] + [s["h2"].shape[0] for s in streams])
    dest_rows = jnp.concatenate([dest.T for dest in dests], axis=0)
    xs = _dispatch([s["h2"] for s in streams], dest_rows, zstart, n_used, n_rows, tm)
    ys = _experts(xs, block_e, n_used, padded > 0, w_up, b_up, w_down, b_down)
    return [_combine(s["x1"], s["gate"].T, s["idx"], dest, off, s["before"], ys, s["gt"], g_final, 256)
            for s, dest in zip(streams, dests)]


def _stream(x, mod, w, ssm_ops, s0, cache, params, counts_in):
    b, t, d = x.shape
    n = b * t
    sh_mix, sc_mix, gt_mix, sh_ffn, sc_ffn, gt_ffn = mod
    prompt = cache is None
    tm = 1024 if prompt else 512
    u4, q, kt, vt, lft, vtb, kaug = _inproj(x, sh_mix, sc_mix, params["g_mix"], w, prompt=prompt, tm=tm)
    n_chunks = t // SSM_CHUNK
    y4, s_fin = _ssm(u4, ssm_ops, s0, b, n_chunks)
    if prompt:
        att = _attn_p(q, kaug, vtb, b, t)
    else:
        cache_k, cache_v, cache_lf = cache
        p = cache_k.shape[1]
        ck_t = jnp.transpose(cache_k, (0, 2, 3, 1)).reshape(b, D_ATT, p)
        cv_t = jnp.transpose(cache_v, (0, 2, 3, 1)).reshape(b, D_ATT, p)
        cl_t = jnp.transpose(cache_lf, (0, 2, 1))
        ln_t = jnp.pad(lft, ((0, 0), (0, 0), (0, LANES - t)))
        fc, fn = _fprep_s(cl_t, ln_t)
        att = _attn_s(q.reshape(b, t, D_ATT), ck_t, cv_t, kt, vt, fc, fn).reshape(n, D_ATT)
    x1, h2, idx, gate, rank, counts = _outproj(x, y4, att, gt_mix, sh_ffn, sc_ffn, params["g_ffn"], w, counts_in,
                                               att_transposed=prompt, tm=tm)
    routed = {"x1": x1, "h2": h2, "idx": idx, "gate": gate, "rank": rank, "gt": gt_ffn,
              "before": counts_in[:, 0].astype(I32)}
    heads = lambda a: jnp.transpose(a.reshape(b, N_HEADS, HEAD_DIM, t), (0, 3, 1, 2))[None]
    return routed, counts, heads(kt), heads(vt), jnp.transpose(lft, (0, 2, 1))[None], s_fin


def _state_in(re, im):
    b = re.shape[0]
    s = jnp.concatenate([re.reshape(b, N_OCT, OCT_STATE // 2), im.reshape(b, N_OCT, OCT_STATE // 2)], axis=-1)
    return jnp.transpose(s, (1, 0, 2))


def _state_out(s):
    b = s.shape[1]
    s = jnp.transpose(s, (1, 0, 2))
    half = OCT_STATE // 2
    re = s[:, :, :half].reshape(1, b, N_SSM_GROUPS, SSM_STATE)
    im = s[:, :, half:].reshape(1, b, N_SSM_GROUPS, SSM_STATE)
    return re, im


def kernel(x_prompt, x_sample, c_prompt, c_sample, cache_k, cache_v, cache_logf, state_ssm_re, state_ssm_im, w_ada, b_ada, g_mix, w_in, b_forget, ssm_log_dt, ssm_a_re, ssm_a_im, ssm_b_re, ssm_b_im, ssm_c_re, ssm_c_im, ssm_d, w_glu, b_glu, w_out, g_ffn, w_router, b_router, w_up, b_up, w_down, b_down, g_final):
    assert w_ada.shape[0] == 1, "single-layer trunk"
    d = D_MODEL
    bp, tp, _ = x_prompt.shape
    bs, ts, _ = x_sample.shape
    n_c = bp + bs
    c_all = jnp.pad(jnp.concatenate([c_prompt, c_sample], axis=0), ((0, -n_c % 8), (0, 0)))
    mod = _ada(c_all, w_ada[0], b_ada[0].reshape(1, 6 * d))
    mod_p = [mod[:bp, j * d:(j + 1) * d].reshape(bp, 1, d) for j in range(6)]
    mod_s = [mod[bp:n_c, j * d:(j + 1) * d].reshape(bs, 1, d) for j in range(6)]
    wi = w_in[0]
    o1, o2, o3, o4 = D_SSM, D_SSM + D_ATT, D_SSM + 2 * D_ATT, D_SSM + 3 * D_ATT
    w_q = wi[:, o1:o2].astype(BF16)
    w_k = wi[:, o2:o3].astype(BF16)
    unused = jnp.zeros((8, LANES), BF16)
    base = {"wu": wi[:, :o1].astype(BF16), "wkt": w_k.T, "wvt": wi[:, o3:o4].astype(BF16).T,
            "wft": wi[:, o4:].astype(BF16).T, "bfc": b_forget[0].reshape(N_HEADS, 1), "pm": _bias_placement(),
            "wglu": w_glu[0].astype(BF16), "bglu": b_glu[0].reshape(1, D_SSM), "wout": w_out[0].astype(BF16),
            "wr": w_router[0].T, "br": b_router[0].reshape(N_EXPERTS, 1)}
    w_p = dict(base, wq=w_q.T, wk=w_k)
    w_s = dict(base, wq=w_q, wk=unused)
    params = {"g_mix": g_mix[0].reshape(1, d), "g_ffn": g_ffn[0].reshape(1, d), "g_final": g_final.reshape(1, d),
              "w_up": w_up[0], "b_up": b_up[0].reshape(N_EXPERTS, 1, 2 * D_FF),
              "w_down": w_down[0], "b_down": b_down[0].reshape(N_EXPERTS, 1, d)}
    ssm_ops = _ssm_operators(ssm_log_dt[0], ssm_a_re[0], ssm_a_im[0], ssm_b_re[0], ssm_b_im[0],
                             ssm_c_re[0], ssm_c_im[0], ssm_d[0])
    zero_state = jnp.zeros((N_OCT, bp, OCT_STATE), F32)
    counts0 = jnp.zeros((N_EXPERTS, LANES), F32)
    r_p, counts_p, k_p, v_p, f_p, s_p = _stream(x_prompt, mod_p, w_p, ssm_ops, zero_state, None, params, counts0)
    s0 = _state_in(state_ssm_re[0], state_ssm_im[0])
    r_s, counts, k_s, v_s, f_s, s_s = _stream(x_sample, mod_s, w_s, ssm_ops, s0,
                                              (cache_k[0], cache_v[0], cache_logf[0]), params, counts_p)
    y_p, y_s = _moe([r_p, r_s], counts, params["g_final"],
                    params["w_up"], params["b_up"], params["w_down"], params["b_down"])
    sre_p, sim_p = _state_out(s_p)
    sre_s, sim_s = _state_out(s_s)
    return (y_p, y_s, k_p, v_p, f_p, sre_p, sim_p, k_s, v_s, f_s, sre_s, sim_s)
```

```python
import functools
import math

import jax
import jax.numpy as jnp
from jax import lax
from jax.experimental import pallas as pl
from jax.experimental.pallas import tpu as pltpu

F32 = jnp.float32
BF16 = jnp.bfloat16
I32 = jnp.int32

D_MODEL = 1024
D_SSM = 512
SSM_GROUP = 16
N_SSM_GROUPS = 32
SSM_STATE = 64
D_ATT = 512
HEAD_DIM = 64
N_HEADS = 8
N_EXPERTS = 32
TOP_K = 4
D_FF = 1024
SWIGLU_LIMIT = 7.0
SWIGLU_ALPHA = 1.702
NORM_EPS = 1e-5

LANES = 128
N_OCT = D_SSM // LANES
OCT_STATE = 2 * (N_SSM_GROUPS // N_OCT) * SSM_STATE
SSM_CHUNK = 8
SSM_ROWS = 512
SSM_INTERLEAVE = 4
ATT_BLK = 256
ATT_PAIRS = 4
LOG2E = math.log2(math.e)
SAMPLE_KV_BLK = 2048
MOE_BLK = 256
RUN_ALIGN = 16
WAIT_GROUP = 16
VMEM_LIMIT = 52 * 1024 * 1024


def _cparams(sem, vmem=VMEM_LIMIT):
    return pltpu.CompilerParams(dimension_semantics=sem, vmem_limit_bytes=vmem)


def _dot(a, b):
    return jnp.dot(a, b, preferred_element_type=F32)


def _dot_nt(a, b):
    return lax.dot_general(a, b, (((1,), (1,)), ((), ())), preferred_element_type=F32)


def _dot_hi(a, b):
    return jnp.dot(a, b, preferred_element_type=F32, precision=lax.Precision.HIGHEST)


def _ada_kernel(c_ref, w_ref, b_ref, o_ref):
    c = c_ref[...]
    s = c * jax.nn.sigmoid(c)
    o_ref[...] = _dot_hi(s, w_ref[...]) + b_ref[...]


def _ada(c_all, w_ada, b_ada):
    m, d = c_all.shape
    n = w_ada.shape[1]
    return pl.pallas_call(
        _ada_kernel,
        out_shape=jax.ShapeDtypeStruct((m, n), F32),
        grid=(n // d,),
        in_specs=[pl.BlockSpec((m, d), lambda j: (0, 0)),
                  pl.BlockSpec((d, d), lambda j: (0, j)),
                  pl.BlockSpec((1, d), lambda j: (0, j))],
        out_specs=pl.BlockSpec((m, d), lambda j: (0, j)),
        compiler_params=_cparams(("arbitrary",)),
        name="ada",
    )(c_all, w_ada, b_ada)


def _norm_mod(x, g, scale, shift):
    ms = jnp.mean(x * x, axis=-1, keepdims=True)
    y = x * lax.rsqrt(ms + NORM_EPS) * g
    return y * (1.0 + scale) + shift


def _log_sigmoid(z):
    return jnp.minimum(z, 0.0) - jnp.log1p(jnp.exp(-jnp.abs(z)))


def _inproj_kernel(x_ref, sh_ref, sc_ref, g_ref, wu_ref, wq_ref, wkt_ref, wvt_ref, wft_ref, bfc_ref,
                   wk_ref, pm_ref,
                   u4_ref, q_ref, kt_ref, vt_ref, lft_ref, vtb_ref, ka_ref, us_ref, carry_ref, *, prompt, tpb):
    nb, tt, d = x_ref.shape
    tm = nb * tt
    h = _norm_mod(x_ref[...], g_ref[...], sc_ref[...], sh_ref[...]).reshape(tm, d).astype(BF16)
    u = _dot(h, wu_ref[...])
    for o in range(N_OCT):
        us_ref[o] = u[:, o * LANES:(o + 1) * LANES]
    for s in range(SSM_CHUNK):
        for o in range(N_OCT):
            piece = us_ref[o, pl.ds(s, tm // SSM_CHUNK, stride=SSM_CHUNK), :]
            u4_ref[o, :, s * LANES:(s + 1) * LANES] = piece.astype(BF16)
    kt = _dot_nt(wkt_ref[...], h)
    vt = _dot_nt(wvt_ref[...], h)
    lft = _log_sigmoid(_dot_nt(wft_ref[...], h) + bfc_ref[...])
    scale = HEAD_DIM ** -0.5
    if prompt:
        kt_ref[...] = kt
        vt_ref[...] = vt
        lft_ref[...] = lft
        qt = (_dot_nt(wq_ref[...], h) * (scale * LOG2E)).astype(BF16)
        vtb = vt.astype(BF16)
        for j in range(tm // ATT_BLK):
            q_ref[j] = qt[:, j * ATT_BLK:(j + 1) * ATT_BLK]
            vtb_ref[j] = vtb[:, j * ATT_BLK:(j + 1) * ATT_BLK]
        k_tok = _dot(h, wk_ref[...])

        @pl.when(pl.program_id(0) % tpb == 0)
        def _():
            carry_ref[...] = jnp.zeros_like(carry_ref)

        cum = _lane_cumsum(lft) + carry_ref[:, 0:1]
        carry_ref[...] = jnp.broadcast_to(cum[:, tm - 1:tm], carry_ref.shape)
        nf = cum * (-LOG2E)
        hi = nf.astype(BF16).astype(F32)
        r1 = nf - hi
        mid = r1.astype(BF16).astype(F32)
        lo = (r1 - mid).astype(BF16).astype(F32)
        place = lambda piece, pm: lax.dot_general(piece, pm, (((0,), (0,)), ((), ())), preferred_element_type=F32)
        slab = place(hi, pm_ref[0]) + place(mid, pm_ref[1]) + place(lo, pm_ref[2])
        for j in range(tm // ATT_BLK):
            rows = slice(j * ATT_BLK, (j + 1) * ATT_BLK)
            for p in range(N_HEADS // 2):
                ka_ref[j, :, 2 * p * LANES:(2 * p + 1) * LANES] = k_tok[rows, p * LANES:(p + 1) * LANES].astype(BF16)
                ka_ref[j, :, (2 * p + 1) * LANES:(2 * p + 2) * LANES] = (
                    slab[rows, p * LANES:(p + 1) * LANES].astype(BF16))
    else:
        for bl in range(nb):
            kt_ref[bl] = kt[:, bl * tt:(bl + 1) * tt]
            vt_ref[bl] = vt[:, bl * tt:(bl + 1) * tt]
            lft_ref[bl] = lft[:, bl * tt:(bl + 1) * tt]
        q_ref[...] = (_dot(h, wq_ref[...]) * scale).astype(BF16)
        vtb_ref[...] = jnp.zeros_like(vtb_ref)
        ka_ref[...] = jnp.zeros_like(ka_ref)


def _inproj(x, shift, scale, g_mix, w, *, prompt, tm):
    b, t, d = x.shape
    n = b * t
    tm = min(tm, n)
    tt = min(t, tm)
    nb = tm // tt
    tpb = t // tt
    assert (nb == 1) == prompt
    grid = (n // tm,)

    def xmap(i):
        return (i // tpb, i % tpb, 0) if nb == 1 else (i, 0, 0)

    def bmap(i):
        return (i // tpb, 0, 0) if nb == 1 else (i, 0, 0)

    def tmap(i):
        return (i // tpb, 0, i % tpb) if nb == 1 else (i, 0, 0)

    full = lambda a: pl.BlockSpec(a.shape, lambda i: (0,) * a.ndim)
    names = ("wu", "wq", "wkt", "wvt", "wft", "bfc", "wk", "pm")
    in_specs = [pl.BlockSpec((nb, tt, d), xmap),
                pl.BlockSpec((nb, 1, d), bmap), pl.BlockSpec((nb, 1, d), bmap),
                pl.BlockSpec((1, d), lambda i: (0, 0))] + [full(w[k]) for k in names]
    tblk = lambda rows: pl.BlockSpec((None if nb == 1 else nb, rows, tt), tmap)
    dummy = (jax.ShapeDtypeStruct((8, LANES), F32), pl.BlockSpec((8, LANES), lambda i: (0, 0)))
    if prompt:
        nblk = n // ATT_BLK
        blocked = (jax.ShapeDtypeStruct((nblk, D_ATT, ATT_BLK), BF16),
                   pl.BlockSpec((tm // ATT_BLK, D_ATT, ATT_BLK), lambda i: (i, 0, 0)))
        q_out, vtb_out = blocked, blocked
        ka_out = (jax.ShapeDtypeStruct((nblk, ATT_BLK, 2 * D_ATT), BF16),
                  pl.BlockSpec((tm // ATT_BLK, ATT_BLK, 2 * D_ATT), lambda i: (i, 0, 0)))
    else:
        q_out = (jax.ShapeDtypeStruct((n, D_ATT), BF16), pl.BlockSpec((tm, D_ATT), lambda i: (i, 0)))
        vtb_out = ka_out = dummy
    kdim = SSM_CHUNK * LANES
    outs = [(jax.ShapeDtypeStruct((N_OCT, n // SSM_CHUNK, kdim), BF16),
             pl.BlockSpec((N_OCT, tm // SSM_CHUNK, kdim), lambda i: (0, i, 0))),
            q_out,
            (jax.ShapeDtypeStruct((b, D_ATT, t), F32), tblk(D_ATT)),
            (jax.ShapeDtypeStruct((b, D_ATT, t), F32), tblk(D_ATT)),
            (jax.ShapeDtypeStruct((b, N_HEADS, t), F32), tblk(N_HEADS)),
            vtb_out, ka_out]
    return pl.pallas_call(
        functools.partial(_inproj_kernel, prompt=prompt, tpb=tpb),
        out_shape=tuple(o[0] for o in outs), grid=grid, in_specs=in_specs, out_specs=tuple(o[1] for o in outs),
        scratch_shapes=[pltpu.VMEM((N_OCT, tm, LANES), F32), pltpu.VMEM((8, LANES), F32)],
        compiler_params=_cparams(("arbitrary",)),
        name="inproj_p" if prompt else "inproj_s",
    )(x, shift, scale, g_mix, *[w[k] for k in names])


def _ssm_operators(log_dt, a_re, a_im, b_re, b_im, c_re, c_im, d_skip):
    L, G, P, C = SSM_CHUNK, N_SSM_GROUPS, SSM_STATE, SSM_GROUP
    q = G // N_OCT
    lam = lax.complex(a_re, a_im)
    dt = jnp.exp(log_dt)[:, None]
    lam_dt = lam * dt
    lam_bar = jnp.exp(lam_dt)
    b_bar = ((lam_bar - 1.0) / lam)[:, :, None] * lax.complex(b_re, b_im)
    c_mat = lax.complex(c_re, c_im)
    pw = jnp.exp(lam_dt[None] * jnp.arange(L + 1, dtype=F32)[:, None, None])
    kern = jnp.real(jnp.einsum('gcp,dgp,gpk->dgck', c_mat, pw[:L], b_bar))
    ksm = jnp.transpose(kern, (1, 0, 3, 2)).reshape(N_OCT, q, L, C, C)
    ksm = jnp.transpose(ksm, (0, 2, 1, 3, 4)).reshape(N_OCT, L, q * C, C)
    pwr = pw[L - 1 - jnp.arange(L)]
    wb = jnp.einsum('jgp,gpk->jgkp', pwr, b_bar)
    wsm = jnp.stack([jnp.real(wb), jnp.imag(wb)], axis=0).reshape(2, L, N_OCT, q * C, P)
    wsm = jnp.transpose(wsm, (2, 1, 0, 3, 4))
    cl = c_mat[None] * pw[1:L + 1][:, :, None, :]
    vsm = jnp.stack([jnp.real(cl), -jnp.imag(cl)], axis=0).reshape(2, L, N_OCT, q, C, P)
    vsm = jnp.transpose(vsm, (2, 1, 0, 3, 5, 4)).reshape(N_OCT, L, OCT_STATE, C)
    tmat, wend, win = _ssm_prep(ksm, wsm, vsm, d_skip.reshape(N_OCT, 1, LANES))
    dec = pw[L].reshape(N_OCT, q * P)
    decay = jnp.concatenate([jnp.real(dec), jnp.imag(dec)], axis=-1).reshape(N_OCT, 1, OCT_STATE)
    return tmat, wend, win, decay.astype(F32)


def _ssm_prep_kernel(ksm_ref, wsm_ref, vsm_ref, d_ref, tmat_ref, wend_ref, win_ref):
    L, C, P = SSM_CHUNK, SSM_GROUP, SSM_STATE
    half = OCT_STATE // 2

    def spread(period, width):
        r = lax.broadcasted_iota(I32, (period, width), 0)
        c = lax.broadcasted_iota(I32, (period, width), 1)
        return jnp.where(c % period == r, 1.0, 0.0)

    def same_group(shape, row_div, row_mod, lane_div):
        r = lax.broadcasted_iota(I32, shape, 0)
        c = lax.broadcasted_iota(I32, shape, 1)
        return (r % row_mod) // row_div == c // lane_div

    e_c = spread(C, LANES)
    e_p = spread(P, half)
    grp = same_group((LANES, LANES), C, LANES, C)
    rr = lax.broadcasted_iota(I32, (LANES, LANES), 0)
    cc = lax.broadcasted_iota(I32, (LANES, LANES), 1)
    lag = []
    for d in range(L):
        blk = jnp.where(grp, _dot_hi(ksm_ref[d], e_c), 0.0)
        if d == 0:
            blk = blk + jnp.where(rr == cc, jnp.broadcast_to(d_ref[...], (LANES, LANES)), 0.0)
        lag.append(blk.astype(BF16))
    zero = jnp.zeros((LANES, LANES), BF16)
    for j in range(L):
        for t in range(L):
            tmat_ref[j * LANES:(j + 1) * LANES, t * LANES:(t + 1) * LANES] = lag[t - j] if t >= j else zero
    grp_e = same_group((LANES, half), C, LANES, P)
    for j in range(L):
        for part in range(2):
            blk = jnp.where(grp_e, _dot_hi(wsm_ref[j, part], e_p), 0.0)
            wend_ref[j * LANES:(j + 1) * LANES, part * half:(part + 1) * half] = blk.astype(BF16)
    grp_i = same_group((OCT_STATE, LANES), P, half, C)
    for t in range(L):
        blk = jnp.where(grp_i, _dot_hi(vsm_ref[t], e_c), 0.0)
        win_ref[:, t * LANES:(t + 1) * LANES] = blk.astype(BF16)


def _ssm_prep(ksm, wsm, vsm, dvec):
    kdim = SSM_CHUNK * LANES
    blk = lambda a: pl.BlockSpec((None,) + a.shape[1:], lambda o: (o,) + (0,) * (a.ndim - 1))
    out = lambda r, c: (jax.ShapeDtypeStruct((N_OCT, r, c), BF16), pl.BlockSpec((None, r, c), lambda o: (o, 0, 0)))
    outs = [out(kdim, kdim), out(kdim, OCT_STATE), out(OCT_STATE, kdim)]
    return pl.pallas_call(
        _ssm_prep_kernel,
        out_shape=tuple(o[0] for o in outs),
        grid=(N_OCT,),
        in_specs=[blk(ksm), blk(wsm), blk(vsm), blk(dvec)],
        out_specs=tuple(o[1] for o in outs),
        compiler_params=_cparams(("arbitrary",)),
        name="ssm_prep",
    )(ksm, wsm, vsm, dvec)


def _ssm_kernel(u_ref, tmat_ref, wend_ref, win_ref, dec_ref, s0_ref, y_ref, sfin_ref, e_ref, sp_ref, st_ref,
                *, nb, cpt):
    i = pl.program_id(1)
    half = OCT_STATE // 2
    rows = nb * cpt

    @pl.when(i == 0)
    def _():
        st_ref[...] = s0_ref[...]

    u = u_ref[...].reshape(rows, u_ref.shape[-1])
    e_ref[...] = _dot(u, wend_ref[...])
    dec = dec_ref[...]
    ar, ai = dec[:, :half], dec[:, half:]

    def advance(row, st):
        sp_ref[pl.ds(row, 1), :] = st
        e = e_ref[pl.ds(row, 1), :]
        re, im = st[:, :half], st[:, half:]
        nre = ar * re - ai * im + e[:, :half]
        nim = ar * im + ai * re + e[:, half:]
        return jnp.concatenate([nre, nim], axis=-1)

    if nb <= SSM_INTERLEAVE:
        def per_chunk(r, sts):
            return tuple(advance(bb * cpt + r, sts[bb]) for bb in range(nb))

        sts = lax.fori_loop(0, cpt, per_chunk, tuple(st_ref[bb:bb + 1, :] for bb in range(nb)))
        for bb in range(nb):
            st_ref[bb:bb + 1, :] = sts[bb]
    else:
        def per_batch(bb, carry):
            st = lax.fori_loop(0, cpt, lambda r, st: advance(bb * cpt + r, st), st_ref[pl.ds(bb, 1), :])
            st_ref[pl.ds(bb, 1), :] = st
            return carry

        lax.fori_loop(0, nb, per_batch, 0)
    y = _dot(u, tmat_ref[...]) + _dot(sp_ref[...].astype(BF16), win_ref[...])
    y_ref[...] = jax.nn.gelu(y).astype(BF16).reshape(y_ref.shape)
    sfin_ref[...] = st_ref[...]


def _ssm(u4, ops, s0, batch, n_chunks):
    tmat, wend, win, decay = ops
    kdim = SSM_CHUNK * LANES
    cpt = SSM_ROWS if n_chunks % SSM_ROWS == 0 else n_chunks
    tiles = n_chunks // cpt
    if tiles > 1:
        u_in = u4.reshape(N_OCT, batch, n_chunks, kdim)
        u_spec = pl.BlockSpec((None, batch, cpt, kdim), lambda o, i: (o, 0, i, 0))
    else:
        u_in = u4
        u_spec = pl.BlockSpec((None, batch * n_chunks, kdim), lambda o, i: (o, 0, 0))
    r = batch * cpt
    wspec = lambda shape: pl.BlockSpec((None,) + shape, lambda o, i: (o, 0, 0))
    y, s_fin = pl.pallas_call(
        functools.partial(_ssm_kernel, nb=batch, cpt=cpt),
        out_shape=(jax.ShapeDtypeStruct(u_in.shape, BF16),
                   jax.ShapeDtypeStruct((N_OCT, batch, OCT_STATE), F32)),
        grid=(N_OCT, tiles),
        in_specs=[u_spec,
                  wspec((kdim, kdim)), wspec((kdim, OCT_STATE)), wspec((OCT_STATE, kdim)), wspec((1, OCT_STATE)),
                  wspec((batch, OCT_STATE))],
        out_specs=(u_spec, wspec((batch, OCT_STATE))),
        scratch_shapes=[pltpu.VMEM((r, OCT_STATE), F32), pltpu.VMEM((r, OCT_STATE), F32),
                        pltpu.VMEM((batch, OCT_STATE), F32)],
        compiler_params=_cparams(("arbitrary", "arbitrary")),
        name="ssm",
    )(u_in, tmat, wend, win, decay, s0)
    return y.reshape(u4.shape), s_fin


def _bias_placement():
    import numpy as np
    pm = np.zeros((3, N_HEADS, N_HEADS // 2 * LANES), np.float32)
    for piece in range(3):
        for h in range(N_HEADS):
            pm[piece, h, LANES * (h // 2) + 3 * (h % 2) + piece] = 1.0
    return jnp.asarray(pm)


def _attn_p_kernel(qt_ref, ka_ref, vt_ref, o_ref, qbd_ref, acc_ref, sa_ref, sb_ref):
    i = pl.program_id(2)
    bq = ATT_BLK
    hd = HEAD_DIM
    r = lax.broadcasted_iota(I32, (LANES, 2 * bq), 0)
    c = lax.broadcasted_iota(I32, (LANES, 2 * bq), 1)
    ones = jnp.where(((r < 3) & (c < bq)) | ((r >= 3) & (r < 6) & (c >= bq)), 1.0, 0.0).astype(BF16)
    zero = jnp.zeros((hd, bq), BF16)
    for pp in range(ATT_PAIRS):
        qb = qt_ref[pp * LANES:(pp + 1) * LANES, :]
        qbd_ref[pp, 0:hd, 0:bq] = qb[0:hd]
        qbd_ref[pp, 0:hd, bq:2 * bq] = zero
        qbd_ref[pp, hd:2 * hd, 0:bq] = zero
        qbd_ref[pp, hd:2 * hd, bq:2 * bq] = qb[hd:2 * hd]
        qbd_ref[pp, 2 * hd:2 * hd + LANES, :] = ones
    acc_ref[...] = jnp.zeros_like(acc_ref)

    nq = ka_ref.shape[0]

    def scores(s_ref, j):
        for pp in range(ATT_PAIRS):
            s_ref[pp] = _dot(ka_ref[j, :, pp * 2 * LANES:(pp + 1) * 2 * LANES], qbd_ref[pp])

    def attend(s_ref, j, carry, masked):
        out = []
        for pp in range(ATT_PAIRS):
            m_prev, l_prev = carry[2 * pp], carry[2 * pp + 1]
            s = s_ref[pp]
            if masked:
                key = j * bq + lax.broadcasted_iota(I32, s.shape, 0)
                qq = lax.broadcasted_iota(I32, s.shape, 1)
                qq = i * bq + jnp.where(qq >= bq, qq - bq, qq)
                s = jnp.where(key <= qq, s, -jnp.inf)
            m_new = jnp.maximum(m_prev, jnp.max(s, axis=0, keepdims=True))
            alpha = jnp.exp2(m_prev - m_new)
            p = jnp.exp2(s - m_new)
            l_new = alpha * l_prev + jnp.sum(p, axis=0, keepdims=True)
            pb = p.astype(BF16)
            vb = vt_ref[jnp.minimum(j, nq - 1), pp * LANES:(pp + 1) * LANES, :]
            r0 = pp * LANES
            acc_ref[r0:r0 + hd] = alpha[:, 0:bq] * acc_ref[r0:r0 + hd] + _dot(vb[0:hd], pb[:, 0:bq])
            acc_ref[r0 + hd:r0 + 2 * hd] = (alpha[:, bq:2 * bq] * acc_ref[r0 + hd:r0 + 2 * hd]
                                            + _dot(vb[hd:2 * hd], pb[:, bq:2 * bq]))
            out.extend((m_new, l_new))
        return tuple(out)

    def double_step(m, carry):
        j = 2 * m
        scores(sb_ref, j + 1)
        carry = attend(sa_ref, j, carry, False)
        scores(sa_ref, j + 2)
        return attend(sb_ref, j + 1, carry, False)

    init = (jnp.full((1, 2 * bq), -jnp.inf, F32), jnp.zeros((1, 2 * bq), F32)) * ATT_PAIRS
    scores(sa_ref, 0)
    carry = lax.fori_loop(0, i // 2, double_step, init)
    j = 2 * (i // 2)
    scores(sb_ref, jnp.minimum(j + 1, nq - 1))
    carry = attend(sa_ref, j, carry, True)
    carry = attend(sb_ref, j + 1, carry, True)
    for pp in range(ATT_PAIRS):
        l = carry[2 * pp + 1]
        r0 = pp * LANES
        o_ref[r0:r0 + hd] = acc_ref[r0:r0 + hd] / l[:, 0:bq]
        o_ref[r0 + hd:r0 + 2 * hd] = acc_ref[r0 + hd:r0 + 2 * hd] / l[:, bq:2 * bq]


def _attn_p(qt, kaug, vt, batch, t):
    blk = ATT_BLK
    nq = t // blk
    ngrp = N_HEADS // 2 // ATT_PAIRS
    rows = ATT_PAIRS * LANES
    qt4 = qt.reshape(batch, nq, D_ATT, blk)
    vt4 = vt.reshape(batch, nq, D_ATT, blk)
    ka4 = kaug.reshape(batch, nq, blk, 2 * D_ATT)
    return pl.pallas_call(
        _attn_p_kernel,
        out_shape=jax.ShapeDtypeStruct((batch, D_ATT, t), F32),
        grid=(batch, ngrp, nq),
        in_specs=[pl.BlockSpec((None, None, rows, blk), lambda b, p, i: (b, i, p, 0)),
                  pl.BlockSpec((None, nq, blk, 2 * rows), lambda b, p, i: (b, 0, 0, p), pipeline_mode=pl.Buffered(1)),
                  pl.BlockSpec((None, nq, rows, blk), lambda b, p, i: (b, 0, p, 0), pipeline_mode=pl.Buffered(1))],
        out_specs=pl.BlockSpec((None, rows, blk), lambda b, p, i: (b, p, i)),
        scratch_shapes=[pltpu.VMEM((ATT_PAIRS, 2 * LANES, 2 * blk), BF16), pltpu.VMEM((rows, blk), F32),
                        pltpu.VMEM((ATT_PAIRS, blk, 2 * blk), F32), pltpu.VMEM((ATT_PAIRS, blk, 2 * blk), F32)],
        compiler_params=_cparams(("arbitrary", "arbitrary", "arbitrary")),
        name="attn_p",
    )(qt4, ka4, vt4)


def _lane_cumsum(x):
    n = x.shape[-1]
    lane = lax.broadcasted_iota(I32, x.shape, x.ndim - 1)
    s = 1
    while s < n:
        x = x + jnp.where(lane >= s, pltpu.roll(x, s, x.ndim - 1), 0.0)
        s *= 2
    return x


def _fprep_s_kernel(cl_ref, ln_ref, fc_ref, fn_ref):
    b, h, p = cl_ref.shape
    cum = _lane_cumsum(cl_ref[...].reshape(b * h, p))
    fc_ref[...] = (-cum).reshape(b, h, p)
    total = cum[:, p - 1:p]
    cn = _lane_cumsum(ln_ref[...].reshape(b * h, LANES))
    fn_ref[...] = (-(total + cn)).reshape(b, h, LANES)


def _fprep_s(cache_lf_t, new_lf_t):
    b, h, p = cache_lf_t.shape
    return pl.pallas_call(
        _fprep_s_kernel,
        out_shape=(jax.ShapeDtypeStruct((b, h, p), F32), jax.ShapeDtypeStruct((b, h, LANES), F32)),
        grid=(1,),
        in_specs=[pl.BlockSpec((b, h, p), lambda i: (0, 0, 0)), pl.BlockSpec((b, h, LANES), lambda i: (0, 0, 0))],
        out_specs=(pl.BlockSpec((b, h, p), lambda i: (0, 0, 0)), pl.BlockSpec((b, h, LANES), lambda i: (0, 0, 0))),
        compiler_params=_cparams(("arbitrary",)),
        name="fprep_s",
    )(cache_lf_t, new_lf_t)


def _attn_s_kernel(q_ref, ck_ref, cv_ref, kn_ref, vn_ref, fc_ref, fn_ref, o_ref, qbd_ref, m_ref, l_ref, acc_ref,
                   *, tq):
    j = pl.program_id(1)
    nkv = pl.num_programs(1)
    rows = N_HEADS * tq
    bk = ck_ref.shape[1]

    @pl.when(j == 0)
    def _():
        q = q_ref[...]
        qrep = jnp.broadcast_to(q[None], (N_HEADS, tq, D_ATT)).reshape(rows, D_ATT)
        rh = lax.broadcasted_iota(I32, (rows, D_ATT), 0) // tq
        ch = lax.broadcasted_iota(I32, (rows, D_ATT), 1) // HEAD_DIM
        qbd_ref[...] = jnp.where(rh == ch, qrep, jnp.zeros_like(qrep))
        m_ref[...] = jnp.full_like(m_ref, -jnp.inf)
        l_ref[...] = jnp.zeros_like(l_ref)
        acc_ref[...] = jnp.zeros_like(acc_ref)

    def update(s, vt):
        m_prev = m_ref[...]
        m_new = jnp.maximum(m_prev, jnp.max(s, axis=1, keepdims=True))
        alpha = jnp.exp(m_prev - m_new)
        p = jnp.exp(s - m_new)
        l_ref[...] = alpha * l_ref[...] + jnp.sum(p, axis=1, keepdims=True)
        m_ref[...] = m_new
        acc_ref[...] = alpha * acc_ref[...] + _dot_nt(p.astype(BF16), vt)

    def bias(f, width):
        return jnp.broadcast_to(f[:, None, :], (N_HEADS, tq, width)).reshape(rows, width)

    s = _dot(qbd_ref[...], ck_ref[...].astype(BF16)) + bias(fc_ref[...], bk)
    update(s, cv_ref[...].astype(BF16))

    @pl.when(j == nkv - 1)
    def _():
        s2 = _dot(qbd_ref[...], kn_ref[...].astype(BF16)) + bias(fn_ref[...][:, 0:tq], tq)
        key = lax.broadcasted_iota(I32, (rows, tq), 1)
        qq = lax.broadcasted_iota(I32, (rows, tq), 0) % tq
        update(jnp.where(key <= qq, s2, -jnp.inf), vn_ref[...].astype(BF16))
        o = acc_ref[...] / l_ref[...]
        rh = lax.broadcasted_iota(I32, (rows, D_ATT), 0) // tq
        ch = lax.broadcasted_iota(I32, (rows, D_ATT), 1) // HEAD_DIM
        o = jnp.where(rh == ch, o, 0.0).reshape(N_HEADS, tq, D_ATT)
        o_ref[...] = jnp.sum(o, axis=0)


def _attn_s(q, cache_k, cache_v, k_new, v_new, fc, fn):
    b, tq, _ = q.shape
    p = cache_k.shape[2]
    bk = min(SAMPLE_KV_BLK, p)
    nkv = p // bk
    rows = N_HEADS * tq
    tok = lambda: pl.BlockSpec((None, tq, D_ATT), lambda bi, j: (bi, 0, 0))
    new = lambda: pl.BlockSpec((None, D_ATT, tq), lambda bi, j: (bi, 0, 0))
    return pl.pallas_call(
        functools.partial(_attn_s_kernel, tq=tq),
        out_shape=jax.ShapeDtypeStruct((b, tq, D_ATT), F32),
        grid=(b, nkv),
        in_specs=[tok(),
                  pl.BlockSpec((None, D_ATT, bk), lambda bi, j: (bi, 0, j)),
                  pl.BlockSpec((None, D_ATT, bk), lambda bi, j: (bi, 0, j)),
                  new(), new(),
                  pl.BlockSpec((None, N_HEADS, bk), lambda bi, j: (bi, 0, j)),
                  pl.BlockSpec((None, N_HEADS, LANES), lambda bi, j: (bi, 0, 0))],
        out_specs=tok(),
        scratch_shapes=[pltpu.VMEM((rows, D_ATT), BF16), pltpu.VMEM((rows, 1), F32),
                        pltpu.VMEM((rows, 1), F32), pltpu.VMEM((rows, D_ATT), F32)],
        compiler_params=_cparams(("arbitrary", "arbitrary")),
        name="attn_s",
    )(q, cache_k, cache_v, k_new, v_new, fc, fn)


def _outproj_kernel(x_ref, y4_ref, att_ref, gt_ref, sh_ref, sc_ref, gf_ref, wglu_ref, bglu_ref, wout_ref,
                    wr_ref, br_ref, cin_ref,
                    x1_ref, h2_ref, idx_ref, gate_ref, rank_ref, cnt_ref, carry_ref, ys_ref, *, att_transposed):
    i = pl.program_id(0)
    nb, tt, d = x_ref.shape
    tm = nb * tt

    @pl.when(i == 0)
    def _():
        carry_ref[...] = cin_ref[...]

    for s in range(SSM_CHUNK):
        for o in range(N_OCT):
            ys_ref[o, pl.ds(s, tm // SSM_CHUNK, stride=SSM_CHUNK), :] = (
                y4_ref[o, :, s * LANES:(s + 1) * LANES].astype(F32))
    ysf = jnp.concatenate([ys_ref[o] for o in range(N_OCT)], axis=-1)
    glu = ysf * jax.nn.sigmoid(_dot(ysf.astype(BF16), wglu_ref[...]) + bglu_ref[...])
    att = att_ref[...]
    if att_transposed:
        att = att.T
    mix = _dot(glu.astype(BF16), wout_ref[0:D_SSM, :]) + _dot(att.astype(BF16), wout_ref[D_SSM:, :])
    x1 = x_ref[...] + gt_ref[...] * mix.reshape(nb, tt, d)
    x1_ref[...] = x1
    h2 = _norm_mod(x1, gf_ref[...], sc_ref[...], sh_ref[...]).reshape(tm, d)
    bits = lax.bitcast_convert_type(h2.astype(BF16).astype(F32), I32)
    h2_ref[...] = (bits[:, d // 2:] & jnp.int32(-65536)) | lax.shift_right_logical(bits[:, :d // 2], 16)

    logits = lax.dot_general(wr_ref[...], h2, (((1,), (1,)), ((), ())), preferred_element_type=F32,
                             precision=lax.Precision.HIGHEST) + br_ref[...]
    sub = lax.broadcasted_iota(I32, logits.shape, 0)
    work = logits
    vals, idxs = [], []
    for _ in range(TOP_K):
        mx = jnp.max(work, axis=0, keepdims=True)
        ix = jnp.min(jnp.where(work == mx, sub, N_EXPERTS), axis=0, keepdims=True)
        vals.append(mx)
        idxs.append(ix)
        work = jnp.where(sub == ix, -jnp.inf, work)
    ex = [jnp.exp(v - vals[0]) for v in vals]
    den = ex[0] + ex[1] + ex[2] + ex[3]
    mh = jnp.where(work == -jnp.inf, 1.0, 0.0)
    r = lax.broadcasted_iota(I32, (tm, tm), 0)
    c = lax.broadcasted_iota(I32, (tm, tm), 1)
    earlier = jnp.where(r < c, 1.0, 0.0).astype(BF16)
    carry = carry_ref[...]
    before = _dot(mh.astype(BF16), earlier) + carry[:, 0:1]
    carry_ref[...] = carry + jnp.sum(mh, axis=1, keepdims=True)
    for kk in range(TOP_K):
        idx_ref[kk:kk + 1, :] = idxs[kk]
        gate_ref[kk:kk + 1, :] = ex[kk] / den
        rk = jnp.sum(jnp.where(sub == idxs[kk], before, 0.0), axis=0, keepdims=True)
        rank_ref[kk:kk + 1, :] = rk.astype(I32)
    cnt_ref[...] = carry_ref[...]


def _outproj(x, y4, att, gt, sh, sc, g_ffn, w, counts_in, *, att_transposed, tm):
    b, t, d = x.shape
    n = b * t
    tm = min(tm, n)
    tt = min(t, tm)
    nb = tm // tt
    tpb = t // tt

    def xmap(i):
        return (i // tpb, i % tpb, 0) if nb == 1 else (i, 0, 0)

    def bmap(i):
        return (i // tpb, 0, 0) if nb == 1 else (i, 0, 0)

    full = lambda shape: pl.BlockSpec(shape, lambda i: (0,) * len(shape))
    if att_transposed:
        att_spec = pl.BlockSpec((None, D_ATT, tm), lambda i: (i // tpb, 0, i % tpb))
    else:
        att_spec = pl.BlockSpec((tm, D_ATT), lambda i: (i, 0))
    choice = lambda dt: (jax.ShapeDtypeStruct((TOP_K, n), dt), pl.BlockSpec((TOP_K, tm), lambda i: (0, i)))
    outs = [(jax.ShapeDtypeStruct((b, t, d), F32), pl.BlockSpec((nb, tt, d), xmap)),
            (jax.ShapeDtypeStruct((n, d // 2), I32), pl.BlockSpec((tm, d // 2), lambda i: (i, 0))),
            choice(I32), choice(F32), choice(I32),
            (jax.ShapeDtypeStruct((N_EXPERTS, LANES), F32), full((N_EXPERTS, LANES)))]
    return pl.pallas_call(
        functools.partial(_outproj_kernel, att_transposed=att_transposed),
        out_shape=tuple(o[0] for o in outs),
        grid=(n // tm,),
        in_specs=[pl.BlockSpec((nb, tt, d), xmap),
                  pl.BlockSpec((N_OCT, tm // SSM_CHUNK, SSM_CHUNK * LANES), lambda i: (0, i, 0)),
                  att_spec,
                  pl.BlockSpec((nb, 1, d), bmap), pl.BlockSpec((nb, 1, d), bmap), pl.BlockSpec((nb, 1, d), bmap),
                  full((1, d)), full((D_SSM, D_SSM)), full((1, D_SSM)), full((d, d)),
                  full((N_EXPERTS, d)), full((N_EXPERTS, 1)), full((N_EXPERTS, LANES))],
        out_specs=tuple(o[1] for o in outs),
        scratch_shapes=[pltpu.VMEM((N_EXPERTS, LANES), F32), pltpu.VMEM((N_OCT, tm, LANES), F32)],
        compiler_params=_cparams(("arbitrary",)),
        name="outproj_t" if att_transposed else "outproj",
    )(x, y4, att, gt, sh, sc, g_ffn, w["wglu"], w["bglu"], w["wout"], w["wr"], w["br"], counts_in)


def _dispatch_kernel(zst_ref, nu_ref, dest_ref, *rest, tm, tiles):
    h_refs = rest[:len(tiles)]
    xs_ref, zero_ref, hbuf_ref, zsem, sems = rest[len(tiles):]
    i = pl.program_id(0)
    n_blocks = xs_ref.shape[0] // MOE_BLK

    @pl.when(i == 0)
    def _():
        zero_ref[...] = jnp.zeros_like(zero_ref)

        def zero_copy(start):
            start = pl.multiple_of(start, MOE_BLK)
            return pltpu.make_async_copy(zero_ref, xs_ref.at[pl.ds(start, MOE_BLK)], zsem)

        for e in range(N_EXPERTS):
            @pl.when(zst_ref[e] >= 0)
            def _():
                zero_copy(jnp.maximum(zst_ref[e], 0)).start()

        def tail_start(j, carry):
            zero_copy(j * MOE_BLK).start()
            return carry

        def tail_wait(j, carry):
            zero_copy(j * MOE_BLK).wait()
            return carry

        lax.fori_loop(nu_ref[0], n_blocks, tail_start, 0)
        for e in range(N_EXPERTS):
            @pl.when(zst_ref[e] >= 0)
            def _():
                zero_copy(jnp.maximum(zst_ref[e], 0)).wait()
        lax.fori_loop(nu_ref[0], n_blocks, tail_wait, 0)

    slot = i % 2

    def scatter_rows(h_ref):
        hbuf_ref[slot] = h_ref[...]

        def issue(t, carry):
            for kk in range(TOP_K):
                dst = dest_ref[0, 0, t * TOP_K + kk]
                pltpu.make_async_copy(hbuf_ref.at[slot, pl.ds(t, 1)], xs_ref.at[pl.ds(dst, 1)],
                                      sems.at[slot]).start(priority=kk % 2)
            return carry

        lax.fori_loop(0, tm, issue, 0, unroll=8)

    first_tile = 0
    for h_ref, n_tiles in zip(h_refs, tiles):
        pl.when((i >= first_tile) & (i < first_tile + n_tiles))(functools.partial(scatter_rows, h_ref))
        first_tile += n_tiles

    def wait_tile(s):
        n_copied = tm * TOP_K
        pltpu.make_async_copy(xs_ref.at[pl.ds(0, n_copied)], xs_ref.at[pl.ds(0, n_copied)], sems.at[s]).wait()

    @pl.when(i > 0)
    def _():
        wait_tile(1 - slot)

    @pl.when(i == pl.num_programs(0) - 1)
    def _():
        wait_tile(slot)


def _dispatch(h2s, dest, zstart, n_used, n_rows, tm):
    d = h2s[0].shape[1]
    dt = h2s[0].dtype
    tiles = tuple(h.shape[0] // tm for h in h2s)
    nt = sum(tiles)
    dest3 = dest.reshape(nt, 1, tm * TOP_K)
    in_specs = [pl.BlockSpec((1, 1, tm * TOP_K), lambda i, *_: (i, 0, 0), memory_space=pltpu.SMEM)]
    first_tile = 0
    for n_tiles in tiles:
        in_specs.append(pl.BlockSpec(
            (tm, d), lambda i, *_, f=first_tile, m=n_tiles: (jnp.clip(i - f, 0, m - 1), 0)))
        first_tile += n_tiles
    return pl.pallas_call(
        functools.partial(_dispatch_kernel, tm=tm, tiles=tiles),
        out_shape=jax.ShapeDtypeStruct((n_rows, d), dt),
        grid_spec=pltpu.PrefetchScalarGridSpec(
            num_scalar_prefetch=2, grid=(nt,),
            in_specs=in_specs,
            out_specs=pl.BlockSpec(memory_space=pl.ANY),
            scratch_shapes=[pltpu.VMEM((MOE_BLK, d), dt), pltpu.VMEM((2, tm, d), dt),
                            pltpu.SemaphoreType.DMA, pltpu.SemaphoreType.DMA((2,))]),
        compiler_params=_cparams(("arbitrary",)),
        name="moe_dispatch",
    )(zstart, n_used, dest3, *h2s)


def _expert_kernel(be_ref, nxt_ref, slot_ref, first_ref, nu_ref, xs_ref, wup_hbm, bup_ref, wdn_hbm, bdn_ref,
                   ys_ref, wup_f32, wdn_f32, wup_bf, wdn_bf, sems):
    i = pl.program_id(0)
    e = be_ref[i]

    def fetch(expert, slot):
        return (pltpu.make_async_copy(wup_hbm.at[expert], wup_f32.at[slot], sems.at[0, slot]),
                pltpu.make_async_copy(wdn_hbm.at[expert], wdn_f32.at[slot], sems.at[1, slot]))

    @pl.when(i < nu_ref[0])
    def _():
        @pl.when(first_ref[i] == 1)
        def _():
            slot = slot_ref[i]

            @pl.when(i == 0)
            def _():
                for copy in fetch(e, slot):
                    copy.start()

            for copy in fetch(e, slot):
                copy.wait()

            @pl.when(nxt_ref[i] >= 0)
            def _():
                for copy in fetch(jnp.maximum(nxt_ref[i], 0), 1 - slot):
                    copy.start()

            wup_bf[...] = wup_f32[slot].astype(BF16)
            wdn_bf[...] = wdn_f32[slot].astype(BF16)

        words = xs_ref[...]
        half = words.shape[1]
        x_lo = lax.bitcast_convert_type(words << 16, F32).astype(BF16)
        x_hi = lax.bitcast_convert_type(words & jnp.int32(-65536), F32).astype(BF16)
        up = _dot(x_lo, wup_bf[0:half, :]) + _dot(x_hi, wup_bf[half:, :]) + bup_ref[...]
        x_glu = jnp.minimum(up[:, :D_FF], SWIGLU_LIMIT)
        x_lin = jnp.clip(up[:, D_FF:], -SWIGLU_LIMIT, SWIGLU_LIMIT)
        act = (x_lin + 1.0) * (x_glu * jax.nn.sigmoid(SWIGLU_ALPHA * x_glu))
        ys_ref[...] = (_dot(act.astype(BF16), wdn_bf[...]) + bdn_ref[...]).astype(ys_ref.dtype)

    @pl.when(i >= nu_ref[0])
    def _():
        ys_ref[...] = jnp.zeros_like(ys_ref)


def _experts(xs, block_e, n_used, has_rows, w_up, b_up, w_down, b_down):
    n_rows, packed = xs.shape
    d = 2 * packed
    nblk = n_rows // MOE_BLK
    blocks = jnp.arange(nblk, dtype=I32)
    used = blocks < n_used[0]
    first = (used & ((blocks == 0) | (block_e != jnp.roll(block_e, 1)))).astype(I32)
    slot = ((jnp.cumsum(first) - 1) % 2).astype(I32)
    experts = jnp.arange(N_EXPERTS, dtype=I32)
    later = jnp.where(has_rows[None, :] & (experts[None, :] > experts[:, None]), experts[None, :], N_EXPERTS)
    nxt_e = jnp.min(later, axis=1)
    nxt_e = jnp.where(nxt_e < N_EXPERTS, nxt_e, -1).astype(I32)
    nxt = jnp.sum(jnp.where(block_e[:, None] == experts[None, :], nxt_e[None, :], 0), axis=1).astype(I32)
    rowmap = lambda i, be, nx, sl, fi, nu: (jnp.minimum(i, nu[0] - 1), 0)
    emap = lambda i, be, nx, sl, fi, nu: (be[i], 0, 0)
    return pl.pallas_call(
        _expert_kernel,
        out_shape=jax.ShapeDtypeStruct((n_rows, d), BF16),
        grid_spec=pltpu.PrefetchScalarGridSpec(
            num_scalar_prefetch=5, grid=(nblk,),
            in_specs=[pl.BlockSpec((MOE_BLK, packed), rowmap),
                      pl.BlockSpec(memory_space=pl.ANY), pl.BlockSpec((None, 1, 2 * D_FF), emap),
                      pl.BlockSpec(memory_space=pl.ANY), pl.BlockSpec((None, 1, d), emap)],
            out_specs=pl.BlockSpec((MOE_BLK, d), lambda i, *_: (i, 0)),
            scratch_shapes=[pltpu.VMEM((2, d, 2 * D_FF), F32), pltpu.VMEM((2, D_FF, d), F32),
                            pltpu.VMEM((d, 2 * D_FF), BF16), pltpu.VMEM((D_FF, d), BF16),
                            pltpu.SemaphoreType.DMA((2, 2))]),
        compiler_params=_cparams(("arbitrary",)),
        name="moe_experts",
    )(block_e, nxt, slot, first, n_used, xs, w_up, b_up, w_down, b_down)


def _combine_kernel(meta_ref, metan_ref, x1_ref, col_ref, gate_ref, gt_ref, gfin_ref, ys_ref, y_ref, ybuf_ref, sems,
                    *, tm):
    i = pl.program_id(0)
    nt = pl.num_programs(0)
    nb, tt, d = x1_ref.shape
    rows = ybuf_ref.shape[1]

    ra = RUN_ALIGN

    def start_copies(m_ref, slot):
        def per_expert(e, carry):
            a = m_ref[0, 0, e]
            n_chunks = m_ref[0, 0, N_EXPERTS + e]
            bo = m_ref[0, 0, 2 * N_EXPERTS + e]

            def per_chunk(c, carry2):
                src = ys_ref.at[pl.ds(pl.multiple_of(a + ra * c, ra), ra)]
                dst = ybuf_ref.at[slot, pl.ds(pl.multiple_of(bo + ra * c, ra), ra)]
                pltpu.make_async_copy(src, dst, sems.at[slot]).start()
                return carry2

            lax.fori_loop(0, n_chunks, per_chunk, 0)
            return carry

        lax.fori_loop(0, N_EXPERTS, per_expert, 0)

    def wait_copies(m_ref, slot):
        total = m_ref[0, 0, 3 * N_EXPERTS]

        def wait_rows(n_rows):
            pltpu.make_async_copy(ys_ref.at[pl.ds(0, n_rows)], ybuf_ref.at[slot, pl.ds(0, n_rows)],
                                  sems.at[slot]).wait()

        lax.fori_loop(0, total // WAIT_GROUP, lambda c, carry: (wait_rows(WAIT_GROUP * ra), carry)[1], 0)
        lax.fori_loop(0, total % WAIT_GROUP, lambda c, carry: (wait_rows(ra), carry)[1], 0)

    @pl.when(i == 0)
    def _():
        ybuf_ref[...] = jnp.zeros_like(ybuf_ref)
        start_copies(meta_ref, 0)

    @pl.when(i + 1 < nt)
    def _():
        start_copies(metan_ref, (i + 1) % 2)

    slot = i % 2
    wait_copies(meta_ref, slot)
    col = col_ref[...]
    gate = gate_ref[...]
    lane = lax.broadcasted_iota(I32, (tm, rows), 1)
    pick = jnp.where(lane == col[:, 0:1], gate[:, 0:1], 0.0)
    for kk in range(1, TOP_K):
        pick = pick + jnp.where(lane == col[:, kk:kk + 1], gate[:, kk:kk + 1], 0.0)
    moe = _dot(pick.astype(BF16), ybuf_ref[slot])
    x2 = x1_ref[...] + gt_ref[...] * moe.reshape(nb, tt, d)
    ms = jnp.mean(x2 * x2, axis=-1, keepdims=True)
    y_ref[...] = x2 * lax.rsqrt(ms + NORM_EPS) * gfin_ref[...]


def _combine_plan(idx, dest, off, before0, tm):
    n = idx.shape[1]
    nt = n // tm
    is_e = idx[None, :, :] == jnp.arange(N_EXPERTS, dtype=I32)[:, None, None]
    cnt = jnp.any(is_e, axis=1).astype(I32).reshape(N_EXPERTS, nt, tm).sum(axis=2).T
    before = before0[None, :] + jnp.cumsum(cnt, axis=0) - cnt
    start = off[None, :] + before
    ra = RUN_ALIGN
    lo = start // ra * ra
    hi = jnp.where(cnt > 0, (start + cnt + ra - 1) // ra * ra, lo)
    span = hi - lo
    boff = jnp.cumsum(span, axis=1) - span
    delta = jnp.repeat((boff - lo).T, tm, axis=1)
    col = dest + jnp.sum(jnp.where(is_e, delta[:, None, :], 0), axis=0)
    total = jnp.sum(span // ra, axis=1, keepdims=True)
    meta = jnp.concatenate([lo, span // ra, boff, jnp.broadcast_to(total, lo.shape)], axis=1).astype(I32)
    return meta.reshape(nt, 1, 4 * N_EXPERTS), col.T.astype(I32)


def _combine(x1, gate, idx, dest, off, before0, ys, gt, g_final, tm):
    b, t, d = x1.shape
    n = b * t
    tm = min(tm, n)
    tt = min(t, tm)
    nb = tm // tt
    tpb = t // tt
    nt = n // tm
    rows = -(-(tm * TOP_K + 2 * (RUN_ALIGN - 1) * N_EXPERTS) // 256) * 256
    meta, col = _combine_plan(idx, dest, off, before0, tm)

    def xmap(i):
        return (i // tpb, i % tpb, 0) if nb == 1 else (i, 0, 0)

    def bmap(i):
        return (i // tpb, 0, 0) if nb == 1 else (i, 0, 0)

    smem = lambda imap: pl.BlockSpec((1, 1, 4 * N_EXPERTS), imap, memory_space=pltpu.SMEM)
    return pl.pallas_call(
        functools.partial(_combine_kernel, tm=tm),
        out_shape=jax.ShapeDtypeStruct((b, t, d), F32),
        grid=(nt,),
        in_specs=[smem(lambda i: (i, 0, 0)), smem(lambda i: (jnp.minimum(i + 1, nt - 1), 0, 0)),
                  pl.BlockSpec((nb, tt, d), xmap),
                  pl.BlockSpec((tm, TOP_K), lambda i: (i, 0)),
                  pl.BlockSpec((tm, TOP_K), lambda i: (i, 0)),
                  pl.BlockSpec((nb, 1, d), bmap),
                  pl.BlockSpec((1, d), lambda i: (0, 0)),
                  pl.BlockSpec(memory_space=pl.ANY)],
        out_specs=pl.BlockSpec((nb, tt, d), xmap),
        scratch_shapes=[pltpu.VMEM((2, rows, d), BF16), pltpu.SemaphoreType.DMA((2,))],
        compiler_params=_cparams(("arbitrary",)),
        name="moe_combine",
    )(meta, meta, x1, col, gate, gt, g_final, ys)


def _moe(streams, counts, g_final, w_up, b_up, w_down, b_down):
    n_total = sum(s["h2"].shape[0] for s in streams)
    cnt = counts[:, 0].astype(I32)
    padded = (cnt + MOE_BLK - 1) // MOE_BLK * MOE_BLK
    pad_end = jnp.cumsum(padded)
    off = (pad_end - padded).astype(I32)
    n_blocks = -(-n_total * TOP_K // MOE_BLK) + N_EXPERTS
    n_rows = n_blocks * MOE_BLK
    starts = jnp.arange(n_blocks, dtype=I32) * MOE_BLK
    block_e = jnp.minimum(jnp.sum((pad_end[None, :] <= starts[:, None]).astype(I32), axis=1), N_EXPERTS - 1)
    n_used = (pad_end[-1:] // MOE_BLK).astype(I32)
    zstart = jnp.where(padded > 0, pad_end - MOE_BLK, -1).astype(I32)
    experts = jnp.arange(N_EXPERTS, dtype=I32)
    dests = []
    for s in streams:
        seg = jnp.sum(jnp.where(s["idx"][None] == experts[:, None, None], off[:, None, None], 0), axis=0)
        dests.append((seg + s["rank"]).astype(I32))
    tm = min([512] + [s["h2"].shape[0] for s in streams])
    dest_rows = jnp.concatenate([dest.T for dest in dests], axis=0)
    xs = _dispatch([s["h2"] for s in streams], dest_rows, zstart, n_used, n_rows, tm)
    ys = _experts(xs, block_e, n_used, padded > 0, w_up, b_up, w_down, b_down)
    return [_combine(s["x1"], s["gate"].T, s["idx"], dest, off, s["before"], ys, s["gt"], g_final, 256)
            for s, dest in zip(streams, dests)]


def _stream(x, mod, w, ssm_ops, s0, cache, params, counts_in):
    b, t, d = x.shape
    n = b * t
    sh_mix, sc_mix, gt_mix, sh_ffn, sc_ffn, gt_ffn = mod
    prompt = cache is None
    tm = 1024 if prompt else 512
    u4, q, kt, vt, lft, vtb, kaug = _inproj(x, sh_mix, sc_mix, params["g_mix"], w, prompt=prompt, tm=tm)
    n_chunks = t // SSM_CHUNK
    y4, s_fin = _ssm(u4, ssm_ops, s0, b, n_chunks)
    if prompt:
        att = _attn_p(q, kaug, vtb, b, t)
    else:
        cache_k, cache_v, cache_lf = cache
        p = cache_k.shape[1]
        ck_t = jnp.transpose(cache_k, (0, 2, 3, 1)).reshape(b, D_ATT, p)
        cv_t = jnp.transpose(cache_v, (0, 2, 3, 1)).reshape(b, D_ATT, p)
        cl_t = jnp.transpose(cache_lf, (0, 2, 1))
        ln_t = jnp.pad(lft, ((0, 0), (0, 0), (0, LANES - t)))
        fc, fn = _fprep_s(cl_t, ln_t)
        att = _attn_s(q.reshape(b, t, D_ATT), ck_t, cv_t, kt, vt, fc, fn).reshape(n, D_ATT)
    x1, h2, idx, gate, rank, counts = _outproj(x, y4, att, gt_mix, sh_ffn, sc_ffn, params["g_ffn"], w, counts_in,
                                               att_transposed=prompt, tm=tm)
    routed = {"x1": x1, "h2": h2, "idx": idx, "gate": gate, "rank": rank, "gt": gt_ffn,
              "before": counts_in[:, 0].astype(I32)}
    heads = lambda a: jnp.transpose(a.reshape(b, N_HEADS, HEAD_DIM, t), (0, 3, 1, 2))[None]
    return routed, counts, heads(kt), heads(vt), jnp.transpose(lft, (0, 2, 1))[None], s_fin


def _state_in(re, im):
    b = re.shape[0]
    s = jnp.concatenate([re.reshape(b, N_OCT, OCT_STATE // 2), im.reshape(b, N_OCT, OCT_STATE // 2)], axis=-1)
    return jnp.transpose(s, (1, 0, 2))


def _state_out(s):
    b = s.shape[1]
    s = jnp.transpose(s, (1, 0, 2))
    half = OCT_STATE // 2
    re = s[:, :, :half].reshape(1, b, N_SSM_GROUPS, SSM_STATE)
    im = s[:, :, half:].reshape(1, b, N_SSM_GROUPS, SSM_STATE)
    return re, im


def kernel(x_prompt, x_sample, c_prompt, c_sample, cache_k, cache_v, cache_logf, state_ssm_re, state_ssm_im, w_ada, b_ada, g_mix, w_in, b_forget, ssm_log_dt, ssm_a_re, ssm_a_im, ssm_b_re, ssm_b_im, ssm_c_re, ssm_c_im, ssm_d, w_glu, b_glu, w_out, g_ffn, w_router, b_router, w_up, b_up, w_down, b_down, g_final):
    assert w_ada.shape[0] == 1, "single-layer trunk"
    d = D_MODEL
    bp, tp, _ = x_prompt.shape
    bs, ts, _ = x_sample.shape
    n_c = bp + bs
    c_all = jnp.pad(jnp.concatenate([c_prompt, c_sample], axis=0), ((0, -n_c % 8), (0, 0)))
    mod = _ada(c_all, w_ada[0], b_ada[0].reshape(1, 6 * d))
    mod_p = [mod[:bp, j * d:(j + 1) * d].reshape(bp, 1, d) for j in range(6)]
    mod_s = [mod[bp:n_c, j * d:(j + 1) * d].reshape(bs, 1, d) for j in range(6)]
    wi = w_in[0]
    o1, o2, o3, o4 = D_SSM, D_SSM + D_ATT, D_SSM + 2 * D_ATT, D_SSM + 3 * D_ATT
    w_q = wi[:, o1:o2].astype(BF16)
    w_k = wi[:, o2:o3].astype(BF16)
    unused = jnp.zeros((8, LANES), BF16)
    base = {"wu": wi[:, :o1].astype(BF16), "wkt": w_k.T, "wvt": wi[:, o3:o4].astype(BF16).T,
            "wft": wi[:, o4:].astype(BF16).T, "bfc": b_forget[0].reshape(N_HEADS, 1), "pm": _bias_placement(),
            "wglu": w_glu[0].astype(BF16), "bglu": b_glu[0].reshape(1, D_SSM), "wout": w_out[0].astype(BF16),
            "wr": w_router[0].T, "br": b_router[0].reshape(N_EXPERTS, 1)}
    w_p = dict(base, wq=w_q.T, wk=w_k)
    w_s = dict(base, wq=w_q, wk=unused)
    params = {"g_mix": g_mix[0].reshape(1, d), "g_ffn": g_ffn[0].reshape(1, d), "g_final": g_final.reshape(1, d),
              "w_up": w_up[0], "b_up": b_up[0].reshape(N_EXPERTS, 1, 2 * D_FF),
              "w_down": w_down[0], "b_down": b_down[0].reshape(N_EXPERTS, 1, d)}
    ssm_ops = _ssm_operators(ssm_log_dt[0], ssm_a_re[0], ssm_a_im[0], ssm_b_re[0], ssm_b_im[0],
                             ssm_c_re[0], ssm_c_im[0], ssm_d[0])
    zero_state = jnp.zeros((N_OCT, bp, OCT_STATE), F32)
    counts0 = jnp.zeros((N_EXPERTS, LANES), F32)
    r_p, counts_p, k_p, v_p, f_p, s_p = _stream(x_prompt, mod_p, w_p, ssm_ops, zero_state, None, params, counts0)
    s0 = _state_in(state_ssm_re[0], state_ssm_im[0])
    r_s, counts, k_s, v_s, f_s, s_s = _stream(x_sample, mod_s, w_s, ssm_ops, s0,
                                              (cache_k[0], cache_v[0], cache_logf[0]), params, counts_p)
    y_p, y_s = _moe([r_p, r_s], counts, params["g_final"],
                    params["w_up"], params["b_up"], params["w_down"], params["b_down"])
    sre_p, sim_p = _state_out(s_p)
    sre_s, sim_s = _state_out(s_s)
    return (y_p, y_s, k_p, v_p, f_p, sre_p, sim_p, k_s, v_s, f_s, sre_s, sim_s)
```

```python
import functools
import math

import jax
import jax.numpy as jnp
from jax import lax
from jax.experimental import pallas as pl
from jax.experimental.pallas import tpu as pltpu

F32 = jnp.float32
BF16 = jnp.bfloat16
I32 = jnp.int32

D_MODEL = 1024
D_SSM = 512
SSM_GROUP = 16
N_SSM_GROUPS = 32
SSM_STATE = 64
D_ATT = 512
HEAD_DIM = 64
N_HEADS = 8
N_EXPERTS = 32
TOP_K = 4
D_FF = 1024
SWIGLU_LIMIT = 7.0
SWIGLU_ALPHA = 1.702
NORM_EPS = 1e-5

LANES = 128
N_OCT = D_SSM // LANES
OCT_STATE = 2 * (N_SSM_GROUPS // N_OCT) * SSM_STATE
SSM_CHUNK = 8
SSM_ROWS = 512
SSM_INTERLEAVE = 4
ATT_BLK = 256
ATT_PAIRS = 4
LOG2E = math.log2(math.e)
SAMPLE_KV_BLK = 2048
MOE_BLK = 256
RUN_ALIGN = 16
WAIT_GROUP = 16
VMEM_LIMIT = 52 * 1024 * 1024


def _cparams(sem, vmem=VMEM_LIMIT):
    return pltpu.CompilerParams(dimension_semantics=sem, vmem_limit_bytes=vmem)


def _dot(a, b):
    return jnp.dot(a, b, preferred_element_type=F32)


def _dot_nt(a, b):
    return lax.dot_general(a, b, (((1,), (1,)), ((), ())), preferred_element_type=F32)


def _dot_hi(a, b):
    return jnp.dot(a, b, preferred_element_type=F32, precision=lax.Precision.HIGHEST)


def _ada_kernel(c_ref, w_ref, b_ref, o_ref):
    c = c_ref[...]
    s = c * jax.nn.sigmoid(c)
    o_ref[...] = _dot_hi(s, w_ref[...]) + b_ref[...]


def _ada(c_all, w_ada, b_ada):
    m, d = c_all.shape
    n = w_ada.shape[1]
    return pl.pallas_call(
        _ada_kernel,
        out_shape=jax.ShapeDtypeStruct((m, n), F32),
        grid=(n // d,),
        in_specs=[pl.BlockSpec((m, d), lambda j: (0, 0)),
                  pl.BlockSpec((d, d), lambda j: (0, j)),
                  pl.BlockSpec((1, d), lambda j: (0, j))],
        out_specs=pl.BlockSpec((m, d), lambda j: (0, j)),
        compiler_params=_cparams(("arbitrary",)),
        name="ada",
    )(c_all, w_ada, b_ada)


def _norm_mod(x, g, scale, shift):
    ms = jnp.mean(x * x, axis=-1, keepdims=True)
    y = x * lax.rsqrt(ms + NORM_EPS) * g
    return y * (1.0 + scale) + shift


def _log_sigmoid(z):
    return jnp.minimum(z, 0.0) - jnp.log1p(jnp.exp(-jnp.abs(z)))


def _inproj_kernel(x_ref, sh_ref, sc_ref, g_ref, wu_ref, wq_ref, wkt_ref, wvt_ref, wft_ref, bfc_ref,
                   wk_ref, pm_ref,
                   u4_ref, q_ref, kt_ref, vt_ref, lft_ref, vtb_ref, ka_ref, us_ref, carry_ref, *, prompt, tpb):
    nb, tt, d = x_ref.shape
    tm = nb * tt
    h = _norm_mod(x_ref[...], g_ref[...], sc_ref[...], sh_ref[...]).reshape(tm, d).astype(BF16)
    u = _dot(h, wu_ref[...])
    for o in range(N_OCT):
        us_ref[o] = u[:, o * LANES:(o + 1) * LANES]
    for s in range(SSM_CHUNK):
        for o in range(N_OCT):
            piece = us_ref[o, pl.ds(s, tm // SSM_CHUNK, stride=SSM_CHUNK), :]
            u4_ref[o, :, s * LANES:(s + 1) * LANES] = piece.astype(BF16)
    kt = _dot_nt(wkt_ref[...], h)
    vt = _dot_nt(wvt_ref[...], h)
    lft = _log_sigmoid(_dot_nt(wft_ref[...], h) + bfc_ref[...])
    scale = HEAD_DIM ** -0.5
    if prompt:
        kt_ref[...] = kt
        vt_ref[...] = vt
        lft_ref[...] = lft
        qt = (_dot_nt(wq_ref[...], h) * (scale * LOG2E)).astype(BF16)
        vtb = vt.astype(BF16)
        for j in range(tm // ATT_BLK):
            q_ref[j] = qt[:, j * ATT_BLK:(j + 1) * ATT_BLK]
            vtb_ref[j] = vtb[:, j * ATT_BLK:(j + 1) * ATT_BLK]
        k_tok = _dot(h, wk_ref[...])

        @pl.when(pl.program_id(0) % tpb == 0)
        def _():
            carry_ref[...] = jnp.zeros_like(carry_ref)

        cum = _lane_cumsum(lft) + carry_ref[:, 0:1]
        carry_ref[...] = jnp.broadcast_to(cum[:, tm - 1:tm], carry_ref.shape)
        nf = cum * (-LOG2E)
        hi = nf.astype(BF16).astype(F32)
        r1 = nf - hi
        mid = r1.astype(BF16).astype(F32)
        lo = (r1 - mid).astype(BF16).astype(F32)
        place = lambda piece, pm: lax.dot_general(piece, pm, (((0,), (0,)), ((), ())), preferred_element_type=F32)
        slab = place(hi, pm_ref[0]) + place(mid, pm_ref[1]) + place(lo, pm_ref[2])
        for j in range(tm // ATT_BLK):
            rows = slice(j * ATT_BLK, (j + 1) * ATT_BLK)
            for p in range(N_HEADS // 2):
                ka_ref[j, :, 2 * p * LANES:(2 * p + 1) * LANES] = k_tok[rows, p * LANES:(p + 1) * LANES].astype(BF16)
                ka_ref[j, :, (2 * p + 1) * LANES:(2 * p + 2) * LANES] = (
                    slab[rows, p * LANES:(p + 1) * LANES].astype(BF16))
    else:
        for bl in range(nb):
            kt_ref[bl] = kt[:, bl * tt:(bl + 1) * tt]
            vt_ref[bl] = vt[:, bl * tt:(bl + 1) * tt]
            lft_ref[bl] = lft[:, bl * tt:(bl + 1) * tt]
        q_ref[...] = (_dot(h, wq_ref[...]) * scale).astype(BF16)
        vtb_ref[...] = jnp.zeros_like(vtb_ref)
        ka_ref[...] = jnp.zeros_like(ka_ref)


def _inproj(x, shift, scale, g_mix, w, *, prompt, tm):
    b, t, d = x.shape
    n = b * t
    tm = min(tm, n)
    tt = min(t, tm)
    nb = tm // tt
    tpb = t // tt
    assert (nb == 1) == prompt
    grid = (n // tm,)

    def xmap(i):
        return (i // tpb, i % tpb, 0) if nb == 1 else (i, 0, 0)

    def bmap(i):
        return (i // tpb, 0, 0) if nb == 1 else (i, 0, 0)

    def tmap(i):
        return (i // tpb, 0, i % tpb) if nb == 1 else (i, 0, 0)

    full = lambda a: pl.BlockSpec(a.shape, lambda i: (0,) * a.ndim)
    names = ("wu", "wq", "wkt", "wvt", "wft", "bfc", "wk", "pm")
    in_specs = [pl.BlockSpec((nb, tt, d), xmap),
                pl.BlockSpec((nb, 1, d), bmap), pl.BlockSpec((nb, 1, d), bmap),
                pl.BlockSpec((1, d), lambda i: (0, 0))] + [full(w[k]) for k in names]
    tblk = lambda rows: pl.BlockSpec((None if nb == 1 else nb, rows, tt), tmap)
    dummy = (jax.ShapeDtypeStruct((8, LANES), F32), pl.BlockSpec((8, LANES), lambda i: (0, 0)))
    if prompt:
        nblk = n // ATT_BLK
        blocked = (jax.ShapeDtypeStruct((nblk, D_ATT, ATT_BLK), BF16),
                   pl.BlockSpec((tm // ATT_BLK, D_ATT, ATT_BLK), lambda i: (i, 0, 0)))
        q_out, vtb_out = blocked, blocked
        ka_out = (jax.ShapeDtypeStruct((nblk, ATT_BLK, 2 * D_ATT), BF16),
                  pl.BlockSpec((tm // ATT_BLK, ATT_BLK, 2 * D_ATT), lambda i: (i, 0, 0)))
    else:
        q_out = (jax.ShapeDtypeStruct((n, D_ATT), BF16), pl.BlockSpec((tm, D_ATT), lambda i: (i, 0)))
        vtb_out = ka_out = dummy
    kdim = SSM_CHUNK * LANES
    outs = [(jax.ShapeDtypeStruct((N_OCT, n // SSM_CHUNK, kdim), BF16),
             pl.BlockSpec((N_OCT, tm // SSM_CHUNK, kdim), lambda i: (0, i, 0))),
            q_out,
            (jax.ShapeDtypeStruct((b, D_ATT, t), F32), tblk(D_ATT)),
            (jax.ShapeDtypeStruct((b, D_ATT, t), F32), tblk(D_ATT)),
            (jax.ShapeDtypeStruct((b, N_HEADS, t), F32), tblk(N_HEADS)),
            vtb_out, ka_out]
    return pl.pallas_call(
        functools.partial(_inproj_kernel, prompt=prompt, tpb=tpb),
        out_shape=tuple(o[0] for o in outs), grid=grid, in_specs=in_specs, out_specs=tuple(o[1] for o in outs),
        scratch_shapes=[pltpu.VMEM((N_OCT, tm, LANES), F32), pltpu.VMEM((8, LANES), F32)],
        compiler_params=_cparams(("arbitrary",)),
        name="inproj_p" if prompt else "inproj_s",
    )(x, shift, scale, g_mix, *[w[k] for k in names])


def _ssm_operators(log_dt, a_re, a_im, b_re, b_im, c_re, c_im, d_skip):
    L, G, P, C = SSM_CHUNK, N_SSM_GROUPS, SSM_STATE, SSM_GROUP
    q = G // N_OCT
    lam = lax.complex(a_re, a_im)
    dt = jnp.exp(log_dt)[:, None]
    lam_dt = lam * dt
    lam_bar = jnp.exp(lam_dt)
    b_bar = ((lam_bar - 1.0) / lam)[:, :, None] * lax.complex(b_re, b_im)
    c_mat = lax.complex(c_re, c_im)
    pw = jnp.exp(lam_dt[None] * jnp.arange(L + 1, dtype=F32)[:, None, None])
    kern = jnp.real(jnp.einsum('gcp,dgp,gpk->dgck', c_mat, pw[:L], b_bar))
    ksm = jnp.transpose(kern, (1, 0, 3, 2)).reshape(N_OCT, q, L, C, C)
    ksm = jnp.transpose(ksm, (0, 2, 1, 3, 4)).reshape(N_OCT, L, q * C, C)
    pwr = pw[L - 1 - jnp.arange(L)]
    wb = jnp.einsum('jgp,gpk->jgkp', pwr, b_bar)
    wsm = jnp.stack([jnp.real(wb), jnp.imag(wb)], axis=0).reshape(2, L, N_OCT, q * C, P)
    wsm = jnp.transpose(wsm, (2, 1, 0, 3, 4))
    cl = c_mat[None] * pw[1:L + 1][:, :, None, :]
    vsm = jnp.stack([jnp.real(cl), -jnp.imag(cl)], axis=0).reshape(2, L, N_OCT, q, C, P)
    vsm = jnp.transpose(vsm, (2, 1, 0, 3, 5, 4)).reshape(N_OCT, L, OCT_STATE, C)
    tmat, wend, win = _ssm_prep(ksm, wsm, vsm, d_skip.reshape(N_OCT, 1, LANES))
    dec = pw[L].reshape(N_OCT, q * P)
    decay = jnp.concatenate([jnp.real(dec), jnp.imag(dec)], axis=-1).reshape(N_OCT, 1, OCT_STATE)
    return tmat, wend, win, decay.astype(F32)


def _ssm_prep_kernel(ksm_ref, wsm_ref, vsm_ref, d_ref, tmat_ref, wend_ref, win_ref):
    L, C, P = SSM_CHUNK, SSM_GROUP, SSM_STATE
    half = OCT_STATE // 2

    def spread(period, width):
        r = lax.broadcasted_iota(I32, (period, width), 0)
        c = lax.broadcasted_iota(I32, (period, width), 1)
        return jnp.where(c % period == r, 1.0, 0.0)

    def same_group(shape, row_div, row_mod, lane_div):
        r = lax.broadcasted_iota(I32, shape, 0)
        c = lax.broadcasted_iota(I32, shape, 1)
        return (r % row_mod) // row_div == c // lane_div

    e_c = spread(C, LANES)
    e_p = spread(P, half)

    def repeat_bf16(x, e):
        return _dot(x.astype(BF16), e.astype(BF16))

    grp = same_group((LANES, LANES), C, LANES, C)
    rr = lax.broadcasted_iota(I32, (LANES, LANES), 0)
    cc = lax.broadcasted_iota(I32, (LANES, LANES), 1)
    lag = []
    for d in range(L):
        if d == 0:
            blk = jnp.where(grp, _dot_hi(ksm_ref[d], e_c), 0.0)
            blk = blk + jnp.where(rr == cc, jnp.broadcast_to(d_ref[...], (LANES, LANES)), 0.0)
        else:
            blk = jnp.where(grp, repeat_bf16(ksm_ref[d], e_c), 0.0)
        lag.append(blk.astype(BF16))
    zero = jnp.zeros((LANES, LANES), BF16)
    for j in range(L):
        for t in range(L):
            tmat_ref[j * LANES:(j + 1) * LANES, t * LANES:(t + 1) * LANES] = lag[t - j] if t >= j else zero
    grp_e = same_group((LANES, half), C, LANES, P)
    for j in range(L):
        for part in range(2):
            blk = jnp.where(grp_e, repeat_bf16(wsm_ref[j, part], e_p), 0.0)
            wend_ref[j * LANES:(j + 1) * LANES, part * half:(part + 1) * half] = blk.astype(BF16)
    grp_i = same_group((OCT_STATE, LANES), P, half, C)
    for t in range(L):
        blk = jnp.where(grp_i, repeat_bf16(vsm_ref[t], e_c), 0.0)
        win_ref[:, t * LANES:(t + 1) * LANES] = blk.astype(BF16)


def _ssm_prep(ksm, wsm, vsm, dvec):
    kdim = SSM_CHUNK * LANES
    blk = lambda a: pl.BlockSpec((None,) + a.shape[1:], lambda o: (o,) + (0,) * (a.ndim - 1))
    out = lambda r, c: (jax.ShapeDtypeStruct((N_OCT, r, c), BF16), pl.BlockSpec((None, r, c), lambda o: (o, 0, 0)))
    outs = [out(kdim, kdim), out(kdim, OCT_STATE), out(OCT_STATE, kdim)]
    return pl.pallas_call(
        _ssm_prep_kernel,
        out_shape=tuple(o[0] for o in outs),
        grid=(N_OCT,),
        in_specs=[blk(ksm), blk(wsm), blk(vsm), blk(dvec)],
        out_specs=tuple(o[1] for o in outs),
        compiler_params=_cparams(("arbitrary",)),
        name="ssm_prep",
    )(ksm, wsm, vsm, dvec)


def _ssm_kernel(u_ref, tmat_ref, wend_ref, win_ref, dec_ref, s0_ref, y_ref, sfin_ref, e_ref, sp_ref, st_ref,
                *, nb, cpt):
    i = pl.program_id(1)
    half = OCT_STATE // 2
    rows = nb * cpt

    @pl.when(i == 0)
    def _():
        st_ref[...] = s0_ref[...]

    u = u_ref[...].reshape(rows, u_ref.shape[-1])
    e_ref[...] = _dot(u, wend_ref[...])
    dec = dec_ref[...]
    ar, ai = dec[:, :half], dec[:, half:]

    def advance(row, st):
        sp_ref[pl.ds(row, 1), :] = st
        e = e_ref[pl.ds(row, 1), :]
        re, im = st[:, :half], st[:, half:]
        nre = ar * re - ai * im + e[:, :half]
        nim = ar * im + ai * re + e[:, half:]
        return jnp.concatenate([nre, nim], axis=-1)

    if nb <= SSM_INTERLEAVE:
        def per_chunk(r, sts):
            return tuple(advance(bb * cpt + r, sts[bb]) for bb in range(nb))

        sts = lax.fori_loop(0, cpt, per_chunk, tuple(st_ref[bb:bb + 1, :] for bb in range(nb)))
        for bb in range(nb):
            st_ref[bb:bb + 1, :] = sts[bb]
    else:
        def per_batch(bb, carry):
            st = lax.fori_loop(0, cpt, lambda r, st: advance(bb * cpt + r, st), st_ref[pl.ds(bb, 1), :])
            st_ref[pl.ds(bb, 1), :] = st
            return carry

        lax.fori_loop(0, nb, per_batch, 0)
    y = _dot(u, tmat_ref[...]) + _dot(sp_ref[...].astype(BF16), win_ref[...])
    y_ref[...] = jax.nn.gelu(y).astype(BF16).reshape(y_ref.shape)
    sfin_ref[...] = st_ref[...]


def _ssm(u4, ops, s0, batch, n_chunks):
    tmat, wend, win, decay = ops
    kdim = SSM_CHUNK * LANES
    cpt = SSM_ROWS if n_chunks % SSM_ROWS == 0 else n_chunks
    tiles = n_chunks // cpt
    if tiles > 1:
        u_in = u4.reshape(N_OCT, batch, n_chunks, kdim)
        u_spec = pl.BlockSpec((None, batch, cpt, kdim), lambda o, i: (o, 0, i, 0))
    else:
        u_in = u4
        u_spec = pl.BlockSpec((None, batch * n_chunks, kdim), lambda o, i: (o, 0, 0))
    r = batch * cpt
    wspec = lambda shape: pl.BlockSpec((None,) + shape, lambda o, i: (o, 0, 0))
    y, s_fin = pl.pallas_call(
        functools.partial(_ssm_kernel, nb=batch, cpt=cpt),
        out_shape=(jax.ShapeDtypeStruct(u_in.shape, BF16),
                   jax.ShapeDtypeStruct((N_OCT, batch, OCT_STATE), F32)),
        grid=(N_OCT, tiles),
        in_specs=[u_spec,
                  wspec((kdim, kdim)), wspec((kdim, OCT_STATE)), wspec((OCT_STATE, kdim)), wspec((1, OCT_STATE)),
                  wspec((batch, OCT_STATE))],
        out_specs=(u_spec, wspec((batch, OCT_STATE))),
        scratch_shapes=[pltpu.VMEM((r, OCT_STATE), F32), pltpu.VMEM((r, OCT_STATE), F32),
                        pltpu.VMEM((batch, OCT_STATE), F32)],
        compiler_params=_cparams(("arbitrary", "arbitrary")),
        name="ssm",
    )(u_in, tmat, wend, win, decay, s0)
    return y.reshape(u4.shape), s_fin


def _bias_placement():
    import numpy as np
    pm = np.zeros((3, N_HEADS, N_HEADS // 2 * LANES), np.float32)
    for piece in range(3):
        for h in range(N_HEADS):
            pm[piece, h, LANES * (h // 2) + 3 * (h % 2) + piece] = 1.0
    return jnp.asarray(pm)


def _attn_p_kernel(qt_ref, ka_ref, vt_ref, o_ref, qbd_ref, acc_ref, sa_ref, sb_ref):
    i = pl.program_id(2)
    bq = ATT_BLK
    hd = HEAD_DIM
    r = lax.broadcasted_iota(I32, (LANES, 2 * bq), 0)
    c = lax.broadcasted_iota(I32, (LANES, 2 * bq), 1)
    ones = jnp.where(((r < 3) & (c < bq)) | ((r >= 3) & (r < 6) & (c >= bq)), 1.0, 0.0).astype(BF16)
    zero = jnp.zeros((hd, bq), BF16)
    for pp in range(ATT_PAIRS):
        qb = qt_ref[pp * LANES:(pp + 1) * LANES, :]
        qbd_ref[pp, 0:hd, 0:bq] = qb[0:hd]
        qbd_ref[pp, 0:hd, bq:2 * bq] = zero
        qbd_ref[pp, hd:2 * hd, 0:bq] = zero
        qbd_ref[pp, hd:2 * hd, bq:2 * bq] = qb[hd:2 * hd]
        qbd_ref[pp, 2 * hd:2 * hd + LANES, :] = ones
    acc_ref[...] = jnp.zeros_like(acc_ref)

    nq = ka_ref.shape[0]

    def scores(s_ref, j):
        for pp in range(ATT_PAIRS):
            s_ref[pp] = _dot(ka_ref[j, :, pp * 2 * LANES:(pp + 1) * 2 * LANES], qbd_ref[pp])

    def attend(s_ref, j, carry, masked):
        out = []
        for pp in range(ATT_PAIRS):
            m_prev, l_prev = carry[2 * pp], carry[2 * pp + 1]
            s = s_ref[pp]
            if masked:
                key = j * bq + lax.broadcasted_iota(I32, s.shape, 0)
                qq = lax.broadcasted_iota(I32, s.shape, 1)
                qq = i * bq + jnp.where(qq >= bq, qq - bq, qq)
                s = jnp.where(key <= qq, s, -jnp.inf)
            m_new = jnp.maximum(m_prev, jnp.max(s, axis=0, keepdims=True))
            alpha = jnp.exp2(m_prev - m_new)
            p = jnp.exp2(s - m_new)
            l_new = alpha * l_prev + jnp.sum(p, axis=0, keepdims=True)
            pb = p.astype(BF16)
            vb = vt_ref[jnp.minimum(j, nq - 1), pp * LANES:(pp + 1) * LANES, :]
            r0 = pp * LANES
            acc_ref[r0:r0 + hd] = alpha[:, 0:bq] * acc_ref[r0:r0 + hd] + _dot(vb[0:hd], pb[:, 0:bq])
            acc_ref[r0 + hd:r0 + 2 * hd] = (alpha[:, bq:2 * bq] * acc_ref[r0 + hd:r0 + 2 * hd]
                                            + _dot(vb[hd:2 * hd], pb[:, bq:2 * bq]))
            out.extend((m_new, l_new))
        return tuple(out)

    def double_step(m, carry):
        j = 2 * m
        scores(sb_ref, j + 1)
        carry = attend(sa_ref, j, carry, False)
        scores(sa_ref, j + 2)
        return attend(sb_ref, j + 1, carry, False)

    init = (jnp.full((1, 2 * bq), -jnp.inf, F32), jnp.zeros((1, 2 * bq), F32)) * ATT_PAIRS
    scores(sa_ref, 0)
    carry = lax.fori_loop(0, i // 2, double_step, init)
    j = 2 * (i // 2)
    scores(sb_ref, jnp.minimum(j + 1, nq - 1))
    carry = attend(sa_ref, j, carry, True)
    carry = attend(sb_ref, j + 1, carry, True)
    for pp in range(ATT_PAIRS):
        l = carry[2 * pp + 1]
        r0 = pp * LANES
        o_ref[r0:r0 + hd] = acc_ref[r0:r0 + hd] / l[:, 0:bq]
        o_ref[r0 + hd:r0 + 2 * hd] = acc_ref[r0 + hd:r0 + 2 * hd] / l[:, bq:2 * bq]


def _attn_p(qt, kaug, vt, batch, t):
    blk = ATT_BLK
    nq = t // blk
    ngrp = N_HEADS // 2 // ATT_PAIRS
    rows = ATT_PAIRS * LANES
    qt4 = qt.reshape(batch, nq, D_ATT, blk)
    vt4 = vt.reshape(batch, nq, D_ATT, blk)
    ka4 = kaug.reshape(batch, nq, blk, 2 * D_ATT)
    return pl.pallas_call(
        _attn_p_kernel,
        out_shape=jax.ShapeDtypeStruct((batch, D_ATT, t), F32),
        grid=(batch, ngrp, nq),
        in_specs=[pl.BlockSpec((None, None, rows, blk), lambda b, p, i: (b, i, p, 0)),
                  pl.BlockSpec((None, nq, blk, 2 * rows), lambda b, p, i: (b, 0, 0, p), pipeline_mode=pl.Buffered(1)),
                  pl.BlockSpec((None, nq, rows, blk), lambda b, p, i: (b, 0, p, 0), pipeline_mode=pl.Buffered(1))],
        out_specs=pl.BlockSpec((None, rows, blk), lambda b, p, i: (b, p, i)),
        scratch_shapes=[pltpu.VMEM((ATT_PAIRS, 2 * LANES, 2 * blk), BF16), pltpu.VMEM((rows, blk), F32),
                        pltpu.VMEM((ATT_PAIRS, blk, 2 * blk), F32), pltpu.VMEM((ATT_PAIRS, blk, 2 * blk), F32)],
        compiler_params=_cparams(("arbitrary", "arbitrary", "arbitrary")),
        name="attn_p",
    )(qt4, ka4, vt4)


def _lane_cumsum(x):
    n = x.shape[-1]
    lane = lax.broadcasted_iota(I32, x.shape, x.ndim - 1)
    s = 1
    while s < n:
        x = x + jnp.where(lane >= s, pltpu.roll(x, s, x.ndim - 1), 0.0)
        s *= 2
    return x


def _fprep_s_kernel(cl_ref, ln_ref, fc_ref, fn_ref):
    b, h, p = cl_ref.shape
    cum = _lane_cumsum(cl_ref[...].reshape(b * h, p))
    fc_ref[...] = (-cum).reshape(b, h, p)
    total = cum[:, p - 1:p]
    cn = _lane_cumsum(ln_ref[...].reshape(b * h, LANES))
    fn_ref[...] = (-(total + cn)).reshape(b, h, LANES)


def _fprep_s(cache_lf_t, new_lf_t):
    b, h, p = cache_lf_t.shape
    return pl.pallas_call(
        _fprep_s_kernel,
        out_shape=(jax.ShapeDtypeStruct((b, h, p), F32), jax.ShapeDtypeStruct((b, h, LANES), F32)),
        grid=(1,),
        in_specs=[pl.BlockSpec((b, h, p), lambda i: (0, 0, 0)), pl.BlockSpec((b, h, LANES), lambda i: (0, 0, 0))],
        out_specs=(pl.BlockSpec((b, h, p), lambda i: (0, 0, 0)), pl.BlockSpec((b, h, LANES), lambda i: (0, 0, 0))),
        compiler_params=_cparams(("arbitrary",)),
        name="fprep_s",
    )(cache_lf_t, new_lf_t)


def _attn_s_kernel(q_ref, ck_ref, cv_ref, kn_ref, vn_ref, fc_ref, fn_ref, o_ref, qbd_ref, m_ref, l_ref, acc_ref,
                   *, tq):
    j = pl.program_id(1)
    nkv = pl.num_programs(1)
    rows = N_HEADS * tq
    bk = ck_ref.shape[1]

    @pl.when(j == 0)
    def _():
        q = q_ref[...]
        qrep = jnp.broadcast_to(q[None], (N_HEADS, tq, D_ATT)).reshape(rows, D_ATT)
        rh = lax.broadcasted_iota(I32, (rows, D_ATT), 0) // tq
        ch = lax.broadcasted_iota(I32, (rows, D_ATT), 1) // HEAD_DIM
        qbd_ref[...] = jnp.where(rh == ch, qrep, jnp.zeros_like(qrep))
        m_ref[...] = jnp.full_like(m_ref, -jnp.inf)
        l_ref[...] = jnp.zeros_like(l_ref)
        acc_ref[...] = jnp.zeros_like(acc_ref)

    def update(s, vt):
        m_prev = m_ref[...]
        m_new = jnp.maximum(m_prev, jnp.max(s, axis=1, keepdims=True))
        alpha = jnp.exp(m_prev - m_new)
        p = jnp.exp(s - m_new)
        l_ref[...] = alpha * l_ref[...] + jnp.sum(p, axis=1, keepdims=True)
        m_ref[...] = m_new
        acc_ref[...] = alpha * acc_ref[...] + _dot_nt(p.astype(BF16), vt)

    def bias(f, width):
        return jnp.broadcast_to(f[:, None, :], (N_HEADS, tq, width)).reshape(rows, width)

    s = _dot(qbd_ref[...], ck_ref[...].astype(BF16)) + bias(fc_ref[...], bk)
    update(s, cv_ref[...].astype(BF16))

    @pl.when(j == nkv - 1)
    def _():
        s2 = _dot(qbd_ref[...], kn_ref[...].astype(BF16)) + bias(fn_ref[...][:, 0:tq], tq)
        key = lax.broadcasted_iota(I32, (rows, tq), 1)
        qq = lax.broadcasted_iota(I32, (rows, tq), 0) % tq
        update(jnp.where(key <= qq, s2, -jnp.inf), vn_ref[...].astype(BF16))
        o = acc_ref[...] / l_ref[...]
        rh = lax.broadcasted_iota(I32, (rows, D_ATT), 0) // tq
        ch = lax.broadcasted_iota(I32, (rows, D_ATT), 1) // HEAD_DIM
        o = jnp.where(rh == ch, o, 0.0).reshape(N_HEADS, tq, D_ATT)
        o_ref[...] = jnp.sum(o, axis=0)


def _attn_s(q, cache_k, cache_v, k_new, v_new, fc, fn):
    b, tq, _ = q.shape
    p = cache_k.shape[2]
    bk = min(SAMPLE_KV_BLK, p)
    nkv = p // bk
    rows = N_HEADS * tq
    tok = lambda: pl.BlockSpec((None, tq, D_ATT), lambda bi, j: (bi, 0, 0))
    new = lambda: pl.BlockSpec((None, D_ATT, tq), lambda bi, j: (bi, 0, 0))
    return pl.pallas_call(
        functools.partial(_attn_s_kernel, tq=tq),
        out_shape=jax.ShapeDtypeStruct((b, tq, D_ATT), F32),
        grid=(b, nkv),
        in_specs=[tok(),
                  pl.BlockSpec((None, D_ATT, bk), lambda bi, j: (bi, 0, j)),
                  pl.BlockSpec((None, D_ATT, bk), lambda bi, j: (bi, 0, j)),
                  new(), new(),
                  pl.BlockSpec((None, N_HEADS, bk), lambda bi, j: (bi, 0, j)),
                  pl.BlockSpec((None, N_HEADS, LANES), lambda bi, j: (bi, 0, 0))],
        out_specs=tok(),
        scratch_shapes=[pltpu.VMEM((rows, D_ATT), BF16), pltpu.VMEM((rows, 1), F32),
                        pltpu.VMEM((rows, 1), F32), pltpu.VMEM((rows, D_ATT), F32)],
        compiler_params=_cparams(("arbitrary", "arbitrary")),
        name="attn_s",
    )(q, cache_k, cache_v, k_new, v_new, fc, fn)


def _outproj_kernel(x_ref, y4_ref, att_ref, gt_ref, sh_ref, sc_ref, gf_ref, wglu_ref, bglu_ref, wout_ref,
                    wr_ref, br_ref, cin_ref,
                    x1_ref, h2_ref, idx_ref, gate_ref, rank_ref, cnt_ref, carry_ref, ys_ref, *, att_transposed):
    i = pl.program_id(0)
    nb, tt, d = x_ref.shape
    tm = nb * tt

    @pl.when(i == 0)
    def _():
        carry_ref[...] = cin_ref[...]

    for s in range(SSM_CHUNK):
        for o in range(N_OCT):
            ys_ref[o, pl.ds(s, tm // SSM_CHUNK, stride=SSM_CHUNK), :] = (
                y4_ref[o, :, s * LANES:(s + 1) * LANES].astype(F32))
    ysf = jnp.concatenate([ys_ref[o] for o in range(N_OCT)], axis=-1)
    glu = ysf * jax.nn.sigmoid(_dot(ysf.astype(BF16), wglu_ref[...]) + bglu_ref[...])
    att = att_ref[...]
    if att_transposed:
        att = att.T
    mix = _dot(glu.astype(BF16), wout_ref[0:D_SSM, :]) + _dot(att.astype(BF16), wout_ref[D_SSM:, :])
    x1 = x_ref[...] + gt_ref[...] * mix.reshape(nb, tt, d)
    x1_ref[...] = x1
    h2 = _norm_mod(x1, gf_ref[...], sc_ref[...], sh_ref[...]).reshape(tm, d)
    bits = lax.bitcast_convert_type(h2.astype(BF16).astype(F32), I32)
    h2_ref[...] = (bits[:, d // 2:] & jnp.int32(-65536)) | lax.shift_right_logical(bits[:, :d // 2], 16)

    logits = lax.dot_general(wr_ref[...], h2, (((1,), (1,)), ((), ())), preferred_element_type=F32,
                             precision=lax.Precision.HIGHEST) + br_ref[...]
    sub = lax.broadcasted_iota(I32, logits.shape, 0)
    work = logits
    vals, idxs = [], []
    for _ in range(TOP_K):
        mx = jnp.max(work, axis=0, keepdims=True)
        ix = jnp.min(jnp.where(work == mx, sub, N_EXPERTS), axis=0, keepdims=True)
        vals.append(mx)
        idxs.append(ix)
        work = jnp.where(sub == ix, -jnp.inf, work)
    ex = [jnp.exp(v - vals[0]) for v in vals]
    den = ex[0] + ex[1] + ex[2] + ex[3]
    mh = jnp.where(work == -jnp.inf, 1.0, 0.0)
    r = lax.broadcasted_iota(I32, (tm, tm), 0)
    c = lax.broadcasted_iota(I32, (tm, tm), 1)
    earlier = jnp.where(r < c, 1.0, 0.0).astype(BF16)
    carry = carry_ref[...]
    before = _dot(mh.astype(BF16), earlier) + carry[:, 0:1]
    carry_ref[...] = carry + jnp.sum(mh, axis=1, keepdims=True)
    for kk in range(TOP_K):
        idx_ref[kk:kk + 1, :] = idxs[kk]
        gate_ref[kk:kk + 1, :] = ex[kk] / den
        rk = jnp.sum(jnp.where(sub == idxs[kk], before, 0.0), axis=0, keepdims=True)
        rank_ref[kk:kk + 1, :] = rk.astype(I32)
    cnt_ref[...] = carry_ref[...]


def _outproj(x, y4, att, gt, sh, sc, g_ffn, w, counts_in, *, att_transposed, tm):
    b, t, d = x.shape
    n = b * t
    tm = min(tm, n)
    tt = min(t, tm)
    nb = tm // tt
    tpb = t // tt

    def xmap(i):
        return (i // tpb, i % tpb, 0) if nb == 1 else (i, 0, 0)

    def bmap(i):
        return (i // tpb, 0, 0) if nb == 1 else (i, 0, 0)

    full = lambda shape: pl.BlockSpec(shape, lambda i: (0,) * len(shape))
    if att_transposed:
        att_spec = pl.BlockSpec((None, D_ATT, tm), lambda i: (i // tpb, 0, i % tpb))
    else:
        att_spec = pl.BlockSpec((tm, D_ATT), lambda i: (i, 0))
    choice = lambda dt: (jax.ShapeDtypeStruct((TOP_K, n), dt), pl.BlockSpec((TOP_K, tm), lambda i: (0, i)))
    outs = [(jax.ShapeDtypeStruct((b, t, d), F32), pl.BlockSpec((nb, tt, d), xmap)),
            (jax.ShapeDtypeStruct((n, d // 2), I32), pl.BlockSpec((tm, d // 2), lambda i: (i, 0))),
            choice(I32), choice(F32), choice(I32),
            (jax.ShapeDtypeStruct((N_EXPERTS, LANES), F32), full((N_EXPERTS, LANES)))]
    return pl.pallas_call(
        functools.partial(_outproj_kernel, att_transposed=att_transposed),
        out_shape=tuple(o[0] for o in outs),
        grid=(n // tm,),
        in_specs=[pl.BlockSpec((nb, tt, d), xmap),
                  pl.BlockSpec((N_OCT, tm // SSM_CHUNK, SSM_CHUNK * LANES), lambda i: (0, i, 0)),
                  att_spec,
                  pl.BlockSpec((nb, 1, d), bmap), pl.BlockSpec((nb, 1, d), bmap), pl.BlockSpec((nb, 1, d), bmap),
                  full((1, d)), full((D_SSM, D_SSM)), full((1, D_SSM)), full((d, d)),
                  full((N_EXPERTS, d)), full((N_EXPERTS, 1)), full((N_EXPERTS, LANES))],
        out_specs=tuple(o[1] for o in outs),
        scratch_shapes=[pltpu.VMEM((N_EXPERTS, LANES), F32), pltpu.VMEM((N_OCT, tm, LANES), F32)],
        compiler_params=_cparams(("arbitrary",)),
        name="outproj_t" if att_transposed else "outproj",
    )(x, y4, att, gt, sh, sc, g_ffn, w["wglu"], w["bglu"], w["wout"], w["wr"], w["br"], counts_in)


def _dispatch_kernel(zst_ref, nu_ref, dest_ref, *rest, tm, tiles):
    h_refs = rest[:len(tiles)]
    xs_ref, zero_ref, hbuf_ref, zsem, sems = rest[len(tiles):]
    i = pl.program_id(0)
    n_blocks = xs_ref.shape[0] // MOE_BLK

    @pl.when(i == 0)
    def _():
        zero_ref[...] = jnp.zeros_like(zero_ref)

        def zero_copy(start):
            start = pl.multiple_of(start, MOE_BLK)
            return pltpu.make_async_copy(zero_ref, xs_ref.at[pl.ds(start, MOE_BLK)], zsem)

        for e in range(N_EXPERTS):
            @pl.when(zst_ref[e] >= 0)
            def _():
                zero_copy(jnp.maximum(zst_ref[e], 0)).start()

        def tail_start(j, carry):
            zero_copy(j * MOE_BLK).start()
            return carry

        def tail_wait(j, carry):
            zero_copy(j * MOE_BLK).wait()
            return carry

        lax.fori_loop(nu_ref[0], n_blocks, tail_start, 0)
        for e in range(N_EXPERTS):
            @pl.when(zst_ref[e] >= 0)
            def _():
                zero_copy(jnp.maximum(zst_ref[e], 0)).wait()
        lax.fori_loop(nu_ref[0], n_blocks, tail_wait, 0)

    slot = i % 2

    def scatter_rows(h_ref):
        hbuf_ref[slot] = h_ref[...]

        def issue(t, carry):
            for kk in range(TOP_K):
                dst = dest_ref[0, 0, t * TOP_K + kk]
                pltpu.make_async_copy(hbuf_ref.at[slot, pl.ds(t, 1)], xs_ref.at[pl.ds(dst, 1)],
                                      sems.at[slot]).start(priority=kk % 2)
            return carry

        lax.fori_loop(0, tm, issue, 0, unroll=8)

    first_tile = 0
    for h_ref, n_tiles in zip(h_refs, tiles):
        pl.when((i >= first_tile) & (i < first_tile + n_tiles))(functools.partial(scatter_rows, h_ref))
        first_tile += n_tiles

    def wait_tile(s):
        n_copied = tm * TOP_K
        pltpu.make_async_copy(xs_ref.at[pl.ds(0, n_copied)], xs_ref.at[pl.ds(0, n_copied)], sems.at[s]).wait()

    @pl.when(i > 0)
    def _():
        wait_tile(1 - slot)

    @pl.when(i == pl.num_programs(0) - 1)
    def _():
        wait_tile(slot)


def _dispatch(h2s, dest, zstart, n_used, n_rows, tm):
    d = h2s[0].shape[1]
    dt = h2s[0].dtype
    tiles = tuple(h.shape[0] // tm for h in h2s)
    nt = sum(tiles)
    dest3 = dest.reshape(nt, 1, tm * TOP_K)
    in_specs = [pl.BlockSpec((1, 1, tm * TOP_K), lambda i, *_: (i, 0, 0), memory_space=pltpu.SMEM)]
    first_tile = 0
    for n_tiles in tiles:
        in_specs.append(pl.BlockSpec(
            (tm, d), lambda i, *_, f=first_tile, m=n_tiles: (jnp.clip(i - f, 0, m - 1), 0)))
        first_tile += n_tiles
    return pl.pallas_call(
        functools.partial(_dispatch_kernel, tm=tm, tiles=tiles),
        out_shape=jax.ShapeDtypeStruct((n_rows, d), dt),
        grid_spec=pltpu.PrefetchScalarGridSpec(
            num_scalar_prefetch=2, grid=(nt,),
            in_specs=in_specs,
            out_specs=pl.BlockSpec(memory_space=pl.ANY),
            scratch_shapes=[pltpu.VMEM((MOE_BLK, d), dt), pltpu.VMEM((2, tm, d), dt),
                            pltpu.SemaphoreType.DMA, pltpu.SemaphoreType.DMA((2,))]),
        compiler_params=_cparams(("arbitrary",)),
        name="moe_dispatch",
    )(zstart, n_used, dest3, *h2s)


def _expert_kernel(be_ref, nxt_ref, slot_ref, first_ref, nu_ref, xs_ref, wup_hbm, bup_ref, wdn_hbm, bdn_ref,
                   ys_ref, wup_f32, wdn_f32, wup_bf, wdn_bf, sems):
    i = pl.program_id(0)
    e = be_ref[i]

    def fetch(expert, slot):
        return (pltpu.make_async_copy(wup_hbm.at[expert], wup_f32.at[slot], sems.at[0, slot]),
                pltpu.make_async_copy(wdn_hbm.at[expert], wdn_f32.at[slot], sems.at[1, slot]))

    @pl.when(i < nu_ref[0])
    def _():
        @pl.when(first_ref[i] == 1)
        def _():
            slot = slot_ref[i]

            @pl.when(i == 0)
            def _():
                for copy in fetch(e, slot):
                    copy.start()

            for copy in fetch(e, slot):
                copy.wait()

            @pl.when(nxt_ref[i] >= 0)
            def _():
                for copy in fetch(jnp.maximum(nxt_ref[i], 0), 1 - slot):
                    copy.start()

            wup_bf[...] = wup_f32[slot].astype(BF16)
            wdn_bf[...] = wdn_f32[slot].astype(BF16)

        words = xs_ref[...]
        half = words.shape[1]
        x_lo = lax.bitcast_convert_type(words << 16, F32).astype(BF16)
        x_hi = lax.bitcast_convert_type(words & jnp.int32(-65536), F32).astype(BF16)
        up = _dot(x_lo, wup_bf[0:half, :]) + _dot(x_hi, wup_bf[half:, :]) + bup_ref[...]
        x_glu = jnp.minimum(up[:, :D_FF], SWIGLU_LIMIT)
        x_lin = jnp.clip(up[:, D_FF:], -SWIGLU_LIMIT, SWIGLU_LIMIT)
        act = (x_lin + 1.0) * (x_glu * jax.nn.sigmoid(SWIGLU_ALPHA * x_glu))
        ys_ref[...] = (_dot(act.astype(BF16), wdn_bf[...]) + bdn_ref[...]).astype(ys_ref.dtype)

    @pl.when(i >= nu_ref[0])
    def _():
        ys_ref[...] = jnp.zeros_like(ys_ref)


def _experts(xs, block_e, n_used, has_rows, w_up, b_up, w_down, b_down):
    n_rows, packed = xs.shape
    d = 2 * packed
    nblk = n_rows // MOE_BLK
    blocks = jnp.arange(nblk, dtype=I32)
    used = blocks < n_used[0]
    first = (used & ((blocks == 0) | (block_e != jnp.roll(block_e, 1)))).astype(I32)
    slot = ((jnp.cumsum(first) - 1) % 2).astype(I32)
    experts = jnp.arange(N_EXPERTS, dtype=I32)
    later = jnp.where(has_rows[None, :] & (experts[None, :] > experts[:, None]), experts[None, :], N_EXPERTS)
    nxt_e = jnp.min(later, axis=1)
    nxt_e = jnp.where(nxt_e < N_EXPERTS, nxt_e, -1).astype(I32)
    nxt = jnp.sum(jnp.where(block_e[:, None] == experts[None, :], nxt_e[None, :], 0), axis=1).astype(I32)
    rowmap = lambda i, be, nx, sl, fi, nu: (jnp.minimum(i, nu[0] - 1), 0)
    emap = lambda i, be, nx, sl, fi, nu: (be[i], 0, 0)
    return pl.pallas_call(
        _expert_kernel,
        out_shape=jax.ShapeDtypeStruct((n_rows, d), BF16),
        grid_spec=pltpu.PrefetchScalarGridSpec(
            num_scalar_prefetch=5, grid=(nblk,),
            in_specs=[pl.BlockSpec((MOE_BLK, packed), rowmap),
                      pl.BlockSpec(memory_space=pl.ANY), pl.BlockSpec((None, 1, 2 * D_FF), emap),
                      pl.BlockSpec(memory_space=pl.ANY), pl.BlockSpec((None, 1, d), emap)],
            out_specs=pl.BlockSpec((MOE_BLK, d), lambda i, *_: (i, 0)),
            scratch_shapes=[pltpu.VMEM((2, d, 2 * D_FF), F32), pltpu.VMEM((2, D_FF, d), F32),
                            pltpu.VMEM((d, 2 * D_FF), BF16), pltpu.VMEM((D_FF, d), BF16),
                            pltpu.SemaphoreType.DMA((2, 2))]),
        compiler_params=_cparams(("arbitrary",)),
        name="moe_experts",
    )(block_e, nxt, slot, first, n_used, xs, w_up, b_up, w_down, b_down)


def _combine_kernel(meta_ref, metan_ref, x1_ref, col_ref, gate_ref, gt_ref, gfin_ref, ys_ref, y_ref, ybuf_ref, sems,
                    *, tm):
    i = pl.program_id(0)
    nt = pl.num_programs(0)
    nb, tt, d = x1_ref.shape
    rows = ybuf_ref.shape[1]

    ra = RUN_ALIGN

    def start_copies(m_ref, slot):
        def per_expert(e, carry):
            a = m_ref[0, 0, e]
            n_chunks = m_ref[0, 0, N_EXPERTS + e]
            bo = m_ref[0, 0, 2 * N_EXPERTS + e]

            def per_chunk(c, carry2):
                src = ys_ref.at[pl.ds(pl.multiple_of(a + ra * c, ra), ra)]
                dst = ybuf_ref.at[slot, pl.ds(pl.multiple_of(bo + ra * c, ra), ra)]
                pltpu.make_async_copy(src, dst, sems.at[slot]).start()
                return carry2

            lax.fori_loop(0, n_chunks, per_chunk, 0)
            return carry

        lax.fori_loop(0, N_EXPERTS, per_expert, 0)

    def wait_copies(m_ref, slot):
        total = m_ref[0, 0, 3 * N_EXPERTS]

        def wait_rows(n_rows):
            pltpu.make_async_copy(ys_ref.at[pl.ds(0, n_rows)], ybuf_ref.at[slot, pl.ds(0, n_rows)],
                                  sems.at[slot]).wait()

        lax.fori_loop(0, total // WAIT_GROUP, lambda c, carry: (wait_rows(WAIT_GROUP * ra), carry)[1], 0)
        lax.fori_loop(0, total % WAIT_GROUP, lambda c, carry: (wait_rows(ra), carry)[1], 0)

    @pl.when(i == 0)
    def _():
        ybuf_ref[...] = jnp.zeros_like(ybuf_ref)
        start_copies(meta_ref, 0)

    @pl.when(i + 1 < nt)
    def _():
        start_copies(metan_ref, (i + 1) % 2)

    slot = i % 2
    wait_copies(meta_ref, slot)
    col = col_ref[...]
    gate = gate_ref[...]
    lane = lax.broadcasted_iota(I32, (tm, rows), 1)
    pick = jnp.where(lane == col[:, 0:1], gate[:, 0:1], 0.0)
    for kk in range(1, TOP_K):
        pick = pick + jnp.where(lane == col[:, kk:kk + 1], gate[:, kk:kk + 1], 0.0)
    moe = _dot(pick.astype(BF16), ybuf_ref[slot])
    x2 = x1_ref[...] + gt_ref[...] * moe.reshape(nb, tt, d)
    ms = jnp.mean(x2 * x2, axis=-1, keepdims=True)
    y_ref[...] = x2 * lax.rsqrt(ms + NORM_EPS) * gfin_ref[...]


def _combine_plan(idx, dest, off, before0, tm):
    n = idx.shape[1]
    nt = n // tm
    is_e = idx[None, :, :] == jnp.arange(N_EXPERTS, dtype=I32)[:, None, None]
    cnt = jnp.any(is_e, axis=1).astype(I32).reshape(N_EXPERTS, nt, tm).sum(axis=2).T
    before = before0[None, :] + jnp.cumsum(cnt, axis=0) - cnt
    start = off[None, :] + before
    ra = RUN_ALIGN
    lo = start // ra * ra
    hi = jnp.where(cnt > 0, (start + cnt + ra - 1) // ra * ra, lo)
    span = hi - lo
    boff = jnp.cumsum(span, axis=1) - span
    delta = jnp.repeat((boff - lo).T, tm, axis=1)
    col = dest + jnp.sum(jnp.where(is_e, delta[:, None, :], 0), axis=0)
    total = jnp.sum(span // ra, axis=1, keepdims=True)
    meta = jnp.concatenate([lo, span // ra, boff, jnp.broadcast_to(total, lo.shape)], axis=1).astype(I32)
    return meta.reshape(nt, 1, 4 * N_EXPERTS), col.T.astype(I32)


def _combine(x1, gate, idx, dest, off, before0, ys, gt, g_final, tm):
    b, t, d = x1.shape
    n = b * t
    tm = min(tm, n)
    tt = min(t, tm)
    nb = tm // tt
    tpb = t // tt
    nt = n // tm
    rows = -(-(tm * TOP_K + 2 * (RUN_ALIGN - 1) * N_EXPERTS) // 256) * 256
    meta, col = _combine_plan(idx, dest, off, before0, tm)

    def xmap(i):
        return (i // tpb, i % tpb, 0) if nb == 1 else (i, 0, 0)

    def bmap(i):
        return (i // tpb, 0, 0) if nb == 1 else (i, 0, 0)

    smem = lambda imap: pl.BlockSpec((1, 1, 4 * N_EXPERTS), imap, memory_space=pltpu.SMEM)
    return pl.pallas_call(
        functools.partial(_combine_kernel, tm=tm),
        out_shape=jax.ShapeDtypeStruct((b, t, d), F32),
        grid=(nt,),
        in_specs=[smem(lambda i: (i, 0, 0)), smem(lambda i: (jnp.minimum(i + 1, nt - 1), 0, 0)),
                  pl.BlockSpec((nb, tt, d), xmap),
                  pl.BlockSpec((tm, TOP_K), lambda i: (i, 0)),
                  pl.BlockSpec((tm, TOP_K), lambda i: (i, 0)),
                  pl.BlockSpec((nb, 1, d), bmap),
                  pl.BlockSpec((1, d), lambda i: (0, 0)),
                  pl.BlockSpec(memory_space=pl.ANY)],
        out_specs=pl.BlockSpec((nb, tt, d), xmap),
        scratch_shapes=[pltpu.VMEM((2, rows, d), BF16), pltpu.SemaphoreType.DMA((2,))],
        compiler_params=_cparams(("arbitrary",)),
        name="moe_combine",
    )(meta, meta, x1, col, gate, gt, g_final, ys)


def _moe(streams, counts, g_final, w_up, b_up, w_down, b_down):
    n_total = sum(s["h2"].shape[0] for s in streams)
    cnt = counts[:, 0].astype(I32)
    padded = (cnt + MOE_BLK - 1) // MOE_BLK * MOE_BLK
    pad_end = jnp.cumsum(padded)
    off = (pad_end - padded).astype(I32)
    n_blocks = -(-n_total * TOP_K // MOE_BLK) + N_EXPERTS
    n_rows = n_blocks * MOE_BLK
    starts = jnp.arange(n_blocks, dtype=I32) * MOE_BLK
    block_e = jnp.minimum(jnp.sum((pad_end[None, :] <= starts[:, None]).astype(I32), axis=1), N_EXPERTS - 1)
    n_used = (pad_end[-1:] // MOE_BLK).astype(I32)
    zstart = jnp.where(padded > 0, pad_end - MOE_BLK, -1).astype(I32)
    experts = jnp.arange(N_EXPERTS, dtype=I32)
    dests = []
    for s in streams:
        seg = jnp.sum(jnp.where(s["idx"][None] == experts[:, None, None], off[:, None, None], 0), axis=0)
        dests.append((seg + s["rank"]).astype(I32))
    tm = min([512] + [s["h2"].shape[0] for s in streams])
    dest_rows = jnp.concatenate([dest.T for dest in dests], axis=0)
    xs = _dispatch([s["h2"] for s in streams], dest_rows, zstart, n_used, n_rows, tm)
    ys = _experts(xs, block_e, n_used, padded > 0, w_up, b_up, w_down, b_down)
    return [_combine(s["x1"], s["gate"].T, s["idx"], dest, off, s["before"], ys, s["gt"], g_final, 256)
            for s, dest in zip(streams, dests)]


def _stream(x, mod, w, ssm_ops, s0, cache, params, counts_in):
    b, t, d = x.shape
    n = b * t
    sh_mix, sc_mix, gt_mix, sh_ffn, sc_ffn, gt_ffn = mod
    prompt = cache is None
    tm = 1024 if prompt else 512
    u4, q, kt, vt, lft, vtb, kaug = _inproj(x, sh_mix, sc_mix, params["g_mix"], w, prompt=prompt, tm=tm)
    n_chunks = t // SSM_CHUNK
    y4, s_fin = _ssm(u4, ssm_ops, s0, b, n_chunks)
    if prompt:
        att = _attn_p(q, kaug, vtb, b, t)
    else:
        cache_k, cache_v, cache_lf = cache
        p = cache_k.shape[1]
        ck_t = jnp.transpose(cache_k, (0, 2, 3, 1)).reshape(b, D_ATT, p)
        cv_t = jnp.transpose(cache_v, (0, 2, 3, 1)).reshape(b, D_ATT, p)
        cl_t = jnp.transpose(cache_lf, (0, 2, 1))
        ln_t = jnp.pad(lft, ((0, 0), (0, 0), (0, LANES - t)))
        fc, fn = _fprep_s(cl_t, ln_t)
        att = _attn_s(q.reshape(b, t, D_ATT), ck_t, cv_t, kt, vt, fc, fn).reshape(n, D_ATT)
    x1, h2, idx, gate, rank, counts = _outproj(x, y4, att, gt_mix, sh_ffn, sc_ffn, params["g_ffn"], w, counts_in,
                                               att_transposed=prompt, tm=tm)
    routed = {"x1": x1, "h2": h2, "idx": idx, "gate": gate, "rank": rank, "gt": gt_ffn,
              "before": counts_in[:, 0].astype(I32)}
    heads = lambda a: jnp.transpose(a.reshape(b, N_HEADS, HEAD_DIM, t), (0, 3, 1, 2))[None]
    return routed, counts, heads(kt), heads(vt), jnp.transpose(lft, (0, 2, 1))[None], s_fin


def _state_in(re, im):
    b = re.shape[0]
    s = jnp.concatenate([re.reshape(b, N_OCT, OCT_STATE // 2), im.reshape(b, N_OCT, OCT_STATE // 2)], axis=-1)
    return jnp.transpose(s, (1, 0, 2))


def _state_out(s):
    b = s.shape[1]
    s = jnp.transpose(s, (1, 0, 2))
    half = OCT_STATE // 2
    re = s[:, :, :half].reshape(1, b, N_SSM_GROUPS, SSM_STATE)
    im = s[:, :, half:].reshape(1, b, N_SSM_GROUPS, SSM_STATE)
    return re, im


def kernel(x_prompt, x_sample, c_prompt, c_sample, cache_k, cache_v, cache_logf, state_ssm_re, state_ssm_im, w_ada, b_ada, g_mix, w_in, b_forget, ssm_log_dt, ssm_a_re, ssm_a_im, ssm_b_re, ssm_b_im, ssm_c_re, ssm_c_im, ssm_d, w_glu, b_glu, w_out, g_ffn, w_router, b_router, w_up, b_up, w_down, b_down, g_final):
    assert w_ada.shape[0] == 1, "single-layer trunk"
    d = D_MODEL
    bp, tp, _ = x_prompt.shape
    bs, ts, _ = x_sample.shape
    n_c = bp + bs
    c_all = jnp.pad(jnp.concatenate([c_prompt, c_sample], axis=0), ((0, -n_c % 8), (0, 0)))
    mod = _ada(c_all, w_ada[0], b_ada[0].reshape(1, 6 * d))
    mod_p = [mod[:bp, j * d:(j + 1) * d].reshape(bp, 1, d) for j in range(6)]
    mod_s = [mod[bp:n_c, j * d:(j + 1) * d].reshape(bs, 1, d) for j in range(6)]
    wi = w_in[0]
    o1, o2, o3, o4 = D_SSM, D_SSM + D_ATT, D_SSM + 2 * D_ATT, D_SSM + 3 * D_ATT
    w_q = wi[:, o1:o2].astype(BF16)
    w_k = wi[:, o2:o3].astype(BF16)
    unused = jnp.zeros((8, LANES), BF16)
    base = {"wu": wi[:, :o1].astype(BF16), "wkt": w_k.T, "wvt": wi[:, o3:o4].astype(BF16).T,
            "wft": wi[:, o4:].astype(BF16).T, "bfc": b_forget[0].reshape(N_HEADS, 1), "pm": _bias_placement(),
            "wglu": w_glu[0].astype(BF16), "bglu": b_glu[0].reshape(1, D_SSM), "wout": w_out[0].astype(BF16),
            "wr": w_router[0].T, "br": b_router[0].reshape(N_EXPERTS, 1)}
    w_p = dict(base, wq=w_q.T, wk=w_k)
    w_s = dict(base, wq=w_q, wk=unused)
    params = {"g_mix": g_mix[0].reshape(1, d), "g_ffn": g_ffn[0].reshape(1, d), "g_final": g_final.reshape(1, d),
              "w_up": w_up[0], "b_up": b_up[0].reshape(N_EXPERTS, 1, 2 * D_FF),
              "w_down": w_down[0], "b_down": b_down[0].reshape(N_EXPERTS, 1, d)}
    ssm_ops = _ssm_operators(ssm_log_dt[0], ssm_a_re[0], ssm_a_im[0], ssm_b_re[0], ssm_b_im[0],
                             ssm_c_re[0], ssm_c_im[0], ssm_d[0])
    zero_state = jnp.zeros((N_OCT, bp, OCT_STATE), F32)
    counts0 = jnp.zeros((N_EXPERTS, LANES), F32)
    r_p, counts_p, k_p, v_p, f_p, s_p = _stream(x_prompt, mod_p, w_p, ssm_ops, zero_state, None, params, counts0)
    s0 = _state_in(state_ssm_re[0], state_ssm_im[0])
    r_s, counts, k_s, v_s, f_s, s_s = _stream(x_sample, mod_s, w_s, ssm_ops, s0,
                                              (cache_k[0], cache_v[0], cache_logf[0]), params, counts_p)
    y_p, y_s = _moe([r_p, r_s], counts, params["g_final"],
                    params["w_up"], params["b_up"], params["w_down"], params["b_down"])
    sre_p, sim_p = _state_out(s_p)
    sre_s, sim_s = _state_out(s_s)
    return (y_p, y_s, k_p, v_p, f_p, sre_p, sim_p, k_s, v_s, f_s, sre_s, sim_s)
```

```python
import functools
import math

import jax
import jax.numpy as jnp
from jax import lax
from jax.experimental import pallas as pl
from jax.experimental.pallas import tpu as pltpu

F32 = jnp.float32
BF16 = jnp.bfloat16
I32 = jnp.int32

D_MODEL = 1024
D_SSM = 512
SSM_GROUP = 16
N_SSM_GROUPS = 32
SSM_STATE = 64
D_ATT = 512
HEAD_DIM = 64
N_HEADS = 8
N_EXPERTS = 32
TOP_K = 4
D_FF = 1024
SWIGLU_LIMIT = 7.0
SWIGLU_ALPHA = 1.702
NORM_EPS = 1e-5

LANES = 128
N_OCT = D_SSM // LANES
OCT_STATE = 2 * (N_SSM_GROUPS // N_OCT) * SSM_STATE
SSM_CHUNK = 8
SSM_ROWS = 512
SSM_INTERLEAVE = 4
ATT_BLK = 256
ATT_PAIRS = 4
LOG2E = math.log2(math.e)
SAMPLE_KV_BLK = 2048
MOE_BLK = 256
RUN_ALIGN = 16
WAIT_GROUP = 16
VMEM_LIMIT = 52 * 1024 * 1024


def _cparams(sem, vmem=VMEM_LIMIT):
    return pltpu.CompilerParams(dimension_semantics=sem, vmem_limit_bytes=vmem)


def _dot(a, b):
    return jnp.dot(a, b, preferred_element_type=F32)


def _dot_nt(a, b):
    return lax.dot_general(a, b, (((1,), (1,)), ((), ())), preferred_element_type=F32)


def _dot_hi(a, b):
    return jnp.dot(a, b, preferred_element_type=F32, precision=lax.Precision.HIGHEST)


def _ada_kernel(c_ref, w_ref, b_ref, o_ref):
    c = c_ref[...]
    s = c * jax.nn.sigmoid(c)
    o_ref[...] = _dot_hi(s, w_ref[...]) + b_ref[...]


def _ada(c_all, w_ada, b_ada):
    m, d = c_all.shape
    n = w_ada.shape[1]
    return pl.pallas_call(
        _ada_kernel,
        out_shape=jax.ShapeDtypeStruct((m, n), F32),
        grid=(n // d,),
        in_specs=[pl.BlockSpec((m, d), lambda j: (0, 0)),
                  pl.BlockSpec((d, d), lambda j: (0, j)),
                  pl.BlockSpec((1, d), lambda j: (0, j))],
        out_specs=pl.BlockSpec((m, d), lambda j: (0, j)),
        compiler_params=_cparams(("arbitrary",)),
        name="ada",
    )(c_all, w_ada, b_ada)


def _norm_mod(x, g, scale, shift):
    ms = jnp.mean(x * x, axis=-1, keepdims=True)
    y = x * lax.rsqrt(ms + NORM_EPS) * g
    return y * (1.0 + scale) + shift


def _log_sigmoid(z):
    return jnp.minimum(z, 0.0) - jnp.log1p(jnp.exp(-jnp.abs(z)))


def _inproj_kernel(x_ref, sh_ref, sc_ref, g_ref, wu_ref, wq_ref, wkt_ref, wvt_ref, wft_ref, bfc_ref,
                   wk_ref, pm_ref,
                   u4_ref, q_ref, kt_ref, vt_ref, lft_ref, vtb_ref, ka_ref, us_ref, carry_ref, *, prompt, tpb):
    nb, tt, d = x_ref.shape
    tm = nb * tt
    h = _norm_mod(x_ref[...], g_ref[...], sc_ref[...], sh_ref[...]).reshape(tm, d).astype(BF16)
    u = _dot(h, wu_ref[...])
    for o in range(N_OCT):
        us_ref[o] = u[:, o * LANES:(o + 1) * LANES]
    for s in range(SSM_CHUNK):
        for o in range(N_OCT):
            piece = us_ref[o, pl.ds(s, tm // SSM_CHUNK, stride=SSM_CHUNK), :]
            u4_ref[o, :, s * LANES:(s + 1) * LANES] = piece.astype(BF16)
    kt = _dot_nt(wkt_ref[...], h)
    vt = _dot_nt(wvt_ref[...], h)
    lft = _log_sigmoid(_dot_nt(wft_ref[...], h) + bfc_ref[...])
    scale = HEAD_DIM ** -0.5
    if prompt:
        kt_ref[...] = kt
        vt_ref[...] = vt
        lft_ref[...] = lft
        qt = (_dot_nt(wq_ref[...], h) * (scale * LOG2E)).astype(BF16)
        vtb = vt.astype(BF16)
        for j in range(tm // ATT_BLK):
            q_ref[j] = qt[:, j * ATT_BLK:(j + 1) * ATT_BLK]
            vtb_ref[j] = vtb[:, j * ATT_BLK:(j + 1) * ATT_BLK]
        k_tok = _dot(h, wk_ref[...])

        @pl.when(pl.program_id(0) % tpb == 0)
        def _():
            carry_ref[...] = jnp.zeros_like(carry_ref)

        cum = _lane_cumsum(lft) + carry_ref[:, 0:1]
        carry_ref[...] = jnp.broadcast_to(cum[:, tm - 1:tm], carry_ref.shape)
        nf = cum * (-LOG2E)
        hi = nf.astype(BF16).astype(F32)
        r1 = nf - hi
        mid = r1.astype(BF16).astype(F32)
        lo = (r1 - mid).astype(BF16).astype(F32)
        place = lambda piece, pm: lax.dot_general(piece, pm, (((0,), (0,)), ((), ())), preferred_element_type=F32)
        slab = place(hi, pm_ref[0]) + place(mid, pm_ref[1]) + place(lo, pm_ref[2])
        for j in range(tm // ATT_BLK):
            rows = slice(j * ATT_BLK, (j + 1) * ATT_BLK)
            for p in range(N_HEADS // 2):
                ka_ref[j, :, 2 * p * LANES:(2 * p + 1) * LANES] = k_tok[rows, p * LANES:(p + 1) * LANES].astype(BF16)
                ka_ref[j, :, (2 * p + 1) * LANES:(2 * p + 2) * LANES] = (
                    slab[rows, p * LANES:(p + 1) * LANES].astype(BF16))
    else:
        for bl in range(nb):
            kt_ref[bl] = kt[:, bl * tt:(bl + 1) * tt]
            vt_ref[bl] = vt[:, bl * tt:(bl + 1) * tt]
            lft_ref[bl] = lft[:, bl * tt:(bl + 1) * tt]
        q_ref[...] = (_dot(h, wq_ref[...]) * scale).astype(BF16)
        vtb_ref[...] = jnp.zeros_like(vtb_ref)
        ka_ref[...] = jnp.zeros_like(ka_ref)


def _inproj(x, shift, scale, g_mix, w, *, prompt, tm):
    b, t, d = x.shape
    n = b * t
    tm = min(tm, n)
    tt = min(t, tm)
    nb = tm // tt
    tpb = t // tt
    assert (nb == 1) == prompt
    grid = (n // tm,)

    def xmap(i):
        return (i // tpb, i % tpb, 0) if nb == 1 else (i, 0, 0)

    def bmap(i):
        return (i // tpb, 0, 0) if nb == 1 else (i, 0, 0)

    def tmap(i):
        return (i // tpb, 0, i % tpb) if nb == 1 else (i, 0, 0)

    full = lambda a: pl.BlockSpec(a.shape, lambda i: (0,) * a.ndim)
    names = ("wu", "wq", "wkt", "wvt", "wft", "bfc", "wk", "pm")
    in_specs = [pl.BlockSpec((nb, tt, d), xmap),
                pl.BlockSpec((nb, 1, d), bmap), pl.BlockSpec((nb, 1, d), bmap),
                pl.BlockSpec((1, d), lambda i: (0, 0))] + [full(w[k]) for k in names]
    tblk = lambda rows: pl.BlockSpec((None if nb == 1 else nb, rows, tt), tmap)
    dummy = (jax.ShapeDtypeStruct((8, LANES), F32), pl.BlockSpec((8, LANES), lambda i: (0, 0)))
    if prompt:
        nblk = n // ATT_BLK
        blocked = (jax.ShapeDtypeStruct((nblk, D_ATT, ATT_BLK), BF16),
                   pl.BlockSpec((tm // ATT_BLK, D_ATT, ATT_BLK), lambda i: (i, 0, 0)))
        q_out, vtb_out = blocked, blocked
        ka_out = (jax.ShapeDtypeStruct((nblk, ATT_BLK, 2 * D_ATT), BF16),
                  pl.BlockSpec((tm // ATT_BLK, ATT_BLK, 2 * D_ATT), lambda i: (i, 0, 0)))
    else:
        q_out = (jax.ShapeDtypeStruct((n, D_ATT), BF16), pl.BlockSpec((tm, D_ATT), lambda i: (i, 0)))
        vtb_out = ka_out = dummy
    kdim = SSM_CHUNK * LANES
    outs = [(jax.ShapeDtypeStruct((N_OCT, n // SSM_CHUNK, kdim), BF16),
             pl.BlockSpec((N_OCT, tm // SSM_CHUNK, kdim), lambda i: (0, i, 0))),
            q_out,
            (jax.ShapeDtypeStruct((b, D_ATT, t), F32), tblk(D_ATT)),
            (jax.ShapeDtypeStruct((b, D_ATT, t), F32), tblk(D_ATT)),
            (jax.ShapeDtypeStruct((b, N_HEADS, t), F32), tblk(N_HEADS)),
            vtb_out, ka_out]
    return pl.pallas_call(
        functools.partial(_inproj_kernel, prompt=prompt, tpb=tpb),
        out_shape=tuple(o[0] for o in outs), grid=grid, in_specs=in_specs, out_specs=tuple(o[1] for o in outs),
        scratch_shapes=[pltpu.VMEM((N_OCT, tm, LANES), F32), pltpu.VMEM((8, LANES), F32)],
        compiler_params=_cparams(("arbitrary",)),
        name="inproj_p" if prompt else "inproj_s",
    )(x, shift, scale, g_mix, *[w[k] for k in names])


def _ssm_operators(log_dt, a_re, a_im, b_re, b_im, c_re, c_im, d_skip):
    L, G, P, C = SSM_CHUNK, N_SSM_GROUPS, SSM_STATE, SSM_GROUP
    q = G // N_OCT
    lam = lax.complex(a_re, a_im)
    dt = jnp.exp(log_dt)[:, None]
    lam_dt = lam * dt
    lam_bar = jnp.exp(lam_dt)
    b_bar = ((lam_bar - 1.0) / lam)[:, :, None] * lax.complex(b_re, b_im)
    c_mat = lax.complex(c_re, c_im)
    pw = jnp.exp(lam_dt[None] * jnp.arange(L + 1, dtype=F32)[:, None, None])
    kern = jnp.real(jnp.einsum('gcp,dgp,gpk->dgck', c_mat, pw[:L], b_bar))
    ksm = jnp.transpose(kern, (1, 0, 3, 2)).reshape(N_OCT, q, L, C, C)
    ksm = jnp.transpose(ksm, (0, 2, 1, 3, 4)).reshape(N_OCT, L, q * C, C)
    pwr = pw[L - 1 - jnp.arange(L)]
    wb = jnp.einsum('jgp,gpk->jgkp', pwr, b_bar)
    wsm = jnp.stack([jnp.real(wb), jnp.imag(wb)], axis=0).reshape(2, L, N_OCT, q * C, P)
    wsm = jnp.transpose(wsm, (2, 1, 0, 3, 4))
    cl = c_mat[None] * pw[1:L + 1][:, :, None, :]
    vsm = jnp.stack([jnp.real(cl), -jnp.imag(cl)], axis=0).reshape(2, L, N_OCT, q, C, P)
    vsm = jnp.transpose(vsm, (2, 1, 0, 3, 5, 4)).reshape(N_OCT, L, OCT_STATE, C)
    tmat, wend, win = _ssm_prep(ksm, wsm, vsm, d_skip.reshape(N_OCT, 1, LANES))
    dec = pw[L].reshape(N_OCT, q * P)
    decay = jnp.concatenate([jnp.real(dec), jnp.imag(dec)], axis=-1).reshape(N_OCT, 1, OCT_STATE)
    return tmat, wend, win, decay.astype(F32)


def _ssm_prep_kernel(ksm_ref, wsm_ref, vsm_ref, d_ref, tmat_ref, wend_ref, win_ref):
    L, C, P = SSM_CHUNK, SSM_GROUP, SSM_STATE
    half = OCT_STATE // 2

    def spread(period, width):
        r = lax.broadcasted_iota(I32, (period, width), 0)
        c = lax.broadcasted_iota(I32, (period, width), 1)
        return jnp.where(c % period == r, 1.0, 0.0)

    def same_group(shape, row_div, row_mod, lane_div):
        r = lax.broadcasted_iota(I32, shape, 0)
        c = lax.broadcasted_iota(I32, shape, 1)
        return (r % row_mod) // row_div == c // lane_div

    e_c = spread(C, LANES)
    e_p = spread(P, half)

    def repeat_bf16(x, e):
        return _dot(x.astype(BF16), e.astype(BF16))

    grp = same_group((LANES, LANES), C, LANES, C)
    rr = lax.broadcasted_iota(I32, (LANES, LANES), 0)
    cc = lax.broadcasted_iota(I32, (LANES, LANES), 1)
    lag = []
    for d in range(L):
        if d == 0:
            blk = jnp.where(grp, _dot_hi(ksm_ref[d], e_c), 0.0)
            blk = blk + jnp.where(rr == cc, jnp.broadcast_to(d_ref[...], (LANES, LANES)), 0.0)
        else:
            blk = jnp.where(grp, repeat_bf16(ksm_ref[d], e_c), 0.0)
        lag.append(blk.astype(BF16))
    zero = jnp.zeros((LANES, LANES), BF16)
    for j in range(L):
        for t in range(L):
            tmat_ref[j * LANES:(j + 1) * LANES, t * LANES:(t + 1) * LANES] = lag[t - j] if t >= j else zero
    grp_e = same_group((LANES, half), C, LANES, P)
    for j in range(L):
        for part in range(2):
            blk = jnp.where(grp_e, repeat_bf16(wsm_ref[j, part], e_p), 0.0)
            wend_ref[j * LANES:(j + 1) * LANES, part * half:(part + 1) * half] = blk.astype(BF16)
    grp_i = same_group((OCT_STATE, LANES), P, half, C)
    for t in range(L):
        blk = jnp.where(grp_i, repeat_bf16(vsm_ref[t], e_c), 0.0)
        win_ref[:, t * LANES:(t + 1) * LANES] = blk.astype(BF16)


def _ssm_prep(ksm, wsm, vsm, dvec):
    kdim = SSM_CHUNK * LANES
    blk = lambda a: pl.BlockSpec((None,) + a.shape[1:], lambda o: (o,) + (0,) * (a.ndim - 1))
    out = lambda r, c: (jax.ShapeDtypeStruct((N_OCT, r, c), BF16), pl.BlockSpec((None, r, c), lambda o: (o, 0, 0)))
    outs = [out(kdim, kdim), out(kdim, OCT_STATE), out(OCT_STATE, kdim)]
    return pl.pallas_call(
        _ssm_prep_kernel,
        out_shape=tuple(o[0] for o in outs),
        grid=(N_OCT,),
        in_specs=[blk(ksm), blk(wsm), blk(vsm), blk(dvec)],
        out_specs=tuple(o[1] for o in outs),
        compiler_params=_cparams(("arbitrary",)),
        name="ssm_prep",
    )(ksm, wsm, vsm, dvec)


def _ssm_kernel(u_ref, tmat_ref, wend_ref, win_ref, dec_ref, s0_ref, y_ref, sfin_ref, e_ref, sp_ref, st_ref,
                *, nb, cpt):
    i = pl.program_id(1)
    half = OCT_STATE // 2
    rows = nb * cpt

    @pl.when(i == 0)
    def _():
        st_ref[...] = s0_ref[...]

    u = u_ref[...].reshape(rows, u_ref.shape[-1])
    e_ref[...] = _dot(u, wend_ref[...])
    dec = dec_ref[...]
    ar, ai = dec[:, :half], dec[:, half:]

    def advance(row, st):
        sp_ref[pl.ds(row, 1), :] = st
        e = e_ref[pl.ds(row, 1), :]
        re, im = st[:, :half], st[:, half:]
        nre = ar * re - ai * im + e[:, :half]
        nim = ar * im + ai * re + e[:, half:]
        return jnp.concatenate([nre, nim], axis=-1)

    if nb <= SSM_INTERLEAVE:
        def per_chunk(r, sts):
            return tuple(advance(bb * cpt + r, sts[bb]) for bb in range(nb))

        sts = lax.fori_loop(0, cpt, per_chunk, tuple(st_ref[bb:bb + 1, :] for bb in range(nb)))
        for bb in range(nb):
            st_ref[bb:bb + 1, :] = sts[bb]
    else:
        def per_batch(bb, carry):
            st = lax.fori_loop(0, cpt, lambda r, st: advance(bb * cpt + r, st), st_ref[pl.ds(bb, 1), :])
            st_ref[pl.ds(bb, 1), :] = st
            return carry

        lax.fori_loop(0, nb, per_batch, 0)
    y = _dot(u, tmat_ref[...]) + _dot(sp_ref[...].astype(BF16), win_ref[...])
    y_ref[...] = jax.nn.gelu(y).astype(BF16).reshape(y_ref.shape)
    sfin_ref[...] = st_ref[...]


def _ssm(u4, ops, s0, batch, n_chunks):
    tmat, wend, win, decay = ops
    kdim = SSM_CHUNK * LANES
    cpt = SSM_ROWS if n_chunks % SSM_ROWS == 0 else n_chunks
    tiles = n_chunks // cpt
    if tiles > 1:
        u_in = u4.reshape(N_OCT, batch, n_chunks, kdim)
        u_spec = pl.BlockSpec((None, batch, cpt, kdim), lambda o, i: (o, 0, i, 0))
    else:
        u_in = u4
        u_spec = pl.BlockSpec((None, batch * n_chunks, kdim), lambda o, i: (o, 0, 0))
    r = batch * cpt
    wspec = lambda shape: pl.BlockSpec((None,) + shape, lambda o, i: (o, 0, 0))
    y, s_fin = pl.pallas_call(
        functools.partial(_ssm_kernel, nb=batch, cpt=cpt),
        out_shape=(jax.ShapeDtypeStruct(u_in.shape, BF16),
                   jax.ShapeDtypeStruct((N_OCT, batch, OCT_STATE), F32)),
        grid=(N_OCT, tiles),
        in_specs=[u_spec,
                  wspec((kdim, kdim)), wspec((kdim, OCT_STATE)), wspec((OCT_STATE, kdim)), wspec((1, OCT_STATE)),
                  wspec((batch, OCT_STATE))],
        out_specs=(u_spec, wspec((batch, OCT_STATE))),
        scratch_shapes=[pltpu.VMEM((r, OCT_STATE), F32), pltpu.VMEM((r, OCT_STATE), F32),
                        pltpu.VMEM((batch, OCT_STATE), F32)],
        compiler_params=_cparams(("arbitrary", "arbitrary")),
        name="ssm",
    )(u_in, tmat, wend, win, decay, s0)
    return y.reshape(u4.shape), s_fin


def _bias_placement():
    import numpy as np
    pm = np.zeros((3, N_HEADS, N_HEADS // 2 * LANES), np.float32)
    for piece in range(3):
        for h in range(N_HEADS):
            pm[piece, h, LANES * (h // 2) + 3 * (h % 2) + piece] = 1.0
    return jnp.asarray(pm)


def _attn_p_kernel(qt_ref, ka_ref, vt_ref, o_ref, qbd_ref, acc_ref, sa_ref, sb_ref):
    i = pl.program_id(2)
    bq = ATT_BLK
    hd = HEAD_DIM
    r = lax.broadcasted_iota(I32, (LANES, 2 * bq), 0)
    c = lax.broadcasted_iota(I32, (LANES, 2 * bq), 1)
    ones = jnp.where(((r < 3) & (c < bq)) | ((r >= 3) & (r < 6) & (c >= bq)), 1.0, 0.0).astype(BF16)
    zero = jnp.zeros((hd, bq), BF16)
    for pp in range(ATT_PAIRS):
        qb = qt_ref[pp * LANES:(pp + 1) * LANES, :]
        qbd_ref[pp, 0:hd, 0:bq] = qb[0:hd]
        qbd_ref[pp, 0:hd, bq:2 * bq] = zero
        qbd_ref[pp, hd:2 * hd, 0:bq] = zero
        qbd_ref[pp, hd:2 * hd, bq:2 * bq] = qb[hd:2 * hd]
        qbd_ref[pp, 2 * hd:2 * hd + LANES, :] = ones
    acc_ref[...] = jnp.zeros_like(acc_ref)

    nq = ka_ref.shape[0]

    def scores(s_ref, j):
        for pp in range(ATT_PAIRS):
            s_ref[pp] = _dot(ka_ref[j, :, pp * 2 * LANES:(pp + 1) * 2 * LANES], qbd_ref[pp])

    def attend(s_ref, j, carry, masked):
        out = []
        for pp in range(ATT_PAIRS):
            m_prev, l_prev = carry[2 * pp], carry[2 * pp + 1]
            s = s_ref[pp]
            if masked:
                key = j * bq + lax.broadcasted_iota(I32, s.shape, 0)
                qq = lax.broadcasted_iota(I32, s.shape, 1)
                qq = i * bq + jnp.where(qq >= bq, qq - bq, qq)
                s = jnp.where(key <= qq, s, -jnp.inf)
            m_new = jnp.maximum(m_prev, jnp.max(s, axis=0, keepdims=True))
            alpha = jnp.exp2(m_prev - m_new)
            p = jnp.exp2(s - m_new)
            l_new = alpha * l_prev + jnp.sum(p, axis=0, keepdims=True)
            pb = p.astype(BF16)
            vb = vt_ref[jnp.minimum(j, nq - 1), pp * LANES:(pp + 1) * LANES, :]
            r0 = pp * LANES
            acc_ref[r0:r0 + hd] = alpha[:, 0:bq] * acc_ref[r0:r0 + hd] + _dot(vb[0:hd], pb[:, 0:bq])
            acc_ref[r0 + hd:r0 + 2 * hd] = (alpha[:, bq:2 * bq] * acc_ref[r0 + hd:r0 + 2 * hd]
                                            + _dot(vb[hd:2 * hd], pb[:, bq:2 * bq]))
            out.extend((m_new, l_new))
        return tuple(out)

    def double_step(m, carry):
        j = 2 * m
        scores(sb_ref, j + 1)
        carry = attend(sa_ref, j, carry, False)
        scores(sa_ref, j + 2)
        return attend(sb_ref, j + 1, carry, False)

    init = (jnp.full((1, 2 * bq), -jnp.inf, F32), jnp.zeros((1, 2 * bq), F32)) * ATT_PAIRS
    scores(sa_ref, 0)
    carry = lax.fori_loop(0, i // 2, double_step, init)
    j = 2 * (i // 2)
    scores(sb_ref, jnp.minimum(j + 1, nq - 1))
    carry = attend(sa_ref, j, carry, True)
    carry = attend(sb_ref, j + 1, carry, True)
    for pp in range(ATT_PAIRS):
        l = carry[2 * pp + 1]
        r0 = pp * LANES
        o_ref[r0:r0 + hd] = acc_ref[r0:r0 + hd] / l[:, 0:bq]
        o_ref[r0 + hd:r0 + 2 * hd] = acc_ref[r0 + hd:r0 + 2 * hd] / l[:, bq:2 * bq]


def _attn_p(qt, kaug, vt, batch, t):
    blk = ATT_BLK
    nq = t // blk
    ngrp = N_HEADS // 2 // ATT_PAIRS
    rows = ATT_PAIRS * LANES
    qt4 = qt.reshape(batch, nq, D_ATT, blk)
    vt4 = vt.reshape(batch, nq, D_ATT, blk)
    ka4 = kaug.reshape(batch, nq, blk, 2 * D_ATT)
    return pl.pallas_call(
        _attn_p_kernel,
        out_shape=jax.ShapeDtypeStruct((batch, D_ATT, t), F32),
        grid=(batch, ngrp, nq),
        in_specs=[pl.BlockSpec((None, None, rows, blk), lambda b, p, i: (b, i, p, 0)),
                  pl.BlockSpec((None, nq, blk, 2 * rows), lambda b, p, i: (b, 0, 0, p), pipeline_mode=pl.Buffered(1)),
                  pl.BlockSpec((None, nq, rows, blk), lambda b, p, i: (b, 0, p, 0), pipeline_mode=pl.Buffered(1))],
        out_specs=pl.BlockSpec((None, rows, blk), lambda b, p, i: (b, p, i)),
        scratch_shapes=[pltpu.VMEM((ATT_PAIRS, 2 * LANES, 2 * blk), BF16), pltpu.VMEM((rows, blk), F32),
                        pltpu.VMEM((ATT_PAIRS, blk, 2 * blk), F32), pltpu.VMEM((ATT_PAIRS, blk, 2 * blk), F32)],
        compiler_params=_cparams(("arbitrary", "arbitrary", "arbitrary")),
        name="attn_p",
    )(qt4, ka4, vt4)


def _lane_cumsum(x):
    n = x.shape[-1]
    lane = lax.broadcasted_iota(I32, x.shape, x.ndim - 1)
    s = 1
    while s < n:
        x = x + jnp.where(lane >= s, pltpu.roll(x, s, x.ndim - 1), 0.0)
        s *= 2
    return x


def _fprep_s_kernel(cl_ref, ln_ref, fc_ref, fn_ref):
    b, h, p = cl_ref.shape
    cum = _lane_cumsum(cl_ref[...].reshape(b * h, p))
    fc_ref[...] = (-cum).reshape(b, h, p)
    total = cum[:, p - 1:p]
    cn = _lane_cumsum(ln_ref[...].reshape(b * h, LANES))
    fn_ref[...] = (-(total + cn)).reshape(b, h, LANES)


def _fprep_s(cache_lf_t, new_lf_t):
    b, h, p = cache_lf_t.shape
    return pl.pallas_call(
        _fprep_s_kernel,
        out_shape=(jax.ShapeDtypeStruct((b, h, p), F32), jax.ShapeDtypeStruct((b, h, LANES), F32)),
        grid=(1,),
        in_specs=[pl.BlockSpec((b, h, p), lambda i: (0, 0, 0)), pl.BlockSpec((b, h, LANES), lambda i: (0, 0, 0))],
        out_specs=(pl.BlockSpec((b, h, p), lambda i: (0, 0, 0)), pl.BlockSpec((b, h, LANES), lambda i: (0, 0, 0))),
        compiler_params=_cparams(("arbitrary",)),
        name="fprep_s",
    )(cache_lf_t, new_lf_t)


def _attn_s_kernel(q_ref, ck_ref, cv_ref, kn_ref, vn_ref, fc_ref, fn_ref, o_ref, qbd_ref, m_ref, l_ref, acc_ref,
                   *, tq):
    j = pl.program_id(1)
    nkv = pl.num_programs(1)
    rows = N_HEADS * tq
    bk = ck_ref.shape[1]

    @pl.when(j == 0)
    def _():
        q = q_ref[...]
        qrep = jnp.broadcast_to(q[None], (N_HEADS, tq, D_ATT)).reshape(rows, D_ATT)
        rh = lax.broadcasted_iota(I32, (rows, D_ATT), 0) // tq
        ch = lax.broadcasted_iota(I32, (rows, D_ATT), 1) // HEAD_DIM
        qbd_ref[...] = jnp.where(rh == ch, qrep, jnp.zeros_like(qrep))
        m_ref[...] = jnp.full_like(m_ref, -jnp.inf)
        l_ref[...] = jnp.zeros_like(l_ref)
        acc_ref[...] = jnp.zeros_like(acc_ref)

    def update(s, vt):
        m_prev = m_ref[...]
        m_new = jnp.maximum(m_prev, jnp.max(s, axis=1, keepdims=True))
        alpha = jnp.exp(m_prev - m_new)
        p = jnp.exp(s - m_new)
        l_ref[...] = alpha * l_ref[...] + jnp.sum(p, axis=1, keepdims=True)
        m_ref[...] = m_new
        acc_ref[...] = alpha * acc_ref[...] + _dot_nt(p.astype(BF16), vt)

    def bias(f, width):
        return jnp.broadcast_to(f[:, None, :], (N_HEADS, tq, width)).reshape(rows, width)

    s = _dot(qbd_ref[...], ck_ref[...].astype(BF16)) + bias(fc_ref[...], bk)
    update(s, cv_ref[...].astype(BF16))

    @pl.when(j == nkv - 1)
    def _():
        s2 = _dot(qbd_ref[...], kn_ref[...].astype(BF16)) + bias(fn_ref[...][:, 0:tq], tq)
        key = lax.broadcasted_iota(I32, (rows, tq), 1)
        qq = lax.broadcasted_iota(I32, (rows, tq), 0) % tq
        update(jnp.where(key <= qq, s2, -jnp.inf), vn_ref[...].astype(BF16))
        o = acc_ref[...] / l_ref[...]
        rh = lax.broadcasted_iota(I32, (rows, D_ATT), 0) // tq
        ch = lax.broadcasted_iota(I32, (rows, D_ATT), 1) // HEAD_DIM
        o = jnp.where(rh == ch, o, 0.0).reshape(N_HEADS, tq, D_ATT)
        o_ref[...] = jnp.sum(o, axis=0)


def _attn_s(q, cache_k, cache_v, k_new, v_new, fc, fn):
    b, tq, _ = q.shape
    p = cache_k.shape[2]
    bk = min(SAMPLE_KV_BLK, p)
    nkv = p // bk
    rows = N_HEADS * tq
    tok = lambda: pl.BlockSpec((None, tq, D_ATT), lambda bi, j: (bi, 0, 0))
    new = lambda: pl.BlockSpec((None, D_ATT, tq), lambda bi, j: (bi, 0, 0))
    return pl.pallas_call(
        functools.partial(_attn_s_kernel, tq=tq),
        out_shape=jax.ShapeDtypeStruct((b, tq, D_ATT), F32),
        grid=(b, nkv),
        in_specs=[tok(),
                  pl.BlockSpec((None, D_ATT, bk), lambda bi, j: (bi, 0, j)),
                  pl.BlockSpec((None, D_ATT, bk), lambda bi, j: (bi, 0, j)),
                  new(), new(),
                  pl.BlockSpec((None, N_HEADS, bk), lambda bi, j: (bi, 0, j)),
                  pl.BlockSpec((None, N_HEADS, LANES), lambda bi, j: (bi, 0, 0))],
        out_specs=tok(),
        scratch_shapes=[pltpu.VMEM((rows, D_ATT), BF16), pltpu.VMEM((rows, 1), F32),
                        pltpu.VMEM((rows, 1), F32), pltpu.VMEM((rows, D_ATT), F32)],
        compiler_params=_cparams(("arbitrary", "arbitrary")),
        name="attn_s",
    )(q, cache_k, cache_v, k_new, v_new, fc, fn)


def _outproj_kernel(x_ref, y4_ref, att_ref, gt_ref, sh_ref, sc_ref, gf_ref, wglu_ref, bglu_ref, wout_ref,
                    wr_ref, br_ref, cin_ref,
                    x1_ref, h2_ref, idx_ref, gate_ref, rank_ref, cnt_ref, carry_ref, ys_ref, *, att_transposed):
    i = pl.program_id(0)
    nb, tt, d = x_ref.shape
    tm = nb * tt

    @pl.when(i == 0)
    def _():
        carry_ref[...] = cin_ref[...]

    for s in range(SSM_CHUNK):
        for o in range(N_OCT):
            ys_ref[o, pl.ds(s, tm // SSM_CHUNK, stride=SSM_CHUNK), :] = (
                y4_ref[o, :, s * LANES:(s + 1) * LANES].astype(F32))
    ysf = jnp.concatenate([ys_ref[o] for o in range(N_OCT)], axis=-1)
    glu = ysf * jax.nn.sigmoid(_dot(ysf.astype(BF16), wglu_ref[...]) + bglu_ref[...])
    att = att_ref[...]
    if att_transposed:
        att = att.T
    mix = _dot(glu.astype(BF16), wout_ref[0:D_SSM, :]) + _dot(att.astype(BF16), wout_ref[D_SSM:, :])
    x1 = x_ref[...] + gt_ref[...] * mix.reshape(nb, tt, d)
    x1_ref[...] = x1
    h2 = _norm_mod(x1, gf_ref[...], sc_ref[...], sh_ref[...]).reshape(tm, d)
    bits = lax.bitcast_convert_type(h2.astype(BF16).astype(F32), I32)
    h2_ref[...] = (bits[:, d // 2:] & jnp.int32(-65536)) | lax.shift_right_logical(bits[:, :d // 2], 16)

    logits = lax.dot_general(wr_ref[...], h2, (((1,), (1,)), ((), ())), preferred_element_type=F32,
                             precision=lax.Precision.HIGHEST) + br_ref[...]
    sub = lax.broadcasted_iota(I32, logits.shape, 0)
    work = logits
    vals, idxs = [], []
    for _ in range(TOP_K):
        mx = jnp.max(work, axis=0, keepdims=True)
        ix = jnp.min(jnp.where(work == mx, sub, N_EXPERTS), axis=0, keepdims=True)
        vals.append(mx)
        idxs.append(ix)
        work = jnp.where(sub == ix, -jnp.inf, work)
    ex = [jnp.exp(v - vals[0]) for v in vals]
    den = ex[0] + ex[1] + ex[2] + ex[3]
    mh = jnp.where(work == -jnp.inf, 1.0, 0.0)
    r = lax.broadcasted_iota(I32, (tm, tm), 0)
    c = lax.broadcasted_iota(I32, (tm, tm), 1)
    earlier = jnp.where(r < c, 1.0, 0.0).astype(BF16)
    carry = carry_ref[...]
    before = _dot(mh.astype(BF16), earlier) + carry[:, 0:1]
    carry_ref[...] = carry + jnp.sum(mh, axis=1, keepdims=True)
    for kk in range(TOP_K):
        idx_ref[kk:kk + 1, :] = idxs[kk]
        gate_ref[kk:kk + 1, :] = ex[kk] / den
        rk = jnp.sum(jnp.where(sub == idxs[kk], before, 0.0), axis=0, keepdims=True)
        rank_ref[kk:kk + 1, :] = rk.astype(I32)
    cnt_ref[...] = carry_ref[...]


def _outproj(x, y4, att, gt, sh, sc, g_ffn, w, counts_in, *, att_transposed, tm):
    b, t, d = x.shape
    n = b * t
    tm = min(tm, n)
    tt = min(t, tm)
    nb = tm // tt
    tpb = t // tt

    def xmap(i):
        return (i // tpb, i % tpb, 0) if nb == 1 else (i, 0, 0)

    def bmap(i):
        return (i // tpb, 0, 0) if nb == 1 else (i, 0, 0)

    full = lambda shape: pl.BlockSpec(shape, lambda i: (0,) * len(shape))
    if att_transposed:
        att_spec = pl.BlockSpec((None, D_ATT, tm), lambda i: (i // tpb, 0, i % tpb))
    else:
        att_spec = pl.BlockSpec((tm, D_ATT), lambda i: (i, 0))
    choice = lambda dt: (jax.ShapeDtypeStruct((TOP_K, n), dt), pl.BlockSpec((TOP_K, tm), lambda i: (0, i)))
    outs = [(jax.ShapeDtypeStruct((b, t, d), F32), pl.BlockSpec((nb, tt, d), xmap)),
            (jax.ShapeDtypeStruct((n, d // 2), I32), pl.BlockSpec((tm, d // 2), lambda i: (i, 0))),
            choice(I32), choice(F32), choice(I32),
            (jax.ShapeDtypeStruct((N_EXPERTS, LANES), F32), full((N_EXPERTS, LANES)))]
    return pl.pallas_call(
        functools.partial(_outproj_kernel, att_transposed=att_transposed),
        out_shape=tuple(o[0] for o in outs),
        grid=(n // tm,),
        in_specs=[pl.BlockSpec((nb, tt, d), xmap),
                  pl.BlockSpec((N_OCT, tm // SSM_CHUNK, SSM_CHUNK * LANES), lambda i: (0, i, 0)),
                  att_spec,
                  pl.BlockSpec((nb, 1, d), bmap), pl.BlockSpec((nb, 1, d), bmap), pl.BlockSpec((nb, 1, d), bmap),
                  full((1, d)), full((D_SSM, D_SSM)), full((1, D_SSM)), full((d, d)),
                  full((N_EXPERTS, d)), full((N_EXPERTS, 1)), full((N_EXPERTS, LANES))],
        out_specs=tuple(o[1] for o in outs),
        scratch_shapes=[pltpu.VMEM((N_EXPERTS, LANES), F32), pltpu.VMEM((N_OCT, tm, LANES), F32)],
        compiler_params=_cparams(("arbitrary",)),
        name="outproj_t" if att_transposed else "outproj",
    )(x, y4, att, gt, sh, sc, g_ffn, w["wglu"], w["bglu"], w["wout"], w["wr"], w["br"], counts_in)


def _dispatch_kernel(zst_ref, nu_ref, dest_ref, *rest, tm, tiles):
    h_refs = rest[:len(tiles)]
    xs_ref, zero_ref, hbuf_ref, zsem, sems = rest[len(tiles):]
    i = pl.program_id(0)
    n_blocks = xs_ref.shape[0] // MOE_BLK

    @pl.when(i == 0)
    def _():
        zero_ref[...] = jnp.zeros_like(zero_ref)

        def zero_copy(start):
            start = pl.multiple_of(start, MOE_BLK)
            return pltpu.make_async_copy(zero_ref, xs_ref.at[pl.ds(start, MOE_BLK)], zsem)

        for e in range(N_EXPERTS):
            @pl.when(zst_ref[e] >= 0)
            def _():
                zero_copy(jnp.maximum(zst_ref[e], 0)).start()

        def tail_start(j, carry):
            zero_copy(j * MOE_BLK).start()
            return carry

        def tail_wait(j, carry):
            zero_copy(j * MOE_BLK).wait()
            return carry

        lax.fori_loop(nu_ref[0], n_blocks, tail_start, 0)
        for e in range(N_EXPERTS):
            @pl.when(zst_ref[e] >= 0)
            def _():
                zero_copy(jnp.maximum(zst_ref[e], 0)).wait()
        lax.fori_loop(nu_ref[0], n_blocks, tail_wait, 0)

    slot = i % 2

    def scatter_rows(h_ref):
        hbuf_ref[slot] = h_ref[...]

        def issue(t, carry):
            for kk in range(TOP_K):
                dst = dest_ref[0, 0, t * TOP_K + kk]
                pltpu.make_async_copy(hbuf_ref.at[slot, pl.ds(t, 1)], xs_ref.at[pl.ds(dst, 1)],
                                      sems.at[slot]).start(priority=kk % 2)
            return carry

        lax.fori_loop(0, tm, issue, 0, unroll=8)

    first_tile = 0
    for h_ref, n_tiles in zip(h_refs, tiles):
        pl.when((i >= first_tile) & (i < first_tile + n_tiles))(functools.partial(scatter_rows, h_ref))
        first_tile += n_tiles

    def wait_tile(s):
        n_copied = tm * TOP_K
        pltpu.make_async_copy(xs_ref.at[pl.ds(0, n_copied)], xs_ref.at[pl.ds(0, n_copied)], sems.at[s]).wait()

    @pl.when(i > 0)
    def _():
        wait_tile(1 - slot)

    @pl.when(i == pl.num_programs(0) - 1)
    def _():
        wait_tile(slot)


def _dispatch(h2s, dest, zstart, n_used, n_rows, tm):
    d = h2s[0].shape[1]
    dt = h2s[0].dtype
    tiles = tuple(h.shape[0] // tm for h in h2s)
    nt = sum(tiles)
    dest3 = dest.reshape(nt, 1, tm * TOP_K)
    in_specs = [pl.BlockSpec((1, 1, tm * TOP_K), lambda i, *_: (i, 0, 0), memory_space=pltpu.SMEM)]
    first_tile = 0
    for n_tiles in tiles:
        in_specs.append(pl.BlockSpec(
            (tm, d), lambda i, *_, f=first_tile, m=n_tiles: (jnp.clip(i - f, 0, m - 1), 0)))
        first_tile += n_tiles
    return pl.pallas_call(
        functools.partial(_dispatch_kernel, tm=tm, tiles=tiles),
        out_shape=jax.ShapeDtypeStruct((n_rows, d), dt),
        grid_spec=pltpu.PrefetchScalarGridSpec(
            num_scalar_prefetch=2, grid=(nt,),
            in_specs=in_specs,
            out_specs=pl.BlockSpec(memory_space=pl.ANY),
            scratch_shapes=[pltpu.VMEM((MOE_BLK, d), dt), pltpu.VMEM((2, tm, d), dt),
                            pltpu.SemaphoreType.DMA, pltpu.SemaphoreType.DMA((2,))]),
        compiler_params=_cparams(("arbitrary",)),
        name="moe_dispatch",
    )(zstart, n_used, dest3, *h2s)


def _expert_kernel(be_ref, nxt_ref, slot_ref, first_ref, nu_ref, xs_ref, wup_hbm, bup_ref, wdn_hbm, bdn_ref,
                   ys_ref, wup_f32, wdn_f32, wup_bf, wdn_bf, sems):
    i = pl.program_id(0)
    e = be_ref[i]

    def fetch(expert, slot):
        return (pltpu.make_async_copy(wup_hbm.at[expert], wup_f32.at[slot], sems.at[0, slot]),
                pltpu.make_async_copy(wdn_hbm.at[expert], wdn_f32.at[slot], sems.at[1, slot]))

    @pl.when(i < nu_ref[0])
    def _():
        @pl.when(first_ref[i] == 1)
        def _():
            slot = slot_ref[i]

            @pl.when(i == 0)
            def _():
                for copy in fetch(e, slot):
                    copy.start()

            for copy in fetch(e, slot):
                copy.wait()

            @pl.when(nxt_ref[i] >= 0)
            def _():
                for copy in fetch(jnp.maximum(nxt_ref[i], 0), 1 - slot):
                    copy.start()

            wup_bf[...] = wup_f32[slot].astype(BF16)
            wdn_bf[...] = wdn_f32[slot].astype(BF16)

        words = xs_ref[...]
        half = words.shape[1]
        x_lo = lax.bitcast_convert_type(words << 16, F32).astype(BF16)
        x_hi = lax.bitcast_convert_type(words & jnp.int32(-65536), F32).astype(BF16)
        up = _dot(x_lo, wup_bf[0:half, :]) + _dot(x_hi, wup_bf[half:, :]) + bup_ref[...]
        x_glu = jnp.minimum(up[:, :D_FF], SWIGLU_LIMIT)
        x_lin = jnp.clip(up[:, D_FF:], -SWIGLU_LIMIT, SWIGLU_LIMIT)
        act = (x_lin + 1.0) * (x_glu * jax.nn.sigmoid(SWIGLU_ALPHA * x_glu))
        ys_ref[...] = (_dot(act.astype(BF16), wdn_bf[...]) + bdn_ref[...]).astype(ys_ref.dtype)

    @pl.when(i >= nu_ref[0])
    def _():
        ys_ref[...] = jnp.zeros_like(ys_ref)


def _experts(xs, block_e, n_used, has_rows, w_up, b_up, w_down, b_down):
    n_rows, packed = xs.shape
    d = 2 * packed
    nblk = n_rows // MOE_BLK
    blocks = jnp.arange(nblk, dtype=I32)
    used = blocks < n_used[0]
    first = (used & ((blocks == 0) | (block_e != jnp.roll(block_e, 1)))).astype(I32)
    slot = ((jnp.cumsum(first) - 1) % 2).astype(I32)
    experts = jnp.arange(N_EXPERTS, dtype=I32)
    later = jnp.where(has_rows[None, :] & (experts[None, :] > experts[:, None]), experts[None, :], N_EXPERTS)
    nxt_e = jnp.min(later, axis=1)
    nxt_e = jnp.where(nxt_e < N_EXPERTS, nxt_e, -1).astype(I32)
    nxt = jnp.sum(jnp.where(block_e[:, None] == experts[None, :], nxt_e[None, :], 0), axis=1).astype(I32)
    rowmap = lambda i, be, nx, sl, fi, nu: (jnp.minimum(i, nu[0] - 1), 0)
    emap = lambda i, be, nx, sl, fi, nu: (be[i], 0, 0)
    return pl.pallas_call(
        _expert_kernel,
        out_shape=jax.ShapeDtypeStruct((n_rows, d), BF16),
        grid_spec=pltpu.PrefetchScalarGridSpec(
            num_scalar_prefetch=5, grid=(nblk,),
            in_specs=[pl.BlockSpec((MOE_BLK, packed), rowmap),
                      pl.BlockSpec(memory_space=pl.ANY), pl.BlockSpec((None, 1, 2 * D_FF), emap),
                      pl.BlockSpec(memory_space=pl.ANY), pl.BlockSpec((None, 1, d), emap)],
            out_specs=pl.BlockSpec((MOE_BLK, d), lambda i, *_: (i, 0)),
            scratch_shapes=[pltpu.VMEM((2, d, 2 * D_FF), F32), pltpu.VMEM((2, D_FF, d), F32),
                            pltpu.VMEM((d, 2 * D_FF), BF16), pltpu.VMEM((D_FF, d), BF16),
                            pltpu.SemaphoreType.DMA((2, 2))]),
        compiler_params=_cparams(("arbitrary",)),
        name="moe_experts",
    )(block_e, nxt, slot, first, n_used, xs, w_up, b_up, w_down, b_down)


def _combine_kernel(meta_ref, metan_ref, x1_ref, col_ref, gate_ref, gt_ref, gfin_ref, ys_ref, y_ref, ybuf_ref, sems,
                    *, tm):
    i = pl.program_id(0)
    nt = pl.num_programs(0)
    nb, tt, d = x1_ref.shape
    rows = ybuf_ref.shape[1]

    ra = RUN_ALIGN

    def start_copies(m_ref, slot):
        def per_expert(e, carry):
            a = m_ref[0, 0, e]
            n_chunks = m_ref[0, 0, N_EXPERTS + e]
            bo = m_ref[0, 0, 2 * N_EXPERTS + e]

            def per_chunk(c, carry2):
                src = ys_ref.at[pl.ds(pl.multiple_of(a + ra * c, ra), ra)]
                dst = ybuf_ref.at[slot, pl.ds(pl.multiple_of(bo + ra * c, ra), ra)]
                pltpu.make_async_copy(src, dst, sems.at[slot]).start(priority=e % 2)
                return carry2

            lax.fori_loop(0, n_chunks, per_chunk, 0)
            return carry

        for e in range(N_EXPERTS):
            per_expert(e, 0)

    def wait_copies(m_ref, slot):
        total = m_ref[0, 0, 3 * N_EXPERTS]

        def wait_rows(n_rows):
            pltpu.make_async_copy(ys_ref.at[pl.ds(0, n_rows)], ybuf_ref.at[slot, pl.ds(0, n_rows)],
                                  sems.at[slot]).wait()

        lax.fori_loop(0, total // WAIT_GROUP, lambda c, carry: (wait_rows(WAIT_GROUP * ra), carry)[1], 0)
        lax.fori_loop(0, total % WAIT_GROUP, lambda c, carry: (wait_rows(ra), carry)[1], 0)

    @pl.when(i == 0)
    def _():
        ybuf_ref[...] = jnp.zeros_like(ybuf_ref)
        start_copies(meta_ref, 0)

    @pl.when(i + 1 < nt)
    def _():
        start_copies(metan_ref, (i + 1) % 2)

    slot = i % 2
    wait_copies(meta_ref, slot)
    col = col_ref[...]
    gate = gate_ref[...]
    lane = lax.broadcasted_iota(I32, (tm, rows), 1)
    pick = jnp.where(lane == col[:, 0:1], gate[:, 0:1], 0.0)
    for kk in range(1, TOP_K):
        pick = pick + jnp.where(lane == col[:, kk:kk + 1], gate[:, kk:kk + 1], 0.0)
    moe = _dot(pick.astype(BF16), ybuf_ref[slot])
    x2 = x1_ref[...] + gt_ref[...] * moe.reshape(nb, tt, d)
    ms = jnp.mean(x2 * x2, axis=-1, keepdims=True)
    y_ref[...] = x2 * lax.rsqrt(ms + NORM_EPS) * gfin_ref[...]


def _combine_plan(idx, dest, off, before0, tm):
    n = idx.shape[1]
    nt = n // tm
    is_e = idx[None, :, :] == jnp.arange(N_EXPERTS, dtype=I32)[:, None, None]
    cnt = jnp.any(is_e, axis=1).astype(I32).reshape(N_EXPERTS, nt, tm).sum(axis=2).T
    before = before0[None, :] + jnp.cumsum(cnt, axis=0) - cnt
    start = off[None, :] + before
    ra = RUN_ALIGN
    lo = start // ra * ra
    hi = jnp.where(cnt > 0, (start + cnt + ra - 1) // ra * ra, lo)
    span = hi - lo
    boff = jnp.cumsum(span, axis=1) - span
    delta = jnp.repeat((boff - lo).T, tm, axis=1)
    col = dest + jnp.sum(jnp.where(is_e, delta[:, None, :], 0), axis=0)
    total = jnp.sum(span // ra, axis=1, keepdims=True)
    meta = jnp.concatenate([lo, span // ra, boff, jnp.broadcast_to(total, lo.shape)], axis=1).astype(I32)
    return meta.reshape(nt, 1, 4 * N_EXPERTS), col.T.astype(I32)


def _combine(x1, gate, idx, dest, off, before0, ys, gt, g_final, tm):
    b, t, d = x1.shape
    n = b * t
    tm = min(tm, n)
    tt = min(t, tm)
    nb = tm // tt
    tpb = t // tt
    nt = n // tm
    rows = -(-(tm * TOP_K + 2 * (RUN_ALIGN - 1) * N_EXPERTS) // 256) * 256
    meta, col = _combine_plan(idx, dest, off, before0, tm)

    def xmap(i):
        return (i // tpb, i % tpb, 0) if nb == 1 else (i, 0, 0)

    def bmap(i):
        return (i // tpb, 0, 0) if nb == 1 else (i, 0, 0)

    smem = lambda imap: pl.BlockSpec((1, 1, 4 * N_EXPERTS), imap, memory_space=pltpu.SMEM)
    return pl.pallas_call(
        functools.partial(_combine_kernel, tm=tm),
        out_shape=jax.ShapeDtypeStruct((b, t, d), F32),
        grid=(nt,),
        in_specs=[smem(lambda i: (i, 0, 0)), smem(lambda i: (jnp.minimum(i + 1, nt - 1), 0, 0)),
                  pl.BlockSpec((nb, tt, d), xmap),
                  pl.BlockSpec((tm, TOP_K), lambda i: (i, 0)),
                  pl.BlockSpec((tm, TOP_K), lambda i: (i, 0)),
                  pl.BlockSpec((nb, 1, d), bmap),
                  pl.BlockSpec((1, d), lambda i: (0, 0)),
                  pl.BlockSpec(memory_space=pl.ANY)],
        out_specs=pl.BlockSpec((nb, tt, d), xmap),
        scratch_shapes=[pltpu.VMEM((2, rows, d), BF16), pltpu.SemaphoreType.DMA((2,))],
        compiler_params=_cparams(("arbitrary",)),
        name="moe_combine",
    )(meta, meta, x1, col, gate, gt, g_final, ys)


def _moe(streams, counts, g_final, w_up, b_up, w_down, b_down):
    n_total = sum(s["h2"].shape[0] for s in streams)
    cnt = counts[:, 0].astype(I32)
    padded = (cnt + MOE_BLK - 1) // MOE_BLK * MOE_BLK
    pad_end = jnp.cumsum(padded)
    off = (pad_end - padded).astype(I32)
    n_blocks = -(-n_total * TOP_K // MOE_BLK) + N_EXPERTS
    n_rows = n_blocks * MOE_BLK
    starts = jnp.arange(n_blocks, dtype=I32) * MOE_BLK
    block_e = jnp.minimum(jnp.sum((pad_end[None, :] <= starts[:, None]).astype(I32), axis=1), N_EXPERTS - 1)
    n_used = (pad_end[-1:] // MOE_BLK).astype(I32)
    zstart = jnp.where(padded > 0, pad_end - MOE_BLK, -1).astype(I32)
    experts = jnp.arange(N_EXPERTS, dtype=I32)
    dests = []
    for s in streams:
        seg = jnp.sum(jnp.where(s["idx"][None] == experts[:, None, None], off[:, None, None], 0), axis=0)
        dests.append((seg + s["rank"]).astype(I32))
    tm = min([512] + [s["h2"].shape[0] for s in streams])
    dest_rows = jnp.concatenate([dest.T for dest in dests], axis=0)
    xs = _dispatch([s["h2"] for s in streams], dest_rows, zstart, n_used, n_rows, tm)
    ys = _experts(xs, block_e, n_used, padded > 0, w_up, b_up, w_down, b_down)
    return [_combine(s["x1"], s["gate"].T, s["idx"], dest, off, s["before"], ys, s["gt"], g_final, 256)
            for s, dest in zip(streams, dests)]


def _stream(x, mod, w, ssm_ops, s0, cache, params, counts_in):
    b, t, d = x.shape
    n = b * t
    sh_mix, sc_mix, gt_mix, sh_ffn, sc_ffn, gt_ffn = mod
    prompt = cache is None
    tm = 1024 if prompt else 512
    u4, q, kt, vt, lft, vtb, kaug = _inproj(x, sh_mix, sc_mix, params["g_mix"], w, prompt=prompt, tm=tm)
    n_chunks = t // SSM_CHUNK
    y4, s_fin = _ssm(u4, ssm_ops, s0, b, n_chunks)
    if prompt:
        att = _attn_p(q, kaug, vtb, b, t)
    else:
        cache_k, cache_v, cache_lf = cache
        p = cache_k.shape[1]
        ck_t = jnp.transpose(cache_k, (0, 2, 3, 1)).reshape(b, D_ATT, p)
        cv_t = jnp.transpose(cache_v, (0, 2, 3, 1)).reshape(b, D_ATT, p)
        cl_t = jnp.transpose(cache_lf, (0, 2, 1))
        ln_t = jnp.pad(lft, ((0, 0), (0, 0), (0, LANES - t)))
        fc, fn = _fprep_s(cl_t, ln_t)
        att = _attn_s(q.reshape(b, t, D_ATT), ck_t, cv_t, kt, vt, fc, fn).reshape(n, D_ATT)
    x1, h2, idx, gate, rank, counts = _outproj(x, y4, att, gt_mix, sh_ffn, sc_ffn, params["g_ffn"], w, counts_in,
                                               att_transposed=prompt, tm=tm)
    routed = {"x1": x1, "h2": h2, "idx": idx, "gate": gate, "rank": rank, "gt": gt_ffn,
              "before": counts_in[:, 0].astype(I32)}
    heads = lambda a: jnp.transpose(a.reshape(b, N_HEADS, HEAD_DIM, t), (0, 3, 1, 2))[None]
    return routed, counts, heads(kt), heads(vt), jnp.transpose(lft, (0, 2, 1))[None], s_fin


def _state_in(re, im):
    b = re.shape[0]
    s = jnp.concatenate([re.reshape(b, N_OCT, OCT_STATE // 2), im.reshape(b, N_OCT, OCT_STATE // 2)], axis=-1)
    return jnp.transpose(s, (1, 0, 2))


def _state_out(s):
    b = s.shape[1]
    s = jnp.transpose(s, (1, 0, 2))
    half = OCT_STATE // 2
    re = s[:, :, :half].reshape(1, b, N_SSM_GROUPS, SSM_STATE)
    im = s[:, :, half:].reshape(1, b, N_SSM_GROUPS, SSM_STATE)
    return re, im


def kernel(x_prompt, x_sample, c_prompt, c_sample, cache_k, cache_v, cache_logf, state_ssm_re, state_ssm_im, w_ada, b_ada, g_mix, w_in, b_forget, ssm_log_dt, ssm_a_re, ssm_a_im, ssm_b_re, ssm_b_im, ssm_c_re, ssm_c_im, ssm_d, w_glu, b_glu, w_out, g_ffn, w_router, b_router, w_up, b_up, w_down, b_down, g_final):
    assert w_ada.shape[0] == 1, "single-layer trunk"
    d = D_MODEL
    bp, tp, _ = x_prompt.shape
    bs, ts, _ = x_sample.shape
    n_c = bp + bs
    c_all = jnp.pad(jnp.concatenate([c_prompt, c_sample], axis=0), ((0, -n_c % 8), (0, 0)))
    mod = _ada(c_all, w_ada[0], b_ada[0].reshape(1, 6 * d))
    mod_p = [mod[:bp, j * d:(j + 1) * d].reshape(bp, 1, d) for j in range(6)]
    mod_s = [mod[bp:n_c, j * d:(j + 1) * d].reshape(bs, 1, d) for j in range(6)]
    wi = w_in[0]
    o1, o2, o3, o4 = D_SSM, D_SSM + D_ATT, D_SSM + 2 * D_ATT, D_SSM + 3 * D_ATT
    w_q = wi[:, o1:o2].astype(BF16)
    w_k = wi[:, o2:o3].astype(BF16)
    unused = jnp.zeros((8, LANES), BF16)
    base = {"wu": wi[:, :o1].astype(BF16), "wkt": w_k.T, "wvt": wi[:, o3:o4].astype(BF16).T,
            "wft": wi[:, o4:].astype(BF16).T, "bfc": b_forget[0].reshape(N_HEADS, 1), "pm": _bias_placement(),
            "wglu": w_glu[0].astype(BF16), "bglu": b_glu[0].reshape(1, D_SSM), "wout": w_out[0].astype(BF16),
            "wr": w_router[0].T, "br": b_router[0].reshape(N_EXPERTS, 1)}
    w_p = dict(base, wq=w_q.T, wk=w_k)
    w_s = dict(base, wq=w_q, wk=unused)
    params = {"g_mix": g_mix[0].reshape(1, d), "g_ffn": g_ffn[0].reshape(1, d), "g_final": g_final.reshape(1, d),
              "w_up": w_up[0], "b_up": b_up[0].reshape(N_EXPERTS, 1, 2 * D_FF),
              "w_down": w_down[0], "b_down": b_down[0].reshape(N_EXPERTS, 1, d)}
    ssm_ops = _ssm_operators(ssm_log_dt[0], ssm_a_re[0], ssm_a_im[0], ssm_b_re[0], ssm_b_im[0],
                             ssm_c_re[0], ssm_c_im[0], ssm_d[0])
    zero_state = jnp.zeros((N_OCT, bp, OCT_STATE), F32)
    counts0 = jnp.zeros((N_EXPERTS, LANES), F32)
    r_p, counts_p, k_p, v_p, f_p, s_p = _stream(x_prompt, mod_p, w_p, ssm_ops, zero_state, None, params, counts0)
    s0 = _state_in(state_ssm_re[0], state_ssm_im[0])
    r_s, counts, k_s, v_s, f_s, s_s = _stream(x_sample, mod_s, w_s, ssm_ops, s0,
                                              (cache_k[0], cache_v[0], cache_logf[0]), params, counts_p)
    y_p, y_s = _moe([r_p, r_s], counts, params["g_final"],
                    params["w_up"], params["b_up"], params["w_down"], params["b_down"])
    sre_p, sim_p = _state_out(s_p)
    sre_s, sim_s = _state_out(s_s)
    return (y_p, y_s, k_p, v_p, f_p, sre_p, sim_p, k_s, v_s, f_s, sre_s, sim_s)
```

```python
import functools
import math

import jax
import jax.numpy as jnp
from jax import lax
from jax.experimental import pallas as pl
from jax.experimental.pallas import tpu as pltpu

F32 = jnp.float32
BF16 = jnp.bfloat16
I32 = jnp.int32

D_MODEL = 1024
D_SSM = 512
SSM_GROUP = 16
N_SSM_GROUPS = 32
SSM_STATE = 64
D_ATT = 512
HEAD_DIM = 64
N_HEADS = 8
N_EXPERTS = 32
TOP_K = 4
D_FF = 1024
SWIGLU_LIMIT = 7.0
SWIGLU_ALPHA = 1.702
NORM_EPS = 1e-5

LANES = 128
N_OCT = D_SSM // LANES
OCT_STATE = 2 * (N_SSM_GROUPS // N_OCT) * SSM_STATE
SSM_CHUNK = 8
SSM_ROWS = 512
SSM_INTERLEAVE = 4
ATT_BLK = 256
ATT_PAIRS = 4
LOG2E = math.log2(math.e)
SAMPLE_KV_BLK = 4096
MOE_BLK = 256
RUN_ALIGN = 16
WAIT_GROUP = 16
VMEM_LIMIT = 52 * 1024 * 1024


def _cparams(sem, vmem=VMEM_LIMIT):
    return pltpu.CompilerParams(dimension_semantics=sem, vmem_limit_bytes=vmem)


def _dot(a, b):
    return jnp.dot(a, b, preferred_element_type=F32)


def _dot_nt(a, b):
    return lax.dot_general(a, b, (((1,), (1,)), ((), ())), preferred_element_type=F32)


def _dot_hi(a, b):
    return jnp.dot(a, b, preferred_element_type=F32, precision=lax.Precision.HIGHEST)


def _ada_kernel(c_ref, w_ref, b_ref, o_ref):
    c = c_ref[...]
    s = c * jax.nn.sigmoid(c)
    o_ref[...] = _dot_hi(s, w_ref[...]) + b_ref[...]


def _ada(c_all, w_ada, b_ada):
    m, d = c_all.shape
    n = w_ada.shape[1]
    return pl.pallas_call(
        _ada_kernel,
        out_shape=jax.ShapeDtypeStruct((m, n), F32),
        grid=(n // d,),
        in_specs=[pl.BlockSpec((m, d), lambda j: (0, 0)),
                  pl.BlockSpec((d, d), lambda j: (0, j)),
                  pl.BlockSpec((1, d), lambda j: (0, j))],
        out_specs=pl.BlockSpec((m, d), lambda j: (0, j)),
        compiler_params=_cparams(("arbitrary",)),
        name="ada",
    )(c_all, w_ada, b_ada)


def _norm_mod(x, g, scale, shift):
    ms = jnp.mean(x * x, axis=-1, keepdims=True)
    y = x * lax.rsqrt(ms + NORM_EPS) * g
    return y * (1.0 + scale) + shift


def _log_sigmoid(z):
    return jnp.minimum(z, 0.0) - jnp.log1p(jnp.exp(-jnp.abs(z)))


def _inproj_kernel(x_ref, sh_ref, sc_ref, g_ref, wu_ref, wq_ref, wkt_ref, wvt_ref, wft_ref, bfc_ref,
                   wk_ref, pm_ref,
                   u4_ref, q_ref, kt_ref, vt_ref, lft_ref, vtb_ref, ka_ref, us_ref, carry_ref, *, prompt, tpb):
    nb, tt, d = x_ref.shape
    tm = nb * tt
    h = _norm_mod(x_ref[...], g_ref[...], sc_ref[...], sh_ref[...]).reshape(tm, d).astype(BF16)
    u = _dot(h, wu_ref[...])
    for o in range(N_OCT):
        us_ref[o] = u[:, o * LANES:(o + 1) * LANES]
    for s in range(SSM_CHUNK):
        for o in range(N_OCT):
            piece = us_ref[o, pl.ds(s, tm // SSM_CHUNK, stride=SSM_CHUNK), :]
            u4_ref[o, :, s * LANES:(s + 1) * LANES] = piece.astype(BF16)
    kt = _dot_nt(wkt_ref[...], h)
    vt = _dot_nt(wvt_ref[...], h)
    lft = _log_sigmoid(_dot_nt(wft_ref[...], h) + bfc_ref[...])
    scale = HEAD_DIM ** -0.5
    if prompt:
        kt_ref[...] = kt
        vt_ref[...] = vt
        lft_ref[...] = lft
        qt = (_dot_nt(wq_ref[...], h) * (scale * LOG2E)).astype(BF16)
        vtb = vt.astype(BF16)
        for j in range(tm // ATT_BLK):
            q_ref[j] = qt[:, j * ATT_BLK:(j + 1) * ATT_BLK]
            vtb_ref[j] = vtb[:, j * ATT_BLK:(j + 1) * ATT_BLK]
        k_tok = _dot(h, wk_ref[...])

        @pl.when(pl.program_id(0) % tpb == 0)
        def _():
            carry_ref[...] = jnp.zeros_like(carry_ref)

        cum = _lane_cumsum(lft) + carry_ref[:, 0:1]
        carry_ref[...] = jnp.broadcast_to(cum[:, tm - 1:tm], carry_ref.shape)
        nf = cum * (-LOG2E)
        hi = nf.astype(BF16).astype(F32)
        r1 = nf - hi
        mid = r1.astype(BF16).astype(F32)
        lo = (r1 - mid).astype(BF16).astype(F32)
        place = lambda piece, pm: lax.dot_general(piece, pm, (((0,), (0,)), ((), ())), preferred_element_type=F32)
        slab = place(hi, pm_ref[0]) + place(mid, pm_ref[1]) + place(lo, pm_ref[2])
        for j in range(tm // ATT_BLK):
            rows = slice(j * ATT_BLK, (j + 1) * ATT_BLK)
            for p in range(N_HEADS // 2):
                ka_ref[j, :, 2 * p * LANES:(2 * p + 1) * LANES] = k_tok[rows, p * LANES:(p + 1) * LANES].astype(BF16)
                ka_ref[j, :, (2 * p + 1) * LANES:(2 * p + 2) * LANES] = (
                    slab[rows, p * LANES:(p + 1) * LANES].astype(BF16))
    else:
        for bl in range(nb):
            kt_ref[bl] = kt[:, bl * tt:(bl + 1) * tt]
            vt_ref[bl] = vt[:, bl * tt:(bl + 1) * tt]
            lft_ref[bl] = lft[:, bl * tt:(bl + 1) * tt]
        q_ref[...] = (_dot(h, wq_ref[...]) * scale).astype(BF16)
        vtb_ref[...] = jnp.zeros_like(vtb_ref)
        ka_ref[...] = jnp.zeros_like(ka_ref)


def _inproj(x, shift, scale, g_mix, w, *, prompt, tm):
    b, t, d = x.shape
    n = b * t
    tm = min(tm, n)
    tt = min(t, tm)
    nb = tm // tt
    tpb = t // tt
    assert (nb == 1) == prompt
    grid = (n // tm,)

    def xmap(i):
        return (i // tpb, i % tpb, 0) if nb == 1 else (i, 0, 0)

    def bmap(i):
        return (i // tpb, 0, 0) if nb == 1 else (i, 0, 0)

    def tmap(i):
        return (i // tpb, 0, i % tpb) if nb == 1 else (i, 0, 0)

    full = lambda a: pl.BlockSpec(a.shape, lambda i: (0,) * a.ndim)
    names = ("wu", "wq", "wkt", "wvt", "wft", "bfc", "wk", "pm")
    in_specs = [pl.BlockSpec((nb, tt, d), xmap),
                pl.BlockSpec((nb, 1, d), bmap), pl.BlockSpec((nb, 1, d), bmap),
                pl.BlockSpec((1, d), lambda i: (0, 0))] + [full(w[k]) for k in names]
    tblk = lambda rows: pl.BlockSpec((None if nb == 1 else nb, rows, tt), tmap)
    dummy = (jax.ShapeDtypeStruct((8, LANES), F32), pl.BlockSpec((8, LANES), lambda i: (0, 0)))
    if prompt:
        nblk = n // ATT_BLK
        blocked = (jax.ShapeDtypeStruct((nblk, D_ATT, ATT_BLK), BF16),
                   pl.BlockSpec((tm // ATT_BLK, D_ATT, ATT_BLK), lambda i: (i, 0, 0)))
        q_out, vtb_out = blocked, blocked
        ka_out = (jax.ShapeDtypeStruct((nblk, ATT_BLK, 2 * D_ATT), BF16),
                  pl.BlockSpec((tm // ATT_BLK, ATT_BLK, 2 * D_ATT), lambda i: (i, 0, 0)))
    else:
        q_out = (jax.ShapeDtypeStruct((n, D_ATT), BF16), pl.BlockSpec((tm, D_ATT), lambda i: (i, 0)))
        vtb_out = ka_out = dummy
    kdim = SSM_CHUNK * LANES
    outs = [(jax.ShapeDtypeStruct((N_OCT, n // SSM_CHUNK, kdim), BF16),
             pl.BlockSpec((N_OCT, tm // SSM_CHUNK, kdim), lambda i: (0, i, 0))),
            q_out,
            (jax.ShapeDtypeStruct((b, D_ATT, t), F32), tblk(D_ATT)),
            (jax.ShapeDtypeStruct((b, D_ATT, t), F32), tblk(D_ATT)),
            (jax.ShapeDtypeStruct((b, N_HEADS, t), F32), tblk(N_HEADS)),
            vtb_out, ka_out]
    return pl.pallas_call(
        functools.partial(_inproj_kernel, prompt=prompt, tpb=tpb),
        out_shape=tuple(o[0] for o in outs), grid=grid, in_specs=in_specs, out_specs=tuple(o[1] for o in outs),
        scratch_shapes=[pltpu.VMEM((N_OCT, tm, LANES), F32), pltpu.VMEM((8, LANES), F32)],
        compiler_params=_cparams(("arbitrary",)),
        name="inproj_p" if prompt else "inproj_s",
    )(x, shift, scale, g_mix, *[w[k] for k in names])


def _ssm_operators(log_dt, a_re, a_im, b_re, b_im, c_re, c_im, d_skip):
    L, G, P, C = SSM_CHUNK, N_SSM_GROUPS, SSM_STATE, SSM_GROUP
    q = G // N_OCT
    lam = lax.complex(a_re, a_im)
    dt = jnp.exp(log_dt)[:, None]
    lam_dt = lam * dt
    lam_bar = jnp.exp(lam_dt)
    b_bar = ((lam_bar - 1.0) / lam)[:, :, None] * lax.complex(b_re, b_im)
    c_mat = lax.complex(c_re, c_im)
    pw = jnp.exp(lam_dt[None] * jnp.arange(L + 1, dtype=F32)[:, None, None])
    kern = jnp.real(jnp.einsum('gcp,dgp,gpk->dgck', c_mat, pw[:L], b_bar))
    ksm = jnp.transpose(kern, (1, 0, 3, 2)).reshape(N_OCT, q, L, C, C)
    ksm = jnp.transpose(ksm, (0, 2, 1, 3, 4)).reshape(N_OCT, L, q * C, C)
    pwr = pw[L - 1 - jnp.arange(L)]
    wb = jnp.einsum('jgp,gpk->jgkp', pwr, b_bar)
    wsm = jnp.stack([jnp.real(wb), jnp.imag(wb)], axis=0).reshape(2, L, N_OCT, q * C, P)
    wsm = jnp.transpose(wsm, (2, 1, 0, 3, 4))
    cl = c_mat[None] * pw[1:L + 1][:, :, None, :]
    vsm = jnp.stack([jnp.real(cl), -jnp.imag(cl)], axis=0).reshape(2, L, N_OCT, q, C, P)
    vsm = jnp.transpose(vsm, (2, 1, 0, 3, 5, 4)).reshape(N_OCT, L, OCT_STATE, C)
    tmat, wend, win = _ssm_prep(ksm, wsm, vsm, d_skip.reshape(N_OCT, 1, LANES))
    dec = pw[L].reshape(N_OCT, q * P)
    decay = jnp.concatenate([jnp.real(dec), jnp.imag(dec)], axis=-1).reshape(N_OCT, 1, OCT_STATE)
    return tmat, wend, win, decay.astype(F32)


def _ssm_prep_kernel(ksm_ref, wsm_ref, vsm_ref, d_ref, tmat_ref, wend_ref, win_ref):
    L, C, P = SSM_CHUNK, SSM_GROUP, SSM_STATE
    half = OCT_STATE // 2

    def spread(period, width):
        r = lax.broadcasted_iota(I32, (period, width), 0)
        c = lax.broadcasted_iota(I32, (period, width), 1)
        return jnp.where(c % period == r, 1.0, 0.0)

    def same_group(shape, row_div, row_mod, lane_div):
        r = lax.broadcasted_iota(I32, shape, 0)
        c = lax.broadcasted_iota(I32, shape, 1)
        return (r % row_mod) // row_div == c // lane_div

    e_c = spread(C, LANES)
    e_p = spread(P, half)

    def repeat_bf16(x, e):
        return _dot(x.astype(BF16), e.astype(BF16))

    grp = same_group((LANES, LANES), C, LANES, C)
    rr = lax.broadcasted_iota(I32, (LANES, LANES), 0)
    cc = lax.broadcasted_iota(I32, (LANES, LANES), 1)
    lag = []
    for d in range(L):
        if d == 0:
            blk = jnp.where(grp, _dot_hi(ksm_ref[d], e_c), 0.0)
            blk = blk + jnp.where(rr == cc, jnp.broadcast_to(d_ref[...], (LANES, LANES)), 0.0)
        else:
            blk = jnp.where(grp, repeat_bf16(ksm_ref[d], e_c), 0.0)
        lag.append(blk.astype(BF16))
    zero = jnp.zeros((LANES, LANES), BF16)
    for j in range(L):
        for t in range(L):
            tmat_ref[j * LANES:(j + 1) * LANES, t * LANES:(t + 1) * LANES] = lag[t - j] if t >= j else zero
    grp_e = same_group((LANES, half), C, LANES, P)
    for j in range(L):
        for part in range(2):
            blk = jnp.where(grp_e, repeat_bf16(wsm_ref[j, part], e_p), 0.0)
            wend_ref[j * LANES:(j + 1) * LANES, part * half:(part + 1) * half] = blk.astype(BF16)
    grp_i = same_group((OCT_STATE, LANES), P, half, C)
    for t in range(L):
        blk = jnp.where(grp_i, repeat_bf16(vsm_ref[t], e_c), 0.0)
        win_ref[:, t * LANES:(t + 1) * LANES] = blk.astype(BF16)


def _ssm_prep(ksm, wsm, vsm, dvec):
    kdim = SSM_CHUNK * LANES
    blk = lambda a: pl.BlockSpec((None,) + a.shape[1:], lambda o: (o,) + (0,) * (a.ndim - 1))
    out = lambda r, c: (jax.ShapeDtypeStruct((N_OCT, r, c), BF16), pl.BlockSpec((None, r, c), lambda o: (o, 0, 0)))
    outs = [out(kdim, kdim), out(kdim, OCT_STATE), out(OCT_STATE, kdim)]
    return pl.pallas_call(
        _ssm_prep_kernel,
        out_shape=tuple(o[0] for o in outs),
        grid=(N_OCT,),
        in_specs=[blk(ksm), blk(wsm), blk(vsm), blk(dvec)],
        out_specs=tuple(o[1] for o in outs),
        compiler_params=_cparams(("arbitrary",)),
        name="ssm_prep",
    )(ksm, wsm, vsm, dvec)


def _ssm_kernel(u_ref, tmat_ref, wend_ref, win_ref, dec_ref, s0_ref, y_ref, sfin_ref, e_ref, sp_ref, st_ref,
                *, nb, cpt):
    i = pl.program_id(1)
    half = OCT_STATE // 2
    rows = nb * cpt

    @pl.when(i == 0)
    def _():
        st_ref[...] = s0_ref[...]

    u = u_ref[...].reshape(rows, u_ref.shape[-1])
    e_ref[...] = _dot(u, wend_ref[...])
    dec = dec_ref[...]
    ar, ai = dec[:, :half], dec[:, half:]

    def advance(row, st):
        sp_ref[pl.ds(row, 1), :] = st
        e = e_ref[pl.ds(row, 1), :]
        re, im = st[:, :half], st[:, half:]
        nre = ar * re - ai * im + e[:, :half]
        nim = ar * im + ai * re + e[:, half:]
        return jnp.concatenate([nre, nim], axis=-1)

    if nb <= SSM_INTERLEAVE:
        def per_chunk(r, sts):
            return tuple(advance(bb * cpt + r, sts[bb]) for bb in range(nb))

        sts = lax.fori_loop(0, cpt, per_chunk, tuple(st_ref[bb:bb + 1, :] for bb in range(nb)))
        for bb in range(nb):
            st_ref[bb:bb + 1, :] = sts[bb]
    else:
        def per_batch(bb, carry):
            st = lax.fori_loop(0, cpt, lambda r, st: advance(bb * cpt + r, st), st_ref[pl.ds(bb, 1), :])
            st_ref[pl.ds(bb, 1), :] = st
            return carry

        lax.fori_loop(0, nb, per_batch, 0)
    y = _dot(u, tmat_ref[...]) + _dot(sp_ref[...].astype(BF16), win_ref[...])
    y_ref[...] = jax.nn.gelu(y).astype(BF16).reshape(y_ref.shape)
    sfin_ref[...] = st_ref[...]


def _ssm(u4, ops, s0, batch, n_chunks):
    tmat, wend, win, decay = ops
    kdim = SSM_CHUNK * LANES
    cpt = SSM_ROWS if n_chunks % SSM_ROWS == 0 else n_chunks
    tiles = n_chunks // cpt
    if tiles > 1:
        u_in = u4.reshape(N_OCT, batch, n_chunks, kdim)
        u_spec = pl.BlockSpec((None, batch, cpt, kdim), lambda o, i: (o, 0, i, 0))
    else:
        u_in = u4
        u_spec = pl.BlockSpec((None, batch * n_chunks, kdim), lambda o, i: (o, 0, 0))
    r = batch * cpt
    wspec = lambda shape: pl.BlockSpec((None,) + shape, lambda o, i: (o, 0, 0))
    y, s_fin = pl.pallas_call(
        functools.partial(_ssm_kernel, nb=batch, cpt=cpt),
        out_shape=(jax.ShapeDtypeStruct(u_in.shape, BF16),
                   jax.ShapeDtypeStruct((N_OCT, batch, OCT_STATE), F32)),
        grid=(N_OCT, tiles),
        in_specs=[u_spec,
                  wspec((kdim, kdim)), wspec((kdim, OCT_STATE)), wspec((OCT_STATE, kdim)), wspec((1, OCT_STATE)),
                  wspec((batch, OCT_STATE))],
        out_specs=(u_spec, wspec((batch, OCT_STATE))),
        scratch_shapes=[pltpu.VMEM((r, OCT_STATE), F32), pltpu.VMEM((r, OCT_STATE), F32),
                        pltpu.VMEM((batch, OCT_STATE), F32)],
        compiler_params=_cparams(("arbitrary", "arbitrary")),
        name="ssm",
    )(u_in, tmat, wend, win, decay, s0)
    return y.reshape(u4.shape), s_fin


def _bias_placement():
    import numpy as np
    pm = np.zeros((3, N_HEADS, N_HEADS // 2 * LANES), np.float32)
    for piece in range(3):
        for h in range(N_HEADS):
            pm[piece, h, LANES * (h // 2) + 3 * (h % 2) + piece] = 1.0
    return jnp.asarray(pm)


def _attn_p_kernel(qt_ref, ka_ref, vt_ref, o_ref, qbd_ref, acc_ref, sa_ref, sb_ref):
    i = pl.program_id(2)
    bq = ATT_BLK
    hd = HEAD_DIM
    r = lax.broadcasted_iota(I32, (LANES, 2 * bq), 0)
    c = lax.broadcasted_iota(I32, (LANES, 2 * bq), 1)
    ones = jnp.where(((r < 3) & (c < bq)) | ((r >= 3) & (r < 6) & (c >= bq)), 1.0, 0.0).astype(BF16)
    zero = jnp.zeros((hd, bq), BF16)
    for pp in range(ATT_PAIRS):
        qb = qt_ref[pp * LANES:(pp + 1) * LANES, :]
        qbd_ref[pp, 0:hd, 0:bq] = qb[0:hd]
        qbd_ref[pp, 0:hd, bq:2 * bq] = zero
        qbd_ref[pp, hd:2 * hd, 0:bq] = zero
        qbd_ref[pp, hd:2 * hd, bq:2 * bq] = qb[hd:2 * hd]
        qbd_ref[pp, 2 * hd:2 * hd + LANES, :] = ones
    acc_ref[...] = jnp.zeros_like(acc_ref)

    nq = ka_ref.shape[0]

    def scores(s_ref, j):
        for pp in range(ATT_PAIRS):
            s_ref[pp] = _dot(ka_ref[j, :, pp * 2 * LANES:(pp + 1) * 2 * LANES], qbd_ref[pp])

    def attend(s_ref, j, carry, masked):
        out = []
        for pp in range(ATT_PAIRS):
            m_prev, l_prev = carry[2 * pp], carry[2 * pp + 1]
            s = s_ref[pp]
            if masked:
                key = j * bq + lax.broadcasted_iota(I32, s.shape, 0)
                qq = lax.broadcasted_iota(I32, s.shape, 1)
                qq = i * bq + jnp.where(qq >= bq, qq - bq, qq)
                s = jnp.where(key <= qq, s, -jnp.inf)
            m_new = jnp.maximum(m_prev, jnp.max(s, axis=0, keepdims=True))
            alpha = jnp.exp2(m_prev - m_new)
            p = jnp.exp2(s - m_new)
            l_new = alpha * l_prev + jnp.sum(p, axis=0, keepdims=True)
            pb = p.astype(BF16)
            vb = vt_ref[jnp.minimum(j, nq - 1), pp * LANES:(pp + 1) * LANES, :]
            r0 = pp * LANES
            acc_ref[r0:r0 + hd] = alpha[:, 0:bq] * acc_ref[r0:r0 + hd] + _dot(vb[0:hd], pb[:, 0:bq])
            acc_ref[r0 + hd:r0 + 2 * hd] = (alpha[:, bq:2 * bq] * acc_ref[r0 + hd:r0 + 2 * hd]
                                            + _dot(vb[hd:2 * hd], pb[:, bq:2 * bq]))
            out.extend((m_new, l_new))
        return tuple(out)

    def double_step(m, carry):
        j = 2 * m
        scores(sb_ref, j + 1)
        carry = attend(sa_ref, j, carry, False)
        scores(sa_ref, j + 2)
        return attend(sb_ref, j + 1, carry, False)

    init = (jnp.full((1, 2 * bq), -jnp.inf, F32), jnp.zeros((1, 2 * bq), F32)) * ATT_PAIRS
    scores(sa_ref, 0)
    carry = lax.fori_loop(0, i // 2, double_step, init)
    j = 2 * (i // 2)
    scores(sb_ref, jnp.minimum(j + 1, nq - 1))
    carry = attend(sa_ref, j, carry, True)
    carry = attend(sb_ref, j + 1, carry, True)
    for pp in range(ATT_PAIRS):
        l = carry[2 * pp + 1]
        r0 = pp * LANES
        o_ref[r0:r0 + hd] = acc_ref[r0:r0 + hd] / l[:, 0:bq]
        o_ref[r0 + hd:r0 + 2 * hd] = acc_ref[r0 + hd:r0 + 2 * hd] / l[:, bq:2 * bq]


def _attn_p(qt, kaug, vt, batch, t):
    blk = ATT_BLK
    nq = t // blk
    ngrp = N_HEADS // 2 // ATT_PAIRS
    rows = ATT_PAIRS * LANES
    qt4 = qt.reshape(batch, nq, D_ATT, blk)
    vt4 = vt.reshape(batch, nq, D_ATT, blk)
    ka4 = kaug.reshape(batch, nq, blk, 2 * D_ATT)
    return pl.pallas_call(
        _attn_p_kernel,
        out_shape=jax.ShapeDtypeStruct((batch, D_ATT, t), F32),
        grid=(batch, ngrp, nq),
        in_specs=[pl.BlockSpec((None, None, rows, blk), lambda b, p, i: (b, i, p, 0)),
                  pl.BlockSpec((None, nq, blk, 2 * rows), lambda b, p, i: (b, 0, 0, p), pipeline_mode=pl.Buffered(1)),
                  pl.BlockSpec((None, nq, rows, blk), lambda b, p, i: (b, 0, p, 0), pipeline_mode=pl.Buffered(1))],
        out_specs=pl.BlockSpec((None, rows, blk), lambda b, p, i: (b, p, i)),
        scratch_shapes=[pltpu.VMEM((ATT_PAIRS, 2 * LANES, 2 * blk), BF16), pltpu.VMEM((rows, blk), F32),
                        pltpu.VMEM((ATT_PAIRS, blk, 2 * blk), F32), pltpu.VMEM((ATT_PAIRS, blk, 2 * blk), F32)],
        compiler_params=_cparams(("arbitrary", "arbitrary", "arbitrary")),
        name="attn_p",
    )(qt4, ka4, vt4)


def _lane_cumsum(x):
    n = x.shape[-1]
    lane = lax.broadcasted_iota(I32, x.shape, x.ndim - 1)
    s = 1
    while s < n:
        x = x + jnp.where(lane >= s, pltpu.roll(x, s, x.ndim - 1), 0.0)
        s *= 2
    return x


def _fprep_s_kernel(cl_ref, ln_ref, fc_ref, fn_ref):
    b, h, p = cl_ref.shape
    cum = _lane_cumsum(cl_ref[...].reshape(b * h, p))
    fc_ref[...] = (-cum).reshape(b, h, p)
    total = cum[:, p - 1:p]
    cn = _lane_cumsum(ln_ref[...].reshape(b * h, LANES))
    fn_ref[...] = (-(total + cn)).reshape(b, h, LANES)


def _fprep_s(cache_lf_t, new_lf_t):
    b, h, p = cache_lf_t.shape
    return pl.pallas_call(
        _fprep_s_kernel,
        out_shape=(jax.ShapeDtypeStruct((b, h, p), F32), jax.ShapeDtypeStruct((b, h, LANES), F32)),
        grid=(1,),
        in_specs=[pl.BlockSpec((b, h, p), lambda i: (0, 0, 0)), pl.BlockSpec((b, h, LANES), lambda i: (0, 0, 0))],
        out_specs=(pl.BlockSpec((b, h, p), lambda i: (0, 0, 0)), pl.BlockSpec((b, h, LANES), lambda i: (0, 0, 0))),
        compiler_params=_cparams(("arbitrary",)),
        name="fprep_s",
    )(cache_lf_t, new_lf_t)


def _attn_s_kernel(q_ref, ck_ref, cv_ref, kn_ref, vn_ref, fc_ref, fn_ref, o_ref, qbd_ref, m_ref, l_ref, acc_ref,
                   *, tq):
    j = pl.program_id(1)
    nkv = pl.num_programs(1)
    rows = N_HEADS * tq
    bk = ck_ref.shape[1]

    @pl.when(j == 0)
    def _():
        q = q_ref[...]
        qrep = jnp.broadcast_to(q[None], (N_HEADS, tq, D_ATT)).reshape(rows, D_ATT)
        rh = lax.broadcasted_iota(I32, (rows, D_ATT), 0) // tq
        ch = lax.broadcasted_iota(I32, (rows, D_ATT), 1) // HEAD_DIM
        qbd_ref[...] = jnp.where(rh == ch, qrep, jnp.zeros_like(qrep))
        m_ref[...] = jnp.full_like(m_ref, -jnp.inf)
        l_ref[...] = jnp.zeros_like(l_ref)
        acc_ref[...] = jnp.zeros_like(acc_ref)

    def update(s, vt):
        m_prev = m_ref[...]
        m_new = jnp.maximum(m_prev, jnp.max(s, axis=1, keepdims=True))
        alpha = jnp.exp(m_prev - m_new)
        p = jnp.exp(s - m_new)
        l_ref[...] = alpha * l_ref[...] + jnp.sum(p, axis=1, keepdims=True)
        m_ref[...] = m_new
        acc_ref[...] = alpha * acc_ref[...] + _dot_nt(p.astype(BF16), vt)

    def bias(f, width):
        return jnp.broadcast_to(f[:, None, :], (N_HEADS, tq, width)).reshape(rows, width)

    s = _dot(qbd_ref[...], ck_ref[...].astype(BF16)) + bias(fc_ref[...], bk)
    update(s, cv_ref[...].astype(BF16))

    @pl.when(j == nkv - 1)
    def _():
        s2 = _dot(qbd_ref[...], kn_ref[...].astype(BF16)) + bias(fn_ref[...][:, 0:tq], tq)
        key = lax.broadcasted_iota(I32, (rows, tq), 1)
        qq = lax.broadcasted_iota(I32, (rows, tq), 0) % tq
        update(jnp.where(key <= qq, s2, -jnp.inf), vn_ref[...].astype(BF16))
        o = acc_ref[...] / l_ref[...]
        rh = lax.broadcasted_iota(I32, (rows, D_ATT), 0) // tq
        ch = lax.broadcasted_iota(I32, (rows, D_ATT), 1) // HEAD_DIM
        o = jnp.where(rh == ch, o, 0.0).reshape(N_HEADS, tq, D_ATT)
        o_ref[...] = jnp.sum(o, axis=0)


def _attn_s(q, cache_k, cache_v, k_new, v_new, fc, fn):
    b, tq, _ = q.shape
    p = cache_k.shape[2]
    bk = min(SAMPLE_KV_BLK, p)
    nkv = p // bk
    rows = N_HEADS * tq
    tok = lambda: pl.BlockSpec((None, tq, D_ATT), lambda bi, j: (bi, 0, 0))
    new = lambda: pl.BlockSpec((None, D_ATT, tq), lambda bi, j: (bi, 0, 0))
    return pl.pallas_call(
        functools.partial(_attn_s_kernel, tq=tq),
        out_shape=jax.ShapeDtypeStruct((b, tq, D_ATT), F32),
        grid=(b, nkv),
        in_specs=[tok(),
                  pl.BlockSpec((None, D_ATT, bk), lambda bi, j: (bi, 0, j)),
                  pl.BlockSpec((None, D_ATT, bk), lambda bi, j: (bi, 0, j)),
                  new(), new(),
                  pl.BlockSpec((None, N_HEADS, bk), lambda bi, j: (bi, 0, j)),
                  pl.BlockSpec((None, N_HEADS, LANES), lambda bi, j: (bi, 0, 0))],
        out_specs=tok(),
        scratch_shapes=[pltpu.VMEM((rows, D_ATT), BF16), pltpu.VMEM((rows, 1), F32),
                        pltpu.VMEM((rows, 1), F32), pltpu.VMEM((rows, D_ATT), F32)],
        compiler_params=_cparams(("arbitrary", "arbitrary")),
        name="attn_s",
    )(q, cache_k, cache_v, k_new, v_new, fc, fn)


def _outproj_kernel(x_ref, y4_ref, att_ref, gt_ref, sh_ref, sc_ref, gf_ref, wglu_ref, bglu_ref, wout_ref,
                    wr_ref, br_ref, cin_ref,
                    x1_ref, h2_ref, idx_ref, gate_ref, rank_ref, cnt_ref, carry_ref, ys_ref, *, att_transposed):
    i = pl.program_id(0)
    nb, tt, d = x_ref.shape
    tm = nb * tt

    @pl.when(i == 0)
    def _():
        carry_ref[...] = cin_ref[...]

    for s in range(SSM_CHUNK):
        for o in range(N_OCT):
            ys_ref[o, pl.ds(s, tm // SSM_CHUNK, stride=SSM_CHUNK), :] = (
                y4_ref[o, :, s * LANES:(s + 1) * LANES].astype(F32))
    ysf = jnp.concatenate([ys_ref[o] for o in range(N_OCT)], axis=-1)
    glu = ysf * jax.nn.sigmoid(_dot(ysf.astype(BF16), wglu_ref[...]) + bglu_ref[...])
    att = att_ref[...]
    if att_transposed:
        att = att.T
    mix = _dot(glu.astype(BF16), wout_ref[0:D_SSM, :]) + _dot(att.astype(BF16), wout_ref[D_SSM:, :])
    x1 = x_ref[...] + gt_ref[...] * mix.reshape(nb, tt, d)
    x1_ref[...] = x1
    h2 = _norm_mod(x1, gf_ref[...], sc_ref[...], sh_ref[...]).reshape(tm, d)
    bits = lax.bitcast_convert_type(h2.astype(BF16).astype(F32), I32)
    h2_ref[...] = (bits[:, d // 2:] & jnp.int32(-65536)) | lax.shift_right_logical(bits[:, :d // 2], 16)

    logits = lax.dot_general(wr_ref[...], h2, (((1,), (1,)), ((), ())), preferred_element_type=F32,
                             precision=lax.Precision.HIGHEST) + br_ref[...]
    sub = lax.broadcasted_iota(I32, logits.shape, 0)
    work = logits
    vals, idxs = [], []
    for _ in range(TOP_K):
        mx = jnp.max(work, axis=0, keepdims=True)
        ix = jnp.min(jnp.where(work == mx, sub, N_EXPERTS), axis=0, keepdims=True)
        vals.append(mx)
        idxs.append(ix)
        work = jnp.where(sub == ix, -jnp.inf, work)
    ex = [jnp.exp(v - vals[0]) for v in vals]
    den = ex[0] + ex[1] + ex[2] + ex[3]
    mh = jnp.where(work == -jnp.inf, 1.0, 0.0)
    r = lax.broadcasted_iota(I32, (tm, tm), 0)
    c = lax.broadcasted_iota(I32, (tm, tm), 1)
    earlier = jnp.where(r < c, 1.0, 0.0).astype(BF16)
    carry = carry_ref[...]
    before = _dot(mh.astype(BF16), earlier) + carry[:, 0:1]
    carry_ref[...] = carry + jnp.sum(mh, axis=1, keepdims=True)
    for kk in range(TOP_K):
        idx_ref[kk:kk + 1, :] = idxs[kk]
        gate_ref[kk:kk + 1, :] = ex[kk] / den
        rk = jnp.sum(jnp.where(sub == idxs[kk], before, 0.0), axis=0, keepdims=True)
        rank_ref[kk:kk + 1, :] = rk.astype(I32)
    cnt_ref[...] = carry_ref[...]


def _outproj(x, y4, att, gt, sh, sc, g_ffn, w, counts_in, *, att_transposed, tm):
    b, t, d = x.shape
    n = b * t
    tm = min(tm, n)
    tt = min(t, tm)
    nb = tm // tt
    tpb = t // tt

    def xmap(i):
        return (i // tpb, i % tpb, 0) if nb == 1 else (i, 0, 0)

    def bmap(i):
        return (i // tpb, 0, 0) if nb == 1 else (i, 0, 0)

    full = lambda shape: pl.BlockSpec(shape, lambda i: (0,) * len(shape))
    if att_transposed:
        att_spec = pl.BlockSpec((None, D_ATT, tm), lambda i: (i // tpb, 0, i % tpb))
    else:
        att_spec = pl.BlockSpec((tm, D_ATT), lambda i: (i, 0))
    choice = lambda dt: (jax.ShapeDtypeStruct((TOP_K, n), dt), pl.BlockSpec((TOP_K, tm), lambda i: (0, i)))
    outs = [(jax.ShapeDtypeStruct((b, t, d), F32), pl.BlockSpec((nb, tt, d), xmap)),
            (jax.ShapeDtypeStruct((n, d // 2), I32), pl.BlockSpec((tm, d // 2), lambda i: (i, 0))),
            choice(I32), choice(F32), choice(I32),
            (jax.ShapeDtypeStruct((N_EXPERTS, LANES), F32), full((N_EXPERTS, LANES)))]
    return pl.pallas_call(
        functools.partial(_outproj_kernel, att_transposed=att_transposed),
        out_shape=tuple(o[0] for o in outs),
        grid=(n // tm,),
        in_specs=[pl.BlockSpec((nb, tt, d), xmap),
                  pl.BlockSpec((N_OCT, tm // SSM_CHUNK, SSM_CHUNK * LANES), lambda i: (0, i, 0)),
                  att_spec,
                  pl.BlockSpec((nb, 1, d), bmap), pl.BlockSpec((nb, 1, d), bmap), pl.BlockSpec((nb, 1, d), bmap),
                  full((1, d)), full((D_SSM, D_SSM)), full((1, D_SSM)), full((d, d)),
                  full((N_EXPERTS, d)), full((N_EXPERTS, 1)), full((N_EXPERTS, LANES))],
        out_specs=tuple(o[1] for o in outs),
        scratch_shapes=[pltpu.VMEM((N_EXPERTS, LANES), F32), pltpu.VMEM((N_OCT, tm, LANES), F32)],
        compiler_params=_cparams(("arbitrary",)),
        name="outproj_t" if att_transposed else "outproj",
    )(x, y4, att, gt, sh, sc, g_ffn, w["wglu"], w["bglu"], w["wout"], w["wr"], w["br"], counts_in)


def _dispatch_kernel(zst_ref, nu_ref, dest_ref, *rest, tm, tiles):
    h_refs = rest[:len(tiles)]
    xs_ref, zero_ref, hbuf_ref, zsem, sems = rest[len(tiles):]
    i = pl.program_id(0)
    n_blocks = xs_ref.shape[0] // MOE_BLK

    @pl.when(i == 0)
    def _():
        zero_ref[...] = jnp.zeros_like(zero_ref)

        def zero_copy(start):
            start = pl.multiple_of(start, MOE_BLK)
            return pltpu.make_async_copy(zero_ref, xs_ref.at[pl.ds(start, MOE_BLK)], zsem)

        for e in range(N_EXPERTS):
            @pl.when(zst_ref[e] >= 0)
            def _():
                zero_copy(jnp.maximum(zst_ref[e], 0)).start()

        def tail_start(j, carry):
            zero_copy(j * MOE_BLK).start()
            return carry

        def tail_wait(j, carry):
            zero_copy(j * MOE_BLK).wait()
            return carry

        lax.fori_loop(nu_ref[0], n_blocks, tail_start, 0)
        for e in range(N_EXPERTS):
            @pl.when(zst_ref[e] >= 0)
            def _():
                zero_copy(jnp.maximum(zst_ref[e], 0)).wait()
        lax.fori_loop(nu_ref[0], n_blocks, tail_wait, 0)

    slot = i % 2

    def scatter_rows(h_ref):
        hbuf_ref[slot] = h_ref[...]

        def issue(t, carry):
            for kk in range(TOP_K):
                dst = dest_ref[0, 0, t * TOP_K + kk]
                pltpu.make_async_copy(hbuf_ref.at[slot, pl.ds(t, 1)], xs_ref.at[pl.ds(dst, 1)],
                                      sems.at[slot]).start(priority=kk % 2)
            return carry

        lax.fori_loop(0, tm, issue, 0, unroll=8)

    first_tile = 0
    for h_ref, n_tiles in zip(h_refs, tiles):
        pl.when((i >= first_tile) & (i < first_tile + n_tiles))(functools.partial(scatter_rows, h_ref))
        first_tile += n_tiles

    def wait_tile(s):
        n_copied = tm * TOP_K
        pltpu.make_async_copy(xs_ref.at[pl.ds(0, n_copied)], xs_ref.at[pl.ds(0, n_copied)], sems.at[s]).wait()

    @pl.when(i > 0)
    def _():
        wait_tile(1 - slot)

    @pl.when(i == pl.num_programs(0) - 1)
    def _():
        wait_tile(slot)


def _dispatch(h2s, dest, zstart, n_used, n_rows, tm):
    d = h2s[0].shape[1]
    dt = h2s[0].dtype
    tiles = tuple(h.shape[0] // tm for h in h2s)
    nt = sum(tiles)
    dest3 = dest.reshape(nt, 1, tm * TOP_K)
    in_specs = [pl.BlockSpec((1, 1, tm * TOP_K), lambda i, *_: (i, 0, 0), memory_space=pltpu.SMEM)]
    first_tile = 0
    for n_tiles in tiles:
        in_specs.append(pl.BlockSpec(
            (tm, d), lambda i, *_, f=first_tile, m=n_tiles: (jnp.clip(i - f, 0, m - 1), 0)))
        first_tile += n_tiles
    return pl.pallas_call(
        functools.partial(_dispatch_kernel, tm=tm, tiles=tiles),
        out_shape=jax.ShapeDtypeStruct((n_rows, d), dt),
        grid_spec=pltpu.PrefetchScalarGridSpec(
            num_scalar_prefetch=2, grid=(nt,),
            in_specs=in_specs,
            out_specs=pl.BlockSpec(memory_space=pl.ANY),
            scratch_shapes=[pltpu.VMEM((MOE_BLK, d), dt), pltpu.VMEM((2, tm, d), dt),
                            pltpu.SemaphoreType.DMA, pltpu.SemaphoreType.DMA((2,))]),
        compiler_params=_cparams(("arbitrary",)),
        name="moe_dispatch",
    )(zstart, n_used, dest3, *h2s)


def _expert_kernel(be_ref, nxt_ref, slot_ref, first_ref, nu_ref, xs_ref, wup_hbm, bup_ref, wdn_hbm, bdn_ref,
                   ys_ref, wup_f32, wdn_f32, wup_bf, wdn_bf, sems):
    i = pl.program_id(0)
    e = be_ref[i]

    def fetch(expert, slot):
        return (pltpu.make_async_copy(wup_hbm.at[expert], wup_f32.at[slot], sems.at[0, slot]),
                pltpu.make_async_copy(wdn_hbm.at[expert], wdn_f32.at[slot], sems.at[1, slot]))

    @pl.when(i < nu_ref[0])
    def _():
        @pl.when(first_ref[i] == 1)
        def _():
            slot = slot_ref[i]

            @pl.when(i == 0)
            def _():
                for copy in fetch(e, slot):
                    copy.start()

            for copy in fetch(e, slot):
                copy.wait()

            @pl.when(nxt_ref[i] >= 0)
            def _():
                for copy in fetch(jnp.maximum(nxt_ref[i], 0), 1 - slot):
                    copy.start()

            wup_bf[...] = wup_f32[slot].astype(BF16)
            wdn_bf[...] = wdn_f32[slot].astype(BF16)

        words = xs_ref[...]
        half = words.shape[1]
        x_lo = lax.bitcast_convert_type(words << 16, F32).astype(BF16)
        x_hi = lax.bitcast_convert_type(words & jnp.int32(-65536), F32).astype(BF16)
        up = _dot(x_lo, wup_bf[0:half, :]) + _dot(x_hi, wup_bf[half:, :]) + bup_ref[...]
        x_glu = jnp.minimum(up[:, :D_FF], SWIGLU_LIMIT)
        x_lin = jnp.clip(up[:, D_FF:], -SWIGLU_LIMIT, SWIGLU_LIMIT)
        act = (x_lin + 1.0) * (x_glu * jax.nn.sigmoid(SWIGLU_ALPHA * x_glu))
        ys_ref[...] = (_dot(act.astype(BF16), wdn_bf[...]) + bdn_ref[...]).astype(ys_ref.dtype)

    @pl.when(i >= nu_ref[0])
    def _():
        ys_ref[...] = jnp.zeros_like(ys_ref)


def _experts(xs, block_e, n_used, has_rows, w_up, b_up, w_down, b_down):
    n_rows, packed = xs.shape
    d = 2 * packed
    nblk = n_rows // MOE_BLK
    blocks = jnp.arange(nblk, dtype=I32)
    used = blocks < n_used[0]
    first = (used & ((blocks == 0) | (block_e != jnp.roll(block_e, 1)))).astype(I32)
    slot = ((jnp.cumsum(first) - 1) % 2).astype(I32)
    experts = jnp.arange(N_EXPERTS, dtype=I32)
    later = jnp.where(has_rows[None, :] & (experts[None, :] > experts[:, None]), experts[None, :], N_EXPERTS)
    nxt_e = jnp.min(later, axis=1)
    nxt_e = jnp.where(nxt_e < N_EXPERTS, nxt_e, -1).astype(I32)
    nxt = jnp.sum(jnp.where(block_e[:, None] == experts[None, :], nxt_e[None, :], 0), axis=1).astype(I32)
    rowmap = lambda i, be, nx, sl, fi, nu: (jnp.minimum(i, nu[0] - 1), 0)
    emap = lambda i, be, nx, sl, fi, nu: (be[i], 0, 0)
    return pl.pallas_call(
        _expert_kernel,
        out_shape=jax.ShapeDtypeStruct((n_rows, d), BF16),
        grid_spec=pltpu.PrefetchScalarGridSpec(
            num_scalar_prefetch=5, grid=(nblk,),
            in_specs=[pl.BlockSpec((MOE_BLK, packed), rowmap),
                      pl.BlockSpec(memory_space=pl.ANY), pl.BlockSpec((None, 1, 2 * D_FF), emap),
                      pl.BlockSpec(memory_space=pl.ANY), pl.BlockSpec((None, 1, d), emap)],
            out_specs=pl.BlockSpec((MOE_BLK, d), lambda i, *_: (i, 0)),
            scratch_shapes=[pltpu.VMEM((2, d, 2 * D_FF), F32), pltpu.VMEM((2, D_FF, d), F32),
                            pltpu.VMEM((d, 2 * D_FF), BF16), pltpu.VMEM((D_FF, d), BF16),
                            pltpu.SemaphoreType.DMA((2, 2))]),
        compiler_params=_cparams(("arbitrary",)),
        name="moe_experts",
    )(block_e, nxt, slot, first, n_used, xs, w_up, b_up, w_down, b_down)


def _combine_kernel(meta_ref, metan_ref, x1_ref, col_ref, gate_ref, gt_ref, gfin_ref, ys_ref, y_ref, ybuf_ref, sems,
                    *, tm):
    i = pl.program_id(0)
    nt = pl.num_programs(0)
    nb, tt, d = x1_ref.shape
    rows = ybuf_ref.shape[1]

    ra = RUN_ALIGN

    def start_copies(m_ref, slot):
        def per_expert(e, carry):
            a = m_ref[0, 0, e]
            n_chunks = m_ref[0, 0, N_EXPERTS + e]
            bo = m_ref[0, 0, 2 * N_EXPERTS + e]

            def per_chunk(c, carry2):
                src = ys_ref.at[pl.ds(pl.multiple_of(a + ra * c, ra), ra)]
                dst = ybuf_ref.at[slot, pl.ds(pl.multiple_of(bo + ra * c, ra), ra)]
                pltpu.make_async_copy(src, dst, sems.at[slot]).start(priority=e % 2)
                return carry2

            lax.fori_loop(0, n_chunks, per_chunk, 0)
            return carry

        for e in range(N_EXPERTS):
            per_expert(e, 0)

    def wait_copies(m_ref, slot):
        total = m_ref[0, 0, 3 * N_EXPERTS]

        def wait_rows(n_rows):
            pltpu.make_async_copy(ys_ref.at[pl.ds(0, n_rows)], ybuf_ref.at[slot, pl.ds(0, n_rows)],
                                  sems.at[slot]).wait()

        lax.fori_loop(0, total // WAIT_GROUP, lambda c, carry: (wait_rows(WAIT_GROUP * ra), carry)[1], 0)
        lax.fori_loop(0, total % WAIT_GROUP, lambda c, carry: (wait_rows(ra), carry)[1], 0)

    @pl.when(i == 0)
    def _():
        ybuf_ref[...] = jnp.zeros_like(ybuf_ref)
        start_copies(meta_ref, 0)

    @pl.when(i + 1 < nt)
    def _():
        start_copies(metan_ref, (i + 1) % 2)

    slot = i % 2
    wait_copies(meta_ref, slot)
    col = col_ref[...]
    gate = gate_ref[...]
    lane = lax.broadcasted_iota(I32, (tm, rows), 1)
    pick = jnp.where(lane == col[:, 0:1], gate[:, 0:1], 0.0)
    for kk in range(1, TOP_K):
        pick = pick + jnp.where(lane == col[:, kk:kk + 1], gate[:, kk:kk + 1], 0.0)
    moe = _dot(pick.astype(BF16), ybuf_ref[slot])
    x2 = x1_ref[...] + gt_ref[...] * moe.reshape(nb, tt, d)
    ms = jnp.mean(x2 * x2, axis=-1, keepdims=True)
    y_ref[...] = x2 * lax.rsqrt(ms + NORM_EPS) * gfin_ref[...]


def _combine_plan(idx, dest, off, before0, tm):
    n = idx.shape[1]
    nt = n // tm
    is_e = idx[None, :, :] == jnp.arange(N_EXPERTS, dtype=I32)[:, None, None]
    cnt = jnp.any(is_e, axis=1).astype(I32).reshape(N_EXPERTS, nt, tm).sum(axis=2).T
    before = before0[None, :] + jnp.cumsum(cnt, axis=0) - cnt
    start = off[None, :] + before
    ra = RUN_ALIGN
    lo = start // ra * ra
    hi = jnp.where(cnt > 0, (start + cnt + ra - 1) // ra * ra, lo)
    span = hi - lo
    boff = jnp.cumsum(span, axis=1) - span
    delta = jnp.repeat((boff - lo).T, tm, axis=1)
    col = dest + jnp.sum(jnp.where(is_e, delta[:, None, :], 0), axis=0)
    total = jnp.sum(span // ra, axis=1, keepdims=True)
    meta = jnp.concatenate([lo, span // ra, boff, jnp.broadcast_to(total, lo.shape)], axis=1).astype(I32)
    return meta.reshape(nt, 1, 4 * N_EXPERTS), col.T.astype(I32)


def _combine(x1, gate, idx, dest, off, before0, ys, gt, g_final, tm):
    b, t, d = x1.shape
    n = b * t
    tm = min(tm, n)
    tt = min(t, tm)
    nb = tm // tt
    tpb = t // tt
    nt = n // tm
    rows = -(-(tm * TOP_K + 2 * (RUN_ALIGN - 1) * N_EXPERTS) // 256) * 256
    meta, col = _combine_plan(idx, dest, off, before0, tm)

    def xmap(i):
        return (i // tpb, i % tpb, 0) if nb == 1 else (i, 0, 0)

    def bmap(i):
        return (i // tpb, 0, 0) if nb == 1 else (i, 0, 0)

    smem = lambda imap: pl.BlockSpec((1, 1, 4 * N_EXPERTS), imap, memory_space=pltpu.SMEM)
    return pl.pallas_call(
        functools.partial(_combine_kernel, tm=tm),
        out_shape=jax.ShapeDtypeStruct((b, t, d), F32),
        grid=(nt,),
        in_specs=[smem(lambda i: (i, 0, 0)), smem(lambda i: (jnp.minimum(i + 1, nt - 1), 0, 0)),
                  pl.BlockSpec((nb, tt, d), xmap),
                  pl.BlockSpec((tm, TOP_K), lambda i: (i, 0)),
                  pl.BlockSpec((tm, TOP_K), lambda i: (i, 0)),
                  pl.BlockSpec((nb, 1, d), bmap),
                  pl.BlockSpec((1, d), lambda i: (0, 0)),
                  pl.BlockSpec(memory_space=pl.ANY)],
        out_specs=pl.BlockSpec((nb, tt, d), xmap),
        scratch_shapes=[pltpu.VMEM((2, rows, d), BF16), pltpu.SemaphoreType.DMA((2,))],
        compiler_params=_cparams(("arbitrary",)),
        name="moe_combine",
    )(meta, meta, x1, col, gate, gt, g_final, ys)


def _moe(streams, counts, g_final, w_up, b_up, w_down, b_down):
    n_total = sum(s["h2"].shape[0] for s in streams)
    cnt = counts[:, 0].astype(I32)
    padded = (cnt + MOE_BLK - 1) // MOE_BLK * MOE_BLK
    pad_end = jnp.cumsum(padded)
    off = (pad_end - padded).astype(I32)
    n_blocks = -(-n_total * TOP_K // MOE_BLK) + N_EXPERTS
    n_rows = n_blocks * MOE_BLK
    starts = jnp.arange(n_blocks, dtype=I32) * MOE_BLK
    block_e = jnp.minimum(jnp.sum((pad_end[None, :] <= starts[:, None]).astype(I32), axis=1), N_EXPERTS - 1)
    n_used = (pad_end[-1:] // MOE_BLK).astype(I32)
    zstart = jnp.where(padded > 0, pad_end - MOE_BLK, -1).astype(I32)
    experts = jnp.arange(N_EXPERTS, dtype=I32)
    dests = []
    for s in streams:
        seg = jnp.sum(jnp.where(s["idx"][None] == experts[:, None, None], off[:, None, None], 0), axis=0)
        dests.append((seg + s["rank"]).astype(I32))
    tm = min([512] + [s["h2"].shape[0] for s in streams])
    dest_rows = jnp.concatenate([dest.T for dest in dests], axis=0)
    xs = _dispatch([s["h2"] for s in streams], dest_rows, zstart, n_used, n_rows, tm)
    ys = _experts(xs, block_e, n_used, padded > 0, w_up, b_up, w_down, b_down)
    return [_combine(s["x1"], s["gate"].T, s["idx"], dest, off, s["before"], ys, s["gt"], g_final, 256)
            for s, dest in zip(streams, dests)]


def _stream(x, mod, w, ssm_ops, s0, cache, params, counts_in):
    b, t, d = x.shape
    n = b * t
    sh_mix, sc_mix, gt_mix, sh_ffn, sc_ffn, gt_ffn = mod
    prompt = cache is None
    tm = 1024 if prompt else 512
    u4, q, kt, vt, lft, vtb, kaug = _inproj(x, sh_mix, sc_mix, params["g_mix"], w, prompt=prompt, tm=tm)
    n_chunks = t // SSM_CHUNK
    y4, s_fin = _ssm(u4, ssm_ops, s0, b, n_chunks)
    if prompt:
        att = _attn_p(q, kaug, vtb, b, t)
    else:
        cache_k, cache_v, cache_lf = cache
        p = cache_k.shape[1]
        ck_t = jnp.transpose(cache_k, (0, 2, 3, 1)).reshape(b, D_ATT, p)
        cv_t = jnp.transpose(cache_v, (0, 2, 3, 1)).reshape(b, D_ATT, p)
        cl_t = jnp.transpose(cache_lf, (0, 2, 1))
        ln_t = jnp.pad(lft, ((0, 0), (0, 0), (0, LANES - t)))
        fc, fn = _fprep_s(cl_t, ln_t)
        att = _attn_s(q.reshape(b, t, D_ATT), ck_t, cv_t, kt, vt, fc, fn).reshape(n, D_ATT)
    x1, h2, idx, gate, rank, counts = _outproj(x, y4, att, gt_mix, sh_ffn, sc_ffn, params["g_ffn"], w, counts_in,
                                               att_transposed=prompt, tm=tm)
    routed = {"x1": x1, "h2": h2, "idx": idx, "gate": gate, "rank": rank, "gt": gt_ffn,
              "before": counts_in[:, 0].astype(I32)}
    heads = lambda a: jnp.transpose(a.reshape(b, N_HEADS, HEAD_DIM, t), (0, 3, 1, 2))[None]
    return routed, counts, heads(kt), heads(vt), jnp.transpose(lft, (0, 2, 1))[None], s_fin


def _state_in(re, im):
    b = re.shape[0]
    s = jnp.concatenate([re.reshape(b, N_OCT, OCT_STATE // 2), im.reshape(b, N_OCT, OCT_STATE // 2)], axis=-1)
    return jnp.transpose(s, (1, 0, 2))


def _state_out(s):
    b = s.shape[1]
    s = jnp.transpose(s, (1, 0, 2))
    half = OCT_STATE // 2
    re = s[:, :, :half].reshape(1, b, N_SSM_GROUPS, SSM_STATE)
    im = s[:, :, half:].reshape(1, b, N_SSM_GROUPS, SSM_STATE)
    return re, im


def kernel(x_prompt, x_sample, c_prompt, c_sample, cache_k, cache_v, cache_logf, state_ssm_re, state_ssm_im, w_ada, b_ada, g_mix, w_in, b_forget, ssm_log_dt, ssm_a_re, ssm_a_im, ssm_b_re, ssm_b_im, ssm_c_re, ssm_c_im, ssm_d, w_glu, b_glu, w_out, g_ffn, w_router, b_router, w_up, b_up, w_down, b_down, g_final):
    assert w_ada.shape[0] == 1, "single-layer trunk"
    d = D_MODEL
    bp, tp, _ = x_prompt.shape
    bs, ts, _ = x_sample.shape
    n_c = bp + bs
    c_all = jnp.pad(jnp.concatenate([c_prompt, c_sample], axis=0), ((0, -n_c % 8), (0, 0)))
    mod = _ada(c_all, w_ada[0], b_ada[0].reshape(1, 6 * d))
    mod_p = [mod[:bp, j * d:(j + 1) * d].reshape(bp, 1, d) for j in range(6)]
    mod_s = [mod[bp:n_c, j * d:(j + 1) * d].reshape(bs, 1, d) for j in range(6)]
    wi = w_in[0]
    o1, o2, o3, o4 = D_SSM, D_SSM + D_ATT, D_SSM + 2 * D_ATT, D_SSM + 3 * D_ATT
    w_q = wi[:, o1:o2].astype(BF16)
    w_k = wi[:, o2:o3].astype(BF16)
    unused = jnp.zeros((8, LANES), BF16)
    base = {"wu": wi[:, :o1].astype(BF16), "wkt": w_k.T, "wvt": wi[:, o3:o4].astype(BF16).T,
            "wft": wi[:, o4:].astype(BF16).T, "bfc": b_forget[0].reshape(N_HEADS, 1), "pm": _bias_placement(),
            "wglu": w_glu[0].astype(BF16), "bglu": b_glu[0].reshape(1, D_SSM), "wout": w_out[0].astype(BF16),
            "wr": w_router[0].T, "br": b_router[0].reshape(N_EXPERTS, 1)}
    w_p = dict(base, wq=w_q.T, wk=w_k)
    w_s = dict(base, wq=w_q, wk=unused)
    params = {"g_mix": g_mix[0].reshape(1, d), "g_ffn": g_ffn[0].reshape(1, d), "g_final": g_final.reshape(1, d),
              "w_up": w_up[0], "b_up": b_up[0].reshape(N_EXPERTS, 1, 2 * D_FF),
              "w_down": w_down[0], "b_down": b_down[0].reshape(N_EXPERTS, 1, d)}
    ssm_ops = _ssm_operators(ssm_log_dt[0], ssm_a_re[0], ssm_a_im[0], ssm_b_re[0], ssm_b_im[0],
                             ssm_c_re[0], ssm_c_im[0], ssm_d[0])
    zero_state = jnp.zeros((N_OCT, bp, OCT_STATE), F32)
    counts0 = jnp.zeros((N_EXPERTS, LANES), F32)
    r_p, counts_p, k_p, v_p, f_p, s_p = _stream(x_prompt, mod_p, w_p, ssm_ops, zero_state, None, params, counts0)
    s0 = _state_in(state_ssm_re[0], state_ssm_im[0])
    r_s, counts, k_s, v_s, f_s, s_s = _stream(x_sample, mod_s, w_s, ssm_ops, s0,
                                              (cache_k[0], cache_v[0], cache_logf[0]), params, counts_p)
    y_p, y_s = _moe([r_p, r_s], counts, params["g_final"],
                    params["w_up"], params["b_up"], params["w_down"], params["b_down"])
    sre_p, sim_p = _state_out(s_p)
    sre_s, sim_s = _state_out(s_s)
    return (y_p, y_s, k_p, v_p, f_p, sre_p, sim_p, k_s, v_s, f_s, sre_s, sim_s)
```
